```python
import jax, jax.numpy as jnp
from jax import lax
import numpy as np

D_MODEL = 2048
BATCH = 8
SEQ = 8192
DEPTH = 2

N_META = 16
N_MIXERS = 2
CONV_WIDTH = 3
HEAD_DIM = 64
N_Q_HEADS = D_MODEL // HEAD_DIM
N_KV_HEADS = N_Q_HEADS // 8
GROUP = N_Q_HEADS // N_KV_HEADS
WINDOW = 128
BLOCK = 128
ROPE_THETA = 10000.0
D_FF = 4 * D_MODEL
RMS_EPS = 1e-5
NEG_INF = -1e30

kernel_name = "hybrid_shortconv_swa_sink_block"


def rms_norm(x, g):
    xf = x.astype(jnp.float32)
    var = jnp.mean(xf * xf, axis=-1, keepdims=True)
    return (xf * lax.rsqrt(var + RMS_EPS)).astype(x.dtype) * g


def short_conv_mixer(h, w_in, conv_w, w_out):
    bcu = h @ w_in
    b_gate, c_gate, u = jnp.split(bcu, 3, axis=-1)
    v = c_gate * u
    L = v.shape[1]
    vp = jnp.pad(v, ((0, 0), (CONV_WIDTH - 1, 0), (0, 0)))
    conv = conv_w[0] * vp[:, 0:L]
    for k in range(1, CONV_WIDTH):
        conv = conv + conv_w[k] * vp[:, k:k + L]
    return (b_gate * conv) @ w_out


def rope_tables(n_pos, offset):
    pos = jnp.arange(n_pos, dtype=jnp.float32) - offset
    inv = ROPE_THETA ** (-jnp.arange(0, HEAD_DIM, 2, dtype=jnp.float32) / HEAD_DIM)
    ang = pos[:, None] * inv[None, :]
    return jnp.cos(ang), jnp.sin(ang)


def apply_rope(x, cos, sin):
    x1, x2 = jnp.split(x, 2, axis=-1)
    c = cos[None, :, None, :]
    s = sin[None, :, None, :]
    return jnp.concatenate([x1 * c - x2 * s, x2 * c + x1 * s], axis=-1).astype(x.dtype)


def swa_sink_mixer(h, w_qkv, sinks, w_o):
    Bsz, L, _ = h.shape
    pad = (-L) % BLOCK
    P = L + pad
    nb = P // BLOCK
    hp = jnp.pad(h, ((0, 0), (pad, 0), (0, 0)))
    qkv = hp @ w_qkv
    q, k, v = jnp.split(qkv, [N_Q_HEADS * HEAD_DIM, (N_Q_HEADS + N_KV_HEADS) * HEAD_DIM], axis=-1)
    q = q.reshape(Bsz, P, N_Q_HEADS, HEAD_DIM)
    k = k.reshape(Bsz, P, N_KV_HEADS, HEAD_DIM)
    v = v.reshape(Bsz, P, N_KV_HEADS, HEAD_DIM)
    cos, sin = rope_tables(P, pad)
    q = apply_rope(q, cos, sin)
    k = apply_rope(k, cos, sin)

    qb = q.reshape(Bsz, nb, BLOCK, N_KV_HEADS, GROUP, HEAD_DIM)
    kb = k.reshape(Bsz, nb, BLOCK, N_KV_HEADS, HEAD_DIM)
    vb = v.reshape(Bsz, nb, BLOCK, N_KV_HEADS, HEAD_DIM)
    zpad = ((0, 0), (1, 0), (0, 0), (0, 0), (0, 0))
    k_band = jnp.concatenate([jnp.pad(kb[:, :-1], zpad), kb], axis=2)
    v_band = jnp.concatenate([jnp.pad(vb[:, :-1], zpad), vb], axis=2)

    scale = HEAD_DIM ** -0.5
    s = jnp.einsum("bnqhgd,bnkhd->bhgnqk", qb, k_band).astype(jnp.float32) * scale

    blk = jnp.arange(nb)[:, None, None] * BLOCK
    q_idx = blk + jnp.arange(BLOCK)[None, :, None]
    k_idx = blk - BLOCK + jnp.arange(2 * BLOCK)[None, None, :]
    diff = q_idx - k_idx
    allowed = (diff >= 0) & (diff < WINDOW) & (k_idx >= pad)
    s = jnp.where(allowed[None, None, None], s, NEG_INF)

    sink = sinks.astype(jnp.float32).reshape(N_KV_HEADS, GROUP)[None, :, :, None, None, None]
    m = jnp.maximum(jnp.max(s, axis=-1, keepdims=True), sink)
    e = jnp.exp(s - m)
    den = jnp.sum(e, axis=-1, keepdims=True) + jnp.exp(sink - m)
    p = (e / den).astype(v.dtype)

    o = jnp.einsum("bhgnqk,bnkhd->bnqhgd", p, v_band).reshape(Bsz, P, N_Q_HEADS * HEAD_DIM)
    return o[:, pad:] @ w_o


def squared_relu_mlp(h, w_up, w_down):
    a = jax.nn.relu(h @ w_up)
    return (a * a) @ w_down


def _fwd_setup_inputs(seed: int = 0) -> dict:
    key = jax.random.key(seed)
    ks = jax.random.split(key, 20)
    D = D_MODEL
    f32 = jnp.float32

    def nrm(k, shape, scale):
        return jax.random.normal(k, shape, f32) * scale

    def gain(k):
        return jnp.ones((D,), f32) + 0.02 * jax.random.normal(k, (D,), f32)

    return {
        "x": nrm(ks[0], (BATCH, SEQ, D), 1.0),
        "meta_tokens": nrm(ks[1], (N_META, D), 1.0),
        "norm_mix_0": gain(ks[2]),
        "w_in_conv": nrm(ks[3], (D, 3 * D), D ** -0.5),
        "conv_w": nrm(ks[4], (CONV_WIDTH, D), CONV_WIDTH ** -0.5),
        "w_out_conv": nrm(ks[5], (D, D), D ** -0.5),
        "norm_mlp_0": gain(ks[6]),
        "w_up_0": nrm(ks[7], (D, D_FF), D ** -0.5),
        "w_down_0": nrm(ks[8], (D_FF, D), D_FF ** -0.5),
        "norm_mix_1": gain(ks[9]),
        "w_qkv": nrm(ks[10], (D, (N_Q_HEADS + 2 * N_KV_HEADS) * HEAD_DIM), D ** -0.5),
        "attn_sinks": nrm(ks[11], (N_Q_HEADS,), 0.5),
        "w_o": nrm(ks[12], (N_Q_HEADS * HEAD_DIM, D), (N_Q_HEADS * HEAD_DIM) ** -0.5),
        "norm_mlp_1": gain(ks[13]),
        "w_up_1": nrm(ks[14], (D, D_FF), D ** -0.5),
        "w_down_1": nrm(ks[15], (D_FF, D), D_FF ** -0.5),
        "norm_final": gain(ks[16]),
    }


def _fwd_reference(x, meta_tokens, norm_mix_0, w_in_conv, conv_w, w_out_conv, norm_mlp_0, w_up_0, w_down_0,
              norm_mix_1, w_qkv, attn_sinks, w_o, norm_mlp_1, w_up_1, w_down_1, norm_final):
    Bsz = x.shape[0]
    meta = jnp.broadcast_to(meta_tokens[None].astype(x.dtype), (Bsz, N_META, D_MODEL))
    h = jnp.concatenate([meta, x], axis=1)

    mixers = [
        lambda t: short_conv_mixer(t, w_in_conv, conv_w, w_out_conv),
        lambda t: swa_sink_mixer(t, w_qkv, attn_sinks, w_o),
    ]
    mix_norms = [norm_mix_0, norm_mix_1]
    mlps = [(norm_mlp_0, w_up_0, w_down_0), (norm_mlp_1, w_up_1, w_down_1)]

    for i in range(DEPTH):
        h = h + mixers[i % N_MIXERS](rms_norm(h, mix_norms[i]))
        g, wu, wd = mlps[i]
        h = h + squared_relu_mlp(rms_norm(h, g), wu, wd)

    out = rms_norm(h, norm_final)
    return out[:, N_META:]


import jax as _jax
import jax.numpy as _jnp

TWIN_FORMAT = 'train_step'
FWD_PARAMS = ['x', 'meta_tokens', 'norm_mix_0', 'w_in_conv', 'conv_w', 'w_out_conv', 'norm_mlp_0', 'w_up_0', 'w_down_0', 'norm_mix_1', 'w_qkv', 'attn_sinks', 'w_o', 'norm_mlp_1', 'w_up_1', 'w_down_1', 'norm_final']
TWIN_WEIGHTS = ['meta_tokens', 'norm_mix_0', 'w_in_conv', 'conv_w', 'w_out_conv', 'norm_mlp_0', 'w_up_0', 'w_down_0', 'norm_mix_1', 'w_qkv', 'attn_sinks', 'w_o', 'norm_mlp_1', 'w_up_1', 'w_down_1', 'norm_final']
TWIN_DIFF_INPUT = 'x'
TWIN_INPUTS = ['x', 'meta_tokens', 'norm_mix_0', 'w_in_conv', 'conv_w', 'w_out_conv', 'norm_mlp_0', 'w_up_0', 'w_down_0', 'norm_mix_1', 'w_qkv', 'attn_sinks', 'w_o', 'norm_mlp_1', 'w_up_1', 'w_down_1', 'norm_final', 'loss_target', 'm_meta_tokens', 'm_norm_mix_0', 'm_w_in_conv', 'm_conv_w', 'm_w_out_conv', 'm_norm_mlp_0', 'm_w_up_0', 'm_w_down_0', 'm_norm_mix_1', 'm_w_qkv', 'm_attn_sinks', 'm_w_o', 'm_norm_mlp_1', 'm_w_up_1', 'm_w_down_1', 'm_norm_final', 'v_meta_tokens', 'v_norm_mix_0', 'v_w_in_conv', 'v_conv_w', 'v_w_out_conv', 'v_norm_mlp_0', 'v_w_up_0', 'v_w_down_0', 'v_norm_mix_1', 'v_w_qkv', 'v_attn_sinks', 'v_w_o', 'v_norm_mlp_1', 'v_w_up_1', 'v_w_down_1', 'v_norm_final']
TWIN_OUTPUTS = ['loss', 'grad_x', 'grad_meta_tokens', 'grad_norm_mix_0', 'grad_w_in_conv', 'grad_conv_w', 'grad_w_out_conv', 'grad_norm_mlp_0', 'grad_w_up_0', 'grad_w_down_0', 'grad_norm_mix_1', 'grad_w_qkv', 'grad_attn_sinks', 'grad_w_o', 'grad_norm_mlp_1', 'grad_w_up_1', 'grad_w_down_1', 'grad_norm_final', 'delta_meta_tokens', 'delta_norm_mix_0', 'delta_w_in_conv', 'delta_conv_w', 'delta_w_out_conv', 'delta_norm_mlp_0', 'delta_w_up_0', 'delta_w_down_0', 'delta_norm_mix_1', 'delta_w_qkv', 'delta_attn_sinks', 'delta_w_o', 'delta_norm_mlp_1', 'delta_w_up_1', 'delta_w_down_1', 'delta_norm_final', 'new_m_meta_tokens', 'new_m_norm_mix_0', 'new_m_w_in_conv', 'new_m_conv_w', 'new_m_w_out_conv', 'new_m_norm_mlp_0', 'new_m_w_up_0', 'new_m_w_down_0', 'new_m_norm_mix_1', 'new_m_w_qkv', 'new_m_attn_sinks', 'new_m_w_o', 'new_m_norm_mlp_1', 'new_m_w_up_1', 'new_m_w_down_1', 'new_m_norm_final', 'new_v_meta_tokens', 'new_v_norm_mix_0', 'new_v_w_in_conv', 'new_v_conv_w', 'new_v_w_out_conv', 'new_v_norm_mlp_0', 'new_v_w_up_0', 'new_v_w_down_0', 'new_v_norm_mix_1', 'new_v_w_qkv', 'new_v_attn_sinks', 'new_v_w_o', 'new_v_norm_mlp_1', 'new_v_w_up_1', 'new_v_w_down_1', 'new_v_norm_final']
TWIN_LEAF_KINDS = {'loss': 'loss', 'grad_x': 'grad_x', 'grad_meta_tokens': 'grad_w', 'grad_norm_mix_0': 'grad_w', 'grad_w_in_conv': 'grad_w', 'grad_conv_w': 'grad_w', 'grad_w_out_conv': 'grad_w', 'grad_norm_mlp_0': 'grad_w', 'grad_w_up_0': 'grad_w', 'grad_w_down_0': 'grad_w', 'grad_norm_mix_1': 'grad_w', 'grad_w_qkv': 'grad_w', 'grad_attn_sinks': 'grad_w', 'grad_w_o': 'grad_w', 'grad_norm_mlp_1': 'grad_w', 'grad_w_up_1': 'grad_w', 'grad_w_down_1': 'grad_w', 'grad_norm_final': 'grad_w', 'delta_meta_tokens': 'delta_w', 'delta_norm_mix_0': 'delta_w', 'delta_w_in_conv': 'delta_w', 'delta_conv_w': 'delta_w', 'delta_w_out_conv': 'delta_w', 'delta_norm_mlp_0': 'delta_w', 'delta_w_up_0': 'delta_w', 'delta_w_down_0': 'delta_w', 'delta_norm_mix_1': 'delta_w', 'delta_w_qkv': 'delta_w', 'delta_attn_sinks': 'delta_w', 'delta_w_o': 'delta_w', 'delta_norm_mlp_1': 'delta_w', 'delta_w_up_1': 'delta_w', 'delta_w_down_1': 'delta_w', 'delta_norm_final': 'delta_w', 'new_m_meta_tokens': 'new_m', 'new_m_norm_mix_0': 'new_m', 'new_m_w_in_conv': 'new_m', 'new_m_conv_w': 'new_m', 'new_m_w_out_conv': 'new_m', 'new_m_norm_mlp_0': 'new_m', 'new_m_w_up_0': 'new_m', 'new_m_w_down_0': 'new_m', 'new_m_norm_mix_1': 'new_m', 'new_m_w_qkv': 'new_m', 'new_m_attn_sinks': 'new_m', 'new_m_w_o': 'new_m', 'new_m_norm_mlp_1': 'new_m', 'new_m_w_up_1': 'new_m', 'new_m_w_down_1': 'new_m', 'new_m_norm_final': 'new_m', 'new_v_meta_tokens': 'new_v', 'new_v_norm_mix_0': 'new_v', 'new_v_w_in_conv': 'new_v', 'new_v_conv_w': 'new_v', 'new_v_w_out_conv': 'new_v', 'new_v_norm_mlp_0': 'new_v', 'new_v_w_up_0': 'new_v', 'new_v_w_down_0': 'new_v', 'new_v_norm_mix_1': 'new_v', 'new_v_w_qkv': 'new_v', 'new_v_attn_sinks': 'new_v', 'new_v_w_o': 'new_v', 'new_v_norm_mlp_1': 'new_v', 'new_v_w_up_1': 'new_v', 'new_v_w_down_1': 'new_v', 'new_v_norm_final': 'new_v'}


def _forward(args):
    return _fwd_reference(*[args[k] for k in FWD_PARAMS])


def _output_shape():
    def fwd():
        inp = _fwd_setup_inputs(0)
        return _fwd_reference(*[inp[k] for k in FWD_PARAMS])
    out = _jax.eval_shape(fwd)
    return out.shape, out.dtype

N_MICROBATCH = 1
ADAM_LR = 0.001
ADAM_B1 = 0.9
ADAM_B2 = 0.999
ADAM_EPS = 1e-08
ADAM_WD = 0.01
ADAM_STEP = 10
PER_EXAMPLE_BATCH_AXIS = {'x': 0, 'loss_target': 0}
SHARED_INPUTS = []
_WEIGHT_DTYPES = {'meta_tokens': _jnp.float32, 'norm_mix_0': _jnp.float32, 'w_in_conv': _jnp.float32, 'conv_w': _jnp.float32, 'w_out_conv': _jnp.float32, 'norm_mlp_0': _jnp.float32, 'w_up_0': _jnp.float32, 'w_down_0': _jnp.float32, 'norm_mix_1': _jnp.float32, 'w_qkv': _jnp.float32, 'attn_sinks': _jnp.float32, 'w_o': _jnp.float32, 'norm_mlp_1': _jnp.float32, 'w_up_1': _jnp.float32, 'w_down_1': _jnp.float32, 'norm_final': _jnp.float32}
MOMENT_SCALE = {'meta_tokens': 3.631818e-03, 'norm_mix_0': 1.719882e-01, 'w_in_conv': 9.924463e-02, 'conv_w': 9.962783e-02, 'w_out_conv': 9.912304e-02, 'norm_mlp_0': 1.001376e-01, 'w_up_0': 4.955886e-02, 'w_down_0': 1.091912e-01, 'norm_mix_1': 3.809186e-02, 'w_qkv': 3.507546e-02, 'attn_sinks': 2.444729e-03, 'w_o': 3.864580e-02, 'norm_mlp_1': 8.075012e-02, 'w_up_1': 4.146385e-02, 'w_down_1': 9.106936e-02, 'norm_final': 3.251556e+01}


def _to_microbatches(a, axis):
    t = _jnp.moveaxis(a, axis, 0)
    t = t.reshape((N_MICROBATCH, t.shape[0] // N_MICROBATCH) + t.shape[1:])
    return _jnp.moveaxis(t, 1, axis + 1)


def setup_inputs(seed: int = 0) -> dict:
    inp = _fwd_setup_inputs(seed)
    key = _jax.random.fold_in(_jax.random.key(seed), 7919)
    shape, _ = _output_shape()
    out = dict(inp)
    out["loss_target"] = _jax.random.normal(_jax.random.fold_in(key, 0), shape, _jnp.float32)
    for i, name in enumerate(TWIN_WEIGHTS):
        w = inp[name].astype(_jnp.float32)
        if MOMENT_SCALE is None:
            s = _jnp.sqrt(_jnp.mean(_jnp.square(w)) + 1e-30)
        else:
            s = MOMENT_SCALE[name]
        km, kv = _jax.random.split(_jax.random.fold_in(key, i + 1))
        out[name] = w
        out["m_" + name] = s * _jax.random.normal(km, w.shape, _jnp.float32)
        out["v_" + name] = (s * s) * _jax.random.uniform(kv, w.shape, _jnp.float32, 0.5, 1.5)
    if N_MICROBATCH > 1:
        for name, axis in PER_EXAMPLE_BATCH_AXIS.items():
            out[name] = _to_microbatches(out[name], axis)
    return {'x': out['x'], 'meta_tokens': out['meta_tokens'], 'norm_mix_0': out['norm_mix_0'], 'w_in_conv': out['w_in_conv'], 'conv_w': out['conv_w'], 'w_out_conv': out['w_out_conv'], 'norm_mlp_0': out['norm_mlp_0'], 'w_up_0': out['w_up_0'], 'w_down_0': out['w_down_0'], 'norm_mix_1': out['norm_mix_1'], 'w_qkv': out['w_qkv'], 'attn_sinks': out['attn_sinks'], 'w_o': out['w_o'], 'norm_mlp_1': out['norm_mlp_1'], 'w_up_1': out['w_up_1'], 'w_down_1': out['w_down_1'], 'norm_final': out['norm_final'], 'loss_target': out['loss_target'], 'm_meta_tokens': out['m_meta_tokens'], 'm_norm_mix_0': out['m_norm_mix_0'], 'm_w_in_conv': out['m_w_in_conv'], 'm_conv_w': out['m_conv_w'], 'm_w_out_conv': out['m_w_out_conv'], 'm_norm_mlp_0': out['m_norm_mlp_0'], 'm_w_up_0': out['m_w_up_0'], 'm_w_down_0': out['m_w_down_0'], 'm_norm_mix_1': out['m_norm_mix_1'], 'm_w_qkv': out['m_w_qkv'], 'm_attn_sinks': out['m_attn_sinks'], 'm_w_o': out['m_w_o'], 'm_norm_mlp_1': out['m_norm_mlp_1'], 'm_w_up_1': out['m_w_up_1'], 'm_w_down_1': out['m_w_down_1'], 'm_norm_final': out['m_norm_final'], 'v_meta_tokens': out['v_meta_tokens'], 'v_norm_mix_0': out['v_norm_mix_0'], 'v_w_in_conv': out['v_w_in_conv'], 'v_conv_w': out['v_conv_w'], 'v_w_out_conv': out['v_w_out_conv'], 'v_norm_mlp_0': out['v_norm_mlp_0'], 'v_w_up_0': out['v_w_up_0'], 'v_w_down_0': out['v_w_down_0'], 'v_norm_mix_1': out['v_norm_mix_1'], 'v_w_qkv': out['v_w_qkv'], 'v_attn_sinks': out['v_attn_sinks'], 'v_w_o': out['v_w_o'], 'v_norm_mlp_1': out['v_norm_mlp_1'], 'v_w_up_1': out['v_w_up_1'], 'v_w_down_1': out['v_w_down_1'], 'v_norm_final': out['v_norm_final']}


def _loss(weights, diff, rest, loss_target):
    with _jax.named_scope("forward"):
        args = {**rest, TWIN_DIFF_INPUT: diff, **{k: w.astype(_WEIGHT_DTYPES[k]) for k, w in weights.items()}}
        y = _forward(args)
    with _jax.named_scope("loss_head"):
        err = _jnp.square(y.astype(_jnp.float32) - loss_target)
        return 0.5 * _jnp.sum(_jnp.mean(err, axis=-1)) if err.ndim else 0.5 * err


def _adamw(w, g, m, v):
    m = ADAM_B1 * m + (1.0 - ADAM_B1) * g
    v = ADAM_B2 * v + (1.0 - ADAM_B2) * _jnp.square(g)
    m_hat = m / (1.0 - ADAM_B1 ** ADAM_STEP)
    v_hat = v / (1.0 - ADAM_B2 ** ADAM_STEP)
    delta = -ADAM_LR * (m_hat / (_jnp.sqrt(v_hat) + ADAM_EPS) + ADAM_WD * w)
    return delta, m, v


def reference(x, meta_tokens, norm_mix_0, w_in_conv, conv_w, w_out_conv, norm_mlp_0, w_up_0, w_down_0, norm_mix_1, w_qkv, attn_sinks, w_o, norm_mlp_1, w_up_1, w_down_1, norm_final, loss_target, m_meta_tokens, m_norm_mix_0, m_w_in_conv, m_conv_w, m_w_out_conv, m_norm_mlp_0, m_w_up_0, m_w_down_0, m_norm_mix_1, m_w_qkv, m_attn_sinks, m_w_o, m_norm_mlp_1, m_w_up_1, m_w_down_1, m_norm_final, v_meta_tokens, v_norm_mix_0, v_w_in_conv, v_conv_w, v_w_out_conv, v_norm_mlp_0, v_w_up_0, v_w_down_0, v_norm_mix_1, v_w_qkv, v_attn_sinks, v_w_o, v_norm_mlp_1, v_w_up_1, v_w_down_1, v_norm_final):
    given = dict(x=x, meta_tokens=meta_tokens, norm_mix_0=norm_mix_0, w_in_conv=w_in_conv, conv_w=conv_w, w_out_conv=w_out_conv, norm_mlp_0=norm_mlp_0, w_up_0=w_up_0, w_down_0=w_down_0, norm_mix_1=norm_mix_1, w_qkv=w_qkv, attn_sinks=attn_sinks, w_o=w_o, norm_mlp_1=norm_mlp_1, w_up_1=w_up_1, w_down_1=w_down_1, norm_final=norm_final, loss_target=loss_target, m_meta_tokens=m_meta_tokens, m_norm_mix_0=m_norm_mix_0, m_w_in_conv=m_w_in_conv, m_conv_w=m_conv_w, m_w_out_conv=m_w_out_conv, m_norm_mlp_0=m_norm_mlp_0, m_w_up_0=m_w_up_0, m_w_down_0=m_w_down_0, m_norm_mix_1=m_norm_mix_1, m_w_qkv=m_w_qkv, m_attn_sinks=m_attn_sinks, m_w_o=m_w_o, m_norm_mlp_1=m_norm_mlp_1, m_w_up_1=m_w_up_1, m_w_down_1=m_w_down_1, m_norm_final=m_norm_final, v_meta_tokens=v_meta_tokens, v_norm_mix_0=v_norm_mix_0, v_w_in_conv=v_w_in_conv, v_conv_w=v_conv_w, v_w_out_conv=v_w_out_conv, v_norm_mlp_0=v_norm_mlp_0, v_w_up_0=v_w_up_0, v_w_down_0=v_w_down_0, v_norm_mix_1=v_norm_mix_1, v_w_qkv=v_w_qkv, v_attn_sinks=v_attn_sinks, v_w_o=v_w_o, v_norm_mlp_1=v_norm_mlp_1, v_w_up_1=v_w_up_1, v_w_down_1=v_w_down_1, v_norm_final=v_norm_final)
    weights = {n: given[n] for n in TWIN_WEIGHTS}
    shared = {n: given[n] for n in SHARED_INPUTS}
    per_example = {n: given[n] for n in ['x']}
    grad_fn = _jax.value_and_grad(_loss, argnums=(0, 1))

    def one_microbatch(ex, loss_target):
        ex = dict(ex)
        diff = ex.pop(TWIN_DIFF_INPUT)
        return grad_fn(weights, diff, {**shared, **ex}, loss_target)

    if N_MICROBATCH == 1:
        loss, (grad_w, grad_x) = one_microbatch(per_example, given["loss_target"])
    else:
        def body(carry, xs):
            loss_sum, grad_sum = carry
            l_k, (gw_k, gx_k) = one_microbatch(xs[0], xs[1])
            with _jax.named_scope("update"):
                return (loss_sum + l_k, _jax.tree.map(_jnp.add, grad_sum, gw_k)), gx_k

        init = (_jnp.zeros((), _jnp.float32), _jax.tree.map(_jnp.zeros_like, weights))
        (loss, grad_w), grad_x = _jax.lax.scan(body, init, (per_example, given["loss_target"]))
    with _jax.named_scope("update"):
        delta_w, new_m, new_v = {}, {}, {}
        for n in TWIN_WEIGHTS:
            delta_w[n], new_m[n], new_v[n] = _adamw(weights[n], grad_w[n], given["m_" + n], given["v_" + n])
    return (loss, grad_x, *[grad_w[n] for n in TWIN_WEIGHTS], *[delta_w[n] for n in TWIN_WEIGHTS],
            *[new_m[n] for n in TWIN_WEIGHTS], *[new_v[n] for n in TWIN_WEIGHTS])
```

```python
import functools

import jax
import jax.numpy as jnp
from jax import lax
from jax.experimental import pallas as pl
from jax.experimental.pallas import tpu as pltpu

F32 = jnp.float32
BF16 = jnp.bfloat16

D_MODEL = 2048
SEQ = 8192
N_META = 16
CONV_WIDTH = 3
HEAD_DIM = 64
N_Q_HEADS = 32
N_KV_HEADS = 4
BLOCK = 128
ROPE_THETA = 10000.0
D_FF = 4 * D_MODEL
RMS_EPS = 1e-5
NEG_INF = -1e30

ADAM_LR = 0.001
ADAM_B1 = 0.9
ADAM_B2 = 0.999
ADAM_EPS = 1e-08
ADAM_WD = 0.01
ADAM_STEP = 10

N_CHIPS = 4
N_DEV = 8
MESH = pl.DeviceIdType.MESH
VMEM_LIMIT = 56 * 1024 * 1024
SMALL_ROWS = 32
ROW0 = BLOCK


def _pick(n, target, mult):
    best = None
    for t in range(mult, min(n, target) + 1, mult):
        if n % t == 0:
            best = t
    assert best is not None, (n, target, mult)
    return best


def _params(sem=None):
    return pltpu.CompilerParams(dimension_semantics=sem, vmem_limit_bytes=VMEM_LIMIT)


def _mm(name, a, b, *, dims, grid, a_spec, b_spec, out_shape, out_spec, nk, acc_shape,
        extras=(), extra_specs=(), a_pro=None, epi=None):
    n_ex = len(extras)
    acc_in_out = epi is None and out_shape.dtype == F32

    def body(*refs):
        a_ref, b_ref = refs[0], refs[1]
        ex = refs[2:2 + n_ex]
        o_ref = refs[2 + n_ex]
        av = a_ref[...]
        if a_pro is not None:
            av = a_pro(av)
        part = lax.dot_general(av, b_ref[...], dims, preferred_element_type=F32)

        def finish(acc):
            r = acc if epi is None else epi(acc, *[e[...] for e in ex])
            o_ref[...] = r.astype(o_ref.dtype)

        if nk == 1:
            finish(part)
            return
        acc_ref = o_ref if acc_in_out else refs[3 + n_ex]
        kk = pl.program_id(len(grid) - 1)

        @pl.when(kk == 0)
        def _():
            acc_ref[...] = part

        @pl.when(kk > 0)
        def _():
            acc_ref[...] += part

        if not acc_in_out:
            @pl.when(kk == nk - 1)
            def _():
                finish(acc_ref[...])

    scratch = [] if (nk == 1 or acc_in_out) else [pltpu.VMEM(acc_shape, F32)]
    return pl.pallas_call(
        body, name=name, grid=grid, out_shape=out_shape,
        in_specs=[a_spec, b_spec, *extra_specs], out_specs=out_spec, scratch_shapes=scratch,
        compiler_params=_params(("parallel", "parallel", "arbitrary")),
    )(a, b, *extras)


_NN = (((1,), (0,)), ((), ()))
_NT = (((1,), (1,)), ((), ()))
_TN = (((0,), (0,)), ((), ()))


def _mm_nn_cols(name, a, w, out_dtype, epi=None, extras=()):
    m, k = a.shape
    _, _, ns = w.shape
    tm, tn = _pick(m, 832, 16), _pick(ns, 1024, 128)
    per = ns // tn
    tile = pl.BlockSpec((tm, tn), lambda j, i, kk: (i, j))
    return _mm(name, a, w, dims=_NN, grid=(N_CHIPS * per, m // tm, 1),
               a_spec=pl.BlockSpec((tm, k), lambda j, i, kk: (i, 0)),
               b_spec=pl.BlockSpec((None, k, tn), lambda j, i, kk: (j // per, 0, j % per)),
               out_shape=jax.ShapeDtypeStruct((m, N_CHIPS * ns), out_dtype), out_spec=tile,
               nk=1, acc_shape=(tm, tn), extras=extras, extra_specs=[tile] * len(extras), epi=epi)


def _mm_nn_rows(name, a, w, out_dtype, a_pro=None, epi=None, extras=()):
    m, k = a.shape
    _, n = w.shape
    tm, tn, tk = _pick(m, 832, 16), _pick(n, 1024, 128), _pick(k, 2048, 128)
    tile = pl.BlockSpec((tm, tn), lambda j, i, kk: (i, j))
    return _mm(name, a, w, dims=_NN, grid=(n // tn, m // tm, k // tk),
               a_spec=pl.BlockSpec((tm, tk), lambda j, i, kk: (i, kk)),
               b_spec=pl.BlockSpec((tk, tn), lambda j, i, kk: (kk, j)),
               out_shape=jax.ShapeDtypeStruct((m, n), out_dtype), out_spec=tile,
               nk=k // tk, acc_shape=(tm, tn), extras=extras, extra_specs=[tile] * len(extras),
               a_pro=a_pro, epi=epi)


def _mm_nt_rows(name, a, w, out_dtype, epi=None, extras=()):
    m, c = a.shape
    r, _ = w.shape
    tm, tn = _pick(m, 832, 16), _pick(r, 1024, 128)
    tile = pl.BlockSpec((tm, tn), lambda j, i, kk: (i, j))
    return _mm(name, a, w, dims=_NT, grid=(r // tn, m // tm, 1),
               a_spec=pl.BlockSpec((tm, c), lambda j, i, kk: (i, 0)),
               b_spec=pl.BlockSpec((tn, c), lambda j, i, kk: (j, 0)),
               out_shape=jax.ShapeDtypeStruct((m, r), out_dtype), out_spec=tile,
               nk=1, acc_shape=(tm, tn), extras=extras, extra_specs=[tile] * len(extras), epi=epi)


def _mm_nt_cols(name, a, w, out_dtype):
    m, _ = a.shape
    _, r, ns = w.shape
    tm, tn, tk = _pick(m, 832, 16), _pick(r, 1024, 128), _pick(ns, 1024, 128)
    per = ns // tk
    return _mm(name, a, w, dims=_NT, grid=(r // tn, m // tm, N_CHIPS * per),
               a_spec=pl.BlockSpec((tm, tk), lambda j, i, kk: (i, kk)),
               b_spec=pl.BlockSpec((None, tn, tk), lambda j, i, kk: (kk // per, j, kk % per)),
               out_shape=jax.ShapeDtypeStruct((m, r), out_dtype),
               out_spec=pl.BlockSpec((tm, tn), lambda j, i, kk: (i, j)),
               nk=N_CHIPS * per, acc_shape=(tm, tn))


def _mm_tn(name, a, b, stacked, a_pro=None):
    t, ka = a.shape
    _, nb = b.shape
    ns = nb // N_CHIPS if stacked else nb
    tt, ta, tb = _pick(t, 640, 128), _pick(ka, 2048, 128), _pick(ns, 1024, 128)
    if stacked:
        per = ns // tb
        out_shape = jax.ShapeDtypeStruct((N_CHIPS, ka, ns), F32)
        out_spec = pl.BlockSpec((None, ta, tb), lambda i, j, kk: (j // per, i, j % per))
    else:
        out_shape = jax.ShapeDtypeStruct((ka, nb), F32)
        out_spec = pl.BlockSpec((ta, tb), lambda i, j, kk: (i, j))
    return _mm(name, a, b, dims=_TN, grid=(ka // ta, nb // tb, t // tt),
               a_spec=pl.BlockSpec((tt, ta), lambda i, j, kk: (kk, i)),
               b_spec=pl.BlockSpec((tt, tb), lambda i, j, kk: (kk, j)),
               out_shape=out_shape, out_spec=out_spec, nk=t // tt, acc_shape=(ta, tb), a_pro=a_pro)


def _relu_sq(z):
    a = jnp.maximum(z, 0)
    return a * a


def _rms_fwd(name, h, g):
    m, d = h.shape
    tr = _pick(m, 256, 16)

    def body(h_ref, g_ref, o_ref):
        x = h_ref[...]
        rstd = lax.rsqrt(jnp.mean(x * x, axis=-1, keepdims=True) + RMS_EPS)
        o_ref[...] = ((x * rstd) * g_ref[...]).astype(BF16)

    row = pl.BlockSpec((tr, d), lambda i: (i, 0))
    return pl.pallas_call(
        body, name=name, grid=(m // tr,), out_shape=jax.ShapeDtypeStruct((m, d), BF16),
        in_specs=[row, pl.BlockSpec((1, d), lambda i: (0, 0))], out_specs=row,
        compiler_params=_params(("parallel",)),
    )(h, g.reshape(1, d))


def _rms_bwd_math(x, g, dn):
    rstd = lax.rsqrt(jnp.mean(x * x, axis=-1, keepdims=True) + RMS_EPS)
    xhat = x * rstd
    dxhat = dn * g
    dx = rstd * (dxhat - xhat * jnp.mean(dxhat * xhat, axis=-1, keepdims=True))
    return dx, dn * xhat


def _fold8(v):
    r, c = v.shape
    return jnp.sum(v.reshape(r // 8, 8, c), axis=0)


def _rms_bwd(name, dn, h, g, dh_in):
    m, d = h.shape
    tr = _pick(m, 256, 16)
    nt = m // tr

    def body(dn_ref, h_ref, g_ref, dh_ref, o_ref, ob_ref, dg_ref):
        dx, dgp = _rms_bwd_math(h_ref[...], g_ref[...], dn_ref[...])
        dh = dh_ref[...] + dx
        o_ref[...] = dh
        ob_ref[...] = dh.astype(BF16)

        @pl.when(pl.program_id(0) == 0)
        def _():
            dg_ref[...] = jnp.zeros_like(dg_ref)

        dg_ref[...] += _fold8(dgp)

    row = pl.BlockSpec((tr, d), lambda i: (i, 0))
    return pl.pallas_call(
        body, name=name, grid=(nt,),
        out_shape=(jax.ShapeDtypeStruct((m, d), F32), jax.ShapeDtypeStruct((m, d), BF16),
                   jax.ShapeDtypeStruct((8, d), F32)),
        in_specs=[row, row, pl.BlockSpec((1, d), lambda i: (0, 0)), row],
        out_specs=(row, row, pl.BlockSpec((8, d), lambda i: (0, 0))),
        compiler_params=_params(("arbitrary",)),
    )(dn, h, g.reshape(1, d), dh_in)


def _loss_head(h, g, target):
    m, d = h.shape
    tr = BLOCK

    def body(h_ref, g_ref, t_ref, loss_ref, o_ref, ob_ref, dg_ref):
        i = pl.program_id(0)
        x = h_ref[...]
        gv = g_ref[...]
        rstd = lax.rsqrt(jnp.mean(x * x, axis=-1, keepdims=True) + RMS_EPS)
        err = jnp.where(i > 0, (x * rstd) * gv - t_ref[...], 0.0)
        dx, dgp = _rms_bwd_math(x, gv, err * (1.0 / d))
        o_ref[...] = dx
        ob_ref[...] = dx.astype(BF16)

        @pl.when(i == 0)
        def _():
            dg_ref[...] = jnp.zeros_like(dg_ref)
            loss_ref[...] = jnp.zeros_like(loss_ref)

        dg_ref[...] += _fold8(dgp)
        sq = jnp.mean(err * err, axis=-1, keepdims=True)
        loss_ref[...] += 0.5 * jnp.sum(sq, axis=0, keepdims=True)

    row = pl.BlockSpec((tr, d), lambda i: (i, 0))
    return pl.pallas_call(
        body, name="loss_head", grid=(m // tr,),
        out_shape=(jax.ShapeDtypeStruct((8, 128), F32), jax.ShapeDtypeStruct((m, d), F32),
                   jax.ShapeDtypeStruct((m, d), BF16), jax.ShapeDtypeStruct((8, d), F32)),
        in_specs=[row, pl.BlockSpec((1, d), lambda i: (0, 0)),
                  pl.BlockSpec((tr, d), lambda i: (jnp.maximum(i - 1, 0), 0))],
        out_specs=(pl.BlockSpec((8, 128), lambda i: (0, 0)), row, row,
                   pl.BlockSpec((8, d), lambda i: (0, 0))),
        compiler_params=_params(("arbitrary",)),
    )(h, g.reshape(1, d), target)


HALO = 16


def _shift_down(cat, k):
    return pltpu.roll(cat, k, axis=0)[HALO:]


def _shift_up(cat, k):
    n = cat.shape[0]
    return pltpu.roll(cat, n - k, axis=0)[:n - HALO]


def _conv_fwd(bcu, cw):
    m, d3 = bcu.shape
    d = d3 // 3
    tr, tc = _pick(m, 416, 16), _pick(d, 512, 128)
    nd, hb = d // tc, tr // HALO

    def body(b_ref, c_ref, u_ref, ch_ref, uh_ref, w_ref, o_ref):
        i = pl.program_id(0)
        v = c_ref[...].astype(F32) * u_ref[...].astype(F32)
        vh = jnp.where(i > 0, ch_ref[...].astype(F32) * uh_ref[...].astype(F32), 0.0)
        cat = jnp.concatenate([vh, v], axis=0)
        w = w_ref[...]
        conv = w[2:3] * v + w[1:2] * _shift_down(cat, 1) + w[0:1] * _shift_down(cat, 2)
        o_ref[...] = (b_ref[...].astype(F32) * conv).astype(BF16)

    def part(p):
        return pl.BlockSpec((tr, tc), lambda i, j: (i, p * nd + j))

    def halo(p):
        return pl.BlockSpec((HALO, tc), lambda i, j: (jnp.maximum(i * hb - 1, 0), p * nd + j))

    return pl.pallas_call(
        body, name="conv_fwd", grid=(m // tr, nd), out_shape=jax.ShapeDtypeStruct((m, d), BF16),
        in_specs=[part(0), part(1), part(2), halo(1), halo(2), pl.BlockSpec((8, tc), lambda i, j: (0, j))],
        out_specs=pl.BlockSpec((tr, tc), lambda i, j: (i, j)),
        compiler_params=_params(("parallel", "parallel")),
    )(bcu, bcu, bcu, bcu, bcu, cw)


def _conv_bwd(bcu, cw, dg):
    m, d3 = bcu.shape
    d = d3 // 3
    tr, tc = _pick(m, 416, 16), _pick(d, 512, 128)
    nd, hb, nt = d // tc, tr // HALO, m // tr

    def body(b_ref, c_ref, u_ref, ch_ref, uh_ref, bn_ref, dg_ref, dgn_ref, w_ref, o_ref, dw_ref):
        i, p = pl.program_id(1), pl.program_id(2)
        w = w_ref[...]
        b, c, u = b_ref[...].astype(F32), c_ref[...].astype(F32), u_ref[...].astype(F32)
        dgv = dg_ref[...].astype(F32)
        v = c * u
        vh = jnp.where(i > 0, ch_ref[...].astype(F32) * uh_ref[...].astype(F32), 0.0)
        cat = jnp.concatenate([vh, v], axis=0)
        v1, v2 = _shift_down(cat, 1), _shift_down(cat, 2)
        dconv = dgv * b

        @pl.when(p == 0)
        def _():
            o_ref[...] = (dgv * (w[2:3] * v + w[1:2] * v1 + w[0:1] * v2)).astype(BF16)

            @pl.when(i == 0)
            def _():
                dw_ref[...] = jnp.zeros_like(dw_ref)

            taps = [jnp.sum(dconv * t, axis=0, keepdims=True) for t in (v2, v1, v)]
            dw_ref[...] += jnp.concatenate(taps + [jnp.zeros((5, tc), F32)], axis=0)

        @pl.when(p > 0)
        def _():
            nxt = jnp.where(i < nt - 1, dgn_ref[...].astype(F32) * bn_ref[...].astype(F32), 0.0)
            cat2 = jnp.concatenate([dconv, nxt], axis=0)
            dv = w[2:3] * dconv + w[1:2] * _shift_up(cat2, 1) + w[0:1] * _shift_up(cat2, 2)
            o_ref[...] = jnp.where(p == 1, dv * u, dv * c).astype(BF16)

    def part(q):
        return pl.BlockSpec((tr, tc), lambda j, i, p: (i, q * nd + j))

    def before(q):
        return pl.BlockSpec((HALO, tc), lambda j, i, p: (jnp.maximum(i * hb - 1, 0), q * nd + j))

    def after(q):
        return pl.BlockSpec((HALO, tc), lambda j, i, p: (jnp.minimum((i + 1) * hb, m // HALO - 1), q * nd + j))

    return pl.pallas_call(
        body, name="conv_bwd", grid=(nd, nt, 3),
        out_shape=(jax.ShapeDtypeStruct((m, d3), BF16), jax.ShapeDtypeStruct((8, d), F32)),
        in_specs=[part(0), part(1), part(2), before(1), before(2), after(0),
                  pl.BlockSpec((tr, tc), lambda j, i, p: (i, j)), after(0),
                  pl.BlockSpec((8, tc), lambda j, i, p: (0, j))],
        out_specs=(pl.BlockSpec((tr, tc), lambda j, i, p: (i, p * nd + j)),
                   pl.BlockSpec((8, tc), lambda j, i, p: (0, j))),
        compiler_params=_params(("parallel", "arbitrary", "arbitrary")),
    )(bcu, bcu, bcu, bcu, bcu, bcu, dg, dg, cw)


def _rope_tables(m):
    pad = ROW0 - N_META
    pos = jnp.arange(m, dtype=F32) - pad
    inv = ROPE_THETA ** (-jnp.arange(0, HEAD_DIM, 2, dtype=F32) / HEAD_DIM)
    ang = pos[:, None] * inv[None, :]
    return jnp.cos(ang), jnp.sin(ang)


def _rope(x, c, s):
    half = HEAD_DIM // 2
    x1, x2 = x[:, :half], x[:, half:]
    return jnp.concatenate([x1 * c - x2 * s, x2 * c + x1 * s], axis=-1)


def _rope_t(y, c, s):
    half = HEAD_DIM // 2
    y1, y2 = y[:, :half], y[:, half:]
    return jnp.concatenate([y1 * c + y2 * s, y2 * c - y1 * s], axis=-1)


def _qkv_split(qkv, cos, sin):
    m = qkv.shape[0]
    nb, grp = m // BLOCK, N_Q_HEADS // N_KV_HEADS
    scale = HEAD_DIM ** -0.5

    def body(x_ref, c_ref, s_ref, q_ref, k_ref, v_ref):
        c, s = c_ref[...], s_ref[...]
        for h in range(N_KV_HEADS):
            for g in range(grp):
                col = (h * grp + g) * HEAD_DIM
                xq = x_ref[:, col:col + HEAD_DIM].astype(F32)
                q_ref[h, g * BLOCK:(g + 1) * BLOCK, :] = (_rope(xq, c, s) * scale).astype(BF16)
            col = (N_Q_HEADS + h) * HEAD_DIM
            k_ref[h] = _rope(x_ref[:, col:col + HEAD_DIM].astype(F32), c, s).astype(BF16)
            col = (N_Q_HEADS + N_KV_HEADS + h) * HEAD_DIM
            v_ref[h] = x_ref[:, col:col + HEAD_DIM]

    tab = pl.BlockSpec((BLOCK, HEAD_DIM // 2), lambda i: (i, 0))
    kv = pl.BlockSpec((N_KV_HEADS, BLOCK, HEAD_DIM), lambda i: (0, i, 0))
    return pl.pallas_call(
        body, name="qkv_split", grid=(nb,),
        out_shape=(jax.ShapeDtypeStruct((N_KV_HEADS, nb * grp * BLOCK, HEAD_DIM), BF16),
                   jax.ShapeDtypeStruct((N_KV_HEADS, m, HEAD_DIM), BF16),
                   jax.ShapeDtypeStruct((N_KV_HEADS, m, HEAD_DIM), BF16)),
        in_specs=[pl.BlockSpec((BLOCK, qkv.shape[1]), lambda i: (i, 0)), tab, tab],
        out_specs=(pl.BlockSpec((N_KV_HEADS, grp * BLOCK, HEAD_DIM), lambda i: (0, i, 0)), kv, kv),
        compiler_params=_params(("parallel",)),
    )(qkv, cos, sin)


def _heads_merge(name, o):
    grp = N_Q_HEADS // N_KV_HEADS
    nb = o.shape[1] // (grp * BLOCK)

    def body(o_ref, x_ref):
        for h in range(N_KV_HEADS):
            for g in range(grp):
                col = (h * grp + g) * HEAD_DIM
                x_ref[:, col:col + HEAD_DIM] = o_ref[h, g * BLOCK:(g + 1) * BLOCK, :]

    return pl.pallas_call(
        body, name=name, grid=(nb,),
        out_shape=jax.ShapeDtypeStruct((nb * BLOCK, N_Q_HEADS * HEAD_DIM), o.dtype),
        in_specs=[pl.BlockSpec((N_KV_HEADS, grp * BLOCK, HEAD_DIM), lambda i: (0, i, 0))],
        out_specs=pl.BlockSpec((BLOCK, N_Q_HEADS * HEAD_DIM), lambda i: (i, 0)),
        compiler_params=_params(("parallel",)),
    )(o)


def _heads_split(name, x):
    grp = N_Q_HEADS // N_KV_HEADS
    nb = x.shape[0] // BLOCK

    def body(x_ref, o_ref):
        for h in range(N_KV_HEADS):
            for g in range(grp):
                col = (h * grp + g) * HEAD_DIM
                o_ref[h, g * BLOCK:(g + 1) * BLOCK, :] = x_ref[:, col:col + HEAD_DIM]

    return pl.pallas_call(
        body, name=name, grid=(nb,),
        out_shape=jax.ShapeDtypeStruct((N_KV_HEADS, nb * grp * BLOCK, HEAD_DIM), x.dtype),
        in_specs=[pl.BlockSpec((BLOCK, N_Q_HEADS * HEAD_DIM), lambda i: (i, 0))],
        out_specs=pl.BlockSpec((N_KV_HEADS, grp * BLOCK, HEAD_DIM), lambda i: (0, i, 0)),
        compiler_params=_params(("parallel",)),
    )(x)


def _attn_probs(i, q, kb, sink):
    rows = q.shape[0]
    grp = rows // BLOCK
    s = lax.dot_general(q, kb, _NT, preferred_element_type=F32)
    r = lax.broadcasted_iota(jnp.int32, (BLOCK, 2 * BLOCK), 0)
    cidx = lax.broadcasted_iota(jnp.int32, (BLOCK, 2 * BLOCK), 1)
    key = (i - 1) * BLOCK + cidx
    allowed = (cidx > r) & (cidx <= r + BLOCK) & (key >= ROW0 - N_META)
    s = jnp.where(allowed[None], s.reshape(grp, BLOCK, 2 * BLOCK), NEG_INF).reshape(rows, 2 * BLOCK)
    mx = jnp.maximum(jnp.max(s, axis=-1, keepdims=True), sink)
    e = jnp.exp(s - mx)
    es = jnp.exp(sink - mx)
    den = jnp.sum(e, axis=-1, keepdims=True) + es
    return e / den, es / den


def _attn_specs(grp):
    q = pl.BlockSpec((None, grp * BLOCK, HEAD_DIM), lambda h, i: (h, i, 0))
    cur = pl.BlockSpec((None, BLOCK, HEAD_DIM), lambda h, i: (h, i, 0))
    prev = pl.BlockSpec((None, BLOCK, HEAD_DIM), lambda h, i: (h, jnp.maximum(i - 1, 0), 0))
    sink = pl.BlockSpec((None, grp * BLOCK, 1), lambda h, i: (h, 0, 0))
    return q, cur, prev, sink


def _attn_fwd(q, k, v, sink_rows):
    grp = N_Q_HEADS // N_KV_HEADS
    nb = k.shape[1] // BLOCK

    def body(q_ref, kc_ref, kp_ref, vc_ref, vp_ref, s_ref, o_ref):
        kb = jnp.concatenate([kp_ref[...], kc_ref[...]], axis=0)
        vb = jnp.concatenate([vp_ref[...], vc_ref[...]], axis=0)
        p, _ = _attn_probs(pl.program_id(1), q_ref[...], kb, s_ref[...])
        o_ref[...] = jnp.dot(p.astype(BF16), vb, preferred_element_type=F32).astype(BF16)

    qs, cur, prev, sink = _attn_specs(grp)
    return pl.pallas_call(
        body, name="attn_fwd", grid=(N_KV_HEADS, nb), out_shape=jax.ShapeDtypeStruct(q.shape, BF16),
        in_specs=[qs, cur, prev, cur, prev, sink], out_specs=qs,
        compiler_params=_params(("parallel", "parallel")),
    )(q, k, k, v, v, sink_rows)


def _attn_bwd(q, k, v, sink_rows, do):
    grp = N_Q_HEADS // N_KV_HEADS
    nb = k.shape[1] // BLOCK

    def body(q_ref, kc_ref, kp_ref, vc_ref, vp_ref, s_ref, do_ref, dq_ref, dk_ref, dv_ref, ds_ref):
        i = pl.program_id(1)
        qv, dov = q_ref[...], do_ref[...]
        kb = jnp.concatenate([kp_ref[...], kc_ref[...]], axis=0)
        vb = jnp.concatenate([vp_ref[...], vc_ref[...]], axis=0)
        p, p_sink = _attn_probs(i, qv, kb, s_ref[...])
        pb = p.astype(BF16)
        o = jnp.dot(pb, vb, preferred_element_type=F32)
        delta = jnp.sum(dov.astype(F32) * o, axis=-1, keepdims=True)
        dp = lax.dot_general(dov, vb, _NT, preferred_element_type=F32)
        dsb = (p * (dp - delta)).astype(BF16)
        dq_ref[...] = jnp.dot(dsb, kb, preferred_element_type=F32).astype(BF16)
        dk_ref[...] = lax.dot_general(dsb, qv, _TN, preferred_element_type=F32)
        dv_ref[...] = lax.dot_general(pb, dov, _TN, preferred_element_type=F32)

        @pl.when(i == 0)
        def _():
            ds_ref[...] = jnp.zeros_like(ds_ref)

        ds_ref[...] -= p_sink * delta

    qs, cur, prev, sink = _attn_specs(grp)
    band = pl.BlockSpec((None, None, 2 * BLOCK, HEAD_DIM), lambda h, i: (h, i, 0, 0))
    band_shape = jax.ShapeDtypeStruct((N_KV_HEADS, nb, 2 * BLOCK, HEAD_DIM), F32)
    return pl.pallas_call(
        body, name="attn_bwd", grid=(N_KV_HEADS, nb),
        out_shape=(jax.ShapeDtypeStruct(q.shape, BF16), band_shape, band_shape,
                   jax.ShapeDtypeStruct(sink_rows.shape, F32)),
        in_specs=[qs, cur, prev, cur, prev, sink, qs], out_specs=(qs, band, band, sink),
        compiler_params=_params(("parallel", "arbitrary")),
    )(q, k, k, v, v, sink_rows, do)


def _qkv_merge_bwd(dq, dkb, dvb, cos, sin):
    grp = N_Q_HEADS // N_KV_HEADS
    nb = dkb.shape[1]
    width = (N_Q_HEADS + 2 * N_KV_HEADS) * HEAD_DIM
    scale = HEAD_DIM ** -0.5

    def body(dq_ref, kc_ref, kn_ref, vc_ref, vn_ref, c_ref, s_ref, o_ref):
        last = pl.program_id(0) == nb - 1
        c, s = c_ref[...], s_ref[...]
        for h in range(N_KV_HEADS):
            for g in range(grp):
                col = (h * grp + g) * HEAD_DIM
                y = dq_ref[h, g * BLOCK:(g + 1) * BLOCK, :].astype(F32) * scale
                o_ref[:, col:col + HEAD_DIM] = _rope_t(y, c, s).astype(BF16)
            dk = kc_ref[h, BLOCK:, :] + jnp.where(last, 0.0, kn_ref[h, :BLOCK, :])
            col = (N_Q_HEADS + h) * HEAD_DIM
            o_ref[:, col:col + HEAD_DIM] = _rope_t(dk, c, s).astype(BF16)
            dv = vc_ref[h, BLOCK:, :] + jnp.where(last, 0.0, vn_ref[h, :BLOCK, :])
            col = (N_Q_HEADS + N_KV_HEADS + h) * HEAD_DIM
            o_ref[:, col:col + HEAD_DIM] = dv.astype(BF16)

    tab = pl.BlockSpec((BLOCK, HEAD_DIM // 2), lambda i: (i, 0))
    cur = pl.BlockSpec((N_KV_HEADS, None, 2 * BLOCK, HEAD_DIM), lambda i: (0, i, 0, 0))
    nxt = pl.BlockSpec((N_KV_HEADS, None, 2 * BLOCK, HEAD_DIM), lambda i: (0, jnp.minimum(i + 1, nb - 1), 0, 0))
    return pl.pallas_call(
        body, name="qkv_merge_bwd", grid=(nb,), out_shape=jax.ShapeDtypeStruct((nb * BLOCK, width), BF16),
        in_specs=[pl.BlockSpec((N_KV_HEADS, grp * BLOCK, HEAD_DIM), lambda i: (0, i, 0)),
                  cur, nxt, cur, nxt, tab, tab],
        out_specs=pl.BlockSpec((BLOCK, width), lambda i: (i, 0)),
        compiler_params=_params(("parallel",)),
    )(dq, dkb, dkb, dvb, dvb, cos, sin)


def _tiles2d(r, c):
    tc = _pick(c, 2048, 128) if c % 128 == 0 else c
    tr = _pick(r, max(8, (1 << 20) // tc // 8 * 8), 8) if r % 8 == 0 else r
    return tr, tc


def _cast_bf16(name, w):
    r, c = w.shape
    tr, tc = _tiles2d(r, c)
    if tr % 16:
        tr = r

    def body(w_ref, o_ref):
        o_ref[...] = w_ref[...].astype(BF16)

    tile = pl.BlockSpec((tr, tc), lambda i, j: (i, j))
    return pl.pallas_call(
        body, name=name, grid=(r // tr, c // tc), out_shape=jax.ShapeDtypeStruct((r, c), BF16),
        in_specs=[tile], out_specs=tile, compiler_params=_params(("parallel", "parallel")),
    )(w)


def _pair_sum(name, g, got, core):
    n, r, c = g.shape
    half = r // 2
    tr, tc = _tiles2d(half, c)
    nh = half // tr

    def body(core_ref, g_ref, got_ref, o_ref):
        o_ref[...] = g_ref[...] + got_ref[...]

    tile = pl.BlockSpec((None, tr, tc), lambda k, i, j, core_ref: (k, i, j))
    return pl.pallas_call(
        body, name=name, out_shape=jax.ShapeDtypeStruct((n, half, c), F32),
        grid_spec=pltpu.PrefetchScalarGridSpec(
            num_scalar_prefetch=1, grid=(n, nh, c // tc),
            in_specs=[pl.BlockSpec((None, tr, tc), lambda k, i, j, core_ref: (k, core_ref[0] * nh + i, j)), tile],
            out_specs=tile),
        compiler_params=_params(("parallel", "parallel", "parallel")),
    )(core, g, got)


def _chip_sum(name, parts):
    n, r, c = parts.shape
    tr, tc = _tiles2d(r, c)

    def body(p0, p1, p2, p3, o_ref):
        o_ref[...] = ((p0[...] + p1[...]) + p2[...]) + p3[...]

    def chip(k):
        return pl.BlockSpec((None, tr, tc), lambda i, j: (k, i, j))

    return pl.pallas_call(
        body, name=name, grid=(r // tr, c // tc), out_shape=jax.ShapeDtypeStruct((r, c), F32),
        in_specs=[chip(k) for k in range(n)], out_specs=pl.BlockSpec((tr, tc), lambda i, j: (i, j)),
        compiler_params=_params(("parallel", "parallel")),
    )(parts, parts, parts, parts)


def _dev_sum(gathered):
    def body(g_ref, o_ref):
        acc = g_ref[0]
        for k in range(1, N_DEV):
            acc = acc + g_ref[k]
        o_ref[...] = acc

    return pl.pallas_call(body, name="dev_sum", out_shape=jax.ShapeDtypeStruct(gathered.shape[1:], F32))(gathered)


def _adamw(name, w, g, m, v):
    r, c = w.shape
    tr, tc = _tiles2d(r, c)
    if r % 8 == 0:
        tr = _pick(r, max(8, (1 << 18) // tc // 8 * 8), 8)

    def body(w_ref, g_ref, m_ref, v_ref, d_ref, mo_ref, vo_ref):
        gv = g_ref[...]
        mn = ADAM_B1 * m_ref[...] + (1.0 - ADAM_B1) * gv
        vn = ADAM_B2 * v_ref[...] + (1.0 - ADAM_B2) * jnp.square(gv)
        m_hat = mn / (1.0 - ADAM_B1 ** ADAM_STEP)
        v_hat = vn / (1.0 - ADAM_B2 ** ADAM_STEP)
        d_ref[...] = -ADAM_LR * (m_hat / (jnp.sqrt(v_hat) + ADAM_EPS) + ADAM_WD * w_ref[...])
        mo_ref[...] = mn
        vo_ref[...] = vn

    tile = pl.BlockSpec((tr, tc), lambda i, j: (i, j))
    shape = jax.ShapeDtypeStruct((r, c), F32)
    return pl.pallas_call(
        body, name=name, grid=(r // tr, c // tc), out_shape=(shape, shape, shape),
        in_specs=[tile] * 4, out_specs=(tile,) * 3, compiler_params=_params(("parallel", "parallel")),
    )(w, g, m, v)


HBM_SPEC = pl.BlockSpec(memory_space=pltpu.HBM)


def _place():
    x, y, c = lax.axis_index("x"), lax.axis_index("y"), lax.axis_index("c")
    chips = [(1 - x, y), (x, 1 - y), (1 - x, 1 - y)]
    return x, y, c, 2 * x + y, chips


def _all_gather(shards):
    n = len(shards)
    halves = [s.shape[0] // 2 for s in shards]

    def body(*refs):
        src, dst = refs[:n], refs[n:2 * n]
        local_sems, send_a, recv_a, send_b, recv_b = refs[2 * n:]
        x, y, c, me, chips = _place()
        sibling = (x, y, 1 - c)

        def half(w, chip, hc):
            return dst[w].at[chip, pl.ds(hc * halves[w], halves[w]), :]

        mine = [pltpu.make_async_copy(src[w], dst[w].at[me], local_sems.at[w]) for w in range(n)]
        for cp in mine:
            cp.start()
        started = []
        for j, (cx, cy) in enumerate(chips):
            for w in range(n):
                cp = pltpu.make_async_remote_copy(
                    src_ref=src[w].at[pl.ds(c * halves[w], halves[w]), :], dst_ref=half(w, me, c),
                    send_sem=send_a.at[j * n + w], recv_sem=recv_a.at[j * n + w],
                    device_id=(cx, cy, c), device_id_type=MESH)
                cp.start()
                started.append(cp)
        for j, (cx, cy) in enumerate(chips):
            for w in range(n):
                got = half(w, 2 * cx + cy, c)
                pltpu.make_async_remote_copy(
                    src_ref=got, dst_ref=got, send_sem=send_a.at[j * n + w], recv_sem=recv_a.at[j * n + w],
                    device_id=(cx, cy, c), device_id_type=MESH).wait_recv()
                cp = pltpu.make_async_remote_copy(
                    src_ref=got, dst_ref=got, send_sem=send_b.at[j * n + w], recv_sem=recv_b.at[j * n + w],
                    device_id=sibling, device_id_type=MESH)
                cp.start()
                started.append(cp)
        for j, (cx, cy) in enumerate(chips):
            for w in range(n):
                got = half(w, 2 * cx + cy, 1 - c)
                pltpu.make_async_remote_copy(
                    src_ref=got, dst_ref=got, send_sem=send_b.at[j * n + w], recv_sem=recv_b.at[j * n + w],
                    device_id=sibling, device_id_type=MESH).wait_recv()
        for cp in started:
            cp.wait_send()
        for cp in mine:
            cp.wait()

    return pl.pallas_call(
        body, name="all_gather_weights",
        out_shape=[jax.ShapeDtypeStruct((N_CHIPS,) + s.shape, s.dtype) for s in shards],
        in_specs=[HBM_SPEC] * n, out_specs=[HBM_SPEC] * n,
        scratch_shapes=[pltpu.SemaphoreType.DMA((n,))] + [pltpu.SemaphoreType.DMA((3 * n,))] * 4,
    )(*shards)


def _pair_exchange(grads, small):
    n = len(grads)
    halves = [g.shape[1] // 2 for g in grads]

    def body(*refs):
        src, small_ref = refs[:n], refs[n]
        dst, all_ref = refs[n + 1:2 * n + 1], refs[2 * n + 1]
        send, recv, ssend, srecv, local_sem = refs[2 * n + 2:]
        x, y, c, me, _ = _place()
        sibling = (x, y, 1 - c)
        dev = 2 * me + c
        mine = pltpu.make_async_copy(small_ref, all_ref.at[dev], local_sem)
        mine.start()
        started = []
        for w in range(n):
            cp = pltpu.make_async_remote_copy(
                src_ref=src[w].at[:, pl.ds((1 - c) * halves[w], halves[w]), :], dst_ref=dst[w],
                send_sem=send.at[w], recv_sem=recv.at[w], device_id=sibling, device_id_type=MESH)
            cp.start()
            started.append(cp)
        for t in range(1, N_DEV):
            tx, ty, tc = (t >> 2) & 1, (t >> 1) & 1, t & 1
            cp = pltpu.make_async_remote_copy(
                src_ref=small_ref, dst_ref=all_ref.at[dev], send_sem=ssend.at[t], recv_sem=srecv.at[t],
                device_id=(x ^ tx, y ^ ty, c ^ tc), device_id_type=MESH)
            cp.start()
            started.append(cp)
        for w in range(n):
            pltpu.make_async_remote_copy(
                src_ref=dst[w], dst_ref=dst[w], send_sem=send.at[w], recv_sem=recv.at[w],
                device_id=sibling, device_id_type=MESH).wait_recv()
        for t in range(1, N_DEV):
            tx, ty, tc = (t >> 2) & 1, (t >> 1) & 1, t & 1
            peer = 4 * (x ^ tx) + 2 * (y ^ ty) + (c ^ tc)
            pltpu.make_async_remote_copy(
                src_ref=all_ref.at[peer], dst_ref=all_ref.at[peer], send_sem=ssend.at[t], recv_sem=srecv.at[t],
                device_id=(x ^ tx, y ^ ty, c ^ tc), device_id_type=MESH).wait_recv()
        for cp in started:
            cp.wait_send()
        mine.wait()

    out_shape = [jax.ShapeDtypeStruct((N_CHIPS, h, g.shape[2]), F32) for g, h in zip(grads, halves)]
    out_shape.append(jax.ShapeDtypeStruct((N_DEV,) + small.shape, F32))
    return pl.pallas_call(
        body, name="pair_exchange", out_shape=out_shape,
        in_specs=[HBM_SPEC] * (n + 1), out_specs=[HBM_SPEC] * (n + 1),
        scratch_shapes=[pltpu.SemaphoreType.DMA((n,)), pltpu.SemaphoreType.DMA((n,)),
                        pltpu.SemaphoreType.DMA((N_DEV,)), pltpu.SemaphoreType.DMA((N_DEV,)),
                        pltpu.SemaphoreType.DMA],
    )(*grads, small)


def _chip_exchange(partials):
    n = len(partials)

    def body(*refs):
        src, dst = refs[:n], refs[n:2 * n]
        local_sems, send, recv = refs[2 * n:]
        x, y, c, me, chips = _place()
        mine = [pltpu.make_async_copy(src[w].at[me], dst[w].at[me], local_sems.at[w]) for w in range(n)]
        for cp in mine:
            cp.start()
        started = []
        for j, (cx, cy) in enumerate(chips):
            for w in range(n):
                cp = pltpu.make_async_remote_copy(
                    src_ref=src[w].at[2 * cx + cy], dst_ref=dst[w].at[me],
                    send_sem=send.at[j * n + w], recv_sem=recv.at[j * n + w],
                    device_id=(cx, cy, c), device_id_type=MESH)
                cp.start()
                started.append(cp)
        for j, (cx, cy) in enumerate(chips):
            for w in range(n):
                got = dst[w].at[2 * cx + cy]
                pltpu.make_async_remote_copy(
                    src_ref=got, dst_ref=got, send_sem=send.at[j * n + w], recv_sem=recv.at[j * n + w],
                    device_id=(cx, cy, c), device_id_type=MESH).wait_recv()
        for cp in started:
            cp.wait_send()
        for cp in mine:
            cp.wait()

    return pl.pallas_call(
        body, name="chip_exchange", out_shape=[jax.ShapeDtypeStruct(p.shape, F32) for p in partials],
        in_specs=[HBM_SPEC] * n, out_specs=[HBM_SPEC] * n,
        scratch_shapes=[pltpu.SemaphoreType.DMA((n,))] + [pltpu.SemaphoreType.DMA((3 * n,))] * 2,
    )(*partials)


def _pair_share(sums):
    n = len(sums)

    def body(*refs):
        src, dst = refs[:n], refs[n:2 * n]
        local_sems, send, recv = refs[2 * n:]
        x, y, c, _, _ = _place()
        sibling = (x, y, 1 - c)

        def half(w, hc):
            h = sums[w].shape[0]
            return dst[w].at[pl.ds(hc * h, h), :]

        mine = [pltpu.make_async_copy(src[w], half(w, c), local_sems.at[w]) for w in range(n)]
        sent = [pltpu.make_async_remote_copy(src_ref=src[w], dst_ref=half(w, c), send_sem=send.at[w],
                                             recv_sem=recv.at[w], device_id=sibling, device_id_type=MESH)
                for w in range(n)]
        for cp in mine + sent:
            cp.start()
        for w in range(n):
            got = half(w, 1 - c)
            pltpu.make_async_remote_copy(src_ref=got, dst_ref=got, send_sem=send.at[w], recv_sem=recv.at[w],
                                         device_id=sibling, device_id_type=MESH).wait_recv()
        for cp in sent:
            cp.wait_send()
        for cp in mine:
            cp.wait()

    return pl.pallas_call(
        body, name="pair_share",
        out_shape=[jax.ShapeDtypeStruct((2 * s.shape[0], s.shape[1]), F32) for s in sums],
        in_specs=[HBM_SPEC] * n, out_specs=[HBM_SPEC] * n,
        scratch_shapes=[pltpu.SemaphoreType.DMA((n,))] * 3,
    )(*sums)


MATRICES = ("w_in_conv", "w_out_conv", "w_up_0", "w_down_0", "w_qkv", "w_o", "w_up_1", "w_down_1")
COLUMN_SHARDED = ("w_in_conv", "w_up_0", "w_qkv", "w_up_1")
NORMS = ("norm_mix_0", "norm_mlp_0", "norm_mix_1", "norm_mlp_1", "norm_final")


def _mlp_fwd(tag, h, g, w_up, w_down):
    n = _rms_fwd("norm_mlp_" + tag, h, g)
    z = _mm_nn_cols("mlp_up_" + tag, n, w_up, BF16)
    out = _mm_nn_rows("mlp_down_" + tag, z, w_down, F32, a_pro=_relu_sq,
                      epi=lambda acc, res: acc + res, extras=(h,))
    return out, (h, n, z)


def _mlp_bwd(tag, saved, g, w_up, w_down, dh, dh_bf):
    h, n, z = saved
    dz = _mm_nt_rows("mlp_down_dx_" + tag, dh_bf, w_down, BF16,
                     epi=lambda acc, zt: acc * (2.0 * jnp.maximum(zt.astype(F32), 0.0)), extras=(z,))
    dw_down = _mm_tn("mlp_down_dw_" + tag, z, dh_bf, stacked=False, a_pro=_relu_sq)
    dw_up = _mm_tn("mlp_up_dw_" + tag, n, dz, stacked=True)
    dn = _mm_nt_cols("mlp_up_dx_" + tag, dz, w_up, F32)
    dh, dh_bf, dg = _rms_bwd("norm_mlp_bwd_" + tag, dn, h, g, dh)
    return dh, dh_bf, dg, dw_up, dw_down


def _local_step(h0, target, w, norms, sinks, conv_w8):
    m = h0.shape[0]
    grp = N_Q_HEADS // N_KV_HEADS
    cos, sin = _rope_tables(m)
    sink_rows = jnp.repeat(sinks.astype(F32), BLOCK).reshape(N_KV_HEADS, grp * BLOCK, 1)

    n0 = _rms_fwd("norm_mix_0", h0, norms["norm_mix_0"])
    bcu = _mm_nn_cols("conv_in", n0, w["w_in_conv"], BF16)
    gate = _conv_fwd(bcu, conv_w8)
    h1 = _mm_nn_rows("conv_out", gate, w["w_out_conv"], F32, epi=lambda acc, res: acc + res, extras=(h0,))
    h2, mlp0 = _mlp_fwd("0", h1, norms["norm_mlp_0"], w["w_up_0"], w["w_down_0"])
    n2 = _rms_fwd("norm_mix_1", h2, norms["norm_mix_1"])
    qkv = _mm_nn_cols("attn_qkv", n2, w["w_qkv"], BF16)
    q, k, v = _qkv_split(qkv, cos, sin)
    o = _heads_merge("attn_o_merge", _attn_fwd(q, k, v, sink_rows))
    h3 = _mm_nn_rows("attn_out", o, w["w_o"], F32, epi=lambda acc, res: acc + res, extras=(h2,))
    h4, mlp1 = _mlp_fwd("1", h3, norms["norm_mlp_1"], w["w_up_1"], w["w_down_1"])

    grads = {}
    loss, dh, dh_bf, grads["norm_final"] = _loss_head(h4, norms["norm_final"], target)
    dh, dh_bf, grads["norm_mlp_1"], grads["w_up_1"], grads["w_down_1"] = _mlp_bwd(
        "1", mlp1, norms["norm_mlp_1"], w["w_up_1"], w["w_down_1"], dh, dh_bf)
    do = _mm_nt_rows("attn_out_dx", dh_bf, w["w_o"], BF16)
    grads["w_o"] = _mm_tn("attn_out_dw", o, dh_bf, stacked=False)
    dq, dkb, dvb, dsink = _attn_bwd(q, k, v, sink_rows, _heads_split("attn_do_split", do))
    grads["attn_sinks"] = dsink
    dqkv = _qkv_merge_bwd(dq, dkb, dvb, cos, sin)
    grads["w_qkv"] = _mm_tn("attn_qkv_dw", n2, dqkv, stacked=True)
    dn = _mm_nt_cols("attn_qkv_dx", dqkv, w["w_qkv"], F32)
    dh, dh_bf, grads["norm_mix_1"] = _rms_bwd("norm_mix_bwd_1", dn, h2, norms["norm_mix_1"], dh)
    dh, dh_bf, grads["norm_mlp_0"], grads["w_up_0"], grads["w_down_0"] = _mlp_bwd(
        "0", mlp0, norms["norm_mlp_0"], w["w_up_0"], w["w_down_0"], dh, dh_bf)
    dgate = _mm_nt_rows("conv_out_dx", dh_bf, w["w_out_conv"], BF16)
    grads["w_out_conv"] = _mm_tn("conv_out_dw", gate, dh_bf, stacked=False)
    dbcu, grads["conv_w"] = _conv_bwd(bcu, conv_w8, dgate)
    grads["w_in_conv"] = _mm_tn("conv_in_dw", n0, dbcu, stacked=True)
    dn = _mm_nt_cols("conv_in_dx", dbcu, w["w_in_conv"], F32)
    dh, _, grads["norm_mix_0"] = _rms_bwd("norm_mix_bwd_0", dn, h0, norms["norm_mix_0"], dh)
    return loss, dh, grads


def kernel(x, meta_tokens, norm_mix_0, w_in_conv, conv_w, w_out_conv, norm_mlp_0, w_up_0, w_down_0, norm_mix_1, w_qkv, attn_sinks, w_o, norm_mlp_1, w_up_1, w_down_1, norm_final, loss_target, m_meta_tokens, m_norm_mix_0, m_w_in_conv, m_conv_w, m_w_out_conv, m_norm_mlp_0, m_w_up_0, m_w_down_0, m_norm_mix_1, m_w_qkv, m_attn_sinks, m_w_o, m_norm_mlp_1, m_w_up_1, m_w_down_1, m_norm_final, v_meta_tokens, v_norm_mix_0, v_w_in_conv, v_conv_w, v_w_out_conv, v_norm_mlp_0, v_w_up_0, v_w_down_0, v_norm_mix_1, v_w_qkv, v_attn_sinks, v_w_o, v_norm_mlp_1, v_w_up_1, v_w_down_1, v_norm_final):
    given = dict(locals())
    names = ("meta_tokens", "norm_mix_0", "w_in_conv", "conv_w", "w_out_conv", "norm_mlp_0", "w_up_0", "w_down_0",
             "norm_mix_1", "w_qkv", "attn_sinks", "w_o", "norm_mlp_1", "w_up_1", "w_down_1", "norm_final")
    d = D_MODEL
    dc = d // N_CHIPS
    core = lax.axis_index("c").astype(jnp.int32).reshape(1)

    small = jnp.zeros((SMALL_ROWS, dc), F32).at[:N_META].set(meta_tokens).at[N_META:N_META + CONV_WIDTH].set(conv_w)
    shards = [_cast_bf16("cast_" + n, given[n]) for n in MATRICES] + [small]
    gathered = _all_gather(shards)
    w = {}
    for n, g in zip(MATRICES, gathered):
        w[n] = g if n in COLUMN_SHARDED else g.reshape(N_CHIPS * g.shape[1], g.shape[2])
    small_all = gathered[-1]
    small_full = jnp.transpose(small_all, (1, 0, 2)).reshape(SMALL_ROWS, d)
    meta_full = small_full[:N_META]
    conv_w8 = small_full[N_META:N_META + 8]

    pad = ROW0 - N_META
    h0 = jnp.concatenate([jnp.zeros((pad, d), F32), meta_full, x[0]], axis=0)
    norms = {n: given[n] for n in NORMS}
    loss_part, dh0, grads = _local_step(h0, loss_target[0], w, norms, attn_sinks, conv_w8)
    loss = lax.psum(loss_part[0, 0], ("x", "y", "c"))
    grad_x = dh0[ROW0:][None]

    grp = N_Q_HEADS // N_KV_HEADS
    small_grad = jnp.zeros((SMALL_ROWS, d), F32).at[:N_META].set(dh0[pad:ROW0]).at[N_META:N_META + 8].set(grads["conv_w"])
    small_grad = jnp.transpose(small_grad.reshape(SMALL_ROWS, N_CHIPS, dc), (1, 0, 2))
    full = [grads[n] if n in COLUMN_SHARDED else grads[n].reshape(N_CHIPS, -1, grads[n].shape[1]) for n in MATRICES]
    full.append(small_grad)
    dsink = jnp.sum(grads["attn_sinks"].reshape(N_Q_HEADS, BLOCK), axis=1)
    rep = jnp.zeros((8, d), F32)
    for r, n in enumerate(NORMS):
        rep = rep.at[r].set(jnp.sum(grads[n], axis=0))
    rep = rep.at[len(NORMS), :N_Q_HEADS].set(dsink)

    ex = _pair_exchange(full, rep)
    got, rep_all = ex[:-1], ex[-1]
    partials = [_pair_sum("pair_sum_%d" % i, g, o, core) for i, (g, o) in enumerate(zip(full, got))]
    landed = _chip_exchange(partials)
    sums = [_chip_sum("chip_sum_%d" % i, p) for i, p in enumerate(landed)]
    reduced = _pair_share(sums)
    rep_sum = _dev_sum(rep_all)

    g_out = dict(zip(MATRICES, reduced[:-1]))
    g_out["meta_tokens"] = reduced[-1][:N_META]
    g_out["conv_w"] = reduced[-1][N_META:N_META + CONV_WIDTH]
    for r, n in enumerate(NORMS):
        g_out[n] = rep_sum[r]
    g_out["attn_sinks"] = rep_sum[len(NORMS), :N_Q_HEADS]

    delta, new_m, new_v = {}, {}, {}
    for n in names:
        wt = given[n]
        shape2 = wt.shape if wt.ndim == 2 else (1, wt.shape[0])
        outs = _adamw("adamw_" + n, wt.reshape(shape2), g_out[n].reshape(shape2),
                      given["m_" + n].reshape(shape2), given["v_" + n].reshape(shape2))
        delta[n], new_m[n], new_v[n] = [o.reshape(wt.shape) for o in outs]
    return (loss, grad_x, *[g_out[n] for n in names], *[delta[n] for n in names],
            *[new_m[n] for n in names], *[new_v[n] for n in names])
```

```python
import functools

import jax
import jax.numpy as jnp
from jax import lax
from jax.experimental import pallas as pl
from jax.experimental.pallas import tpu as pltpu

F32 = jnp.float32
BF16 = jnp.bfloat16

D_MODEL = 2048
SEQ = 8192
N_META = 16
CONV_WIDTH = 3
HEAD_DIM = 64
N_Q_HEADS = 32
N_KV_HEADS = 4
BLOCK = 128
ROPE_THETA = 10000.0
D_FF = 4 * D_MODEL
RMS_EPS = 1e-5
NEG_INF = -1e30

ADAM_LR = 0.001
ADAM_B1 = 0.9
ADAM_B2 = 0.999
ADAM_EPS = 1e-08
ADAM_WD = 0.01
ADAM_STEP = 10

N_CHIPS = 4
N_DEV = 8
MESH = pl.DeviceIdType.MESH
VMEM_LIMIT = 56 * 1024 * 1024
SMALL_ROWS = 32
ROW0 = BLOCK


def _pick(n, target, mult):
    best = None
    for t in range(mult, min(n, target) + 1, mult):
        if n % t == 0:
            best = t
    assert best is not None, (n, target, mult)
    return best


def _params(sem=None):
    return pltpu.CompilerParams(dimension_semantics=sem, vmem_limit_bytes=VMEM_LIMIT)


def _mm(name, a, b, *, dims, grid, a_spec, b_spec, out_shape, out_spec, nk, acc_shape,
        extras=(), extra_specs=(), a_pro=None, epi=None):
    n_ex = len(extras)
    acc_in_out = epi is None and out_shape.dtype == F32

    def body(*refs):
        a_ref, b_ref = refs[0], refs[1]
        ex = refs[2:2 + n_ex]
        o_ref = refs[2 + n_ex]
        av = a_ref[...]
        if a_pro is not None:
            av = a_pro(av)
        part = lax.dot_general(av, b_ref[...], dims, preferred_element_type=F32)

        def finish(acc):
            r = acc if epi is None else epi(acc, *[e[...] for e in ex])
            o_ref[...] = r.astype(o_ref.dtype)

        if nk == 1:
            finish(part)
            return
        acc_ref = o_ref if acc_in_out else refs[3 + n_ex]
        kk = pl.program_id(len(grid) - 1)

        @pl.when(kk == 0)
        def _():
            acc_ref[...] = part

        @pl.when(kk > 0)
        def _():
            acc_ref[...] += part

        if not acc_in_out:
            @pl.when(kk == nk - 1)
            def _():
                finish(acc_ref[...])

    scratch = [] if (nk == 1 or acc_in_out) else [pltpu.VMEM(acc_shape, F32)]
    return pl.pallas_call(
        body, name=name, grid=grid, out_shape=out_shape,
        in_specs=[a_spec, b_spec, *extra_specs], out_specs=out_spec, scratch_shapes=scratch,
        compiler_params=_params(("parallel", "parallel", "arbitrary")),
    )(a, b, *extras)


_NN = (((1,), (0,)), ((), ()))
_NT = (((1,), (1,)), ((), ()))
_TN = (((0,), (0,)), ((), ()))


def _mm_nn_cols(name, a, w, out_dtype, epi=None, extras=()):
    m, k = a.shape
    _, _, ns = w.shape
    tm, tn = _pick(m, 832, 16), _pick(ns, 1024, 128)
    per = ns // tn
    tile = pl.BlockSpec((tm, tn), lambda j, i, kk: (i, j))
    return _mm(name, a, w, dims=_NN, grid=(N_CHIPS * per, m // tm, 1),
               a_spec=pl.BlockSpec((tm, k), lambda j, i, kk: (i, 0)),
               b_spec=pl.BlockSpec((None, k, tn), lambda j, i, kk: (j // per, 0, j % per)),
               out_shape=jax.ShapeDtypeStruct((m, N_CHIPS * ns), out_dtype), out_spec=tile,
               nk=1, acc_shape=(tm, tn), extras=extras, extra_specs=[tile] * len(extras), epi=epi)


def _mm_nn_rows(name, a, w, out_dtype, a_pro=None, epi=None, extras=()):
    m, k = a.shape
    _, n = w.shape
    tm, tn, tk = _pick(m, 832, 16), _pick(n, 1024, 128), _pick(k, 2048, 128)
    tile = pl.BlockSpec((tm, tn), lambda j, i, kk: (i, j))
    return _mm(name, a, w, dims=_NN, grid=(n // tn, m // tm, k // tk),
               a_spec=pl.BlockSpec((tm, tk), lambda j, i, kk: (i, kk)),
               b_spec=pl.BlockSpec((tk, tn), lambda j, i, kk: (kk, j)),
               out_shape=jax.ShapeDtypeStruct((m, n), out_dtype), out_spec=tile,
               nk=k // tk, acc_shape=(tm, tn), extras=extras, extra_specs=[tile] * len(extras),
               a_pro=a_pro, epi=epi)


def _mm_nt_rows(name, a, w, out_dtype, epi=None, extras=()):
    m, c = a.shape
    r, _ = w.shape
    tm, tn = _pick(m, 832, 16), _pick(r, 1024, 128)
    tile = pl.BlockSpec((tm, tn), lambda j, i, kk: (i, j))
    return _mm(name, a, w, dims=_NT, grid=(r // tn, m // tm, 1),
               a_spec=pl.BlockSpec((tm, c), lambda j, i, kk: (i, 0)),
               b_spec=pl.BlockSpec((tn, c), lambda j, i, kk: (j, 0)),
               out_shape=jax.ShapeDtypeStruct((m, r), out_dtype), out_spec=tile,
               nk=1, acc_shape=(tm, tn), extras=extras, extra_specs=[tile] * len(extras), epi=epi)


def _mm_nt_cols(name, a, w, out_dtype):
    m, _ = a.shape
    _, r, ns = w.shape
    tm, tn, tk = _pick(m, 832, 16), _pick(r, 1024, 128), _pick(ns, 1024, 128)
    per = ns // tk
    return _mm(name, a, w, dims=_NT, grid=(r // tn, m // tm, N_CHIPS * per),
               a_spec=pl.BlockSpec((tm, tk), lambda j, i, kk: (i, kk)),
               b_spec=pl.BlockSpec((None, tn, tk), lambda j, i, kk: (kk // per, j, kk % per)),
               out_shape=jax.ShapeDtypeStruct((m, r), out_dtype),
               out_spec=pl.BlockSpec((tm, tn), lambda j, i, kk: (i, j)),
               nk=N_CHIPS * per, acc_shape=(tm, tn))


def _mm_tn(name, a, b, stacked, a_pro=None):
    t, ka = a.shape
    _, nb = b.shape
    ns = nb // N_CHIPS if stacked else nb
    tt, ta, tb = _pick(t, 640, 128), _pick(ka, 2048, 128), _pick(ns, 1024, 128)
    if stacked:
        per = ns // tb
        out_shape = jax.ShapeDtypeStruct((N_CHIPS, ka, ns), F32)
        out_spec = pl.BlockSpec((None, ta, tb), lambda i, j, kk: (j // per, i, j % per))
    else:
        out_shape = jax.ShapeDtypeStruct((ka, nb), F32)
        out_spec = pl.BlockSpec((ta, tb), lambda i, j, kk: (i, j))
    return _mm(name, a, b, dims=_TN, grid=(ka // ta, nb // tb, t // tt),
               a_spec=pl.BlockSpec((tt, ta), lambda i, j, kk: (kk, i)),
               b_spec=pl.BlockSpec((tt, tb), lambda i, j, kk: (kk, j)),
               out_shape=out_shape, out_spec=out_spec, nk=t // tt, acc_shape=(ta, tb), a_pro=a_pro)


def _relu_sq(z):
    a = jnp.maximum(z, 0)
    return a * a


def _rms_fwd(name, h, g):
    m, d = h.shape
    tr = _pick(m, 256, 16)

    def body(h_ref, g_ref, o_ref):
        x = h_ref[...]
        rstd = lax.rsqrt(jnp.mean(x * x, axis=-1, keepdims=True) + RMS_EPS)
        o_ref[...] = ((x * rstd) * g_ref[...]).astype(BF16)

    row = pl.BlockSpec((tr, d), lambda i: (i, 0))
    return pl.pallas_call(
        body, name=name, grid=(m // tr,), out_shape=jax.ShapeDtypeStruct((m, d), BF16),
        in_specs=[row, pl.BlockSpec((1, d), lambda i: (0, 0))], out_specs=row,
        compiler_params=_params(("parallel",)),
    )(h, g.reshape(1, d))


def _rms_bwd_math(x, g, dn):
    rstd = lax.rsqrt(jnp.mean(x * x, axis=-1, keepdims=True) + RMS_EPS)
    xhat = x * rstd
    dxhat = dn * g
    dx = rstd * (dxhat - xhat * jnp.mean(dxhat * xhat, axis=-1, keepdims=True))
    return dx, dn * xhat


def _fold8(v):
    r, c = v.shape
    return jnp.sum(v.reshape(r // 8, 8, c), axis=0)


def _rms_bwd(name, dn, h, g, dh_in):
    m, d = h.shape
    tr = _pick(m, 256, 16)
    nt = m // tr

    def body(dn_ref, h_ref, g_ref, dh_ref, o_ref, ob_ref, dg_ref):
        dx, dgp = _rms_bwd_math(h_ref[...], g_ref[...], dn_ref[...])
        dh = dh_ref[...] + dx
        o_ref[...] = dh
        ob_ref[...] = dh.astype(BF16)

        @pl.when(pl.program_id(0) == 0)
        def _():
            dg_ref[...] = jnp.zeros_like(dg_ref)

        dg_ref[...] += _fold8(dgp)

    row = pl.BlockSpec((tr, d), lambda i: (i, 0))
    return pl.pallas_call(
        body, name=name, grid=(nt,),
        out_shape=(jax.ShapeDtypeStruct((m, d), F32), jax.ShapeDtypeStruct((m, d), BF16),
                   jax.ShapeDtypeStruct((8, d), F32)),
        in_specs=[row, row, pl.BlockSpec((1, d), lambda i: (0, 0)), row],
        out_specs=(row, row, pl.BlockSpec((8, d), lambda i: (0, 0))),
        compiler_params=_params(("arbitrary",)),
    )(dn, h, g.reshape(1, d), dh_in)


def _loss_head(h, g, target):
    m, d = h.shape
    tr = BLOCK

    def body(h_ref, g_ref, t_ref, loss_ref, o_ref, ob_ref, dg_ref):
        i = pl.program_id(0)
        x = h_ref[...]
        gv = g_ref[...]
        rstd = lax.rsqrt(jnp.mean(x * x, axis=-1, keepdims=True) + RMS_EPS)
        err = jnp.where(i > 0, (x * rstd) * gv - t_ref[...], 0.0)
        dx, dgp = _rms_bwd_math(x, gv, err * (1.0 / d))
        o_ref[...] = dx
        ob_ref[...] = dx.astype(BF16)

        @pl.when(i == 0)
        def _():
            dg_ref[...] = jnp.zeros_like(dg_ref)
            loss_ref[...] = jnp.zeros_like(loss_ref)

        dg_ref[...] += _fold8(dgp)
        sq = jnp.mean(err * err, axis=-1, keepdims=True)
        loss_ref[...] += 0.5 * jnp.sum(sq, axis=0, keepdims=True)

    row = pl.BlockSpec((tr, d), lambda i: (i, 0))
    return pl.pallas_call(
        body, name="loss_head", grid=(m // tr,),
        out_shape=(jax.ShapeDtypeStruct((8, 128), F32), jax.ShapeDtypeStruct((m, d), F32),
                   jax.ShapeDtypeStruct((m, d), BF16), jax.ShapeDtypeStruct((8, d), F32)),
        in_specs=[row, pl.BlockSpec((1, d), lambda i: (0, 0)),
                  pl.BlockSpec((tr, d), lambda i: (jnp.maximum(i - 1, 0), 0))],
        out_specs=(pl.BlockSpec((8, 128), lambda i: (0, 0)), row, row,
                   pl.BlockSpec((8, d), lambda i: (0, 0))),
        compiler_params=_params(("arbitrary",)),
    )(h, g.reshape(1, d), target)


HALO = 16


def _shift_down(cat, k):
    return pltpu.roll(cat, k, axis=0)[HALO:]


def _shift_up(cat, k):
    n = cat.shape[0]
    return pltpu.roll(cat, n - k, axis=0)[:n - HALO]


def _conv_fwd(bcu, cw):
    m, d3 = bcu.shape
    d = d3 // 3
    tr, tc = _pick(m, 416, 16), _pick(d, 512, 128)
    nd, hb = d // tc, tr // HALO

    def body(b_ref, c_ref, u_ref, ch_ref, uh_ref, w_ref, o_ref):
        i = pl.program_id(0)
        v = c_ref[...].astype(F32) * u_ref[...].astype(F32)
        vh = jnp.where(i > 0, ch_ref[...].astype(F32) * uh_ref[...].astype(F32), 0.0)
        cat = jnp.concatenate([vh, v], axis=0)
        w = w_ref[...]
        conv = w[2:3] * v + w[1:2] * _shift_down(cat, 1) + w[0:1] * _shift_down(cat, 2)
        o_ref[...] = (b_ref[...].astype(F32) * conv).astype(BF16)

    def part(p):
        return pl.BlockSpec((tr, tc), lambda i, j: (i, p * nd + j))

    def halo(p):
        return pl.BlockSpec((HALO, tc), lambda i, j: (jnp.maximum(i * hb - 1, 0), p * nd + j))

    return pl.pallas_call(
        body, name="conv_fwd", grid=(m // tr, nd), out_shape=jax.ShapeDtypeStruct((m, d), BF16),
        in_specs=[part(0), part(1), part(2), halo(1), halo(2), pl.BlockSpec((8, tc), lambda i, j: (0, j))],
        out_specs=pl.BlockSpec((tr, tc), lambda i, j: (i, j)),
        compiler_params=_params(("parallel", "parallel")),
    )(bcu, bcu, bcu, bcu, bcu, cw)


def _conv_bwd(bcu, cw, dg):
    m, d3 = bcu.shape
    d = d3 // 3
    tr, tc = _pick(m, 416, 16), _pick(d, 512, 128)
    nd, hb, nt = d // tc, tr // HALO, m // tr

    def body(b_ref, c_ref, u_ref, ch_ref, uh_ref, bn_ref, dg_ref, dgn_ref, w_ref, o_ref, dw_ref):
        i, p = pl.program_id(1), pl.program_id(2)
        w = w_ref[...]
        b, c, u = b_ref[...].astype(F32), c_ref[...].astype(F32), u_ref[...].astype(F32)
        dgv = dg_ref[...].astype(F32)
        v = c * u
        vh = jnp.where(i > 0, ch_ref[...].astype(F32) * uh_ref[...].astype(F32), 0.0)
        cat = jnp.concatenate([vh, v], axis=0)
        v1, v2 = _shift_down(cat, 1), _shift_down(cat, 2)
        dconv = dgv * b

        @pl.when(p == 0)
        def _():
            o_ref[...] = (dgv * (w[2:3] * v + w[1:2] * v1 + w[0:1] * v2)).astype(BF16)

            @pl.when(i == 0)
            def _():
                dw_ref[...] = jnp.zeros_like(dw_ref)

            taps = [jnp.sum(dconv * t, axis=0, keepdims=True) for t in (v2, v1, v)]
            dw_ref[...] += jnp.concatenate(taps + [jnp.zeros((5, tc), F32)], axis=0)

        @pl.when(p > 0)
        def _():
            nxt = jnp.where(i < nt - 1, dgn_ref[...].astype(F32) * bn_ref[...].astype(F32), 0.0)
            cat2 = jnp.concatenate([dconv, nxt], axis=0)
            dv = w[2:3] * dconv + w[1:2] * _shift_up(cat2, 1) + w[0:1] * _shift_up(cat2, 2)
            o_ref[...] = jnp.where(p == 1, dv * u, dv * c).astype(BF16)

    def part(q):
        return pl.BlockSpec((tr, tc), lambda j, i, p: (i, q * nd + j))

    def before(q):
        return pl.BlockSpec((HALO, tc), lambda j, i, p: (jnp.maximum(i * hb - 1, 0), q * nd + j))

    def after(q):
        return pl.BlockSpec((HALO, tc), lambda j, i, p: (jnp.minimum((i + 1) * hb, m // HALO - 1), q * nd + j))

    return pl.pallas_call(
        body, name="conv_bwd", grid=(nd, nt, 3),
        out_shape=(jax.ShapeDtypeStruct((m, d3), BF16), jax.ShapeDtypeStruct((8, d), F32)),
        in_specs=[part(0), part(1), part(2), before(1), before(2), after(0),
                  pl.BlockSpec((tr, tc), lambda j, i, p: (i, j)), after(0),
                  pl.BlockSpec((8, tc), lambda j, i, p: (0, j))],
        out_specs=(pl.BlockSpec((tr, tc), lambda j, i, p: (i, p * nd + j)),
                   pl.BlockSpec((8, tc), lambda j, i, p: (0, j))),
        compiler_params=_params(("parallel", "arbitrary", "arbitrary")),
    )(bcu, bcu, bcu, bcu, bcu, bcu, dg, dg, cw)


def _rope_tables(m):
    pad = ROW0 - N_META
    pos = jnp.arange(m, dtype=F32) - pad
    inv = ROPE_THETA ** (-jnp.arange(0, HEAD_DIM, 2, dtype=F32) / HEAD_DIM)
    ang = pos[:, None] * inv[None, :]
    return jnp.cos(ang), jnp.sin(ang)


def _rope(x, c, s):
    half = HEAD_DIM // 2
    x1, x2 = x[:, :half], x[:, half:]
    return jnp.concatenate([x1 * c - x2 * s, x2 * c + x1 * s], axis=-1)


def _rope_t(y, c, s):
    half = HEAD_DIM // 2
    y1, y2 = y[:, :half], y[:, half:]
    return jnp.concatenate([y1 * c + y2 * s, y2 * c - y1 * s], axis=-1)


def _qkv_split(qkv, cos, sin):
    m = qkv.shape[0]
    nb, grp = m // BLOCK, N_Q_HEADS // N_KV_HEADS
    scale = HEAD_DIM ** -0.5

    def body(x_ref, c_ref, s_ref, q_ref, k_ref, v_ref):
        c, s = c_ref[...], s_ref[...]
        for h in range(N_KV_HEADS):
            for g in range(grp):
                col = (h * grp + g) * HEAD_DIM
                xq = x_ref[:, col:col + HEAD_DIM].astype(F32)
                q_ref[h, g * BLOCK:(g + 1) * BLOCK, :] = (_rope(xq, c, s) * scale).astype(BF16)
            col = (N_Q_HEADS + h) * HEAD_DIM
            k_ref[h] = _rope(x_ref[:, col:col + HEAD_DIM].astype(F32), c, s).astype(BF16)
            col = (N_Q_HEADS + N_KV_HEADS + h) * HEAD_DIM
            v_ref[h] = x_ref[:, col:col + HEAD_DIM]

    tab = pl.BlockSpec((BLOCK, HEAD_DIM // 2), lambda i: (i, 0))
    kv = pl.BlockSpec((N_KV_HEADS, BLOCK, HEAD_DIM), lambda i: (0, i, 0))
    return pl.pallas_call(
        body, name="qkv_split", grid=(nb,),
        out_shape=(jax.ShapeDtypeStruct((N_KV_HEADS, nb * grp * BLOCK, HEAD_DIM), BF16),
                   jax.ShapeDtypeStruct((N_KV_HEADS, m, HEAD_DIM), BF16),
                   jax.ShapeDtypeStruct((N_KV_HEADS, m, HEAD_DIM), BF16)),
        in_specs=[pl.BlockSpec((BLOCK, qkv.shape[1]), lambda i: (i, 0)), tab, tab],
        out_specs=(pl.BlockSpec((N_KV_HEADS, grp * BLOCK, HEAD_DIM), lambda i: (0, i, 0)), kv, kv),
        compiler_params=_params(("parallel",)),
    )(qkv, cos, sin)


def _heads_merge(name, o):
    grp = N_Q_HEADS // N_KV_HEADS
    nb = o.shape[1] // (grp * BLOCK)

    def body(o_ref, x_ref):
        for h in range(N_KV_HEADS):
            for g in range(grp):
                col = (h * grp + g) * HEAD_DIM
                x_ref[:, col:col + HEAD_DIM] = o_ref[h, g * BLOCK:(g + 1) * BLOCK, :]

    return pl.pallas_call(
        body, name=name, grid=(nb,),
        out_shape=jax.ShapeDtypeStruct((nb * BLOCK, N_Q_HEADS * HEAD_DIM), o.dtype),
        in_specs=[pl.BlockSpec((N_KV_HEADS, grp * BLOCK, HEAD_DIM), lambda i: (0, i, 0))],
        out_specs=pl.BlockSpec((BLOCK, N_Q_HEADS * HEAD_DIM), lambda i: (i, 0)),
        compiler_params=_params(("parallel",)),
    )(o)


def _heads_split(name, x):
    grp = N_Q_HEADS // N_KV_HEADS
    nb = x.shape[0] // BLOCK

    def body(x_ref, o_ref):
        for h in range(N_KV_HEADS):
            for g in range(grp):
                col = (h * grp + g) * HEAD_DIM
                o_ref[h, g * BLOCK:(g + 1) * BLOCK, :] = x_ref[:, col:col + HEAD_DIM]

    return pl.pallas_call(
        body, name=name, grid=(nb,),
        out_shape=jax.ShapeDtypeStruct((N_KV_HEADS, nb * grp * BLOCK, HEAD_DIM), x.dtype),
        in_specs=[pl.BlockSpec((BLOCK, N_Q_HEADS * HEAD_DIM), lambda i: (i, 0))],
        out_specs=pl.BlockSpec((N_KV_HEADS, grp * BLOCK, HEAD_DIM), lambda i: (0, i, 0)),
        compiler_params=_params(("parallel",)),
    )(x)


def _attn_probs(i, q, kb, sink):
    rows = q.shape[0]
    grp = rows // BLOCK
    s = lax.dot_general(q, kb, _NT, preferred_element_type=F32)
    r = lax.broadcasted_iota(jnp.int32, (BLOCK, 2 * BLOCK), 0)
    cidx = lax.broadcasted_iota(jnp.int32, (BLOCK, 2 * BLOCK), 1)
    key = (i - 1) * BLOCK + cidx
    allowed = (cidx > r) & (cidx <= r + BLOCK) & (key >= ROW0 - N_META)
    s = jnp.where(allowed[None], s.reshape(grp, BLOCK, 2 * BLOCK), NEG_INF).reshape(rows, 2 * BLOCK)
    mx = jnp.maximum(jnp.max(s, axis=-1, keepdims=True), sink)
    e = jnp.exp(s - mx)
    es = jnp.exp(sink - mx)
    den = jnp.sum(e, axis=-1, keepdims=True) + es
    return e / den, es / den


def _attn_specs(grp):
    q = pl.BlockSpec((None, grp * BLOCK, HEAD_DIM), lambda h, i: (h, i, 0))
    cur = pl.BlockSpec((None, BLOCK, HEAD_DIM), lambda h, i: (h, i, 0))
    prev = pl.BlockSpec((None, BLOCK, HEAD_DIM), lambda h, i: (h, jnp.maximum(i - 1, 0), 0))
    sink = pl.BlockSpec((None, grp * BLOCK, 1), lambda h, i: (h, 0, 0))
    return q, cur, prev, sink


def _attn_fwd(q, k, v, sink_rows):
    grp = N_Q_HEADS // N_KV_HEADS
    nb = k.shape[1] // BLOCK

    def body(q_ref, kc_ref, kp_ref, vc_ref, vp_ref, s_ref, o_ref):
        kb = jnp.concatenate([kp_ref[...], kc_ref[...]], axis=0)
        vb = jnp.concatenate([vp_ref[...], vc_ref[...]], axis=0)
        p, _ = _attn_probs(pl.program_id(1), q_ref[...], kb, s_ref[...])
        o_ref[...] = jnp.dot(p.astype(BF16), vb, preferred_element_type=F32).astype(BF16)

    qs, cur, prev, sink = _attn_specs(grp)
    return pl.pallas_call(
        body, name="attn_fwd", grid=(N_KV_HEADS, nb), out_shape=jax.ShapeDtypeStruct(q.shape, BF16),
        in_specs=[qs, cur, prev, cur, prev, sink], out_specs=qs,
        compiler_params=_params(("parallel", "parallel")),
    )(q, k, k, v, v, sink_rows)


def _attn_bwd(q, k, v, sink_rows, do):
    grp = N_Q_HEADS // N_KV_HEADS
    nb = k.shape[1] // BLOCK

    def body(q_ref, kc_ref, kp_ref, vc_ref, vp_ref, s_ref, do_ref, dq_ref, dk_ref, dv_ref, ds_ref):
        i = pl.program_id(1)
        qv, dov = q_ref[...], do_ref[...]
        kb = jnp.concatenate([kp_ref[...], kc_ref[...]], axis=0)
        vb = jnp.concatenate([vp_ref[...], vc_ref[...]], axis=0)
        p, p_sink = _attn_probs(i, qv, kb, s_ref[...])
        pb = p.astype(BF16)
        o = jnp.dot(pb, vb, preferred_element_type=F32)
        delta = jnp.sum(dov.astype(F32) * o, axis=-1, keepdims=True)
        dp = lax.dot_general(dov, vb, _NT, preferred_element_type=F32)
        dsb = (p * (dp - delta)).astype(BF16)
        dq_ref[...] = jnp.dot(dsb, kb, preferred_element_type=F32).astype(BF16)
        dk_ref[...] = lax.dot_general(dsb, qv, _TN, preferred_element_type=F32)
        dv_ref[...] = lax.dot_general(pb, dov, _TN, preferred_element_type=F32)

        @pl.when(i == 0)
        def _():
            ds_ref[...] = jnp.zeros_like(ds_ref)

        ds_ref[...] -= p_sink * delta

    qs, cur, prev, sink = _attn_specs(grp)
    band = pl.BlockSpec((None, None, 2 * BLOCK, HEAD_DIM), lambda h, i: (h, i, 0, 0))
    band_shape = jax.ShapeDtypeStruct((N_KV_HEADS, nb, 2 * BLOCK, HEAD_DIM), F32)
    return pl.pallas_call(
        body, name="attn_bwd", grid=(N_KV_HEADS, nb),
        out_shape=(jax.ShapeDtypeStruct(q.shape, BF16), band_shape, band_shape,
                   jax.ShapeDtypeStruct(sink_rows.shape, F32)),
        in_specs=[qs, cur, prev, cur, prev, sink, qs], out_specs=(qs, band, band, sink),
        compiler_params=_params(("parallel", "arbitrary")),
    )(q, k, k, v, v, sink_rows, do)


def _qkv_merge_bwd(dq, dkb, dvb, cos, sin):
    grp = N_Q_HEADS // N_KV_HEADS
    nb = dkb.shape[1]
    width = (N_Q_HEADS + 2 * N_KV_HEADS) * HEAD_DIM
    scale = HEAD_DIM ** -0.5

    def body(dq_ref, kc_ref, kn_ref, vc_ref, vn_ref, c_ref, s_ref, o_ref):
        last = pl.program_id(0) == nb - 1
        c, s = c_ref[...], s_ref[...]
        for h in range(N_KV_HEADS):
            for g in range(grp):
                col = (h * grp + g) * HEAD_DIM
                y = dq_ref[h, g * BLOCK:(g + 1) * BLOCK, :].astype(F32) * scale
                o_ref[:, col:col + HEAD_DIM] = _rope_t(y, c, s).astype(BF16)
            dk = kc_ref[h, BLOCK:, :] + jnp.where(last, 0.0, kn_ref[h, :BLOCK, :])
            col = (N_Q_HEADS + h) * HEAD_DIM
            o_ref[:, col:col + HEAD_DIM] = _rope_t(dk, c, s).astype(BF16)
            dv = vc_ref[h, BLOCK:, :] + jnp.where(last, 0.0, vn_ref[h, :BLOCK, :])
            col = (N_Q_HEADS + N_KV_HEADS + h) * HEAD_DIM
            o_ref[:, col:col + HEAD_DIM] = dv.astype(BF16)

    tab = pl.BlockSpec((BLOCK, HEAD_DIM // 2), lambda i: (i, 0))
    cur = pl.BlockSpec((N_KV_HEADS, None, 2 * BLOCK, HEAD_DIM), lambda i: (0, i, 0, 0))
    nxt = pl.BlockSpec((N_KV_HEADS, None, 2 * BLOCK, HEAD_DIM), lambda i: (0, jnp.minimum(i + 1, nb - 1), 0, 0))
    return pl.pallas_call(
        body, name="qkv_merge_bwd", grid=(nb,), out_shape=jax.ShapeDtypeStruct((nb * BLOCK, width), BF16),
        in_specs=[pl.BlockSpec((N_KV_HEADS, grp * BLOCK, HEAD_DIM), lambda i: (0, i, 0)),
                  cur, nxt, cur, nxt, tab, tab],
        out_specs=pl.BlockSpec((BLOCK, width), lambda i: (i, 0)),
        compiler_params=_params(("parallel",)),
    )(dq, dkb, dkb, dvb, dvb, cos, sin)


def _tiles2d(r, c):
    tc = _pick(c, 2048, 128) if c % 128 == 0 else c
    tr = _pick(r, max(8, (1 << 20) // tc // 8 * 8), 8) if r % 8 == 0 else r
    return tr, tc


def _cast_bf16(name, w, place):
    r, c = w.shape
    tr, tc = _tiles2d(r, c)
    if tr % 16:
        tr = r

    def body(place_ref, w_ref, o_ref):
        o_ref[...] = w_ref[...].astype(BF16)

    return pl.pallas_call(
        body, name=name, out_shape=jax.ShapeDtypeStruct((N_CHIPS, r, c), BF16),
        grid_spec=pltpu.PrefetchScalarGridSpec(
            num_scalar_prefetch=1, grid=(r // tr, c // tc),
            in_specs=[pl.BlockSpec((tr, tc), lambda i, j, p: (i, j))],
            out_specs=pl.BlockSpec((None, tr, tc), lambda i, j, p: (p[1], i, j))),
        compiler_params=_params(("parallel", "parallel")),
    )(place, w)


def _pair_sum(name, g, got, place):
    n, r, c = g.shape
    half = r // 2
    tr, tc = _tiles2d(half, c)
    nh = half // tr

    def body(place_ref, g_ref, got_ref, o_ref, own_ref):
        s = (g_ref[...] + got_ref[...]).astype(BF16)
        o_ref[...] = s

        @pl.when(pl.program_id(2) == place_ref[1])
        def _():
            own_ref[...] = s

    tile = pl.BlockSpec((None, tr, tc), lambda i, j, k, p: (k, i, j))
    shape = jax.ShapeDtypeStruct((n, half, c), BF16)
    return pl.pallas_call(
        body, name=name, out_shape=(shape, shape),
        grid_spec=pltpu.PrefetchScalarGridSpec(
            num_scalar_prefetch=1, grid=(nh, c // tc, n),
            in_specs=[pl.BlockSpec((None, tr, tc), lambda i, j, k, p: (k, p[0] * nh + i, j)), tile],
            out_specs=(tile, pl.BlockSpec((None, tr, tc), lambda i, j, k, p: (p[1], i, j)))),
        compiler_params=_params(("parallel", "parallel", "arbitrary")),
    )(place, g, got)


def _chip_sum(name, parts, place):
    n, half, c = parts.shape
    tr, tc = _tiles2d(half, c)
    nh = half // tr

    def body(place_ref, p0, p1, p2, p3, o_ref):
        o_ref[...] = ((p0[...].astype(F32) + p1[...].astype(F32)) + p2[...].astype(F32)) + p3[...].astype(F32)

    def chip(k):
        return pl.BlockSpec((None, tr, tc), lambda i, j, p: (k, i, j))

    return pl.pallas_call(
        body, name=name, out_shape=jax.ShapeDtypeStruct((2 * half, c), F32),
        grid_spec=pltpu.PrefetchScalarGridSpec(
            num_scalar_prefetch=1, grid=(nh, c // tc),
            in_specs=[chip(k) for k in range(n)],
            out_specs=pl.BlockSpec((tr, tc), lambda i, j, p: (p[0] * nh + i, j))),
        compiler_params=_params(("parallel", "parallel")),
    )(place, parts, parts, parts, parts)


def _dev_sum(gathered):
    def body(g_ref, o_ref):
        acc = g_ref[0]
        for k in range(1, N_DEV):
            acc = acc + g_ref[k]
        o_ref[...] = acc

    return pl.pallas_call(body, name="dev_sum", out_shape=jax.ShapeDtypeStruct(gathered.shape[1:], F32))(gathered)


def _adamw(name, w, g, m, v):
    r, c = w.shape
    tr, tc = _tiles2d(r, c)
    if r % 8 == 0:
        tr = _pick(r, max(8, (1 << 18) // tc // 8 * 8), 8)

    def body(w_ref, g_ref, m_ref, v_ref, d_ref, mo_ref, vo_ref):
        gv = g_ref[...]
        mn = ADAM_B1 * m_ref[...] + (1.0 - ADAM_B1) * gv
        vn = ADAM_B2 * v_ref[...] + (1.0 - ADAM_B2) * jnp.square(gv)
        m_hat = mn / (1.0 - ADAM_B1 ** ADAM_STEP)
        v_hat = vn / (1.0 - ADAM_B2 ** ADAM_STEP)
        d_ref[...] = -ADAM_LR * (m_hat / (jnp.sqrt(v_hat) + ADAM_EPS) + ADAM_WD * w_ref[...])
        mo_ref[...] = mn
        vo_ref[...] = vn

    tile = pl.BlockSpec((tr, tc), lambda i, j: (i, j))
    shape = jax.ShapeDtypeStruct((r, c), F32)
    return pl.pallas_call(
        body, name=name, grid=(r // tr, c // tc), out_shape=(shape, shape, shape),
        in_specs=[tile] * 4, out_specs=(tile,) * 3, compiler_params=_params(("parallel", "parallel")),
    )(w, g, m, v)


HBM_SPEC = pl.BlockSpec(memory_space=pltpu.HBM)


def _place():
    x, y, c = lax.axis_index("x"), lax.axis_index("y"), lax.axis_index("c")
    chips = [(1 - x, y), (x, 1 - y), (1 - x, 1 - y)]
    return x, y, c, 2 * x + y, chips


def _all_gather(stacks):
    n = len(stacks)
    halves = [s.shape[1] // 2 for s in stacks]

    def body(*refs):
        dst = refs[n:2 * n]
        send_a, recv_a, send_b, recv_b = refs[2 * n:]
        x, y, c, me, chips = _place()
        sibling = (x, y, 1 - c)

        def half(w, chip, hc):
            return dst[w].at[chip, pl.ds(hc * halves[w], halves[w]), :]

        started = []
        for j, (cx, cy) in enumerate(chips):
            for w in range(n):
                cp = pltpu.make_async_remote_copy(
                    src_ref=half(w, me, c), dst_ref=half(w, me, c),
                    send_sem=send_a.at[j * n + w], recv_sem=recv_a.at[j * n + w],
                    device_id=(cx, cy, c), device_id_type=MESH)
                cp.start()
                started.append(cp)
        for j, (cx, cy) in enumerate(chips):
            for w in range(n):
                got = half(w, 2 * cx + cy, c)
                pltpu.make_async_remote_copy(
                    src_ref=got, dst_ref=got, send_sem=send_a.at[j * n + w], recv_sem=recv_a.at[j * n + w],
                    device_id=(cx, cy, c), device_id_type=MESH).wait_recv()
                cp = pltpu.make_async_remote_copy(
                    src_ref=got, dst_ref=got, send_sem=send_b.at[j * n + w], recv_sem=recv_b.at[j * n + w],
                    device_id=sibling, device_id_type=MESH)
                cp.start()
                started.append(cp)
        for j, (cx, cy) in enumerate(chips):
            for w in range(n):
                got = half(w, 2 * cx + cy, 1 - c)
                pltpu.make_async_remote_copy(
                    src_ref=got, dst_ref=got, send_sem=send_b.at[j * n + w], recv_sem=recv_b.at[j * n + w],
                    device_id=sibling, device_id_type=MESH).wait_recv()
        for cp in started:
            cp.wait_send()

    return pl.pallas_call(
        body, name="all_gather_weights",
        out_shape=[jax.ShapeDtypeStruct(s.shape, s.dtype) for s in stacks],
        in_specs=[HBM_SPEC] * n, out_specs=[HBM_SPEC] * n,
        input_output_aliases={w: w for w in range(n)},
        scratch_shapes=[pltpu.SemaphoreType.DMA((3 * n,))] * 4,
    )(*stacks)


def _pair_exchange(grads, small):
    n = len(grads)
    halves = [g.shape[1] // 2 for g in grads]

    def body(*refs):
        src, small_ref = refs[:n], refs[n]
        dst, all_ref = refs[n + 1:2 * n + 1], refs[2 * n + 1]
        send, recv, ssend, srecv, local_sem = refs[2 * n + 2:]
        x, y, c, me, _ = _place()
        sibling = (x, y, 1 - c)
        dev = 2 * me + c
        mine = pltpu.make_async_copy(small_ref, all_ref.at[dev], local_sem)
        mine.start()
        started = []
        for w in range(n):
            cp = pltpu.make_async_remote_copy(
                src_ref=src[w].at[:, pl.ds((1 - c) * halves[w], halves[w]), :], dst_ref=dst[w],
                send_sem=send.at[w], recv_sem=recv.at[w], device_id=sibling, device_id_type=MESH)
            cp.start()
            started.append(cp)
        for t in range(1, N_DEV):
            tx, ty, tc = (t >> 2) & 1, (t >> 1) & 1, t & 1
            cp = pltpu.make_async_remote_copy(
                src_ref=small_ref, dst_ref=all_ref.at[dev], send_sem=ssend.at[t], recv_sem=srecv.at[t],
                device_id=(x ^ tx, y ^ ty, c ^ tc), device_id_type=MESH)
            cp.start()
            started.append(cp)
        for w in range(n):
            pltpu.make_async_remote_copy(
                src_ref=dst[w], dst_ref=dst[w], send_sem=send.at[w], recv_sem=recv.at[w],
                device_id=sibling, device_id_type=MESH).wait_recv()
        for t in range(1, N_DEV):
            tx, ty, tc = (t >> 2) & 1, (t >> 1) & 1, t & 1
            peer = 4 * (x ^ tx) + 2 * (y ^ ty) + (c ^ tc)
            pltpu.make_async_remote_copy(
                src_ref=all_ref.at[peer], dst_ref=all_ref.at[peer], send_sem=ssend.at[t], recv_sem=srecv.at[t],
                device_id=(x ^ tx, y ^ ty, c ^ tc), device_id_type=MESH).wait_recv()
        for cp in started:
            cp.wait_send()
        mine.wait()

    out_shape = [jax.ShapeDtypeStruct((N_CHIPS, h, g.shape[2]), F32) for g, h in zip(grads, halves)]
    out_shape.append(jax.ShapeDtypeStruct((N_DEV,) + small.shape, F32))
    return pl.pallas_call(
        body, name="pair_exchange", out_shape=out_shape,
        in_specs=[HBM_SPEC] * (n + 1), out_specs=[HBM_SPEC] * (n + 1),
        scratch_shapes=[pltpu.SemaphoreType.DMA((n,)), pltpu.SemaphoreType.DMA((n,)),
                        pltpu.SemaphoreType.DMA((N_DEV,)), pltpu.SemaphoreType.DMA((N_DEV,)),
                        pltpu.SemaphoreType.DMA],
    )(*grads, small)


def _chip_exchange(partials, landing):
    n = len(partials)

    def body(*refs):
        src, dst = refs[:n], refs[2 * n:3 * n]
        send, recv = refs[3 * n:]
        x, y, c, me, chips = _place()
        started = []
        for j, (cx, cy) in enumerate(chips):
            for w in range(n):
                cp = pltpu.make_async_remote_copy(
                    src_ref=src[w].at[2 * cx + cy], dst_ref=dst[w].at[me],
                    send_sem=send.at[j * n + w], recv_sem=recv.at[j * n + w],
                    device_id=(cx, cy, c), device_id_type=MESH)
                cp.start()
                started.append(cp)
        for j, (cx, cy) in enumerate(chips):
            for w in range(n):
                got = dst[w].at[2 * cx + cy]
                pltpu.make_async_remote_copy(
                    src_ref=got, dst_ref=got, send_sem=send.at[j * n + w], recv_sem=recv.at[j * n + w],
                    device_id=(cx, cy, c), device_id_type=MESH).wait_recv()
        for cp in started:
            cp.wait_send()

    return pl.pallas_call(
        body, name="chip_exchange", out_shape=[jax.ShapeDtypeStruct(p.shape, p.dtype) for p in landing],
        in_specs=[HBM_SPEC] * (2 * n), out_specs=[HBM_SPEC] * n,
        input_output_aliases={n + w: w for w in range(n)},
        scratch_shapes=[pltpu.SemaphoreType.DMA((3 * n,))] * 2,
    )(*partials, *landing)


def _pair_share(blocks):
    n = len(blocks)

    def body(*refs):
        dst = refs[n:2 * n]
        send, recv = refs[2 * n:]
        x, y, c, _, _ = _place()
        sibling = (x, y, 1 - c)

        def half(w, hc):
            h = blocks[w].shape[0] // 2
            return dst[w].at[pl.ds(hc * h, h), :]

        sent = [pltpu.make_async_remote_copy(src_ref=half(w, c), dst_ref=half(w, c), send_sem=send.at[w],
                                             recv_sem=recv.at[w], device_id=sibling, device_id_type=MESH)
                for w in range(n)]
        for cp in sent:
            cp.start()
        for w in range(n):
            got = half(w, 1 - c)
            pltpu.make_async_remote_copy(src_ref=got, dst_ref=got, send_sem=send.at[w], recv_sem=recv.at[w],
                                         device_id=sibling, device_id_type=MESH).wait_recv()
        for cp in sent:
            cp.wait_send()

    return pl.pallas_call(
        body, name="pair_share", out_shape=[jax.ShapeDtypeStruct(b.shape, F32) for b in blocks],
        in_specs=[HBM_SPEC] * n, out_specs=[HBM_SPEC] * n,
        input_output_aliases={w: w for w in range(n)},
        scratch_shapes=[pltpu.SemaphoreType.DMA((n,))] * 2,
    )(*blocks)


MATRICES = ("w_in_conv", "w_out_conv", "w_up_0", "w_down_0", "w_qkv", "w_o", "w_up_1", "w_down_1")
COLUMN_SHARDED = ("w_in_conv", "w_up_0", "w_qkv", "w_up_1")
NORMS = ("norm_mix_0", "norm_mlp_0", "norm_mix_1", "norm_mlp_1", "norm_final")


def _mlp_fwd(tag, h, g, w_up, w_down):
    n = _rms_fwd("norm_mlp_" + tag, h, g)
    z = _mm_nn_cols("mlp_up_" + tag, n, w_up, BF16)
    out = _mm_nn_rows("mlp_down_" + tag, z, w_down, F32, a_pro=_relu_sq,
                      epi=lambda acc, res: acc + res, extras=(h,))
    return out, (h, n, z)


def _mlp_bwd(tag, saved, g, w_up, w_down, dh, dh_bf):
    h, n, z = saved
    dz = _mm_nt_rows("mlp_down_dx_" + tag, dh_bf, w_down, BF16,
                     epi=lambda acc, zt: acc * (2.0 * jnp.maximum(zt.astype(F32), 0.0)), extras=(z,))
    dw_down = _mm_tn("mlp_down_dw_" + tag, z, dh_bf, stacked=False, a_pro=_relu_sq)
    dw_up = _mm_tn("mlp_up_dw_" + tag, n, dz, stacked=True)
    dn = _mm_nt_cols("mlp_up_dx_" + tag, dz, w_up, F32)
    dh, dh_bf, dg = _rms_bwd("norm_mlp_bwd_" + tag, dn, h, g, dh)
    return dh, dh_bf, dg, dw_up, dw_down


def _local_step(h0, target, w, norms, sinks, conv_w8):
    m = h0.shape[0]
    grp = N_Q_HEADS // N_KV_HEADS
    cos, sin = _rope_tables(m)
    sink_rows = jnp.repeat(sinks.astype(F32), BLOCK).reshape(N_KV_HEADS, grp * BLOCK, 1)

    n0 = _rms_fwd("norm_mix_0", h0, norms["norm_mix_0"])
    bcu = _mm_nn_cols("conv_in", n0, w["w_in_conv"], BF16)
    gate = _conv_fwd(bcu, conv_w8)
    h1 = _mm_nn_rows("conv_out", gate, w["w_out_conv"], F32, epi=lambda acc, res: acc + res, extras=(h0,))
    h2, mlp0 = _mlp_fwd("0", h1, norms["norm_mlp_0"], w["w_up_0"], w["w_down_0"])
    n2 = _rms_fwd("norm_mix_1", h2, norms["norm_mix_1"])
    qkv = _mm_nn_cols("attn_qkv", n2, w["w_qkv"], BF16)
    q, k, v = _qkv_split(qkv, cos, sin)
    o = _heads_merge("attn_o_merge", _attn_fwd(q, k, v, sink_rows))
    h3 = _mm_nn_rows("attn_out", o, w["w_o"], F32, epi=lambda acc, res: acc + res, extras=(h2,))
    h4, mlp1 = _mlp_fwd("1", h3, norms["norm_mlp_1"], w["w_up_1"], w["w_down_1"])

    grads = {}
    loss, dh, dh_bf, grads["norm_final"] = _loss_head(h4, norms["norm_final"], target)
    dh, dh_bf, grads["norm_mlp_1"], grads["w_up_1"], grads["w_down_1"] = _mlp_bwd(
        "1", mlp1, norms["norm_mlp_1"], w["w_up_1"], w["w_down_1"], dh, dh_bf)
    do = _mm_nt_rows("attn_out_dx", dh_bf, w["w_o"], BF16)
    grads["w_o"] = _mm_tn("attn_out_dw", o, dh_bf, stacked=False)
    dq, dkb, dvb, dsink = _attn_bwd(q, k, v, sink_rows, _heads_split("attn_do_split", do))
    grads["attn_sinks"] = dsink
    dqkv = _qkv_merge_bwd(dq, dkb, dvb, cos, sin)
    grads["w_qkv"] = _mm_tn("attn_qkv_dw", n2, dqkv, stacked=True)
    dn = _mm_nt_cols("attn_qkv_dx", dqkv, w["w_qkv"], F32)
    dh, dh_bf, grads["norm_mix_1"] = _rms_bwd("norm_mix_bwd_1", dn, h2, norms["norm_mix_1"], dh)
    dh, dh_bf, grads["norm_mlp_0"], grads["w_up_0"], grads["w_down_0"] = _mlp_bwd(
        "0", mlp0, norms["norm_mlp_0"], w["w_up_0"], w["w_down_0"], dh, dh_bf)
    dgate = _mm_nt_rows("conv_out_dx", dh_bf, w["w_out_conv"], BF16)
    grads["w_out_conv"] = _mm_tn("conv_out_dw", gate, dh_bf, stacked=False)
    dbcu, grads["conv_w"] = _conv_bwd(bcu, conv_w8, dgate)
    grads["w_in_conv"] = _mm_tn("conv_in_dw", n0, dbcu, stacked=True)
    dn = _mm_nt_cols("conv_in_dx", dbcu, w["w_in_conv"], F32)
    dh, _, grads["norm_mix_0"] = _rms_bwd("norm_mix_bwd_0", dn, h0, norms["norm_mix_0"], dh)
    return loss, dh, grads


def kernel(x, meta_tokens, norm_mix_0, w_in_conv, conv_w, w_out_conv, norm_mlp_0, w_up_0, w_down_0, norm_mix_1, w_qkv, attn_sinks, w_o, norm_mlp_1, w_up_1, w_down_1, norm_final, loss_target, m_meta_tokens, m_norm_mix_0, m_w_in_conv, m_conv_w, m_w_out_conv, m_norm_mlp_0, m_w_up_0, m_w_down_0, m_norm_mix_1, m_w_qkv, m_attn_sinks, m_w_o, m_norm_mlp_1, m_w_up_1, m_w_down_1, m_norm_final, v_meta_tokens, v_norm_mix_0, v_w_in_conv, v_conv_w, v_w_out_conv, v_norm_mlp_0, v_w_up_0, v_w_down_0, v_norm_mix_1, v_w_qkv, v_attn_sinks, v_w_o, v_norm_mlp_1, v_w_up_1, v_w_down_1, v_norm_final):
    given = dict(locals())
    names = ("meta_tokens", "norm_mix_0", "w_in_conv", "conv_w", "w_out_conv", "norm_mlp_0", "w_up_0", "w_down_0",
             "norm_mix_1", "w_qkv", "attn_sinks", "w_o", "norm_mlp_1", "w_up_1", "w_down_1", "norm_final")
    d = D_MODEL
    dc = d // N_CHIPS
    chip = 2 * lax.axis_index("x") + lax.axis_index("y")
    place = jnp.stack([lax.axis_index("c"), chip]).astype(jnp.int32)

    small = jnp.zeros((SMALL_ROWS, dc), F32).at[:N_META].set(meta_tokens).at[N_META:N_META + CONV_WIDTH].set(conv_w)
    small = lax.dynamic_update_slice(jnp.zeros((N_CHIPS, SMALL_ROWS, dc), F32), small[None], (chip, 0, 0))
    stacks = [_cast_bf16("cast_" + n, given[n], place) for n in MATRICES] + [small]
    gathered = _all_gather(stacks)
    w = {}
    for n, g in zip(MATRICES, gathered):
        w[n] = g if n in COLUMN_SHARDED else g.reshape(N_CHIPS * g.shape[1], g.shape[2])
    small_all = gathered[-1]
    small_full = jnp.transpose(small_all, (1, 0, 2)).reshape(SMALL_ROWS, d)
    meta_full = small_full[:N_META]
    conv_w8 = small_full[N_META:N_META + 8]

    pad = ROW0 - N_META
    h0 = jnp.concatenate([jnp.zeros((pad, d), F32), meta_full, x[0]], axis=0)
    norms = {n: given[n] for n in NORMS}
    loss_part, dh0, grads = _local_step(h0, loss_target[0], w, norms, attn_sinks, conv_w8)
    loss = lax.psum(loss_part[0, 0], ("x", "y", "c"))
    grad_x = dh0[ROW0:][None]

    grp = N_Q_HEADS // N_KV_HEADS
    small_grad = jnp.zeros((SMALL_ROWS, d), F32).at[:N_META].set(dh0[pad:ROW0]).at[N_META:N_META + 8].set(grads["conv_w"])
    small_grad = jnp.transpose(small_grad.reshape(SMALL_ROWS, N_CHIPS, dc), (1, 0, 2))
    full = [grads[n] if n in COLUMN_SHARDED else grads[n].reshape(N_CHIPS, -1, grads[n].shape[1]) for n in MATRICES]
    full.append(small_grad)
    dsink = jnp.sum(grads["attn_sinks"].reshape(N_Q_HEADS, BLOCK), axis=1)
    rep = jnp.zeros((8, d), F32)
    for r, n in enumerate(NORMS):
        rep = rep.at[r].set(jnp.sum(grads[n], axis=0))
    rep = rep.at[len(NORMS), :N_Q_HEADS].set(dsink)

    ex = _pair_exchange(full, rep)
    got, rep_all = ex[:-1], ex[-1]
    summed = [_pair_sum("pair_sum_%d" % i, g, o, place) for i, (g, o) in enumerate(zip(full, got))]
    landed = _chip_exchange([s[0] for s in summed], [s[1] for s in summed])
    reduced = _pair_share([_chip_sum("chip_sum_%d" % i, p, place) for i, p in enumerate(landed)])
    rep_sum = _dev_sum(rep_all)

    g_out = dict(zip(MATRICES, reduced[:-1]))
    g_out["meta_tokens"] = reduced[-1][:N_META]
    g_out["conv_w"] = reduced[-1][N_META:N_META + CONV_WIDTH]
    for r, n in enumerate(NORMS):
        g_out[n] = rep_sum[r]
    g_out["attn_sinks"] = rep_sum[len(NORMS), :N_Q_HEADS]

    delta, new_m, new_v = {}, {}, {}
    for n in names:
        wt = given[n]
        shape2 = wt.shape if wt.ndim == 2 else (1, wt.shape[0])
        outs = _adamw("adamw_" + n, wt.reshape(shape2), g_out[n].reshape(shape2),
                      given["m_" + n].reshape(shape2), given["v_" + n].reshape(shape2))
        delta[n], new_m[n], new_v[n] = [o.reshape(wt.shape) for o in outs]
    return (loss, grad_x, *[g_out[n] for n in names], *[delta[n] for n in names],
            *[new_m[n] for n in names], *[new_v[n] for n in names])
```

```python
import functools

import jax
import jax.numpy as jnp
from jax import lax
from jax.experimental import pallas as pl
from jax.experimental.pallas import tpu as pltpu

F32 = jnp.float32
BF16 = jnp.bfloat16

D_MODEL = 2048
SEQ = 8192
N_META = 16
CONV_WIDTH = 3
HEAD_DIM = 64
N_Q_HEADS = 32
N_KV_HEADS = 4
BLOCK = 128
ROPE_THETA = 10000.0
D_FF = 4 * D_MODEL
RMS_EPS = 1e-5
NEG_INF = -1e30

ADAM_LR = 0.001
ADAM_B1 = 0.9
ADAM_B2 = 0.999
ADAM_EPS = 1e-08
ADAM_WD = 0.01
ADAM_STEP = 10

N_CHIPS = 4
N_DEV = 8
MESH = pl.DeviceIdType.MESH
VMEM_LIMIT = 56 * 1024 * 1024
SMALL_ROWS = 32
ROW0 = BLOCK


def _pick(n, target, mult):
    best = None
    for t in range(mult, min(n, target) + 1, mult):
        if n % t == 0:
            best = t
    assert best is not None, (n, target, mult)
    return best


def _params(sem=None):
    return pltpu.CompilerParams(dimension_semantics=sem, vmem_limit_bytes=VMEM_LIMIT)


def _mm(name, a, b, *, dims, grid, a_spec, b_spec, out_shape, out_spec, nk, acc_shape,
        extras=(), extra_specs=(), a_pro=None, epi=None):
    n_ex = len(extras)
    acc_in_out = epi is None and out_shape.dtype == F32

    def body(*refs):
        a_ref, b_ref = refs[0], refs[1]
        ex = refs[2:2 + n_ex]
        o_ref = refs[2 + n_ex]
        av = a_ref[...]
        if a_pro is not None:
            av = a_pro(av)
        part = lax.dot_general(av, b_ref[...], dims, preferred_element_type=F32)

        def finish(acc):
            r = acc if epi is None else epi(acc, *[e[...] for e in ex])
            o_ref[...] = r.astype(o_ref.dtype)

        if nk == 1:
            finish(part)
            return
        acc_ref = o_ref if acc_in_out else refs[3 + n_ex]
        kk = pl.program_id(len(grid) - 1)

        @pl.when(kk == 0)
        def _():
            acc_ref[...] = part

        @pl.when(kk > 0)
        def _():
            acc_ref[...] += part

        if not acc_in_out:
            @pl.when(kk == nk - 1)
            def _():
                finish(acc_ref[...])

    scratch = [] if (nk == 1 or acc_in_out) else [pltpu.VMEM(acc_shape, F32)]
    return pl.pallas_call(
        body, name=name, grid=grid, out_shape=out_shape,
        in_specs=[a_spec, b_spec, *extra_specs], out_specs=out_spec, scratch_shapes=scratch,
        compiler_params=_params(("parallel", "parallel", "arbitrary")),
    )(a, b, *extras)


_NN = (((1,), (0,)), ((), ()))
_NT = (((1,), (1,)), ((), ()))
_TN = (((0,), (0,)), ((), ()))


def _mm_nn_cols(name, a, w, out_dtype, epi=None, extras=()):
    m, k = a.shape
    _, _, ns = w.shape
    tm, tn = _pick(m, 832, 16), _pick(ns, 1024, 128)
    per = ns // tn
    tile = pl.BlockSpec((tm, tn), lambda j, i, kk: (i, j))
    return _mm(name, a, w, dims=_NN, grid=(N_CHIPS * per, m // tm, 1),
               a_spec=pl.BlockSpec((tm, k), lambda j, i, kk: (i, 0)),
               b_spec=pl.BlockSpec((None, k, tn), lambda j, i, kk: (j // per, 0, j % per)),
               out_shape=jax.ShapeDtypeStruct((m, N_CHIPS * ns), out_dtype), out_spec=tile,
               nk=1, acc_shape=(tm, tn), extras=extras, extra_specs=[tile] * len(extras), epi=epi)


def _mm_nn_rows(name, a, w, out_dtype, a_pro=None, epi=None, extras=()):
    m, k = a.shape
    _, n = w.shape
    tm, tn, tk = _pick(m, 832, 16), _pick(n, 1024, 128), _pick(k, 2048, 128)
    tile = pl.BlockSpec((tm, tn), lambda j, i, kk: (i, j))
    return _mm(name, a, w, dims=_NN, grid=(n // tn, m // tm, k // tk),
               a_spec=pl.BlockSpec((tm, tk), lambda j, i, kk: (i, kk)),
               b_spec=pl.BlockSpec((tk, tn), lambda j, i, kk: (kk, j)),
               out_shape=jax.ShapeDtypeStruct((m, n), out_dtype), out_spec=tile,
               nk=k // tk, acc_shape=(tm, tn), extras=extras, extra_specs=[tile] * len(extras),
               a_pro=a_pro, epi=epi)


def _mm_nt_rows(name, a, w, out_dtype, epi=None, extras=()):
    m, c = a.shape
    r, _ = w.shape
    tm, tn = _pick(m, 832, 16), _pick(r, 1024, 128)
    tile = pl.BlockSpec((tm, tn), lambda j, i, kk: (i, j))
    return _mm(name, a, w, dims=_NT, grid=(r // tn, m // tm, 1),
               a_spec=pl.BlockSpec((tm, c), lambda j, i, kk: (i, 0)),
               b_spec=pl.BlockSpec((tn, c), lambda j, i, kk: (j, 0)),
               out_shape=jax.ShapeDtypeStruct((m, r), out_dtype), out_spec=tile,
               nk=1, acc_shape=(tm, tn), extras=extras, extra_specs=[tile] * len(extras), epi=epi)


def _mm_nt_cols(name, a, w, out_dtype):
    m, _ = a.shape
    _, r, ns = w.shape
    tm, tn, tk = _pick(m, 832, 16), _pick(r, 1024, 128), _pick(ns, 2048, 128)
    per = ns // tk
    return _mm(name, a, w, dims=_NT, grid=(r // tn, m // tm, N_CHIPS * per),
               a_spec=pl.BlockSpec((tm, tk), lambda j, i, kk: (i, kk)),
               b_spec=pl.BlockSpec((None, tn, tk), lambda j, i, kk: (kk // per, j, kk % per)),
               out_shape=jax.ShapeDtypeStruct((m, r), out_dtype),
               out_spec=pl.BlockSpec((tm, tn), lambda j, i, kk: (i, j)),
               nk=N_CHIPS * per, acc_shape=(tm, tn))


def _mm_tn(name, a, b, stacked, a_pro=None):
    t, ka = a.shape
    _, nb = b.shape
    ns = nb // N_CHIPS if stacked else nb
    tt, ta, tb = _pick(t, 1664, 128), _pick(ka, 2048, 128), _pick(ns, 1024, 128)
    if stacked:
        per = ns // tb
        out_shape = jax.ShapeDtypeStruct((N_CHIPS, ka, ns), F32)
        out_spec = pl.BlockSpec((None, ta, tb), lambda i, j, kk: (j // per, i, j % per))
    else:
        out_shape = jax.ShapeDtypeStruct((ka, nb), F32)
        out_spec = pl.BlockSpec((ta, tb), lambda i, j, kk: (i, j))
    return _mm(name, a, b, dims=_TN, grid=(ka // ta, nb // tb, t // tt),
               a_spec=pl.BlockSpec((tt, ta), lambda i, j, kk: (kk, i)),
               b_spec=pl.BlockSpec((tt, tb), lambda i, j, kk: (kk, j)),
               out_shape=out_shape, out_spec=out_spec, nk=t // tt, acc_shape=(ta, tb), a_pro=a_pro)


def _relu_sq(z):
    a = jnp.maximum(z, 0)
    return a * a


def _rms_fwd(name, h, g):
    m, d = h.shape
    tr = _pick(m, 256, 16)

    def body(h_ref, g_ref, o_ref):
        x = h_ref[...]
        rstd = lax.rsqrt(jnp.mean(x * x, axis=-1, keepdims=True) + RMS_EPS)
        o_ref[...] = ((x * rstd) * g_ref[...]).astype(BF16)

    row = pl.BlockSpec((tr, d), lambda i: (i, 0))
    return pl.pallas_call(
        body, name=name, grid=(m // tr,), out_shape=jax.ShapeDtypeStruct((m, d), BF16),
        in_specs=[row, pl.BlockSpec((1, d), lambda i: (0, 0))], out_specs=row,
        compiler_params=_params(("parallel",)),
    )(h, g.reshape(1, d))


def _rms_bwd_math(x, g, dn):
    rstd = lax.rsqrt(jnp.mean(x * x, axis=-1, keepdims=True) + RMS_EPS)
    xhat = x * rstd
    dxhat = dn * g
    dx = rstd * (dxhat - xhat * jnp.mean(dxhat * xhat, axis=-1, keepdims=True))
    return dx, dn * xhat


def _fold8(v):
    r, c = v.shape
    return jnp.sum(v.reshape(r // 8, 8, c), axis=0)


def _rms_bwd(name, dn, h, g, dh_in):
    m, d = h.shape
    tr = _pick(m, 256, 16)
    nt = m // tr

    def body(dn_ref, h_ref, g_ref, dh_ref, o_ref, ob_ref, dg_ref):
        dx, dgp = _rms_bwd_math(h_ref[...], g_ref[...], dn_ref[...])
        dh = dh_ref[...] + dx
        o_ref[...] = dh
        ob_ref[...] = dh.astype(BF16)

        @pl.when(pl.program_id(0) == 0)
        def _():
            dg_ref[...] = jnp.zeros_like(dg_ref)

        dg_ref[...] += _fold8(dgp)

    row = pl.BlockSpec((tr, d), lambda i: (i, 0))
    return pl.pallas_call(
        body, name=name, grid=(nt,),
        out_shape=(jax.ShapeDtypeStruct((m, d), F32), jax.ShapeDtypeStruct((m, d), BF16),
                   jax.ShapeDtypeStruct((8, d), F32)),
        in_specs=[row, row, pl.BlockSpec((1, d), lambda i: (0, 0)), row],
        out_specs=(row, row, pl.BlockSpec((8, d), lambda i: (0, 0))),
        compiler_params=_params(("arbitrary",)),
    )(dn, h, g.reshape(1, d), dh_in)


def _loss_head(h, g, target):
    m, d = h.shape
    tr = BLOCK

    def body(h_ref, g_ref, t_ref, loss_ref, o_ref, ob_ref, dg_ref):
        i = pl.program_id(0)
        x = h_ref[...]
        gv = g_ref[...]
        rstd = lax.rsqrt(jnp.mean(x * x, axis=-1, keepdims=True) + RMS_EPS)
        err = jnp.where(i > 0, (x * rstd) * gv - t_ref[...], 0.0)
        dx, dgp = _rms_bwd_math(x, gv, err * (1.0 / d))
        o_ref[...] = dx
        ob_ref[...] = dx.astype(BF16)

        @pl.when(i == 0)
        def _():
            dg_ref[...] = jnp.zeros_like(dg_ref)
            loss_ref[...] = jnp.zeros_like(loss_ref)

        dg_ref[...] += _fold8(dgp)
        sq = jnp.mean(err * err, axis=-1, keepdims=True)
        loss_ref[...] += 0.5 * jnp.sum(sq, axis=0, keepdims=True)

    row = pl.BlockSpec((tr, d), lambda i: (i, 0))
    return pl.pallas_call(
        body, name="loss_head", grid=(m // tr,),
        out_shape=(jax.ShapeDtypeStruct((8, 128), F32), jax.ShapeDtypeStruct((m, d), F32),
                   jax.ShapeDtypeStruct((m, d), BF16), jax.ShapeDtypeStruct((8, d), F32)),
        in_specs=[row, pl.BlockSpec((1, d), lambda i: (0, 0)),
                  pl.BlockSpec((tr, d), lambda i: (jnp.maximum(i - 1, 0), 0))],
        out_specs=(pl.BlockSpec((8, 128), lambda i: (0, 0)), row, row,
                   pl.BlockSpec((8, d), lambda i: (0, 0))),
        compiler_params=_params(("arbitrary",)),
    )(h, g.reshape(1, d), target)


HALO = 16


def _shift_down(cat, k):
    return pltpu.roll(cat, k, axis=0)[HALO:]


def _shift_up(cat, k):
    n = cat.shape[0]
    return pltpu.roll(cat, n - k, axis=0)[:n - HALO]


def _conv_fwd(bcu, cw):
    m, d3 = bcu.shape
    d = d3 // 3
    tr, tc = _pick(m, 416, 16), _pick(d, 512, 128)
    nd, hb = d // tc, tr // HALO

    def body(b_ref, c_ref, u_ref, ch_ref, uh_ref, w_ref, o_ref):
        i = pl.program_id(0)
        v = c_ref[...].astype(F32) * u_ref[...].astype(F32)
        vh = jnp.where(i > 0, ch_ref[...].astype(F32) * uh_ref[...].astype(F32), 0.0)
        cat = jnp.concatenate([vh, v], axis=0)
        w = w_ref[...]
        conv = w[2:3] * v + w[1:2] * _shift_down(cat, 1) + w[0:1] * _shift_down(cat, 2)
        o_ref[...] = (b_ref[...].astype(F32) * conv).astype(BF16)

    def part(p):
        return pl.BlockSpec((tr, tc), lambda i, j: (i, p * nd + j))

    def halo(p):
        return pl.BlockSpec((HALO, tc), lambda i, j: (jnp.maximum(i * hb - 1, 0), p * nd + j))

    return pl.pallas_call(
        body, name="conv_fwd", grid=(m // tr, nd), out_shape=jax.ShapeDtypeStruct((m, d), BF16),
        in_specs=[part(0), part(1), part(2), halo(1), halo(2), pl.BlockSpec((8, tc), lambda i, j: (0, j))],
        out_specs=pl.BlockSpec((tr, tc), lambda i, j: (i, j)),
        compiler_params=_params(("parallel", "parallel")),
    )(bcu, bcu, bcu, bcu, bcu, cw)


def _conv_bwd(bcu, cw, dg):
    m, d3 = bcu.shape
    d = d3 // 3
    tr, tc = _pick(m, 416, 16), _pick(d, 512, 128)
    nd, hb, nt = d // tc, tr // HALO, m // tr

    def body(b_ref, c_ref, u_ref, ch_ref, uh_ref, bn_ref, dg_ref, dgn_ref, w_ref, o_ref, dw_ref):
        i, p = pl.program_id(1), pl.program_id(2)
        w = w_ref[...]
        b, c, u = b_ref[...].astype(F32), c_ref[...].astype(F32), u_ref[...].astype(F32)
        dgv = dg_ref[...].astype(F32)
        v = c * u
        vh = jnp.where(i > 0, ch_ref[...].astype(F32) * uh_ref[...].astype(F32), 0.0)
        cat = jnp.concatenate([vh, v], axis=0)
        v1, v2 = _shift_down(cat, 1), _shift_down(cat, 2)
        dconv = dgv * b

        @pl.when(p == 0)
        def _():
            o_ref[...] = (dgv * (w[2:3] * v + w[1:2] * v1 + w[0:1] * v2)).astype(BF16)

            @pl.when(i == 0)
            def _():
                dw_ref[...] = jnp.zeros_like(dw_ref)

            taps = [jnp.sum(dconv * t, axis=0, keepdims=True) for t in (v2, v1, v)]
            dw_ref[...] += jnp.concatenate(taps + [jnp.zeros((5, tc), F32)], axis=0)

        @pl.when(p > 0)
        def _():
            nxt = jnp.where(i < nt - 1, dgn_ref[...].astype(F32) * bn_ref[...].astype(F32), 0.0)
            cat2 = jnp.concatenate([dconv, nxt], axis=0)
            dv = w[2:3] * dconv + w[1:2] * _shift_up(cat2, 1) + w[0:1] * _shift_up(cat2, 2)
            o_ref[...] = jnp.where(p == 1, dv * u, dv * c).astype(BF16)

    def part(q):
        return pl.BlockSpec((tr, tc), lambda j, i, p: (i, q * nd + j))

    def before(q):
        return pl.BlockSpec((HALO, tc), lambda j, i, p: (jnp.maximum(i * hb - 1, 0), q * nd + j))

    def after(q):
        return pl.BlockSpec((HALO, tc), lambda j, i, p: (jnp.minimum((i + 1) * hb, m // HALO - 1), q * nd + j))

    return pl.pallas_call(
        body, name="conv_bwd", grid=(nd, nt, 3),
        out_shape=(jax.ShapeDtypeStruct((m, d3), BF16), jax.ShapeDtypeStruct((8, d), F32)),
        in_specs=[part(0), part(1), part(2), before(1), before(2), after(0),
                  pl.BlockSpec((tr, tc), lambda j, i, p: (i, j)), after(0),
                  pl.BlockSpec((8, tc), lambda j, i, p: (0, j))],
        out_specs=(pl.BlockSpec((tr, tc), lambda j, i, p: (i, p * nd + j)),
                   pl.BlockSpec((8, tc), lambda j, i, p: (0, j))),
        compiler_params=_params(("parallel", "arbitrary", "arbitrary")),
    )(bcu, bcu, bcu, bcu, bcu, bcu, dg, dg, cw)


def _rope_tables(m):
    pad = ROW0 - N_META
    pos = jnp.arange(m, dtype=F32) - pad
    inv = ROPE_THETA ** (-jnp.arange(0, HEAD_DIM, 2, dtype=F32) / HEAD_DIM)
    ang = pos[:, None] * inv[None, :]
    return jnp.cos(ang), jnp.sin(ang)


def _rope(x, c, s):
    half = HEAD_DIM // 2
    x1, x2 = x[:, :half], x[:, half:]
    return jnp.concatenate([x1 * c - x2 * s, x2 * c + x1 * s], axis=-1)


def _rope_t(y, c, s):
    half = HEAD_DIM // 2
    y1, y2 = y[:, :half], y[:, half:]
    return jnp.concatenate([y1 * c + y2 * s, y2 * c - y1 * s], axis=-1)


def _qkv_split(qkv, cos, sin):
    m = qkv.shape[0]
    nb, grp = m // BLOCK, N_Q_HEADS // N_KV_HEADS
    scale = HEAD_DIM ** -0.5

    def body(x_ref, c_ref, s_ref, q_ref, k_ref, v_ref):
        c, s = c_ref[...], s_ref[...]
        for h in range(N_KV_HEADS):
            for g in range(grp):
                col = (h * grp + g) * HEAD_DIM
                xq = x_ref[:, col:col + HEAD_DIM].astype(F32)
                q_ref[h, g * BLOCK:(g + 1) * BLOCK, :] = (_rope(xq, c, s) * scale).astype(BF16)
            col = (N_Q_HEADS + h) * HEAD_DIM
            k_ref[h] = _rope(x_ref[:, col:col + HEAD_DIM].astype(F32), c, s).astype(BF16)
            col = (N_Q_HEADS + N_KV_HEADS + h) * HEAD_DIM
            v_ref[h] = x_ref[:, col:col + HEAD_DIM]

    tab = pl.BlockSpec((BLOCK, HEAD_DIM // 2), lambda i: (i, 0))
    kv = pl.BlockSpec((N_KV_HEADS, BLOCK, HEAD_DIM), lambda i: (0, i, 0))
    return pl.pallas_call(
        body, name="qkv_split", grid=(nb,),
        out_shape=(jax.ShapeDtypeStruct((N_KV_HEADS, nb * grp * BLOCK, HEAD_DIM), BF16),
                   jax.ShapeDtypeStruct((N_KV_HEADS, m, HEAD_DIM), BF16),
                   jax.ShapeDtypeStruct((N_KV_HEADS, m, HEAD_DIM), BF16)),
        in_specs=[pl.BlockSpec((BLOCK, qkv.shape[1]), lambda i: (i, 0)), tab, tab],
        out_specs=(pl.BlockSpec((N_KV_HEADS, grp * BLOCK, HEAD_DIM), lambda i: (0, i, 0)), kv, kv),
        compiler_params=_params(("parallel",)),
    )(qkv, cos, sin)


def _heads_merge(name, o):
    grp = N_Q_HEADS // N_KV_HEADS
    nb = o.shape[1] // (grp * BLOCK)

    def body(o_ref, x_ref):
        for h in range(N_KV_HEADS):
            for g in range(grp):
                col = (h * grp + g) * HEAD_DIM
                x_ref[:, col:col + HEAD_DIM] = o_ref[h, g * BLOCK:(g + 1) * BLOCK, :]

    return pl.pallas_call(
        body, name=name, grid=(nb,),
        out_shape=jax.ShapeDtypeStruct((nb * BLOCK, N_Q_HEADS * HEAD_DIM), o.dtype),
        in_specs=[pl.BlockSpec((N_KV_HEADS, grp * BLOCK, HEAD_DIM), lambda i: (0, i, 0))],
        out_specs=pl.BlockSpec((BLOCK, N_Q_HEADS * HEAD_DIM), lambda i: (i, 0)),
        compiler_params=_params(("parallel",)),
    )(o)


def _heads_split(name, x):
    grp = N_Q_HEADS // N_KV_HEADS
    nb = x.shape[0] // BLOCK

    def body(x_ref, o_ref):
        for h in range(N_KV_HEADS):
            for g in range(grp):
                col = (h * grp + g) * HEAD_DIM
                o_ref[h, g * BLOCK:(g + 1) * BLOCK, :] = x_ref[:, col:col + HEAD_DIM]

    return pl.pallas_call(
        body, name=name, grid=(nb,),
        out_shape=jax.ShapeDtypeStruct((N_KV_HEADS, nb * grp * BLOCK, HEAD_DIM), x.dtype),
        in_specs=[pl.BlockSpec((BLOCK, N_Q_HEADS * HEAD_DIM), lambda i: (i, 0))],
        out_specs=pl.BlockSpec((N_KV_HEADS, grp * BLOCK, HEAD_DIM), lambda i: (0, i, 0)),
        compiler_params=_params(("parallel",)),
    )(x)


def _attn_mask(i):
    r = lax.broadcasted_iota(jnp.int32, (BLOCK, 2 * BLOCK), 0)
    cidx = lax.broadcasted_iota(jnp.int32, (BLOCK, 2 * BLOCK), 1)
    key = (i - 1) * BLOCK + cidx
    return (cidx > r) & (cidx <= r + BLOCK) & (key >= ROW0 - N_META)


def _attn_exp(allowed, q, kb, vb1, sink):
    rows = q.shape[0]
    grp = rows // BLOCK
    s = lax.dot_general(q, kb, _NT, preferred_element_type=F32)
    s = jnp.where(allowed[None], s.reshape(grp, BLOCK, 2 * BLOCK), NEG_INF).reshape(rows, 2 * BLOCK)
    mx = jnp.maximum(jnp.max(s, axis=-1, keepdims=True), sink)
    eb = jnp.exp(s - mx).astype(BF16)
    es = jnp.exp(sink - mx)
    ov = jnp.dot(eb, vb1, preferred_element_type=F32)
    inv = 1.0 / (ov[:, HEAD_DIM:HEAD_DIM + 1] + es)
    return eb, ov[:, :HEAD_DIM], inv, es


def _band(prev_ref, cur_ref, h):
    return jnp.concatenate([prev_ref[h], cur_ref[h]], axis=0)


def _with_ones(vb):
    return jnp.concatenate([vb, jnp.ones_like(vb)], axis=1)


def _attn_specs(grp):
    q = pl.BlockSpec((N_KV_HEADS, grp * BLOCK, HEAD_DIM), lambda i: (0, i, 0))
    cur = pl.BlockSpec((N_KV_HEADS, BLOCK, HEAD_DIM), lambda i: (0, i, 0))
    prev = pl.BlockSpec((N_KV_HEADS, BLOCK, HEAD_DIM), lambda i: (0, jnp.maximum(i - 1, 0), 0))
    sink = pl.BlockSpec((N_KV_HEADS, grp * BLOCK, 1), lambda i: (0, 0, 0))
    return q, cur, prev, sink


def _attn_fwd(q, k, v, sink_rows):
    grp = N_Q_HEADS // N_KV_HEADS
    nb = k.shape[1] // BLOCK

    def body(q_ref, kc_ref, kp_ref, vc_ref, vp_ref, s_ref, o_ref):
        allowed = _attn_mask(pl.program_id(0))
        for h in range(N_KV_HEADS):
            vb1 = _with_ones(_band(vp_ref, vc_ref, h))
            _, ov, inv, _ = _attn_exp(allowed, q_ref[h], _band(kp_ref, kc_ref, h), vb1, s_ref[h])
            o_ref[h] = (ov * inv).astype(BF16)

    qs, cur, prev, sink = _attn_specs(grp)
    return pl.pallas_call(
        body, name="attn_fwd", grid=(nb,), out_shape=jax.ShapeDtypeStruct(q.shape, BF16),
        in_specs=[qs, cur, prev, cur, prev, sink], out_specs=qs,
        compiler_params=_params(("parallel",)),
    )(q, k, k, v, v, sink_rows)


def _attn_bwd(q, k, v, sink_rows, do):
    grp = N_Q_HEADS // N_KV_HEADS
    nb = k.shape[1] // BLOCK

    def body(q_ref, kc_ref, kp_ref, vc_ref, vp_ref, s_ref, do_ref, dq_ref, dk_ref, dv_ref, ds_ref):
        i = pl.program_id(0)
        allowed = _attn_mask(i)

        @pl.when(i == 0)
        def _():
            ds_ref[...] = jnp.zeros_like(ds_ref)

        for h in range(N_KV_HEADS):
            qv, dov = q_ref[h], do_ref[h]
            kb, vb = _band(kp_ref, kc_ref, h), _band(vp_ref, vc_ref, h)
            eb, ov, inv, es = _attn_exp(allowed, qv, kb, _with_ones(vb), s_ref[h])
            dof = dov.astype(F32)
            delta = jnp.sum(dof * (ov * inv), axis=-1, keepdims=True)
            dp = lax.dot_general(dov, vb, _NT, preferred_element_type=F32)
            dsb = (eb.astype(F32) * (inv * (dp - delta))).astype(BF16)
            dq_ref[h] = jnp.dot(dsb, kb, preferred_element_type=F32).astype(BF16)
            dk_ref[h] = lax.dot_general(dsb, qv, _TN, preferred_element_type=F32)
            dv_ref[h] = lax.dot_general(eb, (dof * inv).astype(BF16), _TN, preferred_element_type=F32)
            ds_ref[h] -= (es * inv) * delta

    qs, cur, prev, sink = _attn_specs(grp)
    band = pl.BlockSpec((N_KV_HEADS, None, 2 * BLOCK, HEAD_DIM), lambda i: (0, i, 0, 0))
    band_shape = jax.ShapeDtypeStruct((N_KV_HEADS, nb, 2 * BLOCK, HEAD_DIM), F32)
    return pl.pallas_call(
        body, name="attn_bwd", grid=(nb,),
        out_shape=(jax.ShapeDtypeStruct(q.shape, BF16), band_shape, band_shape,
                   jax.ShapeDtypeStruct(sink_rows.shape, F32)),
        in_specs=[qs, cur, prev, cur, prev, sink, qs], out_specs=(qs, band, band, sink),
        compiler_params=_params(("arbitrary",)),
    )(q, k, k, v, v, sink_rows, do)


def _qkv_merge_bwd(dq, dkb, dvb, cos, sin):
    grp = N_Q_HEADS // N_KV_HEADS
    nb = dkb.shape[1]
    width = (N_Q_HEADS + 2 * N_KV_HEADS) * HEAD_DIM
    scale = HEAD_DIM ** -0.5

    def body(dq_ref, kc_ref, kn_ref, vc_ref, vn_ref, c_ref, s_ref, o_ref):
        last = pl.program_id(0) == nb - 1
        c, s = c_ref[...], s_ref[...]
        for h in range(N_KV_HEADS):
            for g in range(grp):
                col = (h * grp + g) * HEAD_DIM
                y = dq_ref[h, g * BLOCK:(g + 1) * BLOCK, :].astype(F32) * scale
                o_ref[:, col:col + HEAD_DIM] = _rope_t(y, c, s).astype(BF16)
            dk = kc_ref[h, BLOCK:, :] + jnp.where(last, 0.0, kn_ref[h, :BLOCK, :])
            col = (N_Q_HEADS + h) * HEAD_DIM
            o_ref[:, col:col + HEAD_DIM] = _rope_t(dk, c, s).astype(BF16)
            dv = vc_ref[h, BLOCK:, :] + jnp.where(last, 0.0, vn_ref[h, :BLOCK, :])
            col = (N_Q_HEADS + N_KV_HEADS + h) * HEAD_DIM
            o_ref[:, col:col + HEAD_DIM] = dv.astype(BF16)

    tab = pl.BlockSpec((BLOCK, HEAD_DIM // 2), lambda i: (i, 0))
    cur = pl.BlockSpec((N_KV_HEADS, None, 2 * BLOCK, HEAD_DIM), lambda i: (0, i, 0, 0))
    nxt = pl.BlockSpec((N_KV_HEADS, None, 2 * BLOCK, HEAD_DIM), lambda i: (0, jnp.minimum(i + 1, nb - 1), 0, 0))
    return pl.pallas_call(
        body, name="qkv_merge_bwd", grid=(nb,), out_shape=jax.ShapeDtypeStruct((nb * BLOCK, width), BF16),
        in_specs=[pl.BlockSpec((N_KV_HEADS, grp * BLOCK, HEAD_DIM), lambda i: (0, i, 0)),
                  cur, nxt, cur, nxt, tab, tab],
        out_specs=pl.BlockSpec((BLOCK, width), lambda i: (i, 0)),
        compiler_params=_params(("parallel",)),
    )(dq, dkb, dkb, dvb, dvb, cos, sin)


def _tiles2d(r, c):
    tc = _pick(c, 2048, 128) if c % 128 == 0 else c
    tr = _pick(r, max(8, (1 << 20) // tc // 8 * 8), 8) if r % 8 == 0 else r
    return tr, tc


def _cast_bf16(name, w, place):
    r, c = w.shape
    tr, tc = _tiles2d(r, c)
    if tr % 16:
        tr = r

    def body(place_ref, w_ref, o_ref):
        o_ref[...] = w_ref[...].astype(BF16)

    return pl.pallas_call(
        body, name=name, out_shape=jax.ShapeDtypeStruct((N_CHIPS, r, c), BF16),
        grid_spec=pltpu.PrefetchScalarGridSpec(
            num_scalar_prefetch=1, grid=(r // tr, c // tc),
            in_specs=[pl.BlockSpec((tr, tc), lambda i, j, p: (i, j))],
            out_specs=pl.BlockSpec((None, tr, tc), lambda i, j, p: (p[1], i, j))),
        compiler_params=_params(("parallel", "parallel")),
    )(place, w)


def _pair_sum(name, g, got, place):
    n, r, c = g.shape
    half = r // 2
    tr, tc = _tiles2d(half, c)
    nh = half // tr

    def body(place_ref, g_ref, got_ref, o_ref, own_ref):
        s = (g_ref[...] + got_ref[...]).astype(BF16)
        o_ref[...] = s

        @pl.when(pl.program_id(2) == place_ref[1])
        def _():
            own_ref[...] = s

    tile = pl.BlockSpec((None, tr, tc), lambda i, j, k, p: (k, i, j))
    shape = jax.ShapeDtypeStruct((n, half, c), BF16)
    return pl.pallas_call(
        body, name=name, out_shape=(shape, shape),
        grid_spec=pltpu.PrefetchScalarGridSpec(
            num_scalar_prefetch=1, grid=(nh, c // tc, n),
            in_specs=[pl.BlockSpec((None, tr, tc), lambda i, j, k, p: (k, p[0] * nh + i, j)), tile],
            out_specs=(tile, pl.BlockSpec((None, tr, tc), lambda i, j, k, p: (p[1], i, j)))),
        compiler_params=_params(("parallel", "parallel", "arbitrary")),
    )(place, g, got)


def _chip_sum(name, parts, place):
    n, half, c = parts.shape
    tr, tc = _tiles2d(half, c)
    nh = half // tr

    def body(place_ref, p0, p1, p2, p3, o_ref):
        o_ref[...] = ((p0[...].astype(F32) + p1[...].astype(F32)) + p2[...].astype(F32)) + p3[...].astype(F32)

    def chip(k):
        return pl.BlockSpec((None, tr, tc), lambda i, j, p: (k, i, j))

    return pl.pallas_call(
        body, name=name, out_shape=jax.ShapeDtypeStruct((2 * half, c), F32),
        grid_spec=pltpu.PrefetchScalarGridSpec(
            num_scalar_prefetch=1, grid=(nh, c // tc),
            in_specs=[chip(k) for k in range(n)],
            out_specs=pl.BlockSpec((tr, tc), lambda i, j, p: (p[0] * nh + i, j))),
        compiler_params=_params(("parallel", "parallel")),
    )(place, parts, parts, parts, parts)


def _dev_sum(gathered):
    def body(g_ref, o_ref):
        acc = g_ref[0]
        for k in range(1, N_DEV):
            acc = acc + g_ref[k]
        o_ref[...] = acc

    return pl.pallas_call(body, name="dev_sum", out_shape=jax.ShapeDtypeStruct(gathered.shape[1:], F32))(gathered)


def _adamw(name, w, g, m, v):
    r, c = w.shape
    tr, tc = _tiles2d(r, c)
    if r % 8 == 0:
        tr = _pick(r, max(8, (1 << 18) // tc // 8 * 8), 8)

    def body(w_ref, g_ref, m_ref, v_ref, d_ref, mo_ref, vo_ref):
        gv = g_ref[...]
        mn = ADAM_B1 * m_ref[...] + (1.0 - ADAM_B1) * gv
        vn = ADAM_B2 * v_ref[...] + (1.0 - ADAM_B2) * jnp.square(gv)
        m_hat = mn / (1.0 - ADAM_B1 ** ADAM_STEP)
        v_hat = vn / (1.0 - ADAM_B2 ** ADAM_STEP)
        d_ref[...] = -ADAM_LR * (m_hat / (jnp.sqrt(v_hat) + ADAM_EPS) + ADAM_WD * w_ref[...])
        mo_ref[...] = mn
        vo_ref[...] = vn

    tile = pl.BlockSpec((tr, tc), lambda i, j: (i, j))
    shape = jax.ShapeDtypeStruct((r, c), F32)
    return pl.pallas_call(
        body, name=name, grid=(r // tr, c // tc), out_shape=(shape, shape, shape),
        in_specs=[tile] * 4, out_specs=(tile,) * 3, compiler_params=_params(("parallel", "parallel")),
    )(w, g, m, v)


HBM_SPEC = pl.BlockSpec(memory_space=pltpu.HBM)


def _place():
    x, y, c = lax.axis_index("x"), lax.axis_index("y"), lax.axis_index("c")
    chips = [(1 - x, y), (x, 1 - y), (1 - x, 1 - y)]
    return x, y, c, 2 * x + y, chips


def _all_gather(stacks):
    n = len(stacks)
    halves = [s.shape[1] // 2 for s in stacks]

    def body(*refs):
        dst = refs[n:2 * n]
        send_a, recv_a, send_b, recv_b = refs[2 * n:]
        x, y, c, me, chips = _place()
        sibling = (x, y, 1 - c)

        def half(w, chip, hc):
            return dst[w].at[chip, pl.ds(hc * halves[w], halves[w]), :]

        started = []
        for j, (cx, cy) in enumerate(chips):
            for w in range(n):
                cp = pltpu.make_async_remote_copy(
                    src_ref=half(w, me, c), dst_ref=half(w, me, c),
                    send_sem=send_a.at[j * n + w], recv_sem=recv_a.at[j * n + w],
                    device_id=(cx, cy, c), device_id_type=MESH)
                cp.start()
                started.append(cp)
        for j, (cx, cy) in enumerate(chips):
            for w in range(n):
                got = half(w, 2 * cx + cy, c)
                pltpu.make_async_remote_copy(
                    src_ref=got, dst_ref=got, send_sem=send_a.at[j * n + w], recv_sem=recv_a.at[j * n + w],
                    device_id=(cx, cy, c), device_id_type=MESH).wait_recv()
                cp = pltpu.make_async_remote_copy(
                    src_ref=got, dst_ref=got, send_sem=send_b.at[j * n + w], recv_sem=recv_b.at[j * n + w],
                    device_id=sibling, device_id_type=MESH)
                cp.start()
                started.append(cp)
        for j, (cx, cy) in enumerate(chips):
            for w in range(n):
                got = half(w, 2 * cx + cy, 1 - c)
                pltpu.make_async_remote_copy(
                    src_ref=got, dst_ref=got, send_sem=send_b.at[j * n + w], recv_sem=recv_b.at[j * n + w],
                    device_id=sibling, device_id_type=MESH).wait_recv()
        for cp in started:
            cp.wait_send()

    return pl.pallas_call(
        body, name="all_gather_weights",
        out_shape=[jax.ShapeDtypeStruct(s.shape, s.dtype) for s in stacks],
        in_specs=[HBM_SPEC] * n, out_specs=[HBM_SPEC] * n,
        input_output_aliases={w: w for w in range(n)},
        scratch_shapes=[pltpu.SemaphoreType.DMA((3 * n,))] * 4,
    )(*stacks)


def _pair_exchange(grads, small):
    n = len(grads)
    halves = [g.shape[1] // 2 for g in grads]

    def body(*refs):
        src, small_ref = refs[:n], refs[n]
        dst, all_ref = refs[n + 1:2 * n + 1], refs[2 * n + 1]
        send, recv, ssend, srecv, local_sem = refs[2 * n + 2:]
        x, y, c, me, _ = _place()
        sibling = (x, y, 1 - c)
        dev = 2 * me + c
        mine = pltpu.make_async_copy(small_ref, all_ref.at[dev], local_sem)
        mine.start()
        started = []
        for w in range(n):
            cp = pltpu.make_async_remote_copy(
                src_ref=src[w].at[:, pl.ds((1 - c) * halves[w], halves[w]), :], dst_ref=dst[w],
                send_sem=send.at[w], recv_sem=recv.at[w], device_id=sibling, device_id_type=MESH)
            cp.start()
            started.append(cp)
        for t in range(1, N_DEV):
            tx, ty, tc = (t >> 2) & 1, (t >> 1) & 1, t & 1
            cp = pltpu.make_async_remote_copy(
                src_ref=small_ref, dst_ref=all_ref.at[dev], send_sem=ssend.at[t], recv_sem=srecv.at[t],
                device_id=(x ^ tx, y ^ ty, c ^ tc), device_id_type=MESH)
            cp.start()
            started.append(cp)
        for w in range(n):
            pltpu.make_async_remote_copy(
                src_ref=dst[w], dst_ref=dst[w], send_sem=send.at[w], recv_sem=recv.at[w],
                device_id=sibling, device_id_type=MESH).wait_recv()
        for t in range(1, N_DEV):
            tx, ty, tc = (t >> 2) & 1, (t >> 1) & 1, t & 1
            peer = 4 * (x ^ tx) + 2 * (y ^ ty) + (c ^ tc)
            pltpu.make_async_remote_copy(
                src_ref=all_ref.at[peer], dst_ref=all_ref.at[peer], send_sem=ssend.at[t], recv_sem=srecv.at[t],
                device_id=(x ^ tx, y ^ ty, c ^ tc), device_id_type=MESH).wait_recv()
        for cp in started:
            cp.wait_send()
        mine.wait()

    out_shape = [jax.ShapeDtypeStruct((N_CHIPS, h, g.shape[2]), F32) for g, h in zip(grads, halves)]
    out_shape.append(jax.ShapeDtypeStruct((N_DEV,) + small.shape, F32))
    return pl.pallas_call(
        body, name="pair_exchange", out_shape=out_shape,
        in_specs=[HBM_SPEC] * (n + 1), out_specs=[HBM_SPEC] * (n + 1),
        scratch_shapes=[pltpu.SemaphoreType.DMA((n,)), pltpu.SemaphoreType.DMA((n,)),
                        pltpu.SemaphoreType.DMA((N_DEV,)), pltpu.SemaphoreType.DMA((N_DEV,)),
                        pltpu.SemaphoreType.DMA],
    )(*grads, small)


def _chip_exchange(partials, landing):
    n = len(partials)

    def body(*refs):
        src, dst = refs[:n], refs[2 * n:3 * n]
        send, recv = refs[3 * n:]
        x, y, c, me, chips = _place()
        started = []
        for j, (cx, cy) in enumerate(chips):
            for w in range(n):
                cp = pltpu.make_async_remote_copy(
                    src_ref=src[w].at[2 * cx + cy], dst_ref=dst[w].at[me],
                    send_sem=send.at[j * n + w], recv_sem=recv.at[j * n + w],
                    device_id=(cx, cy, c), device_id_type=MESH)
                cp.start()
                started.append(cp)
        for j, (cx, cy) in enumerate(chips):
            for w in range(n):
                got = dst[w].at[2 * cx + cy]
                pltpu.make_async_remote_copy(
                    src_ref=got, dst_ref=got, send_sem=send.at[j * n + w], recv_sem=recv.at[j * n + w],
                    device_id=(cx, cy, c), device_id_type=MESH).wait_recv()
        for cp in started:
            cp.wait_send()

    return pl.pallas_call(
        body, name="chip_exchange", out_shape=[jax.ShapeDtypeStruct(p.shape, p.dtype) for p in landing],
        in_specs=[HBM_SPEC] * (2 * n), out_specs=[HBM_SPEC] * n,
        input_output_aliases={n + w: w for w in range(n)},
        scratch_shapes=[pltpu.SemaphoreType.DMA((3 * n,))] * 2,
    )(*partials, *landing)


def _pair_share(blocks):
    n = len(blocks)

    def body(*refs):
        dst = refs[n:2 * n]
        send, recv = refs[2 * n:]
        x, y, c, _, _ = _place()
        sibling = (x, y, 1 - c)

        def half(w, hc):
            h = blocks[w].shape[0] // 2
            return dst[w].at[pl.ds(hc * h, h), :]

        sent = [pltpu.make_async_remote_copy(src_ref=half(w, c), dst_ref=half(w, c), send_sem=send.at[w],
                                             recv_sem=recv.at[w], device_id=sibling, device_id_type=MESH)
                for w in range(n)]
        for cp in sent:
            cp.start()
        for w in range(n):
            got = half(w, 1 - c)
            pltpu.make_async_remote_copy(src_ref=got, dst_ref=got, send_sem=send.at[w], recv_sem=recv.at[w],
                                         device_id=sibling, device_id_type=MESH).wait_recv()
        for cp in sent:
            cp.wait_send()

    return pl.pallas_call(
        body, name="pair_share", out_shape=[jax.ShapeDtypeStruct(b.shape, F32) for b in blocks],
        in_specs=[HBM_SPEC] * n, out_specs=[HBM_SPEC] * n,
        input_output_aliases={w: w for w in range(n)},
        scratch_shapes=[pltpu.SemaphoreType.DMA((n,))] * 2,
    )(*blocks)


MATRICES = ("w_in_conv", "w_out_conv", "w_up_0", "w_down_0", "w_qkv", "w_o", "w_up_1", "w_down_1")
COLUMN_SHARDED = ("w_in_conv", "w_up_0", "w_qkv", "w_up_1")
NORMS = ("norm_mix_0", "norm_mlp_0", "norm_mix_1", "norm_mlp_1", "norm_final")


def _mlp_fwd(tag, h, g, w_up, w_down):
    n = _rms_fwd("norm_mlp_" + tag, h, g)
    z = _mm_nn_cols("mlp_up_" + tag, n, w_up, BF16)
    out = _mm_nn_rows("mlp_down_" + tag, z, w_down, F32, a_pro=_relu_sq,
                      epi=lambda acc, res: acc + res, extras=(h,))
    return out, (h, n, z)


def _mlp_bwd(tag, saved, g, w_up, w_down, dh, dh_bf):
    h, n, z = saved
    dz = _mm_nt_rows("mlp_down_dx_" + tag, dh_bf, w_down, BF16,
                     epi=lambda acc, zt: acc * (2.0 * jnp.maximum(zt.astype(F32), 0.0)), extras=(z,))
    dw_down = _mm_tn("mlp_down_dw_" + tag, z, dh_bf, stacked=False, a_pro=_relu_sq)
    dw_up = _mm_tn("mlp_up_dw_" + tag, n, dz, stacked=True)
    dn = _mm_nt_cols("mlp_up_dx_" + tag, dz, w_up, F32)
    dh, dh_bf, dg = _rms_bwd("norm_mlp_bwd_" + tag, dn, h, g, dh)
    return dh, dh_bf, dg, dw_up, dw_down


def _local_step(h0, target, w, norms, sinks, conv_w8):
    m = h0.shape[0]
    grp = N_Q_HEADS // N_KV_HEADS
    cos, sin = _rope_tables(m)
    sink_rows = jnp.repeat(sinks.astype(F32), BLOCK).reshape(N_KV_HEADS, grp * BLOCK, 1)

    n0 = _rms_fwd("norm_mix_0", h0, norms["norm_mix_0"])
    bcu = _mm_nn_cols("conv_in", n0, w["w_in_conv"], BF16)
    gate = _conv_fwd(bcu, conv_w8)
    h1 = _mm_nn_rows("conv_out", gate, w["w_out_conv"], F32, epi=lambda acc, res: acc + res, extras=(h0,))
    h2, mlp0 = _mlp_fwd("0", h1, norms["norm_mlp_0"], w["w_up_0"], w["w_down_0"])
    n2 = _rms_fwd("norm_mix_1", h2, norms["norm_mix_1"])
    qkv = _mm_nn_cols("attn_qkv", n2, w["w_qkv"], BF16)
    q, k, v = _qkv_split(qkv, cos, sin)
    o = _heads_merge("attn_o_merge", _attn_fwd(q, k, v, sink_rows))
    h3 = _mm_nn_rows("attn_out", o, w["w_o"], F32, epi=lambda acc, res: acc + res, extras=(h2,))
    h4, mlp1 = _mlp_fwd("1", h3, norms["norm_mlp_1"], w["w_up_1"], w["w_down_1"])

    grads = {}
    loss, dh, dh_bf, grads["norm_final"] = _loss_head(h4, norms["norm_final"], target)
    dh, dh_bf, grads["norm_mlp_1"], grads["w_up_1"], grads["w_down_1"] = _mlp_bwd(
        "1", mlp1, norms["norm_mlp_1"], w["w_up_1"], w["w_down_1"], dh, dh_bf)
    do = _mm_nt_rows("attn_out_dx", dh_bf, w["w_o"], BF16)
    grads["w_o"] = _mm_tn("attn_out_dw", o, dh_bf, stacked=False)
    dq, dkb, dvb, dsink = _attn_bwd(q, k, v, sink_rows, _heads_split("attn_do_split", do))
    grads["attn_sinks"] = dsink
    dqkv = _qkv_merge_bwd(dq, dkb, dvb, cos, sin)
    grads["w_qkv"] = _mm_tn("attn_qkv_dw", n2, dqkv, stacked=True)
    dn = _mm_nt_cols("attn_qkv_dx", dqkv, w["w_qkv"], F32)
    dh, dh_bf, grads["norm_mix_1"] = _rms_bwd("norm_mix_bwd_1", dn, h2, norms["norm_mix_1"], dh)
    dh, dh_bf, grads["norm_mlp_0"], grads["w_up_0"], grads["w_down_0"] = _mlp_bwd(
        "0", mlp0, norms["norm_mlp_0"], w["w_up_0"], w["w_down_0"], dh, dh_bf)
    dgate = _mm_nt_rows("conv_out_dx", dh_bf, w["w_out_conv"], BF16)
    grads["w_out_conv"] = _mm_tn("conv_out_dw", gate, dh_bf, stacked=False)
    dbcu, grads["conv_w"] = _conv_bwd(bcu, conv_w8, dgate)
    grads["w_in_conv"] = _mm_tn("conv_in_dw", n0, dbcu, stacked=True)
    dn = _mm_nt_cols("conv_in_dx", dbcu, w["w_in_conv"], F32)
    dh, _, grads["norm_mix_0"] = _rms_bwd("norm_mix_bwd_0", dn, h0, norms["norm_mix_0"], dh)
    return loss, dh, grads


def kernel(x, meta_tokens, norm_mix_0, w_in_conv, conv_w, w_out_conv, norm_mlp_0, w_up_0, w_down_0, norm_mix_1, w_qkv, attn_sinks, w_o, norm_mlp_1, w_up_1, w_down_1, norm_final, loss_target, m_meta_tokens, m_norm_mix_0, m_w_in_conv, m_conv_w, m_w_out_conv, m_norm_mlp_0, m_w_up_0, m_w_down_0, m_norm_mix_1, m_w_qkv, m_attn_sinks, m_w_o, m_norm_mlp_1, m_w_up_1, m_w_down_1, m_norm_final, v_meta_tokens, v_norm_mix_0, v_w_in_conv, v_conv_w, v_w_out_conv, v_norm_mlp_0, v_w_up_0, v_w_down_0, v_norm_mix_1, v_w_qkv, v_attn_sinks, v_w_o, v_norm_mlp_1, v_w_up_1, v_w_down_1, v_norm_final):
    given = dict(locals())
    names = ("meta_tokens", "norm_mix_0", "w_in_conv", "conv_w", "w_out_conv", "norm_mlp_0", "w_up_0", "w_down_0",
             "norm_mix_1", "w_qkv", "attn_sinks", "w_o", "norm_mlp_1", "w_up_1", "w_down_1", "norm_final")
    d = D_MODEL
    dc = d // N_CHIPS
    chip = 2 * lax.axis_index("x") + lax.axis_index("y")
    place = jnp.stack([lax.axis_index("c"), chip]).astype(jnp.int32)

    small = jnp.zeros((SMALL_ROWS, dc), F32).at[:N_META].set(meta_tokens).at[N_META:N_META + CONV_WIDTH].set(conv_w)
    small = lax.dynamic_update_slice(jnp.zeros((N_CHIPS, SMALL_ROWS, dc), F32), small[None], (chip, 0, 0))
    stacks = [_cast_bf16("cast_" + n, given[n], place) for n in MATRICES] + [small]
    gathered = _all_gather(stacks)
    w = {}
    for n, g in zip(MATRICES, gathered):
        w[n] = g if n in COLUMN_SHARDED else g.reshape(N_CHIPS * g.shape[1], g.shape[2])
    small_all = gathered[-1]
    small_full = jnp.transpose(small_all, (1, 0, 2)).reshape(SMALL_ROWS, d)
    meta_full = small_full[:N_META]
    conv_w8 = small_full[N_META:N_META + 8]

    pad = ROW0 - N_META
    h0 = jnp.concatenate([jnp.zeros((pad, d), F32), meta_full, x[0]], axis=0)
    norms = {n: given[n] for n in NORMS}
    loss_part, dh0, grads = _local_step(h0, loss_target[0], w, norms, attn_sinks, conv_w8)
    loss = lax.psum(loss_part[0, 0], ("x", "y", "c"))
    grad_x = dh0[ROW0:][None]

    grp = N_Q_HEADS // N_KV_HEADS
    small_grad = jnp.zeros((SMALL_ROWS, d), F32).at[:N_META].set(dh0[pad:ROW0]).at[N_META:N_META + 8].set(grads["conv_w"])
    small_grad = jnp.transpose(small_grad.reshape(SMALL_ROWS, N_CHIPS, dc), (1, 0, 2))
    full = [grads[n] if n in COLUMN_SHARDED else grads[n].reshape(N_CHIPS, -1, grads[n].shape[1]) for n in MATRICES]
    full.append(small_grad)
    dsink = jnp.sum(grads["attn_sinks"].reshape(N_Q_HEADS, BLOCK), axis=1)
    rep = jnp.zeros((8, d), F32)
    for r, n in enumerate(NORMS):
        rep = rep.at[r].set(jnp.sum(grads[n], axis=0))
    rep = rep.at[len(NORMS), :N_Q_HEADS].set(dsink)

    ex = _pair_exchange(full, rep)
    got, rep_all = ex[:-1], ex[-1]
    summed = [_pair_sum("pair_sum_%d" % i, g, o, place) for i, (g, o) in enumerate(zip(full, got))]
    landed = _chip_exchange([s[0] for s in summed], [s[1] for s in summed])
    reduced = _pair_share([_chip_sum("chip_sum_%d" % i, p, place) for i, p in enumerate(landed)])
    rep_sum = _dev_sum(rep_all)

    g_out = dict(zip(MATRICES, reduced[:-1]))
    g_out["meta_tokens"] = reduced[-1][:N_META]
    g_out["conv_w"] = reduced[-1][N_META:N_META + CONV_WIDTH]
    for r, n in enumerate(NORMS):
        g_out[n] = rep_sum[r]
    g_out["attn_sinks"] = rep_sum[len(NORMS), :N_Q_HEADS]

    delta, new_m, new_v = {}, {}, {}
    for n in names:
        wt = given[n]
        shape2 = wt.shape if wt.ndim == 2 else (1, wt.shape[0])
        outs = _adamw("adamw_" + n, wt.reshape(shape2), g_out[n].reshape(shape2),
                      given["m_" + n].reshape(shape2), given["v_" + n].reshape(shape2))
        delta[n], new_m[n], new_v[n] = [o.reshape(wt.shape) for o in outs]
    return (loss, grad_x, *[g_out[n] for n in names], *[delta[n] for n in names],
            *[new_m[n] for n in names], *[new_v[n] for n in names])
```

```python
import functools

import jax
import jax.numpy as jnp
from jax import lax
from jax.experimental import pallas as pl
from jax.experimental.pallas import tpu as pltpu

F32 = jnp.float32
BF16 = jnp.bfloat16

D_MODEL = 2048
SEQ = 8192
N_META = 16
CONV_WIDTH = 3
HEAD_DIM = 64
N_Q_HEADS = 32
N_KV_HEADS = 4
BLOCK = 128
ROPE_THETA = 10000.0
D_FF = 4 * D_MODEL
RMS_EPS = 1e-5
NEG_INF = -1e30

ADAM_LR = 0.001
ADAM_B1 = 0.9
ADAM_B2 = 0.999
ADAM_EPS = 1e-08
ADAM_WD = 0.01
ADAM_STEP = 10

N_CHIPS = 4
N_DEV = 8
MESH = pl.DeviceIdType.MESH
VMEM_LIMIT = 56 * 1024 * 1024
SMALL_ROWS = 32
ROW0 = BLOCK


def _pick(n, target, mult):
    best = None
    for t in range(mult, min(n, target) + 1, mult):
        if n % t == 0:
            best = t
    assert best is not None, (n, target, mult)
    return best


def _params(sem=None):
    return pltpu.CompilerParams(dimension_semantics=sem, vmem_limit_bytes=VMEM_LIMIT)


HBM_SPEC = pl.BlockSpec(memory_space=pltpu.HBM)


class _Task:
    def __init__(self, inputs, outputs, aliases, sem_shapes, bind):
        self.inputs, self.outputs, self.aliases = list(inputs), list(outputs), dict(aliases)
        self.sem_shapes, self.bind = list(sem_shapes), bind


def _like(arrays):
    return [jax.ShapeDtypeStruct(a.shape, a.dtype) for a in arrays]


def _bind_tasks(tasks, in_refs, out_refs, sem_refs):
    bound, i, o, s = [], 0, 0, 0
    for t in tasks:
        ni, no, ns = len(t.inputs), len(t.outputs), len(t.sem_shapes)
        bound.append(t.bind(in_refs[i:i + ni], out_refs[o:o + no], sem_refs[s:s + ns]))
        i, o, s = i + ni, o + no, s + ns
    return bound


def _run_phase(bound, phase):
    for b in bound:
        if b[phase] is not None:
            b[phase]()


def _task_plumbing(tasks, in_offset, out_offset):
    ins = [a for t in tasks for a in t.inputs]
    outs = [o for t in tasks for o in t.outputs]
    sems = [s for t in tasks for s in t.sem_shapes]
    aliases, i, o = {}, in_offset, out_offset
    for t in tasks:
        for src, dst in t.aliases.items():
            aliases[i + src] = o + dst
        i, o = i + len(t.inputs), o + len(t.outputs)
    return ins, outs, sems, aliases


def _split_outputs(tasks, flat):
    res, o = [], 0
    for t in tasks:
        res.append(list(flat[o:o + len(t.outputs)]))
        o += len(t.outputs)
    return res


def _run(name, tasks):
    ins, outs, sems, aliases = _task_plumbing(tasks, 0, 0)

    def body(*refs):
        bound = _bind_tasks(tasks, refs[:len(ins)], refs[len(ins):len(ins) + len(outs)], refs[len(ins) + len(outs):])
        for phase in range(3):
            _run_phase(bound, phase)

    flat = pl.pallas_call(
        body, name=name, out_shape=outs, in_specs=[HBM_SPEC] * len(ins), out_specs=[HBM_SPEC] * len(outs),
        input_output_aliases=aliases, scratch_shapes=sems,
    )(*ins)
    return _split_outputs(tasks, flat)


def _place():
    x, y, c = lax.axis_index("x"), lax.axis_index("y"), lax.axis_index("c")
    chips = [(1 - x, y), (x, 1 - y), (1 - x, 1 - y)]
    return x, y, c, 2 * x + y, chips


def _gather_task(stacks):
    n = len(stacks)
    halves = [s.shape[1] // 2 for s in stacks]

    def bind(_, dst, sems):
        send_a, recv_a, send_b, recv_b = sems
        x, y, c, me, chips = _place()
        sibling = (x, y, 1 - c)

        def half(w, chip, hc):
            return dst[w].at[chip, pl.ds(hc * halves[w], halves[w]), :]

        def over_ici(j, w, block):
            return pltpu.make_async_remote_copy(
                src_ref=half(w, block, c), dst_ref=half(w, block, c), send_sem=send_a.at[j * n + w],
                recv_sem=recv_a.at[j * n + w], device_id=(*chips[j], c), device_id_type=MESH)

        def over_d2d(j, w, hc):
            got = half(w, 2 * chips[j][0] + chips[j][1], hc)
            return pltpu.make_async_remote_copy(
                src_ref=got, dst_ref=got, send_sem=send_b.at[j * n + w], recv_sem=recv_b.at[j * n + w],
                device_id=sibling, device_id_type=MESH)

        pairs = [(j, w) for j in range(3) for w in range(n)]

        def start():
            for j, w in pairs:
                over_ici(j, w, me).start()

        def mid():
            for j, w in pairs:
                over_ici(j, w, 2 * chips[j][0] + chips[j][1]).wait_recv()
                over_d2d(j, w, c).start()

        def finish():
            for j, w in pairs:
                over_d2d(j, w, 1 - c).wait_recv()
            for j, w in pairs:
                over_ici(j, w, me).wait_send()
                over_d2d(j, w, c).wait_send()

        return start, mid, finish

    return _Task(stacks, _like(stacks), {w: w for w in range(n)}, [pltpu.SemaphoreType.DMA((3 * n,))] * 4, bind)


def _pair_exchange_task(grads, small=None):
    n = len(grads)
    halves = [g.shape[1] // 2 for g in grads]

    def bind(src, dst, sems):
        send, recv = sems[0], sems[1]
        x, y, c, me, _ = _place()
        sibling = (x, y, 1 - c)
        dev = 2 * me + c

        def to_sibling(w):
            return pltpu.make_async_remote_copy(
                src_ref=src[w].at[:, pl.ds((1 - c) * halves[w], halves[w]), :], dst_ref=dst[w],
                send_sem=send.at[w], recv_sem=recv.at[w], device_id=sibling, device_id_type=MESH)

        def to_peer(t, block):
            tx, ty, tc = (t >> 2) & 1, (t >> 1) & 1, t & 1
            return pltpu.make_async_remote_copy(
                src_ref=src[n], dst_ref=dst[n].at[block], send_sem=sems[2].at[t], recv_sem=sems[3].at[t],
                device_id=(x ^ tx, y ^ ty, c ^ tc), device_id_type=MESH)

        def mine():
            return pltpu.make_async_copy(src[n], dst[n].at[dev], sems[4])

        def start():
            for w in range(n):
                to_sibling(w).start()
            if small is not None:
                mine().start()
                for t in range(1, N_DEV):
                    to_peer(t, dev).start()

        def finish():
            for w in range(n):
                to_sibling(w).wait_recv()
            if small is not None:
                for t in range(1, N_DEV):
                    to_peer(t, dev ^ t).wait_recv()
            for w in range(n):
                to_sibling(w).wait_send()
            if small is not None:
                for t in range(1, N_DEV):
                    to_peer(t, dev).wait_send()
                mine().wait()

        return start, None, finish

    outputs = [jax.ShapeDtypeStruct((N_CHIPS, h, g.shape[2]), g.dtype) for g, h in zip(grads, halves)]
    sem_shapes = [pltpu.SemaphoreType.DMA((n,)), pltpu.SemaphoreType.DMA((n,))]
    inputs = list(grads)
    if small is not None:
        inputs.append(small)
        outputs.append(jax.ShapeDtypeStruct((N_DEV,) + small.shape, small.dtype))
        sem_shapes += [pltpu.SemaphoreType.DMA((N_DEV,)), pltpu.SemaphoreType.DMA((N_DEV,)), pltpu.SemaphoreType.DMA(())]
    return _Task(inputs, outputs, {}, sem_shapes, bind)


def _chip_exchange_task(summed):
    n = len(summed)

    def bind(refs, dst, sems):
        src = refs[:n]
        send, recv = sems
        x, y, c, me, chips = _place()

        def copy(j, w, block_from, block_to):
            return pltpu.make_async_remote_copy(
                src_ref=src[w].at[block_from], dst_ref=dst[w].at[block_to], send_sem=send.at[j * n + w],
                recv_sem=recv.at[j * n + w], device_id=(*chips[j], c), device_id_type=MESH)

        pairs = [(j, w) for j in range(3) for w in range(n)]

        def start():
            for j, w in pairs:
                copy(j, w, 2 * chips[j][0] + chips[j][1], me).start()

        def finish():
            for j, w in pairs:
                copy(j, w, me, 2 * chips[j][0] + chips[j][1]).wait_recv()
            for j, w in pairs:
                copy(j, w, 2 * chips[j][0] + chips[j][1], me).wait_send()

        return start, None, finish

    partials, landing = [s[0] for s in summed], [s[1] for s in summed]
    return _Task(partials + landing, _like(landing), {n + w: w for w in range(n)},
                 [pltpu.SemaphoreType.DMA((3 * n,))] * 2, bind)


def _pair_share_task(blocks):
    n = len(blocks)

    def bind(_, dst, sems):
        send, recv = sems
        x, y, c, _, _ = _place()

        def copy(w, hc):
            h = blocks[w].shape[0] // 2
            rows = dst[w].at[pl.ds(hc * h, h), :]
            return pltpu.make_async_remote_copy(src_ref=rows, dst_ref=rows, send_sem=send.at[w], recv_sem=recv.at[w],
                                                device_id=(x, y, 1 - c), device_id_type=MESH)

        def start():
            for w in range(n):
                copy(w, c).start()

        def finish():
            for w in range(n):
                copy(w, 1 - c).wait_recv()
            for w in range(n):
                copy(w, c).wait_send()

        return start, None, finish

    return _Task(blocks, _like(blocks), {w: w for w in range(n)}, [pltpu.SemaphoreType.DMA((n,))] * 2, bind)


def _mm(name, a, b, *, dims, grid, a_spec, b_spec, out_shape, out_spec, nk, acc_shape,
        extras=(), extra_specs=(), a_pro=None, epi=None, tasks=()):
    n_ex = len(extras)
    acc_in_out = epi is None and out_shape.dtype == F32
    t_ins, t_outs, t_sems, aliases = _task_plumbing(tasks, 2 + n_ex, 1)
    n_ti, n_to = len(t_ins), len(t_outs)
    has_acc = not (nk == 1 or acc_in_out)
    total = grid[0] * grid[1] * grid[2]
    mid_step = max(0, total - 1 - max(1, total // 8))

    def body(*refs):
        a_ref, b_ref = refs[0], refs[1]
        ex = refs[2:2 + n_ex]
        o_ref = refs[2 + n_ex + n_ti]
        scratch = refs[3 + n_ex + n_ti + n_to:]
        if tasks:
            bound = _bind_tasks(tasks, refs[2 + n_ex:2 + n_ex + n_ti],
                                refs[3 + n_ex + n_ti:3 + n_ex + n_ti + n_to], scratch[int(has_acc):])
            step = (pl.program_id(0) * grid[1] + pl.program_id(1)) * grid[2] + pl.program_id(2)

            @pl.when(step == 0)
            def _():
                _run_phase(bound, 0)

        av = a_ref[...]
        if a_pro is not None:
            av = a_pro(av)
        part = lax.dot_general(av, b_ref[...], dims, preferred_element_type=F32)

        def finish(acc):
            r = acc if epi is None else epi(acc, *[e[...] for e in ex])
            o_ref[...] = r.astype(o_ref.dtype)

        if nk == 1:
            finish(part)
        else:
            acc_ref = scratch[0] if has_acc else o_ref
            kk = pl.program_id(2)

            @pl.when(kk == 0)
            def _():
                acc_ref[...] = part

            @pl.when(kk > 0)
            def _():
                acc_ref[...] += part

            if has_acc:
                @pl.when(kk == nk - 1)
                def _():
                    finish(acc_ref[...])

        if tasks:
            @pl.when(step == mid_step)
            def _():
                _run_phase(bound, 1)

            @pl.when(step == total - 1)
            def _():
                _run_phase(bound, 2)

    scratch_shapes = ([pltpu.VMEM(acc_shape, F32)] if has_acc else []) + t_sems
    sem = ("arbitrary",) * 3 if tasks else ("parallel", "parallel", "arbitrary")
    res = pl.pallas_call(
        body, name=name, grid=grid, out_shape=[out_shape, *t_outs],
        in_specs=[a_spec, b_spec, *extra_specs, *[HBM_SPEC] * n_ti], out_specs=[out_spec, *[HBM_SPEC] * n_to],
        input_output_aliases=aliases, scratch_shapes=scratch_shapes, compiler_params=_params(sem),
    )(a, b, *extras, *t_ins)
    return (res[0], _split_outputs(tasks, res[1:])) if tasks else res[0]


_NN = (((1,), (0,)), ((), ()))
_NT = (((1,), (1,)), ((), ()))
_TN = (((0,), (0,)), ((), ()))


def _mm_nn_cols(name, a, w, out_dtype, epi=None, extras=(), tasks=()):
    m, k = a.shape
    _, _, ns = w.shape
    tm, tn = _pick(m, 832, 16), _pick(ns, 1024, 128)
    per = ns // tn
    tile = pl.BlockSpec((tm, tn), lambda j, i, kk: (i, j))
    return _mm(name, a, w, dims=_NN, grid=(N_CHIPS * per, m // tm, 1),
               a_spec=pl.BlockSpec((tm, k), lambda j, i, kk: (i, 0)),
               b_spec=pl.BlockSpec((None, k, tn), lambda j, i, kk: (j // per, 0, j % per)),
               out_shape=jax.ShapeDtypeStruct((m, N_CHIPS * ns), out_dtype), out_spec=tile,
               nk=1, acc_shape=(tm, tn), extras=extras, extra_specs=[tile] * len(extras), epi=epi, tasks=tasks)


def _mm_nn_rows(name, a, w, out_dtype, a_pro=None, epi=None, extras=(), tasks=()):
    m, k = a.shape
    _, n = w.shape
    tm, tn, tk = _pick(m, 832, 16), _pick(n, 1024, 128), _pick(k, 2048, 128)
    tile = pl.BlockSpec((tm, tn), lambda j, i, kk: (i, j))
    return _mm(name, a, w, dims=_NN, grid=(n // tn, m // tm, k // tk),
               a_spec=pl.BlockSpec((tm, tk), lambda j, i, kk: (i, kk)),
               b_spec=pl.BlockSpec((tk, tn), lambda j, i, kk: (kk, j)),
               out_shape=jax.ShapeDtypeStruct((m, n), out_dtype), out_spec=tile,
               nk=k // tk, acc_shape=(tm, tn), extras=extras, extra_specs=[tile] * len(extras),
               a_pro=a_pro, epi=epi, tasks=tasks)


def _mm_nt_rows(name, a, w, out_dtype, epi=None, extras=(), tasks=()):
    m, c = a.shape
    r, _ = w.shape
    tm, tn = _pick(m, 832, 16), _pick(r, 1024, 128)
    tile = pl.BlockSpec((tm, tn), lambda j, i, kk: (i, j))
    return _mm(name, a, w, dims=_NT, grid=(r // tn, m // tm, 1),
               a_spec=pl.BlockSpec((tm, c), lambda j, i, kk: (i, 0)),
               b_spec=pl.BlockSpec((tn, c), lambda j, i, kk: (j, 0)),
               out_shape=jax.ShapeDtypeStruct((m, r), out_dtype), out_spec=tile,
               nk=1, acc_shape=(tm, tn), extras=extras, extra_specs=[tile] * len(extras), epi=epi, tasks=tasks)


def _mm_nt_cols(name, a, w, out_dtype, tasks=()):
    m, _ = a.shape
    _, r, ns = w.shape
    tm, tn, tk = _pick(m, 832, 16), _pick(r, 1024, 128), _pick(ns, 2048, 128)
    per = ns // tk
    return _mm(name, a, w, dims=_NT, grid=(r // tn, m // tm, N_CHIPS * per),
               a_spec=pl.BlockSpec((tm, tk), lambda j, i, kk: (i, kk)),
               b_spec=pl.BlockSpec((None, tn, tk), lambda j, i, kk: (kk // per, j, kk % per)),
               out_shape=jax.ShapeDtypeStruct((m, r), out_dtype),
               out_spec=pl.BlockSpec((tm, tn), lambda j, i, kk: (i, j)),
               nk=N_CHIPS * per, acc_shape=(tm, tn), tasks=tasks)


def _mm_tn(name, a, b, stacked, a_pro=None, tasks=()):
    t, ka = a.shape
    _, nb = b.shape
    ns = nb // N_CHIPS if stacked else nb
    tt, ta, tb = _pick(t, 1664, 128), _pick(ka, 2048, 128), _pick(ns, 1024, 128)
    if stacked:
        per = ns // tb
        out_shape = jax.ShapeDtypeStruct((N_CHIPS, ka, ns), F32)
        out_spec = pl.BlockSpec((None, ta, tb), lambda i, j, kk: (j // per, i, j % per))
    else:
        out_shape = jax.ShapeDtypeStruct((ka, nb), F32)
        out_spec = pl.BlockSpec((ta, tb), lambda i, j, kk: (i, j))
    return _mm(name, a, b, dims=_TN, grid=(ka // ta, nb // tb, t // tt),
               a_spec=pl.BlockSpec((tt, ta), lambda i, j, kk: (kk, i)),
               b_spec=pl.BlockSpec((tt, tb), lambda i, j, kk: (kk, j)),
               out_shape=out_shape, out_spec=out_spec, nk=t // tt, acc_shape=(ta, tb), a_pro=a_pro, tasks=tasks)


def _relu_sq(z):
    a = jnp.maximum(z, 0)
    return a * a


def _rms_fwd(name, h, g):
    m, d = h.shape
    tr = _pick(m, 256, 16)

    def body(h_ref, g_ref, o_ref):
        x = h_ref[...]
        rstd = lax.rsqrt(jnp.mean(x * x, axis=-1, keepdims=True) + RMS_EPS)
        o_ref[...] = ((x * rstd) * g_ref[...]).astype(BF16)

    row = pl.BlockSpec((tr, d), lambda i: (i, 0))
    return pl.pallas_call(
        body, name=name, grid=(m // tr,), out_shape=jax.ShapeDtypeStruct((m, d), BF16),
        in_specs=[row, pl.BlockSpec((1, d), lambda i: (0, 0))], out_specs=row,
        compiler_params=_params(("parallel",)),
    )(h, g.reshape(1, d))


def _rms_bwd_math(x, g, dn):
    rstd = lax.rsqrt(jnp.mean(x * x, axis=-1, keepdims=True) + RMS_EPS)
    xhat = x * rstd
    dxhat = dn * g
    dx = rstd * (dxhat - xhat * jnp.mean(dxhat * xhat, axis=-1, keepdims=True))
    return dx, dn * xhat


def _fold8(v):
    r, c = v.shape
    return jnp.sum(v.reshape(r // 8, 8, c), axis=0)


def _rms_bwd(name, dn, h, g, dh_in):
    m, d = h.shape
    tr = _pick(m, 256, 16)
    nt = m // tr

    def body(dn_ref, h_ref, g_ref, dh_ref, o_ref, ob_ref, dg_ref):
        dx, dgp = _rms_bwd_math(h_ref[...], g_ref[...], dn_ref[...])
        dh = dh_ref[...] + dx
        o_ref[...] = dh
        ob_ref[...] = dh.astype(BF16)

        @pl.when(pl.program_id(0) == 0)
        def _():
            dg_ref[...] = jnp.zeros_like(dg_ref)

        dg_ref[...] += _fold8(dgp)

    row = pl.BlockSpec((tr, d), lambda i: (i, 0))
    return pl.pallas_call(
        body, name=name, grid=(nt,),
        out_shape=(jax.ShapeDtypeStruct((m, d), F32), jax.ShapeDtypeStruct((m, d), BF16),
                   jax.ShapeDtypeStruct((8, d), F32)),
        in_specs=[row, row, pl.BlockSpec((1, d), lambda i: (0, 0)), row],
        out_specs=(row, row, pl.BlockSpec((8, d), lambda i: (0, 0))),
        compiler_params=_params(("arbitrary",)),
    )(dn, h, g.reshape(1, d), dh_in)


def _loss_head(h, g, target):
    m, d = h.shape
    tr = BLOCK

    def body(h_ref, g_ref, t_ref, loss_ref, o_ref, ob_ref, dg_ref):
        i = pl.program_id(0)
        x = h_ref[...]
        gv = g_ref[...]
        rstd = lax.rsqrt(jnp.mean(x * x, axis=-1, keepdims=True) + RMS_EPS)
        err = jnp.where(i > 0, (x * rstd) * gv - t_ref[...], 0.0)
        dx, dgp = _rms_bwd_math(x, gv, err * (1.0 / d))
        o_ref[...] = dx
        ob_ref[...] = dx.astype(BF16)

        @pl.when(i == 0)
        def _():
            dg_ref[...] = jnp.zeros_like(dg_ref)
            loss_ref[...] = jnp.zeros_like(loss_ref)

        dg_ref[...] += _fold8(dgp)
        sq = jnp.mean(err * err, axis=-1, keepdims=True)
        loss_ref[...] += 0.5 * jnp.sum(sq, axis=0, keepdims=True)

    row = pl.BlockSpec((tr, d), lambda i: (i, 0))
    return pl.pallas_call(
        body, name="loss_head", grid=(m // tr,),
        out_shape=(jax.ShapeDtypeStruct((8, 128), F32), jax.ShapeDtypeStruct((m, d), F32),
                   jax.ShapeDtypeStruct((m, d), BF16), jax.ShapeDtypeStruct((8, d), F32)),
        in_specs=[row, pl.BlockSpec((1, d), lambda i: (0, 0)),
                  pl.BlockSpec((tr, d), lambda i: (jnp.maximum(i - 1, 0), 0))],
        out_specs=(pl.BlockSpec((8, 128), lambda i: (0, 0)), row, row,
                   pl.BlockSpec((8, d), lambda i: (0, 0))),
        compiler_params=_params(("arbitrary",)),
    )(h, g.reshape(1, d), target)


HALO = 16


def _shift_down(cat, k):
    return pltpu.roll(cat, k, axis=0)[HALO:]


def _shift_up(cat, k):
    n = cat.shape[0]
    return pltpu.roll(cat, n - k, axis=0)[:n - HALO]


def _conv_fwd(bcu, cw):
    m, d3 = bcu.shape
    d = d3 // 3
    tr, tc = _pick(m, 416, 16), _pick(d, 512, 128)
    nd, hb = d // tc, tr // HALO

    def body(b_ref, c_ref, u_ref, ch_ref, uh_ref, w_ref, o_ref):
        i = pl.program_id(0)
        v = c_ref[...].astype(F32) * u_ref[...].astype(F32)
        vh = jnp.where(i > 0, ch_ref[...].astype(F32) * uh_ref[...].astype(F32), 0.0)
        cat = jnp.concatenate([vh, v], axis=0)
        w = w_ref[...]
        conv = w[2:3] * v + w[1:2] * _shift_down(cat, 1) + w[0:1] * _shift_down(cat, 2)
        o_ref[...] = (b_ref[...].astype(F32) * conv).astype(BF16)

    def part(p):
        return pl.BlockSpec((tr, tc), lambda i, j: (i, p * nd + j))

    def halo(p):
        return pl.BlockSpec((HALO, tc), lambda i, j: (jnp.maximum(i * hb - 1, 0), p * nd + j))

    return pl.pallas_call(
        body, name="conv_fwd", grid=(m // tr, nd), out_shape=jax.ShapeDtypeStruct((m, d), BF16),
        in_specs=[part(0), part(1), part(2), halo(1), halo(2), pl.BlockSpec((8, tc), lambda i, j: (0, j))],
        out_specs=pl.BlockSpec((tr, tc), lambda i, j: (i, j)),
        compiler_params=_params(("parallel", "parallel")),
    )(bcu, bcu, bcu, bcu, bcu, cw)


def _conv_bwd(bcu, cw, dg):
    m, d3 = bcu.shape
    d = d3 // 3
    tr, tc = _pick(m, 416, 16), _pick(d, 512, 128)
    nd, hb, nt = d // tc, tr // HALO, m // tr

    def body(b_ref, c_ref, u_ref, ch_ref, uh_ref, bn_ref, dg_ref, dgn_ref, w_ref, o_ref, dw_ref):
        i, p = pl.program_id(1), pl.program_id(2)
        w = w_ref[...]
        b, c, u = b_ref[...].astype(F32), c_ref[...].astype(F32), u_ref[...].astype(F32)
        dgv = dg_ref[...].astype(F32)
        v = c * u
        vh = jnp.where(i > 0, ch_ref[...].astype(F32) * uh_ref[...].astype(F32), 0.0)
        cat = jnp.concatenate([vh, v], axis=0)
        v1, v2 = _shift_down(cat, 1), _shift_down(cat, 2)
        dconv = dgv * b

        @pl.when(p == 0)
        def _():
            o_ref[...] = (dgv * (w[2:3] * v + w[1:2] * v1 + w[0:1] * v2)).astype(BF16)

            @pl.when(i == 0)
            def _():
                dw_ref[...] = jnp.zeros_like(dw_ref)

            taps = [jnp.sum(dconv * t, axis=0, keepdims=True) for t in (v2, v1, v)]
            dw_ref[...] += jnp.concatenate(taps + [jnp.zeros((5, tc), F32)], axis=0)

        @pl.when(p > 0)
        def _():
            nxt = jnp.where(i < nt - 1, dgn_ref[...].astype(F32) * bn_ref[...].astype(F32), 0.0)
            cat2 = jnp.concatenate([dconv, nxt], axis=0)
            dv = w[2:3] * dconv + w[1:2] * _shift_up(cat2, 1) + w[0:1] * _shift_up(cat2, 2)
            o_ref[...] = jnp.where(p == 1, dv * u, dv * c).astype(BF16)

    def part(q):
        return pl.BlockSpec((tr, tc), lambda j, i, p: (i, q * nd + j))

    def before(q):
        return pl.BlockSpec((HALO, tc), lambda j, i, p: (jnp.maximum(i * hb - 1, 0), q * nd + j))

    def after(q):
        return pl.BlockSpec((HALO, tc), lambda j, i, p: (jnp.minimum((i + 1) * hb, m // HALO - 1), q * nd + j))

    return pl.pallas_call(
        body, name="conv_bwd", grid=(nd, nt, 3),
        out_shape=(jax.ShapeDtypeStruct((m, d3), BF16), jax.ShapeDtypeStruct((8, d), F32)),
        in_specs=[part(0), part(1), part(2), before(1), before(2), after(0),
                  pl.BlockSpec((tr, tc), lambda j, i, p: (i, j)), after(0),
                  pl.BlockSpec((8, tc), lambda j, i, p: (0, j))],
        out_specs=(pl.BlockSpec((tr, tc), lambda j, i, p: (i, p * nd + j)),
                   pl.BlockSpec((8, tc), lambda j, i, p: (0, j))),
        compiler_params=_params(("parallel", "arbitrary", "arbitrary")),
    )(bcu, bcu, bcu, bcu, bcu, bcu, dg, dg, cw)


def _rope_tables(m):
    pad = ROW0 - N_META
    pos = jnp.arange(m, dtype=F32) - pad
    inv = ROPE_THETA ** (-jnp.arange(0, HEAD_DIM, 2, dtype=F32) / HEAD_DIM)
    ang = pos[:, None] * inv[None, :]
    return jnp.cos(ang), jnp.sin(ang)


def _rope(x, c, s):
    half = HEAD_DIM // 2
    x1, x2 = x[:, :half], x[:, half:]
    return jnp.concatenate([x1 * c - x2 * s, x2 * c + x1 * s], axis=-1)


def _rope_t(y, c, s):
    half = HEAD_DIM // 2
    y1, y2 = y[:, :half], y[:, half:]
    return jnp.concatenate([y1 * c + y2 * s, y2 * c - y1 * s], axis=-1)


def _qkv_split(qkv, cos, sin):
    m = qkv.shape[0]
    nb, grp = m // BLOCK, N_Q_HEADS // N_KV_HEADS
    scale = HEAD_DIM ** -0.5

    def body(x_ref, c_ref, s_ref, q_ref, k_ref, v_ref):
        c, s = c_ref[...], s_ref[...]
        for h in range(N_KV_HEADS):
            for g in range(grp):
                col = (h * grp + g) * HEAD_DIM
                xq = x_ref[:, col:col + HEAD_DIM].astype(F32)
                q_ref[h, g * BLOCK:(g + 1) * BLOCK, :] = (_rope(xq, c, s) * scale).astype(BF16)
            col = (N_Q_HEADS + h) * HEAD_DIM
            k_ref[h] = _rope(x_ref[:, col:col + HEAD_DIM].astype(F32), c, s).astype(BF16)
            col = (N_Q_HEADS + N_KV_HEADS + h) * HEAD_DIM
            v_ref[h] = x_ref[:, col:col + HEAD_DIM]

    tab = pl.BlockSpec((BLOCK, HEAD_DIM // 2), lambda i: (i, 0))
    kv = pl.BlockSpec((N_KV_HEADS, BLOCK, HEAD_DIM), lambda i: (0, i, 0))
    return pl.pallas_call(
        body, name="qkv_split", grid=(nb,),
        out_shape=(jax.ShapeDtypeStruct((N_KV_HEADS, nb * grp * BLOCK, HEAD_DIM), BF16),
                   jax.ShapeDtypeStruct((N_KV_HEADS, m, HEAD_DIM), BF16),
                   jax.ShapeDtypeStruct((N_KV_HEADS, m, HEAD_DIM), BF16)),
        in_specs=[pl.BlockSpec((BLOCK, qkv.shape[1]), lambda i: (i, 0)), tab, tab],
        out_specs=(pl.BlockSpec((N_KV_HEADS, grp * BLOCK, HEAD_DIM), lambda i: (0, i, 0)), kv, kv),
        compiler_params=_params(("parallel",)),
    )(qkv, cos, sin)


def _heads_merge(name, o):
    grp = N_Q_HEADS // N_KV_HEADS
    nb = o.shape[1] // (grp * BLOCK)

    def body(o_ref, x_ref):
        for h in range(N_KV_HEADS):
            for g in range(grp):
                col = (h * grp + g) * HEAD_DIM
                x_ref[:, col:col + HEAD_DIM] = o_ref[h, g * BLOCK:(g + 1) * BLOCK, :]

    return pl.pallas_call(
        body, name=name, grid=(nb,),
        out_shape=jax.ShapeDtypeStruct((nb * BLOCK, N_Q_HEADS * HEAD_DIM), o.dtype),
        in_specs=[pl.BlockSpec((N_KV_HEADS, grp * BLOCK, HEAD_DIM), lambda i: (0, i, 0))],
        out_specs=pl.BlockSpec((BLOCK, N_Q_HEADS * HEAD_DIM), lambda i: (i, 0)),
        compiler_params=_params(("parallel",)),
    )(o)


def _heads_split(name, x):
    grp = N_Q_HEADS // N_KV_HEADS
    nb = x.shape[0] // BLOCK

    def body(x_ref, o_ref):
        for h in range(N_KV_HEADS):
            for g in range(grp):
                col = (h * grp + g) * HEAD_DIM
                o_ref[h, g * BLOCK:(g + 1) * BLOCK, :] = x_ref[:, col:col + HEAD_DIM]

    return pl.pallas_call(
        body, name=name, grid=(nb,),
        out_shape=jax.ShapeDtypeStruct((N_KV_HEADS, nb * grp * BLOCK, HEAD_DIM), x.dtype),
        in_specs=[pl.BlockSpec((BLOCK, N_Q_HEADS * HEAD_DIM), lambda i: (i, 0))],
        out_specs=pl.BlockSpec((N_KV_HEADS, grp * BLOCK, HEAD_DIM), lambda i: (0, i, 0)),
        compiler_params=_params(("parallel",)),
    )(x)


def _attn_mask(i):
    r = lax.broadcasted_iota(jnp.int32, (BLOCK, 2 * BLOCK), 0)
    cidx = lax.broadcasted_iota(jnp.int32, (BLOCK, 2 * BLOCK), 1)
    key = (i - 1) * BLOCK + cidx
    return (cidx > r) & (cidx <= r + BLOCK) & (key >= ROW0 - N_META)


def _attn_exp(allowed, q, kb, vb1, sink):
    rows = q.shape[0]
    grp = rows // BLOCK
    s = lax.dot_general(q, kb, _NT, preferred_element_type=F32)
    s = jnp.where(allowed[None], s.reshape(grp, BLOCK, 2 * BLOCK), NEG_INF).reshape(rows, 2 * BLOCK)
    mx = jnp.maximum(jnp.max(s, axis=-1, keepdims=True), sink)
    eb = jnp.exp(s - mx).astype(BF16)
    es = jnp.exp(sink - mx)
    ov = jnp.dot(eb, vb1, preferred_element_type=F32)
    inv = 1.0 / (ov[:, HEAD_DIM:HEAD_DIM + 1] + es)
    return eb, ov[:, :HEAD_DIM], inv, es


def _band(prev_ref, cur_ref, h):
    return jnp.concatenate([prev_ref[h], cur_ref[h]], axis=0)


def _with_ones(vb):
    return jnp.concatenate([vb, jnp.ones_like(vb)], axis=1)


def _attn_specs(grp):
    q = pl.BlockSpec((N_KV_HEADS, grp * BLOCK, HEAD_DIM), lambda i: (0, i, 0))
    cur = pl.BlockSpec((N_KV_HEADS, BLOCK, HEAD_DIM), lambda i: (0, i, 0))
    prev = pl.BlockSpec((N_KV_HEADS, BLOCK, HEAD_DIM), lambda i: (0, jnp.maximum(i - 1, 0), 0))
    sink = pl.BlockSpec((N_KV_HEADS, grp * BLOCK, 1), lambda i: (0, 0, 0))
    return q, cur, prev, sink


def _attn_fwd(q, k, v, sink_rows):
    grp = N_Q_HEADS // N_KV_HEADS
    nb = k.shape[1] // BLOCK

    def body(q_ref, kc_ref, kp_ref, vc_ref, vp_ref, s_ref, o_ref):
        allowed = _attn_mask(pl.program_id(0))
        for h in range(N_KV_HEADS):
            vb1 = _with_ones(_band(vp_ref, vc_ref, h))
            _, ov, inv, _ = _attn_exp(allowed, q_ref[h], _band(kp_ref, kc_ref, h), vb1, s_ref[h])
            o_ref[h] = (ov * inv).astype(BF16)

    qs, cur, prev, sink = _attn_specs(grp)
    return pl.pallas_call(
        body, name="attn_fwd", grid=(nb,), out_shape=jax.ShapeDtypeStruct(q.shape, BF16),
        in_specs=[qs, cur, prev, cur, prev, sink], out_specs=qs,
        compiler_params=_params(("parallel",)),
    )(q, k, k, v, v, sink_rows)


def _attn_bwd(q, k, v, sink_rows, do):
    grp = N_Q_HEADS // N_KV_HEADS
    nb = k.shape[1] // BLOCK

    def body(q_ref, kc_ref, kp_ref, vc_ref, vp_ref, s_ref, do_ref, dq_ref, dk_ref, dv_ref, ds_ref):
        i = pl.program_id(0)
        allowed = _attn_mask(i)

        @pl.when(i == 0)
        def _():
            ds_ref[...] = jnp.zeros_like(ds_ref)

        for h in range(N_KV_HEADS):
            qv, dov = q_ref[h], do_ref[h]
            kb, vb = _band(kp_ref, kc_ref, h), _band(vp_ref, vc_ref, h)
            eb, ov, inv, es = _attn_exp(allowed, qv, kb, _with_ones(vb), s_ref[h])
            dof = dov.astype(F32)
            delta = jnp.sum(dof * (ov * inv), axis=-1, keepdims=True)
            dp = lax.dot_general(dov, vb, _NT, preferred_element_type=F32)
            dsb = (eb.astype(F32) * (inv * (dp - delta))).astype(BF16)
            dq_ref[h] = jnp.dot(dsb, kb, preferred_element_type=F32).astype(BF16)
            dk_ref[h] = lax.dot_general(dsb, qv, _TN, preferred_element_type=F32)
            dv_ref[h] = lax.dot_general(eb, (dof * inv).astype(BF16), _TN, preferred_element_type=F32)
            ds_ref[h] -= (es * inv) * delta

    qs, cur, prev, sink = _attn_specs(grp)
    band = pl.BlockSpec((N_KV_HEADS, None, 2 * BLOCK, HEAD_DIM), lambda i: (0, i, 0, 0))
    band_shape = jax.ShapeDtypeStruct((N_KV_HEADS, nb, 2 * BLOCK, HEAD_DIM), F32)
    return pl.pallas_call(
        body, name="attn_bwd", grid=(nb,),
        out_shape=(jax.ShapeDtypeStruct(q.shape, BF16), band_shape, band_shape,
                   jax.ShapeDtypeStruct(sink_rows.shape, F32)),
        in_specs=[qs, cur, prev, cur, prev, sink, qs], out_specs=(qs, band, band, sink),
        compiler_params=_params(("arbitrary",)),
    )(q, k, k, v, v, sink_rows, do)


def _qkv_merge_bwd(dq, dkb, dvb, cos, sin):
    grp = N_Q_HEADS // N_KV_HEADS
    nb = dkb.shape[1]
    width = (N_Q_HEADS + 2 * N_KV_HEADS) * HEAD_DIM
    scale = HEAD_DIM ** -0.5

    def body(dq_ref, kc_ref, kn_ref, vc_ref, vn_ref, c_ref, s_ref, o_ref):
        last = pl.program_id(0) == nb - 1
        c, s = c_ref[...], s_ref[...]
        for h in range(N_KV_HEADS):
            for g in range(grp):
                col = (h * grp + g) * HEAD_DIM
                y = dq_ref[h, g * BLOCK:(g + 1) * BLOCK, :].astype(F32) * scale
                o_ref[:, col:col + HEAD_DIM] = _rope_t(y, c, s).astype(BF16)
            dk = kc_ref[h, BLOCK:, :] + jnp.where(last, 0.0, kn_ref[h, :BLOCK, :])
            col = (N_Q_HEADS + h) * HEAD_DIM
            o_ref[:, col:col + HEAD_DIM] = _rope_t(dk, c, s).astype(BF16)
            dv = vc_ref[h, BLOCK:, :] + jnp.where(last, 0.0, vn_ref[h, :BLOCK, :])
            col = (N_Q_HEADS + N_KV_HEADS + h) * HEAD_DIM
            o_ref[:, col:col + HEAD_DIM] = dv.astype(BF16)

    tab = pl.BlockSpec((BLOCK, HEAD_DIM // 2), lambda i: (i, 0))
    cur = pl.BlockSpec((N_KV_HEADS, None, 2 * BLOCK, HEAD_DIM), lambda i: (0, i, 0, 0))
    nxt = pl.BlockSpec((N_KV_HEADS, None, 2 * BLOCK, HEAD_DIM), lambda i: (0, jnp.minimum(i + 1, nb - 1), 0, 0))
    return pl.pallas_call(
        body, name="qkv_merge_bwd", grid=(nb,), out_shape=jax.ShapeDtypeStruct((nb * BLOCK, width), BF16),
        in_specs=[pl.BlockSpec((N_KV_HEADS, grp * BLOCK, HEAD_DIM), lambda i: (0, i, 0)),
                  cur, nxt, cur, nxt, tab, tab],
        out_specs=pl.BlockSpec((BLOCK, width), lambda i: (i, 0)),
        compiler_params=_params(("parallel",)),
    )(dq, dkb, dkb, dvb, dvb, cos, sin)


def _tiles2d(r, c):
    tc = _pick(c, 2048, 128) if c % 128 == 0 else c
    tr = _pick(r, max(8, (1 << 20) // tc // 8 * 8), 8) if r % 8 == 0 else r
    return tr, tc


def _cast_bf16(name, w, place):
    r, c = w.shape
    tr, tc = _tiles2d(r, c)
    if tr % 16:
        tr = r

    def body(place_ref, w_ref, o_ref):
        o_ref[...] = w_ref[...].astype(BF16)

    return pl.pallas_call(
        body, name=name, out_shape=jax.ShapeDtypeStruct((N_CHIPS, r, c), BF16),
        grid_spec=pltpu.PrefetchScalarGridSpec(
            num_scalar_prefetch=1, grid=(r // tr, c // tc),
            in_specs=[pl.BlockSpec((tr, tc), lambda i, j, p: (i, j))],
            out_specs=pl.BlockSpec((None, tr, tc), lambda i, j, p: (p[1], i, j))),
        compiler_params=_params(("parallel", "parallel")),
    )(place, w)


def _pair_sum(name, g, got, place):
    n, r, c = g.shape
    half = r // 2
    tr, tc = _tiles2d(half, c)
    nh = half // tr

    def body(place_ref, g_ref, got_ref, o_ref, own_ref):
        s = (g_ref[...] + got_ref[...]).astype(BF16)
        o_ref[...] = s

        @pl.when(pl.program_id(2) == place_ref[1])
        def _():
            own_ref[...] = s

    tile = pl.BlockSpec((None, tr, tc), lambda i, j, k, p: (k, i, j))
    shape = jax.ShapeDtypeStruct((n, half, c), BF16)
    return pl.pallas_call(
        body, name=name, out_shape=(shape, shape),
        grid_spec=pltpu.PrefetchScalarGridSpec(
            num_scalar_prefetch=1, grid=(nh, c // tc, n),
            in_specs=[pl.BlockSpec((None, tr, tc), lambda i, j, k, p: (k, p[0] * nh + i, j)), tile],
            out_specs=(tile, pl.BlockSpec((None, tr, tc), lambda i, j, k, p: (p[1], i, j)))),
        compiler_params=_params(("parallel", "parallel", "arbitrary")),
    )(place, g, got)


def _chip_sum(name, parts, place):
    n, half, c = parts.shape
    tr, tc = _tiles2d(half, c)
    nh = half // tr

    def body(place_ref, p0, p1, p2, p3, o_ref):
        o_ref[...] = ((p0[...].astype(F32) + p1[...].astype(F32)) + p2[...].astype(F32)) + p3[...].astype(F32)

    def chip(k):
        return pl.BlockSpec((None, tr, tc), lambda i, j, p: (k, i, j))

    return pl.pallas_call(
        body, name=name, out_shape=jax.ShapeDtypeStruct((2 * half, c), F32),
        grid_spec=pltpu.PrefetchScalarGridSpec(
            num_scalar_prefetch=1, grid=(nh, c // tc),
            in_specs=[chip(k) for k in range(n)],
            out_specs=pl.BlockSpec((tr, tc), lambda i, j, p: (p[0] * nh + i, j))),
        compiler_params=_params(("parallel", "parallel")),
    )(place, parts, parts, parts, parts)


def _dev_sum(gathered):
    def body(g_ref, o_ref):
        acc = g_ref[0]
        for k in range(1, N_DEV):
            acc = acc + g_ref[k]
        o_ref[...] = acc

    return pl.pallas_call(body, name="dev_sum", out_shape=jax.ShapeDtypeStruct(gathered.shape[1:], F32))(gathered)


def _adamw(name, w, g, m, v):
    r, c = w.shape
    tr, tc = _tiles2d(r, c)
    if r % 8 == 0:
        tr = _pick(r, max(8, (1 << 18) // tc // 8 * 8), 8)

    def body(w_ref, g_ref, m_ref, v_ref, d_ref, mo_ref, vo_ref):
        gv = g_ref[...]
        mn = ADAM_B1 * m_ref[...] + (1.0 - ADAM_B1) * gv
        vn = ADAM_B2 * v_ref[...] + (1.0 - ADAM_B2) * jnp.square(gv)
        m_hat = mn / (1.0 - ADAM_B1 ** ADAM_STEP)
        v_hat = vn / (1.0 - ADAM_B2 ** ADAM_STEP)
        d_ref[...] = -ADAM_LR * (m_hat / (jnp.sqrt(v_hat) + ADAM_EPS) + ADAM_WD * w_ref[...])
        mo_ref[...] = mn
        vo_ref[...] = vn

    tile = pl.BlockSpec((tr, tc), lambda i, j: (i, j))
    shape = jax.ShapeDtypeStruct((r, c), F32)
    return pl.pallas_call(
        body, name=name, grid=(r // tr, c // tc), out_shape=(shape, shape, shape),
        in_specs=[tile] * 4, out_specs=(tile,) * 3, compiler_params=_params(("parallel", "parallel")),
    )(w, g, m, v)


MATRICES = ("w_in_conv", "w_out_conv", "w_up_0", "w_down_0", "w_qkv", "w_o", "w_up_1", "w_down_1")
COLUMN_SHARDED = ("w_in_conv", "w_up_0", "w_qkv", "w_up_1")
NORMS = ("norm_mix_0", "norm_mlp_0", "norm_mix_1", "norm_mlp_1", "norm_final")


def _rows(stack):
    return stack.reshape(N_CHIPS * stack.shape[1], stack.shape[2])


def _stack(full):
    return full.reshape(N_CHIPS, full.shape[0] // N_CHIPS, full.shape[1])


def _add_residual(acc, res):
    return acc + res


def _relu_sq_grad(acc, z):
    return acc * (2.0 * jnp.maximum(z.astype(F32), 0.0))


def _step(x, target, stacks, small, norms, sinks, place):
    d = D_MODEL
    dc = d // N_CHIPS
    pad = ROW0 - N_META
    m = x.shape[0] + ROW0
    grp = N_Q_HEADS // N_KV_HEADS
    cos, sin = _rope_tables(m)
    sink_rows = jnp.repeat(sinks.astype(F32), BLOCK).reshape(N_KV_HEADS, grp * BLOCK, 1)

    def gather(*names):
        return _gather_task([stacks[n] for n in names])

    def pair_sum(tag, grad, got):
        return _pair_sum("pair_sum_" + tag, grad, got, place)

    def chip_sum(tag, landed):
        return _chip_sum("chip_sum_" + tag, landed, place)

    (w_in, small_all), = _run("gather_first", [_gather_task([stacks["w_in_conv"], small])])
    small_full = jnp.transpose(small_all, (1, 0, 2)).reshape(SMALL_ROWS, d)
    conv_w8 = small_full[N_META:N_META + 8]
    h0 = jnp.concatenate([jnp.zeros((pad, d), F32), small_full[:N_META], x], axis=0)

    n0 = _rms_fwd("norm_mix_0", h0, norms["norm_mix_0"])
    bcu, ((w_out, w_up0),) = _mm_nn_cols("conv_in", n0, w_in, BF16, tasks=[gather("w_out_conv", "w_up_0")])
    gate = _conv_fwd(bcu, conv_w8)
    h1 = _mm_nn_rows("conv_out", gate, _rows(w_out), F32, epi=_add_residual, extras=(h0,))
    n1 = _rms_fwd("norm_mlp_0", h1, norms["norm_mlp_0"])
    z0, ((w_down0,),) = _mm_nn_cols("mlp_up_0", n1, w_up0, BF16, tasks=[gather("w_down_0")])
    h2, ((w_qkv, w_o, w_up1),) = _mm_nn_rows("mlp_down_0", z0, _rows(w_down0), F32, a_pro=_relu_sq, epi=_add_residual,
                                             extras=(h1,), tasks=[gather("w_qkv", "w_o", "w_up_1")])
    n2 = _rms_fwd("norm_mix_1", h2, norms["norm_mix_1"])
    qkv = _mm_nn_cols("attn_qkv", n2, w_qkv, BF16)
    q, k, v = _qkv_split(qkv, cos, sin)
    o = _heads_merge("attn_o_merge", _attn_fwd(q, k, v, sink_rows))
    h3 = _mm_nn_rows("attn_out", o, _rows(w_o), F32, epi=_add_residual, extras=(h2,))
    n3 = _rms_fwd("norm_mlp_1", h3, norms["norm_mlp_1"])
    z1, ((w_down1,),) = _mm_nn_cols("mlp_up_1", n3, w_up1, BF16, tasks=[gather("w_down_1")])
    h4 = _mm_nn_rows("mlp_down_1", z1, _rows(w_down1), F32, a_pro=_relu_sq, epi=_add_residual, extras=(h3,))

    gn = {}
    loss, dh, dh_bf, gn["norm_final"] = _loss_head(h4, norms["norm_final"], target)
    dz = _mm_nt_rows("mlp_down_dx_1", dh_bf, _rows(w_down1), BF16, epi=_relu_sq_grad, extras=(z1,))
    g_d1 = _stack(_mm_tn("mlp_down_dw_1", z1, dh_bf, stacked=False, a_pro=_relu_sq))
    g_u1, ((got,),) = _mm_tn("mlp_up_dw_1", n3, dz, stacked=True, tasks=[_pair_exchange_task([g_d1])])
    s_d1 = pair_sum("d1", g_d1, got)
    dn, ((got,), (landed,)) = _mm_nt_cols("mlp_up_dx_1", dz, w_up1, F32,
                                          tasks=[_pair_exchange_task([g_u1]), _chip_exchange_task([s_d1])])
    s_u1, b_d1 = pair_sum("u1", g_u1, got), chip_sum("d1", landed)
    dh, dh_bf, gn["norm_mlp_1"] = _rms_bwd("norm_mlp_bwd_1", dn, h3, norms["norm_mlp_1"], dh)
    do = _mm_nt_rows("attn_out_dx", dh_bf, _rows(w_o), BF16)
    g_o = _stack(_mm_tn("attn_out_dw", o, dh_bf, stacked=False))
    dq, dkb, dvb, dsink = _attn_bwd(q, k, v, sink_rows, _heads_split("attn_do_split", do))
    dqkv = _qkv_merge_bwd(dq, dkb, dvb, cos, sin)
    g_qkv, ((got,),) = _mm_tn("attn_qkv_dw", n2, dqkv, stacked=True, tasks=[_pair_exchange_task([g_o])])
    s_o = pair_sum("o", g_o, got)
    dn, ((got,), (landed,)) = _mm_nt_cols("attn_qkv_dx", dqkv, w_qkv, F32,
                                          tasks=[_pair_exchange_task([g_qkv]), _chip_exchange_task([s_u1])])
    s_qkv, b_u1 = pair_sum("qkv", g_qkv, got), chip_sum("u1", landed)
    dh, dh_bf, gn["norm_mix_1"] = _rms_bwd("norm_mix_bwd_1", dn, h2, norms["norm_mix_1"], dh)
    dz, ((landed_o, landed_qkv), (r_d1,)) = _mm_nt_rows(
        "mlp_down_dx_0", dh_bf, _rows(w_down0), BF16, epi=_relu_sq_grad, extras=(z0,),
        tasks=[_chip_exchange_task([s_o, s_qkv]), _pair_share_task([b_d1])])
    b_o, b_qkv = chip_sum("o", landed_o), chip_sum("qkv", landed_qkv)
    g_d0, ((r_u1,),) = _mm_tn("mlp_down_dw_0", z0, dh_bf, stacked=False, a_pro=_relu_sq, tasks=[_pair_share_task([b_u1])])
    g_d0 = _stack(g_d0)
    g_u0, ((got,), (r_o, r_qkv)) = _mm_tn("mlp_up_dw_0", n1, dz, stacked=True,
                                          tasks=[_pair_exchange_task([g_d0]), _pair_share_task([b_o, b_qkv])])
    s_d0 = pair_sum("d0", g_d0, got)
    dn, ((got,), (landed,)) = _mm_nt_cols("mlp_up_dx_0", dz, w_up0, F32,
                                          tasks=[_pair_exchange_task([g_u0]), _chip_exchange_task([s_d0])])
    s_u0, b_d0 = pair_sum("u0", g_u0, got), chip_sum("d0", landed)
    dh, dh_bf, gn["norm_mlp_0"] = _rms_bwd("norm_mlp_bwd_0", dn, h1, norms["norm_mlp_0"], dh)
    dgate = _mm_nt_rows("conv_out_dx", dh_bf, _rows(w_out), BF16)
    g_out = _stack(_mm_tn("conv_out_dw", gate, dh_bf, stacked=False))
    dbcu, g_conv_w = _conv_bwd(bcu, conv_w8, dgate)
    dn, ((got,), (landed,), (r_d0,)) = _mm_nt_cols(
        "conv_in_dx", dbcu, w_in, F32,
        tasks=[_pair_exchange_task([g_out]), _chip_exchange_task([s_u0]), _pair_share_task([b_d0])])
    s_out, b_u0 = pair_sum("out", g_out, got), chip_sum("u0", landed)
    dh0, _, gn["norm_mix_0"] = _rms_bwd("norm_mix_bwd_0", dn, h0, norms["norm_mix_0"], dh)

    g_small = jnp.zeros((SMALL_ROWS, d), F32).at[:N_META].set(dh0[pad:ROW0]).at[N_META:N_META + 8].set(g_conv_w)
    g_small = jnp.transpose(g_small.reshape(SMALL_ROWS, N_CHIPS, dc), (1, 0, 2))
    rep = jnp.zeros((8, d), F32)
    for r, n in enumerate(NORMS):
        rep = rep.at[r].set(jnp.sum(gn[n], axis=0))
    rep = rep.at[len(NORMS), :N_Q_HEADS].set(jnp.sum(dsink.reshape(N_Q_HEADS, BLOCK), axis=1))
    g_in, ((landed,), (got, rep_all), (r_u0,)) = _mm_tn(
        "conv_in_dw", n0, dbcu, stacked=True,
        tasks=[_chip_exchange_task([s_out]), _pair_exchange_task([g_small], small=rep), _pair_share_task([b_u0])])
    b_out, s_small = chip_sum("out", landed), pair_sum("small", g_small, got)
    (got,), = _run("tail_pair_exchange", [_pair_exchange_task([g_in])])
    s_in = pair_sum("in", g_in, got)
    (landed_in, landed_small), = _run("tail_chip_exchange", [_chip_exchange_task([s_in, s_small])])
    b_in, b_small = chip_sum("in", landed_in), chip_sum("small", landed_small)
    (r_in, r_small, r_out), = _run("tail_pair_share", [_pair_share_task([b_in, b_small, b_out])])

    reduced = {"w_in_conv": r_in, "w_out_conv": r_out, "w_up_0": r_u0, "w_down_0": r_d0, "w_qkv": r_qkv, "w_o": r_o,
               "w_up_1": r_u1, "w_down_1": r_d1}
    return loss, dh0, reduced, r_small, rep_all


def kernel(x, meta_tokens, norm_mix_0, w_in_conv, conv_w, w_out_conv, norm_mlp_0, w_up_0, w_down_0, norm_mix_1, w_qkv, attn_sinks, w_o, norm_mlp_1, w_up_1, w_down_1, norm_final, loss_target, m_meta_tokens, m_norm_mix_0, m_w_in_conv, m_conv_w, m_w_out_conv, m_norm_mlp_0, m_w_up_0, m_w_down_0, m_norm_mix_1, m_w_qkv, m_attn_sinks, m_w_o, m_norm_mlp_1, m_w_up_1, m_w_down_1, m_norm_final, v_meta_tokens, v_norm_mix_0, v_w_in_conv, v_conv_w, v_w_out_conv, v_norm_mlp_0, v_w_up_0, v_w_down_0, v_norm_mix_1, v_w_qkv, v_attn_sinks, v_w_o, v_norm_mlp_1, v_w_up_1, v_w_down_1, v_norm_final):
    given = dict(locals())
    names = ("meta_tokens", "norm_mix_0", "w_in_conv", "conv_w", "w_out_conv", "norm_mlp_0", "w_up_0", "w_down_0",
             "norm_mix_1", "w_qkv", "attn_sinks", "w_o", "norm_mlp_1", "w_up_1", "w_down_1", "norm_final")
    d = D_MODEL
    dc = d // N_CHIPS
    chip = 2 * lax.axis_index("x") + lax.axis_index("y")
    place = jnp.stack([lax.axis_index("c"), chip]).astype(jnp.int32)

    small = jnp.zeros((SMALL_ROWS, dc), F32).at[:N_META].set(meta_tokens).at[N_META:N_META + CONV_WIDTH].set(conv_w)
    small = lax.dynamic_update_slice(jnp.zeros((N_CHIPS, SMALL_ROWS, dc), F32), small[None], (chip, 0, 0))
    stacks = {n: _cast_bf16("cast_" + n, given[n], place) for n in MATRICES}

    norms = {n: given[n] for n in NORMS}
    loss_part, dh0, g_out, r_small, rep_all = _step(x[0], loss_target[0], stacks, small, norms, attn_sinks, place)
    loss = lax.psum(loss_part[0, 0], ("x", "y", "c"))
    grad_x = dh0[ROW0:][None]
    rep_sum = _dev_sum(rep_all)
    g_out["meta_tokens"] = r_small[:N_META]
    g_out["conv_w"] = r_small[N_META:N_META + CONV_WIDTH]
    for r, n in enumerate(NORMS):
        g_out[n] = rep_sum[r]
    g_out["attn_sinks"] = rep_sum[len(NORMS), :N_Q_HEADS]

    delta, new_m, new_v = {}, {}, {}
    for n in names:
        wt = given[n]
        shape2 = wt.shape if wt.ndim == 2 else (1, wt.shape[0])
        outs = _adamw("adamw_" + n, wt.reshape(shape2), g_out[n].reshape(shape2),
                      given["m_" + n].reshape(shape2), given["v_" + n].reshape(shape2))
        delta[n], new_m[n], new_v[n] = [o.reshape(wt.shape) for o in outs]
    return (loss, grad_x, *[g_out[n] for n in names], *[delta[n] for n in names],
            *[new_m[n] for n in names], *[new_v[n] for n in names])
```

```python
import functools

import jax
import jax.numpy as jnp
from jax import lax
from jax.experimental import pallas as pl
from jax.experimental.pallas import tpu as pltpu

F32 = jnp.float32
BF16 = jnp.bfloat16

D_MODEL = 2048
SEQ = 8192
N_META = 16
CONV_WIDTH = 3
HEAD_DIM = 64
N_Q_HEADS = 32
N_KV_HEADS = 4
BLOCK = 128
ROPE_THETA = 10000.0
D_FF = 4 * D_MODEL
RMS_EPS = 1e-5
NEG_INF = -1e30

ADAM_LR = 0.001
ADAM_B1 = 0.9
ADAM_B2 = 0.999
ADAM_EPS = 1e-08
ADAM_WD = 0.01
ADAM_STEP = 10

N_CHIPS = 4
N_DEV = 8
MESH = pl.DeviceIdType.MESH
VMEM_LIMIT = 56 * 1024 * 1024
SMALL_ROWS = 32
ROW0 = BLOCK


def _pick(n, target, mult):
    best = None
    for t in range(mult, min(n, target) + 1, mult):
        if n % t == 0:
            best = t
    assert best is not None, (n, target, mult)
    return best


def _params(sem=None):
    return pltpu.CompilerParams(dimension_semantics=sem, vmem_limit_bytes=VMEM_LIMIT)


HBM_SPEC = pl.BlockSpec(memory_space=pltpu.HBM)


class _Task:
    def __init__(self, inputs, outputs, aliases, sem_shapes, bind):
        self.inputs, self.outputs, self.aliases = list(inputs), list(outputs), dict(aliases)
        self.sem_shapes, self.bind = list(sem_shapes), bind


def _like(arrays):
    return [jax.ShapeDtypeStruct(a.shape, a.dtype) for a in arrays]


def _bind_tasks(tasks, in_refs, out_refs, sem_refs):
    bound, i, o, s = [], 0, 0, 0
    for t in tasks:
        ni, no, ns = len(t.inputs), len(t.outputs), len(t.sem_shapes)
        bound.append(t.bind(in_refs[i:i + ni], out_refs[o:o + no], sem_refs[s:s + ns]))
        i, o, s = i + ni, o + no, s + ns
    return bound


def _run_phase(bound, phase):
    for b in bound:
        if b[phase] is not None:
            b[phase]()


def _task_plumbing(tasks, in_offset, out_offset):
    ins = [a for t in tasks for a in t.inputs]
    outs = [o for t in tasks for o in t.outputs]
    sems = [s for t in tasks for s in t.sem_shapes]
    aliases, i, o = {}, in_offset, out_offset
    for t in tasks:
        for src, dst in t.aliases.items():
            aliases[i + src] = o + dst
        i, o = i + len(t.inputs), o + len(t.outputs)
    return ins, outs, sems, aliases


def _split_outputs(tasks, flat):
    res, o = [], 0
    for t in tasks:
        res.append(list(flat[o:o + len(t.outputs)]))
        o += len(t.outputs)
    return res


def _run(name, tasks):
    ins, outs, sems, aliases = _task_plumbing(tasks, 0, 0)

    def body(*refs):
        bound = _bind_tasks(tasks, refs[:len(ins)], refs[len(ins):len(ins) + len(outs)], refs[len(ins) + len(outs):])
        for phase in range(3):
            _run_phase(bound, phase)

    flat = pl.pallas_call(
        body, name=name, out_shape=outs, in_specs=[HBM_SPEC] * len(ins), out_specs=[HBM_SPEC] * len(outs),
        input_output_aliases=aliases, scratch_shapes=sems,
    )(*ins)
    return _split_outputs(tasks, flat)


def _place():
    x, y, c = lax.axis_index("x"), lax.axis_index("y"), lax.axis_index("c")
    chips = [(1 - x, y), (x, 1 - y), (1 - x, 1 - y)]
    return x, y, c, 2 * x + y, chips


def _gather_task(stacks):
    n = len(stacks)
    halves = [s.shape[1] // 2 for s in stacks]

    def bind(_, dst, sems):
        send_a, recv_a, send_b, recv_b = sems
        x, y, c, me, chips = _place()
        sibling = (x, y, 1 - c)

        def half(w, chip, hc):
            return dst[w].at[chip, pl.ds(hc * halves[w], halves[w]), :]

        def over_ici(j, w, block):
            return pltpu.make_async_remote_copy(
                src_ref=half(w, block, c), dst_ref=half(w, block, c), send_sem=send_a.at[j * n + w],
                recv_sem=recv_a.at[j * n + w], device_id=(*chips[j], c), device_id_type=MESH)

        def over_d2d(j, w, hc):
            got = half(w, 2 * chips[j][0] + chips[j][1], hc)
            return pltpu.make_async_remote_copy(
                src_ref=got, dst_ref=got, send_sem=send_b.at[j * n + w], recv_sem=recv_b.at[j * n + w],
                device_id=sibling, device_id_type=MESH)

        pairs = [(j, w) for j in range(3) for w in range(n)]

        def start():
            for j, w in pairs:
                over_ici(j, w, me).start()

        def mid():
            for j, w in pairs:
                over_ici(j, w, 2 * chips[j][0] + chips[j][1]).wait_recv()
                over_d2d(j, w, c).start()

        def finish():
            for j, w in pairs:
                over_d2d(j, w, 1 - c).wait_recv()
            for j, w in pairs:
                over_ici(j, w, me).wait_send()
                over_d2d(j, w, c).wait_send()

        return start, mid, finish

    return _Task(stacks, _like(stacks), {w: w for w in range(n)}, [pltpu.SemaphoreType.DMA((3 * n,))] * 4, bind)


def _pair_exchange_task(grads, small=None):
    n = len(grads)
    halves = [g.shape[1] // 2 for g in grads]

    def bind(src, dst, sems):
        send, recv = sems[0], sems[1]
        x, y, c, me, _ = _place()
        sibling = (x, y, 1 - c)
        dev = 2 * me + c

        def to_sibling(w):
            return pltpu.make_async_remote_copy(
                src_ref=src[w].at[:, pl.ds((1 - c) * halves[w], halves[w]), :], dst_ref=dst[w],
                send_sem=send.at[w], recv_sem=recv.at[w], device_id=sibling, device_id_type=MESH)

        def to_peer(t, block):
            tx, ty, tc = (t >> 2) & 1, (t >> 1) & 1, t & 1
            return pltpu.make_async_remote_copy(
                src_ref=src[n], dst_ref=dst[n].at[block], send_sem=sems[2].at[t], recv_sem=sems[3].at[t],
                device_id=(x ^ tx, y ^ ty, c ^ tc), device_id_type=MESH)

        def mine():
            return pltpu.make_async_copy(src[n], dst[n].at[dev], sems[4])

        def start():
            for w in range(n):
                to_sibling(w).start()
            if small is not None:
                mine().start()
                for t in range(1, N_DEV):
                    to_peer(t, dev).start()

        def finish():
            for w in range(n):
                to_sibling(w).wait_recv()
            if small is not None:
                for t in range(1, N_DEV):
                    to_peer(t, dev ^ t).wait_recv()
            for w in range(n):
                to_sibling(w).wait_send()
            if small is not None:
                for t in range(1, N_DEV):
                    to_peer(t, dev).wait_send()
                mine().wait()

        return start, None, finish

    outputs = [jax.ShapeDtypeStruct((N_CHIPS, h, g.shape[2]), g.dtype) for g, h in zip(grads, halves)]
    sem_shapes = [pltpu.SemaphoreType.DMA((n,)), pltpu.SemaphoreType.DMA((n,))]
    inputs = list(grads)
    if small is not None:
        inputs.append(small)
        outputs.append(jax.ShapeDtypeStruct((N_DEV,) + small.shape, small.dtype))
        sem_shapes += [pltpu.SemaphoreType.DMA((N_DEV,)), pltpu.SemaphoreType.DMA((N_DEV,)), pltpu.SemaphoreType.DMA(())]
    return _Task(inputs, outputs, {}, sem_shapes, bind)


def _chip_exchange_task(summed):
    n = len(summed)

    def bind(refs, dst, sems):
        src = refs[:n]
        send, recv = sems
        x, y, c, me, chips = _place()

        def copy(j, w, block_from, block_to):
            return pltpu.make_async_remote_copy(
                src_ref=src[w].at[block_from], dst_ref=dst[w].at[block_to], send_sem=send.at[j * n + w],
                recv_sem=recv.at[j * n + w], device_id=(*chips[j], c), device_id_type=MESH)

        pairs = [(j, w) for j in range(3) for w in range(n)]

        def start():
            for j, w in pairs:
                copy(j, w, 2 * chips[j][0] + chips[j][1], me).start()

        def finish():
            for j, w in pairs:
                copy(j, w, me, 2 * chips[j][0] + chips[j][1]).wait_recv()
            for j, w in pairs:
                copy(j, w, 2 * chips[j][0] + chips[j][1], me).wait_send()

        return start, None, finish

    partials, landing = [s[0] for s in summed], [s[1] for s in summed]
    return _Task(partials + landing, _like(landing), {n + w: w for w in range(n)},
                 [pltpu.SemaphoreType.DMA((3 * n,))] * 2, bind)


def _pair_share_task(blocks):
    n = len(blocks)

    def bind(_, dst, sems):
        send, recv = sems
        x, y, c, _, _ = _place()

        def copy(w, hc):
            h = blocks[w].shape[0] // 2
            rows = dst[w].at[pl.ds(hc * h, h), :]
            return pltpu.make_async_remote_copy(src_ref=rows, dst_ref=rows, send_sem=send.at[w], recv_sem=recv.at[w],
                                                device_id=(x, y, 1 - c), device_id_type=MESH)

        def start():
            for w in range(n):
                copy(w, c).start()

        def finish():
            for w in range(n):
                copy(w, 1 - c).wait_recv()
            for w in range(n):
                copy(w, c).wait_send()

        return start, None, finish

    return _Task(blocks, _like(blocks), {w: w for w in range(n)}, [pltpu.SemaphoreType.DMA((n,))] * 2, bind)


def _mm(name, a, b, *, dims, grid, a_spec, b_spec, out_shape, out_spec, nk, acc_shape,
        extras=(), extra_specs=(), a_pro=None, epi=None, tasks=()):
    n_ex = len(extras)
    acc_in_out = epi is None and out_shape.dtype == F32
    t_ins, t_outs, t_sems, aliases = _task_plumbing(tasks, 2 + n_ex, 1)
    n_ti, n_to = len(t_ins), len(t_outs)
    has_acc = not (nk == 1 or acc_in_out)
    total = grid[0] * grid[1] * grid[2]
    mid_step = max(0, total - 1 - max(1, total // 8))

    def body(*refs):
        a_ref, b_ref = refs[0], refs[1]
        ex = refs[2:2 + n_ex]
        o_ref = refs[2 + n_ex + n_ti]
        scratch = refs[3 + n_ex + n_ti + n_to:]
        if tasks:
            bound = _bind_tasks(tasks, refs[2 + n_ex:2 + n_ex + n_ti],
                                refs[3 + n_ex + n_ti:3 + n_ex + n_ti + n_to], scratch[int(has_acc):])
            step = (pl.program_id(0) * grid[1] + pl.program_id(1)) * grid[2] + pl.program_id(2)

            @pl.when(step == 0)
            def _():
                _run_phase(bound, 0)

        av = a_ref[...]
        if a_pro is not None:
            av = a_pro(av)
        part = lax.dot_general(av, b_ref[...], dims, preferred_element_type=F32)

        def finish(acc):
            r = acc if epi is None else epi(acc, *[e[...] for e in ex])
            o_ref[...] = r.astype(o_ref.dtype)

        if nk == 1:
            finish(part)
        else:
            acc_ref = scratch[0] if has_acc else o_ref
            kk = pl.program_id(2)

            @pl.when(kk == 0)
            def _():
                acc_ref[...] = part

            @pl.when(kk > 0)
            def _():
                acc_ref[...] += part

            if has_acc:
                @pl.when(kk == nk - 1)
                def _():
                    finish(acc_ref[...])

        if tasks:
            @pl.when(step == mid_step)
            def _():
                _run_phase(bound, 1)

            @pl.when(step == total - 1)
            def _():
                _run_phase(bound, 2)

    scratch_shapes = ([pltpu.VMEM(acc_shape, F32)] if has_acc else []) + t_sems
    sem = ("arbitrary",) * 3 if tasks else ("parallel", "parallel", "arbitrary")
    res = pl.pallas_call(
        body, name=name, grid=grid, out_shape=[out_shape, *t_outs],
        in_specs=[a_spec, b_spec, *extra_specs, *[HBM_SPEC] * n_ti], out_specs=[out_spec, *[HBM_SPEC] * n_to],
        input_output_aliases=aliases, scratch_shapes=scratch_shapes, compiler_params=_params(sem),
    )(a, b, *extras, *t_ins)
    return (res[0], _split_outputs(tasks, res[1:])) if tasks else res[0]


_NN = (((1,), (0,)), ((), ()))
_NT = (((1,), (1,)), ((), ()))
_TN = (((0,), (0,)), ((), ()))


def _mm_nn_cols(name, a, w, out_dtype, epi=None, extras=(), tasks=()):
    m, k = a.shape
    _, _, ns = w.shape
    tm, tn = _pick(m, 832, 16), _pick(ns, 1024, 128)
    per = ns // tn
    tile = pl.BlockSpec((tm, tn), lambda j, i, kk: (i, j))
    return _mm(name, a, w, dims=_NN, grid=(N_CHIPS * per, m // tm, 1),
               a_spec=pl.BlockSpec((tm, k), lambda j, i, kk: (i, 0)),
               b_spec=pl.BlockSpec((None, k, tn), lambda j, i, kk: (j // per, 0, j % per)),
               out_shape=jax.ShapeDtypeStruct((m, N_CHIPS * ns), out_dtype), out_spec=tile,
               nk=1, acc_shape=(tm, tn), extras=extras, extra_specs=[tile] * len(extras), epi=epi, tasks=tasks)


def _mm_nn_rows(name, a, w, out_dtype, a_pro=None, epi=None, extras=(), tasks=()):
    m, k = a.shape
    _, n = w.shape
    tk = k
    tm, tn = (_pick(m, 832, 16), _pick(n, 1024, 128)) if k <= 2048 else (_pick(m, 416, 16), _pick(n, 512, 128))
    tile = pl.BlockSpec((tm, tn), lambda j, i, kk: (i, j))
    return _mm(name, a, w, dims=_NN, grid=(n // tn, m // tm, k // tk),
               a_spec=pl.BlockSpec((tm, tk), lambda j, i, kk: (i, kk)),
               b_spec=pl.BlockSpec((tk, tn), lambda j, i, kk: (kk, j)),
               out_shape=jax.ShapeDtypeStruct((m, n), out_dtype), out_spec=tile,
               nk=k // tk, acc_shape=(tm, tn), extras=extras, extra_specs=[tile] * len(extras),
               a_pro=a_pro, epi=epi, tasks=tasks)


def _mm_nt_rows(name, a, w, out_dtype, epi=None, extras=(), tasks=()):
    m, c = a.shape
    r, _ = w.shape
    tm, tn = _pick(m, 832, 16), _pick(r, 1024, 128)
    tile = pl.BlockSpec((tm, tn), lambda j, i, kk: (i, j))
    return _mm(name, a, w, dims=_NT, grid=(r // tn, m // tm, 1),
               a_spec=pl.BlockSpec((tm, c), lambda j, i, kk: (i, 0)),
               b_spec=pl.BlockSpec((tn, c), lambda j, i, kk: (j, 0)),
               out_shape=jax.ShapeDtypeStruct((m, r), out_dtype), out_spec=tile,
               nk=1, acc_shape=(tm, tn), extras=extras, extra_specs=[tile] * len(extras), epi=epi, tasks=tasks)


def _mm_nt_cols(name, a, w, out_dtype, tasks=()):
    m, _ = a.shape
    _, r, ns = w.shape
    tm, tn, tk = _pick(m, 832, 16), _pick(r, 1024, 128), _pick(ns, 2048, 128)
    per = ns // tk
    return _mm(name, a, w, dims=_NT, grid=(r // tn, m // tm, N_CHIPS * per),
               a_spec=pl.BlockSpec((tm, tk), lambda j, i, kk: (i, kk)),
               b_spec=pl.BlockSpec((None, tn, tk), lambda j, i, kk: (kk // per, j, kk % per)),
               out_shape=jax.ShapeDtypeStruct((m, r), out_dtype),
               out_spec=pl.BlockSpec((tm, tn), lambda j, i, kk: (i, j)),
               nk=N_CHIPS * per, acc_shape=(tm, tn), tasks=tasks)


def _mm_tn(name, a, b, stacked, a_pro=None, tasks=()):
    t, ka = a.shape
    _, nb = b.shape
    ns = nb // N_CHIPS if stacked else nb
    tt, ta, tb = t, _pick(ka, 512, 128), _pick(ns, 640, 128)
    if stacked:
        per = ns // tb
        out_shape = jax.ShapeDtypeStruct((N_CHIPS, ka, ns), F32)
        out_spec = pl.BlockSpec((None, ta, tb), lambda i, j, kk: (j // per, i, j % per))
    else:
        out_shape = jax.ShapeDtypeStruct((ka, nb), F32)
        out_spec = pl.BlockSpec((ta, tb), lambda i, j, kk: (i, j))
    return _mm(name, a, b, dims=_TN, grid=(ka // ta, nb // tb, t // tt),
               a_spec=pl.BlockSpec((tt, ta), lambda i, j, kk: (kk, i)),
               b_spec=pl.BlockSpec((tt, tb), lambda i, j, kk: (kk, j)),
               out_shape=out_shape, out_spec=out_spec, nk=t // tt, acc_shape=(ta, tb), a_pro=a_pro, tasks=tasks)


def _relu_sq(z):
    a = jnp.maximum(z, 0)
    return a * a


def _rms_fwd(name, h, g):
    m, d = h.shape
    tr = _pick(m, 256, 16)

    def body(h_ref, g_ref, o_ref):
        x = h_ref[...]
        rstd = lax.rsqrt(jnp.mean(x * x, axis=-1, keepdims=True) + RMS_EPS)
        o_ref[...] = ((x * rstd) * g_ref[...]).astype(BF16)

    row = pl.BlockSpec((tr, d), lambda i: (i, 0))
    return pl.pallas_call(
        body, name=name, grid=(m // tr,), out_shape=jax.ShapeDtypeStruct((m, d), BF16),
        in_specs=[row, pl.BlockSpec((1, d), lambda i: (0, 0))], out_specs=row,
        compiler_params=_params(("parallel",)),
    )(h, g.reshape(1, d))


def _rms_bwd_math(x, g, dn):
    rstd = lax.rsqrt(jnp.mean(x * x, axis=-1, keepdims=True) + RMS_EPS)
    xhat = x * rstd
    dxhat = dn * g
    dx = rstd * (dxhat - xhat * jnp.mean(dxhat * xhat, axis=-1, keepdims=True))
    return dx, dn * xhat


def _fold8(v):
    r, c = v.shape
    return jnp.sum(v.reshape(r // 8, 8, c), axis=0)


def _rms_bwd(name, dn, h, g, dh_in):
    m, d = h.shape
    tr = _pick(m, 256, 16)
    nt = m // tr

    def body(dn_ref, h_ref, g_ref, dh_ref, o_ref, ob_ref, dg_ref):
        dx, dgp = _rms_bwd_math(h_ref[...], g_ref[...], dn_ref[...])
        dh = dh_ref[...] + dx
        o_ref[...] = dh
        ob_ref[...] = dh.astype(BF16)

        @pl.when(pl.program_id(0) == 0)
        def _():
            dg_ref[...] = jnp.zeros_like(dg_ref)

        dg_ref[...] += _fold8(dgp)

    row = pl.BlockSpec((tr, d), lambda i: (i, 0))
    return pl.pallas_call(
        body, name=name, grid=(nt,),
        out_shape=(jax.ShapeDtypeStruct((m, d), F32), jax.ShapeDtypeStruct((m, d), BF16),
                   jax.ShapeDtypeStruct((8, d), F32)),
        in_specs=[row, row, pl.BlockSpec((1, d), lambda i: (0, 0)), row],
        out_specs=(row, row, pl.BlockSpec((8, d), lambda i: (0, 0))),
        compiler_params=_params(("arbitrary",)),
    )(dn, h, g.reshape(1, d), dh_in)


def _loss_head(h, g, target):
    m, d = h.shape
    tr = BLOCK

    def body(h_ref, g_ref, t_ref, loss_ref, o_ref, ob_ref, dg_ref):
        i = pl.program_id(0)
        x = h_ref[...]
        gv = g_ref[...]
        rstd = lax.rsqrt(jnp.mean(x * x, axis=-1, keepdims=True) + RMS_EPS)
        err = jnp.where(i > 0, (x * rstd) * gv - t_ref[...], 0.0)
        dx, dgp = _rms_bwd_math(x, gv, err * (1.0 / d))
        o_ref[...] = dx
        ob_ref[...] = dx.astype(BF16)

        @pl.when(i == 0)
        def _():
            dg_ref[...] = jnp.zeros_like(dg_ref)
            loss_ref[...] = jnp.zeros_like(loss_ref)

        dg_ref[...] += _fold8(dgp)
        sq = jnp.mean(err * err, axis=-1, keepdims=True)
        loss_ref[...] += 0.5 * jnp.sum(sq, axis=0, keepdims=True)

    row = pl.BlockSpec((tr, d), lambda i: (i, 0))
    return pl.pallas_call(
        body, name="loss_head", grid=(m // tr,),
        out_shape=(jax.ShapeDtypeStruct((8, 128), F32), jax.ShapeDtypeStruct((m, d), F32),
                   jax.ShapeDtypeStruct((m, d), BF16), jax.ShapeDtypeStruct((8, d), F32)),
        in_specs=[row, pl.BlockSpec((1, d), lambda i: (0, 0)),
                  pl.BlockSpec((tr, d), lambda i: (jnp.maximum(i - 1, 0), 0))],
        out_specs=(pl.BlockSpec((8, 128), lambda i: (0, 0)), row, row,
                   pl.BlockSpec((8, d), lambda i: (0, 0))),
        compiler_params=_params(("arbitrary",)),
    )(h, g.reshape(1, d), target)


HALO = 16


def _shift_down(cat, k):
    return pltpu.roll(cat, k, axis=0)[HALO:]


def _shift_up(cat, k):
    n = cat.shape[0]
    return pltpu.roll(cat, n - k, axis=0)[:n - HALO]


def _conv_fwd(bcu, cw):
    m, d3 = bcu.shape
    d = d3 // 3
    tr, tc = _pick(m, 416, 16), _pick(d, 512, 128)
    nd, hb = d // tc, tr // HALO

    def body(b_ref, c_ref, u_ref, ch_ref, uh_ref, w_ref, o_ref):
        i = pl.program_id(0)
        v = c_ref[...].astype(F32) * u_ref[...].astype(F32)
        vh = jnp.where(i > 0, ch_ref[...].astype(F32) * uh_ref[...].astype(F32), 0.0)
        cat = jnp.concatenate([vh, v], axis=0)
        w = w_ref[...]
        conv = w[2:3] * v + w[1:2] * _shift_down(cat, 1) + w[0:1] * _shift_down(cat, 2)
        o_ref[...] = (b_ref[...].astype(F32) * conv).astype(BF16)

    def part(p):
        return pl.BlockSpec((tr, tc), lambda i, j: (i, p * nd + j))

    def halo(p):
        return pl.BlockSpec((HALO, tc), lambda i, j: (jnp.maximum(i * hb - 1, 0), p * nd + j))

    return pl.pallas_call(
        body, name="conv_fwd", grid=(m // tr, nd), out_shape=jax.ShapeDtypeStruct((m, d), BF16),
        in_specs=[part(0), part(1), part(2), halo(1), halo(2), pl.BlockSpec((8, tc), lambda i, j: (0, j))],
        out_specs=pl.BlockSpec((tr, tc), lambda i, j: (i, j)),
        compiler_params=_params(("parallel", "parallel")),
    )(bcu, bcu, bcu, bcu, bcu, cw)


def _conv_bwd(bcu, cw, dg):
    m, d3 = bcu.shape
    d = d3 // 3
    tr, tc = _pick(m, 416, 16), _pick(d, 512, 128)
    nd, hb, nt = d // tc, tr // HALO, m // tr

    def body(b_ref, c_ref, u_ref, ch_ref, uh_ref, bn_ref, dg_ref, dgn_ref, w_ref, o_ref, dw_ref, dcu_ref):
        i, p = pl.program_id(1), pl.program_id(2)

        @pl.when(p == 0)
        def _():
            w = w_ref[...]
            b, c, u = b_ref[...].astype(F32), c_ref[...].astype(F32), u_ref[...].astype(F32)
            dgv = dg_ref[...].astype(F32)
            v = c * u
            vh = jnp.where(i > 0, ch_ref[...].astype(F32) * uh_ref[...].astype(F32), 0.0)
            cat = jnp.concatenate([vh, v], axis=0)
            v1, v2 = _shift_down(cat, 1), _shift_down(cat, 2)
            dconv = dgv * b
            o_ref[...] = (dgv * (w[2:3] * v + w[1:2] * v1 + w[0:1] * v2)).astype(BF16)

            @pl.when(i == 0)
            def _():
                dw_ref[...] = jnp.zeros_like(dw_ref)

            taps = [jnp.sum(dconv * t, axis=0, keepdims=True) for t in (v2, v1, v)]
            dw_ref[...] += jnp.concatenate(taps + [jnp.zeros((5, tc), F32)], axis=0)
            nxt = jnp.where(i < nt - 1, dgn_ref[...].astype(F32) * bn_ref[...].astype(F32), 0.0)
            cat2 = jnp.concatenate([dconv, nxt], axis=0)
            dv = w[2:3] * dconv + w[1:2] * _shift_up(cat2, 1) + w[0:1] * _shift_up(cat2, 2)
            dcu_ref[0] = (dv * u).astype(BF16)
            dcu_ref[1] = (dv * c).astype(BF16)

        @pl.when(p > 0)
        def _():
            o_ref[...] = dcu_ref[p - 1]

    def part(q):
        return pl.BlockSpec((tr, tc), lambda j, i, p: (i, q * nd + j))

    def before(q):
        return pl.BlockSpec((HALO, tc), lambda j, i, p: (jnp.maximum(i * hb - 1, 0), q * nd + j))

    def after(q):
        return pl.BlockSpec((HALO, tc), lambda j, i, p: (jnp.minimum((i + 1) * hb, m // HALO - 1), q * nd + j))

    return pl.pallas_call(
        body, name="conv_bwd", grid=(nd, nt, 3),
        out_shape=(jax.ShapeDtypeStruct((m, d3), BF16), jax.ShapeDtypeStruct((8, d), F32)),
        in_specs=[part(0), part(1), part(2), before(1), before(2), after(0),
                  pl.BlockSpec((tr, tc), lambda j, i, p: (i, j)), after(0),
                  pl.BlockSpec((8, tc), lambda j, i, p: (0, j))],
        out_specs=(pl.BlockSpec((tr, tc), lambda j, i, p: (i, p * nd + j)),
                   pl.BlockSpec((8, tc), lambda j, i, p: (0, j))),
        scratch_shapes=[pltpu.VMEM((2, tr, tc), BF16)],
        compiler_params=_params(("parallel", "arbitrary", "arbitrary")),
    )(bcu, bcu, bcu, bcu, bcu, bcu, dg, dg, cw)


def _rope_tables(m):
    pad = ROW0 - N_META
    pos = jnp.arange(m, dtype=F32) - pad
    inv = ROPE_THETA ** (-jnp.arange(0, HEAD_DIM, 2, dtype=F32) / HEAD_DIM)
    ang = pos[:, None] * inv[None, :]
    return jnp.cos(ang), jnp.sin(ang)


def _rope(x, c, s):
    half = HEAD_DIM // 2
    x1, x2 = x[:, :half], x[:, half:]
    return jnp.concatenate([x1 * c - x2 * s, x2 * c + x1 * s], axis=-1)


def _rope_t(y, c, s):
    half = HEAD_DIM // 2
    y1, y2 = y[:, :half], y[:, half:]
    return jnp.concatenate([y1 * c + y2 * s, y2 * c - y1 * s], axis=-1)


def _qkv_split(qkv, cos, sin):
    m = qkv.shape[0]
    nb, grp = m // BLOCK, N_Q_HEADS // N_KV_HEADS
    scale = HEAD_DIM ** -0.5

    def body(x_ref, c_ref, s_ref, q_ref, k_ref, v_ref):
        c, s = c_ref[...], s_ref[...]
        for h in range(N_KV_HEADS):
            for g in range(grp):
                col = (h * grp + g) * HEAD_DIM
                xq = x_ref[:, col:col + HEAD_DIM].astype(F32)
                q_ref[h, g * BLOCK:(g + 1) * BLOCK, :] = (_rope(xq, c, s) * scale).astype(BF16)
            col = (N_Q_HEADS + h) * HEAD_DIM
            k_ref[h] = _rope(x_ref[:, col:col + HEAD_DIM].astype(F32), c, s).astype(BF16)
            col = (N_Q_HEADS + N_KV_HEADS + h) * HEAD_DIM
            v_ref[h] = x_ref[:, col:col + HEAD_DIM]

    tab = pl.BlockSpec((BLOCK, HEAD_DIM // 2), lambda i: (i, 0))
    kv = pl.BlockSpec((N_KV_HEADS, BLOCK, HEAD_DIM), lambda i: (0, i, 0))
    return pl.pallas_call(
        body, name="qkv_split", grid=(nb,),
        out_shape=(jax.ShapeDtypeStruct((N_KV_HEADS, nb * grp * BLOCK, HEAD_DIM), BF16),
                   jax.ShapeDtypeStruct((N_KV_HEADS, m, HEAD_DIM), BF16),
                   jax.ShapeDtypeStruct((N_KV_HEADS, m, HEAD_DIM), BF16)),
        in_specs=[pl.BlockSpec((BLOCK, qkv.shape[1]), lambda i: (i, 0)), tab, tab],
        out_specs=(pl.BlockSpec((N_KV_HEADS, grp * BLOCK, HEAD_DIM), lambda i: (0, i, 0)), kv, kv),
        compiler_params=_params(("parallel",)),
    )(qkv, cos, sin)


def _heads_merge(name, o):
    grp = N_Q_HEADS // N_KV_HEADS
    nb = o.shape[1] // (grp * BLOCK)

    def body(o_ref, x_ref):
        for h in range(N_KV_HEADS):
            for g in range(grp):
                col = (h * grp + g) * HEAD_DIM
                x_ref[:, col:col + HEAD_DIM] = o_ref[h, g * BLOCK:(g + 1) * BLOCK, :]

    return pl.pallas_call(
        body, name=name, grid=(nb,),
        out_shape=jax.ShapeDtypeStruct((nb * BLOCK, N_Q_HEADS * HEAD_DIM), o.dtype),
        in_specs=[pl.BlockSpec((N_KV_HEADS, grp * BLOCK, HEAD_DIM), lambda i: (0, i, 0))],
        out_specs=pl.BlockSpec((BLOCK, N_Q_HEADS * HEAD_DIM), lambda i: (i, 0)),
        compiler_params=_params(("parallel",)),
    )(o)


def _heads_split(name, x):
    grp = N_Q_HEADS // N_KV_HEADS
    nb = x.shape[0] // BLOCK

    def body(x_ref, o_ref):
        for h in range(N_KV_HEADS):
            for g in range(grp):
                col = (h * grp + g) * HEAD_DIM
                o_ref[h, g * BLOCK:(g + 1) * BLOCK, :] = x_ref[:, col:col + HEAD_DIM]

    return pl.pallas_call(
        body, name=name, grid=(nb,),
        out_shape=jax.ShapeDtypeStruct((N_KV_HEADS, nb * grp * BLOCK, HEAD_DIM), x.dtype),
        in_specs=[pl.BlockSpec((BLOCK, N_Q_HEADS * HEAD_DIM), lambda i: (i, 0))],
        out_specs=pl.BlockSpec((N_KV_HEADS, grp * BLOCK, HEAD_DIM), lambda i: (0, i, 0)),
        compiler_params=_params(("parallel",)),
    )(x)


def _attn_mask(i):
    r = lax.broadcasted_iota(jnp.int32, (BLOCK, 2 * BLOCK), 0)
    cidx = lax.broadcasted_iota(jnp.int32, (BLOCK, 2 * BLOCK), 1)
    key = (i - 1) * BLOCK + cidx
    return (cidx > r) & (cidx <= r + BLOCK) & (key >= ROW0 - N_META)


def _attn_exp(allowed, q, kb, vb1, sink):
    rows = q.shape[0]
    grp = rows // BLOCK
    s = lax.dot_general(q, kb, _NT, preferred_element_type=F32)
    s = jnp.where(allowed[None], s.reshape(grp, BLOCK, 2 * BLOCK), NEG_INF).reshape(rows, 2 * BLOCK)
    mx = jnp.maximum(jnp.max(s, axis=-1, keepdims=True), sink)
    eb = jnp.exp(s - mx).astype(BF16)
    es = jnp.exp(sink - mx)
    ov = jnp.dot(eb, vb1, preferred_element_type=F32)
    inv = 1.0 / (ov[:, HEAD_DIM:HEAD_DIM + 1] + es)
    return eb, ov[:, :HEAD_DIM], inv, es


def _band(prev_ref, cur_ref, h):
    return jnp.concatenate([prev_ref[h], cur_ref[h]], axis=0)


def _with_ones(vb):
    return jnp.concatenate([vb, jnp.ones_like(vb)], axis=1)


def _attn_specs(grp):
    q = pl.BlockSpec((N_KV_HEADS, grp * BLOCK, HEAD_DIM), lambda i: (0, i, 0))
    cur = pl.BlockSpec((N_KV_HEADS, BLOCK, HEAD_DIM), lambda i: (0, i, 0))
    prev = pl.BlockSpec((N_KV_HEADS, BLOCK, HEAD_DIM), lambda i: (0, jnp.maximum(i - 1, 0), 0))
    sink = pl.BlockSpec((N_KV_HEADS, grp * BLOCK, 1), lambda i: (0, 0, 0))
    return q, cur, prev, sink


def _attn_fwd(q, k, v, sink_rows):
    grp = N_Q_HEADS // N_KV_HEADS
    nb = k.shape[1] // BLOCK

    def body(q_ref, kc_ref, kp_ref, vc_ref, vp_ref, s_ref, o_ref):
        allowed = _attn_mask(pl.program_id(0))
        for h in range(N_KV_HEADS):
            vb1 = _with_ones(_band(vp_ref, vc_ref, h))
            _, ov, inv, _ = _attn_exp(allowed, q_ref[h], _band(kp_ref, kc_ref, h), vb1, s_ref[h])
            o_ref[h] = (ov * inv).astype(BF16)

    qs, cur, prev, sink = _attn_specs(grp)
    return pl.pallas_call(
        body, name="attn_fwd", grid=(nb,), out_shape=jax.ShapeDtypeStruct(q.shape, BF16),
        in_specs=[qs, cur, prev, cur, prev, sink], out_specs=qs,
        compiler_params=_params(("parallel",)),
    )(q, k, k, v, v, sink_rows)


def _attn_bwd(q, k, v, sink_rows, do):
    grp = N_Q_HEADS // N_KV_HEADS
    nb = k.shape[1] // BLOCK

    def body(q_ref, kc_ref, kp_ref, vc_ref, vp_ref, s_ref, do_ref, dq_ref, dk_ref, dv_ref, ds_ref):
        i = pl.program_id(0)
        allowed = _attn_mask(i)

        @pl.when(i == 0)
        def _():
            ds_ref[...] = jnp.zeros_like(ds_ref)

        for h in range(N_KV_HEADS):
            qv, dov = q_ref[h], do_ref[h]
            kb, vb = _band(kp_ref, kc_ref, h), _band(vp_ref, vc_ref, h)
            eb, ov, inv, es = _attn_exp(allowed, qv, kb, _with_ones(vb), s_ref[h])
            dof = dov.astype(F32)
            delta = jnp.sum(dof * (ov * inv), axis=-1, keepdims=True)
            dp = lax.dot_general(dov, vb, _NT, preferred_element_type=F32)
            dsb = (eb.astype(F32) * (inv * (dp - delta))).astype(BF16)
            dq_ref[h] = jnp.dot(dsb, kb, preferred_element_type=F32).astype(BF16)
            dk_ref[h] = lax.dot_general(dsb, qv, _TN, preferred_element_type=F32)
            dv_ref[h] = lax.dot_general(eb, (dof * inv).astype(BF16), _TN, preferred_element_type=F32)
            ds_ref[h] -= (es * inv) * delta

    qs, cur, prev, sink = _attn_specs(grp)
    band = pl.BlockSpec((N_KV_HEADS, None, 2 * BLOCK, HEAD_DIM), lambda i: (0, i, 0, 0))
    band_shape = jax.ShapeDtypeStruct((N_KV_HEADS, nb, 2 * BLOCK, HEAD_DIM), F32)
    return pl.pallas_call(
        body, name="attn_bwd", grid=(nb,),
        out_shape=(jax.ShapeDtypeStruct(q.shape, BF16), band_shape, band_shape,
                   jax.ShapeDtypeStruct(sink_rows.shape, F32)),
        in_specs=[qs, cur, prev, cur, prev, sink, qs], out_specs=(qs, band, band, sink),
        compiler_params=_params(("arbitrary",)),
    )(q, k, k, v, v, sink_rows, do)


def _qkv_merge_bwd(dq, dkb, dvb, cos, sin):
    grp = N_Q_HEADS // N_KV_HEADS
    nb = dkb.shape[1]
    width = (N_Q_HEADS + 2 * N_KV_HEADS) * HEAD_DIM
    scale = HEAD_DIM ** -0.5

    def body(dq_ref, kc_ref, kn_ref, vc_ref, vn_ref, c_ref, s_ref, o_ref):
        last = pl.program_id(0) == nb - 1
        c, s = c_ref[...], s_ref[...]
        for h in range(N_KV_HEADS):
            for g in range(grp):
                col = (h * grp + g) * HEAD_DIM
                y = dq_ref[h, g * BLOCK:(g + 1) * BLOCK, :].astype(F32) * scale
                o_ref[:, col:col + HEAD_DIM] = _rope_t(y, c, s).astype(BF16)
            dk = kc_ref[h, BLOCK:, :] + jnp.where(last, 0.0, kn_ref[h, :BLOCK, :])
            col = (N_Q_HEADS + h) * HEAD_DIM
            o_ref[:, col:col + HEAD_DIM] = _rope_t(dk, c, s).astype(BF16)
            dv = vc_ref[h, BLOCK:, :] + jnp.where(last, 0.0, vn_ref[h, :BLOCK, :])
            col = (N_Q_HEADS + N_KV_HEADS + h) * HEAD_DIM
            o_ref[:, col:col + HEAD_DIM] = dv.astype(BF16)

    tab = pl.BlockSpec((BLOCK, HEAD_DIM // 2), lambda i: (i, 0))
    cur = pl.BlockSpec((N_KV_HEADS, None, 2 * BLOCK, HEAD_DIM), lambda i: (0, i, 0, 0))
    nxt = pl.BlockSpec((N_KV_HEADS, None, 2 * BLOCK, HEAD_DIM), lambda i: (0, jnp.minimum(i + 1, nb - 1), 0, 0))
    return pl.pallas_call(
        body, name="qkv_merge_bwd", grid=(nb,), out_shape=jax.ShapeDtypeStruct((nb * BLOCK, width), BF16),
        in_specs=[pl.BlockSpec((N_KV_HEADS, grp * BLOCK, HEAD_DIM), lambda i: (0, i, 0)),
                  cur, nxt, cur, nxt, tab, tab],
        out_specs=pl.BlockSpec((BLOCK, width), lambda i: (i, 0)),
        compiler_params=_params(("parallel",)),
    )(dq, dkb, dkb, dvb, dvb, cos, sin)


def _tiles2d(r, c):
    tc = _pick(c, 2048, 128) if c % 128 == 0 else c
    tr = _pick(r, max(8, (1 << 20) // tc // 8 * 8), 8) if r % 8 == 0 else r
    return tr, tc


def _cast_bf16(name, w, place):
    r, c = w.shape
    tr, tc = _tiles2d(r, c)
    if tr % 16:
        tr = r

    def body(place_ref, w_ref, o_ref):
        o_ref[...] = w_ref[...].astype(BF16)

    return pl.pallas_call(
        body, name=name, out_shape=jax.ShapeDtypeStruct((N_CHIPS, r, c), BF16),
        grid_spec=pltpu.PrefetchScalarGridSpec(
            num_scalar_prefetch=1, grid=(r // tr, c // tc),
            in_specs=[pl.BlockSpec((tr, tc), lambda i, j, p: (i, j))],
            out_specs=pl.BlockSpec((None, tr, tc), lambda i, j, p: (p[1], i, j))),
        compiler_params=_params(("parallel", "parallel")),
    )(place, w)


def _pair_sum(name, g, got, place):
    n, r, c = g.shape
    half = r // 2
    tr, tc = _tiles2d(half, c)
    nh = half // tr

    def body(place_ref, g_ref, got_ref, o_ref, own_ref):
        s = (g_ref[...] + got_ref[...]).astype(BF16)
        o_ref[...] = s

        @pl.when(pl.program_id(2) == place_ref[1])
        def _():
            own_ref[...] = s

    tile = pl.BlockSpec((None, tr, tc), lambda i, j, k, p: (k, i, j))
    shape = jax.ShapeDtypeStruct((n, half, c), BF16)
    return pl.pallas_call(
        body, name=name, out_shape=(shape, shape),
        grid_spec=pltpu.PrefetchScalarGridSpec(
            num_scalar_prefetch=1, grid=(nh, c // tc, n),
            in_specs=[pl.BlockSpec((None, tr, tc), lambda i, j, k, p: (k, p[0] * nh + i, j)), tile],
            out_specs=(tile, pl.BlockSpec((None, tr, tc), lambda i, j, k, p: (p[1], i, j)))),
        compiler_params=_params(("parallel", "parallel", "arbitrary")),
    )(place, g, got)


def _chip_sum(name, parts, place):
    n, half, c = parts.shape
    tr, tc = _tiles2d(half, c)
    nh = half // tr

    def body(place_ref, p0, p1, p2, p3, o_ref):
        o_ref[...] = ((p0[...].astype(F32) + p1[...].astype(F32)) + p2[...].astype(F32)) + p3[...].astype(F32)

    def chip(k):
        return pl.BlockSpec((None, tr, tc), lambda i, j, p: (k, i, j))

    return pl.pallas_call(
        body, name=name, out_shape=jax.ShapeDtypeStruct((2 * half, c), F32),
        grid_spec=pltpu.PrefetchScalarGridSpec(
            num_scalar_prefetch=1, grid=(nh, c // tc),
            in_specs=[chip(k) for k in range(n)],
            out_specs=pl.BlockSpec((tr, tc), lambda i, j, p: (p[0] * nh + i, j))),
        compiler_params=_params(("parallel", "parallel")),
    )(place, parts, parts, parts, parts)


def _dev_sum(gathered):
    def body(g_ref, o_ref):
        acc = g_ref[0]
        for k in range(1, N_DEV):
            acc = acc + g_ref[k]
        o_ref[...] = acc

    return pl.pallas_call(body, name="dev_sum", out_shape=jax.ShapeDtypeStruct(gathered.shape[1:], F32))(gathered)


def _adamw(name, w, g, m, v):
    r, c = w.shape
    tr, tc = _tiles2d(r, c)
    if r % 8 == 0:
        tr = _pick(r, max(8, (1 << 18) // tc // 8 * 8), 8)

    def body(w_ref, g_ref, m_ref, v_ref, d_ref, mo_ref, vo_ref):
        gv = g_ref[...]
        mn = ADAM_B1 * m_ref[...] + (1.0 - ADAM_B1) * gv
        vn = ADAM_B2 * v_ref[...] + (1.0 - ADAM_B2) * jnp.square(gv)
        m_hat = mn / (1.0 - ADAM_B1 ** ADAM_STEP)
        v_hat = vn / (1.0 - ADAM_B2 ** ADAM_STEP)
        d_ref[...] = -ADAM_LR * (m_hat / (jnp.sqrt(v_hat) + ADAM_EPS) + ADAM_WD * w_ref[...])
        mo_ref[...] = mn
        vo_ref[...] = vn

    tile = pl.BlockSpec((tr, tc), lambda i, j: (i, j))
    shape = jax.ShapeDtypeStruct((r, c), F32)
    return pl.pallas_call(
        body, name=name, grid=(r // tr, c // tc), out_shape=(shape, shape, shape),
        in_specs=[tile] * 4, out_specs=(tile,) * 3, compiler_params=_params(("parallel", "parallel")),
    )(w, g, m, v)


MATRICES = ("w_in_conv", "w_out_conv", "w_up_0", "w_down_0", "w_qkv", "w_o", "w_up_1", "w_down_1")
COLUMN_SHARDED = ("w_in_conv", "w_up_0", "w_qkv", "w_up_1")
NORMS = ("norm_mix_0", "norm_mlp_0", "norm_mix_1", "norm_mlp_1", "norm_final")


def _rows(stack):
    return stack.reshape(N_CHIPS * stack.shape[1], stack.shape[2])


def _stack(full):
    return full.reshape(N_CHIPS, full.shape[0] // N_CHIPS, full.shape[1])


def _add_residual(acc, res):
    return acc + res


def _relu_sq_grad(acc, z):
    return acc * (2.0 * jnp.maximum(z.astype(F32), 0.0))


def _step(x, target, stacks, small, norms, sinks, place):
    d = D_MODEL
    dc = d // N_CHIPS
    pad = ROW0 - N_META
    m = x.shape[0] + ROW0
    grp = N_Q_HEADS // N_KV_HEADS
    cos, sin = _rope_tables(m)
    sink_rows = jnp.repeat(sinks.astype(F32), BLOCK).reshape(N_KV_HEADS, grp * BLOCK, 1)

    def gather(*names):
        return _gather_task([stacks[n] for n in names])

    def pair_sum(tag, grad, got):
        return _pair_sum("pair_sum_" + tag, grad, got, place)

    def chip_sum(tag, landed):
        return _chip_sum("chip_sum_" + tag, landed, place)

    (w_in, small_all), = _run("gather_first", [_gather_task([stacks["w_in_conv"], small])])
    small_full = jnp.transpose(small_all, (1, 0, 2)).reshape(SMALL_ROWS, d)
    conv_w8 = small_full[N_META:N_META + 8]
    h0 = jnp.concatenate([jnp.zeros((pad, d), F32), small_full[:N_META], x], axis=0)

    n0 = _rms_fwd("norm_mix_0", h0, norms["norm_mix_0"])
    bcu, ((w_out, w_up0),) = _mm_nn_cols("conv_in", n0, w_in, BF16, tasks=[gather("w_out_conv", "w_up_0")])
    gate = _conv_fwd(bcu, conv_w8)
    h1 = _mm_nn_rows("conv_out", gate, _rows(w_out), F32, epi=_add_residual, extras=(h0,))
    n1 = _rms_fwd("norm_mlp_0", h1, norms["norm_mlp_0"])
    z0, ((w_down0,),) = _mm_nn_cols("mlp_up_0", n1, w_up0, BF16, tasks=[gather("w_down_0")])
    h2, ((w_qkv, w_o, w_up1),) = _mm_nn_rows("mlp_down_0", z0, _rows(w_down0), F32, a_pro=_relu_sq, epi=_add_residual,
                                             extras=(h1,), tasks=[gather("w_qkv", "w_o", "w_up_1")])
    n2 = _rms_fwd("norm_mix_1", h2, norms["norm_mix_1"])
    qkv = _mm_nn_cols("attn_qkv", n2, w_qkv, BF16)
    q, k, v = _qkv_split(qkv, cos, sin)
    o = _heads_merge("attn_o_merge", _attn_fwd(q, k, v, sink_rows))
    h3 = _mm_nn_rows("attn_out", o, _rows(w_o), F32, epi=_add_residual, extras=(h2,))
    n3 = _rms_fwd("norm_mlp_1", h3, norms["norm_mlp_1"])
    z1, ((w_down1,),) = _mm_nn_cols("mlp_up_1", n3, w_up1, BF16, tasks=[gather("w_down_1")])
    h4 = _mm_nn_rows("mlp_down_1", z1, _rows(w_down1), F32, a_pro=_relu_sq, epi=_add_residual, extras=(h3,))

    gn = {}
    loss, dh, dh_bf, gn["norm_final"] = _loss_head(h4, norms["norm_final"], target)
    dz = _mm_nt_rows("mlp_down_dx_1", dh_bf, _rows(w_down1), BF16, epi=_relu_sq_grad, extras=(z1,))
    g_d1 = _stack(_mm_tn("mlp_down_dw_1", z1, dh_bf, stacked=False, a_pro=_relu_sq))
    g_u1, ((got,),) = _mm_tn("mlp_up_dw_1", n3, dz, stacked=True, tasks=[_pair_exchange_task([g_d1])])
    s_d1 = pair_sum("d1", g_d1, got)
    dn, ((got,), (landed,)) = _mm_nt_cols("mlp_up_dx_1", dz, w_up1, F32,
                                          tasks=[_pair_exchange_task([g_u1]), _chip_exchange_task([s_d1])])
    s_u1, b_d1 = pair_sum("u1", g_u1, got), chip_sum("d1", landed)
    dh, dh_bf, gn["norm_mlp_1"] = _rms_bwd("norm_mlp_bwd_1", dn, h3, norms["norm_mlp_1"], dh)
    do = _mm_nt_rows("attn_out_dx", dh_bf, _rows(w_o), BF16)
    g_o = _stack(_mm_tn("attn_out_dw", o, dh_bf, stacked=False))
    dq, dkb, dvb, dsink = _attn_bwd(q, k, v, sink_rows, _heads_split("attn_do_split", do))
    dqkv = _qkv_merge_bwd(dq, dkb, dvb, cos, sin)
    g_qkv, ((got,),) = _mm_tn("attn_qkv_dw", n2, dqkv, stacked=True, tasks=[_pair_exchange_task([g_o])])
    s_o = pair_sum("o", g_o, got)
    dn, ((got,), (landed,)) = _mm_nt_cols("attn_qkv_dx", dqkv, w_qkv, F32,
                                          tasks=[_pair_exchange_task([g_qkv]), _chip_exchange_task([s_u1])])
    s_qkv, b_u1 = pair_sum("qkv", g_qkv, got), chip_sum("u1", landed)
    dh, dh_bf, gn["norm_mix_1"] = _rms_bwd("norm_mix_bwd_1", dn, h2, norms["norm_mix_1"], dh)
    dz, ((landed_o, landed_qkv), (r_d1,)) = _mm_nt_rows(
        "mlp_down_dx_0", dh_bf, _rows(w_down0), BF16, epi=_relu_sq_grad, extras=(z0,),
        tasks=[_chip_exchange_task([s_o, s_qkv]), _pair_share_task([b_d1])])
    b_o, b_qkv = chip_sum("o", landed_o), chip_sum("qkv", landed_qkv)
    g_d0, ((r_u1,),) = _mm_tn("mlp_down_dw_0", z0, dh_bf, stacked=False, a_pro=_relu_sq, tasks=[_pair_share_task([b_u1])])
    g_d0 = _stack(g_d0)
    g_u0, ((got,), (r_o, r_qkv)) = _mm_tn("mlp_up_dw_0", n1, dz, stacked=True,
                                          tasks=[_pair_exchange_task([g_d0]), _pair_share_task([b_o, b_qkv])])
    s_d0 = pair_sum("d0", g_d0, got)
    dn, ((got,), (landed,)) = _mm_nt_cols("mlp_up_dx_0", dz, w_up0, F32,
                                          tasks=[_pair_exchange_task([g_u0]), _chip_exchange_task([s_d0])])
    s_u0, b_d0 = pair_sum("u0", g_u0, got), chip_sum("d0", landed)
    dh, dh_bf, gn["norm_mlp_0"] = _rms_bwd("norm_mlp_bwd_0", dn, h1, norms["norm_mlp_0"], dh)
    dgate = _mm_nt_rows("conv_out_dx", dh_bf, _rows(w_out), BF16)
    dbcu, g_conv_w = _conv_bwd(bcu, conv_w8, dgate)
    g_in, ((landed,), (r_d0,)) = _mm_tn("conv_in_dw", n0, dbcu, stacked=True,
                                        tasks=[_chip_exchange_task([s_u0]), _pair_share_task([b_d0])])
    b_u0 = chip_sum("u0", landed)
    dn, ((got,), (r_u0,)) = _mm_nt_cols("conv_in_dx", dbcu, w_in, F32,
                                        tasks=[_pair_exchange_task([g_in]), _pair_share_task([b_u0])])
    s_in = pair_sum("in", g_in, got)
    dh0, _, gn["norm_mix_0"] = _rms_bwd("norm_mix_bwd_0", dn, h0, norms["norm_mix_0"], dh)

    g_small = jnp.zeros((SMALL_ROWS, d), F32).at[:N_META].set(dh0[pad:ROW0]).at[N_META:N_META + 8].set(g_conv_w)
    g_small = jnp.transpose(g_small.reshape(SMALL_ROWS, N_CHIPS, dc), (1, 0, 2))
    rep = jnp.zeros((8, d), F32)
    for r, n in enumerate(NORMS):
        rep = rep.at[r].set(jnp.sum(gn[n], axis=0))
    rep = rep.at[len(NORMS), :N_Q_HEADS].set(jnp.sum(dsink.reshape(N_Q_HEADS, BLOCK), axis=1))
    g_out, ((landed,), (got, rep_all)) = _mm_tn(
        "conv_out_dw", gate, dh_bf, stacked=False,
        tasks=[_chip_exchange_task([s_in]), _pair_exchange_task([g_small], small=rep)])
    g_out = _stack(g_out)
    b_in, s_small = chip_sum("in", landed), pair_sum("small", g_small, got)
    (got,), = _run("tail_pair_exchange", [_pair_exchange_task([g_out])])
    s_out = pair_sum("out", g_out, got)
    (landed_out, landed_small), = _run("tail_chip_exchange", [_chip_exchange_task([s_out, s_small])])
    b_out, b_small = chip_sum("out", landed_out), chip_sum("small", landed_small)
    (r_out, r_small, r_in), = _run("tail_pair_share", [_pair_share_task([b_out, b_small, b_in])])

    reduced = {"w_in_conv": r_in, "w_out_conv": r_out, "w_up_0": r_u0, "w_down_0": r_d0, "w_qkv": r_qkv, "w_o": r_o,
               "w_up_1": r_u1, "w_down_1": r_d1}
    return loss, dh0, reduced, r_small, rep_all


def kernel(x, meta_tokens, norm_mix_0, w_in_conv, conv_w, w_out_conv, norm_mlp_0, w_up_0, w_down_0, norm_mix_1, w_qkv, attn_sinks, w_o, norm_mlp_1, w_up_1, w_down_1, norm_final, loss_target, m_meta_tokens, m_norm_mix_0, m_w_in_conv, m_conv_w, m_w_out_conv, m_norm_mlp_0, m_w_up_0, m_w_down_0, m_norm_mix_1, m_w_qkv, m_attn_sinks, m_w_o, m_norm_mlp_1, m_w_up_1, m_w_down_1, m_norm_final, v_meta_tokens, v_norm_mix_0, v_w_in_conv, v_conv_w, v_w_out_conv, v_norm_mlp_0, v_w_up_0, v_w_down_0, v_norm_mix_1, v_w_qkv, v_attn_sinks, v_w_o, v_norm_mlp_1, v_w_up_1, v_w_down_1, v_norm_final):
    given = dict(locals())
    names = ("meta_tokens", "norm_mix_0", "w_in_conv", "conv_w", "w_out_conv", "norm_mlp_0", "w_up_0", "w_down_0",
             "norm_mix_1", "w_qkv", "attn_sinks", "w_o", "norm_mlp_1", "w_up_1", "w_down_1", "norm_final")
    d = D_MODEL
    dc = d // N_CHIPS
    chip = 2 * lax.axis_index("x") + lax.axis_index("y")
    place = jnp.stack([lax.axis_index("c"), chip]).astype(jnp.int32)

    small = jnp.zeros((SMALL_ROWS, dc), F32).at[:N_META].set(meta_tokens).at[N_META:N_META + CONV_WIDTH].set(conv_w)
    small = lax.dynamic_update_slice(jnp.zeros((N_CHIPS, SMALL_ROWS, dc), F32), small[None], (chip, 0, 0))
    stacks = {n: _cast_bf16("cast_" + n, given[n], place) for n in MATRICES}

    norms = {n: given[n] for n in NORMS}
    loss_part, dh0, g_out, r_small, rep_all = _step(x[0], loss_target[0], stacks, small, norms, attn_sinks, place)
    loss = lax.psum(loss_part[0, 0], ("x", "y", "c"))
    grad_x = dh0[ROW0:][None]
    rep_sum = _dev_sum(rep_all)
    g_out["meta_tokens"] = r_small[:N_META]
    g_out["conv_w"] = r_small[N_META:N_META + CONV_WIDTH]
    for r, n in enumerate(NORMS):
        g_out[n] = rep_sum[r]
    g_out["attn_sinks"] = rep_sum[len(NORMS), :N_Q_HEADS]

    delta, new_m, new_v = {}, {}, {}
    for n in names:
        wt = given[n]
        shape2 = wt.shape if wt.ndim == 2 else (1, wt.shape[0])
        outs = _adamw("adamw_" + n, wt.reshape(shape2), g_out[n].reshape(shape2),
                      given["m_" + n].reshape(shape2), given["v_" + n].reshape(shape2))
        delta[n], new_m[n], new_v[n] = [o.reshape(wt.shape) for o in outs]
    return (loss, grad_x, *[g_out[n] for n in names], *[delta[n] for n in names],
            *[new_m[n] for n in names], *[new_v[n] for n in names])
```

```python
import functools

import jax
import jax.numpy as jnp
from jax import lax
from jax.experimental import pallas as pl
from jax.experimental.pallas import tpu as pltpu

F32 = jnp.float32
BF16 = jnp.bfloat16

D_MODEL = 2048
SEQ = 8192
N_META = 16
CONV_WIDTH = 3
HEAD_DIM = 64
N_Q_HEADS = 32
N_KV_HEADS = 4
BLOCK = 128
ROPE_THETA = 10000.0
D_FF = 4 * D_MODEL
RMS_EPS = 1e-5
NEG_INF = -1e30

ADAM_LR = 0.001
ADAM_B1 = 0.9
ADAM_B2 = 0.999
ADAM_EPS = 1e-08
ADAM_WD = 0.01
ADAM_STEP = 10

N_CHIPS = 4
N_DEV = 8
MESH = pl.DeviceIdType.MESH
VMEM_LIMIT = 56 * 1024 * 1024
SMALL_ROWS = 32
ROW0 = BLOCK


def _pick(n, target, mult):
    best = None
    for t in range(mult, min(n, target) + 1, mult):
        if n % t == 0:
            best = t
    assert best is not None, (n, target, mult)
    return best


def _params(sem=None):
    return pltpu.CompilerParams(dimension_semantics=sem, vmem_limit_bytes=VMEM_LIMIT)


HBM_SPEC = pl.BlockSpec(memory_space=pltpu.HBM)


class _Task:
    def __init__(self, inputs, outputs, aliases, sem_shapes, bind):
        self.inputs, self.outputs, self.aliases = list(inputs), list(outputs), dict(aliases)
        self.sem_shapes, self.bind = list(sem_shapes), bind


def _like(arrays):
    return [jax.ShapeDtypeStruct(a.shape, a.dtype) for a in arrays]


def _bind_tasks(tasks, in_refs, out_refs, sem_refs):
    bound, i, o, s = [], 0, 0, 0
    for t in tasks:
        ni, no, ns = len(t.inputs), len(t.outputs), len(t.sem_shapes)
        bound.append(t.bind(in_refs[i:i + ni], out_refs[o:o + no], sem_refs[s:s + ns]))
        i, o, s = i + ni, o + no, s + ns
    return bound


def _run_phase(bound, phase):
    for b in bound:
        if b[phase] is not None:
            b[phase]()


def _task_plumbing(tasks, in_offset, out_offset):
    ins = [a for t in tasks for a in t.inputs]
    outs = [o for t in tasks for o in t.outputs]
    sems = [s for t in tasks for s in t.sem_shapes]
    aliases, i, o = {}, in_offset, out_offset
    for t in tasks:
        for src, dst in t.aliases.items():
            aliases[i + src] = o + dst
        i, o = i + len(t.inputs), o + len(t.outputs)
    return ins, outs, sems, aliases


def _split_outputs(tasks, flat):
    res, o = [], 0
    for t in tasks:
        res.append(list(flat[o:o + len(t.outputs)]))
        o += len(t.outputs)
    return res


def _run(name, tasks):
    ins, outs, sems, aliases = _task_plumbing(tasks, 0, 0)

    def body(*refs):
        bound = _bind_tasks(tasks, refs[:len(ins)], refs[len(ins):len(ins) + len(outs)], refs[len(ins) + len(outs):])
        for phase in range(3):
            _run_phase(bound, phase)

    flat = pl.pallas_call(
        body, name=name, out_shape=outs, in_specs=[HBM_SPEC] * len(ins), out_specs=[HBM_SPEC] * len(outs),
        input_output_aliases=aliases, scratch_shapes=sems,
    )(*ins)
    return _split_outputs(tasks, flat)


def _place():
    x, y, c = lax.axis_index("x"), lax.axis_index("y"), lax.axis_index("c")
    chips = [(1 - x, y), (x, 1 - y), (1 - x, 1 - y)]
    return x, y, c, 2 * x + y, chips


def _gather_task(stacks):
    n = len(stacks)
    halves = [s.shape[-2] // 2 for s in stacks]

    def bind(_, dst, sems):
        send_a, recv_a, send_b, recv_b = sems
        x, y, c, me, chips = _place()
        sibling = (x, y, 1 - c)

        def half(w, chip, hc):
            rows = pl.ds(hc * halves[w], halves[w])
            if len(stacks[w].shape) == 3:
                return dst[w].at[chip, rows, :]
            cols = stacks[w].shape[1] // N_CHIPS
            return dst[w].at[rows, pl.ds(chip * cols, cols)]

        def over_ici(j, w, block):
            return pltpu.make_async_remote_copy(
                src_ref=half(w, block, c), dst_ref=half(w, block, c), send_sem=send_a.at[j * n + w],
                recv_sem=recv_a.at[j * n + w], device_id=(*chips[j], c), device_id_type=MESH)

        def over_d2d(j, w, hc):
            got = half(w, 2 * chips[j][0] + chips[j][1], hc)
            return pltpu.make_async_remote_copy(
                src_ref=got, dst_ref=got, send_sem=send_b.at[j * n + w], recv_sem=recv_b.at[j * n + w],
                device_id=sibling, device_id_type=MESH)

        pairs = [(j, w) for j in range(3) for w in range(n)]

        def start():
            for j, w in pairs:
                over_ici(j, w, me).start()

        def mid():
            for j, w in pairs:
                over_ici(j, w, 2 * chips[j][0] + chips[j][1]).wait_recv()
                over_d2d(j, w, c).start()

        def finish():
            for j, w in pairs:
                over_d2d(j, w, 1 - c).wait_recv()
            for j, w in pairs:
                over_ici(j, w, me).wait_send()
                over_d2d(j, w, c).wait_send()

        return start, mid, finish

    return _Task(stacks, _like(stacks), {w: w for w in range(n)}, [pltpu.SemaphoreType.DMA((3 * n,))] * 4, bind)


def _pair_exchange_task(grads, small=None):
    n = len(grads)
    halves = [g.shape[1] // 2 for g in grads]

    def bind(src, dst, sems):
        send, recv = sems[0], sems[1]
        x, y, c, me, _ = _place()
        sibling = (x, y, 1 - c)
        dev = 2 * me + c

        def to_sibling(w):
            return pltpu.make_async_remote_copy(
                src_ref=src[w].at[:, pl.ds((1 - c) * halves[w], halves[w]), :], dst_ref=dst[w],
                send_sem=send.at[w], recv_sem=recv.at[w], device_id=sibling, device_id_type=MESH)

        def to_peer(t, block):
            tx, ty, tc = (t >> 2) & 1, (t >> 1) & 1, t & 1
            return pltpu.make_async_remote_copy(
                src_ref=src[n], dst_ref=dst[n].at[block], send_sem=sems[2].at[t], recv_sem=sems[3].at[t],
                device_id=(x ^ tx, y ^ ty, c ^ tc), device_id_type=MESH)

        def mine():
            return pltpu.make_async_copy(src[n], dst[n].at[dev], sems[4])

        def start():
            for w in range(n):
                to_sibling(w).start()
            if small is not None:
                mine().start()
                for t in range(1, N_DEV):
                    to_peer(t, dev).start()

        def finish():
            for w in range(n):
                to_sibling(w).wait_recv()
            if small is not None:
                for t in range(1, N_DEV):
                    to_peer(t, dev ^ t).wait_recv()
            for w in range(n):
                to_sibling(w).wait_send()
            if small is not None:
                for t in range(1, N_DEV):
                    to_peer(t, dev).wait_send()
                mine().wait()

        return start, None, finish

    outputs = [jax.ShapeDtypeStruct((N_CHIPS, h, g.shape[2]), g.dtype) for g, h in zip(grads, halves)]
    sem_shapes = [pltpu.SemaphoreType.DMA((n,)), pltpu.SemaphoreType.DMA((n,))]
    inputs = list(grads)
    if small is not None:
        inputs.append(small)
        outputs.append(jax.ShapeDtypeStruct((N_DEV,) + small.shape, small.dtype))
        sem_shapes += [pltpu.SemaphoreType.DMA((N_DEV,)), pltpu.SemaphoreType.DMA((N_DEV,)), pltpu.SemaphoreType.DMA(())]
    return _Task(inputs, outputs, {}, sem_shapes, bind)


def _chip_exchange_task(summed):
    n = len(summed)

    def bind(refs, dst, sems):
        src = refs[:n]
        send, recv = sems
        x, y, c, me, chips = _place()

        def copy(j, w, block_from, block_to):
            return pltpu.make_async_remote_copy(
                src_ref=src[w].at[block_from], dst_ref=dst[w].at[block_to], send_sem=send.at[j * n + w],
                recv_sem=recv.at[j * n + w], device_id=(*chips[j], c), device_id_type=MESH)

        pairs = [(j, w) for j in range(3) for w in range(n)]

        def start():
            for j, w in pairs:
                copy(j, w, 2 * chips[j][0] + chips[j][1], me).start()

        def finish():
            for j, w in pairs:
                copy(j, w, me, 2 * chips[j][0] + chips[j][1]).wait_recv()
            for j, w in pairs:
                copy(j, w, 2 * chips[j][0] + chips[j][1], me).wait_send()

        return start, None, finish

    partials, landing = [s[0] for s in summed], [s[1] for s in summed]
    return _Task(partials + landing, _like(landing), {n + w: w for w in range(n)},
                 [pltpu.SemaphoreType.DMA((3 * n,))] * 2, bind)


def _pair_share_task(blocks):
    n = len(blocks)

    def bind(_, dst, sems):
        send, recv = sems
        x, y, c, _, _ = _place()

        def copy(w, hc):
            h = blocks[w].shape[0] // 2
            rows = dst[w].at[pl.ds(hc * h, h), :]
            return pltpu.make_async_remote_copy(src_ref=rows, dst_ref=rows, send_sem=send.at[w], recv_sem=recv.at[w],
                                                device_id=(x, y, 1 - c), device_id_type=MESH)

        def start():
            for w in range(n):
                copy(w, c).start()

        def finish():
            for w in range(n):
                copy(w, 1 - c).wait_recv()
            for w in range(n):
                copy(w, c).wait_send()

        return start, None, finish

    return _Task(blocks, _like(blocks), {w: w for w in range(n)}, [pltpu.SemaphoreType.DMA((n,))] * 2, bind)


def _mm(name, a, b, *, dims, grid, a_spec, b_spec, out_shape, out_spec,
        extras=(), extra_specs=(), a_pro=None, epi=None, tasks=()):
    n_ex = len(extras)
    t_ins, t_outs, t_sems, aliases = _task_plumbing(tasks, 2 + n_ex, 1)
    n_ti, n_to = len(t_ins), len(t_outs)
    total = grid[0] * grid[1]
    mid_step = max(0, total - 1 - max(1, total // 8))

    def body(*refs):
        a_ref, b_ref = refs[0], refs[1]
        ex = refs[2:2 + n_ex]
        o_ref = refs[2 + n_ex + n_ti]
        if tasks:
            bound = _bind_tasks(tasks, refs[2 + n_ex:2 + n_ex + n_ti],
                                refs[3 + n_ex + n_ti:3 + n_ex + n_ti + n_to], refs[3 + n_ex + n_ti + n_to:])
            step = pl.program_id(0) * grid[1] + pl.program_id(1)

            @pl.when(step == 0)
            def _():
                _run_phase(bound, 0)

        av = a_ref[...]
        if a_pro is not None:
            av = a_pro(av)
        acc = lax.dot_general(av, b_ref[...], dims, preferred_element_type=F32)
        if epi is not None:
            acc = epi(acc, *[e[...] for e in ex])
        o_ref[...] = acc.astype(o_ref.dtype)

        if tasks:
            @pl.when(step == mid_step)
            def _():
                _run_phase(bound, 1)

            @pl.when(step == total - 1)
            def _():
                _run_phase(bound, 2)

    sem = ("arbitrary", "arbitrary") if tasks else ("parallel", "parallel")
    res = pl.pallas_call(
        body, name=name, grid=grid, out_shape=[out_shape, *t_outs],
        in_specs=[a_spec, b_spec, *extra_specs, *[HBM_SPEC] * n_ti], out_specs=[out_spec, *[HBM_SPEC] * n_to],
        input_output_aliases=aliases, scratch_shapes=t_sems, compiler_params=_params(sem),
    )(a, b, *extras, *t_ins)
    return (res[0], _split_outputs(tasks, res[1:])) if tasks else res[0]


_NN = (((1,), (0,)), ((), ()))
_NT = (((1,), (1,)), ((), ()))
_TN = (((0,), (0,)), ((), ()))


def _mm_tiles(m, n, contraction):
    if contraction <= 2048:
        return _pick(m, 832, 16), _pick(n, 1024, 128)
    return _pick(m, 416, 16), _pick(n, 512, 128)


def _mm_nn(name, a, w, out_dtype, a_pro=None, epi=None, extras=(), tasks=()):
    m, k = a.shape
    _, n = w.shape
    tm, tn = _mm_tiles(m, n, k)
    tile = pl.BlockSpec((tm, tn), lambda j, i: (i, j))
    return _mm(name, a, w, dims=_NN, grid=(n // tn, m // tm),
               a_spec=pl.BlockSpec((tm, k), lambda j, i: (i, 0)), b_spec=pl.BlockSpec((k, tn), lambda j, i: (0, j)),
               out_shape=jax.ShapeDtypeStruct((m, n), out_dtype), out_spec=tile,
               extras=extras, extra_specs=[tile] * len(extras), a_pro=a_pro, epi=epi, tasks=tasks)


def _mm_nt(name, a, w, out_dtype, epi=None, extras=(), tasks=()):
    m, c = a.shape
    r, _ = w.shape
    tm, tn = _mm_tiles(m, r, c)
    tile = pl.BlockSpec((tm, tn), lambda j, i: (i, j))
    return _mm(name, a, w, dims=_NT, grid=(r // tn, m // tm),
               a_spec=pl.BlockSpec((tm, c), lambda j, i: (i, 0)), b_spec=pl.BlockSpec((tn, c), lambda j, i: (j, 0)),
               out_shape=jax.ShapeDtypeStruct((m, r), out_dtype), out_spec=tile,
               extras=extras, extra_specs=[tile] * len(extras), epi=epi, tasks=tasks)


def _mm_tn(name, a, b, stacked, a_pro=None, tasks=()):
    t, ka = a.shape
    _, nb = b.shape
    ns = nb // N_CHIPS if stacked else nb
    ta, tb = _pick(ka, 512, 128), _pick(ns, 640, 128)
    if stacked:
        per = ns // tb
        out_shape = jax.ShapeDtypeStruct((N_CHIPS, ka, ns), F32)
        out_spec = pl.BlockSpec((None, ta, tb), lambda i, j: (j // per, i, j % per))
    else:
        out_shape = jax.ShapeDtypeStruct((ka, nb), F32)
        out_spec = pl.BlockSpec((ta, tb), lambda i, j: (i, j))
    return _mm(name, a, b, dims=_TN, grid=(ka // ta, nb // tb),
               a_spec=pl.BlockSpec((t, ta), lambda i, j: (0, i)), b_spec=pl.BlockSpec((t, tb), lambda i, j: (0, j)),
               out_shape=out_shape, out_spec=out_spec, a_pro=a_pro, tasks=tasks)


def _relu_sq(z):
    a = jnp.maximum(z, 0)
    return a * a


def _rms_fwd(name, h, g):
    m, d = h.shape
    tr = _pick(m, 256, 16)

    def body(h_ref, g_ref, o_ref):
        x = h_ref[...]
        rstd = lax.rsqrt(jnp.mean(x * x, axis=-1, keepdims=True) + RMS_EPS)
        o_ref[...] = ((x * rstd) * g_ref[...]).astype(BF16)

    row = pl.BlockSpec((tr, d), lambda i: (i, 0))
    return pl.pallas_call(
        body, name=name, grid=(m // tr,), out_shape=jax.ShapeDtypeStruct((m, d), BF16),
        in_specs=[row, pl.BlockSpec((1, d), lambda i: (0, 0))], out_specs=row,
        compiler_params=_params(("parallel",)),
    )(h, g.reshape(1, d))


def _rms_bwd_math(x, g, dn):
    rstd = lax.rsqrt(jnp.mean(x * x, axis=-1, keepdims=True) + RMS_EPS)
    xhat = x * rstd
    dxhat = dn * g
    dx = rstd * (dxhat - xhat * jnp.mean(dxhat * xhat, axis=-1, keepdims=True))
    return dx, dn * xhat


def _fold8(v):
    r, c = v.shape
    return jnp.sum(v.reshape(r // 8, 8, c), axis=0)


def _rms_bwd(name, dn, h, g, dh_in):
    m, d = h.shape
    tr = _pick(m, 256, 16)
    nt = m // tr

    def body(dn_ref, h_ref, g_ref, dh_ref, o_ref, ob_ref, dg_ref):
        dx, dgp = _rms_bwd_math(h_ref[...], g_ref[...], dn_ref[...])
        dh = dh_ref[...] + dx
        o_ref[...] = dh
        ob_ref[...] = dh.astype(BF16)

        @pl.when(pl.program_id(0) == 0)
        def _():
            dg_ref[...] = jnp.zeros_like(dg_ref)

        dg_ref[...] += _fold8(dgp)

    row = pl.BlockSpec((tr, d), lambda i: (i, 0))
    return pl.pallas_call(
        body, name=name, grid=(nt,),
        out_shape=(jax.ShapeDtypeStruct((m, d), F32), jax.ShapeDtypeStruct((m, d), BF16),
                   jax.ShapeDtypeStruct((8, d), F32)),
        in_specs=[row, row, pl.BlockSpec((1, d), lambda i: (0, 0)), row],
        out_specs=(row, row, pl.BlockSpec((8, d), lambda i: (0, 0))),
        compiler_params=_params(("arbitrary",)),
    )(dn, h, g.reshape(1, d), dh_in)


def _loss_head(h, g, target):
    m, d = h.shape
    tr = BLOCK

    def body(h_ref, g_ref, t_ref, loss_ref, o_ref, ob_ref, dg_ref):
        i = pl.program_id(0)
        x = h_ref[...]
        gv = g_ref[...]
        rstd = lax.rsqrt(jnp.mean(x * x, axis=-1, keepdims=True) + RMS_EPS)
        err = jnp.where(i > 0, (x * rstd) * gv - t_ref[...], 0.0)
        dx, dgp = _rms_bwd_math(x, gv, err * (1.0 / d))
        o_ref[...] = dx
        ob_ref[...] = dx.astype(BF16)

        @pl.when(i == 0)
        def _():
            dg_ref[...] = jnp.zeros_like(dg_ref)
            loss_ref[...] = jnp.zeros_like(loss_ref)

        dg_ref[...] += _fold8(dgp)
        sq = jnp.mean(err * err, axis=-1, keepdims=True)
        loss_ref[...] += 0.5 * jnp.sum(sq, axis=0, keepdims=True)

    row = pl.BlockSpec((tr, d), lambda i: (i, 0))
    return pl.pallas_call(
        body, name="loss_head", grid=(m // tr,),
        out_shape=(jax.ShapeDtypeStruct((8, 128), F32), jax.ShapeDtypeStruct((m, d), F32),
                   jax.ShapeDtypeStruct((m, d), BF16), jax.ShapeDtypeStruct((8, d), F32)),
        in_specs=[row, pl.BlockSpec((1, d), lambda i: (0, 0)),
                  pl.BlockSpec((tr, d), lambda i: (jnp.maximum(i - 1, 0), 0))],
        out_specs=(pl.BlockSpec((8, 128), lambda i: (0, 0)), row, row,
                   pl.BlockSpec((8, d), lambda i: (0, 0))),
        compiler_params=_params(("arbitrary",)),
    )(h, g.reshape(1, d), target)


HALO = 16


def _shift_down(cat, k):
    return pltpu.roll(cat, k, axis=0)[HALO:]


def _shift_up(cat, k):
    n = cat.shape[0]
    return pltpu.roll(cat, n - k, axis=0)[:n - HALO]


def _conv_fwd(bcu, cw):
    m, d3 = bcu.shape
    d = d3 // 3
    tr, tc = _pick(m, 416, 16), _pick(d, 512, 128)
    nd, hb = d // tc, tr // HALO

    def body(b_ref, c_ref, u_ref, ch_ref, uh_ref, w_ref, o_ref):
        i = pl.program_id(0)
        v = c_ref[...].astype(F32) * u_ref[...].astype(F32)
        vh = jnp.where(i > 0, ch_ref[...].astype(F32) * uh_ref[...].astype(F32), 0.0)
        cat = jnp.concatenate([vh, v], axis=0)
        w = w_ref[...]
        conv = w[2:3] * v + w[1:2] * _shift_down(cat, 1) + w[0:1] * _shift_down(cat, 2)
        o_ref[...] = (b_ref[...].astype(F32) * conv).astype(BF16)

    def part(p):
        return pl.BlockSpec((tr, tc), lambda i, j: (i, p * nd + j))

    def halo(p):
        return pl.BlockSpec((HALO, tc), lambda i, j: (jnp.maximum(i * hb - 1, 0), p * nd + j))

    return pl.pallas_call(
        body, name="conv_fwd", grid=(m // tr, nd), out_shape=jax.ShapeDtypeStruct((m, d), BF16),
        in_specs=[part(0), part(1), part(2), halo(1), halo(2), pl.BlockSpec((8, tc), lambda i, j: (0, j))],
        out_specs=pl.BlockSpec((tr, tc), lambda i, j: (i, j)),
        compiler_params=_params(("parallel", "parallel")),
    )(bcu, bcu, bcu, bcu, bcu, cw)


def _conv_bwd(bcu, cw, dg):
    m, d3 = bcu.shape
    d = d3 // 3
    tr, tc = _pick(m, 208, 16), _pick(d, 512, 128)
    hb, nt = tr // HALO, m // tr

    def body(x_ref, xb_ref, xa_ref, dg_ref, dga_ref, w_ref, o_ref, dw_ref):
        i = pl.program_id(0)

        @pl.when(i == 0)
        def _():
            dw_ref[...] = jnp.zeros_like(dw_ref)

        for j in range(d // tc):
            col = slice(j * tc, (j + 1) * tc)
            cb, cc, cu = (slice(q * d + j * tc, q * d + (j + 1) * tc) for q in range(3))
            w = w_ref[:, col]
            b, c, u = x_ref[:, cb].astype(F32), x_ref[:, cc].astype(F32), x_ref[:, cu].astype(F32)
            dgv = dg_ref[:, col].astype(F32)
            v = c * u
            vh = jnp.where(i > 0, xb_ref[:, cc].astype(F32) * xb_ref[:, cu].astype(F32), 0.0)
            cat = jnp.concatenate([vh, v], axis=0)
            v1, v2 = _shift_down(cat, 1), _shift_down(cat, 2)
            dconv = dgv * b
            o_ref[:, cb] = (dgv * (w[2:3] * v + w[1:2] * v1 + w[0:1] * v2)).astype(BF16)
            taps = [jnp.sum(dconv * t, axis=0, keepdims=True) for t in (v2, v1, v)]
            dw_ref[:, col] += jnp.concatenate(taps + [jnp.zeros((5, tc), F32)], axis=0)
            nxt = jnp.where(i < nt - 1, dga_ref[:, col].astype(F32) * xa_ref[:, cb].astype(F32), 0.0)
            cat2 = jnp.concatenate([dconv, nxt], axis=0)
            dv = w[2:3] * dconv + w[1:2] * _shift_up(cat2, 1) + w[0:1] * _shift_up(cat2, 2)
            o_ref[:, cc] = (dv * u).astype(BF16)
            o_ref[:, cu] = (dv * c).astype(BF16)

    def rows(width):
        return pl.BlockSpec((tr, width), lambda i: (i, 0))

    def before(width):
        return pl.BlockSpec((HALO, width), lambda i: (jnp.maximum(i * hb - 1, 0), 0))

    def after(width):
        return pl.BlockSpec((HALO, width), lambda i: (jnp.minimum((i + 1) * hb, m // HALO - 1), 0))

    return pl.pallas_call(
        body, name="conv_bwd", grid=(nt,),
        out_shape=(jax.ShapeDtypeStruct((m, d3), BF16), jax.ShapeDtypeStruct((8, d), F32)),
        in_specs=[rows(d3), before(d3), after(d3), rows(d), after(d), pl.BlockSpec((8, d), lambda i: (0, 0))],
        out_specs=(rows(d3), pl.BlockSpec((8, d), lambda i: (0, 0))),
        compiler_params=_params(("arbitrary",)),
    )(bcu, bcu, bcu, dg, dg, cw)


def _rope_tables(m):
    pad = ROW0 - N_META
    pos = jnp.arange(m, dtype=F32) - pad
    inv = ROPE_THETA ** (-jnp.arange(0, HEAD_DIM, 2, dtype=F32) / HEAD_DIM)
    ang = pos[:, None] * inv[None, :]
    return jnp.cos(ang), jnp.sin(ang)


def _rope(x, c, s):
    half = HEAD_DIM // 2
    x1, x2 = x[:, :half], x[:, half:]
    return jnp.concatenate([x1 * c - x2 * s, x2 * c + x1 * s], axis=-1)


def _rope_t(y, c, s):
    half = HEAD_DIM // 2
    y1, y2 = y[:, :half], y[:, half:]
    return jnp.concatenate([y1 * c + y2 * s, y2 * c - y1 * s], axis=-1)


def _qkv_split(qkv, cos, sin):
    m = qkv.shape[0]
    nb, grp = m // BLOCK, N_Q_HEADS // N_KV_HEADS
    scale = HEAD_DIM ** -0.5

    def body(x_ref, c_ref, s_ref, q_ref, k_ref, v_ref):
        c, s = c_ref[...], s_ref[...]
        for h in range(N_KV_HEADS):
            for g in range(grp):
                col = (h * grp + g) * HEAD_DIM
                xq = x_ref[:, col:col + HEAD_DIM].astype(F32)
                q_ref[h, g * BLOCK:(g + 1) * BLOCK, :] = (_rope(xq, c, s) * scale).astype(BF16)
            col = (N_Q_HEADS + h) * HEAD_DIM
            k_ref[h] = _rope(x_ref[:, col:col + HEAD_DIM].astype(F32), c, s).astype(BF16)
            col = (N_Q_HEADS + N_KV_HEADS + h) * HEAD_DIM
            v_ref[h] = x_ref[:, col:col + HEAD_DIM]

    tab = pl.BlockSpec((BLOCK, HEAD_DIM // 2), lambda i: (i, 0))
    kv = pl.BlockSpec((N_KV_HEADS, BLOCK, HEAD_DIM), lambda i: (0, i, 0))
    return pl.pallas_call(
        body, name="qkv_split", grid=(nb,),
        out_shape=(jax.ShapeDtypeStruct((N_KV_HEADS, nb * grp * BLOCK, HEAD_DIM), BF16),
                   jax.ShapeDtypeStruct((N_KV_HEADS, m, HEAD_DIM), BF16),
                   jax.ShapeDtypeStruct((N_KV_HEADS, m, HEAD_DIM), BF16)),
        in_specs=[pl.BlockSpec((BLOCK, qkv.shape[1]), lambda i: (i, 0)), tab, tab],
        out_specs=(pl.BlockSpec((N_KV_HEADS, grp * BLOCK, HEAD_DIM), lambda i: (0, i, 0)), kv, kv),
        compiler_params=_params(("parallel",)),
    )(qkv, cos, sin)


def _heads_merge(name, o):
    grp = N_Q_HEADS // N_KV_HEADS
    nb = o.shape[1] // (grp * BLOCK)

    def body(o_ref, x_ref):
        for h in range(N_KV_HEADS):
            for g in range(grp):
                col = (h * grp + g) * HEAD_DIM
                x_ref[:, col:col + HEAD_DIM] = o_ref[h, g * BLOCK:(g + 1) * BLOCK, :]

    return pl.pallas_call(
        body, name=name, grid=(nb,),
        out_shape=jax.ShapeDtypeStruct((nb * BLOCK, N_Q_HEADS * HEAD_DIM), o.dtype),
        in_specs=[pl.BlockSpec((N_KV_HEADS, grp * BLOCK, HEAD_DIM), lambda i: (0, i, 0))],
        out_specs=pl.BlockSpec((BLOCK, N_Q_HEADS * HEAD_DIM), lambda i: (i, 0)),
        compiler_params=_params(("parallel",)),
    )(o)


def _heads_split(name, x):
    grp = N_Q_HEADS // N_KV_HEADS
    nb = x.shape[0] // BLOCK

    def body(x_ref, o_ref):
        for h in range(N_KV_HEADS):
            for g in range(grp):
                col = (h * grp + g) * HEAD_DIM
                o_ref[h, g * BLOCK:(g + 1) * BLOCK, :] = x_ref[:, col:col + HEAD_DIM]

    return pl.pallas_call(
        body, name=name, grid=(nb,),
        out_shape=jax.ShapeDtypeStruct((N_KV_HEADS, nb * grp * BLOCK, HEAD_DIM), x.dtype),
        in_specs=[pl.BlockSpec((BLOCK, N_Q_HEADS * HEAD_DIM), lambda i: (i, 0))],
        out_specs=pl.BlockSpec((N_KV_HEADS, grp * BLOCK, HEAD_DIM), lambda i: (0, i, 0)),
        compiler_params=_params(("parallel",)),
    )(x)


def _attn_mask(i):
    r = lax.broadcasted_iota(jnp.int32, (BLOCK, 2 * BLOCK), 0)
    cidx = lax.broadcasted_iota(jnp.int32, (BLOCK, 2 * BLOCK), 1)
    key = (i - 1) * BLOCK + cidx
    return (cidx > r) & (cidx <= r + BLOCK) & (key >= ROW0 - N_META)


def _attn_exp(allowed, q, kb, vb1, sink):
    rows = q.shape[0]
    grp = rows // BLOCK
    s = lax.dot_general(q, kb, _NT, preferred_element_type=F32)
    s = jnp.where(allowed[None], s.reshape(grp, BLOCK, 2 * BLOCK), NEG_INF).reshape(rows, 2 * BLOCK)
    mx = jnp.maximum(jnp.max(s, axis=-1, keepdims=True), sink)
    eb = jnp.exp(s - mx).astype(BF16)
    es = jnp.exp(sink - mx)
    ov = jnp.dot(eb, vb1, preferred_element_type=F32)
    inv = 1.0 / (ov[:, HEAD_DIM:HEAD_DIM + 1] + es)
    return eb, ov[:, :HEAD_DIM], inv, es


def _band(prev_ref, cur_ref, h):
    return jnp.concatenate([prev_ref[h], cur_ref[h]], axis=0)


def _with_ones(vb):
    return jnp.concatenate([vb, jnp.ones_like(vb)], axis=1)


def _attn_specs(grp):
    q = pl.BlockSpec((N_KV_HEADS, grp * BLOCK, HEAD_DIM), lambda i: (0, i, 0))
    cur = pl.BlockSpec((N_KV_HEADS, BLOCK, HEAD_DIM), lambda i: (0, i, 0))
    prev = pl.BlockSpec((N_KV_HEADS, BLOCK, HEAD_DIM), lambda i: (0, jnp.maximum(i - 1, 0), 0))
    sink = pl.BlockSpec((N_KV_HEADS, grp * BLOCK, 1), lambda i: (0, 0, 0))
    return q, cur, prev, sink


def _attn_fwd(q, k, v, sink_rows):
    grp = N_Q_HEADS // N_KV_HEADS
    nb = k.shape[1] // BLOCK

    def body(q_ref, kc_ref, kp_ref, vc_ref, vp_ref, s_ref, o_ref):
        allowed = _attn_mask(pl.program_id(0))
        for h in range(N_KV_HEADS):
            vb1 = _with_ones(_band(vp_ref, vc_ref, h))
            _, ov, inv, _ = _attn_exp(allowed, q_ref[h], _band(kp_ref, kc_ref, h), vb1, s_ref[h])
            o_ref[h] = (ov * inv).astype(BF16)

    qs, cur, prev, sink = _attn_specs(grp)
    return pl.pallas_call(
        body, name="attn_fwd", grid=(nb,), out_shape=jax.ShapeDtypeStruct(q.shape, BF16),
        in_specs=[qs, cur, prev, cur, prev, sink], out_specs=qs,
        compiler_params=_params(("parallel",)),
    )(q, k, k, v, v, sink_rows)


def _attn_bwd(q, k, v, sink_rows, do):
    grp = N_Q_HEADS // N_KV_HEADS
    nb = k.shape[1] // BLOCK

    def body(q_ref, kc_ref, kp_ref, vc_ref, vp_ref, s_ref, do_ref, dq_ref, dk_ref, dv_ref, ds_ref):
        i = pl.program_id(0)
        allowed = _attn_mask(i)

        @pl.when(i == 0)
        def _():
            ds_ref[...] = jnp.zeros_like(ds_ref)

        for h in range(N_KV_HEADS):
            qv, dov = q_ref[h], do_ref[h]
            kb, vb = _band(kp_ref, kc_ref, h), _band(vp_ref, vc_ref, h)
            eb, ov, inv, es = _attn_exp(allowed, qv, kb, _with_ones(vb), s_ref[h])
            dof = dov.astype(F32)
            delta = jnp.sum(dof * (ov * inv), axis=-1, keepdims=True)
            dp = lax.dot_general(dov, vb, _NT, preferred_element_type=F32)
            dsb = (eb.astype(F32) * (inv * (dp - delta))).astype(BF16)
            dq_ref[h] = jnp.dot(dsb, kb, preferred_element_type=F32).astype(BF16)
            dk_ref[h] = lax.dot_general(dsb, qv, _TN, preferred_element_type=F32)
            dv_ref[h] = lax.dot_general(eb, (dof * inv).astype(BF16), _TN, preferred_element_type=F32)
            ds_ref[h] -= (es * inv) * delta

    qs, cur, prev, sink = _attn_specs(grp)
    band = pl.BlockSpec((N_KV_HEADS, None, 2 * BLOCK, HEAD_DIM), lambda i: (0, i, 0, 0))
    band_shape = jax.ShapeDtypeStruct((N_KV_HEADS, nb, 2 * BLOCK, HEAD_DIM), F32)
    return pl.pallas_call(
        body, name="attn_bwd", grid=(nb,),
        out_shape=(jax.ShapeDtypeStruct(q.shape, BF16), band_shape, band_shape,
                   jax.ShapeDtypeStruct(sink_rows.shape, F32)),
        in_specs=[qs, cur, prev, cur, prev, sink, qs], out_specs=(qs, band, band, sink),
        compiler_params=_params(("arbitrary",)),
    )(q, k, k, v, v, sink_rows, do)


def _qkv_merge_bwd(dq, dkb, dvb, cos, sin):
    grp = N_Q_HEADS // N_KV_HEADS
    nb = dkb.shape[1]
    width = (N_Q_HEADS + 2 * N_KV_HEADS) * HEAD_DIM
    scale = HEAD_DIM ** -0.5

    def body(dq_ref, kc_ref, kn_ref, vc_ref, vn_ref, c_ref, s_ref, o_ref):
        last = pl.program_id(0) == nb - 1
        c, s = c_ref[...], s_ref[...]
        for h in range(N_KV_HEADS):
            for g in range(grp):
                col = (h * grp + g) * HEAD_DIM
                y = dq_ref[h, g * BLOCK:(g + 1) * BLOCK, :].astype(F32) * scale
                o_ref[:, col:col + HEAD_DIM] = _rope_t(y, c, s).astype(BF16)
            dk = kc_ref[h, BLOCK:, :] + jnp.where(last, 0.0, kn_ref[h, :BLOCK, :])
            col = (N_Q_HEADS + h) * HEAD_DIM
            o_ref[:, col:col + HEAD_DIM] = _rope_t(dk, c, s).astype(BF16)
            dv = vc_ref[h, BLOCK:, :] + jnp.where(last, 0.0, vn_ref[h, :BLOCK, :])
            col = (N_Q_HEADS + N_KV_HEADS + h) * HEAD_DIM
            o_ref[:, col:col + HEAD_DIM] = dv.astype(BF16)

    tab = pl.BlockSpec((BLOCK, HEAD_DIM // 2), lambda i: (i, 0))
    cur = pl.BlockSpec((N_KV_HEADS, None, 2 * BLOCK, HEAD_DIM), lambda i: (0, i, 0, 0))
    nxt = pl.BlockSpec((N_KV_HEADS, None, 2 * BLOCK, HEAD_DIM), lambda i: (0, jnp.minimum(i + 1, nb - 1), 0, 0))
    return pl.pallas_call(
        body, name="qkv_merge_bwd", grid=(nb,), out_shape=jax.ShapeDtypeStruct((nb * BLOCK, width), BF16),
        in_specs=[pl.BlockSpec((N_KV_HEADS, grp * BLOCK, HEAD_DIM), lambda i: (0, i, 0)),
                  cur, nxt, cur, nxt, tab, tab],
        out_specs=pl.BlockSpec((BLOCK, width), lambda i: (i, 0)),
        compiler_params=_params(("parallel",)),
    )(dq, dkb, dkb, dvb, dvb, cos, sin)


def _tiles2d(r, c):
    tc = _pick(c, 2048, 128) if c % 128 == 0 else c
    tr = _pick(r, max(8, (1 << 20) // tc // 8 * 8), 8) if r % 8 == 0 else r
    return tr, tc


def _cast_bf16(name, w, place, wide):
    r, c = w.shape
    tr, tc = _tiles2d(r, c)
    if tr % 16:
        tr = r
    nc = c // tc

    def body(place_ref, w_ref, o_ref):
        o_ref[...] = w_ref[...].astype(BF16)

    if wide:
        out_shape = jax.ShapeDtypeStruct((r, N_CHIPS * c), BF16)
        out_spec = pl.BlockSpec((tr, tc), lambda i, j, p: (i, p[1] * nc + j))
    else:
        out_shape = jax.ShapeDtypeStruct((N_CHIPS, r, c), BF16)
        out_spec = pl.BlockSpec((None, tr, tc), lambda i, j, p: (p[1], i, j))
    return pl.pallas_call(
        body, name=name, out_shape=out_shape,
        grid_spec=pltpu.PrefetchScalarGridSpec(
            num_scalar_prefetch=1, grid=(r // tr, nc),
            in_specs=[pl.BlockSpec((tr, tc), lambda i, j, p: (i, j))], out_specs=out_spec),
        compiler_params=_params(("parallel", "parallel")),
    )(place, w)


def _pair_sum(name, g, got, place):
    n, r, c = g.shape
    half = r // 2
    tr, tc = _tiles2d(half, c)
    nh = half // tr

    def body(place_ref, g_ref, got_ref, o_ref, own_ref):
        s = (g_ref[...] + got_ref[...]).astype(BF16)
        o_ref[...] = s

        @pl.when(pl.program_id(2) == place_ref[1])
        def _():
            own_ref[...] = s

    tile = pl.BlockSpec((None, tr, tc), lambda i, j, k, p: (k, i, j))
    shape = jax.ShapeDtypeStruct((n, half, c), BF16)
    return pl.pallas_call(
        body, name=name, out_shape=(shape, shape),
        grid_spec=pltpu.PrefetchScalarGridSpec(
            num_scalar_prefetch=1, grid=(nh, c // tc, n),
            in_specs=[pl.BlockSpec((None, tr, tc), lambda i, j, k, p: (k, p[0] * nh + i, j)), tile],
            out_specs=(tile, pl.BlockSpec((None, tr, tc), lambda i, j, k, p: (p[1], i, j)))),
        compiler_params=_params(("parallel", "parallel", "arbitrary")),
    )(place, g, got)


def _chip_sum(name, parts, place):
    n, half, c = parts.shape
    tr, tc = _tiles2d(half, c)
    nh = half // tr

    def body(place_ref, p0, p1, p2, p3, o_ref):
        o_ref[...] = ((p0[...].astype(F32) + p1[...].astype(F32)) + p2[...].astype(F32)) + p3[...].astype(F32)

    def chip(k):
        return pl.BlockSpec((None, tr, tc), lambda i, j, p: (k, i, j))

    return pl.pallas_call(
        body, name=name, out_shape=jax.ShapeDtypeStruct((2 * half, c), F32),
        grid_spec=pltpu.PrefetchScalarGridSpec(
            num_scalar_prefetch=1, grid=(nh, c // tc),
            in_specs=[chip(k) for k in range(n)],
            out_specs=pl.BlockSpec((tr, tc), lambda i, j, p: (p[0] * nh + i, j))),
        compiler_params=_params(("parallel", "parallel")),
    )(place, parts, parts, parts, parts)


def _dev_sum(gathered):
    def body(g_ref, o_ref):
        acc = g_ref[0]
        for k in range(1, N_DEV):
            acc = acc + g_ref[k]
        o_ref[...] = acc

    return pl.pallas_call(body, name="dev_sum", out_shape=jax.ShapeDtypeStruct(gathered.shape[1:], F32))(gathered)


def _adamw(name, w, g, m, v):
    r, c = w.shape
    tr, tc = _tiles2d(r, c)
    if r % 8 == 0:
        tr = _pick(r, max(8, (1 << 18) // tc // 8 * 8), 8)

    def body(w_ref, g_ref, m_ref, v_ref, d_ref, mo_ref, vo_ref):
        gv = g_ref[...]
        mn = ADAM_B1 * m_ref[...] + (1.0 - ADAM_B1) * gv
        vn = ADAM_B2 * v_ref[...] + (1.0 - ADAM_B2) * jnp.square(gv)
        m_hat = mn / (1.0 - ADAM_B1 ** ADAM_STEP)
        v_hat = vn / (1.0 - ADAM_B2 ** ADAM_STEP)
        d_ref[...] = -ADAM_LR * (m_hat / (jnp.sqrt(v_hat) + ADAM_EPS) + ADAM_WD * w_ref[...])
        mo_ref[...] = mn
        vo_ref[...] = vn

    tile = pl.BlockSpec((tr, tc), lambda i, j: (i, j))
    shape = jax.ShapeDtypeStruct((r, c), F32)
    return pl.pallas_call(
        body, name=name, grid=(r // tr, c // tc), out_shape=(shape, shape, shape),
        in_specs=[tile] * 4, out_specs=(tile,) * 3, compiler_params=_params(("parallel", "parallel")),
    )(w, g, m, v)


MATRICES = ("w_in_conv", "w_out_conv", "w_up_0", "w_down_0", "w_qkv", "w_o", "w_up_1", "w_down_1")
COLUMN_SHARDED = ("w_in_conv", "w_up_0", "w_qkv", "w_up_1")
NORMS = ("norm_mix_0", "norm_mlp_0", "norm_mix_1", "norm_mlp_1", "norm_final")


def _rows(stack):
    return stack.reshape(N_CHIPS * stack.shape[1], stack.shape[2])


def _stack(full):
    return full.reshape(N_CHIPS, full.shape[0] // N_CHIPS, full.shape[1])


def _add_residual(acc, res):
    return acc + res


def _relu_sq_grad(acc, z):
    return acc * (2.0 * jnp.maximum(z.astype(F32), 0.0))


def _step(x, target, stacks, small, norms, sinks, place):
    d = D_MODEL
    dc = d // N_CHIPS
    pad = ROW0 - N_META
    m = x.shape[0] + ROW0
    grp = N_Q_HEADS // N_KV_HEADS
    cos, sin = _rope_tables(m)
    sink_rows = jnp.repeat(sinks.astype(F32), BLOCK).reshape(N_KV_HEADS, grp * BLOCK, 1)

    def gather(*names):
        return _gather_task([stacks[n] for n in names])

    def pair_sum(tag, grad, got):
        return _pair_sum("pair_sum_" + tag, grad, got, place)

    def chip_sum(tag, landed):
        return _chip_sum("chip_sum_" + tag, landed, place)

    (w_in, small_all), = _run("gather_first", [_gather_task([stacks["w_in_conv"], small])])
    small_full = jnp.transpose(small_all, (1, 0, 2)).reshape(SMALL_ROWS, d)
    conv_w8 = small_full[N_META:N_META + 8]
    h0 = jnp.concatenate([jnp.zeros((pad, d), F32), small_full[:N_META], x], axis=0)

    n0 = _rms_fwd("norm_mix_0", h0, norms["norm_mix_0"])
    bcu, ((w_out, w_up0),) = _mm_nn("conv_in", n0, w_in, BF16, tasks=[gather("w_out_conv", "w_up_0")])
    gate = _conv_fwd(bcu, conv_w8)
    h1 = _mm_nn("conv_out", gate, _rows(w_out), F32, epi=_add_residual, extras=(h0,))
    n1 = _rms_fwd("norm_mlp_0", h1, norms["norm_mlp_0"])
    z0, ((w_down0,),) = _mm_nn("mlp_up_0", n1, w_up0, BF16, tasks=[gather("w_down_0")])
    h2, ((w_qkv, w_o, w_up1),) = _mm_nn("mlp_down_0", z0, _rows(w_down0), F32, a_pro=_relu_sq, epi=_add_residual,
                                             extras=(h1,), tasks=[gather("w_qkv", "w_o", "w_up_1")])
    n2 = _rms_fwd("norm_mix_1", h2, norms["norm_mix_1"])
    qkv = _mm_nn("attn_qkv", n2, w_qkv, BF16)
    q, k, v = _qkv_split(qkv, cos, sin)
    o = _heads_merge("attn_o_merge", _attn_fwd(q, k, v, sink_rows))
    h3 = _mm_nn("attn_out", o, _rows(w_o), F32, epi=_add_residual, extras=(h2,))
    n3 = _rms_fwd("norm_mlp_1", h3, norms["norm_mlp_1"])
    z1, ((w_down1,),) = _mm_nn("mlp_up_1", n3, w_up1, BF16, tasks=[gather("w_down_1")])
    h4 = _mm_nn("mlp_down_1", z1, _rows(w_down1), F32, a_pro=_relu_sq, epi=_add_residual, extras=(h3,))

    gn = {}
    loss, dh, dh_bf, gn["norm_final"] = _loss_head(h4, norms["norm_final"], target)
    dz = _mm_nt("mlp_down_dx_1", dh_bf, _rows(w_down1), BF16, epi=_relu_sq_grad, extras=(z1,))
    g_d1 = _stack(_mm_tn("mlp_down_dw_1", z1, dh_bf, stacked=False, a_pro=_relu_sq))
    g_u1, ((got,),) = _mm_tn("mlp_up_dw_1", n3, dz, stacked=True, tasks=[_pair_exchange_task([g_d1])])
    s_d1 = pair_sum("d1", g_d1, got)
    dn, ((got,), (landed,)) = _mm_nt("mlp_up_dx_1", dz, w_up1, F32,
                                          tasks=[_pair_exchange_task([g_u1]), _chip_exchange_task([s_d1])])
    s_u1, b_d1 = pair_sum("u1", g_u1, got), chip_sum("d1", landed)
    dh, dh_bf, gn["norm_mlp_1"] = _rms_bwd("norm_mlp_bwd_1", dn, h3, norms["norm_mlp_1"], dh)
    do = _mm_nt("attn_out_dx", dh_bf, _rows(w_o), BF16)
    g_o = _stack(_mm_tn("attn_out_dw", o, dh_bf, stacked=False))
    dq, dkb, dvb, dsink = _attn_bwd(q, k, v, sink_rows, _heads_split("attn_do_split", do))
    dqkv = _qkv_merge_bwd(dq, dkb, dvb, cos, sin)
    g_qkv, ((got,),) = _mm_tn("attn_qkv_dw", n2, dqkv, stacked=True, tasks=[_pair_exchange_task([g_o])])
    s_o = pair_sum("o", g_o, got)
    dn, ((got,), (landed,)) = _mm_nt("attn_qkv_dx", dqkv, w_qkv, F32,
                                          tasks=[_pair_exchange_task([g_qkv]), _chip_exchange_task([s_u1])])
    s_qkv, b_u1 = pair_sum("qkv", g_qkv, got), chip_sum("u1", landed)
    dh, dh_bf, gn["norm_mix_1"] = _rms_bwd("norm_mix_bwd_1", dn, h2, norms["norm_mix_1"], dh)
    dz, ((landed_o, landed_qkv), (r_d1,)) = _mm_nt(
        "mlp_down_dx_0", dh_bf, _rows(w_down0), BF16, epi=_relu_sq_grad, extras=(z0,),
        tasks=[_chip_exchange_task([s_o, s_qkv]), _pair_share_task([b_d1])])
    b_o, b_qkv = chip_sum("o", landed_o), chip_sum("qkv", landed_qkv)
    g_d0, ((r_u1,),) = _mm_tn("mlp_down_dw_0", z0, dh_bf, stacked=False, a_pro=_relu_sq, tasks=[_pair_share_task([b_u1])])
    g_d0 = _stack(g_d0)
    g_u0, ((got,), (r_o, r_qkv)) = _mm_tn("mlp_up_dw_0", n1, dz, stacked=True,
                                          tasks=[_pair_exchange_task([g_d0]), _pair_share_task([b_o, b_qkv])])
    s_d0 = pair_sum("d0", g_d0, got)
    dn, ((got,), (landed,)) = _mm_nt("mlp_up_dx_0", dz, w_up0, F32,
                                          tasks=[_pair_exchange_task([g_u0]), _chip_exchange_task([s_d0])])
    s_u0, b_d0 = pair_sum("u0", g_u0, got), chip_sum("d0", landed)
    dh, dh_bf, gn["norm_mlp_0"] = _rms_bwd("norm_mlp_bwd_0", dn, h1, norms["norm_mlp_0"], dh)
    dgate = _mm_nt("conv_out_dx", dh_bf, _rows(w_out), BF16)
    dbcu, g_conv_w = _conv_bwd(bcu, conv_w8, dgate)
    g_in, ((landed,), (r_d0,)) = _mm_tn("conv_in_dw", n0, dbcu, stacked=True,
                                        tasks=[_chip_exchange_task([s_u0]), _pair_share_task([b_d0])])
    b_u0 = chip_sum("u0", landed)
    dn, ((got,), (r_u0,)) = _mm_nt("conv_in_dx", dbcu, w_in, F32,
                                        tasks=[_pair_exchange_task([g_in]), _pair_share_task([b_u0])])
    s_in = pair_sum("in", g_in, got)
    dh0, _, gn["norm_mix_0"] = _rms_bwd("norm_mix_bwd_0", dn, h0, norms["norm_mix_0"], dh)

    g_small = jnp.zeros((SMALL_ROWS, d), F32).at[:N_META].set(dh0[pad:ROW0]).at[N_META:N_META + 8].set(g_conv_w)
    g_small = jnp.transpose(g_small.reshape(SMALL_ROWS, N_CHIPS, dc), (1, 0, 2))
    rep = jnp.zeros((8, d), F32)
    for r, n in enumerate(NORMS):
        rep = rep.at[r].set(jnp.sum(gn[n], axis=0))
    rep = rep.at[len(NORMS), :N_Q_HEADS].set(jnp.sum(dsink.reshape(N_Q_HEADS, BLOCK), axis=1))
    g_out, ((landed,), (got, rep_all)) = _mm_tn(
        "conv_out_dw", gate, dh_bf, stacked=False,
        tasks=[_chip_exchange_task([s_in]), _pair_exchange_task([g_small], small=rep)])
    g_out = _stack(g_out)
    b_in, s_small = chip_sum("in", landed), pair_sum("small", g_small, got)
    (got,), = _run("tail_pair_exchange", [_pair_exchange_task([g_out])])
    s_out = pair_sum("out", g_out, got)
    (landed_out, landed_small), = _run("tail_chip_exchange", [_chip_exchange_task([s_out, s_small])])
    b_out, b_small = chip_sum("out", landed_out), chip_sum("small", landed_small)
    (r_out, r_small, r_in), = _run("tail_pair_share", [_pair_share_task([b_out, b_small, b_in])])

    reduced = {"w_in_conv": r_in, "w_out_conv": r_out, "w_up_0": r_u0, "w_down_0": r_d0, "w_qkv": r_qkv, "w_o": r_o,
               "w_up_1": r_u1, "w_down_1": r_d1}
    return loss, dh0, reduced, r_small, rep_all


def kernel(x, meta_tokens, norm_mix_0, w_in_conv, conv_w, w_out_conv, norm_mlp_0, w_up_0, w_down_0, norm_mix_1, w_qkv, attn_sinks, w_o, norm_mlp_1, w_up_1, w_down_1, norm_final, loss_target, m_meta_tokens, m_norm_mix_0, m_w_in_conv, m_conv_w, m_w_out_conv, m_norm_mlp_0, m_w_up_0, m_w_down_0, m_norm_mix_1, m_w_qkv, m_attn_sinks, m_w_o, m_norm_mlp_1, m_w_up_1, m_w_down_1, m_norm_final, v_meta_tokens, v_norm_mix_0, v_w_in_conv, v_conv_w, v_w_out_conv, v_norm_mlp_0, v_w_up_0, v_w_down_0, v_norm_mix_1, v_w_qkv, v_attn_sinks, v_w_o, v_norm_mlp_1, v_w_up_1, v_w_down_1, v_norm_final):
    given = dict(locals())
    names = ("meta_tokens", "norm_mix_0", "w_in_conv", "conv_w", "w_out_conv", "norm_mlp_0", "w_up_0", "w_down_0",
             "norm_mix_1", "w_qkv", "attn_sinks", "w_o", "norm_mlp_1", "w_up_1", "w_down_1", "norm_final")
    d = D_MODEL
    dc = d // N_CHIPS
    chip = 2 * lax.axis_index("x") + lax.axis_index("y")
    place = jnp.stack([lax.axis_index("c"), chip]).astype(jnp.int32)

    small = jnp.zeros((SMALL_ROWS, dc), F32).at[:N_META].set(meta_tokens).at[N_META:N_META + CONV_WIDTH].set(conv_w)
    small = lax.dynamic_update_slice(jnp.zeros((N_CHIPS, SMALL_ROWS, dc), F32), small[None], (chip, 0, 0))
    stacks = {n: _cast_bf16("cast_" + n, given[n], place, n in COLUMN_SHARDED) for n in MATRICES}

    norms = {n: given[n] for n in NORMS}
    loss_part, dh0, g_out, r_small, rep_all = _step(x[0], loss_target[0], stacks, small, norms, attn_sinks, place)
    loss = lax.psum(loss_part[0, 0], ("x", "y", "c"))
    grad_x = dh0[ROW0:][None]
    rep_sum = _dev_sum(rep_all)
    g_out["meta_tokens"] = r_small[:N_META]
    g_out["conv_w"] = r_small[N_META:N_META + CONV_WIDTH]
    for r, n in enumerate(NORMS):
        g_out[n] = rep_sum[r]
    g_out["attn_sinks"] = rep_sum[len(NORMS), :N_Q_HEADS]

    delta, new_m, new_v = {}, {}, {}
    for n in names:
        wt = given[n]
        shape2 = wt.shape if wt.ndim == 2 else (1, wt.shape[0])
        outs = _adamw("adamw_" + n, wt.reshape(shape2), g_out[n].reshape(shape2),
                      given["m_" + n].reshape(shape2), given["v_" + n].reshape(shape2))
        delta[n], new_m[n], new_v[n] = [o.reshape(wt.shape) for o in outs]
    return (loss, grad_x, *[g_out[n] for n in names], *[delta[n] for n in names],
            *[new_m[n] for n in names], *[new_v[n] for n in names])
```

```python
import functools

import jax
import jax.numpy as jnp
from jax import lax
from jax.experimental import pallas as pl
from jax.experimental.pallas import tpu as pltpu

F32 = jnp.float32
BF16 = jnp.bfloat16

D_MODEL = 2048
SEQ = 8192
N_META = 16
CONV_WIDTH = 3
HEAD_DIM = 64
N_Q_HEADS = 32
N_KV_HEADS = 4
BLOCK = 128
ROPE_THETA = 10000.0
D_FF = 4 * D_MODEL
RMS_EPS = 1e-5
NEG_INF = -1e30

ADAM_LR = 0.001
ADAM_B1 = 0.9
ADAM_B2 = 0.999
ADAM_EPS = 1e-08
ADAM_WD = 0.01
ADAM_STEP = 10

N_CHIPS = 4
N_DEV = 8
MESH = pl.DeviceIdType.MESH
VMEM_LIMIT = 56 * 1024 * 1024
SMALL_ROWS = 32
ROW0 = BLOCK


def _pick(n, target, mult):
    best = None
    for t in range(mult, min(n, target) + 1, mult):
        if n % t == 0:
            best = t
    assert best is not None, (n, target, mult)
    return best


def _params(sem=None):
    return pltpu.CompilerParams(dimension_semantics=sem, vmem_limit_bytes=VMEM_LIMIT)


HBM_SPEC = pl.BlockSpec(memory_space=pltpu.HBM)


class _Task:
    def __init__(self, inputs, outputs, aliases, sem_shapes, bind):
        self.inputs, self.outputs, self.aliases = list(inputs), list(outputs), dict(aliases)
        self.sem_shapes, self.bind = list(sem_shapes), bind


def _like(arrays):
    return [jax.ShapeDtypeStruct(a.shape, a.dtype) for a in arrays]


def _bind_tasks(tasks, in_refs, out_refs, sem_refs):
    bound, i, o, s = [], 0, 0, 0
    for t in tasks:
        ni, no, ns = len(t.inputs), len(t.outputs), len(t.sem_shapes)
        bound.append(t.bind(in_refs[i:i + ni], out_refs[o:o + no], sem_refs[s:s + ns]))
        i, o, s = i + ni, o + no, s + ns
    return bound


def _run_phase(bound, phase):
    for b in bound:
        if b[phase] is not None:
            b[phase]()


def _task_plumbing(tasks, in_offset, out_offset):
    ins = [a for t in tasks for a in t.inputs]
    outs = [o for t in tasks for o in t.outputs]
    sems = [s for t in tasks for s in t.sem_shapes]
    aliases, i, o = {}, in_offset, out_offset
    for t in tasks:
        for src, dst in t.aliases.items():
            aliases[i + src] = o + dst
        i, o = i + len(t.inputs), o + len(t.outputs)
    return ins, outs, sems, aliases


def _split_outputs(tasks, flat):
    res, o = [], 0
    for t in tasks:
        res.append(list(flat[o:o + len(t.outputs)]))
        o += len(t.outputs)
    return res


def _run(name, tasks):
    ins, outs, sems, aliases = _task_plumbing(tasks, 0, 0)

    def body(*refs):
        bound = _bind_tasks(tasks, refs[:len(ins)], refs[len(ins):len(ins) + len(outs)], refs[len(ins) + len(outs):])
        for phase in range(3):
            _run_phase(bound, phase)

    flat = pl.pallas_call(
        body, name=name, out_shape=outs, in_specs=[HBM_SPEC] * len(ins), out_specs=[HBM_SPEC] * len(outs),
        input_output_aliases=aliases, scratch_shapes=sems,
    )(*ins)
    return _split_outputs(tasks, flat)


def _place():
    x, y, c = lax.axis_index("x"), lax.axis_index("y"), lax.axis_index("c")
    chips = [(1 - x, y), (x, 1 - y), (1 - x, 1 - y)]
    return x, y, c, 2 * x + y, chips


def _gather_task(stacks):
    n = len(stacks)
    halves = [s.shape[-2] // 2 for s in stacks]

    def bind(_, dst, sems):
        send_a, recv_a, send_b, recv_b = sems
        x, y, c, me, chips = _place()
        sibling = (x, y, 1 - c)

        def half(w, chip, hc):
            rows = pl.ds(hc * halves[w], halves[w])
            if len(stacks[w].shape) == 3:
                return dst[w].at[chip, rows, :]
            cols = stacks[w].shape[1] // N_CHIPS
            return dst[w].at[rows, pl.ds(chip * cols, cols)]

        def over_ici(j, w, block):
            return pltpu.make_async_remote_copy(
                src_ref=half(w, block, c), dst_ref=half(w, block, c), send_sem=send_a.at[j * n + w],
                recv_sem=recv_a.at[j * n + w], device_id=(*chips[j], c), device_id_type=MESH)

        def over_d2d(j, w, hc):
            got = half(w, 2 * chips[j][0] + chips[j][1], hc)
            return pltpu.make_async_remote_copy(
                src_ref=got, dst_ref=got, send_sem=send_b.at[j * n + w], recv_sem=recv_b.at[j * n + w],
                device_id=sibling, device_id_type=MESH)

        pairs = [(j, w) for j in range(3) for w in range(n)]

        def start():
            for j, w in pairs:
                over_ici(j, w, me).start()

        def mid():
            for j, w in pairs:
                over_ici(j, w, 2 * chips[j][0] + chips[j][1]).wait_recv()
                over_d2d(j, w, c).start()

        def finish():
            for j, w in pairs:
                over_d2d(j, w, 1 - c).wait_recv()
            for j, w in pairs:
                over_ici(j, w, me).wait_send()
                over_d2d(j, w, c).wait_send()

        return start, mid, finish

    return _Task(stacks, _like(stacks), {w: w for w in range(n)}, [pltpu.SemaphoreType.DMA((3 * n,))] * 4, bind)


def _pair_exchange_task(grads, small=None):
    n = len(grads)
    halves = [g.shape[1] // 2 for g in grads]

    def bind(src, dst, sems):
        send, recv = sems[0], sems[1]
        x, y, c, me, _ = _place()
        sibling = (x, y, 1 - c)
        dev = 2 * me + c

        def to_sibling(w):
            return pltpu.make_async_remote_copy(
                src_ref=src[w].at[:, pl.ds((1 - c) * halves[w], halves[w]), :], dst_ref=dst[w],
                send_sem=send.at[w], recv_sem=recv.at[w], device_id=sibling, device_id_type=MESH)

        def to_peer(t, block):
            tx, ty, tc = (t >> 2) & 1, (t >> 1) & 1, t & 1
            return pltpu.make_async_remote_copy(
                src_ref=src[n], dst_ref=dst[n].at[block], send_sem=sems[2].at[t], recv_sem=sems[3].at[t],
                device_id=(x ^ tx, y ^ ty, c ^ tc), device_id_type=MESH)

        def mine():
            return pltpu.make_async_copy(src[n], dst[n].at[dev], sems[4])

        def start():
            for w in range(n):
                to_sibling(w).start()
            if small is not None:
                mine().start()
                for t in range(1, N_DEV):
                    to_peer(t, dev).start()

        def finish():
            for w in range(n):
                to_sibling(w).wait_recv()
            if small is not None:
                for t in range(1, N_DEV):
                    to_peer(t, dev ^ t).wait_recv()
            for w in range(n):
                to_sibling(w).wait_send()
            if small is not None:
                for t in range(1, N_DEV):
                    to_peer(t, dev).wait_send()
                mine().wait()

        return start, None, finish

    outputs = [jax.ShapeDtypeStruct((N_CHIPS, h, g.shape[2]), g.dtype) for g, h in zip(grads, halves)]
    sem_shapes = [pltpu.SemaphoreType.DMA((n,)), pltpu.SemaphoreType.DMA((n,))]
    inputs = list(grads)
    if small is not None:
        inputs.append(small)
        outputs.append(jax.ShapeDtypeStruct((N_DEV,) + small.shape, small.dtype))
        sem_shapes += [pltpu.SemaphoreType.DMA((N_DEV,)), pltpu.SemaphoreType.DMA((N_DEV,)), pltpu.SemaphoreType.DMA(())]
    return _Task(inputs, outputs, {}, sem_shapes, bind)


def _chip_exchange_task(summed):
    n = len(summed)

    def bind(refs, dst, sems):
        src = refs[:n]
        send, recv = sems
        x, y, c, me, chips = _place()

        def copy(j, w, block_from, block_to):
            return pltpu.make_async_remote_copy(
                src_ref=src[w].at[block_from], dst_ref=dst[w].at[block_to], send_sem=send.at[j * n + w],
                recv_sem=recv.at[j * n + w], device_id=(*chips[j], c), device_id_type=MESH)

        pairs = [(j, w) for j in range(3) for w in range(n)]

        def start():
            for j, w in pairs:
                copy(j, w, 2 * chips[j][0] + chips[j][1], me).start()

        def finish():
            for j, w in pairs:
                copy(j, w, me, 2 * chips[j][0] + chips[j][1]).wait_recv()
            for j, w in pairs:
                copy(j, w, 2 * chips[j][0] + chips[j][1], me).wait_send()

        return start, None, finish

    partials, landing = [s[0] for s in summed], [s[1] for s in summed]
    return _Task(partials + landing, _like(landing), {n + w: w for w in range(n)},
                 [pltpu.SemaphoreType.DMA((3 * n,))] * 2, bind)


def _pair_share_task(blocks):
    n = len(blocks)

    def bind(_, dst, sems):
        send, recv = sems
        x, y, c, _, _ = _place()

        def copy(w, hc):
            h = blocks[w].shape[0] // 2
            rows = dst[w].at[pl.ds(hc * h, h), :]
            return pltpu.make_async_remote_copy(src_ref=rows, dst_ref=rows, send_sem=send.at[w], recv_sem=recv.at[w],
                                                device_id=(x, y, 1 - c), device_id_type=MESH)

        def start():
            for w in range(n):
                copy(w, c).start()

        def finish():
            for w in range(n):
                copy(w, 1 - c).wait_recv()
            for w in range(n):
                copy(w, c).wait_send()

        return start, None, finish

    return _Task(blocks, _like(blocks), {w: w for w in range(n)}, [pltpu.SemaphoreType.DMA((n,))] * 2, bind)


def _mm(name, a, b, *, dims, grid, a_spec, b_spec, out_shape, out_spec,
        extras=(), extra_specs=(), a_pro=None, epi=None, tasks=()):
    n_ex = len(extras)
    t_ins, t_outs, t_sems, aliases = _task_plumbing(tasks, 2 + n_ex, 1)
    n_ti, n_to = len(t_ins), len(t_outs)
    total = grid[0] * grid[1]
    mid_step = max(0, total - 1 - max(1, total // 8))

    def body(*refs):
        a_ref, b_ref = refs[0], refs[1]
        ex = refs[2:2 + n_ex]
        o_ref = refs[2 + n_ex + n_ti]
        if tasks:
            bound = _bind_tasks(tasks, refs[2 + n_ex:2 + n_ex + n_ti],
                                refs[3 + n_ex + n_ti:3 + n_ex + n_ti + n_to], refs[3 + n_ex + n_ti + n_to:])
            step = pl.program_id(0) * grid[1] + pl.program_id(1)

            @pl.when(step == 0)
            def _():
                _run_phase(bound, 0)

        av = a_ref[...]
        if a_pro is not None:
            av = a_pro(av)
        acc = lax.dot_general(av, b_ref[...], dims, preferred_element_type=F32)
        if epi is not None:
            acc = epi(acc, *[e[...] for e in ex])
        o_ref[...] = acc.astype(o_ref.dtype)

        if tasks:
            @pl.when(step == mid_step)
            def _():
                _run_phase(bound, 1)

            @pl.when(step == total - 1)
            def _():
                _run_phase(bound, 2)

    sem = ("arbitrary", "arbitrary") if tasks else ("parallel", "parallel")
    res = pl.pallas_call(
        body, name=name, grid=grid, out_shape=[out_shape, *t_outs],
        in_specs=[a_spec, b_spec, *extra_specs, *[HBM_SPEC] * n_ti], out_specs=[out_spec, *[HBM_SPEC] * n_to],
        input_output_aliases=aliases, scratch_shapes=t_sems, compiler_params=_params(sem),
    )(a, b, *extras, *t_ins)
    return (res[0], _split_outputs(tasks, res[1:])) if tasks else res[0]


_NN = (((1,), (0,)), ((), ()))
_NT = (((1,), (1,)), ((), ()))
_TN = (((0,), (0,)), ((), ()))


MM_TILE_BUDGET = 46 * 1024 * 1024


def _mm_tiles(m, n, contraction, out_bytes):
    for rows, cols in ((1664, 1024), (832, 1024), (416, 1024), (416, 512)):
        tm, tn = _pick(m, rows, 16), _pick(n, cols, 128)
        if 2 * 2 * contraction * (tm + tn) + tm * tn * (4 + 2 * out_bytes) <= MM_TILE_BUDGET:
            break
    return tm, tn


def _out_bytes(out_dtype, extras):
    return jnp.dtype(out_dtype).itemsize + sum(e.dtype.itemsize for e in extras)


def _mm_nn(name, a, w, out_dtype, a_pro=None, epi=None, extras=(), tasks=()):
    m, k = a.shape
    _, n = w.shape
    tm, tn = _mm_tiles(m, n, k, _out_bytes(out_dtype, extras))
    tile = pl.BlockSpec((tm, tn), lambda j, i: (i, j))
    return _mm(name, a, w, dims=_NN, grid=(n // tn, m // tm),
               a_spec=pl.BlockSpec((tm, k), lambda j, i: (i, 0)), b_spec=pl.BlockSpec((k, tn), lambda j, i: (0, j)),
               out_shape=jax.ShapeDtypeStruct((m, n), out_dtype), out_spec=tile,
               extras=extras, extra_specs=[tile] * len(extras), a_pro=a_pro, epi=epi, tasks=tasks)


def _mm_nt(name, a, w, out_dtype, epi=None, extras=(), tasks=()):
    m, c = a.shape
    r, _ = w.shape
    tm, tn = _mm_tiles(m, r, c, _out_bytes(out_dtype, extras))
    tile = pl.BlockSpec((tm, tn), lambda j, i: (i, j))
    return _mm(name, a, w, dims=_NT, grid=(r // tn, m // tm),
               a_spec=pl.BlockSpec((tm, c), lambda j, i: (i, 0)), b_spec=pl.BlockSpec((tn, c), lambda j, i: (j, 0)),
               out_shape=jax.ShapeDtypeStruct((m, r), out_dtype), out_spec=tile,
               extras=extras, extra_specs=[tile] * len(extras), epi=epi, tasks=tasks)


def _mm_tn(name, a, b, stacked, a_pro=None, tasks=()):
    t, ka = a.shape
    _, nb = b.shape
    ns = nb // N_CHIPS if stacked else nb
    ta, tb = _pick(ka, 512, 128), _pick(ns, 640, 128)
    if stacked:
        per = ns // tb
        out_shape = jax.ShapeDtypeStruct((N_CHIPS, ka, ns), F32)
        out_spec = pl.BlockSpec((None, ta, tb), lambda i, j: (j // per, i, j % per))
    else:
        out_shape = jax.ShapeDtypeStruct((ka, nb), F32)
        out_spec = pl.BlockSpec((ta, tb), lambda i, j: (i, j))
    return _mm(name, a, b, dims=_TN, grid=(ka // ta, nb // tb),
               a_spec=pl.BlockSpec((t, ta), lambda i, j: (0, i)), b_spec=pl.BlockSpec((t, tb), lambda i, j: (0, j)),
               out_shape=out_shape, out_spec=out_spec, a_pro=a_pro, tasks=tasks)


def _relu_sq(z):
    a = jnp.maximum(z, 0)
    return a * a


def _rms_fwd(name, h, g):
    m, d = h.shape
    tr = _pick(m, 256, 16)

    def body(h_ref, g_ref, o_ref):
        x = h_ref[...]
        rstd = lax.rsqrt(jnp.mean(x * x, axis=-1, keepdims=True) + RMS_EPS)
        o_ref[...] = ((x * rstd) * g_ref[...]).astype(BF16)

    row = pl.BlockSpec((tr, d), lambda i: (i, 0))
    return pl.pallas_call(
        body, name=name, grid=(m // tr,), out_shape=jax.ShapeDtypeStruct((m, d), BF16),
        in_specs=[row, pl.BlockSpec((1, d), lambda i: (0, 0))], out_specs=row,
        compiler_params=_params(("parallel",)),
    )(h, g.reshape(1, d))


def _rms_bwd_math(x, g, dn):
    rstd = lax.rsqrt(jnp.mean(x * x, axis=-1, keepdims=True) + RMS_EPS)
    xhat = x * rstd
    dxhat = dn * g
    dx = rstd * (dxhat - xhat * jnp.mean(dxhat * xhat, axis=-1, keepdims=True))
    return dx, dn * xhat


def _fold8(v):
    r, c = v.shape
    return jnp.sum(v.reshape(r // 8, 8, c), axis=0)


def _rms_bwd(name, dn, h, g, dh_in):
    m, d = h.shape
    tr = _pick(m, 256, 16)
    nt = m // tr

    def body(dn_ref, h_ref, g_ref, dh_ref, o_ref, ob_ref, dg_ref):
        dx, dgp = _rms_bwd_math(h_ref[...], g_ref[...], dn_ref[...])
        dh = dh_ref[...] + dx
        o_ref[...] = dh
        ob_ref[...] = dh.astype(BF16)

        @pl.when(pl.program_id(0) == 0)
        def _():
            dg_ref[...] = jnp.zeros_like(dg_ref)

        dg_ref[...] += _fold8(dgp)

    row = pl.BlockSpec((tr, d), lambda i: (i, 0))
    return pl.pallas_call(
        body, name=name, grid=(nt,),
        out_shape=(jax.ShapeDtypeStruct((m, d), F32), jax.ShapeDtypeStruct((m, d), BF16),
                   jax.ShapeDtypeStruct((8, d), F32)),
        in_specs=[row, row, pl.BlockSpec((1, d), lambda i: (0, 0)), row],
        out_specs=(row, row, pl.BlockSpec((8, d), lambda i: (0, 0))),
        compiler_params=_params(("arbitrary",)),
    )(dn, h, g.reshape(1, d), dh_in)


def _rms_bwd_tokens(name, dn, h, g, dh_in):
    m, d = h.shape
    nb = m // BLOCK

    def body(dn_ref, h_ref, g_ref, dh_ref, gx_ref, first_ref, dg_ref):
        i = pl.program_id(0)
        dx, dgp = _rms_bwd_math(h_ref[...], g_ref[...], dn_ref[...])
        dh = dh_ref[...] + dx
        gx_ref[...] = dh

        @pl.when(i == 0)
        def _():
            first_ref[...] = dh
            dg_ref[...] = jnp.zeros_like(dg_ref)

        dg_ref[...] += _fold8(dgp)

    row = pl.BlockSpec((BLOCK, d), lambda i: (i, 0))
    return pl.pallas_call(
        body, name=name, grid=(nb,),
        out_shape=(jax.ShapeDtypeStruct((m - ROW0, d), F32), jax.ShapeDtypeStruct((ROW0, d), F32),
                   jax.ShapeDtypeStruct((8, d), F32)),
        in_specs=[row, row, pl.BlockSpec((1, d), lambda i: (0, 0)), row],
        out_specs=(pl.BlockSpec((BLOCK, d), lambda i: (jnp.maximum(i - 1, 0), 0)),
                   pl.BlockSpec((ROW0, d), lambda i: (0, 0)), pl.BlockSpec((8, d), lambda i: (0, 0))),
        compiler_params=_params(("arbitrary",)),
    )(dn, h, g.reshape(1, d), dh_in)


def _loss_head(h, g, target):
    m, d = h.shape
    tr = BLOCK

    def body(h_ref, g_ref, t_ref, loss_ref, o_ref, ob_ref, dg_ref):
        i = pl.program_id(0)
        x = h_ref[...]
        gv = g_ref[...]
        rstd = lax.rsqrt(jnp.mean(x * x, axis=-1, keepdims=True) + RMS_EPS)
        err = jnp.where(i > 0, (x * rstd) * gv - t_ref[...], 0.0)
        dx, dgp = _rms_bwd_math(x, gv, err * (1.0 / d))
        o_ref[...] = dx
        ob_ref[...] = dx.astype(BF16)

        @pl.when(i == 0)
        def _():
            dg_ref[...] = jnp.zeros_like(dg_ref)
            loss_ref[...] = jnp.zeros_like(loss_ref)

        dg_ref[...] += _fold8(dgp)
        sq = jnp.mean(err * err, axis=-1, keepdims=True)
        loss_ref[...] += 0.5 * jnp.sum(sq, axis=0, keepdims=True)

    row = pl.BlockSpec((tr, d), lambda i: (i, 0))
    return pl.pallas_call(
        body, name="loss_head", grid=(m // tr,),
        out_shape=(jax.ShapeDtypeStruct((8, 128), F32), jax.ShapeDtypeStruct((m, d), F32),
                   jax.ShapeDtypeStruct((m, d), BF16), jax.ShapeDtypeStruct((8, d), F32)),
        in_specs=[row, pl.BlockSpec((1, d), lambda i: (0, 0)),
                  pl.BlockSpec((tr, d), lambda i: (jnp.maximum(i - 1, 0), 0))],
        out_specs=(pl.BlockSpec((8, 128), lambda i: (0, 0)), row, row,
                   pl.BlockSpec((8, d), lambda i: (0, 0))),
        compiler_params=_params(("arbitrary",)),
    )(h, g.reshape(1, d), target)


HALO = 16


def _shift_down(cat, k):
    return pltpu.roll(cat, k, axis=0)[HALO:]


def _shift_up(cat, k):
    n = cat.shape[0]
    return pltpu.roll(cat, n - k, axis=0)[:n - HALO]


def _conv_fwd(bcu, cw):
    m, d3 = bcu.shape
    d = d3 // 3
    tr, tc = _pick(m, 416, 16), _pick(d, 512, 128)
    hb = tr // HALO

    def body(x_ref, xb_ref, w_ref, o_ref):
        i = pl.program_id(0)
        for j in range(d // tc):
            col = slice(j * tc, (j + 1) * tc)
            cb, cc, cu = (slice(q * d + j * tc, q * d + (j + 1) * tc) for q in range(3))
            v = x_ref[:, cc].astype(F32) * x_ref[:, cu].astype(F32)
            vh = jnp.where(i > 0, xb_ref[:, cc].astype(F32) * xb_ref[:, cu].astype(F32), 0.0)
            cat = jnp.concatenate([vh, v], axis=0)
            w = w_ref[:, col]
            conv = w[2:3] * v + w[1:2] * _shift_down(cat, 1) + w[0:1] * _shift_down(cat, 2)
            o_ref[:, col] = (x_ref[:, cb].astype(F32) * conv).astype(BF16)

    return pl.pallas_call(
        body, name="conv_fwd", grid=(m // tr,), out_shape=jax.ShapeDtypeStruct((m, d), BF16),
        in_specs=[pl.BlockSpec((tr, d3), lambda i: (i, 0)),
                  pl.BlockSpec((HALO, d3), lambda i: (jnp.maximum(i * hb - 1, 0), 0)),
                  pl.BlockSpec((8, d), lambda i: (0, 0))],
        out_specs=pl.BlockSpec((tr, d), lambda i: (i, 0)),
        compiler_params=_params(("parallel",)),
    )(bcu, bcu, cw)


def _conv_bwd(bcu, cw, dg):
    m, d3 = bcu.shape
    d = d3 // 3
    tr, tc = _pick(m, 208, 16), _pick(d, 512, 128)
    hb, nt = tr // HALO, m // tr

    def body(x_ref, xb_ref, xa_ref, dg_ref, dga_ref, w_ref, o_ref, dw_ref):
        i = pl.program_id(0)

        @pl.when(i == 0)
        def _():
            dw_ref[...] = jnp.zeros_like(dw_ref)

        for j in range(d // tc):
            col = slice(j * tc, (j + 1) * tc)
            cb, cc, cu = (slice(q * d + j * tc, q * d + (j + 1) * tc) for q in range(3))
            w = w_ref[:, col]
            b, c, u = x_ref[:, cb].astype(F32), x_ref[:, cc].astype(F32), x_ref[:, cu].astype(F32)
            dgv = dg_ref[:, col].astype(F32)
            v = c * u
            vh = jnp.where(i > 0, xb_ref[:, cc].astype(F32) * xb_ref[:, cu].astype(F32), 0.0)
            cat = jnp.concatenate([vh, v], axis=0)
            v1, v2 = _shift_down(cat, 1), _shift_down(cat, 2)
            dconv = dgv * b
            o_ref[:, cb] = (dgv * (w[2:3] * v + w[1:2] * v1 + w[0:1] * v2)).astype(BF16)
            taps = [jnp.sum(dconv * t, axis=0, keepdims=True) for t in (v2, v1, v)]
            dw_ref[:, col] += jnp.concatenate(taps + [jnp.zeros((5, tc), F32)], axis=0)
            nxt = jnp.where(i < nt - 1, dga_ref[:, col].astype(F32) * xa_ref[:, cb].astype(F32), 0.0)
            cat2 = jnp.concatenate([dconv, nxt], axis=0)
            dv = w[2:3] * dconv + w[1:2] * _shift_up(cat2, 1) + w[0:1] * _shift_up(cat2, 2)
            o_ref[:, cc] = (dv * u).astype(BF16)
            o_ref[:, cu] = (dv * c).astype(BF16)

    def rows(width):
        return pl.BlockSpec((tr, width), lambda i: (i, 0))

    def before(width):
        return pl.BlockSpec((HALO, width), lambda i: (jnp.maximum(i * hb - 1, 0), 0))

    def after(width):
        return pl.BlockSpec((HALO, width), lambda i: (jnp.minimum((i + 1) * hb, m // HALO - 1), 0))

    return pl.pallas_call(
        body, name="conv_bwd", grid=(nt,),
        out_shape=(jax.ShapeDtypeStruct((m, d3), BF16), jax.ShapeDtypeStruct((8, d), F32)),
        in_specs=[rows(d3), before(d3), after(d3), rows(d), after(d), pl.BlockSpec((8, d), lambda i: (0, 0))],
        out_specs=(rows(d3), pl.BlockSpec((8, d), lambda i: (0, 0))),
        compiler_params=_params(("arbitrary",)),
    )(bcu, bcu, bcu, dg, dg, cw)


PAIR = 2 * HEAD_DIM


def _rope_tables(m):
    pad = ROW0 - N_META
    pos = jnp.arange(m, dtype=F32) - pad
    inv = ROPE_THETA ** (-jnp.arange(0, HEAD_DIM, 2, dtype=F32) / HEAD_DIM)
    ang = pos[:, None] * inv[None, :]
    cos, sin = jnp.cos(ang), jnp.sin(ang)
    return jnp.tile(jnp.concatenate([cos, cos], axis=1), (1, 2)), jnp.tile(jnp.concatenate([-sin, sin], axis=1), (1, 2))


def _rope_pair(x, c, s):
    half = HEAD_DIM // 2
    lane = lax.broadcasted_iota(jnp.int32, x.shape, 1)
    swapped = jnp.where(lane % HEAD_DIM < half, pltpu.roll(x, PAIR - half, axis=1), pltpu.roll(x, half, axis=1))
    return x * c + swapped * s


def _qkv_split(qkv, cos, sin):
    m = qkv.shape[0]
    nb, grp = m // BLOCK, N_Q_HEADS // N_KV_HEADS
    scale = HEAD_DIM ** -0.5
    k0, v0 = N_Q_HEADS * HEAD_DIM, (N_Q_HEADS + N_KV_HEADS) * HEAD_DIM

    def body(x_ref, c_ref, s_ref, q_ref, k_ref, v_ref):
        c, s = c_ref[...], s_ref[...]
        cq, sq = c * scale, s * scale
        for t in range(N_Q_HEADS // 2):
            r = _rope_pair(x_ref[:, t * PAIR:(t + 1) * PAIR].astype(F32), cq, sq).astype(BF16)
            for e in range(2):
                h, g = divmod(2 * t + e, grp)
                q_ref[h, g * BLOCK:(g + 1) * BLOCK, :] = r[:, e * HEAD_DIM:(e + 1) * HEAD_DIM]
        for t in range(N_KV_HEADS // 2):
            r = _rope_pair(x_ref[:, k0 + t * PAIR:k0 + (t + 1) * PAIR].astype(F32), c, s).astype(BF16)
            vv = x_ref[:, v0 + t * PAIR:v0 + (t + 1) * PAIR]
            for e in range(2):
                k_ref[2 * t + e] = r[:, e * HEAD_DIM:(e + 1) * HEAD_DIM]
                v_ref[2 * t + e] = vv[:, e * HEAD_DIM:(e + 1) * HEAD_DIM]

    tab = pl.BlockSpec((BLOCK, PAIR), lambda i: (i, 0))
    kv = pl.BlockSpec((N_KV_HEADS, BLOCK, HEAD_DIM), lambda i: (0, i, 0))
    return pl.pallas_call(
        body, name="qkv_split", grid=(nb,),
        out_shape=(jax.ShapeDtypeStruct((N_KV_HEADS, nb * grp * BLOCK, HEAD_DIM), BF16),
                   jax.ShapeDtypeStruct((N_KV_HEADS, m, HEAD_DIM), BF16),
                   jax.ShapeDtypeStruct((N_KV_HEADS, m, HEAD_DIM), BF16)),
        in_specs=[pl.BlockSpec((BLOCK, qkv.shape[1]), lambda i: (i, 0)), tab, tab],
        out_specs=(pl.BlockSpec((N_KV_HEADS, grp * BLOCK, HEAD_DIM), lambda i: (0, i, 0)), kv, kv),
        compiler_params=_params(("parallel",)),
    )(qkv, cos, sin)


def _heads_merge(name, o):
    grp = N_Q_HEADS // N_KV_HEADS
    nb = o.shape[1] // (grp * BLOCK)

    def body(o_ref, x_ref):
        for t in range(N_Q_HEADS // 2):
            pieces = []
            for e in range(2):
                h, g = divmod(2 * t + e, grp)
                pieces.append(o_ref[h, g * BLOCK:(g + 1) * BLOCK, :])
            x_ref[:, t * PAIR:(t + 1) * PAIR] = jnp.concatenate(pieces, axis=1)

    return pl.pallas_call(
        body, name=name, grid=(nb,),
        out_shape=jax.ShapeDtypeStruct((nb * BLOCK, N_Q_HEADS * HEAD_DIM), o.dtype),
        in_specs=[pl.BlockSpec((N_KV_HEADS, grp * BLOCK, HEAD_DIM), lambda i: (0, i, 0))],
        out_specs=pl.BlockSpec((BLOCK, N_Q_HEADS * HEAD_DIM), lambda i: (i, 0)),
        compiler_params=_params(("parallel",)),
    )(o)


def _heads_split(name, x):
    grp = N_Q_HEADS // N_KV_HEADS
    nb = x.shape[0] // BLOCK

    def body(x_ref, o_ref):
        for h in range(N_KV_HEADS):
            for g in range(grp):
                col = (h * grp + g) * HEAD_DIM
                o_ref[h, g * BLOCK:(g + 1) * BLOCK, :] = x_ref[:, col:col + HEAD_DIM]

    return pl.pallas_call(
        body, name=name, grid=(nb,),
        out_shape=jax.ShapeDtypeStruct((N_KV_HEADS, nb * grp * BLOCK, HEAD_DIM), x.dtype),
        in_specs=[pl.BlockSpec((BLOCK, N_Q_HEADS * HEAD_DIM), lambda i: (i, 0))],
        out_specs=pl.BlockSpec((N_KV_HEADS, grp * BLOCK, HEAD_DIM), lambda i: (0, i, 0)),
        compiler_params=_params(("parallel",)),
    )(x)


def _attn_mask(i):
    r = lax.broadcasted_iota(jnp.int32, (BLOCK, 2 * BLOCK), 0)
    cidx = lax.broadcasted_iota(jnp.int32, (BLOCK, 2 * BLOCK), 1)
    key = (i - 1) * BLOCK + cidx
    return (cidx > r) & (cidx <= r + BLOCK) & (key >= ROW0 - N_META)


def _attn_exp(allowed, q, kb, vb1, sink):
    rows = q.shape[0]
    grp = rows // BLOCK
    s = lax.dot_general(q, kb, _NT, preferred_element_type=F32)
    s = jnp.where(allowed[None], s.reshape(grp, BLOCK, 2 * BLOCK), NEG_INF).reshape(rows, 2 * BLOCK)
    mx = jnp.maximum(jnp.max(s, axis=-1, keepdims=True), sink)
    eb = jnp.exp(s - mx).astype(BF16)
    es = jnp.exp(sink - mx)
    ov = jnp.dot(eb, vb1, preferred_element_type=F32)
    inv = 1.0 / (ov[:, HEAD_DIM:HEAD_DIM + 1] + es)
    return eb, ov[:, :HEAD_DIM], inv, es


def _band(prev_ref, cur_ref, h):
    return jnp.concatenate([prev_ref[h], cur_ref[h]], axis=0)


def _with_ones(vb):
    return jnp.concatenate([vb, jnp.ones_like(vb)], axis=1)


def _attn_specs(grp):
    q = pl.BlockSpec((N_KV_HEADS, grp * BLOCK, HEAD_DIM), lambda i: (0, i, 0))
    cur = pl.BlockSpec((N_KV_HEADS, BLOCK, HEAD_DIM), lambda i: (0, i, 0))
    prev = pl.BlockSpec((N_KV_HEADS, BLOCK, HEAD_DIM), lambda i: (0, jnp.maximum(i - 1, 0), 0))
    sink = pl.BlockSpec((N_KV_HEADS, grp * BLOCK, 1), lambda i: (0, 0, 0))
    return q, cur, prev, sink


def _attn_fwd(q, k, v, sink_rows):
    grp = N_Q_HEADS // N_KV_HEADS
    nb = k.shape[1] // BLOCK

    def body(q_ref, kc_ref, kp_ref, vc_ref, vp_ref, s_ref, o_ref):
        allowed = _attn_mask(pl.program_id(0))
        for h in range(N_KV_HEADS):
            vb1 = _with_ones(_band(vp_ref, vc_ref, h))
            _, ov, inv, _ = _attn_exp(allowed, q_ref[h], _band(kp_ref, kc_ref, h), vb1, s_ref[h])
            o_ref[h] = (ov * inv).astype(BF16)

    qs, cur, prev, sink = _attn_specs(grp)
    return pl.pallas_call(
        body, name="attn_fwd", grid=(nb,), out_shape=jax.ShapeDtypeStruct(q.shape, BF16),
        in_specs=[qs, cur, prev, cur, prev, sink], out_specs=qs,
        compiler_params=_params(("parallel",)),
    )(q, k, k, v, v, sink_rows)


def _attn_bwd(q, k, v, sink_rows, do):
    grp = N_Q_HEADS // N_KV_HEADS
    nb = k.shape[1] // BLOCK

    def body(q_ref, kc_ref, kp_ref, vc_ref, vp_ref, s_ref, do_ref, dq_ref, dk_ref, dv_ref, ds_ref):
        i = pl.program_id(0)
        allowed = _attn_mask(i)

        @pl.when(i == 0)
        def _():
            ds_ref[...] = jnp.zeros_like(ds_ref)

        for h in range(N_KV_HEADS):
            qv, dov = q_ref[h], do_ref[h]
            kb, vb = _band(kp_ref, kc_ref, h), _band(vp_ref, vc_ref, h)
            eb, ov, inv, es = _attn_exp(allowed, qv, kb, _with_ones(vb), s_ref[h])
            dof = dov.astype(F32)
            delta = jnp.sum(dof * (ov * inv), axis=-1, keepdims=True)
            dp = lax.dot_general(dov, vb, _NT, preferred_element_type=F32)
            dsb = (eb.astype(F32) * (inv * (dp - delta))).astype(BF16)
            dq_ref[h] = jnp.dot(dsb, kb, preferred_element_type=F32).astype(BF16)
            dk_ref[h] = lax.dot_general(dsb, qv, _TN, preferred_element_type=F32)
            dv_ref[h] = lax.dot_general(eb, (dof * inv).astype(BF16), _TN, preferred_element_type=F32)
            ds_ref[h] -= (es * inv) * delta

    qs, cur, prev, sink = _attn_specs(grp)
    band = pl.BlockSpec((N_KV_HEADS, None, 2 * BLOCK, HEAD_DIM), lambda i: (0, i, 0, 0))
    band_shape = jax.ShapeDtypeStruct((N_KV_HEADS, nb, 2 * BLOCK, HEAD_DIM), F32)
    return pl.pallas_call(
        body, name="attn_bwd", grid=(nb,),
        out_shape=(jax.ShapeDtypeStruct(q.shape, BF16), band_shape, band_shape,
                   jax.ShapeDtypeStruct(sink_rows.shape, F32)),
        in_specs=[qs, cur, prev, cur, prev, sink, qs], out_specs=(qs, band, band, sink),
        compiler_params=_params(("arbitrary",)),
    )(q, k, k, v, v, sink_rows, do)


def _qkv_merge_bwd(dq, dkb, dvb, cos, sin):
    grp = N_Q_HEADS // N_KV_HEADS
    nb = dkb.shape[1]
    width = (N_Q_HEADS + 2 * N_KV_HEADS) * HEAD_DIM
    scale = HEAD_DIM ** -0.5

    k0, v0 = N_Q_HEADS * HEAD_DIM, (N_Q_HEADS + N_KV_HEADS) * HEAD_DIM

    def body(dq_ref, kc_ref, kn_ref, vc_ref, vn_ref, c_ref, s_ref, o_ref):
        last = pl.program_id(0) == nb - 1
        c, s = c_ref[...], -s_ref[...]

        def band_sum(cur_ref, nxt_ref, h):
            return cur_ref[h, BLOCK:, :] + jnp.where(last, 0.0, nxt_ref[h, :BLOCK, :])

        cq, sq = c * scale, s * scale
        for t in range(N_Q_HEADS // 2):
            pieces = []
            for e in range(2):
                h, g = divmod(2 * t + e, grp)
                pieces.append(dq_ref[h, g * BLOCK:(g + 1) * BLOCK, :].astype(F32))
            o_ref[:, t * PAIR:(t + 1) * PAIR] = _rope_pair(jnp.concatenate(pieces, axis=1), cq, sq).astype(BF16)
        for t in range(N_KV_HEADS // 2):
            dk = jnp.concatenate([band_sum(kc_ref, kn_ref, 2 * t + e) for e in range(2)], axis=1)
            o_ref[:, k0 + t * PAIR:k0 + (t + 1) * PAIR] = _rope_pair(dk, c, s).astype(BF16)
            dv = jnp.concatenate([band_sum(vc_ref, vn_ref, 2 * t + e) for e in range(2)], axis=1)
            o_ref[:, v0 + t * PAIR:v0 + (t + 1) * PAIR] = dv.astype(BF16)

    tab = pl.BlockSpec((BLOCK, PAIR), lambda i: (i, 0))
    cur = pl.BlockSpec((N_KV_HEADS, None, 2 * BLOCK, HEAD_DIM), lambda i: (0, i, 0, 0))
    nxt = pl.BlockSpec((N_KV_HEADS, None, 2 * BLOCK, HEAD_DIM), lambda i: (0, jnp.minimum(i + 1, nb - 1), 0, 0))
    return pl.pallas_call(
        body, name="qkv_merge_bwd", grid=(nb,), out_shape=jax.ShapeDtypeStruct((nb * BLOCK, width), BF16),
        in_specs=[pl.BlockSpec((N_KV_HEADS, grp * BLOCK, HEAD_DIM), lambda i: (0, i, 0)),
                  cur, nxt, cur, nxt, tab, tab],
        out_specs=pl.BlockSpec((BLOCK, width), lambda i: (i, 0)),
        compiler_params=_params(("parallel",)),
    )(dq, dkb, dkb, dvb, dvb, cos, sin)


def _tiles2d(r, c):
    tc = _pick(c, 2048, 128) if c % 128 == 0 else c
    tr = _pick(r, max(8, (1 << 20) // tc // 8 * 8), 8) if r % 8 == 0 else r
    return tr, tc


def _cast_bf16(name, w, place, wide):
    r, c = w.shape
    tr, tc = _tiles2d(r, c)
    if tr % 16:
        tr = r
    nc = c // tc

    def body(place_ref, w_ref, o_ref):
        o_ref[...] = w_ref[...].astype(BF16)

    if wide:
        out_shape = jax.ShapeDtypeStruct((r, N_CHIPS * c), BF16)
        out_spec = pl.BlockSpec((tr, tc), lambda i, j, p: (i, p[1] * nc + j))
    else:
        out_shape = jax.ShapeDtypeStruct((N_CHIPS, r, c), BF16)
        out_spec = pl.BlockSpec((None, tr, tc), lambda i, j, p: (p[1], i, j))
    return pl.pallas_call(
        body, name=name, out_shape=out_shape,
        grid_spec=pltpu.PrefetchScalarGridSpec(
            num_scalar_prefetch=1, grid=(r // tr, nc),
            in_specs=[pl.BlockSpec((tr, tc), lambda i, j, p: (i, j))], out_specs=out_spec),
        compiler_params=_params(("parallel", "parallel")),
    )(place, w)


def _pair_sum(name, g, got, place):
    n, r, c = g.shape
    half = r // 2
    tr, tc = _tiles2d(half, c)
    nh = half // tr

    def body(place_ref, g_ref, got_ref, o_ref, own_ref):
        s = (g_ref[...] + got_ref[...]).astype(BF16)
        o_ref[...] = s

        @pl.when(pl.program_id(2) == place_ref[1])
        def _():
            own_ref[...] = s

    tile = pl.BlockSpec((None, tr, tc), lambda i, j, k, p: (k, i, j))
    shape = jax.ShapeDtypeStruct((n, half, c), BF16)
    return pl.pallas_call(
        body, name=name, out_shape=(shape, shape),
        grid_spec=pltpu.PrefetchScalarGridSpec(
            num_scalar_prefetch=1, grid=(nh, c // tc, n),
            in_specs=[pl.BlockSpec((None, tr, tc), lambda i, j, k, p: (k, p[0] * nh + i, j)), tile],
            out_specs=(tile, pl.BlockSpec((None, tr, tc), lambda i, j, k, p: (p[1], i, j)))),
        compiler_params=_params(("parallel", "parallel", "arbitrary")),
    )(place, g, got)


def _chip_sum(name, parts, place):
    n, half, c = parts.shape
    tr, tc = _tiles2d(half, c)
    nh = half // tr

    def body(place_ref, p0, p1, p2, p3, o_ref):
        o_ref[...] = ((p0[...].astype(F32) + p1[...].astype(F32)) + p2[...].astype(F32)) + p3[...].astype(F32)

    def chip(k):
        return pl.BlockSpec((None, tr, tc), lambda i, j, p: (k, i, j))

    return pl.pallas_call(
        body, name=name, out_shape=jax.ShapeDtypeStruct((2 * half, c), F32),
        grid_spec=pltpu.PrefetchScalarGridSpec(
            num_scalar_prefetch=1, grid=(nh, c // tc),
            in_specs=[chip(k) for k in range(n)],
            out_specs=pl.BlockSpec((tr, tc), lambda i, j, p: (p[0] * nh + i, j))),
        compiler_params=_params(("parallel", "parallel")),
    )(place, parts, parts, parts, parts)


def _dev_sum(gathered):
    def body(g_ref, o_ref):
        acc = g_ref[0]
        for k in range(1, N_DEV):
            acc = acc + g_ref[k]
        o_ref[...] = acc

    return pl.pallas_call(body, name="dev_sum", out_shape=jax.ShapeDtypeStruct(gathered.shape[1:], F32))(gathered)


def _adamw(name, w, g, m, v):
    r, c = w.shape
    tr, tc = _tiles2d(r, c)
    if r % 8 == 0:
        tr = _pick(r, max(8, (1 << 18) // tc // 8 * 8), 8)

    def body(w_ref, g_ref, m_ref, v_ref, d_ref, mo_ref, vo_ref):
        gv = g_ref[...]
        mn = ADAM_B1 * m_ref[...] + (1.0 - ADAM_B1) * gv
        vn = ADAM_B2 * v_ref[...] + (1.0 - ADAM_B2) * jnp.square(gv)
        m_hat = mn / (1.0 - ADAM_B1 ** ADAM_STEP)
        v_hat = vn / (1.0 - ADAM_B2 ** ADAM_STEP)
        d_ref[...] = -ADAM_LR * (m_hat / (jnp.sqrt(v_hat) + ADAM_EPS) + ADAM_WD * w_ref[...])
        mo_ref[...] = mn
        vo_ref[...] = vn

    tile = pl.BlockSpec((tr, tc), lambda i, j: (i, j))
    shape = jax.ShapeDtypeStruct((r, c), F32)
    return pl.pallas_call(
        body, name=name, grid=(r // tr, c // tc), out_shape=(shape, shape, shape),
        in_specs=[tile] * 4, out_specs=(tile,) * 3, compiler_params=_params(("parallel", "parallel")),
    )(w, g, m, v)


MATRICES = ("w_in_conv", "w_out_conv", "w_up_0", "w_down_0", "w_qkv", "w_o", "w_up_1", "w_down_1")
COLUMN_SHARDED = ("w_in_conv", "w_up_0", "w_qkv", "w_up_1")
NORMS = ("norm_mix_0", "norm_mlp_0", "norm_mix_1", "norm_mlp_1", "norm_final")


def _rows(stack):
    return stack.reshape(N_CHIPS * stack.shape[1], stack.shape[2])


def _stack(full):
    return full.reshape(N_CHIPS, full.shape[0] // N_CHIPS, full.shape[1])


def _add_residual(acc, res):
    return acc + res


def _relu_sq_grad(acc, z):
    return acc * (2.0 * jnp.maximum(z.astype(F32), 0.0))


def _step(x, target, stacks, small, norms, sinks, place):
    d = D_MODEL
    dc = d // N_CHIPS
    pad = ROW0 - N_META
    m = x.shape[0] + ROW0
    grp = N_Q_HEADS // N_KV_HEADS
    cos, sin = _rope_tables(m)
    sink_rows = jnp.repeat(sinks.astype(F32), BLOCK).reshape(N_KV_HEADS, grp * BLOCK, 1)

    def gather(*names):
        return _gather_task([stacks[n] for n in names])

    def pair_sum(tag, grad, got):
        return _pair_sum("pair_sum_" + tag, grad, got, place)

    def chip_sum(tag, landed):
        return _chip_sum("chip_sum_" + tag, landed, place)

    (w_in, small_all), = _run("gather_first", [_gather_task([stacks["w_in_conv"], small])])
    small_full = jnp.transpose(small_all, (1, 0, 2)).reshape(SMALL_ROWS, d)
    conv_w8 = small_full[N_META:N_META + 8]
    h0 = jnp.concatenate([jnp.zeros((pad, d), F32), small_full[:N_META], x], axis=0)

    n0 = _rms_fwd("norm_mix_0", h0, norms["norm_mix_0"])
    bcu, ((w_out, w_up0),) = _mm_nn("conv_in", n0, w_in, BF16, tasks=[gather("w_out_conv", "w_up_0")])
    gate = _conv_fwd(bcu, conv_w8)
    h1 = _mm_nn("conv_out", gate, _rows(w_out), F32, epi=_add_residual, extras=(h0,))
    n1 = _rms_fwd("norm_mlp_0", h1, norms["norm_mlp_0"])
    z0, ((w_down0,),) = _mm_nn("mlp_up_0", n1, w_up0, BF16, tasks=[gather("w_down_0")])
    h2, ((w_qkv, w_o, w_up1),) = _mm_nn("mlp_down_0", z0, _rows(w_down0), F32, a_pro=_relu_sq, epi=_add_residual,
                                             extras=(h1,), tasks=[gather("w_qkv", "w_o", "w_up_1")])
    n2 = _rms_fwd("norm_mix_1", h2, norms["norm_mix_1"])
    qkv = _mm_nn("attn_qkv", n2, w_qkv, BF16)
    q, k, v = _qkv_split(qkv, cos, sin)
    o = _heads_merge("attn_o_merge", _attn_fwd(q, k, v, sink_rows))
    h3 = _mm_nn("attn_out", o, _rows(w_o), F32, epi=_add_residual, extras=(h2,))
    n3 = _rms_fwd("norm_mlp_1", h3, norms["norm_mlp_1"])
    z1, ((w_down1,),) = _mm_nn("mlp_up_1", n3, w_up1, BF16, tasks=[gather("w_down_1")])
    h4 = _mm_nn("mlp_down_1", z1, _rows(w_down1), F32, a_pro=_relu_sq, epi=_add_residual, extras=(h3,))

    gn = {}
    loss, dh, dh_bf, gn["norm_final"] = _loss_head(h4, norms["norm_final"], target)
    dz = _mm_nt("mlp_down_dx_1", dh_bf, _rows(w_down1), BF16, epi=_relu_sq_grad, extras=(z1,))
    g_d1 = _stack(_mm_tn("mlp_down_dw_1", z1, dh_bf, stacked=False, a_pro=_relu_sq))
    g_u1, ((got,),) = _mm_tn("mlp_up_dw_1", n3, dz, stacked=True, tasks=[_pair_exchange_task([g_d1])])
    s_d1 = pair_sum("d1", g_d1, got)
    dn, ((got,), (landed,)) = _mm_nt("mlp_up_dx_1", dz, w_up1, F32,
                                          tasks=[_pair_exchange_task([g_u1]), _chip_exchange_task([s_d1])])
    s_u1, b_d1 = pair_sum("u1", g_u1, got), chip_sum("d1", landed)
    dh, dh_bf, gn["norm_mlp_1"] = _rms_bwd("norm_mlp_bwd_1", dn, h3, norms["norm_mlp_1"], dh)
    do = _mm_nt("attn_out_dx", dh_bf, _rows(w_o), BF16)
    g_o = _stack(_mm_tn("attn_out_dw", o, dh_bf, stacked=False))
    dq, dkb, dvb, dsink = _attn_bwd(q, k, v, sink_rows, _heads_split("attn_do_split", do))
    dqkv = _qkv_merge_bwd(dq, dkb, dvb, cos, sin)
    g_qkv, ((got,),) = _mm_tn("attn_qkv_dw", n2, dqkv, stacked=True, tasks=[_pair_exchange_task([g_o])])
    s_o = pair_sum("o", g_o, got)
    dn, ((got,), (landed,)) = _mm_nt("attn_qkv_dx", dqkv, w_qkv, F32,
                                          tasks=[_pair_exchange_task([g_qkv]), _chip_exchange_task([s_u1])])
    s_qkv, b_u1 = pair_sum("qkv", g_qkv, got), chip_sum("u1", landed)
    dh, dh_bf, gn["norm_mix_1"] = _rms_bwd("norm_mix_bwd_1", dn, h2, norms["norm_mix_1"], dh)
    dz, ((landed_o, landed_qkv), (r_d1,)) = _mm_nt(
        "mlp_down_dx_0", dh_bf, _rows(w_down0), BF16, epi=_relu_sq_grad, extras=(z0,),
        tasks=[_chip_exchange_task([s_o, s_qkv]), _pair_share_task([b_d1])])
    b_o, b_qkv = chip_sum("o", landed_o), chip_sum("qkv", landed_qkv)
    g_d0, ((r_u1,),) = _mm_tn("mlp_down_dw_0", z0, dh_bf, stacked=False, a_pro=_relu_sq, tasks=[_pair_share_task([b_u1])])
    g_d0 = _stack(g_d0)
    g_u0, ((got,), (r_o, r_qkv)) = _mm_tn("mlp_up_dw_0", n1, dz, stacked=True,
                                          tasks=[_pair_exchange_task([g_d0]), _pair_share_task([b_o, b_qkv])])
    s_d0 = pair_sum("d0", g_d0, got)
    dn, ((got,), (landed,)) = _mm_nt("mlp_up_dx_0", dz, w_up0, F32,
                                          tasks=[_pair_exchange_task([g_u0]), _chip_exchange_task([s_d0])])
    s_u0, b_d0 = pair_sum("u0", g_u0, got), chip_sum("d0", landed)
    dh, dh_bf, gn["norm_mlp_0"] = _rms_bwd("norm_mlp_bwd_0", dn, h1, norms["norm_mlp_0"], dh)
    dgate = _mm_nt("conv_out_dx", dh_bf, _rows(w_out), BF16)
    dbcu, g_conv_w = _conv_bwd(bcu, conv_w8, dgate)
    g_in, ((landed,), (r_d0,)) = _mm_tn("conv_in_dw", n0, dbcu, stacked=True,
                                        tasks=[_chip_exchange_task([s_u0]), _pair_share_task([b_d0])])
    b_u0 = chip_sum("u0", landed)
    dn, ((got,), (r_u0,)) = _mm_nt("conv_in_dx", dbcu, w_in, F32,
                                        tasks=[_pair_exchange_task([g_in]), _pair_share_task([b_u0])])
    s_in = pair_sum("in", g_in, got)
    grad_x, dh_first, gn["norm_mix_0"] = _rms_bwd_tokens("norm_mix_bwd_0", dn, h0, norms["norm_mix_0"], dh)

    g_small = jnp.zeros((SMALL_ROWS, d), F32).at[:N_META].set(dh_first[pad:ROW0]).at[N_META:N_META + 8].set(g_conv_w)
    g_small = jnp.transpose(g_small.reshape(SMALL_ROWS, N_CHIPS, dc), (1, 0, 2))
    rep = jnp.zeros((8, d), F32)
    for r, n in enumerate(NORMS):
        rep = rep.at[r].set(jnp.sum(gn[n], axis=0))
    rep = rep.at[len(NORMS), :N_Q_HEADS].set(jnp.sum(dsink.reshape(N_Q_HEADS, BLOCK), axis=1))
    g_out, ((landed,), (got, rep_all)) = _mm_tn(
        "conv_out_dw", gate, dh_bf, stacked=False,
        tasks=[_chip_exchange_task([s_in]), _pair_exchange_task([g_small], small=rep)])
    g_out = _stack(g_out)
    b_in, s_small = chip_sum("in", landed), pair_sum("small", g_small, got)
    (got,), = _run("tail_pair_exchange", [_pair_exchange_task([g_out])])
    s_out = pair_sum("out", g_out, got)
    (landed_out, landed_small), = _run("tail_chip_exchange", [_chip_exchange_task([s_out, s_small])])
    b_out, b_small = chip_sum("out", landed_out), chip_sum("small", landed_small)
    (r_out, r_small, r_in), = _run("tail_pair_share", [_pair_share_task([b_out, b_small, b_in])])

    reduced = {"w_in_conv": r_in, "w_out_conv": r_out, "w_up_0": r_u0, "w_down_0": r_d0, "w_qkv": r_qkv, "w_o": r_o,
               "w_up_1": r_u1, "w_down_1": r_d1}
    return loss, grad_x, reduced, r_small, rep_all


def kernel(x, meta_tokens, norm_mix_0, w_in_conv, conv_w, w_out_conv, norm_mlp_0, w_up_0, w_down_0, norm_mix_1, w_qkv, attn_sinks, w_o, norm_mlp_1, w_up_1, w_down_1, norm_final, loss_target, m_meta_tokens, m_norm_mix_0, m_w_in_conv, m_conv_w, m_w_out_conv, m_norm_mlp_0, m_w_up_0, m_w_down_0, m_norm_mix_1, m_w_qkv, m_attn_sinks, m_w_o, m_norm_mlp_1, m_w_up_1, m_w_down_1, m_norm_final, v_meta_tokens, v_norm_mix_0, v_w_in_conv, v_conv_w, v_w_out_conv, v_norm_mlp_0, v_w_up_0, v_w_down_0, v_norm_mix_1, v_w_qkv, v_attn_sinks, v_w_o, v_norm_mlp_1, v_w_up_1, v_w_down_1, v_norm_final):
    given = dict(locals())
    names = ("meta_tokens", "norm_mix_0", "w_in_conv", "conv_w", "w_out_conv", "norm_mlp_0", "w_up_0", "w_down_0",
             "norm_mix_1", "w_qkv", "attn_sinks", "w_o", "norm_mlp_1", "w_up_1", "w_down_1", "norm_final")
    d = D_MODEL
    dc = d // N_CHIPS
    chip = 2 * lax.axis_index("x") + lax.axis_index("y")
    place = jnp.stack([lax.axis_index("c"), chip]).astype(jnp.int32)

    small = jnp.zeros((SMALL_ROWS, dc), F32).at[:N_META].set(meta_tokens).at[N_META:N_META + CONV_WIDTH].set(conv_w)
    small = lax.dynamic_update_slice(jnp.zeros((N_CHIPS, SMALL_ROWS, dc), F32), small[None], (chip, 0, 0))
    stacks = {n: _cast_bf16("cast_" + n, given[n], place, n in COLUMN_SHARDED) for n in MATRICES}

    norms = {n: given[n] for n in NORMS}
    loss_part, grad_x, g_out, r_small, rep_all = _step(x[0], loss_target[0], stacks, small, norms, attn_sinks, place)
    loss = lax.psum(loss_part[0, 0], ("x", "y", "c"))
    grad_x = grad_x[None]
    rep_sum = _dev_sum(rep_all)
    g_out["meta_tokens"] = r_small[:N_META]
    g_out["conv_w"] = r_small[N_META:N_META + CONV_WIDTH]
    for r, n in enumerate(NORMS):
        g_out[n] = rep_sum[r]
    g_out["attn_sinks"] = rep_sum[len(NORMS), :N_Q_HEADS]

    delta, new_m, new_v = {}, {}, {}
    for n in names:
        wt = given[n]
        shape2 = wt.shape if wt.ndim == 2 else (1, wt.shape[0])
        outs = _adamw("adamw_" + n, wt.reshape(shape2), g_out[n].reshape(shape2),
                      given["m_" + n].reshape(shape2), given["v_" + n].reshape(shape2))
        delta[n], new_m[n], new_v[n] = [o.reshape(wt.shape) for o in outs]
    return (loss, grad_x, *[g_out[n] for n in names], *[delta[n] for n in names],
            *[new_m[n] for n in names], *[new_v[n] for n in names])
```

```python
import functools

import jax
import jax.numpy as jnp
from jax import lax
from jax.experimental import pallas as pl
from jax.experimental.pallas import tpu as pltpu

F32 = jnp.float32
BF16 = jnp.bfloat16

D_MODEL = 2048
SEQ = 8192
N_META = 16
CONV_WIDTH = 3
HEAD_DIM = 64
N_Q_HEADS = 32
N_KV_HEADS = 4
BLOCK = 128
ROPE_THETA = 10000.0
D_FF = 4 * D_MODEL
RMS_EPS = 1e-5
NEG_INF = -1e30

ADAM_LR = 0.001
ADAM_B1 = 0.9
ADAM_B2 = 0.999
ADAM_EPS = 1e-08
ADAM_WD = 0.01
ADAM_STEP = 10

N_CHIPS = 4
N_DEV = 8
MESH = pl.DeviceIdType.MESH
VMEM_LIMIT = 56 * 1024 * 1024
SMALL_ROWS = 32
ROW0 = BLOCK


def _pick(n, target, mult):
    best = None
    for t in range(mult, min(n, target) + 1, mult):
        if n % t == 0:
            best = t
    assert best is not None, (n, target, mult)
    return best


def _params(sem=None):
    return pltpu.CompilerParams(dimension_semantics=sem, vmem_limit_bytes=VMEM_LIMIT)


HBM_SPEC = pl.BlockSpec(memory_space=pltpu.HBM)


class _Task:
    def __init__(self, inputs, outputs, aliases, sem_shapes, bind):
        self.inputs, self.outputs, self.aliases = list(inputs), list(outputs), dict(aliases)
        self.sem_shapes, self.bind = list(sem_shapes), bind


def _like(arrays):
    return [jax.ShapeDtypeStruct(a.shape, a.dtype) for a in arrays]


def _bind_tasks(tasks, in_refs, out_refs, sem_refs):
    bound, i, o, s = [], 0, 0, 0
    for t in tasks:
        ni, no, ns = len(t.inputs), len(t.outputs), len(t.sem_shapes)
        bound.append(t.bind(in_refs[i:i + ni], out_refs[o:o + no], sem_refs[s:s + ns]))
        i, o, s = i + ni, o + no, s + ns
    return bound


def _run_phase(bound, phase):
    for b in bound:
        if b[phase] is not None:
            b[phase]()


def _task_plumbing(tasks, in_offset, out_offset):
    ins = [a for t in tasks for a in t.inputs]
    outs = [o for t in tasks for o in t.outputs]
    sems = [s for t in tasks for s in t.sem_shapes]
    aliases, i, o = {}, in_offset, out_offset
    for t in tasks:
        for src, dst in t.aliases.items():
            aliases[i + src] = o + dst
        i, o = i + len(t.inputs), o + len(t.outputs)
    return ins, outs, sems, aliases


def _split_outputs(tasks, flat):
    res, o = [], 0
    for t in tasks:
        res.append(list(flat[o:o + len(t.outputs)]))
        o += len(t.outputs)
    return res


def _run(name, tasks):
    ins, outs, sems, aliases = _task_plumbing(tasks, 0, 0)

    def body(*refs):
        bound = _bind_tasks(tasks, refs[:len(ins)], refs[len(ins):len(ins) + len(outs)], refs[len(ins) + len(outs):])
        for phase in range(3):
            _run_phase(bound, phase)

    flat = pl.pallas_call(
        body, name=name, out_shape=outs, in_specs=[HBM_SPEC] * len(ins), out_specs=[HBM_SPEC] * len(outs),
        input_output_aliases=aliases, scratch_shapes=sems,
    )(*ins)
    return _split_outputs(tasks, flat)


def _place():
    x, y, c = lax.axis_index("x"), lax.axis_index("y"), lax.axis_index("c")
    chips = [(1 - x, y), (x, 1 - y), (1 - x, 1 - y)]
    return x, y, c, 2 * x + y, chips


def _gather_task(stacks):
    n = len(stacks)
    halves = [s.shape[-2] // 2 for s in stacks]

    def bind(_, dst, sems):
        send_a, recv_a, send_b, recv_b = sems
        x, y, c, me, chips = _place()
        sibling = (x, y, 1 - c)

        def half(w, chip, hc):
            rows = pl.ds(hc * halves[w], halves[w])
            if len(stacks[w].shape) == 3:
                return dst[w].at[chip, rows, :]
            cols = stacks[w].shape[1] // N_CHIPS
            return dst[w].at[rows, pl.ds(chip * cols, cols)]

        def over_ici(j, w, block):
            return pltpu.make_async_remote_copy(
                src_ref=half(w, block, c), dst_ref=half(w, block, c), send_sem=send_a.at[j * n + w],
                recv_sem=recv_a.at[j * n + w], device_id=(*chips[j], c), device_id_type=MESH)

        def over_d2d(j, w, hc):
            got = half(w, 2 * chips[j][0] + chips[j][1], hc)
            return pltpu.make_async_remote_copy(
                src_ref=got, dst_ref=got, send_sem=send_b.at[j * n + w], recv_sem=recv_b.at[j * n + w],
                device_id=sibling, device_id_type=MESH)

        pairs = [(j, w) for j in range(3) for w in range(n)]

        def start():
            for j, w in pairs:
                over_ici(j, w, me).start()

        def mid():
            for j, w in pairs:
                over_ici(j, w, 2 * chips[j][0] + chips[j][1]).wait_recv()
                over_d2d(j, w, c).start()

        def finish():
            for j, w in pairs:
                over_d2d(j, w, 1 - c).wait_recv()
            for j, w in pairs:
                over_ici(j, w, me).wait_send()
                over_d2d(j, w, c).wait_send()

        return start, mid, finish

    return _Task(stacks, _like(stacks), {w: w for w in range(n)}, [pltpu.SemaphoreType.DMA((3 * n,))] * 4, bind)


def _pair_exchange_task(grads, small=None):
    n = len(grads)
    halves = [g.shape[1] // 2 for g in grads]

    def bind(src, dst, sems):
        send, recv = sems[0], sems[1]
        x, y, c, me, _ = _place()
        sibling = (x, y, 1 - c)
        dev = 2 * me + c

        def to_sibling(w):
            return pltpu.make_async_remote_copy(
                src_ref=src[w].at[:, pl.ds((1 - c) * halves[w], halves[w]), :], dst_ref=dst[w],
                send_sem=send.at[w], recv_sem=recv.at[w], device_id=sibling, device_id_type=MESH)

        def to_peer(t, block):
            tx, ty, tc = (t >> 2) & 1, (t >> 1) & 1, t & 1
            return pltpu.make_async_remote_copy(
                src_ref=src[n], dst_ref=dst[n].at[block], send_sem=sems[2].at[t], recv_sem=sems[3].at[t],
                device_id=(x ^ tx, y ^ ty, c ^ tc), device_id_type=MESH)

        def mine():
            return pltpu.make_async_copy(src[n], dst[n].at[dev], sems[4])

        def start():
            for w in range(n):
                to_sibling(w).start()
            if small is not None:
                mine().start()
                for t in range(1, N_DEV):
                    to_peer(t, dev).start()

        def finish():
            for w in range(n):
                to_sibling(w).wait_recv()
            if small is not None:
                for t in range(1, N_DEV):
                    to_peer(t, dev ^ t).wait_recv()
            for w in range(n):
                to_sibling(w).wait_send()
            if small is not None:
                for t in range(1, N_DEV):
                    to_peer(t, dev).wait_send()
                mine().wait()

        return start, None, finish

    outputs = [jax.ShapeDtypeStruct((N_CHIPS, h, g.shape[2]), g.dtype) for g, h in zip(grads, halves)]
    sem_shapes = [pltpu.SemaphoreType.DMA((n,)), pltpu.SemaphoreType.DMA((n,))]
    inputs = list(grads)
    if small is not None:
        inputs.append(small)
        outputs.append(jax.ShapeDtypeStruct((N_DEV,) + small.shape, small.dtype))
        sem_shapes += [pltpu.SemaphoreType.DMA((N_DEV,)), pltpu.SemaphoreType.DMA((N_DEV,)), pltpu.SemaphoreType.DMA(())]
    return _Task(inputs, outputs, {}, sem_shapes, bind)


def _chip_exchange_task(summed):
    n = len(summed)

    def bind(refs, dst, sems):
        src = refs[:n]
        send, recv = sems
        x, y, c, me, chips = _place()

        def copy(j, w, block_from, block_to):
            return pltpu.make_async_remote_copy(
                src_ref=src[w].at[block_from], dst_ref=dst[w].at[block_to], send_sem=send.at[j * n + w],
                recv_sem=recv.at[j * n + w], device_id=(*chips[j], c), device_id_type=MESH)

        pairs = [(j, w) for j in range(3) for w in range(n)]

        def start():
            for j, w in pairs:
                copy(j, w, 2 * chips[j][0] + chips[j][1], me).start()

        def finish():
            for j, w in pairs:
                copy(j, w, me, 2 * chips[j][0] + chips[j][1]).wait_recv()
            for j, w in pairs:
                copy(j, w, 2 * chips[j][0] + chips[j][1], me).wait_send()

        return start, None, finish

    partials, landing = [s[0] for s in summed], [s[1] for s in summed]
    return _Task(partials + landing, _like(landing), {n + w: w for w in range(n)},
                 [pltpu.SemaphoreType.DMA((3 * n,))] * 2, bind)


def _pair_share_task(blocks):
    n = len(blocks)

    def bind(_, dst, sems):
        send, recv = sems
        x, y, c, _, _ = _place()

        def copy(w, hc):
            h = blocks[w].shape[0] // 2
            rows = dst[w].at[pl.ds(hc * h, h), :]
            return pltpu.make_async_remote_copy(src_ref=rows, dst_ref=rows, send_sem=send.at[w], recv_sem=recv.at[w],
                                                device_id=(x, y, 1 - c), device_id_type=MESH)

        def start():
            for w in range(n):
                copy(w, c).start()

        def finish():
            for w in range(n):
                copy(w, 1 - c).wait_recv()
            for w in range(n):
                copy(w, c).wait_send()

        return start, None, finish

    return _Task(blocks, _like(blocks), {w: w for w in range(n)}, [pltpu.SemaphoreType.DMA((n,))] * 2, bind)


def _mm(name, a, b, *, dims, grid, a_spec, b_spec, out_shape, out_spec,
        extras=(), extra_specs=(), a_pro=None, epi=None, tasks=()):
    n_ex = len(extras)
    t_ins, t_outs, t_sems, aliases = _task_plumbing(tasks, 2 + n_ex, 1)
    n_ti, n_to = len(t_ins), len(t_outs)
    total = grid[0] * grid[1]
    mid_step = max(0, total - 1 - max(1, total // 8))

    def body(*refs):
        a_ref, b_ref = refs[0], refs[1]
        ex = refs[2:2 + n_ex]
        o_ref = refs[2 + n_ex + n_ti]
        if tasks:
            bound = _bind_tasks(tasks, refs[2 + n_ex:2 + n_ex + n_ti],
                                refs[3 + n_ex + n_ti:3 + n_ex + n_ti + n_to], refs[3 + n_ex + n_ti + n_to:])
            step = pl.program_id(0) * grid[1] + pl.program_id(1)

            @pl.when(step == 0)
            def _():
                _run_phase(bound, 0)

        av = a_ref[...]
        if a_pro is not None:
            av = a_pro(av)
        acc = lax.dot_general(av, b_ref[...], dims, preferred_element_type=F32)
        if epi is not None:
            acc = epi(acc, *[e[...] for e in ex])
        o_ref[...] = acc.astype(o_ref.dtype)

        if tasks:
            @pl.when(step == mid_step)
            def _():
                _run_phase(bound, 1)

            @pl.when(step == total - 1)
            def _():
                _run_phase(bound, 2)

    sem = ("arbitrary", "arbitrary") if tasks else ("parallel", "parallel")
    res = pl.pallas_call(
        body, name=name, grid=grid, out_shape=[out_shape, *t_outs],
        in_specs=[a_spec, b_spec, *extra_specs, *[HBM_SPEC] * n_ti], out_specs=[out_spec, *[HBM_SPEC] * n_to],
        input_output_aliases=aliases, scratch_shapes=t_sems, compiler_params=_params(sem),
    )(a, b, *extras, *t_ins)
    return (res[0], _split_outputs(tasks, res[1:])) if tasks else res[0]


_NN = (((1,), (0,)), ((), ()))
_NT = (((1,), (1,)), ((), ()))
_TN = (((0,), (0,)), ((), ()))


MM_TILE_BUDGET = 46 * 1024 * 1024


def _mm_tiles(m, n, contraction, out_bytes):
    for rows, cols in ((1664, 1024), (832, 1024), (416, 1024), (416, 512)):
        tm, tn = _pick(m, rows, 16), _pick(n, cols, 128)
        if 2 * 2 * contraction * (tm + tn) + tm * tn * (4 + 2 * out_bytes) <= MM_TILE_BUDGET:
            break
    return tm, tn


def _out_bytes(out_dtype, extras):
    return jnp.dtype(out_dtype).itemsize + sum(e.dtype.itemsize for e in extras)


def _mm_nn(name, a, w, out_dtype, a_pro=None, epi=None, extras=(), tasks=()):
    m, k = a.shape
    _, n = w.shape
    tm, tn = _mm_tiles(m, n, k, _out_bytes(out_dtype, extras))
    tile = pl.BlockSpec((tm, tn), lambda j, i: (i, j))
    return _mm(name, a, w, dims=_NN, grid=(n // tn, m // tm),
               a_spec=pl.BlockSpec((tm, k), lambda j, i: (i, 0)), b_spec=pl.BlockSpec((k, tn), lambda j, i: (0, j)),
               out_shape=jax.ShapeDtypeStruct((m, n), out_dtype), out_spec=tile,
               extras=extras, extra_specs=[tile] * len(extras), a_pro=a_pro, epi=epi, tasks=tasks)


def _mm_nt(name, a, w, out_dtype, epi=None, extras=(), tasks=()):
    m, c = a.shape
    r, _ = w.shape
    tm, tn = _mm_tiles(m, r, c, _out_bytes(out_dtype, extras))
    tile = pl.BlockSpec((tm, tn), lambda j, i: (i, j))
    return _mm(name, a, w, dims=_NT, grid=(r // tn, m // tm),
               a_spec=pl.BlockSpec((tm, c), lambda j, i: (i, 0)), b_spec=pl.BlockSpec((tn, c), lambda j, i: (j, 0)),
               out_shape=jax.ShapeDtypeStruct((m, r), out_dtype), out_spec=tile,
               extras=extras, extra_specs=[tile] * len(extras), epi=epi, tasks=tasks)


def _mm_tn(name, a, b, stacked, a_pro=None, tasks=()):
    t, ka = a.shape
    _, nb = b.shape
    ns = nb // N_CHIPS if stacked else nb
    ta, tb = _pick(ka, 512, 128), _pick(ns, 640, 128)
    if stacked:
        per = ns // tb
        out_shape = jax.ShapeDtypeStruct((N_CHIPS, ka, ns), F32)
        out_spec = pl.BlockSpec((None, ta, tb), lambda i, j: (j // per, i, j % per))
    else:
        out_shape = jax.ShapeDtypeStruct((ka, nb), F32)
        out_spec = pl.BlockSpec((ta, tb), lambda i, j: (i, j))
    return _mm(name, a, b, dims=_TN, grid=(ka // ta, nb // tb),
               a_spec=pl.BlockSpec((t, ta), lambda i, j: (0, i)), b_spec=pl.BlockSpec((t, tb), lambda i, j: (0, j)),
               out_shape=out_shape, out_spec=out_spec, a_pro=a_pro, tasks=tasks)


def _relu_sq(z):
    a = jnp.maximum(z, 0)
    return a * a


def _rms_fwd(name, h, g):
    m, d = h.shape
    tr = _pick(m, 256, 16)

    def body(h_ref, g_ref, o_ref):
        x = h_ref[...]
        rstd = lax.rsqrt(jnp.mean(x * x, axis=-1, keepdims=True) + RMS_EPS)
        o_ref[...] = ((x * rstd) * g_ref[...]).astype(BF16)

    row = pl.BlockSpec((tr, d), lambda i: (i, 0))
    return pl.pallas_call(
        body, name=name, grid=(m // tr,), out_shape=jax.ShapeDtypeStruct((m, d), BF16),
        in_specs=[row, pl.BlockSpec((1, d), lambda i: (0, 0))], out_specs=row,
        compiler_params=_params(("parallel",)),
    )(h, g.reshape(1, d))


def _rms_bwd_math(x, g, dn):
    rstd = lax.rsqrt(jnp.mean(x * x, axis=-1, keepdims=True) + RMS_EPS)
    xhat = x * rstd
    dxhat = dn * g
    dx = rstd * (dxhat - xhat * jnp.mean(dxhat * xhat, axis=-1, keepdims=True))
    return dx, dn * xhat


def _fold8(v):
    r, c = v.shape
    return jnp.sum(v.reshape(r // 8, 8, c), axis=0)


def _rms_bwd(name, dn, h, g, dh_in):
    m, d = h.shape
    tr = _pick(m, 256, 16)
    nt = m // tr

    def body(dn_ref, h_ref, g_ref, dh_ref, o_ref, ob_ref, dg_ref):
        dx, dgp = _rms_bwd_math(h_ref[...], g_ref[...], dn_ref[...].astype(F32))
        dh = dh_ref[...] + dx
        o_ref[...] = dh
        ob_ref[...] = dh.astype(BF16)

        @pl.when(pl.program_id(0) == 0)
        def _():
            dg_ref[...] = jnp.zeros_like(dg_ref)

        dg_ref[...] += _fold8(dgp)

    row = pl.BlockSpec((tr, d), lambda i: (i, 0))
    return pl.pallas_call(
        body, name=name, grid=(nt,),
        out_shape=(jax.ShapeDtypeStruct((m, d), F32), jax.ShapeDtypeStruct((m, d), BF16),
                   jax.ShapeDtypeStruct((8, d), F32)),
        in_specs=[row, row, pl.BlockSpec((1, d), lambda i: (0, 0)), row],
        out_specs=(row, row, pl.BlockSpec((8, d), lambda i: (0, 0))),
        compiler_params=_params(("arbitrary",)),
    )(dn, h, g.reshape(1, d), dh_in)


def _rms_bwd_tokens(name, dn, h, g, dh_in):
    m, d = h.shape
    nb = m // BLOCK

    def body(dn_ref, h_ref, g_ref, dh_ref, gx_ref, first_ref, dg_ref):
        i = pl.program_id(0)
        dx, dgp = _rms_bwd_math(h_ref[...], g_ref[...], dn_ref[...].astype(F32))
        dh = dh_ref[...] + dx
        gx_ref[...] = dh

        @pl.when(i == 0)
        def _():
            first_ref[...] = dh
            dg_ref[...] = jnp.zeros_like(dg_ref)

        dg_ref[...] += _fold8(dgp)

    row = pl.BlockSpec((BLOCK, d), lambda i: (i, 0))
    return pl.pallas_call(
        body, name=name, grid=(nb,),
        out_shape=(jax.ShapeDtypeStruct((m - ROW0, d), F32), jax.ShapeDtypeStruct((ROW0, d), F32),
                   jax.ShapeDtypeStruct((8, d), F32)),
        in_specs=[row, row, pl.BlockSpec((1, d), lambda i: (0, 0)), row],
        out_specs=(pl.BlockSpec((BLOCK, d), lambda i: (jnp.maximum(i - 1, 0), 0)),
                   pl.BlockSpec((ROW0, d), lambda i: (0, 0)), pl.BlockSpec((8, d), lambda i: (0, 0))),
        compiler_params=_params(("arbitrary",)),
    )(dn, h, g.reshape(1, d), dh_in)


def _loss_head(h, g, target):
    m, d = h.shape
    tr = BLOCK

    def body(h_ref, g_ref, t_ref, loss_ref, o_ref, ob_ref, dg_ref):
        i = pl.program_id(0)
        x = h_ref[...]
        gv = g_ref[...]
        rstd = lax.rsqrt(jnp.mean(x * x, axis=-1, keepdims=True) + RMS_EPS)
        err = jnp.where(i > 0, (x * rstd) * gv - t_ref[...], 0.0)
        dx, dgp = _rms_bwd_math(x, gv, err * (1.0 / d))
        o_ref[...] = dx
        ob_ref[...] = dx.astype(BF16)

        @pl.when(i == 0)
        def _():
            dg_ref[...] = jnp.zeros_like(dg_ref)
            loss_ref[...] = jnp.zeros_like(loss_ref)

        dg_ref[...] += _fold8(dgp)
        sq = jnp.mean(err * err, axis=-1, keepdims=True)
        loss_ref[...] += 0.5 * jnp.sum(sq, axis=0, keepdims=True)

    row = pl.BlockSpec((tr, d), lambda i: (i, 0))
    return pl.pallas_call(
        body, name="loss_head", grid=(m // tr,),
        out_shape=(jax.ShapeDtypeStruct((8, 128), F32), jax.ShapeDtypeStruct((m, d), F32),
                   jax.ShapeDtypeStruct((m, d), BF16), jax.ShapeDtypeStruct((8, d), F32)),
        in_specs=[row, pl.BlockSpec((1, d), lambda i: (0, 0)),
                  pl.BlockSpec((tr, d), lambda i: (jnp.maximum(i - 1, 0), 0))],
        out_specs=(pl.BlockSpec((8, 128), lambda i: (0, 0)), row, row,
                   pl.BlockSpec((8, d), lambda i: (0, 0))),
        compiler_params=_params(("arbitrary",)),
    )(h, g.reshape(1, d), target)


HALO = 16


def _shift_down(cat, k):
    return pltpu.roll(cat, k, axis=0)[HALO:]


def _shift_up(cat, k):
    n = cat.shape[0]
    return pltpu.roll(cat, n - k, axis=0)[:n - HALO]


def _conv_fwd(bcu, cw):
    m, d3 = bcu.shape
    d = d3 // 3
    tr, tc = _pick(m, 416, 16), _pick(d, 512, 128)
    hb = tr // HALO

    def body(x_ref, xb_ref, w_ref, o_ref):
        i = pl.program_id(0)
        for j in range(d // tc):
            col = slice(j * tc, (j + 1) * tc)
            cb, cc, cu = (slice(q * d + j * tc, q * d + (j + 1) * tc) for q in range(3))
            v = x_ref[:, cc].astype(F32) * x_ref[:, cu].astype(F32)
            vh = jnp.where(i > 0, xb_ref[:, cc].astype(F32) * xb_ref[:, cu].astype(F32), 0.0)
            cat = jnp.concatenate([vh, v], axis=0)
            w = w_ref[:, col]
            conv = w[2:3] * v + w[1:2] * _shift_down(cat, 1) + w[0:1] * _shift_down(cat, 2)
            o_ref[:, col] = (x_ref[:, cb].astype(F32) * conv).astype(BF16)

    return pl.pallas_call(
        body, name="conv_fwd", grid=(m // tr,), out_shape=jax.ShapeDtypeStruct((m, d), BF16),
        in_specs=[pl.BlockSpec((tr, d3), lambda i: (i, 0)),
                  pl.BlockSpec((HALO, d3), lambda i: (jnp.maximum(i * hb - 1, 0), 0)),
                  pl.BlockSpec((8, d), lambda i: (0, 0))],
        out_specs=pl.BlockSpec((tr, d), lambda i: (i, 0)),
        compiler_params=_params(("parallel",)),
    )(bcu, bcu, cw)


def _conv_bwd(bcu, cw, dg):
    m, d3 = bcu.shape
    d = d3 // 3
    tr, tc = _pick(m, 208, 16), _pick(d, 512, 128)
    hb, nt = tr // HALO, m // tr

    def body(x_ref, xb_ref, xa_ref, dg_ref, dga_ref, w_ref, o_ref, dw_ref):
        i = pl.program_id(0)

        @pl.when(i == 0)
        def _():
            dw_ref[...] = jnp.zeros_like(dw_ref)

        for j in range(d // tc):
            col = slice(j * tc, (j + 1) * tc)
            cb, cc, cu = (slice(q * d + j * tc, q * d + (j + 1) * tc) for q in range(3))
            w = w_ref[:, col]
            b, c, u = x_ref[:, cb].astype(F32), x_ref[:, cc].astype(F32), x_ref[:, cu].astype(F32)
            dgv = dg_ref[:, col].astype(F32)
            v = c * u
            vh = jnp.where(i > 0, xb_ref[:, cc].astype(F32) * xb_ref[:, cu].astype(F32), 0.0)
            cat = jnp.concatenate([vh, v], axis=0)
            v1, v2 = _shift_down(cat, 1), _shift_down(cat, 2)
            dconv = dgv * b
            o_ref[:, cb] = (dgv * (w[2:3] * v + w[1:2] * v1 + w[0:1] * v2)).astype(BF16)
            taps = [jnp.sum(dconv * t, axis=0, keepdims=True) for t in (v2, v1, v)]
            dw_ref[:, col] += jnp.concatenate(taps + [jnp.zeros((5, tc), F32)], axis=0)
            nxt = jnp.where(i < nt - 1, dga_ref[:, col].astype(F32) * xa_ref[:, cb].astype(F32), 0.0)
            cat2 = jnp.concatenate([dconv, nxt], axis=0)
            dv = w[2:3] * dconv + w[1:2] * _shift_up(cat2, 1) + w[0:1] * _shift_up(cat2, 2)
            o_ref[:, cc] = (dv * u).astype(BF16)
            o_ref[:, cu] = (dv * c).astype(BF16)

    def rows(width):
        return pl.BlockSpec((tr, width), lambda i: (i, 0))

    def before(width):
        return pl.BlockSpec((HALO, width), lambda i: (jnp.maximum(i * hb - 1, 0), 0))

    def after(width):
        return pl.BlockSpec((HALO, width), lambda i: (jnp.minimum((i + 1) * hb, m // HALO - 1), 0))

    return pl.pallas_call(
        body, name="conv_bwd", grid=(nt,),
        out_shape=(jax.ShapeDtypeStruct((m, d3), BF16), jax.ShapeDtypeStruct((8, d), F32)),
        in_specs=[rows(d3), before(d3), after(d3), rows(d), after(d), pl.BlockSpec((8, d), lambda i: (0, 0))],
        out_specs=(rows(d3), pl.BlockSpec((8, d), lambda i: (0, 0))),
        compiler_params=_params(("arbitrary",)),
    )(bcu, bcu, bcu, dg, dg, cw)


PAIR = 2 * HEAD_DIM


def _rope_tables(m):
    pad = ROW0 - N_META
    pos = jnp.arange(m, dtype=F32) - pad
    inv = ROPE_THETA ** (-jnp.arange(0, HEAD_DIM, 2, dtype=F32) / HEAD_DIM)
    ang = pos[:, None] * inv[None, :]
    cos, sin = jnp.cos(ang), jnp.sin(ang)
    return jnp.tile(jnp.concatenate([cos, cos], axis=1), (1, 2)), jnp.tile(jnp.concatenate([-sin, sin], axis=1), (1, 2))


def _rope_pair(x, c, s):
    half = HEAD_DIM // 2
    lane = lax.broadcasted_iota(jnp.int32, x.shape, 1)
    swapped = jnp.where(lane % HEAD_DIM < half, pltpu.roll(x, PAIR - half, axis=1), pltpu.roll(x, half, axis=1))
    return x * c + swapped * s


def _qkv_split(qkv, cos, sin):
    m = qkv.shape[0]
    nb, grp = m // BLOCK, N_Q_HEADS // N_KV_HEADS
    scale = HEAD_DIM ** -0.5
    k0, v0 = N_Q_HEADS * HEAD_DIM, (N_Q_HEADS + N_KV_HEADS) * HEAD_DIM

    def body(x_ref, c_ref, s_ref, q_ref, k_ref, v_ref):
        c, s = c_ref[...], s_ref[...]
        cq, sq = c * scale, s * scale
        for t in range(N_Q_HEADS // 2):
            r = _rope_pair(x_ref[:, t * PAIR:(t + 1) * PAIR].astype(F32), cq, sq).astype(BF16)
            for e in range(2):
                h, g = divmod(2 * t + e, grp)
                q_ref[h, g * BLOCK:(g + 1) * BLOCK, :] = r[:, e * HEAD_DIM:(e + 1) * HEAD_DIM]
        for t in range(N_KV_HEADS // 2):
            r = _rope_pair(x_ref[:, k0 + t * PAIR:k0 + (t + 1) * PAIR].astype(F32), c, s).astype(BF16)
            vv = x_ref[:, v0 + t * PAIR:v0 + (t + 1) * PAIR]
            for e in range(2):
                k_ref[2 * t + e] = r[:, e * HEAD_DIM:(e + 1) * HEAD_DIM]
                v_ref[2 * t + e] = vv[:, e * HEAD_DIM:(e + 1) * HEAD_DIM]

    tab = pl.BlockSpec((BLOCK, PAIR), lambda i: (i, 0))
    kv = pl.BlockSpec((N_KV_HEADS, BLOCK, HEAD_DIM), lambda i: (0, i, 0))
    return pl.pallas_call(
        body, name="qkv_split", grid=(nb,),
        out_shape=(jax.ShapeDtypeStruct((N_KV_HEADS, nb * grp * BLOCK, HEAD_DIM), BF16),
                   jax.ShapeDtypeStruct((N_KV_HEADS, m, HEAD_DIM), BF16),
                   jax.ShapeDtypeStruct((N_KV_HEADS, m, HEAD_DIM), BF16)),
        in_specs=[pl.BlockSpec((BLOCK, qkv.shape[1]), lambda i: (i, 0)), tab, tab],
        out_specs=(pl.BlockSpec((N_KV_HEADS, grp * BLOCK, HEAD_DIM), lambda i: (0, i, 0)), kv, kv),
        compiler_params=_params(("parallel",)),
    )(qkv, cos, sin)


def _heads_merge(name, o):
    grp = N_Q_HEADS // N_KV_HEADS
    nb = o.shape[1] // (grp * BLOCK)

    def body(o_ref, x_ref):
        for t in range(N_Q_HEADS // 2):
            pieces = []
            for e in range(2):
                h, g = divmod(2 * t + e, grp)
                pieces.append(o_ref[h, g * BLOCK:(g + 1) * BLOCK, :])
            x_ref[:, t * PAIR:(t + 1) * PAIR] = jnp.concatenate(pieces, axis=1)

    return pl.pallas_call(
        body, name=name, grid=(nb,),
        out_shape=jax.ShapeDtypeStruct((nb * BLOCK, N_Q_HEADS * HEAD_DIM), o.dtype),
        in_specs=[pl.BlockSpec((N_KV_HEADS, grp * BLOCK, HEAD_DIM), lambda i: (0, i, 0))],
        out_specs=pl.BlockSpec((BLOCK, N_Q_HEADS * HEAD_DIM), lambda i: (i, 0)),
        compiler_params=_params(("parallel",)),
    )(o)


def _heads_split(name, x):
    grp = N_Q_HEADS // N_KV_HEADS
    nb = x.shape[0] // BLOCK

    def body(x_ref, o_ref):
        for h in range(N_KV_HEADS):
            for g in range(grp):
                col = (h * grp + g) * HEAD_DIM
                o_ref[h, g * BLOCK:(g + 1) * BLOCK, :] = x_ref[:, col:col + HEAD_DIM]

    return pl.pallas_call(
        body, name=name, grid=(nb,),
        out_shape=jax.ShapeDtypeStruct((N_KV_HEADS, nb * grp * BLOCK, HEAD_DIM), x.dtype),
        in_specs=[pl.BlockSpec((BLOCK, N_Q_HEADS * HEAD_DIM), lambda i: (i, 0))],
        out_specs=pl.BlockSpec((N_KV_HEADS, grp * BLOCK, HEAD_DIM), lambda i: (0, i, 0)),
        compiler_params=_params(("parallel",)),
    )(x)


def _attn_mask(i):
    r = lax.broadcasted_iota(jnp.int32, (BLOCK, 2 * BLOCK), 0)
    cidx = lax.broadcasted_iota(jnp.int32, (BLOCK, 2 * BLOCK), 1)
    key = (i - 1) * BLOCK + cidx
    return (cidx > r) & (cidx <= r + BLOCK) & (key >= ROW0 - N_META)


def _band(prev_ref, cur_ref, h):
    return jnp.concatenate([prev_ref[h], cur_ref[h]], axis=0)


def _first_column(rows):
    return lax.broadcasted_iota(jnp.int32, (rows, 2 * BLOCK), 1) == 0


def _attn_specs(grp):
    q = pl.BlockSpec((N_KV_HEADS, grp * BLOCK, HEAD_DIM), lambda i: (0, i, 0))
    cur = pl.BlockSpec((N_KV_HEADS, BLOCK, HEAD_DIM), lambda i: (0, i, 0))
    prev = pl.BlockSpec((N_KV_HEADS, BLOCK, HEAD_DIM), lambda i: (0, jnp.maximum(i - 1, 0), 0))
    probs = pl.BlockSpec((N_KV_HEADS, grp * BLOCK, 2 * BLOCK), lambda i: (0, i, 0))
    return q, cur, prev, probs


def _attn_fwd(q, k, v, sink_rows):
    grp = N_Q_HEADS // N_KV_HEADS
    nb = k.shape[1] // BLOCK
    rows = grp * BLOCK

    def body(q_ref, kc_ref, kp_ref, vc_ref, vp_ref, s_ref, o_ref, p_ref):
        allowed = _attn_mask(pl.program_id(0))
        col0 = _first_column(rows)
        for h in range(N_KV_HEADS):
            kb, vb = _band(kp_ref, kc_ref, h), _band(vp_ref, vc_ref, h)
            sink = s_ref[h]
            s = lax.dot_general(q_ref[h], kb, _NT, preferred_element_type=F32)
            s = jnp.where(allowed[None], s.reshape(grp, BLOCK, 2 * BLOCK), NEG_INF).reshape(rows, 2 * BLOCK)
            mx = jnp.maximum(jnp.max(s, axis=-1, keepdims=True), sink)
            eb = jnp.exp(s - mx).astype(BF16)
            es = jnp.exp(sink - mx)
            vb1 = jnp.concatenate([vb, jnp.ones_like(vb)], axis=1)
            ov = jnp.dot(eb, vb1, preferred_element_type=F32)
            inv = 1.0 / (ov[:, HEAD_DIM:HEAD_DIM + 1] + es)
            o_ref[h] = (ov[:, :HEAD_DIM] * inv).astype(BF16)
            p_ref[h] = jnp.where(col0, es * inv, eb.astype(F32) * inv).astype(BF16)

    qs, cur, prev, probs = _attn_specs(grp)
    return pl.pallas_call(
        body, name="attn_fwd", grid=(nb,),
        out_shape=(jax.ShapeDtypeStruct(q.shape, BF16), jax.ShapeDtypeStruct((N_KV_HEADS, nb * rows, 2 * BLOCK), BF16)),
        in_specs=[qs, cur, prev, cur, prev, pl.BlockSpec((N_KV_HEADS, rows, 1), lambda i: (0, 0, 0))],
        out_specs=(qs, probs), compiler_params=_params(("parallel",)),
    )(q, k, k, v, v, sink_rows)


def _attn_bwd(q, k, v, p, o, do):
    grp = N_Q_HEADS // N_KV_HEADS
    nb = k.shape[1] // BLOCK
    rows = grp * BLOCK

    def body(q_ref, kc_ref, kp_ref, vc_ref, vp_ref, p_ref, o_ref, do_ref, dq_ref, dk_ref, dv_ref, ds_ref):
        col0 = _first_column(rows)

        @pl.when(pl.program_id(0) == 0)
        def _():
            ds_ref[...] = jnp.zeros_like(ds_ref)

        for h in range(N_KV_HEADS):
            qv, dov, pv = q_ref[h], do_ref[h], p_ref[h]
            kb, vb = _band(kp_ref, kc_ref, h), _band(vp_ref, vc_ref, h)
            p_sink = pv[:, 0:1].astype(F32)
            pb = jnp.where(col0, jnp.zeros_like(pv), pv)
            delta = jnp.sum(dov.astype(F32) * o_ref[h].astype(F32), axis=-1, keepdims=True)
            dp = lax.dot_general(dov, vb, _NT, preferred_element_type=F32)
            dsb = (pb.astype(F32) * (dp - delta)).astype(BF16)
            dq_ref[h] = jnp.dot(dsb, kb, preferred_element_type=F32).astype(BF16)
            dk_ref[h] = lax.dot_general(dsb, qv, _TN, preferred_element_type=F32)
            dv_ref[h] = lax.dot_general(pb, dov, _TN, preferred_element_type=F32)
            ds_ref[h] -= p_sink * delta

    qs, cur, prev, probs = _attn_specs(grp)
    band = pl.BlockSpec((N_KV_HEADS, None, 2 * BLOCK, HEAD_DIM), lambda i: (0, i, 0, 0))
    band_shape = jax.ShapeDtypeStruct((N_KV_HEADS, nb, 2 * BLOCK, HEAD_DIM), F32)
    sink = pl.BlockSpec((N_KV_HEADS, rows, 1), lambda i: (0, 0, 0))
    return pl.pallas_call(
        body, name="attn_bwd", grid=(nb,),
        out_shape=(jax.ShapeDtypeStruct(q.shape, BF16), band_shape, band_shape,
                   jax.ShapeDtypeStruct((N_KV_HEADS, rows, 1), F32)),
        in_specs=[qs, cur, prev, cur, prev, probs, qs, qs], out_specs=(qs, band, band, sink),
        compiler_params=_params(("arbitrary",)),
    )(q, k, k, v, v, p, o, do)


def _qkv_merge_bwd(dq, dkb, dvb, cos, sin):
    grp = N_Q_HEADS // N_KV_HEADS
    nb = dkb.shape[1]
    width = (N_Q_HEADS + 2 * N_KV_HEADS) * HEAD_DIM
    scale = HEAD_DIM ** -0.5

    k0, v0 = N_Q_HEADS * HEAD_DIM, (N_Q_HEADS + N_KV_HEADS) * HEAD_DIM

    def body(dq_ref, kc_ref, kn_ref, vc_ref, vn_ref, c_ref, s_ref, o_ref):
        last = pl.program_id(0) == nb - 1
        c, s = c_ref[...], -s_ref[...]

        def band_sum(cur_ref, nxt_ref, h):
            return cur_ref[h, BLOCK:, :] + jnp.where(last, 0.0, nxt_ref[h, :BLOCK, :])

        cq, sq = c * scale, s * scale
        for t in range(N_Q_HEADS // 2):
            pieces = []
            for e in range(2):
                h, g = divmod(2 * t + e, grp)
                pieces.append(dq_ref[h, g * BLOCK:(g + 1) * BLOCK, :].astype(F32))
            o_ref[:, t * PAIR:(t + 1) * PAIR] = _rope_pair(jnp.concatenate(pieces, axis=1), cq, sq).astype(BF16)
        for t in range(N_KV_HEADS // 2):
            dk = jnp.concatenate([band_sum(kc_ref, kn_ref, 2 * t + e) for e in range(2)], axis=1)
            o_ref[:, k0 + t * PAIR:k0 + (t + 1) * PAIR] = _rope_pair(dk, c, s).astype(BF16)
            dv = jnp.concatenate([band_sum(vc_ref, vn_ref, 2 * t + e) for e in range(2)], axis=1)
            o_ref[:, v0 + t * PAIR:v0 + (t + 1) * PAIR] = dv.astype(BF16)

    tab = pl.BlockSpec((BLOCK, PAIR), lambda i: (i, 0))
    cur = pl.BlockSpec((N_KV_HEADS, None, 2 * BLOCK, HEAD_DIM), lambda i: (0, i, 0, 0))
    nxt = pl.BlockSpec((N_KV_HEADS, None, 2 * BLOCK, HEAD_DIM), lambda i: (0, jnp.minimum(i + 1, nb - 1), 0, 0))
    return pl.pallas_call(
        body, name="qkv_merge_bwd", grid=(nb,), out_shape=jax.ShapeDtypeStruct((nb * BLOCK, width), BF16),
        in_specs=[pl.BlockSpec((N_KV_HEADS, grp * BLOCK, HEAD_DIM), lambda i: (0, i, 0)),
                  cur, nxt, cur, nxt, tab, tab],
        out_specs=pl.BlockSpec((BLOCK, width), lambda i: (i, 0)),
        compiler_params=_params(("parallel",)),
    )(dq, dkb, dkb, dvb, dvb, cos, sin)


def _tiles2d(r, c):
    tc = _pick(c, 2048, 128) if c % 128 == 0 else c
    tr = _pick(r, max(8, (1 << 20) // tc // 8 * 8), 8) if r % 8 == 0 else r
    return tr, tc


def _cast_bf16(name, w, place, wide):
    r, c = w.shape
    tr, tc = _tiles2d(r, c)
    if tr % 16:
        tr = r
    nc = c // tc

    def body(place_ref, w_ref, o_ref):
        o_ref[...] = w_ref[...].astype(BF16)

    if wide:
        out_shape = jax.ShapeDtypeStruct((r, N_CHIPS * c), BF16)
        out_spec = pl.BlockSpec((tr, tc), lambda i, j, p: (i, p[1] * nc + j))
    else:
        out_shape = jax.ShapeDtypeStruct((N_CHIPS, r, c), BF16)
        out_spec = pl.BlockSpec((None, tr, tc), lambda i, j, p: (p[1], i, j))
    return pl.pallas_call(
        body, name=name, out_shape=out_shape,
        grid_spec=pltpu.PrefetchScalarGridSpec(
            num_scalar_prefetch=1, grid=(r // tr, nc),
            in_specs=[pl.BlockSpec((tr, tc), lambda i, j, p: (i, j))], out_specs=out_spec),
        compiler_params=_params(("parallel", "parallel")),
    )(place, w)


def _pair_sum(name, g, got, place):
    n, r, c = g.shape
    half = r // 2
    tr, tc = _tiles2d(half, c)
    nh = half // tr

    def body(place_ref, g_ref, got_ref, o_ref, own_ref):
        s = (g_ref[...] + got_ref[...]).astype(BF16)
        o_ref[...] = s

        @pl.when(pl.program_id(2) == place_ref[1])
        def _():
            own_ref[...] = s

    tile = pl.BlockSpec((None, tr, tc), lambda i, j, k, p: (k, i, j))
    shape = jax.ShapeDtypeStruct((n, half, c), BF16)
    return pl.pallas_call(
        body, name=name, out_shape=(shape, shape),
        grid_spec=pltpu.PrefetchScalarGridSpec(
            num_scalar_prefetch=1, grid=(nh, c // tc, n),
            in_specs=[pl.BlockSpec((None, tr, tc), lambda i, j, k, p: (k, p[0] * nh + i, j)), tile],
            out_specs=(tile, pl.BlockSpec((None, tr, tc), lambda i, j, k, p: (p[1], i, j)))),
        compiler_params=_params(("parallel", "parallel", "arbitrary")),
    )(place, g, got)


def _chip_sum(name, parts, place):
    n, half, c = parts.shape
    tr, tc = _tiles2d(half, c)
    nh = half // tr

    def body(place_ref, p0, p1, p2, p3, o_ref):
        o_ref[...] = ((p0[...].astype(F32) + p1[...].astype(F32)) + p2[...].astype(F32)) + p3[...].astype(F32)

    def chip(k):
        return pl.BlockSpec((None, tr, tc), lambda i, j, p: (k, i, j))

    return pl.pallas_call(
        body, name=name, out_shape=jax.ShapeDtypeStruct((2 * half, c), F32),
        grid_spec=pltpu.PrefetchScalarGridSpec(
            num_scalar_prefetch=1, grid=(nh, c // tc),
            in_specs=[chip(k) for k in range(n)],
            out_specs=pl.BlockSpec((tr, tc), lambda i, j, p: (p[0] * nh + i, j))),
        compiler_params=_params(("parallel", "parallel")),
    )(place, parts, parts, parts, parts)


def _dev_sum(gathered):
    def body(g_ref, o_ref):
        acc = g_ref[0]
        for k in range(1, N_DEV):
            acc = acc + g_ref[k]
        o_ref[...] = acc

    return pl.pallas_call(body, name="dev_sum", out_shape=jax.ShapeDtypeStruct(gathered.shape[1:], F32))(gathered)


def _adamw(name, w, g, m, v):
    r, c = w.shape
    tr, tc = _tiles2d(r, c)
    if r % 8 == 0:
        tr = _pick(r, max(8, (1 << 18) // tc // 8 * 8), 8)

    def body(w_ref, g_ref, m_ref, v_ref, d_ref, mo_ref, vo_ref):
        gv = g_ref[...]
        mn = ADAM_B1 * m_ref[...] + (1.0 - ADAM_B1) * gv
        vn = ADAM_B2 * v_ref[...] + (1.0 - ADAM_B2) * jnp.square(gv)
        m_hat = mn / (1.0 - ADAM_B1 ** ADAM_STEP)
        v_hat = vn / (1.0 - ADAM_B2 ** ADAM_STEP)
        d_ref[...] = -ADAM_LR * (m_hat / (jnp.sqrt(v_hat) + ADAM_EPS) + ADAM_WD * w_ref[...])
        mo_ref[...] = mn
        vo_ref[...] = vn

    tile = pl.BlockSpec((tr, tc), lambda i, j: (i, j))
    shape = jax.ShapeDtypeStruct((r, c), F32)
    return pl.pallas_call(
        body, name=name, grid=(r // tr, c // tc), out_shape=(shape, shape, shape),
        in_specs=[tile] * 4, out_specs=(tile,) * 3, compiler_params=_params(("parallel", "parallel")),
    )(w, g, m, v)


MATRICES = ("w_in_conv", "w_out_conv", "w_up_0", "w_down_0", "w_qkv", "w_o", "w_up_1", "w_down_1")
COLUMN_SHARDED = ("w_in_conv", "w_up_0", "w_qkv", "w_up_1")
NORMS = ("norm_mix_0", "norm_mlp_0", "norm_mix_1", "norm_mlp_1", "norm_final")


def _rows(stack):
    return stack.reshape(N_CHIPS * stack.shape[1], stack.shape[2])


def _stack(full):
    return full.reshape(N_CHIPS, full.shape[0] // N_CHIPS, full.shape[1])


def _add_residual(acc, res):
    return acc + res


def _relu_sq_grad(acc, z):
    return acc * (2.0 * jnp.maximum(z.astype(F32), 0.0))


def _step(x, target, stacks, small, norms, sinks, place):
    d = D_MODEL
    dc = d // N_CHIPS
    pad = ROW0 - N_META
    m = x.shape[0] + ROW0
    grp = N_Q_HEADS // N_KV_HEADS
    cos, sin = _rope_tables(m)
    sink_rows = jnp.repeat(sinks.astype(F32), BLOCK).reshape(N_KV_HEADS, grp * BLOCK, 1)

    def gather(*names):
        return _gather_task([stacks[n] for n in names])

    def pair_sum(tag, grad, got):
        return _pair_sum("pair_sum_" + tag, grad, got, place)

    def chip_sum(tag, landed):
        return _chip_sum("chip_sum_" + tag, landed, place)

    (w_in, small_all), = _run("gather_first", [_gather_task([stacks["w_in_conv"], small])])
    small_full = jnp.transpose(small_all, (1, 0, 2)).reshape(SMALL_ROWS, d)
    conv_w8 = small_full[N_META:N_META + 8]
    h0 = jnp.concatenate([jnp.zeros((pad, d), F32), small_full[:N_META], x], axis=0)

    n0 = _rms_fwd("norm_mix_0", h0, norms["norm_mix_0"])
    bcu, ((w_out, w_up0),) = _mm_nn("conv_in", n0, w_in, BF16, tasks=[gather("w_out_conv", "w_up_0")])
    gate = _conv_fwd(bcu, conv_w8)
    h1 = _mm_nn("conv_out", gate, _rows(w_out), F32, epi=_add_residual, extras=(h0,))
    n1 = _rms_fwd("norm_mlp_0", h1, norms["norm_mlp_0"])
    z0, ((w_down0,),) = _mm_nn("mlp_up_0", n1, w_up0, BF16, tasks=[gather("w_down_0")])
    h2, ((w_qkv, w_o, w_up1),) = _mm_nn("mlp_down_0", z0, _rows(w_down0), F32, a_pro=_relu_sq, epi=_add_residual,
                                             extras=(h1,), tasks=[gather("w_qkv", "w_o", "w_up_1")])
    n2 = _rms_fwd("norm_mix_1", h2, norms["norm_mix_1"])
    qkv = _mm_nn("attn_qkv", n2, w_qkv, BF16)
    q, k, v = _qkv_split(qkv, cos, sin)
    o_heads, probs = _attn_fwd(q, k, v, sink_rows)
    o = _heads_merge("attn_o_merge", o_heads)
    h3 = _mm_nn("attn_out", o, _rows(w_o), F32, epi=_add_residual, extras=(h2,))
    n3 = _rms_fwd("norm_mlp_1", h3, norms["norm_mlp_1"])
    z1, ((w_down1,),) = _mm_nn("mlp_up_1", n3, w_up1, BF16, tasks=[gather("w_down_1")])
    h4 = _mm_nn("mlp_down_1", z1, _rows(w_down1), F32, a_pro=_relu_sq, epi=_add_residual, extras=(h3,))

    gn = {}
    loss, dh, dh_bf, gn["norm_final"] = _loss_head(h4, norms["norm_final"], target)
    dz = _mm_nt("mlp_down_dx_1", dh_bf, _rows(w_down1), BF16, epi=_relu_sq_grad, extras=(z1,))
    g_d1 = _stack(_mm_tn("mlp_down_dw_1", z1, dh_bf, stacked=False, a_pro=_relu_sq))
    g_u1, ((got,),) = _mm_tn("mlp_up_dw_1", n3, dz, stacked=True, tasks=[_pair_exchange_task([g_d1])])
    s_d1 = pair_sum("d1", g_d1, got)
    dn, ((got,), (landed,)) = _mm_nt("mlp_up_dx_1", dz, w_up1, BF16,
                                          tasks=[_pair_exchange_task([g_u1]), _chip_exchange_task([s_d1])])
    s_u1, b_d1 = pair_sum("u1", g_u1, got), chip_sum("d1", landed)
    dh, dh_bf, gn["norm_mlp_1"] = _rms_bwd("norm_mlp_bwd_1", dn, h3, norms["norm_mlp_1"], dh)
    do = _mm_nt("attn_out_dx", dh_bf, _rows(w_o), BF16)
    g_o = _stack(_mm_tn("attn_out_dw", o, dh_bf, stacked=False))
    dq, dkb, dvb, dsink = _attn_bwd(q, k, v, probs, o_heads, _heads_split("attn_do_split", do))
    dqkv = _qkv_merge_bwd(dq, dkb, dvb, cos, sin)
    g_qkv, ((got,),) = _mm_tn("attn_qkv_dw", n2, dqkv, stacked=True, tasks=[_pair_exchange_task([g_o])])
    s_o = pair_sum("o", g_o, got)
    dn, ((got,), (landed,)) = _mm_nt("attn_qkv_dx", dqkv, w_qkv, BF16,
                                          tasks=[_pair_exchange_task([g_qkv]), _chip_exchange_task([s_u1])])
    s_qkv, b_u1 = pair_sum("qkv", g_qkv, got), chip_sum("u1", landed)
    dh, dh_bf, gn["norm_mix_1"] = _rms_bwd("norm_mix_bwd_1", dn, h2, norms["norm_mix_1"], dh)
    dz, ((landed_o, landed_qkv), (r_d1,)) = _mm_nt(
        "mlp_down_dx_0", dh_bf, _rows(w_down0), BF16, epi=_relu_sq_grad, extras=(z0,),
        tasks=[_chip_exchange_task([s_o, s_qkv]), _pair_share_task([b_d1])])
    b_o, b_qkv = chip_sum("o", landed_o), chip_sum("qkv", landed_qkv)
    g_d0, ((r_u1,),) = _mm_tn("mlp_down_dw_0", z0, dh_bf, stacked=False, a_pro=_relu_sq, tasks=[_pair_share_task([b_u1])])
    g_d0 = _stack(g_d0)
    g_u0, ((got,), (r_o, r_qkv)) = _mm_tn("mlp_up_dw_0", n1, dz, stacked=True,
                                          tasks=[_pair_exchange_task([g_d0]), _pair_share_task([b_o, b_qkv])])
    s_d0 = pair_sum("d0", g_d0, got)
    dn, ((got,), (landed,)) = _mm_nt("mlp_up_dx_0", dz, w_up0, BF16,
                                          tasks=[_pair_exchange_task([g_u0]), _chip_exchange_task([s_d0])])
    s_u0, b_d0 = pair_sum("u0", g_u0, got), chip_sum("d0", landed)
    dh, dh_bf, gn["norm_mlp_0"] = _rms_bwd("norm_mlp_bwd_0", dn, h1, norms["norm_mlp_0"], dh)
    dgate = _mm_nt("conv_out_dx", dh_bf, _rows(w_out), BF16)
    dbcu, g_conv_w = _conv_bwd(bcu, conv_w8, dgate)
    g_in, ((landed,), (r_d0,)) = _mm_tn("conv_in_dw", n0, dbcu, stacked=True,
                                        tasks=[_chip_exchange_task([s_u0]), _pair_share_task([b_d0])])
    b_u0 = chip_sum("u0", landed)
    dn, ((got,), (r_u0,)) = _mm_nt("conv_in_dx", dbcu, w_in, BF16,
                                        tasks=[_pair_exchange_task([g_in]), _pair_share_task([b_u0])])
    s_in = pair_sum("in", g_in, got)
    grad_x, dh_first, gn["norm_mix_0"] = _rms_bwd_tokens("norm_mix_bwd_0", dn, h0, norms["norm_mix_0"], dh)

    g_small = jnp.zeros((SMALL_ROWS, d), F32).at[:N_META].set(dh_first[pad:ROW0]).at[N_META:N_META + 8].set(g_conv_w)
    g_small = jnp.transpose(g_small.reshape(SMALL_ROWS, N_CHIPS, dc), (1, 0, 2))
    rep = jnp.zeros((8, d), F32)
    for r, n in enumerate(NORMS):
        rep = rep.at[r].set(jnp.sum(gn[n], axis=0))
    rep = rep.at[len(NORMS), :N_Q_HEADS].set(jnp.sum(dsink.reshape(N_Q_HEADS, BLOCK), axis=1))
    g_out, ((landed,), (got, rep_all)) = _mm_tn(
        "conv_out_dw", gate, dh_bf, stacked=False,
        tasks=[_chip_exchange_task([s_in]), _pair_exchange_task([g_small], small=rep)])
    g_out = _stack(g_out)
    b_in, s_small = chip_sum("in", landed), pair_sum("small", g_small, got)
    (got,), = _run("tail_pair_exchange", [_pair_exchange_task([g_out])])
    s_out = pair_sum("out", g_out, got)
    (landed_out, landed_small), = _run("tail_chip_exchange", [_chip_exchange_task([s_out, s_small])])
    b_out, b_small = chip_sum("out", landed_out), chip_sum("small", landed_small)
    (r_out, r_small, r_in), = _run("tail_pair_share", [_pair_share_task([b_out, b_small, b_in])])

    reduced = {"w_in_conv": r_in, "w_out_conv": r_out, "w_up_0": r_u0, "w_down_0": r_d0, "w_qkv": r_qkv, "w_o": r_o,
               "w_up_1": r_u1, "w_down_1": r_d1}
    return loss, grad_x, reduced, r_small, rep_all


def kernel(x, meta_tokens, norm_mix_0, w_in_conv, conv_w, w_out_conv, norm_mlp_0, w_up_0, w_down_0, norm_mix_1, w_qkv, attn_sinks, w_o, norm_mlp_1, w_up_1, w_down_1, norm_final, loss_target, m_meta_tokens, m_norm_mix_0, m_w_in_conv, m_conv_w, m_w_out_conv, m_norm_mlp_0, m_w_up_0, m_w_down_0, m_norm_mix_1, m_w_qkv, m_attn_sinks, m_w_o, m_norm_mlp_1, m_w_up_1, m_w_down_1, m_norm_final, v_meta_tokens, v_norm_mix_0, v_w_in_conv, v_conv_w, v_w_out_conv, v_norm_mlp_0, v_w_up_0, v_w_down_0, v_norm_mix_1, v_w_qkv, v_attn_sinks, v_w_o, v_norm_mlp_1, v_w_up_1, v_w_down_1, v_norm_final):
    given = dict(locals())
    names = ("meta_tokens", "norm_mix_0", "w_in_conv", "conv_w", "w_out_conv", "norm_mlp_0", "w_up_0", "w_down_0",
             "norm_mix_1", "w_qkv", "attn_sinks", "w_o", "norm_mlp_1", "w_up_1", "w_down_1", "norm_final")
    d = D_MODEL
    dc = d // N_CHIPS
    chip = 2 * lax.axis_index("x") + lax.axis_index("y")
    place = jnp.stack([lax.axis_index("c"), chip]).astype(jnp.int32)

    small = jnp.zeros((SMALL_ROWS, dc), F32).at[:N_META].set(meta_tokens).at[N_META:N_META + CONV_WIDTH].set(conv_w)
    small = lax.dynamic_update_slice(jnp.zeros((N_CHIPS, SMALL_ROWS, dc), F32), small[None], (chip, 0, 0))
    stacks = {n: _cast_bf16("cast_" + n, given[n], place, n in COLUMN_SHARDED) for n in MATRICES}

    norms = {n: given[n] for n in NORMS}
    loss_part, grad_x, g_out, r_small, rep_all = _step(x[0], loss_target[0], stacks, small, norms, attn_sinks, place)
    loss = lax.psum(loss_part[0, 0], ("x", "y", "c"))
    grad_x = grad_x[None]
    rep_sum = _dev_sum(rep_all)
    g_out["meta_tokens"] = r_small[:N_META]
    g_out["conv_w"] = r_small[N_META:N_META + CONV_WIDTH]
    for r, n in enumerate(NORMS):
        g_out[n] = rep_sum[r]
    g_out["attn_sinks"] = rep_sum[len(NORMS), :N_Q_HEADS]

    delta, new_m, new_v = {}, {}, {}
    for n in names:
        wt = given[n]
        shape2 = wt.shape if wt.ndim == 2 else (1, wt.shape[0])
        outs = _adamw("adamw_" + n, wt.reshape(shape2), g_out[n].reshape(shape2),
                      given["m_" + n].reshape(shape2), given["v_" + n].reshape(shape2))
        delta[n], new_m[n], new_v[n] = [o.reshape(wt.shape) for o in outs]
    return (loss, grad_x, *[g_out[n] for n in names], *[delta[n] for n in names],
            *[new_m[n] for n in names], *[new_v[n] for n in names])
```

```python
import jax
import jax.numpy as jnp
from jax import lax
from jax.experimental import pallas as pl
from jax.experimental.pallas import tpu as pltpu

F32 = jnp.float32
BF16 = jnp.bfloat16

D_MODEL = 2048
SEQ = 8192
N_META = 16
CONV_WIDTH = 3
HEAD_DIM = 64
N_Q_HEADS = 32
N_KV_HEADS = 4
BLOCK = 128
ROPE_THETA = 10000.0
D_FF = 4 * D_MODEL
RMS_EPS = 1e-5
NEG_INF = -1e30

ADAM_LR = 0.001
ADAM_B1 = 0.9
ADAM_B2 = 0.999
ADAM_EPS = 1e-08
ADAM_WD = 0.01
ADAM_STEP = 10

N_CHIPS = 4
N_DEV = 8
MESH = pl.DeviceIdType.MESH
VMEM_LIMIT = 56 * 1024 * 1024
SMALL_ROWS = 32
ROW0 = BLOCK


def _pick(n, target, mult):
    best = None
    for t in range(mult, min(n, target) + 1, mult):
        if n % t == 0:
            best = t
    assert best is not None, (n, target, mult)
    return best


def _params(sem=None):
    return pltpu.CompilerParams(dimension_semantics=sem, vmem_limit_bytes=VMEM_LIMIT)


HBM_SPEC = pl.BlockSpec(memory_space=pltpu.HBM)


class _Task:
    def __init__(self, inputs, outputs, aliases, sem_shapes, bind):
        self.inputs, self.outputs, self.aliases = list(inputs), list(outputs), dict(aliases)
        self.sem_shapes, self.bind = list(sem_shapes), bind


def _like(arrays):
    return [jax.ShapeDtypeStruct(a.shape, a.dtype) for a in arrays]


def _bind_tasks(tasks, in_refs, out_refs, sem_refs):
    bound, i, o, s = [], 0, 0, 0
    for t in tasks:
        ni, no, ns = len(t.inputs), len(t.outputs), len(t.sem_shapes)
        bound.append(t.bind(in_refs[i:i + ni], out_refs[o:o + no], sem_refs[s:s + ns]))
        i, o, s = i + ni, o + no, s + ns
    return bound


def _run_phase(bound, phase):
    for b in bound:
        if b[phase] is not None:
            b[phase]()


def _task_plumbing(tasks, in_offset, out_offset):
    ins = [a for t in tasks for a in t.inputs]
    outs = [o for t in tasks for o in t.outputs]
    sems = [s for t in tasks for s in t.sem_shapes]
    aliases, i, o = {}, in_offset, out_offset
    for t in tasks:
        for src, dst in t.aliases.items():
            aliases[i + src] = o + dst
        i, o = i + len(t.inputs), o + len(t.outputs)
    return ins, outs, sems, aliases


def _split_outputs(tasks, flat):
    res, o = [], 0
    for t in tasks:
        res.append(list(flat[o:o + len(t.outputs)]))
        o += len(t.outputs)
    return res


def _run(name, tasks):
    ins, outs, sems, aliases = _task_plumbing(tasks, 0, 0)

    def body(*refs):
        bound = _bind_tasks(tasks, refs[:len(ins)], refs[len(ins):len(ins) + len(outs)], refs[len(ins) + len(outs):])
        for phase in range(3):
            _run_phase(bound, phase)

    flat = pl.pallas_call(
        body, name=name, out_shape=outs, in_specs=[HBM_SPEC] * len(ins), out_specs=[HBM_SPEC] * len(outs),
        input_output_aliases=aliases, scratch_shapes=sems,
    )(*ins)
    return _split_outputs(tasks, flat)


def _place():
    x, y, c = lax.axis_index("x"), lax.axis_index("y"), lax.axis_index("c")
    chips = [(1 - x, y), (x, 1 - y), (1 - x, 1 - y)]
    return x, y, c, 2 * x + y, chips


def _gather_task(stacks):
    n = len(stacks)
    halves = [s.shape[-2] // 2 for s in stacks]

    def bind(_, dst, sems):
        send_a, recv_a, send_b, recv_b = sems
        x, y, c, me, chips = _place()
        sibling = (x, y, 1 - c)

        def half(w, chip, hc):
            rows = pl.ds(hc * halves[w], halves[w])
            if len(stacks[w].shape) == 3:
                return dst[w].at[chip, rows, :]
            cols = stacks[w].shape[1] // N_CHIPS
            return dst[w].at[rows, pl.ds(chip * cols, cols)]

        def over_ici(j, w, block):
            return pltpu.make_async_remote_copy(
                src_ref=half(w, block, c), dst_ref=half(w, block, c), send_sem=send_a.at[j * n + w],
                recv_sem=recv_a.at[j * n + w], device_id=(*chips[j], c), device_id_type=MESH)

        def over_d2d(j, w, hc):
            got = half(w, 2 * chips[j][0] + chips[j][1], hc)
            return pltpu.make_async_remote_copy(
                src_ref=got, dst_ref=got, send_sem=send_b.at[j * n + w], recv_sem=recv_b.at[j * n + w],
                device_id=sibling, device_id_type=MESH)

        pairs = [(j, w) for j in range(3) for w in range(n)]

        def start():
            for j, w in pairs:
                over_ici(j, w, me).start()

        def mid():
            for j, w in pairs:
                over_ici(j, w, 2 * chips[j][0] + chips[j][1]).wait_recv()
                over_d2d(j, w, c).start()

        def finish():
            for j, w in pairs:
                over_d2d(j, w, 1 - c).wait_recv()
            for j, w in pairs:
                over_ici(j, w, me).wait_send()
                over_d2d(j, w, c).wait_send()

        return start, mid, finish

    return _Task(stacks, _like(stacks), {w: w for w in range(n)}, [pltpu.SemaphoreType.DMA((3 * n,))] * 4, bind)


def _pair_exchange_task(grads, small=None):
    n = len(grads)
    halves = [g.shape[1] // 2 for g in grads]

    def bind(src, dst, sems):
        send, recv = sems[0], sems[1]
        x, y, c, me, _ = _place()
        sibling = (x, y, 1 - c)
        dev = 2 * me + c

        def to_sibling(w):
            return pltpu.make_async_remote_copy(
                src_ref=src[w].at[:, pl.ds((1 - c) * halves[w], halves[w]), :], dst_ref=dst[w],
                send_sem=send.at[w], recv_sem=recv.at[w], device_id=sibling, device_id_type=MESH)

        def to_peer(t, block):
            tx, ty, tc = (t >> 2) & 1, (t >> 1) & 1, t & 1
            return pltpu.make_async_remote_copy(
                src_ref=src[n], dst_ref=dst[n].at[block], send_sem=sems[2].at[t], recv_sem=sems[3].at[t],
                device_id=(x ^ tx, y ^ ty, c ^ tc), device_id_type=MESH)

        def mine():
            return pltpu.make_async_copy(src[n], dst[n].at[dev], sems[4])

        def start():
            for w in range(n):
                to_sibling(w).start()
            if small is not None:
                mine().start()
                for t in range(1, N_DEV):
                    to_peer(t, dev).start()

        def finish():
            for w in range(n):
                to_sibling(w).wait_recv()
            if small is not None:
                for t in range(1, N_DEV):
                    to_peer(t, dev ^ t).wait_recv()
            for w in range(n):
                to_sibling(w).wait_send()
            if small is not None:
                for t in range(1, N_DEV):
                    to_peer(t, dev).wait_send()
                mine().wait()

        return start, None, finish

    outputs = [jax.ShapeDtypeStruct((N_CHIPS, h, g.shape[2]), g.dtype) for g, h in zip(grads, halves)]
    sem_shapes = [pltpu.SemaphoreType.DMA((n,)), pltpu.SemaphoreType.DMA((n,))]
    inputs = list(grads)
    if small is not None:
        inputs.append(small)
        outputs.append(jax.ShapeDtypeStruct((N_DEV,) + small.shape, small.dtype))
        sem_shapes += [pltpu.SemaphoreType.DMA((N_DEV,)), pltpu.SemaphoreType.DMA((N_DEV,)), pltpu.SemaphoreType.DMA(())]
    return _Task(inputs, outputs, {}, sem_shapes, bind)


def _chip_exchange_task(summed):
    n = len(summed)

    def bind(refs, dst, sems):
        src = refs[:n]
        send, recv = sems
        x, y, c, me, chips = _place()

        def copy(j, w, block_from, block_to):
            return pltpu.make_async_remote_copy(
                src_ref=src[w].at[block_from], dst_ref=dst[w].at[block_to], send_sem=send.at[j * n + w],
                recv_sem=recv.at[j * n + w], device_id=(*chips[j], c), device_id_type=MESH)

        pairs = [(j, w) for j in range(3) for w in range(n)]

        def start():
            for j, w in pairs:
                copy(j, w, 2 * chips[j][0] + chips[j][1], me).start()

        def finish():
            for j, w in pairs:
                copy(j, w, me, 2 * chips[j][0] + chips[j][1]).wait_recv()
            for j, w in pairs:
                copy(j, w, 2 * chips[j][0] + chips[j][1], me).wait_send()

        return start, None, finish

    partials, landing = [s[0] for s in summed], [s[1] for s in summed]
    return _Task(partials + landing, _like(landing), {n + w: w for w in range(n)},
                 [pltpu.SemaphoreType.DMA((3 * n,))] * 2, bind)


def _pair_share_task(blocks):
    n = len(blocks)

    def bind(_, dst, sems):
        send, recv = sems
        x, y, c, _, _ = _place()

        def copy(w, hc):
            h = blocks[w].shape[0] // 2
            rows = dst[w].at[pl.ds(hc * h, h), :]
            return pltpu.make_async_remote_copy(src_ref=rows, dst_ref=rows, send_sem=send.at[w], recv_sem=recv.at[w],
                                                device_id=(x, y, 1 - c), device_id_type=MESH)

        def start():
            for w in range(n):
                copy(w, c).start()

        def finish():
            for w in range(n):
                copy(w, 1 - c).wait_recv()
            for w in range(n):
                copy(w, c).wait_send()

        return start, None, finish

    return _Task(blocks, _like(blocks), {w: w for w in range(n)}, [pltpu.SemaphoreType.DMA((n,))] * 2, bind)


def _mm(name, a, b, *, dims, grid, a_spec, b_spec, out_shape, out_spec,
        extras=(), extra_specs=(), a_pro=None, epi=None, tasks=()):
    n_ex = len(extras)
    t_ins, t_outs, t_sems, aliases = _task_plumbing(tasks, 2 + n_ex, 1)
    n_ti, n_to = len(t_ins), len(t_outs)
    total = grid[0] * grid[1]
    mid_step = max(0, total - 1 - max(1, total // 8))

    def body(*refs):
        a_ref, b_ref = refs[0], refs[1]
        ex = refs[2:2 + n_ex]
        o_ref = refs[2 + n_ex + n_ti]
        if tasks:
            bound = _bind_tasks(tasks, refs[2 + n_ex:2 + n_ex + n_ti],
                                refs[3 + n_ex + n_ti:3 + n_ex + n_ti + n_to], refs[3 + n_ex + n_ti + n_to:])
            step = pl.program_id(0) * grid[1] + pl.program_id(1)

            @pl.when(step == 0)
            def _():
                _run_phase(bound, 0)

        av = a_ref[...]
        if a_pro is not None:
            av = a_pro(av)
        acc = lax.dot_general(av, b_ref[...], dims, preferred_element_type=F32)
        if epi is not None:
            acc = epi(acc, *[e[...] for e in ex])
        o_ref[...] = acc.astype(o_ref.dtype)

        if tasks:
            @pl.when(step == mid_step)
            def _():
                _run_phase(bound, 1)

            @pl.when(step == total - 1)
            def _():
                _run_phase(bound, 2)

    sem = ("arbitrary", "arbitrary") if tasks else ("parallel", "parallel")
    res = pl.pallas_call(
        body, name=name, grid=grid, out_shape=[out_shape, *t_outs],
        in_specs=[a_spec, b_spec, *extra_specs, *[HBM_SPEC] * n_ti], out_specs=[out_spec, *[HBM_SPEC] * n_to],
        input_output_aliases=aliases, scratch_shapes=t_sems, compiler_params=_params(sem),
    )(a, b, *extras, *t_ins)
    return (res[0], _split_outputs(tasks, res[1:])) if tasks else res[0]


_NN = (((1,), (0,)), ((), ()))
_NT = (((1,), (1,)), ((), ()))
_TN = (((0,), (0,)), ((), ()))


MM_TILE_BUDGET = 46 * 1024 * 1024


def _mm_tiles(m, n, contraction, out_bytes):
    for rows, cols in ((1664, 1024), (832, 1024), (416, 1024), (416, 512)):
        tm, tn = _pick(m, rows, 16), _pick(n, cols, 128)
        if 2 * 2 * contraction * (tm + tn) + tm * tn * (4 + 2 * out_bytes) <= MM_TILE_BUDGET:
            break
    return tm, tn


def _out_bytes(out_dtype, extras):
    return jnp.dtype(out_dtype).itemsize + sum(e.dtype.itemsize for e in extras)


def _mm_nn(name, a, w, out_dtype, a_pro=None, epi=None, extras=(), tasks=()):
    m, k = a.shape
    _, n = w.shape
    tm, tn = _mm_tiles(m, n, k, _out_bytes(out_dtype, extras))
    tile = pl.BlockSpec((tm, tn), lambda j, i: (i, j))
    return _mm(name, a, w, dims=_NN, grid=(n // tn, m // tm),
               a_spec=pl.BlockSpec((tm, k), lambda j, i: (i, 0)), b_spec=pl.BlockSpec((k, tn), lambda j, i: (0, j)),
               out_shape=jax.ShapeDtypeStruct((m, n), out_dtype), out_spec=tile,
               extras=extras, extra_specs=[tile] * len(extras), a_pro=a_pro, epi=epi, tasks=tasks)


def _mm_nt(name, a, w, out_dtype, epi=None, extras=(), tasks=()):
    m, c = a.shape
    r, _ = w.shape
    tm, tn = _mm_tiles(m, r, c, _out_bytes(out_dtype, extras))
    tile = pl.BlockSpec((tm, tn), lambda j, i: (i, j))
    return _mm(name, a, w, dims=_NT, grid=(r // tn, m // tm),
               a_spec=pl.BlockSpec((tm, c), lambda j, i: (i, 0)), b_spec=pl.BlockSpec((tn, c), lambda j, i: (j, 0)),
               out_shape=jax.ShapeDtypeStruct((m, r), out_dtype), out_spec=tile,
               extras=extras, extra_specs=[tile] * len(extras), epi=epi, tasks=tasks)


def _mm_tn(name, a, b, stacked, a_pro=None, tasks=()):
    t, ka = a.shape
    _, nb = b.shape
    ns = nb // N_CHIPS if stacked else nb
    ta, tb = _pick(ka, 512, 128), _pick(ns, 640, 128)
    if stacked:
        per = ns // tb
        out_shape = jax.ShapeDtypeStruct((N_CHIPS, ka, ns), F32)
        out_spec = pl.BlockSpec((None, ta, tb), lambda i, j: (j // per, i, j % per))
    else:
        out_shape = jax.ShapeDtypeStruct((ka, nb), F32)
        out_spec = pl.BlockSpec((ta, tb), lambda i, j: (i, j))
    return _mm(name, a, b, dims=_TN, grid=(ka // ta, nb // tb),
               a_spec=pl.BlockSpec((t, ta), lambda i, j: (0, i)), b_spec=pl.BlockSpec((t, tb), lambda i, j: (0, j)),
               out_shape=out_shape, out_spec=out_spec, a_pro=a_pro, tasks=tasks)


def _relu_sq(z):
    a = jnp.maximum(z, 0)
    return a * a


def _rms_fwd(name, h, g):
    m, d = h.shape
    tr = _pick(m, 256, 16)

    def body(h_ref, g_ref, o_ref):
        x = h_ref[...]
        rstd = lax.rsqrt(jnp.mean(x * x, axis=-1, keepdims=True) + RMS_EPS)
        o_ref[...] = ((x * rstd) * g_ref[...]).astype(BF16)

    row = pl.BlockSpec((tr, d), lambda i: (i, 0))
    return pl.pallas_call(
        body, name=name, grid=(m // tr,), out_shape=jax.ShapeDtypeStruct((m, d), BF16),
        in_specs=[row, pl.BlockSpec((1, d), lambda i: (0, 0))], out_specs=row,
        compiler_params=_params(("parallel",)),
    )(h, g.reshape(1, d))


def _rms_bwd_math(x, g, dn):
    rstd = lax.rsqrt(jnp.mean(x * x, axis=-1, keepdims=True) + RMS_EPS)
    xhat = x * rstd
    dxhat = dn * g
    dx = rstd * (dxhat - xhat * jnp.mean(dxhat * xhat, axis=-1, keepdims=True))
    return dx, dn * xhat


def _fold8(v):
    r, c = v.shape
    return jnp.sum(v.reshape(r // 8, 8, c), axis=0)


def _rms_bwd(name, dn, h, g, dh_in):
    m, d = h.shape
    tr = _pick(m, 256, 16)
    nt = m // tr

    def body(dn_ref, h_ref, g_ref, dh_ref, o_ref, ob_ref, dg_ref):
        dx, dgp = _rms_bwd_math(h_ref[...], g_ref[...], dn_ref[...].astype(F32))
        dh = dh_ref[...] + dx
        o_ref[...] = dh
        ob_ref[...] = dh.astype(BF16)

        @pl.when(pl.program_id(0) == 0)
        def _():
            dg_ref[...] = jnp.zeros_like(dg_ref)

        dg_ref[...] += _fold8(dgp)

    row = pl.BlockSpec((tr, d), lambda i: (i, 0))
    return pl.pallas_call(
        body, name=name, grid=(nt,),
        out_shape=(jax.ShapeDtypeStruct((m, d), F32), jax.ShapeDtypeStruct((m, d), BF16),
                   jax.ShapeDtypeStruct((8, d), F32)),
        in_specs=[row, row, pl.BlockSpec((1, d), lambda i: (0, 0)), row],
        out_specs=(row, row, pl.BlockSpec((8, d), lambda i: (0, 0))),
        compiler_params=_params(("arbitrary",)),
    )(dn, h, g.reshape(1, d), dh_in)


def _rms_bwd_tokens(name, dn, h, g, dh_in):
    m, d = h.shape
    nb = m // BLOCK

    def body(dn_ref, h_ref, g_ref, dh_ref, gx_ref, first_ref, dg_ref):
        i = pl.program_id(0)
        dx, dgp = _rms_bwd_math(h_ref[...], g_ref[...], dn_ref[...].astype(F32))
        dh = dh_ref[...] + dx
        gx_ref[...] = dh

        @pl.when(i == 0)
        def _():
            first_ref[...] = dh
            dg_ref[...] = jnp.zeros_like(dg_ref)

        dg_ref[...] += _fold8(dgp)

    row = pl.BlockSpec((BLOCK, d), lambda i: (i, 0))
    return pl.pallas_call(
        body, name=name, grid=(nb,),
        out_shape=(jax.ShapeDtypeStruct((m - ROW0, d), F32), jax.ShapeDtypeStruct((ROW0, d), F32),
                   jax.ShapeDtypeStruct((8, d), F32)),
        in_specs=[row, row, pl.BlockSpec((1, d), lambda i: (0, 0)), row],
        out_specs=(pl.BlockSpec((BLOCK, d), lambda i: (jnp.maximum(i - 1, 0), 0)),
                   pl.BlockSpec((ROW0, d), lambda i: (0, 0)), pl.BlockSpec((8, d), lambda i: (0, 0))),
        compiler_params=_params(("arbitrary",)),
    )(dn, h, g.reshape(1, d), dh_in)


def _loss_head(h, g, target):
    m, d = h.shape
    tr = BLOCK

    def body(h_ref, g_ref, t_ref, loss_ref, o_ref, ob_ref, dg_ref):
        i = pl.program_id(0)
        x = h_ref[...]
        gv = g_ref[...]
        rstd = lax.rsqrt(jnp.mean(x * x, axis=-1, keepdims=True) + RMS_EPS)
        err = jnp.where(i > 0, (x * rstd) * gv - t_ref[...], 0.0)
        dx, dgp = _rms_bwd_math(x, gv, err * (1.0 / d))
        o_ref[...] = dx
        ob_ref[...] = dx.astype(BF16)

        @pl.when(i == 0)
        def _():
            dg_ref[...] = jnp.zeros_like(dg_ref)
            loss_ref[...] = jnp.zeros_like(loss_ref)

        dg_ref[...] += _fold8(dgp)
        sq = jnp.mean(err * err, axis=-1, keepdims=True)
        loss_ref[...] += 0.5 * jnp.sum(sq, axis=0, keepdims=True)

    row = pl.BlockSpec((tr, d), lambda i: (i, 0))
    return pl.pallas_call(
        body, name="loss_head", grid=(m // tr,),
        out_shape=(jax.ShapeDtypeStruct((8, 128), F32), jax.ShapeDtypeStruct((m, d), F32),
                   jax.ShapeDtypeStruct((m, d), BF16), jax.ShapeDtypeStruct((8, d), F32)),
        in_specs=[row, pl.BlockSpec((1, d), lambda i: (0, 0)),
                  pl.BlockSpec((tr, d), lambda i: (jnp.maximum(i - 1, 0), 0))],
        out_specs=(pl.BlockSpec((8, 128), lambda i: (0, 0)), row, row,
                   pl.BlockSpec((8, d), lambda i: (0, 0))),
        compiler_params=_params(("arbitrary",)),
    )(h, g.reshape(1, d), target)


HALO = 16


def _shift_down(cat, k):
    return pltpu.roll(cat, k, axis=0)[HALO:]


def _shift_up(cat, k):
    n = cat.shape[0]
    return pltpu.roll(cat, n - k, axis=0)[:n - HALO]


def _conv_fwd(bcu, cw):
    m, d3 = bcu.shape
    d = d3 // 3
    tr, tc = _pick(m, 416, 16), _pick(d, 512, 128)
    hb = tr // HALO

    def body(x_ref, xb_ref, w_ref, o_ref):
        i = pl.program_id(0)
        for j in range(d // tc):
            col = slice(j * tc, (j + 1) * tc)
            cb, cc, cu = (slice(q * d + j * tc, q * d + (j + 1) * tc) for q in range(3))
            v = x_ref[:, cc].astype(F32) * x_ref[:, cu].astype(F32)
            vh = jnp.where(i > 0, xb_ref[:, cc].astype(F32) * xb_ref[:, cu].astype(F32), 0.0)
            cat = jnp.concatenate([vh, v], axis=0)
            w = w_ref[:, col]
            conv = w[2:3] * v + w[1:2] * _shift_down(cat, 1) + w[0:1] * _shift_down(cat, 2)
            o_ref[:, col] = (x_ref[:, cb].astype(F32) * conv).astype(BF16)

    return pl.pallas_call(
        body, name="conv_fwd", grid=(m // tr,), out_shape=jax.ShapeDtypeStruct((m, d), BF16),
        in_specs=[pl.BlockSpec((tr, d3), lambda i: (i, 0)),
                  pl.BlockSpec((HALO, d3), lambda i: (jnp.maximum(i * hb - 1, 0), 0)),
                  pl.BlockSpec((8, d), lambda i: (0, 0))],
        out_specs=pl.BlockSpec((tr, d), lambda i: (i, 0)),
        compiler_params=_params(("parallel",)),
    )(bcu, bcu, cw)


def _conv_bwd(bcu, cw, dg):
    m, d3 = bcu.shape
    d = d3 // 3
    tr, tc = _pick(m, 208, 16), _pick(d, 512, 128)
    hb, nt = tr // HALO, m // tr

    def body(x_ref, xb_ref, xa_ref, dg_ref, dga_ref, w_ref, o_ref, dw_ref):
        i = pl.program_id(0)

        @pl.when(i == 0)
        def _():
            dw_ref[...] = jnp.zeros_like(dw_ref)

        for j in range(d // tc):
            col = slice(j * tc, (j + 1) * tc)
            cb, cc, cu = (slice(q * d + j * tc, q * d + (j + 1) * tc) for q in range(3))
            w = w_ref[:, col]
            b, c, u = x_ref[:, cb].astype(F32), x_ref[:, cc].astype(F32), x_ref[:, cu].astype(F32)
            dgv = dg_ref[:, col].astype(F32)
            v = c * u
            vh = jnp.where(i > 0, xb_ref[:, cc].astype(F32) * xb_ref[:, cu].astype(F32), 0.0)
            cat = jnp.concatenate([vh, v], axis=0)
            v1, v2 = _shift_down(cat, 1), _shift_down(cat, 2)
            dconv = dgv * b
            o_ref[:, cb] = (dgv * (w[2:3] * v + w[1:2] * v1 + w[0:1] * v2)).astype(BF16)
            taps = [jnp.sum(dconv * t, axis=0, keepdims=True) for t in (v2, v1, v)]
            dw_ref[:, col] += jnp.concatenate(taps + [jnp.zeros((5, tc), F32)], axis=0)
            nxt = jnp.where(i < nt - 1, dga_ref[:, col].astype(F32) * xa_ref[:, cb].astype(F32), 0.0)
            cat2 = jnp.concatenate([dconv, nxt], axis=0)
            dv = w[2:3] * dconv + w[1:2] * _shift_up(cat2, 1) + w[0:1] * _shift_up(cat2, 2)
            o_ref[:, cc] = (dv * u).astype(BF16)
            o_ref[:, cu] = (dv * c).astype(BF16)

    def rows(width):
        return pl.BlockSpec((tr, width), lambda i: (i, 0))

    def before(width):
        return pl.BlockSpec((HALO, width), lambda i: (jnp.maximum(i * hb - 1, 0), 0))

    def after(width):
        return pl.BlockSpec((HALO, width), lambda i: (jnp.minimum((i + 1) * hb, m // HALO - 1), 0))

    return pl.pallas_call(
        body, name="conv_bwd", grid=(nt,),
        out_shape=(jax.ShapeDtypeStruct((m, d3), BF16), jax.ShapeDtypeStruct((8, d), F32)),
        in_specs=[rows(d3), before(d3), after(d3), rows(d), after(d), pl.BlockSpec((8, d), lambda i: (0, 0))],
        out_specs=(rows(d3), pl.BlockSpec((8, d), lambda i: (0, 0))),
        compiler_params=_params(("arbitrary",)),
    )(bcu, bcu, bcu, dg, dg, cw)


PAIR = 2 * HEAD_DIM


def _rope_tables(m):
    pad = ROW0 - N_META
    pos = jnp.arange(m, dtype=F32) - pad
    inv = ROPE_THETA ** (-jnp.arange(0, HEAD_DIM, 2, dtype=F32) / HEAD_DIM)
    ang = pos[:, None] * inv[None, :]
    cos, sin = jnp.cos(ang), jnp.sin(ang)
    return jnp.tile(jnp.concatenate([cos, cos], axis=1), (1, 2)), jnp.tile(jnp.concatenate([-sin, sin], axis=1), (1, 2))


def _rope_pair(x, c, s):
    half = HEAD_DIM // 2
    lane = lax.broadcasted_iota(jnp.int32, x.shape, 1)
    swapped = jnp.where(lane % HEAD_DIM < half, pltpu.roll(x, PAIR - half, axis=1), pltpu.roll(x, half, axis=1))
    return x * c + swapped * s


def _attn_mask(i, rows):
    r = lax.broadcasted_iota(jnp.int32, (rows, 2 * BLOCK), 0) % BLOCK
    cidx = lax.broadcasted_iota(jnp.int32, (rows, 2 * BLOCK), 1)
    key = (i - 1) * BLOCK + cidx
    return (cidx > r) & (cidx <= r + BLOCK) & (key >= ROW0 - N_META)


BAND = 2 * BLOCK


def _rope_qk(qkv, cos, sin):
    m, width = qkv.shape
    scale = HEAD_DIM ** -0.5
    nq, nk = N_Q_HEADS // 2, N_KV_HEADS // 2

    def body(x_ref, c_ref, s_ref, o_ref):
        c, s = c_ref[...], s_ref[...]
        cq, sq = c * scale, s * scale
        for t in range(nq + 2 * nk):
            col = slice(t * PAIR, (t + 1) * PAIR)
            if t < nq:
                o_ref[:, col] = _rope_pair(x_ref[:, col].astype(F32), cq, sq).astype(BF16)
            elif t < nq + nk:
                o_ref[:, col] = _rope_pair(x_ref[:, col].astype(F32), c, s).astype(BF16)
            else:
                o_ref[:, col] = x_ref[:, col]

    row = pl.BlockSpec((BLOCK, width), lambda i: (i, 0))
    tab = pl.BlockSpec((BLOCK, PAIR), lambda i: (i, 0))
    return pl.pallas_call(
        body, name="rope_qk", grid=(m // BLOCK,), out_shape=jax.ShapeDtypeStruct((m, width), BF16),
        in_specs=[row, tab, tab], out_specs=row, compiler_params=_params(("parallel",)),
    )(qkv, cos, sin)


def _pair_rows(ref, h):
    pairs = N_Q_HEADS // N_KV_HEADS // 2
    return jnp.concatenate([ref[:, (h * pairs + g) * PAIR:(h * pairs + g + 1) * PAIR] for g in range(pairs)], axis=0)


def _twice(x):
    z = jnp.zeros_like(x)
    return jnp.concatenate([jnp.concatenate([x, z], axis=1), jnp.concatenate([z, x], axis=1)], axis=0)


def _kv_band(cur_ref, prev_ref, h):
    k0, v0 = N_Q_HEADS * HEAD_DIM + h * HEAD_DIM, (N_Q_HEADS + N_KV_HEADS) * HEAD_DIM + h * HEAD_DIM
    p0, p1 = h * HEAD_DIM, (N_KV_HEADS + h) * HEAD_DIM
    k = jnp.concatenate([prev_ref[:, p0:p0 + HEAD_DIM], cur_ref[:, k0:k0 + HEAD_DIM]], axis=0)
    v = jnp.concatenate([prev_ref[:, p1:p1 + HEAD_DIM], cur_ref[:, v0:v0 + HEAD_DIM]], axis=0)
    return k, v


def _attn2_specs(width):
    kvw = 2 * N_KV_HEADS * HEAD_DIM
    cur = pl.BlockSpec((BLOCK, width), lambda i: (i, 0))
    prev = pl.BlockSpec((BLOCK, kvw), lambda i: (jnp.maximum(i - 1, 0), N_Q_HEADS * HEAD_DIM // kvw))
    return cur, prev


def _attn2_fwd(qkr, sink2):
    m, width = qkr.shape
    nb, rows = m // BLOCK, N_Q_HEADS // N_KV_HEADS // 2 * BLOCK
    dq = N_Q_HEADS * HEAD_DIM

    def body(x_ref, prev_ref, s_ref, o_ref, p_ref):
        allowed = _attn_mask(pl.program_id(0), rows)
        col0 = lax.broadcasted_iota(jnp.int32, (rows, BAND), 1) == 0
        lane = lax.broadcasted_iota(jnp.int32, (rows, PAIR), 1)
        rsel = lax.broadcasted_iota(jnp.int32, (2 * BAND, PAIR), 0) < BAND
        lsel = lax.broadcasted_iota(jnp.int32, (2 * BAND, PAIR), 1) < HEAD_DIM
        ones2 = jnp.where(rsel == lsel, 1.0, 0.0).astype(BF16)
        for h in range(N_KV_HEADS):
            k, v = _kv_band(x_ref, prev_ref, h)
            s2 = lax.dot_general(_pair_rows(x_ref, h), _twice(k), _NT, preferred_element_type=F32)
            sink = s_ref[h]
            e, mx = [], []
            for half in range(2):
                s = jnp.where(allowed, s2[:, half * BAND:(half + 1) * BAND], NEG_INF)
                mx.append(jnp.maximum(jnp.max(s, axis=-1, keepdims=True), sink[:, half * HEAD_DIM:half * HEAD_DIM + 1]))
                e.append(jnp.exp(s - mx[half]).astype(BF16))
            eb2 = jnp.concatenate(e, axis=1)
            es2 = jnp.exp(sink - jnp.where(lane < HEAD_DIM, mx[0], mx[1]))
            ov2 = jnp.dot(eb2, _twice(v), preferred_element_type=F32)
            inv2 = 1.0 / (jnp.dot(eb2, ones2, preferred_element_type=F32) + es2)
            o2 = (ov2 * inv2).astype(BF16)
            ps2 = es2 * inv2
            for g in range(rows // BLOCK):
                col = (h * (rows // BLOCK) + g) * PAIR
                o_ref[:, col:col + PAIR] = o2[g * BLOCK:(g + 1) * BLOCK, :]
            for half in range(2):
                at = half * HEAD_DIM
                p = jnp.where(col0, ps2[:, at:at + 1], e[half].astype(F32) * inv2[:, at:at + 1])
                p_ref[h, :, half * BAND:(half + 1) * BAND] = p.astype(BF16)

    cur, prev = _attn2_specs(width)
    return pl.pallas_call(
        body, name="attn_fwd", grid=(nb,),
        out_shape=(jax.ShapeDtypeStruct((m, dq), BF16), jax.ShapeDtypeStruct((N_KV_HEADS, nb * rows, 2 * BAND), BF16)),
        in_specs=[cur, prev, pl.BlockSpec((N_KV_HEADS, rows, PAIR), lambda i: (0, 0, 0))],
        out_specs=(pl.BlockSpec((BLOCK, dq), lambda i: (i, 0)),
                   pl.BlockSpec((N_KV_HEADS, rows, 2 * BAND), lambda i: (0, i, 0))),
        compiler_params=_params(("parallel",)),
    )(qkr, qkr, sink2)


def _attn2_bwd(qkr, p, o, do):
    m, width = qkr.shape
    nb, rows = m // BLOCK, N_Q_HEADS // N_KV_HEADS // 2 * BLOCK
    dq = N_Q_HEADS * HEAD_DIM

    def body(x_ref, prev_ref, p_ref, o_ref, do_ref, dq_ref, dk_ref, dv_ref, ds_ref):
        colz = lax.broadcasted_iota(jnp.int32, (rows, 2 * BAND), 1) % BAND == 0
        even = lax.broadcasted_iota(jnp.int32, (rows, PAIR), 1) < HEAD_DIM

        @pl.when(pl.program_id(0) == 0)
        def _():
            ds_ref[...] = jnp.zeros_like(ds_ref)

        for h in range(N_KV_HEADS):
            k, v = _kv_band(x_ref, prev_ref, h)
            k2, v2 = _twice(k), _twice(v)
            q2, do2, pv = _pair_rows(x_ref, h), _pair_rows(do_ref, h), p_ref[h]
            prod = do2.astype(F32) * _pair_rows(o_ref, h).astype(F32)
            delta = [jnp.sum(jnp.where(even, prod, 0.0), axis=-1, keepdims=True),
                     jnp.sum(jnp.where(even, 0.0, prod), axis=-1, keepdims=True)]
            dp2 = lax.dot_general(do2, v2, _NT, preferred_element_type=F32)
            pb = jnp.where(colz, jnp.zeros_like(pv), pv)
            ds = [(pb[:, half * BAND:(half + 1) * BAND].astype(F32)
                   * (dp2[:, half * BAND:(half + 1) * BAND] - delta[half])).astype(BF16) for half in range(2)]
            dsb2 = jnp.concatenate(ds, axis=1)
            dq2 = jnp.dot(dsb2, k2, preferred_element_type=F32).astype(BF16)
            for g in range(rows // BLOCK):
                col = (h * (rows // BLOCK) + g) * PAIR
                dq_ref[:, col:col + PAIR] = dq2[g * BLOCK:(g + 1) * BLOCK, :]
            dkk = lax.dot_general(q2, dsb2, _TN, preferred_element_type=F32)
            dk_ref[h] = (dkk[:HEAD_DIM, :BAND] + dkk[HEAD_DIM:, BAND:]).T
            dvv = lax.dot_general(do2, pb, _TN, preferred_element_type=F32)
            dv_ref[h] = (dvv[:HEAD_DIM, :BAND] + dvv[HEAD_DIM:, BAND:]).T
            ds_ref[h] -= jnp.where(even, pv[:, 0:1].astype(F32) * delta[0], pv[:, BAND:BAND + 1].astype(F32) * delta[1])

    cur, prev = _attn2_specs(width)
    heads = pl.BlockSpec((BLOCK, dq), lambda i: (i, 0))
    band = pl.BlockSpec((N_KV_HEADS, None, BAND, HEAD_DIM), lambda i: (0, i, 0, 0))
    band_shape = jax.ShapeDtypeStruct((N_KV_HEADS, nb, BAND, HEAD_DIM), F32)
    sink = pl.BlockSpec((N_KV_HEADS, rows, PAIR), lambda i: (0, 0, 0))
    return pl.pallas_call(
        body, name="attn_bwd", grid=(nb,),
        out_shape=(jax.ShapeDtypeStruct((m, dq), BF16), band_shape, band_shape,
                   jax.ShapeDtypeStruct((N_KV_HEADS, rows, PAIR), F32)),
        in_specs=[cur, prev, pl.BlockSpec((N_KV_HEADS, rows, 2 * BAND), lambda i: (0, i, 0)), heads, heads],
        out_specs=(heads, band, band, sink), compiler_params=_params(("arbitrary",)),
    )(qkr, qkr, p, o, do)


def _rope_qk_bwd(dq, dkb, dvb, cos, sin):
    nb = dkb.shape[1]
    width = (N_Q_HEADS + 2 * N_KV_HEADS) * HEAD_DIM
    scale = HEAD_DIM ** -0.5
    nq, nk = N_Q_HEADS // 2, N_KV_HEADS // 2

    def body(dq_ref, kc_ref, kn_ref, vc_ref, vn_ref, c_ref, s_ref, o_ref):
        last = pl.program_id(0) == nb - 1
        c, s = c_ref[...], -s_ref[...]

        def band_sum(cur_ref, nxt_ref, t):
            return jnp.concatenate([cur_ref[2 * t + e, BLOCK:, :] + jnp.where(last, 0.0, nxt_ref[2 * t + e, :BLOCK, :])
                                    for e in range(2)], axis=1)

        cq, sq = c * scale, s * scale
        for t in range(nq):
            col = slice(t * PAIR, (t + 1) * PAIR)
            o_ref[:, col] = _rope_pair(dq_ref[:, col].astype(F32), cq, sq).astype(BF16)
        for t in range(nk):
            o_ref[:, (nq + t) * PAIR:(nq + t + 1) * PAIR] = _rope_pair(band_sum(kc_ref, kn_ref, t), c, s).astype(BF16)
            o_ref[:, (nq + nk + t) * PAIR:(nq + nk + t + 1) * PAIR] = band_sum(vc_ref, vn_ref, t).astype(BF16)

    tab = pl.BlockSpec((BLOCK, PAIR), lambda i: (i, 0))
    cur = pl.BlockSpec((N_KV_HEADS, None, BAND, HEAD_DIM), lambda i: (0, i, 0, 0))
    nxt = pl.BlockSpec((N_KV_HEADS, None, BAND, HEAD_DIM), lambda i: (0, jnp.minimum(i + 1, nb - 1), 0, 0))
    return pl.pallas_call(
        body, name="rope_qk_bwd", grid=(nb,), out_shape=jax.ShapeDtypeStruct((nb * BLOCK, width), BF16),
        in_specs=[pl.BlockSpec((BLOCK, N_Q_HEADS * HEAD_DIM), lambda i: (i, 0)), cur, nxt, cur, nxt, tab, tab],
        out_specs=pl.BlockSpec((BLOCK, width), lambda i: (i, 0)),
        compiler_params=_params(("parallel",)),
    )(dq, dkb, dkb, dvb, dvb, cos, sin)


def _tiles2d(r, c):
    tc = _pick(c, 2048, 128) if c % 128 == 0 else c
    tr = _pick(r, max(8, (1 << 20) // tc // 8 * 8), 8) if r % 8 == 0 else r
    return tr, tc


def _cast_bf16(name, w, place, wide):
    r, c = w.shape
    tr, tc = _tiles2d(r, c)
    if tr % 16:
        tr = r
    nc = c // tc

    def body(place_ref, w_ref, o_ref):
        o_ref[...] = w_ref[...].astype(BF16)

    if wide:
        out_shape = jax.ShapeDtypeStruct((r, N_CHIPS * c), BF16)
        out_spec = pl.BlockSpec((tr, tc), lambda i, j, p: (i, p[1] * nc + j))
    else:
        out_shape = jax.ShapeDtypeStruct((N_CHIPS, r, c), BF16)
        out_spec = pl.BlockSpec((None, tr, tc), lambda i, j, p: (p[1], i, j))
    return pl.pallas_call(
        body, name=name, out_shape=out_shape,
        grid_spec=pltpu.PrefetchScalarGridSpec(
            num_scalar_prefetch=1, grid=(r // tr, nc),
            in_specs=[pl.BlockSpec((tr, tc), lambda i, j, p: (i, j))], out_specs=out_spec),
        compiler_params=_params(("parallel", "parallel")),
    )(place, w)


def _pair_sum(name, g, got, place):
    n, r, c = g.shape
    half = r // 2
    tr, tc = _tiles2d(half, c)
    nh = half // tr

    def body(place_ref, g_ref, got_ref, o_ref, own_ref):
        s = (g_ref[...] + got_ref[...]).astype(BF16)
        o_ref[...] = s

        @pl.when(pl.program_id(2) == place_ref[1])
        def _():
            own_ref[...] = s

    tile = pl.BlockSpec((None, tr, tc), lambda i, j, k, p: (k, i, j))
    shape = jax.ShapeDtypeStruct((n, half, c), BF16)
    return pl.pallas_call(
        body, name=name, out_shape=(shape, shape),
        grid_spec=pltpu.PrefetchScalarGridSpec(
            num_scalar_prefetch=1, grid=(nh, c // tc, n),
            in_specs=[pl.BlockSpec((None, tr, tc), lambda i, j, k, p: (k, p[0] * nh + i, j)), tile],
            out_specs=(tile, pl.BlockSpec((None, tr, tc), lambda i, j, k, p: (p[1], i, j)))),
        compiler_params=_params(("parallel", "parallel", "arbitrary")),
    )(place, g, got)


def _chip_sum(name, parts, place):
    n, half, c = parts.shape
    tr, tc = _tiles2d(half, c)
    nh = half // tr

    def body(place_ref, p0, p1, p2, p3, o_ref):
        o_ref[...] = ((p0[...].astype(F32) + p1[...].astype(F32)) + p2[...].astype(F32)) + p3[...].astype(F32)

    def chip(k):
        return pl.BlockSpec((None, tr, tc), lambda i, j, p: (k, i, j))

    return pl.pallas_call(
        body, name=name, out_shape=jax.ShapeDtypeStruct((2 * half, c), F32),
        grid_spec=pltpu.PrefetchScalarGridSpec(
            num_scalar_prefetch=1, grid=(nh, c // tc),
            in_specs=[chip(k) for k in range(n)],
            out_specs=pl.BlockSpec((tr, tc), lambda i, j, p: (p[0] * nh + i, j))),
        compiler_params=_params(("parallel", "parallel")),
    )(place, parts, parts, parts, parts)


def _dev_sum(gathered):
    def body(g_ref, o_ref):
        acc = g_ref[0]
        for k in range(1, N_DEV):
            acc = acc + g_ref[k]
        o_ref[...] = acc

    return pl.pallas_call(body, name="dev_sum", out_shape=jax.ShapeDtypeStruct(gathered.shape[1:], F32))(gathered)


def _adamw(name, w, g, m, v):
    r, c = w.shape
    tr, tc = _tiles2d(r, c)
    if r % 8 == 0:
        tr = _pick(r, max(8, (1 << 18) // tc // 8 * 8), 8)

    def body(w_ref, g_ref, m_ref, v_ref, d_ref, mo_ref, vo_ref):
        gv = g_ref[...]
        mn = ADAM_B1 * m_ref[...] + (1.0 - ADAM_B1) * gv
        vn = ADAM_B2 * v_ref[...] + (1.0 - ADAM_B2) * jnp.square(gv)
        m_hat = mn / (1.0 - ADAM_B1 ** ADAM_STEP)
        v_hat = vn / (1.0 - ADAM_B2 ** ADAM_STEP)
        d_ref[...] = -ADAM_LR * (m_hat / (jnp.sqrt(v_hat) + ADAM_EPS) + ADAM_WD * w_ref[...])
        mo_ref[...] = mn
        vo_ref[...] = vn

    tile = pl.BlockSpec((tr, tc), lambda i, j: (i, j))
    shape = jax.ShapeDtypeStruct((r, c), F32)
    return pl.pallas_call(
        body, name=name, grid=(r // tr, c // tc), out_shape=(shape, shape, shape),
        in_specs=[tile] * 4, out_specs=(tile,) * 3, compiler_params=_params(("parallel", "parallel")),
    )(w, g, m, v)


MATRICES = ("w_in_conv", "w_out_conv", "w_up_0", "w_down_0", "w_qkv", "w_o", "w_up_1", "w_down_1")
COLUMN_SHARDED = ("w_in_conv", "w_up_0", "w_qkv", "w_up_1")
NORMS = ("norm_mix_0", "norm_mlp_0", "norm_mix_1", "norm_mlp_1", "norm_final")


def _rows(stack):
    return stack.reshape(N_CHIPS * stack.shape[1], stack.shape[2])


def _stack(full):
    return full.reshape(N_CHIPS, full.shape[0] // N_CHIPS, full.shape[1])


def _add_residual(acc, res):
    return acc + res


def _relu_sq_grad(acc, z):
    return acc * (2.0 * jnp.maximum(z.astype(F32), 0.0))


def _step(x, target, stacks, small, norms, sinks, place):
    d = D_MODEL
    dc = d // N_CHIPS
    pad = ROW0 - N_META
    m = x.shape[0] + ROW0
    grp = N_Q_HEADS // N_KV_HEADS
    cos, sin = _rope_tables(m)
    pairs = grp // 2
    sink2 = jnp.broadcast_to(sinks.astype(F32).reshape(N_KV_HEADS, pairs, 1, 2, 1),
                             (N_KV_HEADS, pairs, BLOCK, 2, HEAD_DIM)).reshape(N_KV_HEADS, pairs * BLOCK, PAIR)

    def gather(*names):
        return _gather_task([stacks[n] for n in names])

    def pair_sum(tag, grad, got):
        return _pair_sum("pair_sum_" + tag, grad, got, place)

    def chip_sum(tag, landed):
        return _chip_sum("chip_sum_" + tag, landed, place)

    (w_in, small_all), = _run("gather_first", [_gather_task([stacks["w_in_conv"], small])])
    small_full = jnp.transpose(small_all, (1, 0, 2)).reshape(SMALL_ROWS, d)
    conv_w8 = small_full[N_META:N_META + 8]
    h0 = jnp.concatenate([jnp.zeros((pad, d), F32), small_full[:N_META], x], axis=0)

    n0 = _rms_fwd("norm_mix_0", h0, norms["norm_mix_0"])
    bcu, ((w_out, w_up0),) = _mm_nn("conv_in", n0, w_in, BF16, tasks=[gather("w_out_conv", "w_up_0")])
    gate = _conv_fwd(bcu, conv_w8)
    h1 = _mm_nn("conv_out", gate, _rows(w_out), F32, epi=_add_residual, extras=(h0,))
    n1 = _rms_fwd("norm_mlp_0", h1, norms["norm_mlp_0"])
    z0, ((w_down0,),) = _mm_nn("mlp_up_0", n1, w_up0, BF16, tasks=[gather("w_down_0")])
    h2, ((w_qkv, w_o, w_up1),) = _mm_nn("mlp_down_0", z0, _rows(w_down0), F32, a_pro=_relu_sq, epi=_add_residual,
                                             extras=(h1,), tasks=[gather("w_qkv", "w_o", "w_up_1")])
    n2 = _rms_fwd("norm_mix_1", h2, norms["norm_mix_1"])
    qkv = _mm_nn("attn_qkv", n2, w_qkv, BF16)
    qkr = _rope_qk(qkv, cos, sin)
    o, probs = _attn2_fwd(qkr, sink2)
    h3 = _mm_nn("attn_out", o, _rows(w_o), F32, epi=_add_residual, extras=(h2,))
    n3 = _rms_fwd("norm_mlp_1", h3, norms["norm_mlp_1"])
    z1, ((w_down1,),) = _mm_nn("mlp_up_1", n3, w_up1, BF16, tasks=[gather("w_down_1")])
    h4 = _mm_nn("mlp_down_1", z1, _rows(w_down1), F32, a_pro=_relu_sq, epi=_add_residual, extras=(h3,))

    gn = {}
    loss, dh, dh_bf, gn["norm_final"] = _loss_head(h4, norms["norm_final"], target)
    dz = _mm_nt("mlp_down_dx_1", dh_bf, _rows(w_down1), BF16, epi=_relu_sq_grad, extras=(z1,))
    g_d1 = _stack(_mm_tn("mlp_down_dw_1", z1, dh_bf, stacked=False, a_pro=_relu_sq))
    g_u1, ((got,),) = _mm_tn("mlp_up_dw_1", n3, dz, stacked=True, tasks=[_pair_exchange_task([g_d1])])
    s_d1 = pair_sum("d1", g_d1, got)
    dn, ((got,), (landed,)) = _mm_nt("mlp_up_dx_1", dz, w_up1, BF16,
                                          tasks=[_pair_exchange_task([g_u1]), _chip_exchange_task([s_d1])])
    s_u1, b_d1 = pair_sum("u1", g_u1, got), chip_sum("d1", landed)
    dh, dh_bf, gn["norm_mlp_1"] = _rms_bwd("norm_mlp_bwd_1", dn, h3, norms["norm_mlp_1"], dh)
    do = _mm_nt("attn_out_dx", dh_bf, _rows(w_o), BF16)
    g_o = _stack(_mm_tn("attn_out_dw", o, dh_bf, stacked=False))
    dq, dkb, dvb, dsink = _attn2_bwd(qkr, probs, o, do)
    dqkv = _rope_qk_bwd(dq, dkb, dvb, cos, sin)
    g_qkv, ((got,),) = _mm_tn("attn_qkv_dw", n2, dqkv, stacked=True, tasks=[_pair_exchange_task([g_o])])
    s_o = pair_sum("o", g_o, got)
    dn, ((got,), (landed,)) = _mm_nt("attn_qkv_dx", dqkv, w_qkv, BF16,
                                          tasks=[_pair_exchange_task([g_qkv]), _chip_exchange_task([s_u1])])
    s_qkv, b_u1 = pair_sum("qkv", g_qkv, got), chip_sum("u1", landed)
    dh, dh_bf, gn["norm_mix_1"] = _rms_bwd("norm_mix_bwd_1", dn, h2, norms["norm_mix_1"], dh)
    dz, ((landed_o, landed_qkv), (r_d1,)) = _mm_nt(
        "mlp_down_dx_0", dh_bf, _rows(w_down0), BF16, epi=_relu_sq_grad, extras=(z0,),
        tasks=[_chip_exchange_task([s_o, s_qkv]), _pair_share_task([b_d1])])
    b_o, b_qkv = chip_sum("o", landed_o), chip_sum("qkv", landed_qkv)
    g_d0, ((r_u1,),) = _mm_tn("mlp_down_dw_0", z0, dh_bf, stacked=False, a_pro=_relu_sq, tasks=[_pair_share_task([b_u1])])
    g_d0 = _stack(g_d0)
    g_u0, ((got,), (r_o, r_qkv)) = _mm_tn("mlp_up_dw_0", n1, dz, stacked=True,
                                          tasks=[_pair_exchange_task([g_d0]), _pair_share_task([b_o, b_qkv])])
    s_d0 = pair_sum("d0", g_d0, got)
    dn, ((got,), (landed,)) = _mm_nt("mlp_up_dx_0", dz, w_up0, BF16,
                                          tasks=[_pair_exchange_task([g_u0]), _chip_exchange_task([s_d0])])
    s_u0, b_d0 = pair_sum("u0", g_u0, got), chip_sum("d0", landed)
    dh, dh_bf, gn["norm_mlp_0"] = _rms_bwd("norm_mlp_bwd_0", dn, h1, norms["norm_mlp_0"], dh)
    dgate = _mm_nt("conv_out_dx", dh_bf, _rows(w_out), BF16)
    dbcu, g_conv_w = _conv_bwd(bcu, conv_w8, dgate)
    g_in, ((landed,), (r_d0,)) = _mm_tn("conv_in_dw", n0, dbcu, stacked=True,
                                        tasks=[_chip_exchange_task([s_u0]), _pair_share_task([b_d0])])
    b_u0 = chip_sum("u0", landed)
    dn, ((got,), (r_u0,)) = _mm_nt("conv_in_dx", dbcu, w_in, BF16,
                                        tasks=[_pair_exchange_task([g_in]), _pair_share_task([b_u0])])
    s_in = pair_sum("in", g_in, got)
    grad_x, dh_first, gn["norm_mix_0"] = _rms_bwd_tokens("norm_mix_bwd_0", dn, h0, norms["norm_mix_0"], dh)

    g_small = jnp.zeros((SMALL_ROWS, d), F32).at[:N_META].set(dh_first[pad:ROW0]).at[N_META:N_META + 8].set(g_conv_w)
    g_small = jnp.transpose(g_small.reshape(SMALL_ROWS, N_CHIPS, dc), (1, 0, 2))
    rep = jnp.zeros((8, d), F32)
    for r, n in enumerate(NORMS):
        rep = rep.at[r].set(jnp.sum(gn[n], axis=0))
    dsink = jnp.sum(dsink.reshape(N_KV_HEADS, pairs, BLOCK, 2, HEAD_DIM)[..., 0], axis=2)
    rep = rep.at[len(NORMS), :N_Q_HEADS].set(dsink.reshape(N_Q_HEADS))
    g_out, ((landed,), (got, rep_all)) = _mm_tn(
        "conv_out_dw", gate, dh_bf, stacked=False,
        tasks=[_chip_exchange_task([s_in]), _pair_exchange_task([g_small], small=rep)])
    g_out = _stack(g_out)
    b_in, s_small = chip_sum("in", landed), pair_sum("small", g_small, got)
    (got,), = _run("tail_pair_exchange", [_pair_exchange_task([g_out])])
    s_out = pair_sum("out", g_out, got)
    (landed_out, landed_small), = _run("tail_chip_exchange", [_chip_exchange_task([s_out, s_small])])
    b_out, b_small = chip_sum("out", landed_out), chip_sum("small", landed_small)
    (r_out, r_small, r_in), = _run("tail_pair_share", [_pair_share_task([b_out, b_small, b_in])])

    reduced = {"w_in_conv": r_in, "w_out_conv": r_out, "w_up_0": r_u0, "w_down_0": r_d0, "w_qkv": r_qkv, "w_o": r_o,
               "w_up_1": r_u1, "w_down_1": r_d1}
    return loss, grad_x, reduced, r_small, rep_all


def kernel(x, meta_tokens, norm_mix_0, w_in_conv, conv_w, w_out_conv, norm_mlp_0, w_up_0, w_down_0, norm_mix_1, w_qkv, attn_sinks, w_o, norm_mlp_1, w_up_1, w_down_1, norm_final, loss_target, m_meta_tokens, m_norm_mix_0, m_w_in_conv, m_conv_w, m_w_out_conv, m_norm_mlp_0, m_w_up_0, m_w_down_0, m_norm_mix_1, m_w_qkv, m_attn_sinks, m_w_o, m_norm_mlp_1, m_w_up_1, m_w_down_1, m_norm_final, v_meta_tokens, v_norm_mix_0, v_w_in_conv, v_conv_w, v_w_out_conv, v_norm_mlp_0, v_w_up_0, v_w_down_0, v_norm_mix_1, v_w_qkv, v_attn_sinks, v_w_o, v_norm_mlp_1, v_w_up_1, v_w_down_1, v_norm_final):
    given = dict(locals())
    names = ("meta_tokens", "norm_mix_0", "w_in_conv", "conv_w", "w_out_conv", "norm_mlp_0", "w_up_0", "w_down_0",
             "norm_mix_1", "w_qkv", "attn_sinks", "w_o", "norm_mlp_1", "w_up_1", "w_down_1", "norm_final")
    d = D_MODEL
    dc = d // N_CHIPS
    chip = 2 * lax.axis_index("x") + lax.axis_index("y")
    place = jnp.stack([lax.axis_index("c"), chip]).astype(jnp.int32)

    small = jnp.zeros((SMALL_ROWS, dc), F32).at[:N_META].set(meta_tokens).at[N_META:N_META + CONV_WIDTH].set(conv_w)
    small = lax.dynamic_update_slice(jnp.zeros((N_CHIPS, SMALL_ROWS, dc), F32), small[None], (chip, 0, 0))
    stacks = {n: _cast_bf16("cast_" + n, given[n], place, n in COLUMN_SHARDED) for n in MATRICES}

    norms = {n: given[n] for n in NORMS}
    loss_part, grad_x, g_out, r_small, rep_all = _step(x[0], loss_target[0], stacks, small, norms, attn_sinks, place)
    loss = lax.psum(loss_part[0, 0], ("x", "y", "c"))
    grad_x = grad_x[None]
    rep_sum = _dev_sum(rep_all)
    g_out["meta_tokens"] = r_small[:N_META]
    g_out["conv_w"] = r_small[N_META:N_META + CONV_WIDTH]
    for r, n in enumerate(NORMS):
        g_out[n] = rep_sum[r]
    g_out["attn_sinks"] = rep_sum[len(NORMS), :N_Q_HEADS]

    delta, new_m, new_v = {}, {}, {}
    for n in names:
        wt = given[n]
        shape2 = wt.shape if wt.ndim == 2 else (1, wt.shape[0])
        outs = _adamw("adamw_" + n, wt.reshape(shape2), g_out[n].reshape(shape2),
                      given["m_" + n].reshape(shape2), given["v_" + n].reshape(shape2))
        delta[n], new_m[n], new_v[n] = [o.reshape(wt.shape) for o in outs]
    return (loss, grad_x, *[g_out[n] for n in names], *[delta[n] for n in names],
            *[new_m[n] for n in names], *[new_v[n] for n in names])
```

```python
import jax
import jax.numpy as jnp
from jax import lax
from jax.experimental import pallas as pl
from jax.experimental.pallas import tpu as pltpu

F32 = jnp.float32
BF16 = jnp.bfloat16

D_MODEL = 2048
SEQ = 8192
N_META = 16
CONV_WIDTH = 3
HEAD_DIM = 64
N_Q_HEADS = 32
N_KV_HEADS = 4
BLOCK = 128
ROPE_THETA = 10000.0
D_FF = 4 * D_MODEL
RMS_EPS = 1e-5
NEG_INF = -1e30

ADAM_LR = 0.001
ADAM_B1 = 0.9
ADAM_B2 = 0.999
ADAM_EPS = 1e-08
ADAM_WD = 0.01
ADAM_STEP = 10

N_CHIPS = 4
N_DEV = 8
MESH = pl.DeviceIdType.MESH
VMEM_LIMIT = 56 * 1024 * 1024
SMALL_ROWS = 32
ROW0 = BLOCK


def _pick(n, target, mult):
    best = None
    for t in range(mult, min(n, target) + 1, mult):
        if n % t == 0:
            best = t
    assert best is not None, (n, target, mult)
    return best


def _params(sem=None):
    return pltpu.CompilerParams(dimension_semantics=sem, vmem_limit_bytes=VMEM_LIMIT)


HBM_SPEC = pl.BlockSpec(memory_space=pltpu.HBM)


class _Task:
    def __init__(self, inputs, outputs, aliases, sem_shapes, bind):
        self.inputs, self.outputs, self.aliases = list(inputs), list(outputs), dict(aliases)
        self.sem_shapes, self.bind = list(sem_shapes), bind


def _like(arrays):
    return [jax.ShapeDtypeStruct(a.shape, a.dtype) for a in arrays]


def _bind_tasks(tasks, in_refs, out_refs, sem_refs):
    bound, i, o, s = [], 0, 0, 0
    for t in tasks:
        ni, no, ns = len(t.inputs), len(t.outputs), len(t.sem_shapes)
        bound.append(t.bind(in_refs[i:i + ni], out_refs[o:o + no], sem_refs[s:s + ns]))
        i, o, s = i + ni, o + no, s + ns
    return bound


def _run_phase(bound, phase):
    for b in bound:
        if b[phase] is not None:
            b[phase]()


def _task_plumbing(tasks, in_offset, out_offset):
    ins = [a for t in tasks for a in t.inputs]
    outs = [o for t in tasks for o in t.outputs]
    sems = [s for t in tasks for s in t.sem_shapes]
    aliases, i, o = {}, in_offset, out_offset
    for t in tasks:
        for src, dst in t.aliases.items():
            aliases[i + src] = o + dst
        i, o = i + len(t.inputs), o + len(t.outputs)
    return ins, outs, sems, aliases


def _split_outputs(tasks, flat):
    res, o = [], 0
    for t in tasks:
        res.append(list(flat[o:o + len(t.outputs)]))
        o += len(t.outputs)
    return res


def _run(name, tasks):
    ins, outs, sems, aliases = _task_plumbing(tasks, 0, 0)

    def body(*refs):
        bound = _bind_tasks(tasks, refs[:len(ins)], refs[len(ins):len(ins) + len(outs)], refs[len(ins) + len(outs):])
        for phase in range(3):
            _run_phase(bound, phase)

    flat = pl.pallas_call(
        body, name=name, out_shape=outs, in_specs=[HBM_SPEC] * len(ins), out_specs=[HBM_SPEC] * len(outs),
        input_output_aliases=aliases, scratch_shapes=sems,
    )(*ins)
    return _split_outputs(tasks, flat)


def _place():
    x, y, c = lax.axis_index("x"), lax.axis_index("y"), lax.axis_index("c")
    chips = [(1 - x, y), (x, 1 - y), (1 - x, 1 - y)]
    return x, y, c, 2 * x + y, chips


def _gather_task(stacks):
    n = len(stacks)
    halves = [s.shape[-2] // 2 for s in stacks]

    def bind(_, dst, sems):
        send_a, recv_a, send_b, recv_b = sems
        x, y, c, me, chips = _place()
        sibling = (x, y, 1 - c)

        def half(w, chip, hc):
            rows = pl.ds(hc * halves[w], halves[w])
            if len(stacks[w].shape) == 3:
                return dst[w].at[chip, rows, :]
            cols = stacks[w].shape[1] // N_CHIPS
            return dst[w].at[rows, pl.ds(chip * cols, cols)]

        def over_ici(j, w, block):
            return pltpu.make_async_remote_copy(
                src_ref=half(w, block, c), dst_ref=half(w, block, c), send_sem=send_a.at[j * n + w],
                recv_sem=recv_a.at[j * n + w], device_id=(*chips[j], c), device_id_type=MESH)

        def over_d2d(j, w, hc):
            got = half(w, 2 * chips[j][0] + chips[j][1], hc)
            return pltpu.make_async_remote_copy(
                src_ref=got, dst_ref=got, send_sem=send_b.at[j * n + w], recv_sem=recv_b.at[j * n + w],
                device_id=sibling, device_id_type=MESH)

        pairs = [(j, w) for j in range(3) for w in range(n)]

        def start():
            for j, w in pairs:
                over_ici(j, w, me).start()

        def mid():
            for j, w in pairs:
                over_ici(j, w, 2 * chips[j][0] + chips[j][1]).wait_recv()
                over_d2d(j, w, c).start()

        def finish():
            for j, w in pairs:
                over_d2d(j, w, 1 - c).wait_recv()
            for j, w in pairs:
                over_ici(j, w, me).wait_send()
                over_d2d(j, w, c).wait_send()

        return start, mid, finish

    return _Task(stacks, _like(stacks), {w: w for w in range(n)}, [pltpu.SemaphoreType.DMA((3 * n,))] * 4, bind)


def _pair_exchange_task(grads, small=None):
    n = len(grads)
    halves = [g.shape[1] // 2 for g in grads]

    def bind(src, dst, sems):
        send, recv = sems[0], sems[1]
        x, y, c, me, _ = _place()
        sibling = (x, y, 1 - c)
        dev = 2 * me + c

        def to_sibling(w):
            return pltpu.make_async_remote_copy(
                src_ref=src[w].at[:, pl.ds((1 - c) * halves[w], halves[w]), :], dst_ref=dst[w],
                send_sem=send.at[w], recv_sem=recv.at[w], device_id=sibling, device_id_type=MESH)

        def to_peer(t, block):
            tx, ty, tc = (t >> 2) & 1, (t >> 1) & 1, t & 1
            return pltpu.make_async_remote_copy(
                src_ref=src[n], dst_ref=dst[n].at[block], send_sem=sems[2].at[t], recv_sem=sems[3].at[t],
                device_id=(x ^ tx, y ^ ty, c ^ tc), device_id_type=MESH)

        def mine():
            return pltpu.make_async_copy(src[n], dst[n].at[dev], sems[4])

        def start():
            for w in range(n):
                to_sibling(w).start()
            if small is not None:
                mine().start()
                for t in range(1, N_DEV):
                    to_peer(t, dev).start()

        def finish():
            for w in range(n):
                to_sibling(w).wait_recv()
            if small is not None:
                for t in range(1, N_DEV):
                    to_peer(t, dev ^ t).wait_recv()
            for w in range(n):
                to_sibling(w).wait_send()
            if small is not None:
                for t in range(1, N_DEV):
                    to_peer(t, dev).wait_send()
                mine().wait()

        return start, None, finish

    outputs = [jax.ShapeDtypeStruct((N_CHIPS, h, g.shape[2]), g.dtype) for g, h in zip(grads, halves)]
    sem_shapes = [pltpu.SemaphoreType.DMA((n,)), pltpu.SemaphoreType.DMA((n,))]
    inputs = list(grads)
    if small is not None:
        inputs.append(small)
        outputs.append(jax.ShapeDtypeStruct((N_DEV,) + small.shape, small.dtype))
        sem_shapes += [pltpu.SemaphoreType.DMA((N_DEV,)), pltpu.SemaphoreType.DMA((N_DEV,)), pltpu.SemaphoreType.DMA(())]
    return _Task(inputs, outputs, {}, sem_shapes, bind)


def _chip_exchange_task(summed, part=(0, 1)):
    n = len(summed)

    def bind(refs, dst, sems):
        src = refs[:n]
        send, recv = sems
        x, y, c, me, chips = _place()

        def copy(j, w, block_from, block_to):
            size = summed[w][0].shape[1] // part[1]
            rows = pl.ds(part[0] * size, size)
            return pltpu.make_async_remote_copy(
                src_ref=src[w].at[block_from, rows], dst_ref=dst[w].at[block_to, rows], send_sem=send.at[j * n + w],
                recv_sem=recv.at[j * n + w], device_id=(*chips[j], c), device_id_type=MESH)

        pairs = [(j, w) for j in range(3) for w in range(n)]

        def start():
            for j, w in pairs:
                copy(j, w, 2 * chips[j][0] + chips[j][1], me).start()

        def finish():
            for j, w in pairs:
                copy(j, w, me, 2 * chips[j][0] + chips[j][1]).wait_recv()
            for j, w in pairs:
                copy(j, w, 2 * chips[j][0] + chips[j][1], me).wait_send()

        return start, None, finish

    partials, landing = [s[0] for s in summed], [s[1] for s in summed]
    return _Task(partials + landing, _like(landing), {n + w: w for w in range(n)},
                 [pltpu.SemaphoreType.DMA((3 * n,))] * 2, bind)


def _pair_share_task(blocks):
    n = len(blocks)

    def bind(_, dst, sems):
        send, recv = sems
        x, y, c, _, _ = _place()

        def copy(w, hc):
            h = blocks[w].shape[0] // 2
            rows = dst[w].at[pl.ds(hc * h, h), :]
            return pltpu.make_async_remote_copy(src_ref=rows, dst_ref=rows, send_sem=send.at[w], recv_sem=recv.at[w],
                                                device_id=(x, y, 1 - c), device_id_type=MESH)

        def start():
            for w in range(n):
                copy(w, c).start()

        def finish():
            for w in range(n):
                copy(w, 1 - c).wait_recv()
            for w in range(n):
                copy(w, c).wait_send()

        return start, None, finish

    return _Task(blocks, _like(blocks), {w: w for w in range(n)}, [pltpu.SemaphoreType.DMA((n,))] * 2, bind)


def _carrier_call(name, body, operands, *, grid, in_specs, out_specs, out_shape, semantics, tasks):
    n_in, n_out = len(operands), len(out_shape)
    t_ins, t_outs, t_sems, aliases = _task_plumbing(tasks, n_in, n_out)
    n_ti, n_to = len(t_ins), len(t_outs)
    total = 1
    for g in grid:
        total *= g
    mid_step = max(0, total - 1 - max(1, total // 8))

    def carrier(*refs):
        outs_at = n_in + n_ti
        if tasks:
            bound = _bind_tasks(tasks, refs[n_in:outs_at], refs[outs_at + n_out:outs_at + n_out + n_to],
                                refs[outs_at + n_out + n_to:])
            step = 0
            for axis, g in enumerate(grid):
                step = step * g + pl.program_id(axis)

            @pl.when(step == 0)
            def _():
                _run_phase(bound, 0)

        body(*refs[:n_in], *refs[outs_at:outs_at + n_out])

        if tasks:
            @pl.when(step == mid_step)
            def _():
                _run_phase(bound, 1)

            @pl.when(step == total - 1)
            def _():
                _run_phase(bound, 2)

    res = pl.pallas_call(
        carrier, name=name, grid=grid, out_shape=[*out_shape, *t_outs],
        in_specs=[*in_specs, *[HBM_SPEC] * n_ti], out_specs=[*out_specs, *[HBM_SPEC] * n_to],
        input_output_aliases=aliases, scratch_shapes=t_sems,
        compiler_params=_params(("arbitrary",) * len(grid) if tasks else semantics),
    )(*operands, *t_ins)
    return list(res[:n_out]), _split_outputs(tasks, res[n_out:])


def _mm(name, a, b, *, dims, grid, a_spec, b_spec, out_shape, out_spec,
        extras=(), extra_specs=(), a_pro=None, epi=None, tasks=()):
    def body(a_ref, b_ref, *rest):
        av = a_ref[...]
        if a_pro is not None:
            av = a_pro(av)
        acc = lax.dot_general(av, b_ref[...], dims, preferred_element_type=F32)
        if epi is not None:
            acc = epi(acc, *[e[...] for e in rest[:-1]])
        rest[-1][...] = acc.astype(rest[-1].dtype)

    res, carried = _carrier_call(name, body, [a, b, *extras], grid=grid, in_specs=[a_spec, b_spec, *extra_specs],
                                 out_specs=[out_spec], out_shape=[out_shape], semantics=("parallel", "parallel"),
                                 tasks=tasks)
    return (res[0], carried) if tasks else res[0]


_NN = (((1,), (0,)), ((), ()))
_NT = (((1,), (1,)), ((), ()))
_TN = (((0,), (0,)), ((), ()))


MM_TILE_BUDGET = 46 * 1024 * 1024


def _mm_tiles(m, n, contraction, out_bytes):
    for rows, cols in ((1664, 1024), (832, 1024), (416, 1024), (416, 512)):
        tm, tn = _pick(m, rows, 16), _pick(n, cols, 128)
        if 2 * 2 * contraction * (tm + tn) + tm * tn * (4 + 2 * out_bytes) <= MM_TILE_BUDGET:
            break
    return tm, tn


def _out_bytes(out_dtype, extras):
    return jnp.dtype(out_dtype).itemsize + sum(e.dtype.itemsize for e in extras)


def _mm_nn(name, a, w, out_dtype, a_pro=None, epi=None, extras=(), tasks=()):
    m, k = a.shape
    _, n = w.shape
    tm, tn = _mm_tiles(m, n, k, _out_bytes(out_dtype, extras))
    tile = pl.BlockSpec((tm, tn), lambda j, i: (i, j))
    return _mm(name, a, w, dims=_NN, grid=(n // tn, m // tm),
               a_spec=pl.BlockSpec((tm, k), lambda j, i: (i, 0)), b_spec=pl.BlockSpec((k, tn), lambda j, i: (0, j)),
               out_shape=jax.ShapeDtypeStruct((m, n), out_dtype), out_spec=tile,
               extras=extras, extra_specs=[tile] * len(extras), a_pro=a_pro, epi=epi, tasks=tasks)


def _mm_nt(name, a, w, out_dtype, epi=None, extras=(), tasks=()):
    m, c = a.shape
    r, _ = w.shape
    tm, tn = _mm_tiles(m, r, c, _out_bytes(out_dtype, extras))
    tile = pl.BlockSpec((tm, tn), lambda j, i: (i, j))
    return _mm(name, a, w, dims=_NT, grid=(r // tn, m // tm),
               a_spec=pl.BlockSpec((tm, c), lambda j, i: (i, 0)), b_spec=pl.BlockSpec((tn, c), lambda j, i: (j, 0)),
               out_shape=jax.ShapeDtypeStruct((m, r), out_dtype), out_spec=tile,
               extras=extras, extra_specs=[tile] * len(extras), epi=epi, tasks=tasks)


def _mm_tn(name, a, b, stacked, a_pro=None, tasks=()):
    t, ka = a.shape
    _, nb = b.shape
    ns = nb // N_CHIPS if stacked else nb
    ta, tb = _pick(ka, 512, 128), _pick(ns, 640, 128)
    if stacked:
        per = ns // tb
        out_shape = jax.ShapeDtypeStruct((N_CHIPS, ka, ns), F32)
        out_spec = pl.BlockSpec((None, ta, tb), lambda i, j: (j // per, i, j % per))
    else:
        out_shape = jax.ShapeDtypeStruct((ka, nb), F32)
        out_spec = pl.BlockSpec((ta, tb), lambda i, j: (i, j))
    return _mm(name, a, b, dims=_TN, grid=(ka // ta, nb // tb),
               a_spec=pl.BlockSpec((t, ta), lambda i, j: (0, i)), b_spec=pl.BlockSpec((t, tb), lambda i, j: (0, j)),
               out_shape=out_shape, out_spec=out_spec, a_pro=a_pro, tasks=tasks)


def _relu_sq(z):
    a = jnp.maximum(z, 0)
    return a * a


def _rms_fwd(name, h, g):
    m, d = h.shape
    tr = _pick(m, 256, 16)

    def body(h_ref, g_ref, o_ref):
        x = h_ref[...]
        rstd = lax.rsqrt(jnp.mean(x * x, axis=-1, keepdims=True) + RMS_EPS)
        o_ref[...] = ((x * rstd) * g_ref[...]).astype(BF16)

    row = pl.BlockSpec((tr, d), lambda i: (i, 0))
    return pl.pallas_call(
        body, name=name, grid=(m // tr,), out_shape=jax.ShapeDtypeStruct((m, d), BF16),
        in_specs=[row, pl.BlockSpec((1, d), lambda i: (0, 0))], out_specs=row,
        compiler_params=_params(("parallel",)),
    )(h, g.reshape(1, d))


def _rms_bwd_math(x, g, dn):
    rstd = lax.rsqrt(jnp.mean(x * x, axis=-1, keepdims=True) + RMS_EPS)
    xhat = x * rstd
    dxhat = dn * g
    dx = rstd * (dxhat - xhat * jnp.mean(dxhat * xhat, axis=-1, keepdims=True))
    return dx, dn * xhat


def _fold8(v):
    r, c = v.shape
    return jnp.sum(v.reshape(r // 8, 8, c), axis=0)


def _rms_bwd(name, dn, h, g, dh_in):
    m, d = h.shape
    tr = _pick(m, 256, 16)
    nt = m // tr

    def body(dn_ref, h_ref, g_ref, dh_ref, o_ref, ob_ref, dg_ref):
        dx, dgp = _rms_bwd_math(h_ref[...], g_ref[...], dn_ref[...].astype(F32))
        dh = dh_ref[...] + dx
        o_ref[...] = dh
        ob_ref[...] = dh.astype(BF16)

        @pl.when(pl.program_id(0) == 0)
        def _():
            dg_ref[...] = jnp.zeros_like(dg_ref)

        dg_ref[...] += _fold8(dgp)

    row = pl.BlockSpec((tr, d), lambda i: (i, 0))
    return pl.pallas_call(
        body, name=name, grid=(nt,),
        out_shape=(jax.ShapeDtypeStruct((m, d), F32), jax.ShapeDtypeStruct((m, d), BF16),
                   jax.ShapeDtypeStruct((8, d), F32)),
        in_specs=[row, row, pl.BlockSpec((1, d), lambda i: (0, 0)), row],
        out_specs=(row, row, pl.BlockSpec((8, d), lambda i: (0, 0))),
        compiler_params=_params(("arbitrary",)),
    )(dn, h, g.reshape(1, d), dh_in)


def _rms_bwd_tokens(name, dn, h, g, dh_in, tasks=()):
    m, d = h.shape
    nb = m // BLOCK

    def body(dn_ref, h_ref, g_ref, dh_ref, gx_ref, first_ref, dg_ref):
        i = pl.program_id(0)
        dx, dgp = _rms_bwd_math(h_ref[...], g_ref[...], dn_ref[...].astype(F32))
        dh = dh_ref[...] + dx
        gx_ref[...] = dh

        @pl.when(i == 0)
        def _():
            first_ref[...] = dh
            dg_ref[...] = jnp.zeros_like(dg_ref)

        dg_ref[...] += _fold8(dgp)

    row = pl.BlockSpec((BLOCK, d), lambda i: (i, 0))
    res, carried = _carrier_call(
        name, body, [dn, h, g.reshape(1, d), dh_in], grid=(nb,),
        in_specs=[row, row, pl.BlockSpec((1, d), lambda i: (0, 0)), row],
        out_specs=[pl.BlockSpec((BLOCK, d), lambda i: (jnp.maximum(i - 1, 0), 0)),
                   pl.BlockSpec((ROW0, d), lambda i: (0, 0)), pl.BlockSpec((8, d), lambda i: (0, 0))],
        out_shape=[jax.ShapeDtypeStruct((m - ROW0, d), F32), jax.ShapeDtypeStruct((ROW0, d), F32),
                   jax.ShapeDtypeStruct((8, d), F32)],
        semantics=("arbitrary",), tasks=tasks)
    return (*res, carried)


def _loss_head(h, g, target):
    m, d = h.shape
    tr = BLOCK

    def body(h_ref, g_ref, t_ref, loss_ref, o_ref, ob_ref, dg_ref):
        i = pl.program_id(0)
        x = h_ref[...]
        gv = g_ref[...]
        rstd = lax.rsqrt(jnp.mean(x * x, axis=-1, keepdims=True) + RMS_EPS)
        err = jnp.where(i > 0, (x * rstd) * gv - t_ref[...], 0.0)
        dx, dgp = _rms_bwd_math(x, gv, err * (1.0 / d))
        o_ref[...] = dx
        ob_ref[...] = dx.astype(BF16)

        @pl.when(i == 0)
        def _():
            dg_ref[...] = jnp.zeros_like(dg_ref)
            loss_ref[...] = jnp.zeros_like(loss_ref)

        dg_ref[...] += _fold8(dgp)
        sq = jnp.mean(err * err, axis=-1, keepdims=True)
        loss_ref[...] += 0.5 * jnp.sum(sq, axis=0, keepdims=True)

    row = pl.BlockSpec((tr, d), lambda i: (i, 0))
    return pl.pallas_call(
        body, name="loss_head", grid=(m // tr,),
        out_shape=(jax.ShapeDtypeStruct((8, 128), F32), jax.ShapeDtypeStruct((m, d), F32),
                   jax.ShapeDtypeStruct((m, d), BF16), jax.ShapeDtypeStruct((8, d), F32)),
        in_specs=[row, pl.BlockSpec((1, d), lambda i: (0, 0)),
                  pl.BlockSpec((tr, d), lambda i: (jnp.maximum(i - 1, 0), 0))],
        out_specs=(pl.BlockSpec((8, 128), lambda i: (0, 0)), row, row,
                   pl.BlockSpec((8, d), lambda i: (0, 0))),
        compiler_params=_params(("arbitrary",)),
    )(h, g.reshape(1, d), target)


HALO = 16


def _shift_down(cat, k):
    return pltpu.roll(cat, k, axis=0)[HALO:]


def _shift_up(cat, k):
    n = cat.shape[0]
    return pltpu.roll(cat, n - k, axis=0)[:n - HALO]


def _conv_fwd(bcu, cw):
    m, d3 = bcu.shape
    d = d3 // 3
    tr, tc = _pick(m, 416, 16), _pick(d, 512, 128)
    hb = tr // HALO

    def body(x_ref, xb_ref, w_ref, o_ref):
        i = pl.program_id(0)
        for j in range(d // tc):
            col = slice(j * tc, (j + 1) * tc)
            cb, cc, cu = (slice(q * d + j * tc, q * d + (j + 1) * tc) for q in range(3))
            v = x_ref[:, cc].astype(F32) * x_ref[:, cu].astype(F32)
            vh = jnp.where(i > 0, xb_ref[:, cc].astype(F32) * xb_ref[:, cu].astype(F32), 0.0)
            cat = jnp.concatenate([vh, v], axis=0)
            w = w_ref[:, col]
            conv = w[2:3] * v + w[1:2] * _shift_down(cat, 1) + w[0:1] * _shift_down(cat, 2)
            o_ref[:, col] = (x_ref[:, cb].astype(F32) * conv).astype(BF16)

    return pl.pallas_call(
        body, name="conv_fwd", grid=(m // tr,), out_shape=jax.ShapeDtypeStruct((m, d), BF16),
        in_specs=[pl.BlockSpec((tr, d3), lambda i: (i, 0)),
                  pl.BlockSpec((HALO, d3), lambda i: (jnp.maximum(i * hb - 1, 0), 0)),
                  pl.BlockSpec((8, d), lambda i: (0, 0))],
        out_specs=pl.BlockSpec((tr, d), lambda i: (i, 0)),
        compiler_params=_params(("parallel",)),
    )(bcu, bcu, cw)


def _conv_bwd(bcu, cw, dg):
    m, d3 = bcu.shape
    d = d3 // 3
    tr, tc = _pick(m, 208, 16), _pick(d, 512, 128)
    hb, nt = tr // HALO, m // tr

    def body(x_ref, xb_ref, xa_ref, dg_ref, dga_ref, w_ref, o_ref, dw_ref):
        i = pl.program_id(0)

        @pl.when(i == 0)
        def _():
            dw_ref[...] = jnp.zeros_like(dw_ref)

        for j in range(d // tc):
            col = slice(j * tc, (j + 1) * tc)
            cb, cc, cu = (slice(q * d + j * tc, q * d + (j + 1) * tc) for q in range(3))
            w = w_ref[:, col]
            b, c, u = x_ref[:, cb].astype(F32), x_ref[:, cc].astype(F32), x_ref[:, cu].astype(F32)
            dgv = dg_ref[:, col].astype(F32)
            v = c * u
            vh = jnp.where(i > 0, xb_ref[:, cc].astype(F32) * xb_ref[:, cu].astype(F32), 0.0)
            cat = jnp.concatenate([vh, v], axis=0)
            v1, v2 = _shift_down(cat, 1), _shift_down(cat, 2)
            dconv = dgv * b
            o_ref[:, cb] = (dgv * (w[2:3] * v + w[1:2] * v1 + w[0:1] * v2)).astype(BF16)
            taps = [jnp.sum(dconv * t, axis=0, keepdims=True) for t in (v2, v1, v)]
            dw_ref[:, col] += jnp.concatenate(taps + [jnp.zeros((5, tc), F32)], axis=0)
            nxt = jnp.where(i < nt - 1, dga_ref[:, col].astype(F32) * xa_ref[:, cb].astype(F32), 0.0)
            cat2 = jnp.concatenate([dconv, nxt], axis=0)
            dv = w[2:3] * dconv + w[1:2] * _shift_up(cat2, 1) + w[0:1] * _shift_up(cat2, 2)
            o_ref[:, cc] = (dv * u).astype(BF16)
            o_ref[:, cu] = (dv * c).astype(BF16)

    def rows(width):
        return pl.BlockSpec((tr, width), lambda i: (i, 0))

    def before(width):
        return pl.BlockSpec((HALO, width), lambda i: (jnp.maximum(i * hb - 1, 0), 0))

    def after(width):
        return pl.BlockSpec((HALO, width), lambda i: (jnp.minimum((i + 1) * hb, m // HALO - 1), 0))

    return pl.pallas_call(
        body, name="conv_bwd", grid=(nt,),
        out_shape=(jax.ShapeDtypeStruct((m, d3), BF16), jax.ShapeDtypeStruct((8, d), F32)),
        in_specs=[rows(d3), before(d3), after(d3), rows(d), after(d), pl.BlockSpec((8, d), lambda i: (0, 0))],
        out_specs=(rows(d3), pl.BlockSpec((8, d), lambda i: (0, 0))),
        compiler_params=_params(("arbitrary",)),
    )(bcu, bcu, bcu, dg, dg, cw)


PAIR = 2 * HEAD_DIM


def _rope_tables(m):
    pad = ROW0 - N_META
    pos = jnp.arange(m, dtype=F32) - pad
    inv = ROPE_THETA ** (-jnp.arange(0, HEAD_DIM, 2, dtype=F32) / HEAD_DIM)
    ang = pos[:, None] * inv[None, :]
    cos, sin = jnp.cos(ang), jnp.sin(ang)
    return jnp.tile(jnp.concatenate([cos, cos], axis=1), (1, 2)), jnp.tile(jnp.concatenate([-sin, sin], axis=1), (1, 2))


def _rope_pair(x, c, s):
    half = HEAD_DIM // 2
    lane = lax.broadcasted_iota(jnp.int32, x.shape, 1)
    swapped = jnp.where(lane % HEAD_DIM < half, pltpu.roll(x, PAIR - half, axis=1), pltpu.roll(x, half, axis=1))
    return x * c + swapped * s


def _attn_mask(i, rows):
    r = lax.broadcasted_iota(jnp.int32, (rows, 2 * BLOCK), 0) % BLOCK
    cidx = lax.broadcasted_iota(jnp.int32, (rows, 2 * BLOCK), 1)
    key = (i - 1) * BLOCK + cidx
    return (cidx > r) & (cidx <= r + BLOCK) & (key >= ROW0 - N_META)


BAND = 2 * BLOCK


def _rope_qk(qkv, cos, sin):
    m, width = qkv.shape
    scale = HEAD_DIM ** -0.5
    nq, nk = N_Q_HEADS // 2, N_KV_HEADS // 2

    def body(x_ref, c_ref, s_ref, o_ref):
        c, s = c_ref[...], s_ref[...]
        cq, sq = c * scale, s * scale
        for t in range(nq + 2 * nk):
            col = slice(t * PAIR, (t + 1) * PAIR)
            if t < nq:
                o_ref[:, col] = _rope_pair(x_ref[:, col].astype(F32), cq, sq).astype(BF16)
            elif t < nq + nk:
                o_ref[:, col] = _rope_pair(x_ref[:, col].astype(F32), c, s).astype(BF16)
            else:
                o_ref[:, col] = x_ref[:, col]

    row = pl.BlockSpec((BLOCK, width), lambda i: (i, 0))
    tab = pl.BlockSpec((BLOCK, PAIR), lambda i: (i, 0))
    return pl.pallas_call(
        body, name="rope_qk", grid=(m // BLOCK,), out_shape=jax.ShapeDtypeStruct((m, width), BF16),
        in_specs=[row, tab, tab], out_specs=row, compiler_params=_params(("parallel",)),
    )(qkv, cos, sin)


def _pair_rows(ref, h):
    pairs = N_Q_HEADS // N_KV_HEADS // 2
    return jnp.concatenate([ref[:, (h * pairs + g) * PAIR:(h * pairs + g + 1) * PAIR] for g in range(pairs)], axis=0)


def _twice(x):
    z = jnp.zeros_like(x)
    return jnp.concatenate([jnp.concatenate([x, z], axis=1), jnp.concatenate([z, x], axis=1)], axis=0)


def _kv_band(cur_ref, prev_ref, h):
    k0, v0 = N_Q_HEADS * HEAD_DIM + h * HEAD_DIM, (N_Q_HEADS + N_KV_HEADS) * HEAD_DIM + h * HEAD_DIM
    p0, p1 = h * HEAD_DIM, (N_KV_HEADS + h) * HEAD_DIM
    k = jnp.concatenate([prev_ref[:, p0:p0 + HEAD_DIM], cur_ref[:, k0:k0 + HEAD_DIM]], axis=0)
    v = jnp.concatenate([prev_ref[:, p1:p1 + HEAD_DIM], cur_ref[:, v0:v0 + HEAD_DIM]], axis=0)
    return k, v


def _attn2_specs(width):
    kvw = 2 * N_KV_HEADS * HEAD_DIM
    cur = pl.BlockSpec((BLOCK, width), lambda i: (i, 0))
    prev = pl.BlockSpec((BLOCK, kvw), lambda i: (jnp.maximum(i - 1, 0), N_Q_HEADS * HEAD_DIM // kvw))
    return cur, prev


def _attn2_fwd(qkr, sink2):
    m, width = qkr.shape
    nb, rows = m // BLOCK, N_Q_HEADS // N_KV_HEADS // 2 * BLOCK
    dq = N_Q_HEADS * HEAD_DIM

    def body(x_ref, prev_ref, s_ref, o_ref, p_ref):
        allowed = _attn_mask(pl.program_id(0), rows)
        col0 = lax.broadcasted_iota(jnp.int32, (rows, BAND), 1) == 0
        lane = lax.broadcasted_iota(jnp.int32, (rows, PAIR), 1)
        rsel = lax.broadcasted_iota(jnp.int32, (2 * BAND, PAIR), 0) < BAND
        lsel = lax.broadcasted_iota(jnp.int32, (2 * BAND, PAIR), 1) < HEAD_DIM
        ones2 = jnp.where(rsel == lsel, 1.0, 0.0).astype(BF16)
        for h in range(N_KV_HEADS):
            k, v = _kv_band(x_ref, prev_ref, h)
            s2 = lax.dot_general(_pair_rows(x_ref, h), _twice(k), _NT, preferred_element_type=F32)
            sink = s_ref[h]
            e, mx = [], []
            for half in range(2):
                s = jnp.where(allowed, s2[:, half * BAND:(half + 1) * BAND], NEG_INF)
                mx.append(jnp.maximum(jnp.max(s, axis=-1, keepdims=True), sink[:, half * HEAD_DIM:half * HEAD_DIM + 1]))
                e.append(jnp.exp(s - mx[half]).astype(BF16))
            eb2 = jnp.concatenate(e, axis=1)
            es2 = jnp.exp(sink - jnp.where(lane < HEAD_DIM, mx[0], mx[1]))
            ov2 = jnp.dot(eb2, _twice(v), preferred_element_type=F32)
            inv2 = 1.0 / (jnp.dot(eb2, ones2, preferred_element_type=F32) + es2)
            o2 = (ov2 * inv2).astype(BF16)
            ps2 = es2 * inv2
            for g in range(rows // BLOCK):
                col = (h * (rows // BLOCK) + g) * PAIR
                o_ref[:, col:col + PAIR] = o2[g * BLOCK:(g + 1) * BLOCK, :]
            for half in range(2):
                at = half * HEAD_DIM
                p = jnp.where(col0, ps2[:, at:at + 1], e[half].astype(F32) * inv2[:, at:at + 1])
                p_ref[h, :, half * BAND:(half + 1) * BAND] = p.astype(BF16)

    cur, prev = _attn2_specs(width)
    return pl.pallas_call(
        body, name="attn_fwd", grid=(nb,),
        out_shape=(jax.ShapeDtypeStruct((m, dq), BF16), jax.ShapeDtypeStruct((N_KV_HEADS, nb * rows, 2 * BAND), BF16)),
        in_specs=[cur, prev, pl.BlockSpec((N_KV_HEADS, rows, PAIR), lambda i: (0, 0, 0))],
        out_specs=(pl.BlockSpec((BLOCK, dq), lambda i: (i, 0)),
                   pl.BlockSpec((N_KV_HEADS, rows, 2 * BAND), lambda i: (0, i, 0))),
        compiler_params=_params(("parallel",)),
    )(qkr, qkr, sink2)


def _attn2_bwd(qkr, p, o, do):
    m, width = qkr.shape
    nb, rows = m // BLOCK, N_Q_HEADS // N_KV_HEADS // 2 * BLOCK
    dq = N_Q_HEADS * HEAD_DIM

    def body(x_ref, prev_ref, p_ref, o_ref, do_ref, dq_ref, dk_ref, dv_ref, ds_ref):
        colz = lax.broadcasted_iota(jnp.int32, (rows, 2 * BAND), 1) % BAND == 0
        even = lax.broadcasted_iota(jnp.int32, (rows, PAIR), 1) < HEAD_DIM

        @pl.when(pl.program_id(0) == 0)
        def _():
            ds_ref[...] = jnp.zeros_like(ds_ref)

        for h in range(N_KV_HEADS):
            k, v = _kv_band(x_ref, prev_ref, h)
            k2, v2 = _twice(k), _twice(v)
            q2, do2, pv = _pair_rows(x_ref, h), _pair_rows(do_ref, h), p_ref[h]
            prod = do2.astype(F32) * _pair_rows(o_ref, h).astype(F32)
            delta = [jnp.sum(jnp.where(even, prod, 0.0), axis=-1, keepdims=True),
                     jnp.sum(jnp.where(even, 0.0, prod), axis=-1, keepdims=True)]
            dp2 = lax.dot_general(do2, v2, _NT, preferred_element_type=F32)
            pb = jnp.where(colz, jnp.zeros_like(pv), pv)
            ds = [(pb[:, half * BAND:(half + 1) * BAND].astype(F32)
                   * (dp2[:, half * BAND:(half + 1) * BAND] - delta[half])).astype(BF16) for half in range(2)]
            dsb2 = jnp.concatenate(ds, axis=1)
            dq2 = jnp.dot(dsb2, k2, preferred_element_type=F32).astype(BF16)
            for g in range(rows // BLOCK):
                col = (h * (rows // BLOCK) + g) * PAIR
                dq_ref[:, col:col + PAIR] = dq2[g * BLOCK:(g + 1) * BLOCK, :]
            dkk = lax.dot_general(q2, dsb2, _TN, preferred_element_type=F32)
            dk_ref[h] = (dkk[:HEAD_DIM, :BAND] + dkk[HEAD_DIM:, BAND:]).T
            dvv = lax.dot_general(do2, pb, _TN, preferred_element_type=F32)
            dv_ref[h] = (dvv[:HEAD_DIM, :BAND] + dvv[HEAD_DIM:, BAND:]).T
            ds_ref[h] -= jnp.where(even, pv[:, 0:1].astype(F32) * delta[0], pv[:, BAND:BAND + 1].astype(F32) * delta[1])

    cur, prev = _attn2_specs(width)
    heads = pl.BlockSpec((BLOCK, dq), lambda i: (i, 0))
    band = pl.BlockSpec((N_KV_HEADS, None, BAND, HEAD_DIM), lambda i: (0, i, 0, 0))
    band_shape = jax.ShapeDtypeStruct((N_KV_HEADS, nb, BAND, HEAD_DIM), F32)
    sink = pl.BlockSpec((N_KV_HEADS, rows, PAIR), lambda i: (0, 0, 0))
    return pl.pallas_call(
        body, name="attn_bwd", grid=(nb,),
        out_shape=(jax.ShapeDtypeStruct((m, dq), BF16), band_shape, band_shape,
                   jax.ShapeDtypeStruct((N_KV_HEADS, rows, PAIR), F32)),
        in_specs=[cur, prev, pl.BlockSpec((N_KV_HEADS, rows, 2 * BAND), lambda i: (0, i, 0)), heads, heads],
        out_specs=(heads, band, band, sink), compiler_params=_params(("arbitrary",)),
    )(qkr, qkr, p, o, do)


def _rope_qk_bwd(dq, dkb, dvb, cos, sin):
    nb = dkb.shape[1]
    width = (N_Q_HEADS + 2 * N_KV_HEADS) * HEAD_DIM
    scale = HEAD_DIM ** -0.5
    nq, nk = N_Q_HEADS // 2, N_KV_HEADS // 2

    def body(dq_ref, kc_ref, kn_ref, vc_ref, vn_ref, c_ref, s_ref, o_ref):
        last = pl.program_id(0) == nb - 1
        c, s = c_ref[...], -s_ref[...]

        def band_sum(cur_ref, nxt_ref, t):
            return jnp.concatenate([cur_ref[2 * t + e, BLOCK:, :] + jnp.where(last, 0.0, nxt_ref[2 * t + e, :BLOCK, :])
                                    for e in range(2)], axis=1)

        cq, sq = c * scale, s * scale
        for t in range(nq):
            col = slice(t * PAIR, (t + 1) * PAIR)
            o_ref[:, col] = _rope_pair(dq_ref[:, col].astype(F32), cq, sq).astype(BF16)
        for t in range(nk):
            o_ref[:, (nq + t) * PAIR:(nq + t + 1) * PAIR] = _rope_pair(band_sum(kc_ref, kn_ref, t), c, s).astype(BF16)
            o_ref[:, (nq + nk + t) * PAIR:(nq + nk + t + 1) * PAIR] = band_sum(vc_ref, vn_ref, t).astype(BF16)

    tab = pl.BlockSpec((BLOCK, PAIR), lambda i: (i, 0))
    cur = pl.BlockSpec((N_KV_HEADS, None, BAND, HEAD_DIM), lambda i: (0, i, 0, 0))
    nxt = pl.BlockSpec((N_KV_HEADS, None, BAND, HEAD_DIM), lambda i: (0, jnp.minimum(i + 1, nb - 1), 0, 0))
    return pl.pallas_call(
        body, name="rope_qk_bwd", grid=(nb,), out_shape=jax.ShapeDtypeStruct((nb * BLOCK, width), BF16),
        in_specs=[pl.BlockSpec((BLOCK, N_Q_HEADS * HEAD_DIM), lambda i: (i, 0)), cur, nxt, cur, nxt, tab, tab],
        out_specs=pl.BlockSpec((BLOCK, width), lambda i: (i, 0)),
        compiler_params=_params(("parallel",)),
    )(dq, dkb, dkb, dvb, dvb, cos, sin)


def _tiles2d(r, c):
    tc = _pick(c, 2048, 128) if c % 128 == 0 else c
    tr = _pick(r, max(8, (1 << 20) // tc // 8 * 8), 8) if r % 8 == 0 else r
    return tr, tc


def _cast_bf16(name, w, place, wide):
    r, c = w.shape
    tr, tc = _tiles2d(r, c)
    if tr % 16:
        tr = r
    nc = c // tc

    def body(place_ref, w_ref, o_ref):
        o_ref[...] = w_ref[...].astype(BF16)

    if wide:
        out_shape = jax.ShapeDtypeStruct((r, N_CHIPS * c), BF16)
        out_spec = pl.BlockSpec((tr, tc), lambda i, j, p: (i, p[1] * nc + j))
    else:
        out_shape = jax.ShapeDtypeStruct((N_CHIPS, r, c), BF16)
        out_spec = pl.BlockSpec((None, tr, tc), lambda i, j, p: (p[1], i, j))
    return pl.pallas_call(
        body, name=name, out_shape=out_shape,
        grid_spec=pltpu.PrefetchScalarGridSpec(
            num_scalar_prefetch=1, grid=(r // tr, nc),
            in_specs=[pl.BlockSpec((tr, tc), lambda i, j, p: (i, j))], out_specs=out_spec),
        compiler_params=_params(("parallel", "parallel")),
    )(place, w)


def _pair_sum(name, g, got, place):
    n, r, c = g.shape
    half = r // 2
    tr, tc = _tiles2d(half, c)
    nh = half // tr

    def body(place_ref, g_ref, got_ref, o_ref, own_ref):
        s = (g_ref[...] + got_ref[...]).astype(BF16)
        o_ref[...] = s

        @pl.when(pl.program_id(2) == place_ref[1])
        def _():
            own_ref[...] = s

    tile = pl.BlockSpec((None, tr, tc), lambda i, j, k, p: (k, i, j))
    shape = jax.ShapeDtypeStruct((n, half, c), BF16)
    return pl.pallas_call(
        body, name=name, out_shape=(shape, shape),
        grid_spec=pltpu.PrefetchScalarGridSpec(
            num_scalar_prefetch=1, grid=(nh, c // tc, n),
            in_specs=[pl.BlockSpec((None, tr, tc), lambda i, j, k, p: (k, p[0] * nh + i, j)), tile],
            out_specs=(tile, pl.BlockSpec((None, tr, tc), lambda i, j, k, p: (p[1], i, j)))),
        compiler_params=_params(("parallel", "parallel", "arbitrary")),
    )(place, g, got)


def _chip_sum(name, parts, place):
    n, half, c = parts.shape
    tr, tc = _tiles2d(half, c)
    nh = half // tr

    def body(place_ref, p0, p1, p2, p3, o_ref):
        o_ref[...] = ((p0[...].astype(F32) + p1[...].astype(F32)) + p2[...].astype(F32)) + p3[...].astype(F32)

    def chip(k):
        return pl.BlockSpec((None, tr, tc), lambda i, j, p: (k, i, j))

    return pl.pallas_call(
        body, name=name, out_shape=jax.ShapeDtypeStruct((2 * half, c), F32),
        grid_spec=pltpu.PrefetchScalarGridSpec(
            num_scalar_prefetch=1, grid=(nh, c // tc),
            in_specs=[chip(k) for k in range(n)],
            out_specs=pl.BlockSpec((tr, tc), lambda i, j, p: (p[0] * nh + i, j))),
        compiler_params=_params(("parallel", "parallel")),
    )(place, parts, parts, parts, parts)


def _dev_sum(gathered):
    def body(g_ref, o_ref):
        acc = g_ref[0]
        for k in range(1, N_DEV):
            acc = acc + g_ref[k]
        o_ref[...] = acc

    return pl.pallas_call(body, name="dev_sum", out_shape=jax.ShapeDtypeStruct(gathered.shape[1:], F32))(gathered)


def _adamw(name, w, g, m, v, tasks=()):
    r, c = w.shape
    tr, tc = _tiles2d(r, c)
    if r % 8 == 0:
        tr = _pick(r, max(8, (1 << 18) // tc // 8 * 8), 8)

    def body(w_ref, g_ref, m_ref, v_ref, d_ref, mo_ref, vo_ref):
        gv = g_ref[...]
        mn = ADAM_B1 * m_ref[...] + (1.0 - ADAM_B1) * gv
        vn = ADAM_B2 * v_ref[...] + (1.0 - ADAM_B2) * jnp.square(gv)
        m_hat = mn / (1.0 - ADAM_B1 ** ADAM_STEP)
        v_hat = vn / (1.0 - ADAM_B2 ** ADAM_STEP)
        d_ref[...] = -ADAM_LR * (m_hat / (jnp.sqrt(v_hat) + ADAM_EPS) + ADAM_WD * w_ref[...])
        mo_ref[...] = mn
        vo_ref[...] = vn

    tile = pl.BlockSpec((tr, tc), lambda i, j: (i, j))
    shape = jax.ShapeDtypeStruct((r, c), F32)
    return _carrier_call(name, body, [w, g, m, v], grid=(r // tr, c // tc), in_specs=[tile] * 4, out_specs=[tile] * 3,
                         out_shape=[shape] * 3, semantics=("parallel", "parallel"), tasks=tasks)


MATRICES = ("w_in_conv", "w_out_conv", "w_up_0", "w_down_0", "w_qkv", "w_o", "w_up_1", "w_down_1")
COLUMN_SHARDED = ("w_in_conv", "w_up_0", "w_qkv", "w_up_1")
NORMS = ("norm_mix_0", "norm_mlp_0", "norm_mix_1", "norm_mlp_1", "norm_final")


def _rows(stack):
    return stack.reshape(N_CHIPS * stack.shape[1], stack.shape[2])


def _stack(full):
    return full.reshape(N_CHIPS, full.shape[0] // N_CHIPS, full.shape[1])


def _add_residual(acc, res):
    return acc + res


def _relu_sq_grad(acc, z):
    return acc * (2.0 * jnp.maximum(z.astype(F32), 0.0))


def _step(x, target, stacks, small, norms, sinks, place, update):
    d = D_MODEL
    dc = d // N_CHIPS
    pad = ROW0 - N_META
    m = x.shape[0] + ROW0
    grp = N_Q_HEADS // N_KV_HEADS
    cos, sin = _rope_tables(m)
    pairs = grp // 2
    sink2 = jnp.broadcast_to(sinks.astype(F32).reshape(N_KV_HEADS, pairs, 1, 2, 1),
                             (N_KV_HEADS, pairs, BLOCK, 2, HEAD_DIM)).reshape(N_KV_HEADS, pairs * BLOCK, PAIR)

    def gather(*names):
        return _gather_task([stacks[n] for n in names])

    def pair_sum(tag, grad, got):
        return _pair_sum("pair_sum_" + tag, grad, got, place)

    def chip_sum(tag, landed):
        return _chip_sum("chip_sum_" + tag, landed, place)

    (w_in, small_all), = _run("gather_first", [_gather_task([stacks["w_in_conv"], small])])
    small_full = jnp.transpose(small_all, (1, 0, 2)).reshape(SMALL_ROWS, d)
    conv_w8 = small_full[N_META:N_META + 8]
    h0 = jnp.concatenate([jnp.zeros((pad, d), F32), small_full[:N_META], x], axis=0)

    n0 = _rms_fwd("norm_mix_0", h0, norms["norm_mix_0"])
    bcu, ((w_out, w_up0),) = _mm_nn("conv_in", n0, w_in, BF16, tasks=[gather("w_out_conv", "w_up_0")])
    gate = _conv_fwd(bcu, conv_w8)
    h1 = _mm_nn("conv_out", gate, _rows(w_out), F32, epi=_add_residual, extras=(h0,))
    n1 = _rms_fwd("norm_mlp_0", h1, norms["norm_mlp_0"])
    z0, ((w_down0,),) = _mm_nn("mlp_up_0", n1, w_up0, BF16, tasks=[gather("w_down_0")])
    h2, ((w_qkv, w_o, w_up1),) = _mm_nn("mlp_down_0", z0, _rows(w_down0), F32, a_pro=_relu_sq, epi=_add_residual,
                                             extras=(h1,), tasks=[gather("w_qkv", "w_o", "w_up_1")])
    n2 = _rms_fwd("norm_mix_1", h2, norms["norm_mix_1"])
    qkv = _mm_nn("attn_qkv", n2, w_qkv, BF16)
    qkr = _rope_qk(qkv, cos, sin)
    o, probs = _attn2_fwd(qkr, sink2)
    h3 = _mm_nn("attn_out", o, _rows(w_o), F32, epi=_add_residual, extras=(h2,))
    n3 = _rms_fwd("norm_mlp_1", h3, norms["norm_mlp_1"])
    z1, ((w_down1,),) = _mm_nn("mlp_up_1", n3, w_up1, BF16, tasks=[gather("w_down_1")])
    h4 = _mm_nn("mlp_down_1", z1, _rows(w_down1), F32, a_pro=_relu_sq, epi=_add_residual, extras=(h3,))

    gn = {}
    loss, dh, dh_bf, gn["norm_final"] = _loss_head(h4, norms["norm_final"], target)
    dz = _mm_nt("mlp_down_dx_1", dh_bf, _rows(w_down1), BF16, epi=_relu_sq_grad, extras=(z1,))
    g_d1 = _stack(_mm_tn("mlp_down_dw_1", z1, dh_bf, stacked=False, a_pro=_relu_sq))
    g_u1, ((got,),) = _mm_tn("mlp_up_dw_1", n3, dz, stacked=True, tasks=[_pair_exchange_task([g_d1])])
    s_d1 = pair_sum("d1", g_d1, got)
    dn, ((got,), (landed,)) = _mm_nt("mlp_up_dx_1", dz, w_up1, BF16,
                                          tasks=[_pair_exchange_task([g_u1]), _chip_exchange_task([s_d1])])
    s_u1, b_d1 = pair_sum("u1", g_u1, got), chip_sum("d1", landed)
    dh, dh_bf, gn["norm_mlp_1"] = _rms_bwd("norm_mlp_bwd_1", dn, h3, norms["norm_mlp_1"], dh)
    do = _mm_nt("attn_out_dx", dh_bf, _rows(w_o), BF16)
    g_o = _stack(_mm_tn("attn_out_dw", o, dh_bf, stacked=False))
    dq, dkb, dvb, dsink = _attn2_bwd(qkr, probs, o, do)
    dqkv = _rope_qk_bwd(dq, dkb, dvb, cos, sin)
    g_qkv, ((got,),) = _mm_tn("attn_qkv_dw", n2, dqkv, stacked=True, tasks=[_pair_exchange_task([g_o])])
    s_o = pair_sum("o", g_o, got)
    dn, ((got,), (landed,)) = _mm_nt("attn_qkv_dx", dqkv, w_qkv, BF16,
                                          tasks=[_pair_exchange_task([g_qkv]), _chip_exchange_task([s_u1])])
    s_qkv, b_u1 = pair_sum("qkv", g_qkv, got), chip_sum("u1", landed)
    dh, dh_bf, gn["norm_mix_1"] = _rms_bwd("norm_mix_bwd_1", dn, h2, norms["norm_mix_1"], dh)
    dz, ((landed_o, landed_qkv), (r_d1,)) = _mm_nt(
        "mlp_down_dx_0", dh_bf, _rows(w_down0), BF16, epi=_relu_sq_grad, extras=(z0,),
        tasks=[_chip_exchange_task([s_o, s_qkv]), _pair_share_task([b_d1])])
    b_o, b_qkv = chip_sum("o", landed_o), chip_sum("qkv", landed_qkv)
    g_d0, ((r_u1,),) = _mm_tn("mlp_down_dw_0", z0, dh_bf, stacked=False, a_pro=_relu_sq, tasks=[_pair_share_task([b_u1])])
    g_d0 = _stack(g_d0)
    g_u0, ((got,), (r_o, r_qkv)) = _mm_tn("mlp_up_dw_0", n1, dz, stacked=True,
                                          tasks=[_pair_exchange_task([g_d0]), _pair_share_task([b_o, b_qkv])])
    s_d0 = pair_sum("d0", g_d0, got)
    dn, ((got,), (landed,)) = _mm_nt("mlp_up_dx_0", dz, w_up0, BF16,
                                          tasks=[_pair_exchange_task([g_u0]), _chip_exchange_task([s_d0])])
    s_u0, b_d0 = pair_sum("u0", g_u0, got), chip_sum("d0", landed)
    dh, dh_bf, gn["norm_mlp_0"] = _rms_bwd("norm_mlp_bwd_0", dn, h1, norms["norm_mlp_0"], dh)
    dgate = _mm_nt("conv_out_dx", dh_bf, _rows(w_out), BF16)
    dbcu, g_conv_w = _conv_bwd(bcu, conv_w8, dgate)
    g_in, ((landed,), (r_d0,)) = _mm_tn("conv_in_dw", n0, dbcu, stacked=True,
                                        tasks=[_chip_exchange_task([s_u0]), _pair_share_task([b_d0])])
    b_u0 = chip_sum("u0", landed)
    dn, ((got,), (r_u0,)) = _mm_nt("conv_in_dx", dbcu, w_in, BF16,
                                        tasks=[_pair_exchange_task([g_in]), _pair_share_task([b_u0])])
    s_in = pair_sum("in", g_in, got)
    grad_x, dh_first, gn["norm_mix_0"], ((landed,),) = _rms_bwd_tokens(
        "norm_mix_bwd_0", dn, h0, norms["norm_mix_0"], dh, tasks=[_chip_exchange_task([s_in], part=(0, 2))])
    s_in = (s_in[0], landed)

    g_small = jnp.zeros((SMALL_ROWS, d), F32).at[:N_META].set(dh_first[pad:ROW0]).at[N_META:N_META + 8].set(g_conv_w)
    g_small = jnp.transpose(g_small.reshape(SMALL_ROWS, N_CHIPS, dc), (1, 0, 2))
    rep = jnp.zeros((8, d), F32)
    for r, n in enumerate(NORMS):
        rep = rep.at[r].set(jnp.sum(gn[n], axis=0))
    dsink = jnp.sum(dsink.reshape(N_KV_HEADS, pairs, BLOCK, 2, HEAD_DIM)[..., 0], axis=2)
    rep = rep.at[len(NORMS), :N_Q_HEADS].set(dsink.reshape(N_Q_HEADS))
    g_out, ((landed,), (got, rep_all)) = _mm_tn(
        "conv_out_dw", gate, dh_bf, stacked=False,
        tasks=[_chip_exchange_task([s_in], part=(1, 2)), _pair_exchange_task([g_small], small=rep)])
    g_out = _stack(g_out)
    b_in, s_small = chip_sum("in", landed), pair_sum("small", g_small, got)
    (got,), = update("w_down_1", r_d1, [_pair_exchange_task([g_out])])
    s_out = pair_sum("out", g_out, got)
    (landed_out, landed_small), = update("w_up_1", r_u1, [_chip_exchange_task([s_out, s_small])])
    b_out, b_small = chip_sum("out", landed_out), chip_sum("small", landed_small)
    (r_out, r_small, r_in), = update("w_down_0", r_d0, [_pair_share_task([b_out, b_small, b_in])])
    for n, r in (("w_up_0", r_u0), ("w_o", r_o), ("w_qkv", r_qkv), ("w_out_conv", r_out), ("w_in_conv", r_in)):
        update(n, r)
    return loss, grad_x, r_small, rep_all


def kernel(x, meta_tokens, norm_mix_0, w_in_conv, conv_w, w_out_conv, norm_mlp_0, w_up_0, w_down_0, norm_mix_1, w_qkv, attn_sinks, w_o, norm_mlp_1, w_up_1, w_down_1, norm_final, loss_target, m_meta_tokens, m_norm_mix_0, m_w_in_conv, m_conv_w, m_w_out_conv, m_norm_mlp_0, m_w_up_0, m_w_down_0, m_norm_mix_1, m_w_qkv, m_attn_sinks, m_w_o, m_norm_mlp_1, m_w_up_1, m_w_down_1, m_norm_final, v_meta_tokens, v_norm_mix_0, v_w_in_conv, v_conv_w, v_w_out_conv, v_norm_mlp_0, v_w_up_0, v_w_down_0, v_norm_mix_1, v_w_qkv, v_attn_sinks, v_w_o, v_norm_mlp_1, v_w_up_1, v_w_down_1, v_norm_final):
    given = dict(locals())
    names = ("meta_tokens", "norm_mix_0", "w_in_conv", "conv_w", "w_out_conv", "norm_mlp_0", "w_up_0", "w_down_0",
             "norm_mix_1", "w_qkv", "attn_sinks", "w_o", "norm_mlp_1", "w_up_1", "w_down_1", "norm_final")
    d = D_MODEL
    dc = d // N_CHIPS
    chip = 2 * lax.axis_index("x") + lax.axis_index("y")
    place = jnp.stack([lax.axis_index("c"), chip]).astype(jnp.int32)

    small = jnp.zeros((SMALL_ROWS, dc), F32).at[:N_META].set(meta_tokens).at[N_META:N_META + CONV_WIDTH].set(conv_w)
    small = lax.dynamic_update_slice(jnp.zeros((N_CHIPS, SMALL_ROWS, dc), F32), small[None], (chip, 0, 0))
    stacks = {n: _cast_bf16("cast_" + n, given[n], place, n in COLUMN_SHARDED) for n in MATRICES}

    g_out, delta, new_m, new_v = {}, {}, {}, {}

    def update(n, grad, tasks=()):
        wt = given[n]
        shape2 = wt.shape if wt.ndim == 2 else (1, wt.shape[0])
        outs, carried = _adamw("adamw_" + n, wt.reshape(shape2), grad.reshape(shape2),
                               given["m_" + n].reshape(shape2), given["v_" + n].reshape(shape2), tasks=tasks)
        g_out[n] = grad
        delta[n], new_m[n], new_v[n] = [o.reshape(wt.shape) for o in outs]
        return carried

    norms = {n: given[n] for n in NORMS}
    loss_part, grad_x, r_small, rep_all = _step(x[0], loss_target[0], stacks, small, norms, attn_sinks, place, update)
    loss = lax.psum(loss_part[0, 0], ("x", "y", "c"))
    rep_sum = _dev_sum(rep_all)
    update("meta_tokens", r_small[:N_META])
    update("conv_w", r_small[N_META:N_META + CONV_WIDTH])
    for r, n in enumerate(NORMS):
        update(n, rep_sum[r])
    update("attn_sinks", rep_sum[len(NORMS), :N_Q_HEADS])
    return (loss, grad_x[None], *[g_out[n] for n in names], *[delta[n] for n in names],
            *[new_m[n] for n in names], *[new_v[n] for n in names])
```

```python
import jax
import jax.numpy as jnp
from jax import lax
from jax.experimental import pallas as pl
from jax.experimental.pallas import tpu as pltpu

F32 = jnp.float32
BF16 = jnp.bfloat16

D_MODEL = 2048
SEQ = 8192
N_META = 16
CONV_WIDTH = 3
HEAD_DIM = 64
N_Q_HEADS = 32
N_KV_HEADS = 4
BLOCK = 128
ROPE_THETA = 10000.0
D_FF = 4 * D_MODEL
RMS_EPS = 1e-5
NEG_INF = -1e30

ADAM_LR = 0.001
ADAM_B1 = 0.9
ADAM_B2 = 0.999
ADAM_EPS = 1e-08
ADAM_WD = 0.01
ADAM_STEP = 10

N_CHIPS = 4
N_DEV = 8
MESH = pl.DeviceIdType.MESH
VMEM_LIMIT = 56 * 1024 * 1024
SMALL_ROWS = 32
ROW0 = BLOCK


def _pick(n, target, mult):
    best = None
    for t in range(mult, min(n, target) + 1, mult):
        if n % t == 0:
            best = t
    assert best is not None, (n, target, mult)
    return best


def _params(sem=None):
    return pltpu.CompilerParams(dimension_semantics=sem, vmem_limit_bytes=VMEM_LIMIT)


HBM_SPEC = pl.BlockSpec(memory_space=pltpu.HBM)


class _Task:
    def __init__(self, inputs, outputs, aliases, sem_shapes, bind):
        self.inputs, self.outputs, self.aliases = list(inputs), list(outputs), dict(aliases)
        self.sem_shapes, self.bind = list(sem_shapes), bind


def _like(arrays):
    return [jax.ShapeDtypeStruct(a.shape, a.dtype) for a in arrays]


def _bind_tasks(tasks, in_refs, out_refs, sem_refs):
    bound, i, o, s = [], 0, 0, 0
    for t in tasks:
        ni, no, ns = len(t.inputs), len(t.outputs), len(t.sem_shapes)
        bound.append(t.bind(in_refs[i:i + ni], out_refs[o:o + no], sem_refs[s:s + ns]))
        i, o, s = i + ni, o + no, s + ns
    return bound


def _run_phase(bound, phase):
    for b in bound:
        if b[phase] is not None:
            b[phase]()


def _task_plumbing(tasks, in_offset, out_offset):
    ins = [a for t in tasks for a in t.inputs]
    outs = [o for t in tasks for o in t.outputs]
    sems = [s for t in tasks for s in t.sem_shapes]
    aliases, i, o = {}, in_offset, out_offset
    for t in tasks:
        for src, dst in t.aliases.items():
            aliases[i + src] = o + dst
        i, o = i + len(t.inputs), o + len(t.outputs)
    return ins, outs, sems, aliases


def _split_outputs(tasks, flat):
    res, o = [], 0
    for t in tasks:
        res.append(list(flat[o:o + len(t.outputs)]))
        o += len(t.outputs)
    return res


def _run(name, tasks):
    ins, outs, sems, aliases = _task_plumbing(tasks, 0, 0)

    def body(*refs):
        bound = _bind_tasks(tasks, refs[:len(ins)], refs[len(ins):len(ins) + len(outs)], refs[len(ins) + len(outs):])
        for phase in range(3):
            _run_phase(bound, phase)

    flat = pl.pallas_call(
        body, name=name, out_shape=outs, in_specs=[HBM_SPEC] * len(ins), out_specs=[HBM_SPEC] * len(outs),
        input_output_aliases=aliases, scratch_shapes=sems,
    )(*ins)
    return _split_outputs(tasks, flat)


def _place():
    x, y, c = lax.axis_index("x"), lax.axis_index("y"), lax.axis_index("c")
    chips = [(1 - x, y), (x, 1 - y), (1 - x, 1 - y)]
    return x, y, c, 2 * x + y, chips


def _gather_task(stacks):
    n = len(stacks)
    halves = [s.shape[-2] // 2 for s in stacks]

    def bind(_, dst, sems):
        send_a, recv_a, send_b, recv_b = sems
        x, y, c, me, chips = _place()
        sibling = (x, y, 1 - c)

        def half(w, chip, hc):
            rows = pl.ds(hc * halves[w], halves[w])
            if len(stacks[w].shape) == 3:
                return dst[w].at[chip, rows, :]
            cols = stacks[w].shape[1] // N_CHIPS
            return dst[w].at[rows, pl.ds(chip * cols, cols)]

        def over_ici(j, w, block):
            return pltpu.make_async_remote_copy(
                src_ref=half(w, block, c), dst_ref=half(w, block, c), send_sem=send_a.at[j * n + w],
                recv_sem=recv_a.at[j * n + w], device_id=(*chips[j], c), device_id_type=MESH)

        def over_d2d(j, w, hc):
            got = half(w, 2 * chips[j][0] + chips[j][1], hc)
            return pltpu.make_async_remote_copy(
                src_ref=got, dst_ref=got, send_sem=send_b.at[j * n + w], recv_sem=recv_b.at[j * n + w],
                device_id=sibling, device_id_type=MESH)

        pairs = [(j, w) for j in range(3) for w in range(n)]

        def start():
            for j, w in pairs:
                over_ici(j, w, me).start()

        def mid():
            for j, w in pairs:
                over_ici(j, w, 2 * chips[j][0] + chips[j][1]).wait_recv()
                over_d2d(j, w, c).start()

        def finish():
            for j, w in pairs:
                over_d2d(j, w, 1 - c).wait_recv()
            for j, w in pairs:
                over_ici(j, w, me).wait_send()
                over_d2d(j, w, c).wait_send()

        return start, mid, finish

    return _Task(stacks, _like(stacks), {w: w for w in range(n)}, [pltpu.SemaphoreType.DMA((3 * n,))] * 4, bind)


def _pair_exchange_task(grads, small=None):
    n = len(grads)
    halves = [g.shape[1] // 2 for g in grads]

    def bind(src, dst, sems):
        send, recv = sems[0], sems[1]
        x, y, c, me, _ = _place()
        sibling = (x, y, 1 - c)
        dev = 2 * me + c

        def to_sibling(w):
            return pltpu.make_async_remote_copy(
                src_ref=src[w].at[:, pl.ds((1 - c) * halves[w], halves[w]), :], dst_ref=dst[w],
                send_sem=send.at[w], recv_sem=recv.at[w], device_id=sibling, device_id_type=MESH)

        def to_peer(t, block):
            tx, ty, tc = (t >> 2) & 1, (t >> 1) & 1, t & 1
            return pltpu.make_async_remote_copy(
                src_ref=src[n], dst_ref=dst[n].at[block], send_sem=sems[2].at[t], recv_sem=sems[3].at[t],
                device_id=(x ^ tx, y ^ ty, c ^ tc), device_id_type=MESH)

        def mine():
            return pltpu.make_async_copy(src[n], dst[n].at[dev], sems[4])

        def start():
            for w in range(n):
                to_sibling(w).start()
            if small is not None:
                mine().start()
                for t in range(1, N_DEV):
                    to_peer(t, dev).start()

        def finish():
            for w in range(n):
                to_sibling(w).wait_recv()
            if small is not None:
                for t in range(1, N_DEV):
                    to_peer(t, dev ^ t).wait_recv()
            for w in range(n):
                to_sibling(w).wait_send()
            if small is not None:
                for t in range(1, N_DEV):
                    to_peer(t, dev).wait_send()
                mine().wait()

        return start, None, finish

    outputs = [jax.ShapeDtypeStruct((N_CHIPS, h, g.shape[2]), g.dtype) for g, h in zip(grads, halves)]
    sem_shapes = [pltpu.SemaphoreType.DMA((n,)), pltpu.SemaphoreType.DMA((n,))]
    inputs = list(grads)
    if small is not None:
        inputs.append(small)
        outputs.append(jax.ShapeDtypeStruct((N_DEV,) + small.shape, small.dtype))
        sem_shapes += [pltpu.SemaphoreType.DMA((N_DEV,)), pltpu.SemaphoreType.DMA((N_DEV,)), pltpu.SemaphoreType.DMA(())]
    return _Task(inputs, outputs, {}, sem_shapes, bind)


def _chip_exchange_task(summed):
    n = len(summed)

    def bind(refs, dst, sems):
        src = refs[:n]
        send, recv = sems
        x, y, c, me, chips = _place()

        def copy(j, w, block_from, block_to):
            return pltpu.make_async_remote_copy(
                src_ref=src[w].at[block_from], dst_ref=dst[w].at[block_to], send_sem=send.at[j * n + w],
                recv_sem=recv.at[j * n + w], device_id=(*chips[j], c), device_id_type=MESH)

        pairs = [(j, w) for j in range(3) for w in range(n)]

        def start():
            for j, w in pairs:
                copy(j, w, 2 * chips[j][0] + chips[j][1], me).start()

        def finish():
            for j, w in pairs:
                copy(j, w, me, 2 * chips[j][0] + chips[j][1]).wait_recv()
            for j, w in pairs:
                copy(j, w, 2 * chips[j][0] + chips[j][1], me).wait_send()

        return start, None, finish

    partials, landing = [s[0] for s in summed], [s[1] for s in summed]
    return _Task(partials + landing, _like(landing), {n + w: w for w in range(n)},
                 [pltpu.SemaphoreType.DMA((3 * n,))] * 2, bind)


def _pair_share_task(blocks):
    n = len(blocks)

    def bind(_, dst, sems):
        send, recv = sems
        x, y, c, _, _ = _place()

        def copy(w, hc):
            h = blocks[w].shape[0] // 2
            rows = dst[w].at[pl.ds(hc * h, h), :]
            return pltpu.make_async_remote_copy(src_ref=rows, dst_ref=rows, send_sem=send.at[w], recv_sem=recv.at[w],
                                                device_id=(x, y, 1 - c), device_id_type=MESH)

        def start():
            for w in range(n):
                copy(w, c).start()

        def finish():
            for w in range(n):
                copy(w, 1 - c).wait_recv()
            for w in range(n):
                copy(w, c).wait_send()

        return start, None, finish

    return _Task(blocks, _like(blocks), {w: w for w in range(n)}, [pltpu.SemaphoreType.DMA((n,))] * 2, bind)


def _carrier_call(name, body, operands, *, grid, in_specs, out_specs, out_shape, semantics, tasks):
    n_in, n_out = len(operands), len(out_shape)
    t_ins, t_outs, t_sems, aliases = _task_plumbing(tasks, n_in, n_out)
    n_ti, n_to = len(t_ins), len(t_outs)
    total = 1
    for g in grid:
        total *= g
    mid_step = max(0, total - 1 - max(1, total // 8))

    def carrier(*refs):
        outs_at = n_in + n_ti
        if tasks:
            bound = _bind_tasks(tasks, refs[n_in:outs_at], refs[outs_at + n_out:outs_at + n_out + n_to],
                                refs[outs_at + n_out + n_to:])
            step = 0
            for axis, g in enumerate(grid):
                step = step * g + pl.program_id(axis)

            @pl.when(step == 0)
            def _():
                _run_phase(bound, 0)

        body(*refs[:n_in], *refs[outs_at:outs_at + n_out])

        if tasks:
            @pl.when(step == mid_step)
            def _():
                _run_phase(bound, 1)

            @pl.when(step == total - 1)
            def _():
                _run_phase(bound, 2)

    res = pl.pallas_call(
        carrier, name=name, grid=grid, out_shape=[*out_shape, *t_outs],
        in_specs=[*in_specs, *[HBM_SPEC] * n_ti], out_specs=[*out_specs, *[HBM_SPEC] * n_to],
        input_output_aliases=aliases, scratch_shapes=t_sems,
        compiler_params=_params(("arbitrary",) * len(grid) if tasks else semantics),
    )(*operands, *t_ins)
    return list(res[:n_out]), _split_outputs(tasks, res[n_out:])


def _mm(name, a, b, *, dims, grid, a_spec, b_spec, out_shape, out_spec,
        extras=(), extra_specs=(), a_pro=None, epi=None, norm_gain=None, tasks=()):
    n_ex = len(extras)
    normed = norm_gain is not None

    def body(a_ref, b_ref, *rest):
        outs = rest[n_ex + normed:]
        av = a_ref[...]
        if a_pro is not None:
            av = a_pro(av)
        acc = lax.dot_general(av, b_ref[...], dims, preferred_element_type=F32)
        if epi is not None:
            acc = epi(acc, *[e[...] for e in rest[:n_ex]])
        outs[0][...] = acc.astype(outs[0].dtype)
        if normed:
            rstd = lax.rsqrt(jnp.mean(acc * acc, axis=-1, keepdims=True) + RMS_EPS)
            outs[1][...] = ((acc * rstd) * rest[n_ex][...]).astype(BF16)

    operands, in_specs = [a, b, *extras], [a_spec, b_spec, *extra_specs]
    out_specs, out_shapes = [out_spec], [out_shape]
    if normed:
        width = out_shape.shape[1]
        operands.append(norm_gain.reshape(1, width))
        in_specs.append(pl.BlockSpec((1, width), lambda j, i: (0, 0)))
        out_specs.append(out_spec)
        out_shapes.append(jax.ShapeDtypeStruct(out_shape.shape, BF16))
    res, carried = _carrier_call(name, body, operands, grid=grid, in_specs=in_specs, out_specs=out_specs,
                                 out_shape=out_shapes, semantics=("parallel", "parallel"), tasks=tasks)
    res = tuple(res) if normed else res[0]
    return (res, carried) if tasks else res


_NN = (((1,), (0,)), ((), ()))
_NT = (((1,), (1,)), ((), ()))
_TN = (((0,), (0,)), ((), ()))


MM_TILE_BUDGET = 46 * 1024 * 1024


def _mm_tiles(m, n, contraction, out_bytes):
    for rows, cols in ((1664, 1024), (832, 1024), (416, 1024), (416, 512)):
        tm, tn = _pick(m, rows, 16), _pick(n, cols, 128)
        if 2 * 2 * contraction * (tm + tn) + tm * tn * (4 + 2 * out_bytes) <= MM_TILE_BUDGET:
            break
    return tm, tn


def _out_bytes(out_dtype, extras):
    return jnp.dtype(out_dtype).itemsize + sum(e.dtype.itemsize for e in extras)


def _mm_nn(name, a, w, out_dtype, a_pro=None, epi=None, extras=(), norm_gain=None, tasks=()):
    m, k = a.shape
    _, n = w.shape
    tm, tn = _mm_tiles(m, n, k, _out_bytes(out_dtype, extras))
    if norm_gain is not None:
        tm, tn = _pick(m, 416, 16), n
    tile = pl.BlockSpec((tm, tn), lambda j, i: (i, j))
    return _mm(name, a, w, dims=_NN, grid=(n // tn, m // tm),
               a_spec=pl.BlockSpec((tm, k), lambda j, i: (i, 0)), b_spec=pl.BlockSpec((k, tn), lambda j, i: (0, j)),
               out_shape=jax.ShapeDtypeStruct((m, n), out_dtype), out_spec=tile,
               extras=extras, extra_specs=[tile] * len(extras), a_pro=a_pro, epi=epi, norm_gain=norm_gain, tasks=tasks)


def _mm_nt(name, a, w, out_dtype, epi=None, extras=(), tasks=()):
    m, c = a.shape
    r, _ = w.shape
    tm, tn = _mm_tiles(m, r, c, _out_bytes(out_dtype, extras))
    tile = pl.BlockSpec((tm, tn), lambda j, i: (i, j))
    return _mm(name, a, w, dims=_NT, grid=(r // tn, m // tm),
               a_spec=pl.BlockSpec((tm, c), lambda j, i: (i, 0)), b_spec=pl.BlockSpec((tn, c), lambda j, i: (j, 0)),
               out_shape=jax.ShapeDtypeStruct((m, r), out_dtype), out_spec=tile,
               extras=extras, extra_specs=[tile] * len(extras), epi=epi, tasks=tasks)


def _mm_tn(name, a, b, stacked, a_pro=None, tasks=()):
    t, ka = a.shape
    _, nb = b.shape
    ns = nb // N_CHIPS if stacked else nb
    ta, tb = _pick(ka, 512, 128), _pick(ns, 640, 128)
    if stacked:
        per = ns // tb
        out_shape = jax.ShapeDtypeStruct((N_CHIPS, ka, ns), F32)
        out_spec = pl.BlockSpec((None, ta, tb), lambda i, j: (j // per, i, j % per))
    else:
        out_shape = jax.ShapeDtypeStruct((ka, nb), F32)
        out_spec = pl.BlockSpec((ta, tb), lambda i, j: (i, j))
    return _mm(name, a, b, dims=_TN, grid=(ka // ta, nb // tb),
               a_spec=pl.BlockSpec((t, ta), lambda i, j: (0, i)), b_spec=pl.BlockSpec((t, tb), lambda i, j: (0, j)),
               out_shape=out_shape, out_spec=out_spec, a_pro=a_pro, tasks=tasks)


def _relu_sq(z):
    a = jnp.maximum(z, 0)
    return a * a


def _rms_fwd(name, h, g):
    m, d = h.shape
    tr = _pick(m, 512, 16)

    def body(h_ref, g_ref, o_ref):
        x = h_ref[...]
        rstd = lax.rsqrt(jnp.mean(x * x, axis=-1, keepdims=True) + RMS_EPS)
        o_ref[...] = ((x * rstd) * g_ref[...]).astype(BF16)

    row = pl.BlockSpec((tr, d), lambda i: (i, 0))
    return pl.pallas_call(
        body, name=name, grid=(m // tr,), out_shape=jax.ShapeDtypeStruct((m, d), BF16),
        in_specs=[row, pl.BlockSpec((1, d), lambda i: (0, 0))], out_specs=row,
        compiler_params=_params(("parallel",)),
    )(h, g.reshape(1, d))


def _rms_bwd_math(x, g, dn):
    rstd = lax.rsqrt(jnp.mean(x * x, axis=-1, keepdims=True) + RMS_EPS)
    xhat = x * rstd
    dxhat = dn * g
    dx = rstd * (dxhat - xhat * jnp.mean(dxhat * xhat, axis=-1, keepdims=True))
    return dx, dn * xhat


def _fold8(v):
    r, c = v.shape
    return jnp.sum(v.reshape(r // 8, 8, c), axis=0)


def _rms_bwd(name, dn, h, g, dh_in):
    m, d = h.shape
    tr = _pick(m, 512, 16)
    nt = m // tr

    def body(dn_ref, h_ref, g_ref, dh_ref, o_ref, ob_ref, dg_ref):
        dx, dgp = _rms_bwd_math(h_ref[...], g_ref[...], dn_ref[...].astype(F32))
        dh = dh_ref[...] + dx
        o_ref[...] = dh
        ob_ref[...] = dh.astype(BF16)

        @pl.when(pl.program_id(0) == 0)
        def _():
            dg_ref[...] = jnp.zeros_like(dg_ref)

        dg_ref[...] += _fold8(dgp)

    row = pl.BlockSpec((tr, d), lambda i: (i, 0))
    return pl.pallas_call(
        body, name=name, grid=(nt,),
        out_shape=(jax.ShapeDtypeStruct((m, d), F32), jax.ShapeDtypeStruct((m, d), BF16),
                   jax.ShapeDtypeStruct((8, d), F32)),
        in_specs=[row, row, pl.BlockSpec((1, d), lambda i: (0, 0)), row],
        out_specs=(row, row, pl.BlockSpec((8, d), lambda i: (0, 0))),
        compiler_params=_params(("arbitrary",)),
    )(dn, h, g.reshape(1, d), dh_in)


def _rms_bwd_tokens(name, dn, h, g, dh_in):
    m, d = h.shape
    nb = m // BLOCK

    def body(dn_ref, h_ref, g_ref, dh_ref, gx_ref, first_ref, dg_ref):
        i = pl.program_id(0)
        dx, dgp = _rms_bwd_math(h_ref[...], g_ref[...], dn_ref[...].astype(F32))
        dh = dh_ref[...] + dx
        gx_ref[...] = dh

        @pl.when(i == 0)
        def _():
            first_ref[...] = dh
            dg_ref[...] = jnp.zeros_like(dg_ref)

        dg_ref[...] += _fold8(dgp)

    row = pl.BlockSpec((BLOCK, d), lambda i: (i, 0))
    return pl.pallas_call(
        body, name=name, grid=(nb,),
        out_shape=(jax.ShapeDtypeStruct((m - ROW0, d), F32), jax.ShapeDtypeStruct((ROW0, d), F32),
                   jax.ShapeDtypeStruct((8, d), F32)),
        in_specs=[row, row, pl.BlockSpec((1, d), lambda i: (0, 0)), row],
        out_specs=(pl.BlockSpec((BLOCK, d), lambda i: (jnp.maximum(i - 1, 0), 0)),
                   pl.BlockSpec((ROW0, d), lambda i: (0, 0)), pl.BlockSpec((8, d), lambda i: (0, 0))),
        compiler_params=_params(("arbitrary",)),
    )(dn, h, g.reshape(1, d), dh_in)


def _loss_head(h, g, target):
    m, d = h.shape
    tr = BLOCK

    def body(h_ref, g_ref, t_ref, loss_ref, o_ref, ob_ref, dg_ref):
        i = pl.program_id(0)
        x = h_ref[...]
        gv = g_ref[...]
        rstd = lax.rsqrt(jnp.mean(x * x, axis=-1, keepdims=True) + RMS_EPS)
        err = jnp.where(i > 0, (x * rstd) * gv - t_ref[...], 0.0)
        dx, dgp = _rms_bwd_math(x, gv, err * (1.0 / d))
        o_ref[...] = dx
        ob_ref[...] = dx.astype(BF16)

        @pl.when(i == 0)
        def _():
            dg_ref[...] = jnp.zeros_like(dg_ref)
            loss_ref[...] = jnp.zeros_like(loss_ref)

        dg_ref[...] += _fold8(dgp)
        sq = jnp.mean(err * err, axis=-1, keepdims=True)
        loss_ref[...] += 0.5 * jnp.sum(sq, axis=0, keepdims=True)

    row = pl.BlockSpec((tr, d), lambda i: (i, 0))
    return pl.pallas_call(
        body, name="loss_head", grid=(m // tr,),
        out_shape=(jax.ShapeDtypeStruct((8, 128), F32), jax.ShapeDtypeStruct((m, d), F32),
                   jax.ShapeDtypeStruct((m, d), BF16), jax.ShapeDtypeStruct((8, d), F32)),
        in_specs=[row, pl.BlockSpec((1, d), lambda i: (0, 0)),
                  pl.BlockSpec((tr, d), lambda i: (jnp.maximum(i - 1, 0), 0))],
        out_specs=(pl.BlockSpec((8, 128), lambda i: (0, 0)), row, row,
                   pl.BlockSpec((8, d), lambda i: (0, 0))),
        compiler_params=_params(("arbitrary",)),
    )(h, g.reshape(1, d), target)


HALO = 16


def _shift_down(cat, k):
    return pltpu.roll(cat, k, axis=0)[HALO:]


def _shift_up(cat, k):
    n = cat.shape[0]
    return pltpu.roll(cat, n - k, axis=0)[:n - HALO]


def _conv_fwd(bcu, cw):
    m, d3 = bcu.shape
    d = d3 // 3
    tr, tc = _pick(m, 416, 16), _pick(d, 512, 128)
    hb = tr // HALO

    def body(x_ref, xb_ref, w_ref, o_ref):
        i = pl.program_id(0)
        for j in range(d // tc):
            col = slice(j * tc, (j + 1) * tc)
            cb, cc, cu = (slice(q * d + j * tc, q * d + (j + 1) * tc) for q in range(3))
            v = x_ref[:, cc].astype(F32) * x_ref[:, cu].astype(F32)
            vh = jnp.where(i > 0, xb_ref[:, cc].astype(F32) * xb_ref[:, cu].astype(F32), 0.0)
            cat = jnp.concatenate([vh, v], axis=0)
            w = w_ref[:, col]
            conv = w[2:3] * v + w[1:2] * _shift_down(cat, 1) + w[0:1] * _shift_down(cat, 2)
            o_ref[:, col] = (x_ref[:, cb].astype(F32) * conv).astype(BF16)

    return pl.pallas_call(
        body, name="conv_fwd", grid=(m // tr,), out_shape=jax.ShapeDtypeStruct((m, d), BF16),
        in_specs=[pl.BlockSpec((tr, d3), lambda i: (i, 0)),
                  pl.BlockSpec((HALO, d3), lambda i: (jnp.maximum(i * hb - 1, 0), 0)),
                  pl.BlockSpec((8, d), lambda i: (0, 0))],
        out_specs=pl.BlockSpec((tr, d), lambda i: (i, 0)),
        compiler_params=_params(("parallel",)),
    )(bcu, bcu, cw)


def _conv_bwd(bcu, cw, dg):
    m, d3 = bcu.shape
    d = d3 // 3
    tr, tc = _pick(m, 208, 16), _pick(d, 512, 128)
    hb, nt = tr // HALO, m // tr

    def body(x_ref, xb_ref, xa_ref, dg_ref, dga_ref, w_ref, o_ref, dw_ref):
        i = pl.program_id(0)

        @pl.when(i == 0)
        def _():
            dw_ref[...] = jnp.zeros_like(dw_ref)

        for j in range(d // tc):
            col = slice(j * tc, (j + 1) * tc)
            cb, cc, cu = (slice(q * d + j * tc, q * d + (j + 1) * tc) for q in range(3))
            w = w_ref[:, col]
            b, c, u = x_ref[:, cb].astype(F32), x_ref[:, cc].astype(F32), x_ref[:, cu].astype(F32)
            dgv = dg_ref[:, col].astype(F32)
            v = c * u
            vh = jnp.where(i > 0, xb_ref[:, cc].astype(F32) * xb_ref[:, cu].astype(F32), 0.0)
            cat = jnp.concatenate([vh, v], axis=0)
            v1, v2 = _shift_down(cat, 1), _shift_down(cat, 2)
            dconv = dgv * b
            o_ref[:, cb] = (dgv * (w[2:3] * v + w[1:2] * v1 + w[0:1] * v2)).astype(BF16)
            taps = [jnp.sum(dconv * t, axis=0, keepdims=True) for t in (v2, v1, v)]
            dw_ref[:, col] += jnp.concatenate(taps + [jnp.zeros((5, tc), F32)], axis=0)
            nxt = jnp.where(i < nt - 1, dga_ref[:, col].astype(F32) * xa_ref[:, cb].astype(F32), 0.0)
            cat2 = jnp.concatenate([dconv, nxt], axis=0)
            dv = w[2:3] * dconv + w[1:2] * _shift_up(cat2, 1) + w[0:1] * _shift_up(cat2, 2)
            o_ref[:, cc] = (dv * u).astype(BF16)
            o_ref[:, cu] = (dv * c).astype(BF16)

    def rows(width):
        return pl.BlockSpec((tr, width), lambda i: (i, 0))

    def before(width):
        return pl.BlockSpec((HALO, width), lambda i: (jnp.maximum(i * hb - 1, 0), 0))

    def after(width):
        return pl.BlockSpec((HALO, width), lambda i: (jnp.minimum((i + 1) * hb, m // HALO - 1), 0))

    return pl.pallas_call(
        body, name="conv_bwd", grid=(nt,),
        out_shape=(jax.ShapeDtypeStruct((m, d3), BF16), jax.ShapeDtypeStruct((8, d), F32)),
        in_specs=[rows(d3), before(d3), after(d3), rows(d), after(d), pl.BlockSpec((8, d), lambda i: (0, 0))],
        out_specs=(rows(d3), pl.BlockSpec((8, d), lambda i: (0, 0))),
        compiler_params=_params(("arbitrary",)),
    )(bcu, bcu, bcu, dg, dg, cw)


PAIR = 2 * HEAD_DIM


def _rope_tables(m):
    pad = ROW0 - N_META
    pos = jnp.arange(m, dtype=F32) - pad
    inv = ROPE_THETA ** (-jnp.arange(0, HEAD_DIM, 2, dtype=F32) / HEAD_DIM)
    ang = pos[:, None] * inv[None, :]
    cos, sin = jnp.cos(ang), jnp.sin(ang)
    return jnp.tile(jnp.concatenate([cos, cos], axis=1), (1, 2)), jnp.tile(jnp.concatenate([-sin, sin], axis=1), (1, 2))


def _rope_pair(x, c, s):
    half = HEAD_DIM // 2
    lane = lax.broadcasted_iota(jnp.int32, x.shape, 1)
    swapped = jnp.where(lane % HEAD_DIM < half, pltpu.roll(x, PAIR - half, axis=1), pltpu.roll(x, half, axis=1))
    return x * c + swapped * s


def _attn_mask(i, rows):
    r = lax.broadcasted_iota(jnp.int32, (rows, 2 * BLOCK), 0) % BLOCK
    cidx = lax.broadcasted_iota(jnp.int32, (rows, 2 * BLOCK), 1)
    key = (i - 1) * BLOCK + cidx
    return (cidx > r) & (cidx <= r + BLOCK) & (key >= ROW0 - N_META)


BAND = 2 * BLOCK


def _rope_qk(qkv, cos, sin):
    m, width = qkv.shape
    scale = HEAD_DIM ** -0.5
    nq, nk = N_Q_HEADS // 2, N_KV_HEADS // 2

    def body(x_ref, c_ref, s_ref, o_ref):
        c, s = c_ref[...], s_ref[...]
        cq, sq = c * scale, s * scale
        for t in range(nq + 2 * nk):
            col = slice(t * PAIR, (t + 1) * PAIR)
            if t < nq:
                o_ref[:, col] = _rope_pair(x_ref[:, col].astype(F32), cq, sq).astype(BF16)
            elif t < nq + nk:
                o_ref[:, col] = _rope_pair(x_ref[:, col].astype(F32), c, s).astype(BF16)
            else:
                o_ref[:, col] = x_ref[:, col]

    row = pl.BlockSpec((BLOCK, width), lambda i: (i, 0))
    tab = pl.BlockSpec((BLOCK, PAIR), lambda i: (i, 0))
    return pl.pallas_call(
        body, name="rope_qk", grid=(m // BLOCK,), out_shape=jax.ShapeDtypeStruct((m, width), BF16),
        in_specs=[row, tab, tab], out_specs=row, compiler_params=_params(("parallel",)),
    )(qkv, cos, sin)


def _pair_rows(ref, h):
    pairs = N_Q_HEADS // N_KV_HEADS // 2
    return jnp.concatenate([ref[:, (h * pairs + g) * PAIR:(h * pairs + g + 1) * PAIR] for g in range(pairs)], axis=0)


def _twice(x):
    z = jnp.zeros_like(x)
    return jnp.concatenate([jnp.concatenate([x, z], axis=1), jnp.concatenate([z, x], axis=1)], axis=0)


def _kv_band(cur_ref, prev_ref, h):
    k0, v0 = N_Q_HEADS * HEAD_DIM + h * HEAD_DIM, (N_Q_HEADS + N_KV_HEADS) * HEAD_DIM + h * HEAD_DIM
    p0, p1 = h * HEAD_DIM, (N_KV_HEADS + h) * HEAD_DIM
    k = jnp.concatenate([prev_ref[:, p0:p0 + HEAD_DIM], cur_ref[:, k0:k0 + HEAD_DIM]], axis=0)
    v = jnp.concatenate([prev_ref[:, p1:p1 + HEAD_DIM], cur_ref[:, v0:v0 + HEAD_DIM]], axis=0)
    return k, v


def _attn2_specs(width):
    kvw = 2 * N_KV_HEADS * HEAD_DIM
    cur = pl.BlockSpec((BLOCK, width), lambda i: (i, 0))
    prev = pl.BlockSpec((BLOCK, kvw), lambda i: (jnp.maximum(i - 1, 0), N_Q_HEADS * HEAD_DIM // kvw))
    return cur, prev


def _attn2_fwd(qkr, sink2):
    m, width = qkr.shape
    nb, rows = m // BLOCK, N_Q_HEADS // N_KV_HEADS // 2 * BLOCK
    dq = N_Q_HEADS * HEAD_DIM

    def body(x_ref, prev_ref, s_ref, o_ref, p_ref):
        allowed = _attn_mask(pl.program_id(0), rows)
        col0 = lax.broadcasted_iota(jnp.int32, (rows, BAND), 1) == 0
        lane = lax.broadcasted_iota(jnp.int32, (rows, PAIR), 1)
        rsel = lax.broadcasted_iota(jnp.int32, (2 * BAND, PAIR), 0) < BAND
        lsel = lax.broadcasted_iota(jnp.int32, (2 * BAND, PAIR), 1) < HEAD_DIM
        ones2 = jnp.where(rsel == lsel, 1.0, 0.0).astype(BF16)
        for h in range(N_KV_HEADS):
            k, v = _kv_band(x_ref, prev_ref, h)
            s2 = lax.dot_general(_pair_rows(x_ref, h), _twice(k), _NT, preferred_element_type=F32)
            sink = s_ref[h]
            e, mx = [], []
            for half in range(2):
                s = jnp.where(allowed, s2[:, half * BAND:(half + 1) * BAND], NEG_INF)
                mx.append(jnp.maximum(jnp.max(s, axis=-1, keepdims=True), sink[:, half * HEAD_DIM:half * HEAD_DIM + 1]))
                e.append(jnp.exp(s - mx[half]).astype(BF16))
            eb2 = jnp.concatenate(e, axis=1)
            es2 = jnp.exp(sink - jnp.where(lane < HEAD_DIM, mx[0], mx[1]))
            ov2 = jnp.dot(eb2, _twice(v), preferred_element_type=F32)
            inv2 = 1.0 / (jnp.dot(eb2, ones2, preferred_element_type=F32) + es2)
            o2 = (ov2 * inv2).astype(BF16)
            ps2 = es2 * inv2
            for g in range(rows // BLOCK):
                col = (h * (rows // BLOCK) + g) * PAIR
                o_ref[:, col:col + PAIR] = o2[g * BLOCK:(g + 1) * BLOCK, :]
            for half in range(2):
                at = half * HEAD_DIM
                p = jnp.where(col0, ps2[:, at:at + 1], e[half].astype(F32) * inv2[:, at:at + 1])
                p_ref[h, :, half * BAND:(half + 1) * BAND] = p.astype(BF16)

    cur, prev = _attn2_specs(width)
    return pl.pallas_call(
        body, name="attn_fwd", grid=(nb,),
        out_shape=(jax.ShapeDtypeStruct((m, dq), BF16), jax.ShapeDtypeStruct((N_KV_HEADS, nb * rows, 2 * BAND), BF16)),
        in_specs=[cur, prev, pl.BlockSpec((N_KV_HEADS, rows, PAIR), lambda i: (0, 0, 0))],
        out_specs=(pl.BlockSpec((BLOCK, dq), lambda i: (i, 0)),
                   pl.BlockSpec((N_KV_HEADS, rows, 2 * BAND), lambda i: (0, i, 0))),
        compiler_params=_params(("parallel",)),
    )(qkr, qkr, sink2)


def _attn2_bwd(qkr, p, o, do):
    m, width = qkr.shape
    nb, rows = m // BLOCK, N_Q_HEADS // N_KV_HEADS // 2 * BLOCK
    dq = N_Q_HEADS * HEAD_DIM

    def body(x_ref, prev_ref, p_ref, o_ref, do_ref, dq_ref, dk_ref, dv_ref, ds_ref):
        colz = lax.broadcasted_iota(jnp.int32, (rows, 2 * BAND), 1) % BAND == 0
        even = lax.broadcasted_iota(jnp.int32, (rows, PAIR), 1) < HEAD_DIM

        @pl.when(pl.program_id(0) == 0)
        def _():
            ds_ref[...] = jnp.zeros_like(ds_ref)

        for h in range(N_KV_HEADS):
            k, v = _kv_band(x_ref, prev_ref, h)
            k2, v2 = _twice(k), _twice(v)
            q2, do2, pv = _pair_rows(x_ref, h), _pair_rows(do_ref, h), p_ref[h]
            prod = do2.astype(F32) * _pair_rows(o_ref, h).astype(F32)
            delta = [jnp.sum(jnp.where(even, prod, 0.0), axis=-1, keepdims=True),
                     jnp.sum(jnp.where(even, 0.0, prod), axis=-1, keepdims=True)]
            dp2 = lax.dot_general(do2, v2, _NT, preferred_element_type=F32)
            pb = jnp.where(colz, jnp.zeros_like(pv), pv)
            ds = [(pb[:, half * BAND:(half + 1) * BAND].astype(F32)
                   * (dp2[:, half * BAND:(half + 1) * BAND] - delta[half])).astype(BF16) for half in range(2)]
            dsb2 = jnp.concatenate(ds, axis=1)
            dq2 = jnp.dot(dsb2, k2, preferred_element_type=F32).astype(BF16)
            for g in range(rows // BLOCK):
                col = (h * (rows // BLOCK) + g) * PAIR
                dq_ref[:, col:col + PAIR] = dq2[g * BLOCK:(g + 1) * BLOCK, :]
            dkk = lax.dot_general(q2, dsb2, _TN, preferred_element_type=F32)
            dk_ref[h] = (dkk[:HEAD_DIM, :BAND] + dkk[HEAD_DIM:, BAND:]).T
            dvv = lax.dot_general(do2, pb, _TN, preferred_element_type=F32)
            dv_ref[h] = (dvv[:HEAD_DIM, :BAND] + dvv[HEAD_DIM:, BAND:]).T
            ds_ref[h] -= jnp.where(even, pv[:, 0:1].astype(F32) * delta[0], pv[:, BAND:BAND + 1].astype(F32) * delta[1])

    cur, prev = _attn2_specs(width)
    heads = pl.BlockSpec((BLOCK, dq), lambda i: (i, 0))
    band = pl.BlockSpec((N_KV_HEADS, None, BAND, HEAD_DIM), lambda i: (0, i, 0, 0))
    band_shape = jax.ShapeDtypeStruct((N_KV_HEADS, nb, BAND, HEAD_DIM), F32)
    sink = pl.BlockSpec((N_KV_HEADS, rows, PAIR), lambda i: (0, 0, 0))
    return pl.pallas_call(
        body, name="attn_bwd", grid=(nb,),
        out_shape=(jax.ShapeDtypeStruct((m, dq), BF16), band_shape, band_shape,
                   jax.ShapeDtypeStruct((N_KV_HEADS, rows, PAIR), F32)),
        in_specs=[cur, prev, pl.BlockSpec((N_KV_HEADS, rows, 2 * BAND), lambda i: (0, i, 0)), heads, heads],
        out_specs=(heads, band, band, sink), compiler_params=_params(("arbitrary",)),
    )(qkr, qkr, p, o, do)


def _rope_qk_bwd(dq, dkb, dvb, cos, sin):
    nb = dkb.shape[1]
    width = (N_Q_HEADS + 2 * N_KV_HEADS) * HEAD_DIM
    scale = HEAD_DIM ** -0.5
    nq, nk = N_Q_HEADS // 2, N_KV_HEADS // 2

    def body(dq_ref, kc_ref, kn_ref, vc_ref, vn_ref, c_ref, s_ref, o_ref):
        last = pl.program_id(0) == nb - 1
        c, s = c_ref[...], -s_ref[...]

        def band_sum(cur_ref, nxt_ref, t):
            return jnp.concatenate([cur_ref[2 * t + e, BLOCK:, :] + jnp.where(last, 0.0, nxt_ref[2 * t + e, :BLOCK, :])
                                    for e in range(2)], axis=1)

        cq, sq = c * scale, s * scale
        for t in range(nq):
            col = slice(t * PAIR, (t + 1) * PAIR)
            o_ref[:, col] = _rope_pair(dq_ref[:, col].astype(F32), cq, sq).astype(BF16)
        for t in range(nk):
            o_ref[:, (nq + t) * PAIR:(nq + t + 1) * PAIR] = _rope_pair(band_sum(kc_ref, kn_ref, t), c, s).astype(BF16)
            o_ref[:, (nq + nk + t) * PAIR:(nq + nk + t + 1) * PAIR] = band_sum(vc_ref, vn_ref, t).astype(BF16)

    tab = pl.BlockSpec((BLOCK, PAIR), lambda i: (i, 0))
    cur = pl.BlockSpec((N_KV_HEADS, None, BAND, HEAD_DIM), lambda i: (0, i, 0, 0))
    nxt = pl.BlockSpec((N_KV_HEADS, None, BAND, HEAD_DIM), lambda i: (0, jnp.minimum(i + 1, nb - 1), 0, 0))
    return pl.pallas_call(
        body, name="rope_qk_bwd", grid=(nb,), out_shape=jax.ShapeDtypeStruct((nb * BLOCK, width), BF16),
        in_specs=[pl.BlockSpec((BLOCK, N_Q_HEADS * HEAD_DIM), lambda i: (i, 0)), cur, nxt, cur, nxt, tab, tab],
        out_specs=pl.BlockSpec((BLOCK, width), lambda i: (i, 0)),
        compiler_params=_params(("parallel",)),
    )(dq, dkb, dkb, dvb, dvb, cos, sin)


def _tiles2d(r, c):
    tc = _pick(c, 2048, 128) if c % 128 == 0 else c
    tr = _pick(r, max(8, (1 << 20) // tc // 8 * 8), 8) if r % 8 == 0 else r
    return tr, tc


def _cast_bf16(name, w, place, wide):
    r, c = w.shape
    tr, tc = _tiles2d(r, c)
    if tr % 16:
        tr = r
    nc = c // tc

    def body(place_ref, w_ref, o_ref):
        o_ref[...] = w_ref[...].astype(BF16)

    if wide:
        out_shape = jax.ShapeDtypeStruct((r, N_CHIPS * c), BF16)
        out_spec = pl.BlockSpec((tr, tc), lambda i, j, p: (i, p[1] * nc + j))
    else:
        out_shape = jax.ShapeDtypeStruct((N_CHIPS, r, c), BF16)
        out_spec = pl.BlockSpec((None, tr, tc), lambda i, j, p: (p[1], i, j))
    return pl.pallas_call(
        body, name=name, out_shape=out_shape,
        grid_spec=pltpu.PrefetchScalarGridSpec(
            num_scalar_prefetch=1, grid=(r // tr, nc),
            in_specs=[pl.BlockSpec((tr, tc), lambda i, j, p: (i, j))], out_specs=out_spec),
        compiler_params=_params(("parallel", "parallel")),
    )(place, w)


def _pair_sum(name, g, got, place):
    n, r, c = g.shape
    half = r // 2
    tr, tc = _tiles2d(half, c)
    nh = half // tr

    def body(place_ref, g_ref, got_ref, o_ref, own_ref):
        s = (g_ref[...] + got_ref[...]).astype(BF16)
        o_ref[...] = s

        @pl.when(pl.program_id(2) == place_ref[1])
        def _():
            own_ref[...] = s

    tile = pl.BlockSpec((None, tr, tc), lambda i, j, k, p: (k, i, j))
    shape = jax.ShapeDtypeStruct((n, half, c), BF16)
    return pl.pallas_call(
        body, name=name, out_shape=(shape, shape),
        grid_spec=pltpu.PrefetchScalarGridSpec(
            num_scalar_prefetch=1, grid=(nh, c // tc, n),
            in_specs=[pl.BlockSpec((None, tr, tc), lambda i, j, k, p: (k, p[0] * nh + i, j)), tile],
            out_specs=(tile, pl.BlockSpec((None, tr, tc), lambda i, j, k, p: (p[1], i, j)))),
        compiler_params=_params(("parallel", "parallel", "arbitrary")),
    )(place, g, got)


def _chip_sum(name, parts, place):
    n, half, c = parts.shape
    tr, tc = _tiles2d(half, c)
    nh = half // tr

    def body(place_ref, p0, p1, p2, p3, o_ref):
        o_ref[...] = ((p0[...].astype(F32) + p1[...].astype(F32)) + p2[...].astype(F32)) + p3[...].astype(F32)

    def chip(k):
        return pl.BlockSpec((None, tr, tc), lambda i, j, p: (k, i, j))

    return pl.pallas_call(
        body, name=name, out_shape=jax.ShapeDtypeStruct((2 * half, c), F32),
        grid_spec=pltpu.PrefetchScalarGridSpec(
            num_scalar_prefetch=1, grid=(nh, c // tc),
            in_specs=[chip(k) for k in range(n)],
            out_specs=pl.BlockSpec((tr, tc), lambda i, j, p: (p[0] * nh + i, j))),
        compiler_params=_params(("parallel", "parallel")),
    )(place, parts, parts, parts, parts)


def _dev_sum(gathered):
    def body(g_ref, o_ref):
        acc = g_ref[0]
        for k in range(1, N_DEV):
            acc = acc + g_ref[k]
        o_ref[...] = acc

    return pl.pallas_call(body, name="dev_sum", out_shape=jax.ShapeDtypeStruct(gathered.shape[1:], F32))(gathered)


def _adamw(name, w, g, m, v):
    r, c = w.shape
    tr, tc = _tiles2d(r, c)
    if r % 8 == 0:
        tr = _pick(r, max(8, (1 << 18) // tc // 8 * 8), 8)

    def body(w_ref, g_ref, m_ref, v_ref, go_ref, d_ref, mo_ref, vo_ref):
        gv = g_ref[...]
        go_ref[...] = gv
        mn = ADAM_B1 * m_ref[...] + (1.0 - ADAM_B1) * gv
        vn = ADAM_B2 * v_ref[...] + (1.0 - ADAM_B2) * jnp.square(gv)
        m_hat = mn / (1.0 - ADAM_B1 ** ADAM_STEP)
        v_hat = vn / (1.0 - ADAM_B2 ** ADAM_STEP)
        d_ref[...] = -ADAM_LR * (m_hat / (jnp.sqrt(v_hat) + ADAM_EPS) + ADAM_WD * w_ref[...])
        mo_ref[...] = mn
        vo_ref[...] = vn

    tile = pl.BlockSpec((tr, tc), lambda i, j: (i, j))
    shape = jax.ShapeDtypeStruct((r, c), F32)
    return pl.pallas_call(
        body, name=name, grid=(r // tr, c // tc), out_shape=(shape,) * 4,
        in_specs=[tile] * 4, out_specs=(tile,) * 4, compiler_params=_params(("parallel", "parallel")),
    )(w, g, m, v)


MATRICES = ("w_in_conv", "w_out_conv", "w_up_0", "w_down_0", "w_qkv", "w_o", "w_up_1", "w_down_1")
COLUMN_SHARDED = ("w_in_conv", "w_up_0", "w_qkv", "w_up_1")
NORMS = ("norm_mix_0", "norm_mlp_0", "norm_mix_1", "norm_mlp_1", "norm_final")


def _rows(stack):
    return stack.reshape(N_CHIPS * stack.shape[1], stack.shape[2])


def _stack(full):
    return full.reshape(N_CHIPS, full.shape[0] // N_CHIPS, full.shape[1])


def _add_residual(acc, res):
    return acc + res


def _relu_sq_grad(acc, z):
    return acc * (2.0 * jnp.maximum(z.astype(F32), 0.0))


def _step(x, target, stacks, small, norms, sinks, place, update):
    d = D_MODEL
    dc = d // N_CHIPS
    pad = ROW0 - N_META
    m = x.shape[0] + ROW0
    grp = N_Q_HEADS // N_KV_HEADS
    cos, sin = _rope_tables(m)
    pairs = grp // 2
    sink2 = jnp.broadcast_to(sinks.astype(F32).reshape(N_KV_HEADS, pairs, 1, 2, 1),
                             (N_KV_HEADS, pairs, BLOCK, 2, HEAD_DIM)).reshape(N_KV_HEADS, pairs * BLOCK, PAIR)

    def gather(*names):
        return _gather_task([stacks[n] for n in names])

    def pair_sum(tag, grad, got):
        return _pair_sum("pair_sum_" + tag, grad, got, place)

    def chip_sum(tag, landed):
        return _chip_sum("chip_sum_" + tag, landed, place)

    (w_in, small_all), = _run("gather_first", [_gather_task([stacks["w_in_conv"], small])])
    small_full = jnp.transpose(small_all, (1, 0, 2)).reshape(SMALL_ROWS, d)
    conv_w8 = small_full[N_META:N_META + 8]
    h0 = jnp.concatenate([jnp.zeros((pad, d), F32), small_full[:N_META], x], axis=0)

    n0 = _rms_fwd("norm_mix_0", h0, norms["norm_mix_0"])
    bcu, ((w_out, w_up0),) = _mm_nn("conv_in", n0, w_in, BF16, tasks=[gather("w_out_conv", "w_up_0")])
    gate = _conv_fwd(bcu, conv_w8)
    h1, n1 = _mm_nn("conv_out", gate, _rows(w_out), F32, epi=_add_residual, extras=(h0,), norm_gain=norms["norm_mlp_0"])
    z0, ((w_down0,),) = _mm_nn("mlp_up_0", n1, w_up0, BF16, tasks=[gather("w_down_0")])
    h2, ((w_qkv, w_o, w_up1),) = _mm_nn("mlp_down_0", z0, _rows(w_down0), F32, a_pro=_relu_sq, epi=_add_residual,
                                             extras=(h1,), tasks=[gather("w_qkv", "w_o", "w_up_1")])
    n2 = _rms_fwd("norm_mix_1", h2, norms["norm_mix_1"])
    qkv = _mm_nn("attn_qkv", n2, w_qkv, BF16)
    qkr = _rope_qk(qkv, cos, sin)
    o, probs = _attn2_fwd(qkr, sink2)
    h3, n3 = _mm_nn("attn_out", o, _rows(w_o), F32, epi=_add_residual, extras=(h2,), norm_gain=norms["norm_mlp_1"])
    z1, ((w_down1,),) = _mm_nn("mlp_up_1", n3, w_up1, BF16, tasks=[gather("w_down_1")])
    h4 = _mm_nn("mlp_down_1", z1, _rows(w_down1), F32, a_pro=_relu_sq, epi=_add_residual, extras=(h3,))

    gn = {}
    loss, dh, dh_bf, gn["norm_final"] = _loss_head(h4, norms["norm_final"], target)
    dz = _mm_nt("mlp_down_dx_1", dh_bf, _rows(w_down1), BF16, epi=_relu_sq_grad, extras=(z1,))
    g_d1 = _stack(_mm_tn("mlp_down_dw_1", z1, dh_bf, stacked=False, a_pro=_relu_sq))
    g_u1, ((got,),) = _mm_tn("mlp_up_dw_1", n3, dz, stacked=True, tasks=[_pair_exchange_task([g_d1])])
    s_d1 = pair_sum("d1", g_d1, got)
    dn, ((got,), (landed,)) = _mm_nt("mlp_up_dx_1", dz, w_up1, BF16,
                                          tasks=[_pair_exchange_task([g_u1]), _chip_exchange_task([s_d1])])
    s_u1, b_d1 = pair_sum("u1", g_u1, got), chip_sum("d1", landed)
    dh, dh_bf, gn["norm_mlp_1"] = _rms_bwd("norm_mlp_bwd_1", dn, h3, norms["norm_mlp_1"], dh)
    do = _mm_nt("attn_out_dx", dh_bf, _rows(w_o), BF16)
    g_o = _stack(_mm_tn("attn_out_dw", o, dh_bf, stacked=False))
    dq, dkb, dvb, dsink = _attn2_bwd(qkr, probs, o, do)
    dqkv = _rope_qk_bwd(dq, dkb, dvb, cos, sin)
    g_qkv, ((got,),) = _mm_tn("attn_qkv_dw", n2, dqkv, stacked=True, tasks=[_pair_exchange_task([g_o])])
    s_o = pair_sum("o", g_o, got)
    dn, ((got,), (landed,)) = _mm_nt("attn_qkv_dx", dqkv, w_qkv, BF16,
                                          tasks=[_pair_exchange_task([g_qkv]), _chip_exchange_task([s_u1])])
    s_qkv, b_u1 = pair_sum("qkv", g_qkv, got), chip_sum("u1", landed)
    dh, dh_bf, gn["norm_mix_1"] = _rms_bwd("norm_mix_bwd_1", dn, h2, norms["norm_mix_1"], dh)
    dz, ((landed_o, landed_qkv), (r_d1,)) = _mm_nt(
        "mlp_down_dx_0", dh_bf, _rows(w_down0), BF16, epi=_relu_sq_grad, extras=(z0,),
        tasks=[_chip_exchange_task([s_o, s_qkv]), _pair_share_task([b_d1])])
    b_o, b_qkv = chip_sum("o", landed_o), chip_sum("qkv", landed_qkv)
    g_d0, ((r_u1,),) = _mm_tn("mlp_down_dw_0", z0, dh_bf, stacked=False, a_pro=_relu_sq, tasks=[_pair_share_task([b_u1])])
    g_d0 = _stack(g_d0)
    g_u0, ((got,), (r_o, r_qkv)) = _mm_tn("mlp_up_dw_0", n1, dz, stacked=True,
                                          tasks=[_pair_exchange_task([g_d0]), _pair_share_task([b_o, b_qkv])])
    s_d0 = pair_sum("d0", g_d0, got)
    dn, ((got,), (landed,)) = _mm_nt("mlp_up_dx_0", dz, w_up0, BF16,
                                          tasks=[_pair_exchange_task([g_u0]), _chip_exchange_task([s_d0])])
    s_u0, b_d0 = pair_sum("u0", g_u0, got), chip_sum("d0", landed)
    dh, dh_bf, gn["norm_mlp_0"] = _rms_bwd("norm_mlp_bwd_0", dn, h1, norms["norm_mlp_0"], dh)
    dgate = _mm_nt("conv_out_dx", dh_bf, _rows(w_out), BF16)
    dbcu, g_conv_w = _conv_bwd(bcu, conv_w8, dgate)
    g_in, ((landed,), (r_d0,)) = _mm_tn("conv_in_dw", n0, dbcu, stacked=True,
                                        tasks=[_chip_exchange_task([s_u0]), _pair_share_task([b_d0])])
    b_u0 = chip_sum("u0", landed)
    dn, ((got,), (r_u0,)) = _mm_nt("conv_in_dx", dbcu, w_in, BF16,
                                        tasks=[_pair_exchange_task([g_in]), _pair_share_task([b_u0])])
    s_in = pair_sum("in", g_in, got)
    grad_x, dh_first, gn["norm_mix_0"] = _rms_bwd_tokens("norm_mix_bwd_0", dn, h0, norms["norm_mix_0"], dh)

    g_small = jnp.zeros((SMALL_ROWS, d), F32).at[:N_META].set(dh_first[pad:ROW0]).at[N_META:N_META + 8].set(g_conv_w)
    g_small = jnp.transpose(g_small.reshape(SMALL_ROWS, N_CHIPS, dc), (1, 0, 2))
    rep = jnp.zeros((8, d), F32)
    for r, n in enumerate(NORMS):
        rep = rep.at[r].set(jnp.sum(gn[n], axis=0))
    dsink = jnp.sum(dsink.reshape(N_KV_HEADS, pairs, BLOCK, 2, HEAD_DIM)[..., 0], axis=2)
    rep = rep.at[len(NORMS), :N_Q_HEADS].set(dsink.reshape(N_Q_HEADS))
    g_out, ((landed,), (got, rep_all)) = _mm_tn(
        "conv_out_dw", gate, dh_bf, stacked=False,
        tasks=[_chip_exchange_task([s_in]), _pair_exchange_task([g_small], small=rep)])
    g_out = _stack(g_out)
    b_in, s_small = chip_sum("in", landed), pair_sum("small", g_small, got)
    (got,), = _run("tail_pair_exchange", [_pair_exchange_task([g_out])])
    s_out = pair_sum("out", g_out, got)
    (landed_out, landed_small), = _run("tail_chip_exchange", [_chip_exchange_task([s_out, s_small])])
    b_out, b_small = chip_sum("out", landed_out), chip_sum("small", landed_small)
    (r_out, r_small, r_in), = _run("tail_pair_share", [_pair_share_task([b_out, b_small, b_in])])
    for n, r in (("w_down_1", r_d1), ("w_up_1", r_u1), ("w_down_0", r_d0), ("w_up_0", r_u0), ("w_o", r_o),
                 ("w_qkv", r_qkv), ("w_out_conv", r_out), ("w_in_conv", r_in)):
        update(n, r)
    return loss, grad_x, r_small, rep_all


def kernel(x, meta_tokens, norm_mix_0, w_in_conv, conv_w, w_out_conv, norm_mlp_0, w_up_0, w_down_0, norm_mix_1, w_qkv, attn_sinks, w_o, norm_mlp_1, w_up_1, w_down_1, norm_final, loss_target, m_meta_tokens, m_norm_mix_0, m_w_in_conv, m_conv_w, m_w_out_conv, m_norm_mlp_0, m_w_up_0, m_w_down_0, m_norm_mix_1, m_w_qkv, m_attn_sinks, m_w_o, m_norm_mlp_1, m_w_up_1, m_w_down_1, m_norm_final, v_meta_tokens, v_norm_mix_0, v_w_in_conv, v_conv_w, v_w_out_conv, v_norm_mlp_0, v_w_up_0, v_w_down_0, v_norm_mix_1, v_w_qkv, v_attn_sinks, v_w_o, v_norm_mlp_1, v_w_up_1, v_w_down_1, v_norm_final):
    given = dict(locals())
    names = ("meta_tokens", "norm_mix_0", "w_in_conv", "conv_w", "w_out_conv", "norm_mlp_0", "w_up_0", "w_down_0",
             "norm_mix_1", "w_qkv", "attn_sinks", "w_o", "norm_mlp_1", "w_up_1", "w_down_1", "norm_final")
    d = D_MODEL
    dc = d // N_CHIPS
    chip = 2 * lax.axis_index("x") + lax.axis_index("y")
    place = jnp.stack([lax.axis_index("c"), chip]).astype(jnp.int32)

    small = jnp.zeros((SMALL_ROWS, dc), F32).at[:N_META].set(meta_tokens).at[N_META:N_META + CONV_WIDTH].set(conv_w)
    small = lax.dynamic_update_slice(jnp.zeros((N_CHIPS, SMALL_ROWS, dc), F32), small[None], (chip, 0, 0))
    stacks = {n: _cast_bf16("cast_" + n, given[n], place, n in COLUMN_SHARDED) for n in MATRICES}

    g_out, delta, new_m, new_v = {}, {}, {}, {}

    def update(n, grad):
        wt = given[n]
        shape2 = wt.shape if wt.ndim == 2 else (1, wt.shape[0])
        outs = _adamw("adamw_" + n, wt.reshape(shape2), grad.reshape(shape2),
                      given["m_" + n].reshape(shape2), given["v_" + n].reshape(shape2))
        g_out[n], delta[n], new_m[n], new_v[n] = [o.reshape(wt.shape) for o in outs]

    norms = {n: given[n] for n in NORMS}
    loss_part, grad_x, r_small, rep_all = _step(x[0], loss_target[0], stacks, small, norms, attn_sinks, place, update)
    loss = lax.psum(loss_part[0, 0], ("x", "y", "c"))
    rep_sum = _dev_sum(rep_all)
    update("meta_tokens", r_small[:N_META])
    update("conv_w", r_small[N_META:N_META + CONV_WIDTH])
    for r, n in enumerate(NORMS):
        update(n, rep_sum[r])
    update("attn_sinks", rep_sum[len(NORMS), :N_Q_HEADS])
    return (loss, grad_x[None], *[g_out[n] for n in names], *[delta[n] for n in names],
            *[new_m[n] for n in names], *[new_v[n] for n in names])
```

```python
import jax
import jax.numpy as jnp
from jax import lax
from jax.experimental import pallas as pl
from jax.experimental.pallas import tpu as pltpu

F32 = jnp.float32
BF16 = jnp.bfloat16

D_MODEL = 2048
SEQ = 8192
N_META = 16
CONV_WIDTH = 3
HEAD_DIM = 64
N_Q_HEADS = 32
N_KV_HEADS = 4
BLOCK = 128
ROPE_THETA = 10000.0
D_FF = 4 * D_MODEL
RMS_EPS = 1e-5
NEG_INF = -1e30

ADAM_LR = 0.001
ADAM_B1 = 0.9
ADAM_B2 = 0.999
ADAM_EPS = 1e-08
ADAM_WD = 0.01
ADAM_STEP = 10

N_CHIPS = 4
N_DEV = 8
MESH = pl.DeviceIdType.MESH
VMEM_LIMIT = 56 * 1024 * 1024
SMALL_ROWS = 32
ROW0 = BLOCK


def _pick(n, target, mult):
    best = None
    for t in range(mult, min(n, target) + 1, mult):
        if n % t == 0:
            best = t
    assert best is not None, (n, target, mult)
    return best


def _params(sem=None):
    return pltpu.CompilerParams(dimension_semantics=sem, vmem_limit_bytes=VMEM_LIMIT)


HBM_SPEC = pl.BlockSpec(memory_space=pltpu.HBM)


class _Task:
    def __init__(self, inputs, outputs, aliases, sem_shapes, bind):
        self.inputs, self.outputs, self.aliases = list(inputs), list(outputs), dict(aliases)
        self.sem_shapes, self.bind = list(sem_shapes), bind


def _like(arrays):
    return [jax.ShapeDtypeStruct(a.shape, a.dtype) for a in arrays]


def _bind_tasks(tasks, in_refs, out_refs, sem_refs):
    bound, i, o, s = [], 0, 0, 0
    for t in tasks:
        ni, no, ns = len(t.inputs), len(t.outputs), len(t.sem_shapes)
        bound.append(t.bind(in_refs[i:i + ni], out_refs[o:o + no], sem_refs[s:s + ns]))
        i, o, s = i + ni, o + no, s + ns
    return bound


def _run_phase(bound, phase):
    for b in bound:
        if b[phase] is not None:
            b[phase]()


def _task_plumbing(tasks, in_offset, out_offset):
    ins = [a for t in tasks for a in t.inputs]
    outs = [o for t in tasks for o in t.outputs]
    sems = [s for t in tasks for s in t.sem_shapes]
    aliases, i, o = {}, in_offset, out_offset
    for t in tasks:
        for src, dst in t.aliases.items():
            aliases[i + src] = o + dst
        i, o = i + len(t.inputs), o + len(t.outputs)
    return ins, outs, sems, aliases


def _split_outputs(tasks, flat):
    res, o = [], 0
    for t in tasks:
        res.append(list(flat[o:o + len(t.outputs)]))
        o += len(t.outputs)
    return res


def _run(name, tasks):
    ins, outs, sems, aliases = _task_plumbing(tasks, 0, 0)

    def body(*refs):
        bound = _bind_tasks(tasks, refs[:len(ins)], refs[len(ins):len(ins) + len(outs)], refs[len(ins) + len(outs):])
        for phase in range(3):
            _run_phase(bound, phase)

    flat = pl.pallas_call(
        body, name=name, out_shape=outs, in_specs=[HBM_SPEC] * len(ins), out_specs=[HBM_SPEC] * len(outs),
        input_output_aliases=aliases, scratch_shapes=sems,
    )(*ins)
    return _split_outputs(tasks, flat)


def _place():
    x, y, c = lax.axis_index("x"), lax.axis_index("y"), lax.axis_index("c")
    chips = [(1 - x, y), (x, 1 - y), (1 - x, 1 - y)]
    return x, y, c, 2 * x + y, chips


def _gather_task(stacks):
    n = len(stacks)
    halves = [s.shape[-2] // 2 for s in stacks]

    def bind(_, dst, sems):
        send_a, recv_a, send_b, recv_b = sems
        x, y, c, me, chips = _place()
        sibling = (x, y, 1 - c)

        def half(w, chip, hc):
            rows = pl.ds(hc * halves[w], halves[w])
            if len(stacks[w].shape) == 3:
                return dst[w].at[chip, rows, :]
            cols = stacks[w].shape[1] // N_CHIPS
            return dst[w].at[rows, pl.ds(chip * cols, cols)]

        def over_ici(j, w, block):
            return pltpu.make_async_remote_copy(
                src_ref=half(w, block, c), dst_ref=half(w, block, c), send_sem=send_a.at[j * n + w],
                recv_sem=recv_a.at[j * n + w], device_id=(*chips[j], c), device_id_type=MESH)

        def over_d2d(j, w, hc):
            got = half(w, 2 * chips[j][0] + chips[j][1], hc)
            return pltpu.make_async_remote_copy(
                src_ref=got, dst_ref=got, send_sem=send_b.at[j * n + w], recv_sem=recv_b.at[j * n + w],
                device_id=sibling, device_id_type=MESH)

        pairs = [(j, w) for j in range(3) for w in range(n)]

        def start():
            for j, w in pairs:
                over_ici(j, w, me).start()

        def mid():
            for j, w in pairs:
                over_ici(j, w, 2 * chips[j][0] + chips[j][1]).wait_recv()
                over_d2d(j, w, c).start()

        def finish():
            for j, w in pairs:
                over_d2d(j, w, 1 - c).wait_recv()
            for j, w in pairs:
                over_ici(j, w, me).wait_send()
                over_d2d(j, w, c).wait_send()

        return start, mid, finish

    return _Task(stacks, _like(stacks), {w: w for w in range(n)}, [pltpu.SemaphoreType.DMA((3 * n,))] * 4, bind)


def _pair_exchange_task(grads, small=None):
    n = len(grads)
    halves = [g.shape[1] // 2 for g in grads]

    def bind(src, dst, sems):
        send, recv = sems[0], sems[1]
        x, y, c, me, _ = _place()
        sibling = (x, y, 1 - c)
        dev = 2 * me + c

        def to_sibling(w):
            return pltpu.make_async_remote_copy(
                src_ref=src[w].at[:, pl.ds((1 - c) * halves[w], halves[w]), :], dst_ref=dst[w],
                send_sem=send.at[w], recv_sem=recv.at[w], device_id=sibling, device_id_type=MESH)

        def to_peer(t, block):
            tx, ty, tc = (t >> 2) & 1, (t >> 1) & 1, t & 1
            return pltpu.make_async_remote_copy(
                src_ref=src[n], dst_ref=dst[n].at[block], send_sem=sems[2].at[t], recv_sem=sems[3].at[t],
                device_id=(x ^ tx, y ^ ty, c ^ tc), device_id_type=MESH)

        def mine():
            return pltpu.make_async_copy(src[n], dst[n].at[dev], sems[4])

        def start():
            for w in range(n):
                to_sibling(w).start()
            if small is not None:
                mine().start()
                for t in range(1, N_DEV):
                    to_peer(t, dev).start()

        def finish():
            for w in range(n):
                to_sibling(w).wait_recv()
            if small is not None:
                for t in range(1, N_DEV):
                    to_peer(t, dev ^ t).wait_recv()
            for w in range(n):
                to_sibling(w).wait_send()
            if small is not None:
                for t in range(1, N_DEV):
                    to_peer(t, dev).wait_send()
                mine().wait()

        return start, None, finish

    outputs = [jax.ShapeDtypeStruct((N_CHIPS, h, g.shape[2]), g.dtype) for g, h in zip(grads, halves)]
    sem_shapes = [pltpu.SemaphoreType.DMA((n,)), pltpu.SemaphoreType.DMA((n,))]
    inputs = list(grads)
    if small is not None:
        inputs.append(small)
        outputs.append(jax.ShapeDtypeStruct((N_DEV,) + small.shape, small.dtype))
        sem_shapes += [pltpu.SemaphoreType.DMA((N_DEV,)), pltpu.SemaphoreType.DMA((N_DEV,)), pltpu.SemaphoreType.DMA(())]
    return _Task(inputs, outputs, {}, sem_shapes, bind)


def _chip_exchange_task(summed):
    n = len(summed)

    def bind(refs, dst, sems):
        src = refs[:n]
        send, recv = sems
        x, y, c, me, chips = _place()

        def copy(j, w, block_from, block_to):
            return pltpu.make_async_remote_copy(
                src_ref=src[w].at[block_from], dst_ref=dst[w].at[block_to], send_sem=send.at[j * n + w],
                recv_sem=recv.at[j * n + w], device_id=(*chips[j], c), device_id_type=MESH)

        pairs = [(j, w) for j in range(3) for w in range(n)]

        def start():
            for j, w in pairs:
                copy(j, w, 2 * chips[j][0] + chips[j][1], me).start()

        def finish():
            for j, w in pairs:
                copy(j, w, me, 2 * chips[j][0] + chips[j][1]).wait_recv()
            for j, w in pairs:
                copy(j, w, 2 * chips[j][0] + chips[j][1], me).wait_send()

        return start, None, finish

    partials, landing = [s[0] for s in summed], [s[1] for s in summed]
    return _Task(partials + landing, _like(landing), {n + w: w for w in range(n)},
                 [pltpu.SemaphoreType.DMA((3 * n,))] * 2, bind)


def _pair_share_task(blocks):
    n = len(blocks)

    def bind(_, dst, sems):
        send, recv = sems
        x, y, c, _, _ = _place()

        def copy(w, hc):
            h = blocks[w].shape[0] // 2
            rows = dst[w].at[pl.ds(hc * h, h), :]
            return pltpu.make_async_remote_copy(src_ref=rows, dst_ref=rows, send_sem=send.at[w], recv_sem=recv.at[w],
                                                device_id=(x, y, 1 - c), device_id_type=MESH)

        def start():
            for w in range(n):
                copy(w, c).start()

        def finish():
            for w in range(n):
                copy(w, 1 - c).wait_recv()
            for w in range(n):
                copy(w, c).wait_send()

        return start, None, finish

    return _Task(blocks, _like(blocks), {w: w for w in range(n)}, [pltpu.SemaphoreType.DMA((n,))] * 2, bind)


def _carrier_call(name, body, operands, *, grid, in_specs, out_specs, out_shape, semantics, tasks):
    n_in, n_out = len(operands), len(out_shape)
    t_ins, t_outs, t_sems, aliases = _task_plumbing(tasks, n_in, n_out)
    n_ti, n_to = len(t_ins), len(t_outs)
    total = 1
    for g in grid:
        total *= g
    mid_step = max(0, total - 1 - max(1, total // 8))

    def carrier(*refs):
        outs_at = n_in + n_ti
        if tasks:
            bound = _bind_tasks(tasks, refs[n_in:outs_at], refs[outs_at + n_out:outs_at + n_out + n_to],
                                refs[outs_at + n_out + n_to:])
            step = 0
            for axis, g in enumerate(grid):
                step = step * g + pl.program_id(axis)

            @pl.when(step == 0)
            def _():
                _run_phase(bound, 0)

        body(*refs[:n_in], *refs[outs_at:outs_at + n_out])

        if tasks:
            @pl.when(step == mid_step)
            def _():
                _run_phase(bound, 1)

            @pl.when(step == total - 1)
            def _():
                _run_phase(bound, 2)

    res = pl.pallas_call(
        carrier, name=name, grid=grid, out_shape=[*out_shape, *t_outs],
        in_specs=[*in_specs, *[HBM_SPEC] * n_ti], out_specs=[*out_specs, *[HBM_SPEC] * n_to],
        input_output_aliases=aliases, scratch_shapes=t_sems,
        compiler_params=_params(("arbitrary",) * len(grid) if tasks else semantics),
    )(*operands, *t_ins)
    return list(res[:n_out]), _split_outputs(tasks, res[n_out:])


def _mm(name, a, b, *, dims, grid, a_spec, b_spec, out_shape, out_spec,
        extras=(), extra_specs=(), a_pro=None, epi=None, norm_gain=None, tasks=()):
    n_ex = len(extras)
    normed = norm_gain is not None

    def body(a_ref, b_ref, *rest):
        outs = rest[n_ex + normed:]
        av = a_ref[...]
        if a_pro is not None:
            av = a_pro(av)
        acc = lax.dot_general(av, b_ref[...], dims, preferred_element_type=F32)
        if epi is not None:
            acc = epi(acc, *[e[...] for e in rest[:n_ex]])
        outs[0][...] = acc.astype(outs[0].dtype)
        if normed:
            rstd = lax.rsqrt(jnp.mean(acc * acc, axis=-1, keepdims=True) + RMS_EPS)
            outs[1][...] = ((acc * rstd) * rest[n_ex][...]).astype(BF16)

    operands, in_specs = [a, b, *extras], [a_spec, b_spec, *extra_specs]
    out_specs, out_shapes = [out_spec], [out_shape]
    if normed:
        width = out_shape.shape[1]
        operands.append(norm_gain.reshape(1, width))
        in_specs.append(pl.BlockSpec((1, width), lambda j, i: (0, 0)))
        out_specs.append(out_spec)
        out_shapes.append(jax.ShapeDtypeStruct(out_shape.shape, BF16))
    res, carried = _carrier_call(name, body, operands, grid=grid, in_specs=in_specs, out_specs=out_specs,
                                 out_shape=out_shapes, semantics=("parallel", "parallel"), tasks=tasks)
    res = tuple(res) if normed else res[0]
    return (res, carried) if tasks else res


_NN = (((1,), (0,)), ((), ()))
_NT = (((1,), (1,)), ((), ()))
_TN = (((0,), (0,)), ((), ()))


MM_TILE_BUDGET = 46 * 1024 * 1024


def _mm_tiles(m, n, contraction, out_bytes):
    for rows, cols in ((1664, 1024), (832, 1024), (416, 1024), (416, 512)):
        tm, tn = _pick(m, rows, 16), _pick(n, cols, 128)
        if 2 * 2 * contraction * (tm + tn) + tm * tn * (4 + 2 * out_bytes) <= MM_TILE_BUDGET:
            break
    return tm, tn


def _out_bytes(out_dtype, extras):
    return jnp.dtype(out_dtype).itemsize + sum(e.dtype.itemsize for e in extras)


def _mm_nn(name, a, w, out_dtype, a_pro=None, epi=None, extras=(), norm_gain=None, tasks=()):
    m, k = a.shape
    _, n = w.shape
    tm, tn = _mm_tiles(m, n, k, _out_bytes(out_dtype, extras))
    if norm_gain is not None:
        tm, tn = _pick(m, 416, 16), n
    tile = pl.BlockSpec((tm, tn), lambda j, i: (i, j))
    return _mm(name, a, w, dims=_NN, grid=(n // tn, m // tm),
               a_spec=pl.BlockSpec((tm, k), lambda j, i: (i, 0)), b_spec=pl.BlockSpec((k, tn), lambda j, i: (0, j)),
               out_shape=jax.ShapeDtypeStruct((m, n), out_dtype), out_spec=tile,
               extras=extras, extra_specs=[tile] * len(extras), a_pro=a_pro, epi=epi, norm_gain=norm_gain, tasks=tasks)


def _mm_nt(name, a, w, out_dtype, epi=None, extras=(), tasks=()):
    m, c = a.shape
    r, _ = w.shape
    tm, tn = _mm_tiles(m, r, c, _out_bytes(out_dtype, extras))
    tile = pl.BlockSpec((tm, tn), lambda j, i: (i, j))
    return _mm(name, a, w, dims=_NT, grid=(r // tn, m // tm),
               a_spec=pl.BlockSpec((tm, c), lambda j, i: (i, 0)), b_spec=pl.BlockSpec((tn, c), lambda j, i: (j, 0)),
               out_shape=jax.ShapeDtypeStruct((m, r), out_dtype), out_spec=tile,
               extras=extras, extra_specs=[tile] * len(extras), epi=epi, tasks=tasks)


def _mm_tn(name, a, b, stacked, a_pro=None, tasks=()):
    t, ka = a.shape
    _, nb = b.shape
    ns = nb // N_CHIPS if stacked else nb
    ta, tb = _pick(ka, 512, 128), _pick(ns, 640, 128)
    if stacked:
        per = ns // tb
        out_shape = jax.ShapeDtypeStruct((N_CHIPS, ka, ns), BF16)
        out_spec = pl.BlockSpec((None, ta, tb), lambda i, j: (j // per, i, j % per))
    else:
        out_shape = jax.ShapeDtypeStruct((ka, nb), BF16)
        out_spec = pl.BlockSpec((ta, tb), lambda i, j: (i, j))
    return _mm(name, a, b, dims=_TN, grid=(ka // ta, nb // tb),
               a_spec=pl.BlockSpec((t, ta), lambda i, j: (0, i)), b_spec=pl.BlockSpec((t, tb), lambda i, j: (0, j)),
               out_shape=out_shape, out_spec=out_spec, a_pro=a_pro, tasks=tasks)


def _relu_sq(z):
    a = jnp.maximum(z, 0)
    return a * a


def _rms_fwd(name, h, g):
    m, d = h.shape
    tr = _pick(m, 512, 16)

    def body(h_ref, g_ref, o_ref):
        x = h_ref[...]
        rstd = lax.rsqrt(jnp.mean(x * x, axis=-1, keepdims=True) + RMS_EPS)
        o_ref[...] = ((x * rstd) * g_ref[...]).astype(BF16)

    row = pl.BlockSpec((tr, d), lambda i: (i, 0))
    return pl.pallas_call(
        body, name=name, grid=(m // tr,), out_shape=jax.ShapeDtypeStruct((m, d), BF16),
        in_specs=[row, pl.BlockSpec((1, d), lambda i: (0, 0))], out_specs=row,
        compiler_params=_params(("parallel",)),
    )(h, g.reshape(1, d))


def _rms_bwd_math(x, g, dn):
    rstd = lax.rsqrt(jnp.mean(x * x, axis=-1, keepdims=True) + RMS_EPS)
    xhat = x * rstd
    dxhat = dn * g
    dx = rstd * (dxhat - xhat * jnp.mean(dxhat * xhat, axis=-1, keepdims=True))
    return dx, dn * xhat


def _fold8(v):
    r, c = v.shape
    return jnp.sum(v.reshape(r // 8, 8, c), axis=0)


def _rms_bwd(name, dn, h, g, dh_in):
    m, d = h.shape
    tr = _pick(m, 512, 16)
    nt = m // tr

    def body(dn_ref, h_ref, g_ref, dh_ref, o_ref, ob_ref, dg_ref):
        dx, dgp = _rms_bwd_math(h_ref[...], g_ref[...], dn_ref[...].astype(F32))
        dh = dh_ref[...] + dx
        o_ref[...] = dh
        ob_ref[...] = dh.astype(BF16)

        @pl.when(pl.program_id(0) == 0)
        def _():
            dg_ref[...] = jnp.zeros_like(dg_ref)

        dg_ref[...] += _fold8(dgp)

    row = pl.BlockSpec((tr, d), lambda i: (i, 0))
    return pl.pallas_call(
        body, name=name, grid=(nt,),
        out_shape=(jax.ShapeDtypeStruct((m, d), F32), jax.ShapeDtypeStruct((m, d), BF16),
                   jax.ShapeDtypeStruct((8, d), F32)),
        in_specs=[row, row, pl.BlockSpec((1, d), lambda i: (0, 0)), row],
        out_specs=(row, row, pl.BlockSpec((8, d), lambda i: (0, 0))),
        compiler_params=_params(("arbitrary",)),
    )(dn, h, g.reshape(1, d), dh_in)


def _rms_bwd_tokens(name, dn, h, g, dh_in):
    m, d = h.shape
    nb = m // BLOCK

    def body(dn_ref, h_ref, g_ref, dh_ref, gx_ref, first_ref, dg_ref):
        i = pl.program_id(0)
        dx, dgp = _rms_bwd_math(h_ref[...], g_ref[...], dn_ref[...].astype(F32))
        dh = dh_ref[...] + dx
        gx_ref[...] = dh

        @pl.when(i == 0)
        def _():
            first_ref[...] = dh
            dg_ref[...] = jnp.zeros_like(dg_ref)

        dg_ref[...] += _fold8(dgp)

    row = pl.BlockSpec((BLOCK, d), lambda i: (i, 0))
    return pl.pallas_call(
        body, name=name, grid=(nb,),
        out_shape=(jax.ShapeDtypeStruct((m - ROW0, d), F32), jax.ShapeDtypeStruct((ROW0, d), F32),
                   jax.ShapeDtypeStruct((8, d), F32)),
        in_specs=[row, row, pl.BlockSpec((1, d), lambda i: (0, 0)), row],
        out_specs=(pl.BlockSpec((BLOCK, d), lambda i: (jnp.maximum(i - 1, 0), 0)),
                   pl.BlockSpec((ROW0, d), lambda i: (0, 0)), pl.BlockSpec((8, d), lambda i: (0, 0))),
        compiler_params=_params(("arbitrary",)),
    )(dn, h, g.reshape(1, d), dh_in)


def _loss_head(h, g, target):
    m, d = h.shape
    tr = BLOCK

    def body(h_ref, g_ref, t_ref, loss_ref, o_ref, ob_ref, dg_ref):
        i = pl.program_id(0)
        x = h_ref[...]
        gv = g_ref[...]
        rstd = lax.rsqrt(jnp.mean(x * x, axis=-1, keepdims=True) + RMS_EPS)
        err = jnp.where(i > 0, (x * rstd) * gv - t_ref[...], 0.0)
        dx, dgp = _rms_bwd_math(x, gv, err * (1.0 / d))
        o_ref[...] = dx
        ob_ref[...] = dx.astype(BF16)

        @pl.when(i == 0)
        def _():
            dg_ref[...] = jnp.zeros_like(dg_ref)
            loss_ref[...] = jnp.zeros_like(loss_ref)

        dg_ref[...] += _fold8(dgp)
        sq = jnp.mean(err * err, axis=-1, keepdims=True)
        loss_ref[...] += 0.5 * jnp.sum(sq, axis=0, keepdims=True)

    row = pl.BlockSpec((tr, d), lambda i: (i, 0))
    return pl.pallas_call(
        body, name="loss_head", grid=(m // tr,),
        out_shape=(jax.ShapeDtypeStruct((8, 128), F32), jax.ShapeDtypeStruct((m, d), F32),
                   jax.ShapeDtypeStruct((m, d), BF16), jax.ShapeDtypeStruct((8, d), F32)),
        in_specs=[row, pl.BlockSpec((1, d), lambda i: (0, 0)),
                  pl.BlockSpec((tr, d), lambda i: (jnp.maximum(i - 1, 0), 0))],
        out_specs=(pl.BlockSpec((8, 128), lambda i: (0, 0)), row, row,
                   pl.BlockSpec((8, d), lambda i: (0, 0))),
        compiler_params=_params(("arbitrary",)),
    )(h, g.reshape(1, d), target)


HALO = 16


def _shift_down(cat, k):
    return pltpu.roll(cat, k, axis=0)[HALO:]


def _shift_up(cat, k):
    n = cat.shape[0]
    return pltpu.roll(cat, n - k, axis=0)[:n - HALO]


def _conv_fwd(bcu, cw):
    m, d3 = bcu.shape
    d = d3 // 3
    tr, tc = _pick(m, 416, 16), _pick(d, 512, 128)
    hb = tr // HALO

    def body(x_ref, xb_ref, w_ref, o_ref):
        i = pl.program_id(0)
        for j in range(d // tc):
            col = slice(j * tc, (j + 1) * tc)
            cb, cc, cu = (slice(q * d + j * tc, q * d + (j + 1) * tc) for q in range(3))
            v = x_ref[:, cc].astype(F32) * x_ref[:, cu].astype(F32)
            vh = jnp.where(i > 0, xb_ref[:, cc].astype(F32) * xb_ref[:, cu].astype(F32), 0.0)
            cat = jnp.concatenate([vh, v], axis=0)
            w = w_ref[:, col]
            conv = w[2:3] * v + w[1:2] * _shift_down(cat, 1) + w[0:1] * _shift_down(cat, 2)
            o_ref[:, col] = (x_ref[:, cb].astype(F32) * conv).astype(BF16)

    return pl.pallas_call(
        body, name="conv_fwd", grid=(m // tr,), out_shape=jax.ShapeDtypeStruct((m, d), BF16),
        in_specs=[pl.BlockSpec((tr, d3), lambda i: (i, 0)),
                  pl.BlockSpec((HALO, d3), lambda i: (jnp.maximum(i * hb - 1, 0), 0)),
                  pl.BlockSpec((8, d), lambda i: (0, 0))],
        out_specs=pl.BlockSpec((tr, d), lambda i: (i, 0)),
        compiler_params=_params(("parallel",)),
    )(bcu, bcu, cw)


def _conv_bwd(bcu, cw, dg):
    m, d3 = bcu.shape
    d = d3 // 3
    tr, tc = _pick(m, 208, 16), _pick(d, 512, 128)
    hb, nt = tr // HALO, m // tr

    def body(x_ref, xb_ref, xa_ref, dg_ref, dga_ref, w_ref, o_ref, dw_ref):
        i = pl.program_id(0)

        @pl.when(i == 0)
        def _():
            dw_ref[...] = jnp.zeros_like(dw_ref)

        for j in range(d // tc):
            col = slice(j * tc, (j + 1) * tc)
            cb, cc, cu = (slice(q * d + j * tc, q * d + (j + 1) * tc) for q in range(3))
            w = w_ref[:, col]
            b, c, u = x_ref[:, cb].astype(F32), x_ref[:, cc].astype(F32), x_ref[:, cu].astype(F32)
            dgv = dg_ref[:, col].astype(F32)
            v = c * u
            vh = jnp.where(i > 0, xb_ref[:, cc].astype(F32) * xb_ref[:, cu].astype(F32), 0.0)
            cat = jnp.concatenate([vh, v], axis=0)
            v1, v2 = _shift_down(cat, 1), _shift_down(cat, 2)
            dconv = dgv * b
            o_ref[:, cb] = (dgv * (w[2:3] * v + w[1:2] * v1 + w[0:1] * v2)).astype(BF16)
            taps = [jnp.sum(dconv * t, axis=0, keepdims=True) for t in (v2, v1, v)]
            dw_ref[:, col] += jnp.concatenate(taps + [jnp.zeros((5, tc), F32)], axis=0)
            nxt = jnp.where(i < nt - 1, dga_ref[:, col].astype(F32) * xa_ref[:, cb].astype(F32), 0.0)
            cat2 = jnp.concatenate([dconv, nxt], axis=0)
            dv = w[2:3] * dconv + w[1:2] * _shift_up(cat2, 1) + w[0:1] * _shift_up(cat2, 2)
            o_ref[:, cc] = (dv * u).astype(BF16)
            o_ref[:, cu] = (dv * c).astype(BF16)

    def rows(width):
        return pl.BlockSpec((tr, width), lambda i: (i, 0))

    def before(width):
        return pl.BlockSpec((HALO, width), lambda i: (jnp.maximum(i * hb - 1, 0), 0))

    def after(width):
        return pl.BlockSpec((HALO, width), lambda i: (jnp.minimum((i + 1) * hb, m // HALO - 1), 0))

    return pl.pallas_call(
        body, name="conv_bwd", grid=(nt,),
        out_shape=(jax.ShapeDtypeStruct((m, d3), BF16), jax.ShapeDtypeStruct((8, d), F32)),
        in_specs=[rows(d3), before(d3), after(d3), rows(d), after(d), pl.BlockSpec((8, d), lambda i: (0, 0))],
        out_specs=(rows(d3), pl.BlockSpec((8, d), lambda i: (0, 0))),
        compiler_params=_params(("arbitrary",)),
    )(bcu, bcu, bcu, dg, dg, cw)


PAIR = 2 * HEAD_DIM


def _rope_tables(m):
    pad = ROW0 - N_META
    pos = jnp.arange(m, dtype=F32) - pad
    inv = ROPE_THETA ** (-jnp.arange(0, HEAD_DIM, 2, dtype=F32) / HEAD_DIM)
    ang = pos[:, None] * inv[None, :]
    cos, sin = jnp.cos(ang), jnp.sin(ang)
    return jnp.tile(jnp.concatenate([cos, cos], axis=1), (1, 2)), jnp.tile(jnp.concatenate([-sin, sin], axis=1), (1, 2))


def _rope_pair(x, c, s):
    half = HEAD_DIM // 2
    lane = lax.broadcasted_iota(jnp.int32, x.shape, 1)
    swapped = jnp.where(lane % HEAD_DIM < half, pltpu.roll(x, PAIR - half, axis=1), pltpu.roll(x, half, axis=1))
    return x * c + swapped * s


def _attn_mask(i, rows):
    r = lax.broadcasted_iota(jnp.int32, (rows, 2 * BLOCK), 0) % BLOCK
    cidx = lax.broadcasted_iota(jnp.int32, (rows, 2 * BLOCK), 1)
    key = (i - 1) * BLOCK + cidx
    return (cidx > r) & (cidx <= r + BLOCK) & (key >= ROW0 - N_META)


BAND = 2 * BLOCK


def _rope_qk(qkv, cos, sin):
    m, width = qkv.shape
    scale = HEAD_DIM ** -0.5
    nq, nk = N_Q_HEADS // 2, N_KV_HEADS // 2

    def body(x_ref, c_ref, s_ref, o_ref):
        c, s = c_ref[...], s_ref[...]
        cq, sq = c * scale, s * scale
        for t in range(nq + 2 * nk):
            col = slice(t * PAIR, (t + 1) * PAIR)
            if t < nq:
                o_ref[:, col] = _rope_pair(x_ref[:, col].astype(F32), cq, sq).astype(BF16)
            elif t < nq + nk:
                o_ref[:, col] = _rope_pair(x_ref[:, col].astype(F32), c, s).astype(BF16)
            else:
                o_ref[:, col] = x_ref[:, col]

    row = pl.BlockSpec((BLOCK, width), lambda i: (i, 0))
    tab = pl.BlockSpec((BLOCK, PAIR), lambda i: (i, 0))
    return pl.pallas_call(
        body, name="rope_qk", grid=(m // BLOCK,), out_shape=jax.ShapeDtypeStruct((m, width), BF16),
        in_specs=[row, tab, tab], out_specs=row, compiler_params=_params(("parallel",)),
    )(qkv, cos, sin)


def _pair_rows(ref, h):
    pairs = N_Q_HEADS // N_KV_HEADS // 2
    return jnp.concatenate([ref[:, (h * pairs + g) * PAIR:(h * pairs + g + 1) * PAIR] for g in range(pairs)], axis=0)


def _twice(x):
    z = jnp.zeros_like(x)
    return jnp.concatenate([jnp.concatenate([x, z], axis=1), jnp.concatenate([z, x], axis=1)], axis=0)


def _kv_band(cur_ref, prev_ref, h):
    k0, v0 = N_Q_HEADS * HEAD_DIM + h * HEAD_DIM, (N_Q_HEADS + N_KV_HEADS) * HEAD_DIM + h * HEAD_DIM
    p0, p1 = h * HEAD_DIM, (N_KV_HEADS + h) * HEAD_DIM
    k = jnp.concatenate([prev_ref[:, p0:p0 + HEAD_DIM], cur_ref[:, k0:k0 + HEAD_DIM]], axis=0)
    v = jnp.concatenate([prev_ref[:, p1:p1 + HEAD_DIM], cur_ref[:, v0:v0 + HEAD_DIM]], axis=0)
    return k, v


def _attn2_specs(width):
    kvw = 2 * N_KV_HEADS * HEAD_DIM
    cur = pl.BlockSpec((BLOCK, width), lambda i: (i, 0))
    prev = pl.BlockSpec((BLOCK, kvw), lambda i: (jnp.maximum(i - 1, 0), N_Q_HEADS * HEAD_DIM // kvw))
    return cur, prev


def _attn2_fwd(qkr, sink2):
    m, width = qkr.shape
    nb, rows = m // BLOCK, N_Q_HEADS // N_KV_HEADS // 2 * BLOCK
    dq = N_Q_HEADS * HEAD_DIM

    def body(x_ref, prev_ref, s_ref, o_ref, p_ref):
        allowed = _attn_mask(pl.program_id(0), rows)
        col0 = lax.broadcasted_iota(jnp.int32, (rows, BAND), 1) == 0
        lane = lax.broadcasted_iota(jnp.int32, (rows, PAIR), 1)
        rsel = lax.broadcasted_iota(jnp.int32, (2 * BAND, PAIR), 0) < BAND
        lsel = lax.broadcasted_iota(jnp.int32, (2 * BAND, PAIR), 1) < HEAD_DIM
        ones2 = jnp.where(rsel == lsel, 1.0, 0.0).astype(BF16)
        for h in range(N_KV_HEADS):
            k, v = _kv_band(x_ref, prev_ref, h)
            s2 = lax.dot_general(_pair_rows(x_ref, h), _twice(k), _NT, preferred_element_type=F32)
            sink = s_ref[h]
            e, mx = [], []
            for half in range(2):
                s = jnp.where(allowed, s2[:, half * BAND:(half + 1) * BAND], NEG_INF)
                mx.append(jnp.maximum(jnp.max(s, axis=-1, keepdims=True), sink[:, half * HEAD_DIM:half * HEAD_DIM + 1]))
                e.append(jnp.exp(s - mx[half]).astype(BF16))
            eb2 = jnp.concatenate(e, axis=1)
            es2 = jnp.exp(sink - jnp.where(lane < HEAD_DIM, mx[0], mx[1]))
            ov2 = jnp.dot(eb2, _twice(v), preferred_element_type=F32)
            inv2 = 1.0 / (jnp.dot(eb2, ones2, preferred_element_type=F32) + es2)
            o2 = (ov2 * inv2).astype(BF16)
            ps2 = es2 * inv2
            for g in range(rows // BLOCK):
                col = (h * (rows // BLOCK) + g) * PAIR
                o_ref[:, col:col + PAIR] = o2[g * BLOCK:(g + 1) * BLOCK, :]
            for half in range(2):
                at = half * HEAD_DIM
                p = jnp.where(col0, ps2[:, at:at + 1], e[half].astype(F32) * inv2[:, at:at + 1])
                p_ref[h, :, half * BAND:(half + 1) * BAND] = p.astype(BF16)

    cur, prev = _attn2_specs(width)
    return pl.pallas_call(
        body, name="attn_fwd", grid=(nb,),
        out_shape=(jax.ShapeDtypeStruct((m, dq), BF16), jax.ShapeDtypeStruct((N_KV_HEADS, nb * rows, 2 * BAND), BF16)),
        in_specs=[cur, prev, pl.BlockSpec((N_KV_HEADS, rows, PAIR), lambda i: (0, 0, 0))],
        out_specs=(pl.BlockSpec((BLOCK, dq), lambda i: (i, 0)),
                   pl.BlockSpec((N_KV_HEADS, rows, 2 * BAND), lambda i: (0, i, 0))),
        compiler_params=_params(("parallel",)),
    )(qkr, qkr, sink2)


def _attn2_bwd(qkr, p, o, do):
    m, width = qkr.shape
    nb, rows = m // BLOCK, N_Q_HEADS // N_KV_HEADS // 2 * BLOCK
    dq = N_Q_HEADS * HEAD_DIM

    def body(x_ref, prev_ref, p_ref, o_ref, do_ref, dq_ref, dk_ref, dv_ref, ds_ref):
        colz = lax.broadcasted_iota(jnp.int32, (rows, 2 * BAND), 1) % BAND == 0
        even = lax.broadcasted_iota(jnp.int32, (rows, PAIR), 1) < HEAD_DIM

        @pl.when(pl.program_id(0) == 0)
        def _():
            ds_ref[...] = jnp.zeros_like(ds_ref)

        for h in range(N_KV_HEADS):
            k, v = _kv_band(x_ref, prev_ref, h)
            k2, v2 = _twice(k), _twice(v)
            q2, do2, pv = _pair_rows(x_ref, h), _pair_rows(do_ref, h), p_ref[h]
            prod = do2.astype(F32) * _pair_rows(o_ref, h).astype(F32)
            delta = [jnp.sum(jnp.where(even, prod, 0.0), axis=-1, keepdims=True),
                     jnp.sum(jnp.where(even, 0.0, prod), axis=-1, keepdims=True)]
            dp2 = lax.dot_general(do2, v2, _NT, preferred_element_type=F32)
            pb = jnp.where(colz, jnp.zeros_like(pv), pv)
            ds = [(pb[:, half * BAND:(half + 1) * BAND].astype(F32)
                   * (dp2[:, half * BAND:(half + 1) * BAND] - delta[half])).astype(BF16) for half in range(2)]
            dsb2 = jnp.concatenate(ds, axis=1)
            dq2 = jnp.dot(dsb2, k2, preferred_element_type=F32).astype(BF16)
            for g in range(rows // BLOCK):
                col = (h * (rows // BLOCK) + g) * PAIR
                dq_ref[:, col:col + PAIR] = dq2[g * BLOCK:(g + 1) * BLOCK, :]
            dkk = lax.dot_general(q2, dsb2, _TN, preferred_element_type=F32)
            dk_ref[h] = (dkk[:HEAD_DIM, :BAND] + dkk[HEAD_DIM:, BAND:]).T
            dvv = lax.dot_general(do2, pb, _TN, preferred_element_type=F32)
            dv_ref[h] = (dvv[:HEAD_DIM, :BAND] + dvv[HEAD_DIM:, BAND:]).T
            ds_ref[h] -= jnp.where(even, pv[:, 0:1].astype(F32) * delta[0], pv[:, BAND:BAND + 1].astype(F32) * delta[1])

    cur, prev = _attn2_specs(width)
    heads = pl.BlockSpec((BLOCK, dq), lambda i: (i, 0))
    band = pl.BlockSpec((N_KV_HEADS, None, BAND, HEAD_DIM), lambda i: (0, i, 0, 0))
    band_shape = jax.ShapeDtypeStruct((N_KV_HEADS, nb, BAND, HEAD_DIM), F32)
    sink = pl.BlockSpec((N_KV_HEADS, rows, PAIR), lambda i: (0, 0, 0))
    return pl.pallas_call(
        body, name="attn_bwd", grid=(nb,),
        out_shape=(jax.ShapeDtypeStruct((m, dq), BF16), band_shape, band_shape,
                   jax.ShapeDtypeStruct((N_KV_HEADS, rows, PAIR), F32)),
        in_specs=[cur, prev, pl.BlockSpec((N_KV_HEADS, rows, 2 * BAND), lambda i: (0, i, 0)), heads, heads],
        out_specs=(heads, band, band, sink), compiler_params=_params(("arbitrary",)),
    )(qkr, qkr, p, o, do)


def _rope_qk_bwd(dq, dkb, dvb, cos, sin):
    nb = dkb.shape[1]
    width = (N_Q_HEADS + 2 * N_KV_HEADS) * HEAD_DIM
    scale = HEAD_DIM ** -0.5
    nq, nk = N_Q_HEADS // 2, N_KV_HEADS // 2

    def body(dq_ref, kc_ref, kn_ref, vc_ref, vn_ref, c_ref, s_ref, o_ref):
        last = pl.program_id(0) == nb - 1
        c, s = c_ref[...], -s_ref[...]

        def band_sum(cur_ref, nxt_ref, t):
            return jnp.concatenate([cur_ref[2 * t + e, BLOCK:, :] + jnp.where(last, 0.0, nxt_ref[2 * t + e, :BLOCK, :])
                                    for e in range(2)], axis=1)

        cq, sq = c * scale, s * scale
        for t in range(nq):
            col = slice(t * PAIR, (t + 1) * PAIR)
            o_ref[:, col] = _rope_pair(dq_ref[:, col].astype(F32), cq, sq).astype(BF16)
        for t in range(nk):
            o_ref[:, (nq + t) * PAIR:(nq + t + 1) * PAIR] = _rope_pair(band_sum(kc_ref, kn_ref, t), c, s).astype(BF16)
            o_ref[:, (nq + nk + t) * PAIR:(nq + nk + t + 1) * PAIR] = band_sum(vc_ref, vn_ref, t).astype(BF16)

    tab = pl.BlockSpec((BLOCK, PAIR), lambda i: (i, 0))
    cur = pl.BlockSpec((N_KV_HEADS, None, BAND, HEAD_DIM), lambda i: (0, i, 0, 0))
    nxt = pl.BlockSpec((N_KV_HEADS, None, BAND, HEAD_DIM), lambda i: (0, jnp.minimum(i + 1, nb - 1), 0, 0))
    return pl.pallas_call(
        body, name="rope_qk_bwd", grid=(nb,), out_shape=jax.ShapeDtypeStruct((nb * BLOCK, width), BF16),
        in_specs=[pl.BlockSpec((BLOCK, N_Q_HEADS * HEAD_DIM), lambda i: (i, 0)), cur, nxt, cur, nxt, tab, tab],
        out_specs=pl.BlockSpec((BLOCK, width), lambda i: (i, 0)),
        compiler_params=_params(("parallel",)),
    )(dq, dkb, dkb, dvb, dvb, cos, sin)


def _tiles2d(r, c):
    tc = _pick(c, 2048, 128) if c % 128 == 0 else c
    tr = _pick(r, max(8, (1 << 20) // tc // 8 * 8), 8) if r % 8 == 0 else r
    return tr, tc


def _cast_bf16(name, w, place, wide):
    r, c = w.shape
    tr, tc = _tiles2d(r, c)
    if tr % 16:
        tr = r
    nc = c // tc

    def body(place_ref, w_ref, o_ref):
        o_ref[...] = w_ref[...].astype(BF16)

    if wide:
        out_shape = jax.ShapeDtypeStruct((r, N_CHIPS * c), BF16)
        out_spec = pl.BlockSpec((tr, tc), lambda i, j, p: (i, p[1] * nc + j))
    else:
        out_shape = jax.ShapeDtypeStruct((N_CHIPS, r, c), BF16)
        out_spec = pl.BlockSpec((None, tr, tc), lambda i, j, p: (p[1], i, j))
    return pl.pallas_call(
        body, name=name, out_shape=out_shape,
        grid_spec=pltpu.PrefetchScalarGridSpec(
            num_scalar_prefetch=1, grid=(r // tr, nc),
            in_specs=[pl.BlockSpec((tr, tc), lambda i, j, p: (i, j))], out_specs=out_spec),
        compiler_params=_params(("parallel", "parallel")),
    )(place, w)


def _pair_sum(name, g, got, place):
    n, r, c = g.shape
    half = r // 2
    tr, tc = _tiles2d(half, c)
    nh = half // tr

    def body(place_ref, g_ref, got_ref, o_ref, own_ref):
        s = (g_ref[...].astype(F32) + got_ref[...].astype(F32)).astype(BF16)
        o_ref[...] = s

        @pl.when(pl.program_id(2) == place_ref[1])
        def _():
            own_ref[...] = s

    tile = pl.BlockSpec((None, tr, tc), lambda i, j, k, p: (k, i, j))
    shape = jax.ShapeDtypeStruct((n, half, c), BF16)
    return pl.pallas_call(
        body, name=name, out_shape=(shape, shape),
        grid_spec=pltpu.PrefetchScalarGridSpec(
            num_scalar_prefetch=1, grid=(nh, c // tc, n),
            in_specs=[pl.BlockSpec((None, tr, tc), lambda i, j, k, p: (k, p[0] * nh + i, j)), tile],
            out_specs=(tile, pl.BlockSpec((None, tr, tc), lambda i, j, k, p: (p[1], i, j)))),
        compiler_params=_params(("parallel", "parallel", "arbitrary")),
    )(place, g, got)


def _chip_sum(name, parts, place):
    n, half, c = parts.shape
    tr, tc = _tiles2d(half, c)
    nh = half // tr

    def body(place_ref, p0, p1, p2, p3, o_ref):
        o_ref[...] = ((p0[...].astype(F32) + p1[...].astype(F32)) + p2[...].astype(F32)) + p3[...].astype(F32)

    def chip(k):
        return pl.BlockSpec((None, tr, tc), lambda i, j, p: (k, i, j))

    return pl.pallas_call(
        body, name=name, out_shape=jax.ShapeDtypeStruct((2 * half, c), F32),
        grid_spec=pltpu.PrefetchScalarGridSpec(
            num_scalar_prefetch=1, grid=(nh, c // tc),
            in_specs=[chip(k) for k in range(n)],
            out_specs=pl.BlockSpec((tr, tc), lambda i, j, p: (p[0] * nh + i, j))),
        compiler_params=_params(("parallel", "parallel")),
    )(place, parts, parts, parts, parts)


def _dev_sum(gathered):
    def body(g_ref, o_ref):
        acc = g_ref[0]
        for k in range(1, N_DEV):
            acc = acc + g_ref[k]
        o_ref[...] = acc

    return pl.pallas_call(body, name="dev_sum", out_shape=jax.ShapeDtypeStruct(gathered.shape[1:], F32))(gathered)


def _adamw(name, w, g, m, v):
    r, c = w.shape
    tr, tc = _tiles2d(r, c)
    if r % 8 == 0:
        tr = _pick(r, max(8, (1 << 18) // tc // 8 * 8), 8)

    def body(w_ref, g_ref, m_ref, v_ref, go_ref, d_ref, mo_ref, vo_ref):
        gv = g_ref[...]
        go_ref[...] = gv
        mn = ADAM_B1 * m_ref[...] + (1.0 - ADAM_B1) * gv
        vn = ADAM_B2 * v_ref[...] + (1.0 - ADAM_B2) * jnp.square(gv)
        m_hat = mn / (1.0 - ADAM_B1 ** ADAM_STEP)
        v_hat = vn / (1.0 - ADAM_B2 ** ADAM_STEP)
        d_ref[...] = -ADAM_LR * (m_hat / (jnp.sqrt(v_hat) + ADAM_EPS) + ADAM_WD * w_ref[...])
        mo_ref[...] = mn
        vo_ref[...] = vn

    tile = pl.BlockSpec((tr, tc), lambda i, j: (i, j))
    shape = jax.ShapeDtypeStruct((r, c), F32)
    return pl.pallas_call(
        body, name=name, grid=(r // tr, c // tc), out_shape=(shape,) * 4,
        in_specs=[tile] * 4, out_specs=(tile,) * 4, compiler_params=_params(("parallel", "parallel")),
    )(w, g, m, v)


MATRICES = ("w_in_conv", "w_out_conv", "w_up_0", "w_down_0", "w_qkv", "w_o", "w_up_1", "w_down_1")
COLUMN_SHARDED = ("w_in_conv", "w_up_0", "w_qkv", "w_up_1")
NORMS = ("norm_mix_0", "norm_mlp_0", "norm_mix_1", "norm_mlp_1", "norm_final")


def _rows(stack):
    return stack.reshape(N_CHIPS * stack.shape[1], stack.shape[2])


def _stack(full):
    return full.reshape(N_CHIPS, full.shape[0] // N_CHIPS, full.shape[1])


def _add_residual(acc, res):
    return acc + res


def _relu_sq_grad(acc, z):
    return acc * (2.0 * jnp.maximum(z.astype(F32), 0.0))


def _step(x, target, stacks, small, norms, sinks, place, update):
    d = D_MODEL
    dc = d // N_CHIPS
    pad = ROW0 - N_META
    m = x.shape[0] + ROW0
    grp = N_Q_HEADS // N_KV_HEADS
    cos, sin = _rope_tables(m)
    pairs = grp // 2
    sink2 = jnp.broadcast_to(sinks.astype(F32).reshape(N_KV_HEADS, pairs, 1, 2, 1),
                             (N_KV_HEADS, pairs, BLOCK, 2, HEAD_DIM)).reshape(N_KV_HEADS, pairs * BLOCK, PAIR)

    def gather(*names):
        return _gather_task([stacks[n] for n in names])

    def pair_sum(tag, grad, got):
        return _pair_sum("pair_sum_" + tag, grad, got, place)

    def chip_sum(tag, landed):
        return _chip_sum("chip_sum_" + tag, landed, place)

    (w_in, small_all), = _run("gather_first", [_gather_task([stacks["w_in_conv"], small])])
    small_full = jnp.transpose(small_all, (1, 0, 2)).reshape(SMALL_ROWS, d)
    conv_w8 = small_full[N_META:N_META + 8]
    h0 = jnp.concatenate([jnp.zeros((pad, d), F32), small_full[:N_META], x], axis=0)

    n0 = _rms_fwd("norm_mix_0", h0, norms["norm_mix_0"])
    bcu, ((w_out, w_up0),) = _mm_nn("conv_in", n0, w_in, BF16, tasks=[gather("w_out_conv", "w_up_0")])
    gate = _conv_fwd(bcu, conv_w8)
    h1, n1 = _mm_nn("conv_out", gate, _rows(w_out), F32, epi=_add_residual, extras=(h0,), norm_gain=norms["norm_mlp_0"])
    z0, ((w_down0,),) = _mm_nn("mlp_up_0", n1, w_up0, BF16, tasks=[gather("w_down_0")])
    h2, ((w_qkv, w_o, w_up1),) = _mm_nn("mlp_down_0", z0, _rows(w_down0), F32, a_pro=_relu_sq, epi=_add_residual,
                                             extras=(h1,), tasks=[gather("w_qkv", "w_o", "w_up_1")])
    n2 = _rms_fwd("norm_mix_1", h2, norms["norm_mix_1"])
    qkv = _mm_nn("attn_qkv", n2, w_qkv, BF16)
    qkr = _rope_qk(qkv, cos, sin)
    o, probs = _attn2_fwd(qkr, sink2)
    h3, n3 = _mm_nn("attn_out", o, _rows(w_o), F32, epi=_add_residual, extras=(h2,), norm_gain=norms["norm_mlp_1"])
    z1, ((w_down1,),) = _mm_nn("mlp_up_1", n3, w_up1, BF16, tasks=[gather("w_down_1")])
    h4 = _mm_nn("mlp_down_1", z1, _rows(w_down1), F32, a_pro=_relu_sq, epi=_add_residual, extras=(h3,))

    gn = {}
    loss, dh, dh_bf, gn["norm_final"] = _loss_head(h4, norms["norm_final"], target)
    dz = _mm_nt("mlp_down_dx_1", dh_bf, _rows(w_down1), BF16, epi=_relu_sq_grad, extras=(z1,))
    g_d1 = _stack(_mm_tn("mlp_down_dw_1", z1, dh_bf, stacked=False, a_pro=_relu_sq))
    g_u1, ((got,),) = _mm_tn("mlp_up_dw_1", n3, dz, stacked=True, tasks=[_pair_exchange_task([g_d1])])
    s_d1 = pair_sum("d1", g_d1, got)
    dn, ((got,), (landed,)) = _mm_nt("mlp_up_dx_1", dz, w_up1, BF16,
                                          tasks=[_pair_exchange_task([g_u1]), _chip_exchange_task([s_d1])])
    s_u1, b_d1 = pair_sum("u1", g_u1, got), chip_sum("d1", landed)
    dh, dh_bf, gn["norm_mlp_1"] = _rms_bwd("norm_mlp_bwd_1", dn, h3, norms["norm_mlp_1"], dh)
    do = _mm_nt("attn_out_dx", dh_bf, _rows(w_o), BF16)
    g_o = _stack(_mm_tn("attn_out_dw", o, dh_bf, stacked=False))
    dq, dkb, dvb, dsink = _attn2_bwd(qkr, probs, o, do)
    dqkv = _rope_qk_bwd(dq, dkb, dvb, cos, sin)
    g_qkv, ((got,),) = _mm_tn("attn_qkv_dw", n2, dqkv, stacked=True, tasks=[_pair_exchange_task([g_o])])
    s_o = pair_sum("o", g_o, got)
    dn, ((got,), (landed,)) = _mm_nt("attn_qkv_dx", dqkv, w_qkv, BF16,
                                          tasks=[_pair_exchange_task([g_qkv]), _chip_exchange_task([s_u1])])
    s_qkv, b_u1 = pair_sum("qkv", g_qkv, got), chip_sum("u1", landed)
    dh, dh_bf, gn["norm_mix_1"] = _rms_bwd("norm_mix_bwd_1", dn, h2, norms["norm_mix_1"], dh)
    dz, ((landed_o, landed_qkv), (r_d1,)) = _mm_nt(
        "mlp_down_dx_0", dh_bf, _rows(w_down0), BF16, epi=_relu_sq_grad, extras=(z0,),
        tasks=[_chip_exchange_task([s_o, s_qkv]), _pair_share_task([b_d1])])
    b_o, b_qkv = chip_sum("o", landed_o), chip_sum("qkv", landed_qkv)
    g_d0, ((r_u1,),) = _mm_tn("mlp_down_dw_0", z0, dh_bf, stacked=False, a_pro=_relu_sq, tasks=[_pair_share_task([b_u1])])
    g_d0 = _stack(g_d0)
    g_u0, ((got,), (r_o, r_qkv)) = _mm_tn("mlp_up_dw_0", n1, dz, stacked=True,
                                          tasks=[_pair_exchange_task([g_d0]), _pair_share_task([b_o, b_qkv])])
    s_d0 = pair_sum("d0", g_d0, got)
    dn, ((got,), (landed,)) = _mm_nt("mlp_up_dx_0", dz, w_up0, BF16,
                                          tasks=[_pair_exchange_task([g_u0]), _chip_exchange_task([s_d0])])
    s_u0, b_d0 = pair_sum("u0", g_u0, got), chip_sum("d0", landed)
    dh, dh_bf, gn["norm_mlp_0"] = _rms_bwd("norm_mlp_bwd_0", dn, h1, norms["norm_mlp_0"], dh)
    dgate = _mm_nt("conv_out_dx", dh_bf, _rows(w_out), BF16)
    dbcu, g_conv_w = _conv_bwd(bcu, conv_w8, dgate)
    g_in, ((landed,), (r_d0,)) = _mm_tn("conv_in_dw", n0, dbcu, stacked=True,
                                        tasks=[_chip_exchange_task([s_u0]), _pair_share_task([b_d0])])
    b_u0 = chip_sum("u0", landed)
    g_out, ((got,), (r_u0,)) = _mm_tn("conv_out_dw", gate, dh_bf, stacked=False,
                                      tasks=[_pair_exchange_task([g_in]), _pair_share_task([b_u0])])
    g_out = _stack(g_out)
    s_in = pair_sum("in", g_in, got)
    dn, ((landed,), (got,)) = _mm_nt("conv_in_dx", dbcu, w_in, BF16,
                                     tasks=[_chip_exchange_task([s_in]), _pair_exchange_task([g_out])])
    b_in, s_out = chip_sum("in", landed), pair_sum("out", g_out, got)
    grad_x, dh_first, gn["norm_mix_0"] = _rms_bwd_tokens("norm_mix_bwd_0", dn, h0, norms["norm_mix_0"], dh)

    g_small = jnp.zeros((SMALL_ROWS, d), F32).at[:N_META].set(dh_first[pad:ROW0]).at[N_META:N_META + 8].set(g_conv_w)
    g_small = jnp.transpose(g_small.reshape(SMALL_ROWS, N_CHIPS, dc), (1, 0, 2))
    rep = jnp.zeros((8, d), F32)
    for r, n in enumerate(NORMS):
        rep = rep.at[r].set(jnp.sum(gn[n], axis=0))
    dsink = jnp.sum(dsink.reshape(N_KV_HEADS, pairs, BLOCK, 2, HEAD_DIM)[..., 0], axis=2)
    rep = rep.at[len(NORMS), :N_Q_HEADS].set(dsink.reshape(N_Q_HEADS))
    (got, rep_all), = _run("tail_pair_exchange", [_pair_exchange_task([g_small], small=rep)])
    s_small = pair_sum("small", g_small, got)
    (landed_out, landed_small), = _run("tail_chip_exchange", [_chip_exchange_task([s_out, s_small])])
    b_out, b_small = chip_sum("out", landed_out), chip_sum("small", landed_small)
    (r_out, r_small, r_in), = _run("tail_pair_share", [_pair_share_task([b_out, b_small, b_in])])
    for n, r in (("w_down_1", r_d1), ("w_up_1", r_u1), ("w_down_0", r_d0), ("w_up_0", r_u0), ("w_o", r_o),
                 ("w_qkv", r_qkv), ("w_out_conv", r_out), ("w_in_conv", r_in)):
        update(n, r)
    return loss, grad_x, r_small, rep_all


def kernel(x, meta_tokens, norm_mix_0, w_in_conv, conv_w, w_out_conv, norm_mlp_0, w_up_0, w_down_0, norm_mix_1, w_qkv, attn_sinks, w_o, norm_mlp_1, w_up_1, w_down_1, norm_final, loss_target, m_meta_tokens, m_norm_mix_0, m_w_in_conv, m_conv_w, m_w_out_conv, m_norm_mlp_0, m_w_up_0, m_w_down_0, m_norm_mix_1, m_w_qkv, m_attn_sinks, m_w_o, m_norm_mlp_1, m_w_up_1, m_w_down_1, m_norm_final, v_meta_tokens, v_norm_mix_0, v_w_in_conv, v_conv_w, v_w_out_conv, v_norm_mlp_0, v_w_up_0, v_w_down_0, v_norm_mix_1, v_w_qkv, v_attn_sinks, v_w_o, v_norm_mlp_1, v_w_up_1, v_w_down_1, v_norm_final):
    given = dict(locals())
    names = ("meta_tokens", "norm_mix_0", "w_in_conv", "conv_w", "w_out_conv", "norm_mlp_0", "w_up_0", "w_down_0",
             "norm_mix_1", "w_qkv", "attn_sinks", "w_o", "norm_mlp_1", "w_up_1", "w_down_1", "norm_final")
    d = D_MODEL
    dc = d // N_CHIPS
    chip = 2 * lax.axis_index("x") + lax.axis_index("y")
    place = jnp.stack([lax.axis_index("c"), chip]).astype(jnp.int32)

    small = jnp.zeros((SMALL_ROWS, dc), F32).at[:N_META].set(meta_tokens).at[N_META:N_META + CONV_WIDTH].set(conv_w)
    small = lax.dynamic_update_slice(jnp.zeros((N_CHIPS, SMALL_ROWS, dc), F32), small[None], (chip, 0, 0))
    stacks = {n: _cast_bf16("cast_" + n, given[n], place, n in COLUMN_SHARDED) for n in MATRICES}

    g_out, delta, new_m, new_v = {}, {}, {}, {}

    def update(n, grad):
        wt = given[n]
        shape2 = wt.shape if wt.ndim == 2 else (1, wt.shape[0])
        outs = _adamw("adamw_" + n, wt.reshape(shape2), grad.reshape(shape2),
                      given["m_" + n].reshape(shape2), given["v_" + n].reshape(shape2))
        g_out[n], delta[n], new_m[n], new_v[n] = [o.reshape(wt.shape) for o in outs]

    norms = {n: given[n] for n in NORMS}
    loss_part, grad_x, r_small, rep_all = _step(x[0], loss_target[0], stacks, small, norms, attn_sinks, place, update)
    loss = lax.psum(loss_part[0, 0], ("x", "y", "c"))
    rep_sum = _dev_sum(rep_all)
    update("meta_tokens", r_small[:N_META])
    update("conv_w", r_small[N_META:N_META + CONV_WIDTH])
    for r, n in enumerate(NORMS):
        update(n, rep_sum[r])
    update("attn_sinks", rep_sum[len(NORMS), :N_Q_HEADS])
    return (loss, grad_x[None], *[g_out[n] for n in names], *[delta[n] for n in names],
            *[new_m[n] for n in names], *[new_v[n] for n in names])
```

```python
import jax
import jax.numpy as jnp
from jax import lax
from jax.experimental import pallas as pl
from jax.experimental.pallas import tpu as pltpu

F32 = jnp.float32
BF16 = jnp.bfloat16

D_MODEL = 2048
SEQ = 8192
N_META = 16
CONV_WIDTH = 3
HEAD_DIM = 64
N_Q_HEADS = 32
N_KV_HEADS = 4
BLOCK = 128
ROPE_THETA = 10000.0
D_FF = 4 * D_MODEL
RMS_EPS = 1e-5
NEG_INF = -1e30

ADAM_LR = 0.001
ADAM_B1 = 0.9
ADAM_B2 = 0.999
ADAM_EPS = 1e-08
ADAM_WD = 0.01
ADAM_STEP = 10

N_CHIPS = 4
N_DEV = 8
MESH = pl.DeviceIdType.MESH
VMEM_LIMIT = 56 * 1024 * 1024
SMALL_ROWS = 32
ROW0 = BLOCK


def _pick(n, target, mult):
    best = None
    for t in range(mult, min(n, target) + 1, mult):
        if n % t == 0:
            best = t
    assert best is not None, (n, target, mult)
    return best


def _params(sem=None):
    return pltpu.CompilerParams(dimension_semantics=sem, vmem_limit_bytes=VMEM_LIMIT)


HBM_SPEC = pl.BlockSpec(memory_space=pltpu.HBM)


class _Task:
    def __init__(self, inputs, outputs, aliases, sem_shapes, bind):
        self.inputs, self.outputs, self.aliases = list(inputs), list(outputs), dict(aliases)
        self.sem_shapes, self.bind = list(sem_shapes), bind


def _like(arrays):
    return [jax.ShapeDtypeStruct(a.shape, a.dtype) for a in arrays]


def _bind_tasks(tasks, in_refs, out_refs, sem_refs):
    bound, i, o, s = [], 0, 0, 0
    for t in tasks:
        ni, no, ns = len(t.inputs), len(t.outputs), len(t.sem_shapes)
        bound.append(t.bind(in_refs[i:i + ni], out_refs[o:o + no], sem_refs[s:s + ns]))
        i, o, s = i + ni, o + no, s + ns
    return bound


def _run_phase(bound, phase):
    for b in bound:
        if b[phase] is not None:
            b[phase]()


def _task_plumbing(tasks, in_offset, out_offset):
    ins = [a for t in tasks for a in t.inputs]
    outs = [o for t in tasks for o in t.outputs]
    sems = [s for t in tasks for s in t.sem_shapes]
    aliases, i, o = {}, in_offset, out_offset
    for t in tasks:
        for src, dst in t.aliases.items():
            aliases[i + src] = o + dst
        i, o = i + len(t.inputs), o + len(t.outputs)
    return ins, outs, sems, aliases


def _split_outputs(tasks, flat):
    res, o = [], 0
    for t in tasks:
        res.append(list(flat[o:o + len(t.outputs)]))
        o += len(t.outputs)
    return res


def _run(name, tasks):
    ins, outs, sems, aliases = _task_plumbing(tasks, 0, 0)

    def body(*refs):
        bound = _bind_tasks(tasks, refs[:len(ins)], refs[len(ins):len(ins) + len(outs)], refs[len(ins) + len(outs):])
        for phase in range(3):
            _run_phase(bound, phase)

    flat = pl.pallas_call(
        body, name=name, out_shape=outs, in_specs=[HBM_SPEC] * len(ins), out_specs=[HBM_SPEC] * len(outs),
        input_output_aliases=aliases, scratch_shapes=sems,
    )(*ins)
    return _split_outputs(tasks, flat)


def _place():
    x, y, c = lax.axis_index("x"), lax.axis_index("y"), lax.axis_index("c")
    chips = [(1 - x, y), (x, 1 - y), (1 - x, 1 - y)]
    return x, y, c, 2 * x + y, chips


def _gather_task(stacks):
    n = len(stacks)
    halves = [s.shape[-2] // 2 for s in stacks]

    def bind(_, dst, sems):
        send_a, recv_a, send_b, recv_b = sems
        x, y, c, me, chips = _place()
        sibling = (x, y, 1 - c)

        def half(w, chip, hc):
            rows = pl.ds(hc * halves[w], halves[w])
            if len(stacks[w].shape) == 3:
                return dst[w].at[chip, rows, :]
            cols = stacks[w].shape[1] // N_CHIPS
            return dst[w].at[rows, pl.ds(chip * cols, cols)]

        def over_ici(j, w, block):
            return pltpu.make_async_remote_copy(
                src_ref=half(w, block, c), dst_ref=half(w, block, c), send_sem=send_a.at[j * n + w],
                recv_sem=recv_a.at[j * n + w], device_id=(*chips[j], c), device_id_type=MESH)

        def over_d2d(j, w, hc):
            got = half(w, 2 * chips[j][0] + chips[j][1], hc)
            return pltpu.make_async_remote_copy(
                src_ref=got, dst_ref=got, send_sem=send_b.at[j * n + w], recv_sem=recv_b.at[j * n + w],
                device_id=sibling, device_id_type=MESH)

        pairs = [(j, w) for j in range(3) for w in range(n)]

        def start():
            for j, w in pairs:
                over_ici(j, w, me).start()

        def mid():
            for j, w in pairs:
                over_ici(j, w, 2 * chips[j][0] + chips[j][1]).wait_recv()
                over_d2d(j, w, c).start()

        def finish():
            for j, w in pairs:
                over_d2d(j, w, 1 - c).wait_recv()
            for j, w in pairs:
                over_ici(j, w, me).wait_send()
                over_d2d(j, w, c).wait_send()

        return start, mid, finish

    return _Task(stacks, _like(stacks), {w: w for w in range(n)}, [pltpu.SemaphoreType.DMA((3 * n,))] * 4, bind)


def _pair_exchange_task(grads, small=None):
    n = len(grads)
    halves = [g.shape[1] // 2 for g in grads]

    def bind(src, dst, sems):
        send, recv = sems[0], sems[1]
        x, y, c, me, _ = _place()
        sibling = (x, y, 1 - c)
        dev = 2 * me + c

        def to_sibling(w):
            return pltpu.make_async_remote_copy(
                src_ref=src[w].at[:, pl.ds((1 - c) * halves[w], halves[w]), :], dst_ref=dst[w],
                send_sem=send.at[w], recv_sem=recv.at[w], device_id=sibling, device_id_type=MESH)

        def to_peer(t, block):
            tx, ty, tc = (t >> 2) & 1, (t >> 1) & 1, t & 1
            return pltpu.make_async_remote_copy(
                src_ref=src[n], dst_ref=dst[n].at[block], send_sem=sems[2].at[t], recv_sem=sems[3].at[t],
                device_id=(x ^ tx, y ^ ty, c ^ tc), device_id_type=MESH)

        def mine():
            return pltpu.make_async_copy(src[n], dst[n].at[dev], sems[4])

        def start():
            for w in range(n):
                to_sibling(w).start()
            if small is not None:
                mine().start()
                for t in range(1, N_DEV):
                    to_peer(t, dev).start()

        def finish():
            for w in range(n):
                to_sibling(w).wait_recv()
            if small is not None:
                for t in range(1, N_DEV):
                    to_peer(t, dev ^ t).wait_recv()
            for w in range(n):
                to_sibling(w).wait_send()
            if small is not None:
                for t in range(1, N_DEV):
                    to_peer(t, dev).wait_send()
                mine().wait()

        return start, None, finish

    outputs = [jax.ShapeDtypeStruct((N_CHIPS, h, g.shape[2]), g.dtype) for g, h in zip(grads, halves)]
    sem_shapes = [pltpu.SemaphoreType.DMA((n,)), pltpu.SemaphoreType.DMA((n,))]
    inputs = list(grads)
    if small is not None:
        inputs.append(small)
        outputs.append(jax.ShapeDtypeStruct((N_DEV,) + small.shape, small.dtype))
        sem_shapes += [pltpu.SemaphoreType.DMA((N_DEV,)), pltpu.SemaphoreType.DMA((N_DEV,)), pltpu.SemaphoreType.DMA(())]
    return _Task(inputs, outputs, {}, sem_shapes, bind)


def _chip_exchange_task(summed):
    n = len(summed)

    def bind(refs, dst, sems):
        src = refs[:n]
        send, recv = sems
        x, y, c, me, chips = _place()

        def copy(j, w, block_from, block_to):
            return pltpu.make_async_remote_copy(
                src_ref=src[w].at[block_from], dst_ref=dst[w].at[block_to], send_sem=send.at[j * n + w],
                recv_sem=recv.at[j * n + w], device_id=(*chips[j], c), device_id_type=MESH)

        pairs = [(j, w) for j in range(3) for w in range(n)]

        def start():
            for j, w in pairs:
                copy(j, w, 2 * chips[j][0] + chips[j][1], me).start()

        def finish():
            for j, w in pairs:
                copy(j, w, me, 2 * chips[j][0] + chips[j][1]).wait_recv()
            for j, w in pairs:
                copy(j, w, 2 * chips[j][0] + chips[j][1], me).wait_send()

        return start, None, finish

    partials, landing = [s[0] for s in summed], [s[1] for s in summed]
    return _Task(partials + landing, _like(landing), {n + w: w for w in range(n)},
                 [pltpu.SemaphoreType.DMA((3 * n,))] * 2, bind)


def _pair_share_task(blocks):
    n = len(blocks)

    def bind(_, dst, sems):
        send, recv = sems
        x, y, c, _, _ = _place()

        def copy(w, hc):
            h = blocks[w].shape[0] // 2
            rows = dst[w].at[pl.ds(hc * h, h), :]
            return pltpu.make_async_remote_copy(src_ref=rows, dst_ref=rows, send_sem=send.at[w], recv_sem=recv.at[w],
                                                device_id=(x, y, 1 - c), device_id_type=MESH)

        def start():
            for w in range(n):
                copy(w, c).start()

        def finish():
            for w in range(n):
                copy(w, 1 - c).wait_recv()
            for w in range(n):
                copy(w, c).wait_send()

        return start, None, finish

    return _Task(blocks, _like(blocks), {w: w for w in range(n)}, [pltpu.SemaphoreType.DMA((n,))] * 2, bind)


def _carrier_call(name, body, operands, *, grid, in_specs, out_specs, out_shape, semantics, tasks):
    n_in, n_out = len(operands), len(out_shape)
    t_ins, t_outs, t_sems, aliases = _task_plumbing(tasks, n_in, n_out)
    n_ti, n_to = len(t_ins), len(t_outs)
    total = 1
    for g in grid:
        total *= g
    mid_step = max(0, total - 1 - max(1, total // 8))

    def carrier(*refs):
        outs_at = n_in + n_ti
        if tasks:
            bound = _bind_tasks(tasks, refs[n_in:outs_at], refs[outs_at + n_out:outs_at + n_out + n_to],
                                refs[outs_at + n_out + n_to:])
            step = 0
            for axis, g in enumerate(grid):
                step = step * g + pl.program_id(axis)

            @pl.when(step == 0)
            def _():
                _run_phase(bound, 0)

        body(*refs[:n_in], *refs[outs_at:outs_at + n_out])

        if tasks:
            @pl.when(step == mid_step)
            def _():
                _run_phase(bound, 1)

            @pl.when(step == total - 1)
            def _():
                _run_phase(bound, 2)

    res = pl.pallas_call(
        carrier, name=name, grid=grid, out_shape=[*out_shape, *t_outs],
        in_specs=[*in_specs, *[HBM_SPEC] * n_ti], out_specs=[*out_specs, *[HBM_SPEC] * n_to],
        input_output_aliases=aliases, scratch_shapes=t_sems,
        compiler_params=_params(("arbitrary",) * len(grid) if tasks else semantics),
    )(*operands, *t_ins)
    return list(res[:n_out]), _split_outputs(tasks, res[n_out:])


def _mm(name, a, b, *, dims, grid, a_spec, b_spec, out_shape, out_spec,
        extras=(), extra_specs=(), a_pro=None, epi=None, norm_gain=None, tasks=()):
    n_ex = len(extras)
    normed = norm_gain is not None

    def body(a_ref, b_ref, *rest):
        outs = rest[n_ex + normed:]
        av = a_ref[...]
        if a_pro is not None:
            av = a_pro(av)
        acc = lax.dot_general(av, b_ref[...], dims, preferred_element_type=F32)
        if epi is not None:
            acc = epi(acc, *[e[...] for e in rest[:n_ex]])
        outs[0][...] = acc.astype(outs[0].dtype)
        if normed:
            rstd = lax.rsqrt(jnp.mean(acc * acc, axis=-1, keepdims=True) + RMS_EPS)
            outs[1][...] = ((acc * rstd) * rest[n_ex][...]).astype(BF16)

    operands, in_specs = [a, b, *extras], [a_spec, b_spec, *extra_specs]
    out_specs, out_shapes = [out_spec], [out_shape]
    if normed:
        width = out_shape.shape[1]
        operands.append(norm_gain.reshape(1, width))
        in_specs.append(pl.BlockSpec((1, width), lambda j, i: (0, 0)))
        out_specs.append(out_spec)
        out_shapes.append(jax.ShapeDtypeStruct(out_shape.shape, BF16))
    res, carried = _carrier_call(name, body, operands, grid=grid, in_specs=in_specs, out_specs=out_specs,
                                 out_shape=out_shapes, semantics=("parallel", "parallel"), tasks=tasks)
    res = tuple(res) if normed else res[0]
    return (res, carried) if tasks else res


_NN = (((1,), (0,)), ((), ()))
_NT = (((1,), (1,)), ((), ()))
_TN = (((0,), (0,)), ((), ()))


MM_TILE_BUDGET = 46 * 1024 * 1024


def _mm_tiles(m, n, contraction, out_bytes):
    for rows, cols in ((1664, 1024), (832, 1024), (416, 1024), (416, 512)):
        tm, tn = _pick(m, rows, 16), _pick(n, cols, 128)
        if 2 * 2 * contraction * (tm + tn) + tm * tn * (4 + 2 * out_bytes) <= MM_TILE_BUDGET:
            break
    return tm, tn


def _out_bytes(out_dtype, extras):
    return jnp.dtype(out_dtype).itemsize + sum(e.dtype.itemsize for e in extras)


def _mm_nn(name, a, w, out_dtype, a_pro=None, epi=None, extras=(), norm_gain=None, tasks=()):
    m, k = a.shape
    _, n = w.shape
    tm, tn = _mm_tiles(m, n, k, _out_bytes(out_dtype, extras))
    if norm_gain is not None:
        tm, tn = _pick(m, 416, 16), n
    tile = pl.BlockSpec((tm, tn), lambda j, i: (i, j))
    return _mm(name, a, w, dims=_NN, grid=(n // tn, m // tm),
               a_spec=pl.BlockSpec((tm, k), lambda j, i: (i, 0)), b_spec=pl.BlockSpec((k, tn), lambda j, i: (0, j)),
               out_shape=jax.ShapeDtypeStruct((m, n), out_dtype), out_spec=tile,
               extras=extras, extra_specs=[tile] * len(extras), a_pro=a_pro, epi=epi, norm_gain=norm_gain, tasks=tasks)


def _mm_nt(name, a, w, out_dtype, epi=None, extras=(), tasks=()):
    m, c = a.shape
    r, _ = w.shape
    tm, tn = _mm_tiles(m, r, c, _out_bytes(out_dtype, extras))
    tile = pl.BlockSpec((tm, tn), lambda j, i: (i, j))
    return _mm(name, a, w, dims=_NT, grid=(r // tn, m // tm),
               a_spec=pl.BlockSpec((tm, c), lambda j, i: (i, 0)), b_spec=pl.BlockSpec((tn, c), lambda j, i: (j, 0)),
               out_shape=jax.ShapeDtypeStruct((m, r), out_dtype), out_spec=tile,
               extras=extras, extra_specs=[tile] * len(extras), epi=epi, tasks=tasks)


def _mm_tn(name, a, b, stacked, a_pro=None, tasks=()):
    t, ka = a.shape
    _, nb = b.shape
    ns = nb // N_CHIPS if stacked else nb
    ta, tb = _pick(ka, 512, 128), _pick(ns, 640, 128)
    if stacked:
        per = ns // tb
        out_shape = jax.ShapeDtypeStruct((N_CHIPS, ka, ns), BF16)
        out_spec = pl.BlockSpec((None, ta, tb), lambda i, j: (j // per, i, j % per))
    else:
        out_shape = jax.ShapeDtypeStruct((ka, nb), BF16)
        out_spec = pl.BlockSpec((ta, tb), lambda i, j: (i, j))
    return _mm(name, a, b, dims=_TN, grid=(ka // ta, nb // tb),
               a_spec=pl.BlockSpec((t, ta), lambda i, j: (0, i)), b_spec=pl.BlockSpec((t, tb), lambda i, j: (0, j)),
               out_shape=out_shape, out_spec=out_spec, a_pro=a_pro, tasks=tasks)


def _relu_sq(z):
    a = jnp.maximum(z, 0)
    return a * a


def _rms_fwd(name, h, g):
    m, d = h.shape
    tr = _pick(m, 512, 16)

    def body(h_ref, g_ref, o_ref):
        x = h_ref[...]
        rstd = lax.rsqrt(jnp.mean(x * x, axis=-1, keepdims=True) + RMS_EPS)
        o_ref[...] = ((x * rstd) * g_ref[...]).astype(BF16)

    row = pl.BlockSpec((tr, d), lambda i: (i, 0))
    return pl.pallas_call(
        body, name=name, grid=(m // tr,), out_shape=jax.ShapeDtypeStruct((m, d), BF16),
        in_specs=[row, pl.BlockSpec((1, d), lambda i: (0, 0))], out_specs=row,
        compiler_params=_params(("parallel",)),
    )(h, g.reshape(1, d))


def _rms_bwd_math(x, g, dn):
    rstd = lax.rsqrt(jnp.mean(x * x, axis=-1, keepdims=True) + RMS_EPS)
    xhat = x * rstd
    dxhat = dn * g
    dx = rstd * (dxhat - xhat * jnp.mean(dxhat * xhat, axis=-1, keepdims=True))
    return dx, dn * xhat


def _fold8(v):
    r, c = v.shape
    return jnp.sum(v.reshape(r // 8, 8, c), axis=0)


def _rms_bwd(name, dn, h, g, dh_in):
    m, d = h.shape
    tr = _pick(m, 512, 16)
    nt = m // tr

    def body(dn_ref, h_ref, g_ref, dh_ref, o_ref, ob_ref, dg_ref):
        dx, dgp = _rms_bwd_math(h_ref[...], g_ref[...], dn_ref[...].astype(F32))
        dh = dh_ref[...] + dx
        o_ref[...] = dh
        ob_ref[...] = dh.astype(BF16)

        @pl.when(pl.program_id(0) == 0)
        def _():
            dg_ref[...] = jnp.zeros_like(dg_ref)

        dg_ref[...] += _fold8(dgp)

    row = pl.BlockSpec((tr, d), lambda i: (i, 0))
    return pl.pallas_call(
        body, name=name, grid=(nt,),
        out_shape=(jax.ShapeDtypeStruct((m, d), F32), jax.ShapeDtypeStruct((m, d), BF16),
                   jax.ShapeDtypeStruct((8, d), F32)),
        in_specs=[row, row, pl.BlockSpec((1, d), lambda i: (0, 0)), row],
        out_specs=(row, row, pl.BlockSpec((8, d), lambda i: (0, 0))),
        compiler_params=_params(("arbitrary",)),
    )(dn, h, g.reshape(1, d), dh_in)


def _rms_bwd_tokens(name, dn, h, g, dh_in):
    m, d = h.shape
    nb = m // BLOCK

    def body(dn_ref, h_ref, g_ref, dh_ref, gx_ref, first_ref, dg_ref):
        i = pl.program_id(0)
        dx, dgp = _rms_bwd_math(h_ref[...], g_ref[...], dn_ref[...].astype(F32))
        dh = dh_ref[...] + dx
        gx_ref[...] = dh

        @pl.when(i == 0)
        def _():
            first_ref[...] = dh
            dg_ref[...] = jnp.zeros_like(dg_ref)

        dg_ref[...] += _fold8(dgp)

    row = pl.BlockSpec((BLOCK, d), lambda i: (i, 0))
    return pl.pallas_call(
        body, name=name, grid=(nb,),
        out_shape=(jax.ShapeDtypeStruct((m - ROW0, d), F32), jax.ShapeDtypeStruct((ROW0, d), F32),
                   jax.ShapeDtypeStruct((8, d), F32)),
        in_specs=[row, row, pl.BlockSpec((1, d), lambda i: (0, 0)), row],
        out_specs=(pl.BlockSpec((BLOCK, d), lambda i: (jnp.maximum(i - 1, 0), 0)),
                   pl.BlockSpec((ROW0, d), lambda i: (0, 0)), pl.BlockSpec((8, d), lambda i: (0, 0))),
        compiler_params=_params(("arbitrary",)),
    )(dn, h, g.reshape(1, d), dh_in)


def _loss_head(h, g, target):
    m, d = h.shape
    tr = BLOCK

    def body(h_ref, g_ref, t_ref, loss_ref, o_ref, ob_ref, dg_ref):
        i = pl.program_id(0)
        x = h_ref[...]
        gv = g_ref[...]
        rstd = lax.rsqrt(jnp.mean(x * x, axis=-1, keepdims=True) + RMS_EPS)
        err = jnp.where(i > 0, (x * rstd) * gv - t_ref[...], 0.0)
        dx, dgp = _rms_bwd_math(x, gv, err * (1.0 / d))
        o_ref[...] = dx
        ob_ref[...] = dx.astype(BF16)

        @pl.when(i == 0)
        def _():
            dg_ref[...] = jnp.zeros_like(dg_ref)
            loss_ref[...] = jnp.zeros_like(loss_ref)

        dg_ref[...] += _fold8(dgp)
        sq = jnp.mean(err * err, axis=-1, keepdims=True)
        loss_ref[...] += 0.5 * jnp.sum(sq, axis=0, keepdims=True)

    row = pl.BlockSpec((tr, d), lambda i: (i, 0))
    return pl.pallas_call(
        body, name="loss_head", grid=(m // tr,),
        out_shape=(jax.ShapeDtypeStruct((8, 128), F32), jax.ShapeDtypeStruct((m, d), F32),
                   jax.ShapeDtypeStruct((m, d), BF16), jax.ShapeDtypeStruct((8, d), F32)),
        in_specs=[row, pl.BlockSpec((1, d), lambda i: (0, 0)),
                  pl.BlockSpec((tr, d), lambda i: (jnp.maximum(i - 1, 0), 0))],
        out_specs=(pl.BlockSpec((8, 128), lambda i: (0, 0)), row, row,
                   pl.BlockSpec((8, d), lambda i: (0, 0))),
        compiler_params=_params(("arbitrary",)),
    )(h, g.reshape(1, d), target)


HALO = 16


def _shift_down(cat, k):
    return pltpu.roll(cat, k, axis=0)[HALO:]


def _shift_up(cat, k):
    n = cat.shape[0]
    return pltpu.roll(cat, n - k, axis=0)[:n - HALO]


def _conv_fwd(bcu, cw):
    m, d3 = bcu.shape
    d = d3 // 3
    tr, tc = _pick(m, 416, 16), _pick(d, 512, 128)
    hb = tr // HALO

    def body(x_ref, xb_ref, w_ref, o_ref):
        i = pl.program_id(0)
        for j in range(d // tc):
            col = slice(j * tc, (j + 1) * tc)
            cb, cc, cu = (slice(q * d + j * tc, q * d + (j + 1) * tc) for q in range(3))
            v = x_ref[:, cc].astype(F32) * x_ref[:, cu].astype(F32)
            vh = jnp.where(i > 0, xb_ref[:, cc].astype(F32) * xb_ref[:, cu].astype(F32), 0.0)
            cat = jnp.concatenate([vh, v], axis=0)
            w = w_ref[:, col]
            conv = w[2:3] * v + w[1:2] * _shift_down(cat, 1) + w[0:1] * _shift_down(cat, 2)
            o_ref[:, col] = (x_ref[:, cb].astype(F32) * conv).astype(BF16)

    return pl.pallas_call(
        body, name="conv_fwd", grid=(m // tr,), out_shape=jax.ShapeDtypeStruct((m, d), BF16),
        in_specs=[pl.BlockSpec((tr, d3), lambda i: (i, 0)),
                  pl.BlockSpec((HALO, d3), lambda i: (jnp.maximum(i * hb - 1, 0), 0)),
                  pl.BlockSpec((8, d), lambda i: (0, 0))],
        out_specs=pl.BlockSpec((tr, d), lambda i: (i, 0)),
        compiler_params=_params(("parallel",)),
    )(bcu, bcu, cw)


def _conv_bwd(bcu, cw, dg):
    m, d3 = bcu.shape
    d = d3 // 3
    tr, tc = _pick(m, 208, 16), _pick(d, 512, 128)
    hb, nt = tr // HALO, m // tr

    def body(x_ref, xb_ref, xa_ref, dg_ref, dga_ref, w_ref, o_ref, dw_ref):
        i = pl.program_id(0)

        @pl.when(i == 0)
        def _():
            dw_ref[...] = jnp.zeros_like(dw_ref)

        for j in range(d // tc):
            col = slice(j * tc, (j + 1) * tc)
            cb, cc, cu = (slice(q * d + j * tc, q * d + (j + 1) * tc) for q in range(3))
            w = w_ref[:, col]
            b, c, u = x_ref[:, cb].astype(F32), x_ref[:, cc].astype(F32), x_ref[:, cu].astype(F32)
            dgv = dg_ref[:, col].astype(F32)
            v = c * u
            vh = jnp.where(i > 0, xb_ref[:, cc].astype(F32) * xb_ref[:, cu].astype(F32), 0.0)
            cat = jnp.concatenate([vh, v], axis=0)
            v1, v2 = _shift_down(cat, 1), _shift_down(cat, 2)
            dconv = dgv * b
            o_ref[:, cb] = (dgv * (w[2:3] * v + w[1:2] * v1 + w[0:1] * v2)).astype(BF16)
            taps = [jnp.sum(dconv * t, axis=0, keepdims=True) for t in (v2, v1, v)]
            dw_ref[:, col] += jnp.concatenate(taps + [jnp.zeros((5, tc), F32)], axis=0)
            nxt = jnp.where(i < nt - 1, dga_ref[:, col].astype(F32) * xa_ref[:, cb].astype(F32), 0.0)
            cat2 = jnp.concatenate([dconv, nxt], axis=0)
            dv = w[2:3] * dconv + w[1:2] * _shift_up(cat2, 1) + w[0:1] * _shift_up(cat2, 2)
            o_ref[:, cc] = (dv * u).astype(BF16)
            o_ref[:, cu] = (dv * c).astype(BF16)

    def rows(width):
        return pl.BlockSpec((tr, width), lambda i: (i, 0))

    def before(width):
        return pl.BlockSpec((HALO, width), lambda i: (jnp.maximum(i * hb - 1, 0), 0))

    def after(width):
        return pl.BlockSpec((HALO, width), lambda i: (jnp.minimum((i + 1) * hb, m // HALO - 1), 0))

    return pl.pallas_call(
        body, name="conv_bwd", grid=(nt,),
        out_shape=(jax.ShapeDtypeStruct((m, d3), BF16), jax.ShapeDtypeStruct((8, d), F32)),
        in_specs=[rows(d3), before(d3), after(d3), rows(d), after(d), pl.BlockSpec((8, d), lambda i: (0, 0))],
        out_specs=(rows(d3), pl.BlockSpec((8, d), lambda i: (0, 0))),
        compiler_params=_params(("arbitrary",)),
    )(bcu, bcu, bcu, dg, dg, cw)


PAIR = 2 * HEAD_DIM


def _rope_tables(m):
    pad = ROW0 - N_META
    pos = jnp.arange(m, dtype=F32) - pad
    inv = ROPE_THETA ** (-jnp.arange(0, HEAD_DIM, 2, dtype=F32) / HEAD_DIM)
    ang = pos[:, None] * inv[None, :]
    cos, sin = jnp.cos(ang), jnp.sin(ang)
    return jnp.tile(jnp.concatenate([cos, cos], axis=1), (1, 2)), jnp.tile(jnp.concatenate([-sin, sin], axis=1), (1, 2))


def _rope_pair(x, c, s):
    half = HEAD_DIM // 2
    lane = lax.broadcasted_iota(jnp.int32, x.shape, 1)
    swapped = jnp.where(lane % HEAD_DIM < half, pltpu.roll(x, PAIR - half, axis=1), pltpu.roll(x, half, axis=1))
    return x * c + swapped * s


def _attn_mask(i, rows):
    r = lax.broadcasted_iota(jnp.int32, (rows, 2 * BLOCK), 0) % BLOCK
    cidx = lax.broadcasted_iota(jnp.int32, (rows, 2 * BLOCK), 1)
    key = (i - 1) * BLOCK + cidx
    return (cidx > r) & (cidx <= r + BLOCK) & (key >= ROW0 - N_META)


BAND = 2 * BLOCK


def _rope_qk(qkv, cos, sin):
    m, width = qkv.shape
    scale = HEAD_DIM ** -0.5
    nq, nk = N_Q_HEADS // 2, N_KV_HEADS // 2

    def body(x_ref, c_ref, s_ref, o_ref):
        c, s = c_ref[...], s_ref[...]
        cq, sq = c * scale, s * scale
        for t in range(nq + 2 * nk):
            col = slice(t * PAIR, (t + 1) * PAIR)
            if t < nq:
                o_ref[:, col] = _rope_pair(x_ref[:, col].astype(F32), cq, sq).astype(BF16)
            elif t < nq + nk:
                o_ref[:, col] = _rope_pair(x_ref[:, col].astype(F32), c, s).astype(BF16)
            else:
                o_ref[:, col] = x_ref[:, col]

    row = pl.BlockSpec((BLOCK, width), lambda i: (i, 0))
    tab = pl.BlockSpec((BLOCK, PAIR), lambda i: (i, 0))
    return pl.pallas_call(
        body, name="rope_qk", grid=(m // BLOCK,), out_shape=jax.ShapeDtypeStruct((m, width), BF16),
        in_specs=[row, tab, tab], out_specs=row, compiler_params=_params(("parallel",)),
    )(qkv, cos, sin)


def _pair_rows(ref, h):
    pairs = N_Q_HEADS // N_KV_HEADS // 2
    return jnp.concatenate([ref[:, (h * pairs + g) * PAIR:(h * pairs + g + 1) * PAIR] for g in range(pairs)], axis=0)


def _twice(x):
    z = jnp.zeros_like(x)
    return jnp.concatenate([jnp.concatenate([x, z], axis=1), jnp.concatenate([z, x], axis=1)], axis=0)


def _kv_band(cur_ref, prev_ref, h):
    k0, v0 = N_Q_HEADS * HEAD_DIM + h * HEAD_DIM, (N_Q_HEADS + N_KV_HEADS) * HEAD_DIM + h * HEAD_DIM
    p0, p1 = h * HEAD_DIM, (N_KV_HEADS + h) * HEAD_DIM
    k = jnp.concatenate([prev_ref[:, p0:p0 + HEAD_DIM], cur_ref[:, k0:k0 + HEAD_DIM]], axis=0)
    v = jnp.concatenate([prev_ref[:, p1:p1 + HEAD_DIM], cur_ref[:, v0:v0 + HEAD_DIM]], axis=0)
    return k, v


def _attn2_specs(width):
    kvw = 2 * N_KV_HEADS * HEAD_DIM
    cur = pl.BlockSpec((BLOCK, width), lambda i: (i, 0))
    prev = pl.BlockSpec((BLOCK, kvw), lambda i: (jnp.maximum(i - 1, 0), N_Q_HEADS * HEAD_DIM // kvw))
    return cur, prev


def _attn2_fwd(qkr, sink2):
    m, width = qkr.shape
    nb, rows = m // BLOCK, N_Q_HEADS // N_KV_HEADS // 2 * BLOCK
    dq = N_Q_HEADS * HEAD_DIM

    def body(x_ref, prev_ref, s_ref, o_ref, p_ref):
        allowed = _attn_mask(pl.program_id(0), rows)
        col0 = lax.broadcasted_iota(jnp.int32, (rows, BAND), 1) == 0
        lane = lax.broadcasted_iota(jnp.int32, (rows, PAIR), 1)
        rsel = lax.broadcasted_iota(jnp.int32, (2 * BAND, PAIR), 0) < BAND
        lsel = lax.broadcasted_iota(jnp.int32, (2 * BAND, PAIR), 1) < HEAD_DIM
        ones2 = jnp.where(rsel == lsel, 1.0, 0.0).astype(BF16)
        for h in range(N_KV_HEADS):
            k, v = _kv_band(x_ref, prev_ref, h)
            s2 = lax.dot_general(_pair_rows(x_ref, h), _twice(k), _NT, preferred_element_type=F32)
            sink = s_ref[h]
            e, mx = [], []
            for half in range(2):
                s = jnp.where(allowed, s2[:, half * BAND:(half + 1) * BAND], NEG_INF)
                mx.append(jnp.maximum(jnp.max(s, axis=-1, keepdims=True), sink[:, half * HEAD_DIM:half * HEAD_DIM + 1]))
                e.append(jnp.exp(s - mx[half]).astype(BF16))
            eb2 = jnp.concatenate(e, axis=1)
            es2 = jnp.exp(sink - jnp.where(lane < HEAD_DIM, mx[0], mx[1]))
            ov2 = jnp.dot(eb2, _twice(v), preferred_element_type=F32)
            inv2 = 1.0 / (jnp.dot(eb2, ones2, preferred_element_type=F32) + es2)
            o2 = (ov2 * inv2).astype(BF16)
            ps2 = es2 * inv2
            for g in range(rows // BLOCK):
                col = (h * (rows // BLOCK) + g) * PAIR
                o_ref[:, col:col + PAIR] = o2[g * BLOCK:(g + 1) * BLOCK, :]
            for half in range(2):
                at = half * HEAD_DIM
                p = jnp.where(col0, ps2[:, at:at + 1], e[half].astype(F32) * inv2[:, at:at + 1])
                p_ref[h, :, half * BAND:(half + 1) * BAND] = p.astype(BF16)

    cur, prev = _attn2_specs(width)
    return pl.pallas_call(
        body, name="attn_fwd", grid=(nb,),
        out_shape=(jax.ShapeDtypeStruct((m, dq), BF16), jax.ShapeDtypeStruct((N_KV_HEADS, nb * rows, 2 * BAND), BF16)),
        in_specs=[cur, prev, pl.BlockSpec((N_KV_HEADS, rows, PAIR), lambda i: (0, 0, 0))],
        out_specs=(pl.BlockSpec((BLOCK, dq), lambda i: (i, 0)),
                   pl.BlockSpec((N_KV_HEADS, rows, 2 * BAND), lambda i: (0, i, 0))),
        compiler_params=_params(("parallel",)),
    )(qkr, qkr, sink2)


def _attn2_bwd(qkr, p, o, do):
    m, width = qkr.shape
    nb, rows = m // BLOCK, N_Q_HEADS // N_KV_HEADS // 2 * BLOCK
    dq = N_Q_HEADS * HEAD_DIM

    def body(x_ref, prev_ref, p_ref, o_ref, do_ref, dq_ref, dk_ref, dv_ref, ds_ref):
        colz = lax.broadcasted_iota(jnp.int32, (rows, 2 * BAND), 1) % BAND == 0
        even = lax.broadcasted_iota(jnp.int32, (rows, PAIR), 1) < HEAD_DIM

        @pl.when(pl.program_id(0) == 0)
        def _():
            ds_ref[...] = jnp.zeros_like(ds_ref)

        for h in range(N_KV_HEADS):
            k, v = _kv_band(x_ref, prev_ref, h)
            k2, v2 = _twice(k), _twice(v)
            q2, do2, pv = _pair_rows(x_ref, h), _pair_rows(do_ref, h), p_ref[h]
            prod = do2.astype(F32) * _pair_rows(o_ref, h).astype(F32)
            delta = [jnp.sum(jnp.where(even, prod, 0.0), axis=-1, keepdims=True),
                     jnp.sum(jnp.where(even, 0.0, prod), axis=-1, keepdims=True)]
            dp2 = lax.dot_general(do2, v2, _NT, preferred_element_type=F32)
            pb = jnp.where(colz, jnp.zeros_like(pv), pv)
            ds = [(pb[:, half * BAND:(half + 1) * BAND].astype(F32)
                   * (dp2[:, half * BAND:(half + 1) * BAND] - delta[half])).astype(BF16) for half in range(2)]
            dsb2 = jnp.concatenate(ds, axis=1)
            dq2 = jnp.dot(dsb2, k2, preferred_element_type=F32).astype(BF16)
            for g in range(rows // BLOCK):
                col = (h * (rows // BLOCK) + g) * PAIR
                dq_ref[:, col:col + PAIR] = dq2[g * BLOCK:(g + 1) * BLOCK, :]
            dkk = lax.dot_general(q2, dsb2, _TN, preferred_element_type=F32)
            dk_ref[h] = (dkk[:HEAD_DIM, :BAND] + dkk[HEAD_DIM:, BAND:]).T
            dvv = lax.dot_general(do2, pb, _TN, preferred_element_type=F32)
            dv_ref[h] = (dvv[:HEAD_DIM, :BAND] + dvv[HEAD_DIM:, BAND:]).T
            ds_ref[h] -= jnp.where(even, pv[:, 0:1].astype(F32) * delta[0], pv[:, BAND:BAND + 1].astype(F32) * delta[1])

    cur, prev = _attn2_specs(width)
    heads = pl.BlockSpec((BLOCK, dq), lambda i: (i, 0))
    band = pl.BlockSpec((N_KV_HEADS, None, BAND, HEAD_DIM), lambda i: (0, i, 0, 0))
    band_shape = jax.ShapeDtypeStruct((N_KV_HEADS, nb, BAND, HEAD_DIM), F32)
    sink = pl.BlockSpec((N_KV_HEADS, rows, PAIR), lambda i: (0, 0, 0))
    return pl.pallas_call(
        body, name="attn_bwd", grid=(nb,),
        out_shape=(jax.ShapeDtypeStruct((m, dq), BF16), band_shape, band_shape,
                   jax.ShapeDtypeStruct((N_KV_HEADS, rows, PAIR), F32)),
        in_specs=[cur, prev, pl.BlockSpec((N_KV_HEADS, rows, 2 * BAND), lambda i: (0, i, 0)), heads, heads],
        out_specs=(heads, band, band, sink), compiler_params=_params(("arbitrary",)),
    )(qkr, qkr, p, o, do)


def _rope_qk_bwd(dq, dkb, dvb, cos, sin):
    nb = dkb.shape[1]
    width = (N_Q_HEADS + 2 * N_KV_HEADS) * HEAD_DIM
    scale = HEAD_DIM ** -0.5
    nq, nk = N_Q_HEADS // 2, N_KV_HEADS // 2

    def body(dq_ref, kc_ref, kn_ref, vc_ref, vn_ref, c_ref, s_ref, o_ref):
        last = pl.program_id(0) == nb - 1
        c, s = c_ref[...], -s_ref[...]

        def band_sum(cur_ref, nxt_ref, t):
            return jnp.concatenate([cur_ref[2 * t + e, BLOCK:, :] + jnp.where(last, 0.0, nxt_ref[2 * t + e, :BLOCK, :])
                                    for e in range(2)], axis=1)

        cq, sq = c * scale, s * scale
        for t in range(nq):
            col = slice(t * PAIR, (t + 1) * PAIR)
            o_ref[:, col] = _rope_pair(dq_ref[:, col].astype(F32), cq, sq).astype(BF16)
        for t in range(nk):
            o_ref[:, (nq + t) * PAIR:(nq + t + 1) * PAIR] = _rope_pair(band_sum(kc_ref, kn_ref, t), c, s).astype(BF16)
            o_ref[:, (nq + nk + t) * PAIR:(nq + nk + t + 1) * PAIR] = band_sum(vc_ref, vn_ref, t).astype(BF16)

    tab = pl.BlockSpec((BLOCK, PAIR), lambda i: (i, 0))
    cur = pl.BlockSpec((N_KV_HEADS, None, BAND, HEAD_DIM), lambda i: (0, i, 0, 0))
    nxt = pl.BlockSpec((N_KV_HEADS, None, BAND, HEAD_DIM), lambda i: (0, jnp.minimum(i + 1, nb - 1), 0, 0))
    return pl.pallas_call(
        body, name="rope_qk_bwd", grid=(nb,), out_shape=jax.ShapeDtypeStruct((nb * BLOCK, width), BF16),
        in_specs=[pl.BlockSpec((BLOCK, N_Q_HEADS * HEAD_DIM), lambda i: (i, 0)), cur, nxt, cur, nxt, tab, tab],
        out_specs=pl.BlockSpec((BLOCK, width), lambda i: (i, 0)),
        compiler_params=_params(("parallel",)),
    )(dq, dkb, dkb, dvb, dvb, cos, sin)


def _tiles2d(r, c):
    tc = _pick(c, 2048, 128) if c % 128 == 0 else c
    tr = _pick(r, max(8, (1 << 20) // tc // 8 * 8), 8) if r % 8 == 0 else r
    return tr, tc


def _cast_bf16(name, w, place, wide):
    r, c = w.shape
    tr, tc = _tiles2d(r, c)
    if tr % 16:
        tr = r
    nc = c // tc

    def body(place_ref, w_ref, o_ref):
        o_ref[...] = w_ref[...].astype(BF16)

    if wide:
        out_shape = jax.ShapeDtypeStruct((r, N_CHIPS * c), BF16)
        out_spec = pl.BlockSpec((tr, tc), lambda i, j, p: (i, p[1] * nc + j))
    else:
        out_shape = jax.ShapeDtypeStruct((N_CHIPS, r, c), BF16)
        out_spec = pl.BlockSpec((None, tr, tc), lambda i, j, p: (p[1], i, j))
    return pl.pallas_call(
        body, name=name, out_shape=out_shape,
        grid_spec=pltpu.PrefetchScalarGridSpec(
            num_scalar_prefetch=1, grid=(r // tr, nc),
            in_specs=[pl.BlockSpec((tr, tc), lambda i, j, p: (i, j))], out_specs=out_spec),
        compiler_params=_params(("parallel", "parallel")),
    )(place, w)


def _pair_sum(name, g, got, place):
    n, r, c = g.shape
    half = r // 2
    tr, tc = _tiles2d(half, c)
    nh = half // tr

    def body(place_ref, g_ref, got_ref, o_ref, own_ref):
        s = (g_ref[...].astype(F32) + got_ref[...].astype(F32)).astype(BF16)
        o_ref[...] = s

        @pl.when(pl.program_id(2) == place_ref[1])
        def _():
            own_ref[...] = s

    tile = pl.BlockSpec((None, tr, tc), lambda i, j, k, p: (k, i, j))
    shape = jax.ShapeDtypeStruct((n, half, c), BF16)
    return pl.pallas_call(
        body, name=name, out_shape=(shape, shape),
        grid_spec=pltpu.PrefetchScalarGridSpec(
            num_scalar_prefetch=1, grid=(nh, c // tc, n),
            in_specs=[pl.BlockSpec((None, tr, tc), lambda i, j, k, p: (k, p[0] * nh + i, j)), tile],
            out_specs=(tile, pl.BlockSpec((None, tr, tc), lambda i, j, k, p: (p[1], i, j)))),
        compiler_params=_params(("parallel", "parallel", "arbitrary")),
    )(place, g, got)


def _chip_sum(name, parts, place):
    n, half, c = parts.shape
    tr, tc = _tiles2d(half, c)
    nh = half // tr

    def body(place_ref, p0, p1, p2, p3, o_ref):
        o_ref[...] = ((p0[...].astype(F32) + p1[...].astype(F32)) + p2[...].astype(F32)) + p3[...].astype(F32)

    def chip(k):
        return pl.BlockSpec((None, tr, tc), lambda i, j, p: (k, i, j))

    return pl.pallas_call(
        body, name=name, out_shape=jax.ShapeDtypeStruct((2 * half, c), F32),
        grid_spec=pltpu.PrefetchScalarGridSpec(
            num_scalar_prefetch=1, grid=(nh, c // tc),
            in_specs=[chip(k) for k in range(n)],
            out_specs=pl.BlockSpec((tr, tc), lambda i, j, p: (p[0] * nh + i, j))),
        compiler_params=_params(("parallel", "parallel")),
    )(place, parts, parts, parts, parts)


def _dev_sum(gathered):
    def body(g_ref, o_ref):
        acc = g_ref[0]
        for k in range(1, N_DEV):
            acc = acc + g_ref[k]
        o_ref[...] = acc

    return pl.pallas_call(body, name="dev_sum", out_shape=jax.ShapeDtypeStruct(gathered.shape[1:], F32))(gathered)


def _adamw(name, w, g, m, v):
    r, c = w.shape
    tr, tc = _tiles2d(r, c)
    if r % 8 == 0:
        tr = _pick(r, max(8, (1 << 18) // tc // 8 * 8), 8)

    def body(w_ref, g_ref, m_ref, v_ref, go_ref, d_ref, mo_ref, vo_ref):
        gv = g_ref[...]
        go_ref[...] = gv
        mn = ADAM_B1 * m_ref[...] + (1.0 - ADAM_B1) * gv
        vn = ADAM_B2 * v_ref[...] + (1.0 - ADAM_B2) * jnp.square(gv)
        m_hat = mn / (1.0 - ADAM_B1 ** ADAM_STEP)
        v_hat = vn / (1.0 - ADAM_B2 ** ADAM_STEP)
        d_ref[...] = -ADAM_LR * (m_hat / (jnp.sqrt(v_hat) + ADAM_EPS) + ADAM_WD * w_ref[...])
        mo_ref[...] = mn
        vo_ref[...] = vn

    tile = pl.BlockSpec((tr, tc), lambda i, j: (i, j))
    shape = jax.ShapeDtypeStruct((r, c), F32)
    return pl.pallas_call(
        body, name=name, grid=(r // tr, c // tc), out_shape=(shape,) * 4,
        in_specs=[tile] * 4, out_specs=(tile,) * 4, compiler_params=_params(("parallel", "parallel")),
    )(w, g, m, v)


MATRICES = ("w_in_conv", "w_out_conv", "w_up_0", "w_down_0", "w_qkv", "w_o", "w_up_1", "w_down_1")
COLUMN_SHARDED = ("w_in_conv", "w_up_0", "w_qkv", "w_up_1")
NORMS = ("norm_mix_0", "norm_mlp_0", "norm_mix_1", "norm_mlp_1", "norm_final")


def _rows(stack):
    return stack.reshape(N_CHIPS * stack.shape[1], stack.shape[2])


def _stack(full):
    return full.reshape(N_CHIPS, full.shape[0] // N_CHIPS, full.shape[1])


def _add_residual(acc, res):
    return acc + res


def _relu_sq_grad(acc, z):
    return acc * (2.0 * jnp.maximum(z.astype(F32), 0.0))


def _step(x, target, stacks, small, norms, sinks, place, update):
    d = D_MODEL
    dc = d // N_CHIPS
    pad = ROW0 - N_META
    m = x.shape[0] + ROW0
    grp = N_Q_HEADS // N_KV_HEADS
    cos, sin = _rope_tables(m)
    pairs = grp // 2
    sink2 = jnp.broadcast_to(sinks.astype(F32).reshape(N_KV_HEADS, pairs, 1, 2, 1),
                             (N_KV_HEADS, pairs, BLOCK, 2, HEAD_DIM)).reshape(N_KV_HEADS, pairs * BLOCK, PAIR)

    def gather(*names):
        return _gather_task([stacks[n] for n in names])

    def pair_sum(tag, grad, got):
        return _pair_sum("pair_sum_" + tag, grad, got, place)

    def chip_sum(tag, landed):
        return _chip_sum("chip_sum_" + tag, landed, place)

    (w_in, small_all), = _run("gather_first", [_gather_task([stacks["w_in_conv"], small])])
    small_full = jnp.transpose(small_all, (1, 0, 2)).reshape(SMALL_ROWS, d)
    conv_w8 = small_full[N_META:N_META + 8]
    h0 = jnp.concatenate([jnp.zeros((pad, d), F32), small_full[:N_META], x], axis=0)

    n0 = _rms_fwd("norm_mix_0", h0, norms["norm_mix_0"])
    bcu, ((w_out, w_up0),) = _mm_nn("conv_in", n0, w_in, BF16, tasks=[gather("w_out_conv", "w_up_0")])
    gate = _conv_fwd(bcu, conv_w8)
    (h1, n1), ((w_qkv,),) = _mm_nn("conv_out", gate, _rows(w_out), F32, epi=_add_residual, extras=(h0,),
                                   norm_gain=norms["norm_mlp_0"], tasks=[gather("w_qkv")])
    z0, ((w_down0, w_o),) = _mm_nn("mlp_up_0", n1, w_up0, BF16, tasks=[gather("w_down_0", "w_o")])
    h2, ((w_up1,),) = _mm_nn("mlp_down_0", z0, _rows(w_down0), F32, a_pro=_relu_sq, epi=_add_residual,
                             extras=(h1,), tasks=[gather("w_up_1")])
    n2 = _rms_fwd("norm_mix_1", h2, norms["norm_mix_1"])
    qkv = _mm_nn("attn_qkv", n2, w_qkv, BF16)
    qkr = _rope_qk(qkv, cos, sin)
    o, probs = _attn2_fwd(qkr, sink2)
    h3, n3 = _mm_nn("attn_out", o, _rows(w_o), F32, epi=_add_residual, extras=(h2,), norm_gain=norms["norm_mlp_1"])
    z1, ((w_down1,),) = _mm_nn("mlp_up_1", n3, w_up1, BF16, tasks=[gather("w_down_1")])
    h4 = _mm_nn("mlp_down_1", z1, _rows(w_down1), F32, a_pro=_relu_sq, epi=_add_residual, extras=(h3,))

    gn = {}
    loss, dh, dh_bf, gn["norm_final"] = _loss_head(h4, norms["norm_final"], target)
    dz = _mm_nt("mlp_down_dx_1", dh_bf, _rows(w_down1), BF16, epi=_relu_sq_grad, extras=(z1,))
    g_d1 = _stack(_mm_tn("mlp_down_dw_1", z1, dh_bf, stacked=False, a_pro=_relu_sq))
    g_u1, ((got,),) = _mm_tn("mlp_up_dw_1", n3, dz, stacked=True, tasks=[_pair_exchange_task([g_d1])])
    s_d1 = pair_sum("d1", g_d1, got)
    dn, ((got,), (landed,)) = _mm_nt("mlp_up_dx_1", dz, w_up1, BF16,
                                          tasks=[_pair_exchange_task([g_u1]), _chip_exchange_task([s_d1])])
    s_u1, b_d1 = pair_sum("u1", g_u1, got), chip_sum("d1", landed)
    dh, dh_bf, gn["norm_mlp_1"] = _rms_bwd("norm_mlp_bwd_1", dn, h3, norms["norm_mlp_1"], dh)
    do = _mm_nt("attn_out_dx", dh_bf, _rows(w_o), BF16)
    g_o = _stack(_mm_tn("attn_out_dw", o, dh_bf, stacked=False))
    dq, dkb, dvb, dsink = _attn2_bwd(qkr, probs, o, do)
    dqkv = _rope_qk_bwd(dq, dkb, dvb, cos, sin)
    g_qkv, ((got,),) = _mm_tn("attn_qkv_dw", n2, dqkv, stacked=True, tasks=[_pair_exchange_task([g_o])])
    s_o = pair_sum("o", g_o, got)
    dn, ((got,), (landed,)) = _mm_nt("attn_qkv_dx", dqkv, w_qkv, BF16,
                                          tasks=[_pair_exchange_task([g_qkv]), _chip_exchange_task([s_u1])])
    s_qkv, b_u1 = pair_sum("qkv", g_qkv, got), chip_sum("u1", landed)
    dh, dh_bf, gn["norm_mix_1"] = _rms_bwd("norm_mix_bwd_1", dn, h2, norms["norm_mix_1"], dh)
    dz, ((landed_o, landed_qkv), (r_d1,)) = _mm_nt(
        "mlp_down_dx_0", dh_bf, _rows(w_down0), BF16, epi=_relu_sq_grad, extras=(z0,),
        tasks=[_chip_exchange_task([s_o, s_qkv]), _pair_share_task([b_d1])])
    b_o, b_qkv = chip_sum("o", landed_o), chip_sum("qkv", landed_qkv)
    g_d0, ((r_u1,),) = _mm_tn("mlp_down_dw_0", z0, dh_bf, stacked=False, a_pro=_relu_sq, tasks=[_pair_share_task([b_u1])])
    g_d0 = _stack(g_d0)
    g_u0, ((got,), (r_o, r_qkv)) = _mm_tn("mlp_up_dw_0", n1, dz, stacked=True,
                                          tasks=[_pair_exchange_task([g_d0]), _pair_share_task([b_o, b_qkv])])
    s_d0 = pair_sum("d0", g_d0, got)
    dn, ((got,), (landed,)) = _mm_nt("mlp_up_dx_0", dz, w_up0, BF16,
                                          tasks=[_pair_exchange_task([g_u0]), _chip_exchange_task([s_d0])])
    s_u0, b_d0 = pair_sum("u0", g_u0, got), chip_sum("d0", landed)
    dh, dh_bf, gn["norm_mlp_0"] = _rms_bwd("norm_mlp_bwd_0", dn, h1, norms["norm_mlp_0"], dh)
    dgate = _mm_nt("conv_out_dx", dh_bf, _rows(w_out), BF16)
    dbcu, g_conv_w = _conv_bwd(bcu, conv_w8, dgate)
    g_in, ((landed,), (r_d0,)) = _mm_tn("conv_in_dw", n0, dbcu, stacked=True,
                                        tasks=[_chip_exchange_task([s_u0]), _pair_share_task([b_d0])])
    b_u0 = chip_sum("u0", landed)
    g_out, ((got,), (r_u0,)) = _mm_tn("conv_out_dw", gate, dh_bf, stacked=False,
                                      tasks=[_pair_exchange_task([g_in]), _pair_share_task([b_u0])])
    g_out = _stack(g_out)
    s_in = pair_sum("in", g_in, got)
    dn, ((landed,), (got,)) = _mm_nt("conv_in_dx", dbcu, w_in, BF16,
                                     tasks=[_chip_exchange_task([s_in]), _pair_exchange_task([g_out])])
    b_in, s_out = chip_sum("in", landed), pair_sum("out", g_out, got)
    grad_x, dh_first, gn["norm_mix_0"] = _rms_bwd_tokens("norm_mix_bwd_0", dn, h0, norms["norm_mix_0"], dh)

    g_small = jnp.zeros((SMALL_ROWS, d), F32).at[:N_META].set(dh_first[pad:ROW0]).at[N_META:N_META + 8].set(g_conv_w)
    g_small = jnp.transpose(g_small.reshape(SMALL_ROWS, N_CHIPS, dc), (1, 0, 2))
    rep = jnp.zeros((8, d), F32)
    for r, n in enumerate(NORMS):
        rep = rep.at[r].set(jnp.sum(gn[n], axis=0))
    dsink = jnp.sum(dsink.reshape(N_KV_HEADS, pairs, BLOCK, 2, HEAD_DIM)[..., 0], axis=2)
    rep = rep.at[len(NORMS), :N_Q_HEADS].set(dsink.reshape(N_Q_HEADS))
    (got, rep_all), = _run("tail_pair_exchange", [_pair_exchange_task([g_small], small=rep)])
    s_small = pair_sum("small", g_small, got)
    (landed_out, landed_small), = _run("tail_chip_exchange", [_chip_exchange_task([s_out, s_small])])
    b_out, b_small = chip_sum("out", landed_out), chip_sum("small", landed_small)
    (r_out, r_small, r_in), = _run("tail_pair_share", [_pair_share_task([b_out, b_small, b_in])])
    for n, r in (("w_down_1", r_d1), ("w_up_1", r_u1), ("w_down_0", r_d0), ("w_up_0", r_u0), ("w_o", r_o),
                 ("w_qkv", r_qkv), ("w_out_conv", r_out), ("w_in_conv", r_in)):
        update(n, r)
    return loss, grad_x, r_small, rep_all


def kernel(x, meta_tokens, norm_mix_0, w_in_conv, conv_w, w_out_conv, norm_mlp_0, w_up_0, w_down_0, norm_mix_1, w_qkv, attn_sinks, w_o, norm_mlp_1, w_up_1, w_down_1, norm_final, loss_target, m_meta_tokens, m_norm_mix_0, m_w_in_conv, m_conv_w, m_w_out_conv, m_norm_mlp_0, m_w_up_0, m_w_down_0, m_norm_mix_1, m_w_qkv, m_attn_sinks, m_w_o, m_norm_mlp_1, m_w_up_1, m_w_down_1, m_norm_final, v_meta_tokens, v_norm_mix_0, v_w_in_conv, v_conv_w, v_w_out_conv, v_norm_mlp_0, v_w_up_0, v_w_down_0, v_norm_mix_1, v_w_qkv, v_attn_sinks, v_w_o, v_norm_mlp_1, v_w_up_1, v_w_down_1, v_norm_final):
    given = dict(locals())
    names = ("meta_tokens", "norm_mix_0", "w_in_conv", "conv_w", "w_out_conv", "norm_mlp_0", "w_up_0", "w_down_0",
             "norm_mix_1", "w_qkv", "attn_sinks", "w_o", "norm_mlp_1", "w_up_1", "w_down_1", "norm_final")
    d = D_MODEL
    dc = d // N_CHIPS
    chip = 2 * lax.axis_index("x") + lax.axis_index("y")
    place = jnp.stack([lax.axis_index("c"), chip]).astype(jnp.int32)

    small = jnp.zeros((SMALL_ROWS, dc), F32).at[:N_META].set(meta_tokens).at[N_META:N_META + CONV_WIDTH].set(conv_w)
    small = lax.dynamic_update_slice(jnp.zeros((N_CHIPS, SMALL_ROWS, dc), F32), small[None], (chip, 0, 0))
    stacks = {n: _cast_bf16("cast_" + n, given[n], place, n in COLUMN_SHARDED) for n in MATRICES}

    g_out, delta, new_m, new_v = {}, {}, {}, {}

    def update(n, grad):
        wt = given[n]
        shape2 = wt.shape if wt.ndim == 2 else (1, wt.shape[0])
        outs = _adamw("adamw_" + n, wt.reshape(shape2), grad.reshape(shape2),
                      given["m_" + n].reshape(shape2), given["v_" + n].reshape(shape2))
        g_out[n], delta[n], new_m[n], new_v[n] = [o.reshape(wt.shape) for o in outs]

    norms = {n: given[n] for n in NORMS}
    loss_part, grad_x, r_small, rep_all = _step(x[0], loss_target[0], stacks, small, norms, attn_sinks, place, update)
    loss = lax.psum(loss_part[0, 0], ("x", "y", "c"))
    rep_sum = _dev_sum(rep_all)
    update("meta_tokens", r_small[:N_META])
    update("conv_w", r_small[N_META:N_META + CONV_WIDTH])
    for r, n in enumerate(NORMS):
        update(n, rep_sum[r])
    update("attn_sinks", rep_sum[len(NORMS), :N_Q_HEADS])
    return (loss, grad_x[None], *[g_out[n] for n in names], *[delta[n] for n in names],
            *[new_m[n] for n in names], *[new_v[n] for n in names])
```

```python
import jax
import jax.numpy as jnp
from jax import lax
from jax.experimental import pallas as pl
from jax.experimental.pallas import tpu as pltpu

F32 = jnp.float32
BF16 = jnp.bfloat16

D_MODEL = 2048
SEQ = 8192
N_META = 16
CONV_WIDTH = 3
HEAD_DIM = 64
N_Q_HEADS = 32
N_KV_HEADS = 4
BLOCK = 128
ROPE_THETA = 10000.0
D_FF = 4 * D_MODEL
RMS_EPS = 1e-5
NEG_INF = -1e30

ADAM_LR = 0.001
ADAM_B1 = 0.9
ADAM_B2 = 0.999
ADAM_EPS = 1e-08
ADAM_WD = 0.01
ADAM_STEP = 10

N_CHIPS = 4
N_DEV = 8
MESH = pl.DeviceIdType.MESH
VMEM_LIMIT = 56 * 1024 * 1024
SMALL_ROWS = 32
ROW0 = BLOCK


def _pick(n, target, mult):
    best = None
    for t in range(mult, min(n, target) + 1, mult):
        if n % t == 0:
            best = t
    assert best is not None, (n, target, mult)
    return best


def _params(sem=None):
    return pltpu.CompilerParams(dimension_semantics=sem, vmem_limit_bytes=VMEM_LIMIT)


HBM_SPEC = pl.BlockSpec(memory_space=pltpu.HBM)


class _Task:
    def __init__(self, inputs, outputs, aliases, sem_shapes, bind):
        self.inputs, self.outputs, self.aliases = list(inputs), list(outputs), dict(aliases)
        self.sem_shapes, self.bind = list(sem_shapes), bind


def _like(arrays):
    return [jax.ShapeDtypeStruct(a.shape, a.dtype) for a in arrays]


def _bind_tasks(tasks, in_refs, out_refs, sem_refs):
    bound, i, o, s = [], 0, 0, 0
    for t in tasks:
        ni, no, ns = len(t.inputs), len(t.outputs), len(t.sem_shapes)
        bound.append(t.bind(in_refs[i:i + ni], out_refs[o:o + no], sem_refs[s:s + ns]))
        i, o, s = i + ni, o + no, s + ns
    return bound


def _run_phase(bound, phase):
    for b in bound:
        if b[phase] is not None:
            b[phase]()


def _task_plumbing(tasks, in_offset, out_offset):
    ins = [a for t in tasks for a in t.inputs]
    outs = [o for t in tasks for o in t.outputs]
    sems = [s for t in tasks for s in t.sem_shapes]
    aliases, i, o = {}, in_offset, out_offset
    for t in tasks:
        for src, dst in t.aliases.items():
            aliases[i + src] = o + dst
        i, o = i + len(t.inputs), o + len(t.outputs)
    return ins, outs, sems, aliases


def _split_outputs(tasks, flat):
    res, o = [], 0
    for t in tasks:
        res.append(list(flat[o:o + len(t.outputs)]))
        o += len(t.outputs)
    return res


def _run(name, tasks):
    ins, outs, sems, aliases = _task_plumbing(tasks, 0, 0)

    def body(*refs):
        bound = _bind_tasks(tasks, refs[:len(ins)], refs[len(ins):len(ins) + len(outs)], refs[len(ins) + len(outs):])
        for phase in range(3):
            _run_phase(bound, phase)

    flat = pl.pallas_call(
        body, name=name, out_shape=outs, in_specs=[HBM_SPEC] * len(ins), out_specs=[HBM_SPEC] * len(outs),
        input_output_aliases=aliases, scratch_shapes=sems,
    )(*ins)
    return _split_outputs(tasks, flat)


def _place():
    x, y, c = lax.axis_index("x"), lax.axis_index("y"), lax.axis_index("c")
    chips = [(1 - x, y), (x, 1 - y), (1 - x, 1 - y)]
    return x, y, c, 2 * x + y, chips


def _gather_task(stacks):
    n = len(stacks)
    halves = [s.shape[-2] // 2 for s in stacks]

    def bind(_, dst, sems):
        send_a, recv_a, send_b, recv_b = sems
        x, y, c, me, chips = _place()
        sibling = (x, y, 1 - c)

        def half(w, chip, hc):
            rows = pl.ds(hc * halves[w], halves[w])
            if len(stacks[w].shape) == 3:
                return dst[w].at[chip, rows, :]
            cols = stacks[w].shape[1] // N_CHIPS
            return dst[w].at[rows, pl.ds(chip * cols, cols)]

        def over_ici(j, w, block):
            return pltpu.make_async_remote_copy(
                src_ref=half(w, block, c), dst_ref=half(w, block, c), send_sem=send_a.at[j * n + w],
                recv_sem=recv_a.at[j * n + w], device_id=(*chips[j], c), device_id_type=MESH)

        def over_d2d(j, w, hc):
            got = half(w, 2 * chips[j][0] + chips[j][1], hc)
            return pltpu.make_async_remote_copy(
                src_ref=got, dst_ref=got, send_sem=send_b.at[j * n + w], recv_sem=recv_b.at[j * n + w],
                device_id=sibling, device_id_type=MESH)

        pairs = [(j, w) for j in range(3) for w in range(n)]

        def start():
            for j, w in pairs:
                over_ici(j, w, me).start()

        def mid():
            for j, w in pairs:
                over_ici(j, w, 2 * chips[j][0] + chips[j][1]).wait_recv()
                over_d2d(j, w, c).start()

        def finish():
            for j, w in pairs:
                over_d2d(j, w, 1 - c).wait_recv()
            for j, w in pairs:
                over_ici(j, w, me).wait_send()
                over_d2d(j, w, c).wait_send()

        return start, mid, finish

    return _Task(stacks, _like(stacks), {w: w for w in range(n)}, [pltpu.SemaphoreType.DMA((3 * n,))] * 4, bind)


def _relayed_gather_task(stacks):
    n = len(stacks)
    halves = [s.shape[-2] // 2 for s in stacks]

    def bind(_, dst, sems):
        send_a, recv_a, send_b, recv_b = sems
        x, y, c, me, _ = _place()
        sibling = (x, y, 1 - c)
        across_x, across_y = (1 - x, y, c), (x, 1 - y, c)
        kx, ky, kd = 2 * (1 - x) + y, 2 * x + (1 - y), 2 * (1 - x) + (1 - y)

        def rows_of(w, chip, start, size):
            rows = pl.ds(start, size)
            if len(stacks[w].shape) == 3:
                return dst[w].at[chip, rows, :]
            cols = stacks[w].shape[1] // N_CHIPS
            return dst[w].at[rows, pl.ds(chip * cols, cols)]

        def half(w, chip, hc):
            return rows_of(w, chip, hc * halves[w], halves[w])

        def quarter(w, chip, q):
            return rows_of(w, chip, c * halves[w] + q * (halves[w] // 2), halves[w] // 2)

        def over_ici(k, w, block, to):
            return pltpu.make_async_remote_copy(src_ref=block, dst_ref=block, send_sem=send_a.at[4 * w + k],
                                                recv_sem=recv_a.at[4 * w + k], device_id=to, device_id_type=MESH)

        def over_d2d(k, w, chip, hc):
            got = half(w, chip, hc)
            return pltpu.make_async_remote_copy(src_ref=got, dst_ref=got, send_sem=send_b.at[3 * w + k],
                                                recv_sem=recv_b.at[3 * w + k], device_id=sibling, device_id_type=MESH)

        def start():
            for w in range(n):
                over_ici(0, w, half(w, me, c), across_x).start()
                over_ici(1, w, half(w, me, c), across_y).start()

        def mid():
            for w in range(n):
                over_ici(0, w, half(w, kx, c), across_x).wait_recv()
                over_ici(2, w, quarter(w, kx, 0), across_y).start()
                over_ici(1, w, half(w, ky, c), across_y).wait_recv()
                over_ici(3, w, quarter(w, ky, 1), across_x).start()
                over_d2d(0, w, kx, c).start()
                over_d2d(1, w, ky, c).start()
            for w in range(n):
                over_ici(2, w, quarter(w, kd, 0), across_y).wait_recv()
                over_ici(3, w, quarter(w, kd, 1), across_x).wait_recv()
                over_d2d(2, w, kd, c).start()

        def finish():
            for w in range(n):
                for k, chip in enumerate((kx, ky, kd)):
                    over_d2d(k, w, chip, 1 - c).wait_recv()
            for w in range(n):
                over_ici(0, w, half(w, me, c), across_x).wait_send()
                over_ici(1, w, half(w, me, c), across_y).wait_send()
                over_ici(2, w, quarter(w, kx, 0), across_y).wait_send()
                over_ici(3, w, quarter(w, ky, 1), across_x).wait_send()
                for k, chip in enumerate((kx, ky, kd)):
                    over_d2d(k, w, chip, c).wait_send()

        return start, mid, finish

    sem_shapes = [pltpu.SemaphoreType.DMA((4 * n,))] * 2 + [pltpu.SemaphoreType.DMA((3 * n,))] * 2
    return _Task(stacks, _like(stacks), {w: w for w in range(n)}, sem_shapes, bind)


def _pair_exchange_task(grads, small=None):
    n = len(grads)
    halves = [g.shape[1] // 2 for g in grads]

    def bind(src, dst, sems):
        send, recv = sems[0], sems[1]
        x, y, c, me, _ = _place()
        sibling = (x, y, 1 - c)
        dev = 2 * me + c

        def to_sibling(w):
            return pltpu.make_async_remote_copy(
                src_ref=src[w].at[:, pl.ds((1 - c) * halves[w], halves[w]), :], dst_ref=dst[w],
                send_sem=send.at[w], recv_sem=recv.at[w], device_id=sibling, device_id_type=MESH)

        def to_peer(t, block):
            tx, ty, tc = (t >> 2) & 1, (t >> 1) & 1, t & 1
            return pltpu.make_async_remote_copy(
                src_ref=src[n], dst_ref=dst[n].at[block], send_sem=sems[2].at[t], recv_sem=sems[3].at[t],
                device_id=(x ^ tx, y ^ ty, c ^ tc), device_id_type=MESH)

        def mine():
            return pltpu.make_async_copy(src[n], dst[n].at[dev], sems[4])

        def start():
            for w in range(n):
                to_sibling(w).start()
            if small is not None:
                mine().start()
                for t in range(1, N_DEV):
                    to_peer(t, dev).start()

        def finish():
            for w in range(n):
                to_sibling(w).wait_recv()
            if small is not None:
                for t in range(1, N_DEV):
                    to_peer(t, dev ^ t).wait_recv()
            for w in range(n):
                to_sibling(w).wait_send()
            if small is not None:
                for t in range(1, N_DEV):
                    to_peer(t, dev).wait_send()
                mine().wait()

        return start, None, finish

    outputs = [jax.ShapeDtypeStruct((N_CHIPS, h, g.shape[2]), g.dtype) for g, h in zip(grads, halves)]
    sem_shapes = [pltpu.SemaphoreType.DMA((n,)), pltpu.SemaphoreType.DMA((n,))]
    inputs = list(grads)
    if small is not None:
        inputs.append(small)
        outputs.append(jax.ShapeDtypeStruct((N_DEV,) + small.shape, small.dtype))
        sem_shapes += [pltpu.SemaphoreType.DMA((N_DEV,)), pltpu.SemaphoreType.DMA((N_DEV,)), pltpu.SemaphoreType.DMA(())]
    return _Task(inputs, outputs, {}, sem_shapes, bind)


def _chip_exchange_task(summed):
    n = len(summed)

    def bind(refs, dst, sems):
        src = refs[:n]
        send, recv = sems
        x, y, c, me, chips = _place()

        def copy(j, w, block_from, block_to):
            return pltpu.make_async_remote_copy(
                src_ref=src[w].at[block_from], dst_ref=dst[w].at[block_to], send_sem=send.at[j * n + w],
                recv_sem=recv.at[j * n + w], device_id=(*chips[j], c), device_id_type=MESH)

        pairs = [(j, w) for j in range(3) for w in range(n)]

        def start():
            for j, w in pairs:
                copy(j, w, 2 * chips[j][0] + chips[j][1], me).start()

        def finish():
            for j, w in pairs:
                copy(j, w, me, 2 * chips[j][0] + chips[j][1]).wait_recv()
            for j, w in pairs:
                copy(j, w, 2 * chips[j][0] + chips[j][1], me).wait_send()

        return start, None, finish

    partials, landing = [s[0] for s in summed], [s[1] for s in summed]
    return _Task(partials + landing, _like(landing), {n + w: w for w in range(n)},
                 [pltpu.SemaphoreType.DMA((3 * n,))] * 2, bind)


def _pair_share_task(blocks):
    n = len(blocks)

    def bind(_, dst, sems):
        send, recv = sems
        x, y, c, _, _ = _place()

        def copy(w, hc):
            h = blocks[w].shape[0] // 2
            rows = dst[w].at[pl.ds(hc * h, h), :]
            return pltpu.make_async_remote_copy(src_ref=rows, dst_ref=rows, send_sem=send.at[w], recv_sem=recv.at[w],
                                                device_id=(x, y, 1 - c), device_id_type=MESH)

        def start():
            for w in range(n):
                copy(w, c).start()

        def finish():
            for w in range(n):
                copy(w, 1 - c).wait_recv()
            for w in range(n):
                copy(w, c).wait_send()

        return start, None, finish

    return _Task(blocks, _like(blocks), {w: w for w in range(n)}, [pltpu.SemaphoreType.DMA((n,))] * 2, bind)


def _carrier_call(name, body, operands, *, grid, in_specs, out_specs, out_shape, semantics, tasks):
    n_in, n_out = len(operands), len(out_shape)
    t_ins, t_outs, t_sems, aliases = _task_plumbing(tasks, n_in, n_out)
    n_ti, n_to = len(t_ins), len(t_outs)
    total = 1
    for g in grid:
        total *= g
    mid_step = max(0, total - 1 - max(1, total // 8))

    def carrier(*refs):
        outs_at = n_in + n_ti
        if tasks:
            bound = _bind_tasks(tasks, refs[n_in:outs_at], refs[outs_at + n_out:outs_at + n_out + n_to],
                                refs[outs_at + n_out + n_to:])
            step = 0
            for axis, g in enumerate(grid):
                step = step * g + pl.program_id(axis)

            @pl.when(step == 0)
            def _():
                _run_phase(bound, 0)

        body(*refs[:n_in], *refs[outs_at:outs_at + n_out])

        if tasks:
            @pl.when(step == mid_step)
            def _():
                _run_phase(bound, 1)

            @pl.when(step == total - 1)
            def _():
                _run_phase(bound, 2)

    res = pl.pallas_call(
        carrier, name=name, grid=grid, out_shape=[*out_shape, *t_outs],
        in_specs=[*in_specs, *[HBM_SPEC] * n_ti], out_specs=[*out_specs, *[HBM_SPEC] * n_to],
        input_output_aliases=aliases, scratch_shapes=t_sems,
        compiler_params=_params(("arbitrary",) * len(grid) if tasks else semantics),
    )(*operands, *t_ins)
    return list(res[:n_out]), _split_outputs(tasks, res[n_out:])


def _mm(name, a, b, *, dims, grid, a_spec, b_spec, out_shape, out_spec,
        extras=(), extra_specs=(), a_pro=None, epi=None, norm_gain=None, tasks=()):
    n_ex = len(extras)
    normed = norm_gain is not None

    def body(a_ref, b_ref, *rest):
        outs = rest[n_ex + normed:]
        av = a_ref[...]
        if a_pro is not None:
            av = a_pro(av)
        acc = lax.dot_general(av, b_ref[...], dims, preferred_element_type=F32)
        if epi is not None:
            acc = epi(acc, *[e[...] for e in rest[:n_ex]])
        outs[0][...] = acc.astype(outs[0].dtype)
        if normed:
            rstd = lax.rsqrt(jnp.mean(acc * acc, axis=-1, keepdims=True) + RMS_EPS)
            outs[1][...] = ((acc * rstd) * rest[n_ex][...]).astype(BF16)

    operands, in_specs = [a, b, *extras], [a_spec, b_spec, *extra_specs]
    out_specs, out_shapes = [out_spec], [out_shape]
    if normed:
        width = out_shape.shape[1]
        operands.append(norm_gain.reshape(1, width))
        in_specs.append(pl.BlockSpec((1, width), lambda j, i: (0, 0)))
        out_specs.append(out_spec)
        out_shapes.append(jax.ShapeDtypeStruct(out_shape.shape, BF16))
    res, carried = _carrier_call(name, body, operands, grid=grid, in_specs=in_specs, out_specs=out_specs,
                                 out_shape=out_shapes, semantics=("parallel", "parallel"), tasks=tasks)
    res = tuple(res) if normed else res[0]
    return (res, carried) if tasks else res


_NN = (((1,), (0,)), ((), ()))
_NT = (((1,), (1,)), ((), ()))
_TN = (((0,), (0,)), ((), ()))


MM_TILE_BUDGET = 46 * 1024 * 1024


def _mm_tiles(m, n, contraction, out_bytes):
    for rows, cols in ((1664, 1024), (832, 1024), (416, 1024), (416, 512)):
        tm, tn = _pick(m, rows, 16), _pick(n, cols, 128)
        if 2 * 2 * contraction * (tm + tn) + tm * tn * (4 + 2 * out_bytes) <= MM_TILE_BUDGET:
            break
    return tm, tn


def _out_bytes(out_dtype, extras):
    return jnp.dtype(out_dtype).itemsize + sum(e.dtype.itemsize for e in extras)


def _mm_nn(name, a, w, out_dtype, a_pro=None, epi=None, extras=(), norm_gain=None, tasks=()):
    m, k = a.shape
    _, n = w.shape
    tm, tn = _mm_tiles(m, n, k, _out_bytes(out_dtype, extras))
    if norm_gain is not None:
        tm, tn = _pick(m, 416, 16), n
    tile = pl.BlockSpec((tm, tn), lambda j, i: (i, j))
    return _mm(name, a, w, dims=_NN, grid=(n // tn, m // tm),
               a_spec=pl.BlockSpec((tm, k), lambda j, i: (i, 0)), b_spec=pl.BlockSpec((k, tn), lambda j, i: (0, j)),
               out_shape=jax.ShapeDtypeStruct((m, n), out_dtype), out_spec=tile,
               extras=extras, extra_specs=[tile] * len(extras), a_pro=a_pro, epi=epi, norm_gain=norm_gain, tasks=tasks)


def _mm_nt(name, a, w, out_dtype, epi=None, extras=(), tasks=()):
    m, c = a.shape
    r, _ = w.shape
    tm, tn = _mm_tiles(m, r, c, _out_bytes(out_dtype, extras))
    tile = pl.BlockSpec((tm, tn), lambda j, i: (i, j))
    return _mm(name, a, w, dims=_NT, grid=(r // tn, m // tm),
               a_spec=pl.BlockSpec((tm, c), lambda j, i: (i, 0)), b_spec=pl.BlockSpec((tn, c), lambda j, i: (j, 0)),
               out_shape=jax.ShapeDtypeStruct((m, r), out_dtype), out_spec=tile,
               extras=extras, extra_specs=[tile] * len(extras), epi=epi, tasks=tasks)


def _mm_tn(name, a, b, stacked, a_pro=None, tasks=()):
    t, ka = a.shape
    _, nb = b.shape
    ns = nb // N_CHIPS if stacked else nb
    ta, tb = _pick(ka, 512, 128), _pick(ns, 640, 128)
    if stacked:
        per = ns // tb
        out_shape = jax.ShapeDtypeStruct((N_CHIPS, ka, ns), BF16)
        out_spec = pl.BlockSpec((None, ta, tb), lambda i, j: (j // per, i, j % per))
    else:
        out_shape = jax.ShapeDtypeStruct((ka, nb), BF16)
        out_spec = pl.BlockSpec((ta, tb), lambda i, j: (i, j))
    return _mm(name, a, b, dims=_TN, grid=(ka // ta, nb // tb),
               a_spec=pl.BlockSpec((t, ta), lambda i, j: (0, i)), b_spec=pl.BlockSpec((t, tb), lambda i, j: (0, j)),
               out_shape=out_shape, out_spec=out_spec, a_pro=a_pro, tasks=tasks)


def _relu_sq(z):
    a = jnp.maximum(z, 0)
    return a * a


def _rms_fwd(name, h, g):
    m, d = h.shape
    tr = _pick(m, 512, 16)

    def body(h_ref, g_ref, o_ref):
        x = h_ref[...]
        rstd = lax.rsqrt(jnp.mean(x * x, axis=-1, keepdims=True) + RMS_EPS)
        o_ref[...] = ((x * rstd) * g_ref[...]).astype(BF16)

    row = pl.BlockSpec((tr, d), lambda i: (i, 0))
    return pl.pallas_call(
        body, name=name, grid=(m // tr,), out_shape=jax.ShapeDtypeStruct((m, d), BF16),
        in_specs=[row, pl.BlockSpec((1, d), lambda i: (0, 0))], out_specs=row,
        compiler_params=_params(("parallel",)),
    )(h, g.reshape(1, d))


def _rms_bwd_math(x, g, dn):
    rstd = lax.rsqrt(jnp.mean(x * x, axis=-1, keepdims=True) + RMS_EPS)
    xhat = x * rstd
    dxhat = dn * g
    dx = rstd * (dxhat - xhat * jnp.mean(dxhat * xhat, axis=-1, keepdims=True))
    return dx, dn * xhat


def _fold8(v):
    r, c = v.shape
    return jnp.sum(v.reshape(r // 8, 8, c), axis=0)


def _rms_bwd(name, dn, h, g, dh_in):
    m, d = h.shape
    tr = _pick(m, 512, 16)
    nt = m // tr

    def body(dn_ref, h_ref, g_ref, dh_ref, o_ref, ob_ref, dg_ref):
        dx, dgp = _rms_bwd_math(h_ref[...], g_ref[...], dn_ref[...].astype(F32))
        dh = dh_ref[...] + dx
        o_ref[...] = dh
        ob_ref[...] = dh.astype(BF16)

        @pl.when(pl.program_id(0) == 0)
        def _():
            dg_ref[...] = jnp.zeros_like(dg_ref)

        dg_ref[...] += _fold8(dgp)

    row = pl.BlockSpec((tr, d), lambda i: (i, 0))
    return pl.pallas_call(
        body, name=name, grid=(nt,),
        out_shape=(jax.ShapeDtypeStruct((m, d), F32), jax.ShapeDtypeStruct((m, d), BF16),
                   jax.ShapeDtypeStruct((8, d), F32)),
        in_specs=[row, row, pl.BlockSpec((1, d), lambda i: (0, 0)), row],
        out_specs=(row, row, pl.BlockSpec((8, d), lambda i: (0, 0))),
        compiler_params=_params(("arbitrary",)),
    )(dn, h, g.reshape(1, d), dh_in)


def _rms_bwd_tokens(name, dn, h, g, dh_in):
    m, d = h.shape
    nb = m // BLOCK

    def body(dn_ref, h_ref, g_ref, dh_ref, gx_ref, first_ref, dg_ref):
        i = pl.program_id(0)
        dx, dgp = _rms_bwd_math(h_ref[...], g_ref[...], dn_ref[...].astype(F32))
        dh = dh_ref[...] + dx
        gx_ref[...] = dh

        @pl.when(i == 0)
        def _():
            first_ref[...] = dh
            dg_ref[...] = jnp.zeros_like(dg_ref)

        dg_ref[...] += _fold8(dgp)

    row = pl.BlockSpec((BLOCK, d), lambda i: (i, 0))
    return pl.pallas_call(
        body, name=name, grid=(nb,),
        out_shape=(jax.ShapeDtypeStruct((m - ROW0, d), F32), jax.ShapeDtypeStruct((ROW0, d), F32),
                   jax.ShapeDtypeStruct((8, d), F32)),
        in_specs=[row, row, pl.BlockSpec((1, d), lambda i: (0, 0)), row],
        out_specs=(pl.BlockSpec((BLOCK, d), lambda i: (jnp.maximum(i - 1, 0), 0)),
                   pl.BlockSpec((ROW0, d), lambda i: (0, 0)), pl.BlockSpec((8, d), lambda i: (0, 0))),
        compiler_params=_params(("arbitrary",)),
    )(dn, h, g.reshape(1, d), dh_in)


def _loss_head(h, g, target):
    m, d = h.shape
    tr = BLOCK

    def body(h_ref, g_ref, t_ref, loss_ref, o_ref, ob_ref, dg_ref):
        i = pl.program_id(0)
        x = h_ref[...]
        gv = g_ref[...]
        rstd = lax.rsqrt(jnp.mean(x * x, axis=-1, keepdims=True) + RMS_EPS)
        err = jnp.where(i > 0, (x * rstd) * gv - t_ref[...], 0.0)
        dx, dgp = _rms_bwd_math(x, gv, err * (1.0 / d))
        o_ref[...] = dx
        ob_ref[...] = dx.astype(BF16)

        @pl.when(i == 0)
        def _():
            dg_ref[...] = jnp.zeros_like(dg_ref)
            loss_ref[...] = jnp.zeros_like(loss_ref)

        dg_ref[...] += _fold8(dgp)
        sq = jnp.mean(err * err, axis=-1, keepdims=True)
        loss_ref[...] += 0.5 * jnp.sum(sq, axis=0, keepdims=True)

    row = pl.BlockSpec((tr, d), lambda i: (i, 0))
    return pl.pallas_call(
        body, name="loss_head", grid=(m // tr,),
        out_shape=(jax.ShapeDtypeStruct((8, 128), F32), jax.ShapeDtypeStruct((m, d), F32),
                   jax.ShapeDtypeStruct((m, d), BF16), jax.ShapeDtypeStruct((8, d), F32)),
        in_specs=[row, pl.BlockSpec((1, d), lambda i: (0, 0)),
                  pl.BlockSpec((tr, d), lambda i: (jnp.maximum(i - 1, 0), 0))],
        out_specs=(pl.BlockSpec((8, 128), lambda i: (0, 0)), row, row,
                   pl.BlockSpec((8, d), lambda i: (0, 0))),
        compiler_params=_params(("arbitrary",)),
    )(h, g.reshape(1, d), target)


HALO = 16


def _shift_down(cat, k):
    return pltpu.roll(cat, k, axis=0)[HALO:]


def _shift_up(cat, k):
    n = cat.shape[0]
    return pltpu.roll(cat, n - k, axis=0)[:n - HALO]


def _conv_fwd(bcu, cw):
    m, d3 = bcu.shape
    d = d3 // 3
    tr, tc = _pick(m, 416, 16), _pick(d, 512, 128)
    hb = tr // HALO

    def body(x_ref, xb_ref, w_ref, o_ref):
        i = pl.program_id(0)
        for j in range(d // tc):
            col = slice(j * tc, (j + 1) * tc)
            cb, cc, cu = (slice(q * d + j * tc, q * d + (j + 1) * tc) for q in range(3))
            v = x_ref[:, cc].astype(F32) * x_ref[:, cu].astype(F32)
            vh = jnp.where(i > 0, xb_ref[:, cc].astype(F32) * xb_ref[:, cu].astype(F32), 0.0)
            cat = jnp.concatenate([vh, v], axis=0)
            w = w_ref[:, col]
            conv = w[2:3] * v + w[1:2] * _shift_down(cat, 1) + w[0:1] * _shift_down(cat, 2)
            o_ref[:, col] = (x_ref[:, cb].astype(F32) * conv).astype(BF16)

    return pl.pallas_call(
        body, name="conv_fwd", grid=(m // tr,), out_shape=jax.ShapeDtypeStruct((m, d), BF16),
        in_specs=[pl.BlockSpec((tr, d3), lambda i: (i, 0)),
                  pl.BlockSpec((HALO, d3), lambda i: (jnp.maximum(i * hb - 1, 0), 0)),
                  pl.BlockSpec((8, d), lambda i: (0, 0))],
        out_specs=pl.BlockSpec((tr, d), lambda i: (i, 0)),
        compiler_params=_params(("parallel",)),
    )(bcu, bcu, cw)


def _conv_bwd(bcu, cw, dg):
    m, d3 = bcu.shape
    d = d3 // 3
    tr, tc = _pick(m, 208, 16), _pick(d, 512, 128)
    hb, nt = tr // HALO, m // tr

    def body(x_ref, xb_ref, xa_ref, dg_ref, dga_ref, w_ref, o_ref, dw_ref):
        i = pl.program_id(0)

        @pl.when(i == 0)
        def _():
            dw_ref[...] = jnp.zeros_like(dw_ref)

        for j in range(d // tc):
            col = slice(j * tc, (j + 1) * tc)
            cb, cc, cu = (slice(q * d + j * tc, q * d + (j + 1) * tc) for q in range(3))
            w = w_ref[:, col]
            b, c, u = x_ref[:, cb].astype(F32), x_ref[:, cc].astype(F32), x_ref[:, cu].astype(F32)
            dgv = dg_ref[:, col].astype(F32)
            v = c * u
            vh = jnp.where(i > 0, xb_ref[:, cc].astype(F32) * xb_ref[:, cu].astype(F32), 0.0)
            cat = jnp.concatenate([vh, v], axis=0)
            v1, v2 = _shift_down(cat, 1), _shift_down(cat, 2)
            dconv = dgv * b
            o_ref[:, cb] = (dgv * (w[2:3] * v + w[1:2] * v1 + w[0:1] * v2)).astype(BF16)
            taps = [jnp.sum(dconv * t, axis=0, keepdims=True) for t in (v2, v1, v)]
            dw_ref[:, col] += jnp.concatenate(taps + [jnp.zeros((5, tc), F32)], axis=0)
            nxt = jnp.where(i < nt - 1, dga_ref[:, col].astype(F32) * xa_ref[:, cb].astype(F32), 0.0)
            cat2 = jnp.concatenate([dconv, nxt], axis=0)
            dv = w[2:3] * dconv + w[1:2] * _shift_up(cat2, 1) + w[0:1] * _shift_up(cat2, 2)
            o_ref[:, cc] = (dv * u).astype(BF16)
            o_ref[:, cu] = (dv * c).astype(BF16)

    def rows(width):
        return pl.BlockSpec((tr, width), lambda i: (i, 0))

    def before(width):
        return pl.BlockSpec((HALO, width), lambda i: (jnp.maximum(i * hb - 1, 0), 0))

    def after(width):
        return pl.BlockSpec((HALO, width), lambda i: (jnp.minimum((i + 1) * hb, m // HALO - 1), 0))

    return pl.pallas_call(
        body, name="conv_bwd", grid=(nt,),
        out_shape=(jax.ShapeDtypeStruct((m, d3), BF16), jax.ShapeDtypeStruct((8, d), F32)),
        in_specs=[rows(d3), before(d3), after(d3), rows(d), after(d), pl.BlockSpec((8, d), lambda i: (0, 0))],
        out_specs=(rows(d3), pl.BlockSpec((8, d), lambda i: (0, 0))),
        compiler_params=_params(("arbitrary",)),
    )(bcu, bcu, bcu, dg, dg, cw)


PAIR = 2 * HEAD_DIM


def _rope_tables(m):
    pad = ROW0 - N_META
    pos = jnp.arange(m, dtype=F32) - pad
    inv = ROPE_THETA ** (-jnp.arange(0, HEAD_DIM, 2, dtype=F32) / HEAD_DIM)
    ang = pos[:, None] * inv[None, :]
    cos, sin = jnp.cos(ang), jnp.sin(ang)
    return jnp.tile(jnp.concatenate([cos, cos], axis=1), (1, 2)), jnp.tile(jnp.concatenate([-sin, sin], axis=1), (1, 2))


def _rope_pair(x, c, s):
    half = HEAD_DIM // 2
    lane = lax.broadcasted_iota(jnp.int32, x.shape, 1)
    swapped = jnp.where(lane % HEAD_DIM < half, pltpu.roll(x, PAIR - half, axis=1), pltpu.roll(x, half, axis=1))
    return x * c + swapped * s


def _attn_mask(i, rows):
    r = lax.broadcasted_iota(jnp.int32, (rows, 2 * BLOCK), 0) % BLOCK
    cidx = lax.broadcasted_iota(jnp.int32, (rows, 2 * BLOCK), 1)
    key = (i - 1) * BLOCK + cidx
    return (cidx > r) & (cidx <= r + BLOCK) & (key >= ROW0 - N_META)


BAND = 2 * BLOCK


def _rope_qk(qkv, cos, sin):
    m, width = qkv.shape
    scale = HEAD_DIM ** -0.5
    nq, nk = N_Q_HEADS // 2, N_KV_HEADS // 2

    def body(x_ref, c_ref, s_ref, o_ref):
        c, s = c_ref[...], s_ref[...]
        cq, sq = c * scale, s * scale
        for t in range(nq + 2 * nk):
            col = slice(t * PAIR, (t + 1) * PAIR)
            if t < nq:
                o_ref[:, col] = _rope_pair(x_ref[:, col].astype(F32), cq, sq).astype(BF16)
            elif t < nq + nk:
                o_ref[:, col] = _rope_pair(x_ref[:, col].astype(F32), c, s).astype(BF16)
            else:
                o_ref[:, col] = x_ref[:, col]

    row = pl.BlockSpec((BLOCK, width), lambda i: (i, 0))
    tab = pl.BlockSpec((BLOCK, PAIR), lambda i: (i, 0))
    return pl.pallas_call(
        body, name="rope_qk", grid=(m // BLOCK,), out_shape=jax.ShapeDtypeStruct((m, width), BF16),
        in_specs=[row, tab, tab], out_specs=row, compiler_params=_params(("parallel",)),
    )(qkv, cos, sin)


def _pair_rows(ref, h):
    pairs = N_Q_HEADS // N_KV_HEADS // 2
    return jnp.concatenate([ref[:, (h * pairs + g) * PAIR:(h * pairs + g + 1) * PAIR] for g in range(pairs)], axis=0)


def _twice(x):
    z = jnp.zeros_like(x)
    return jnp.concatenate([jnp.concatenate([x, z], axis=1), jnp.concatenate([z, x], axis=1)], axis=0)


def _kv_band(cur_ref, prev_ref, h):
    k0, v0 = N_Q_HEADS * HEAD_DIM + h * HEAD_DIM, (N_Q_HEADS + N_KV_HEADS) * HEAD_DIM + h * HEAD_DIM
    p0, p1 = h * HEAD_DIM, (N_KV_HEADS + h) * HEAD_DIM
    k = jnp.concatenate([prev_ref[:, p0:p0 + HEAD_DIM], cur_ref[:, k0:k0 + HEAD_DIM]], axis=0)
    v = jnp.concatenate([prev_ref[:, p1:p1 + HEAD_DIM], cur_ref[:, v0:v0 + HEAD_DIM]], axis=0)
    return k, v


def _attn2_specs(width):
    kvw = 2 * N_KV_HEADS * HEAD_DIM
    cur = pl.BlockSpec((BLOCK, width), lambda i: (i, 0))
    prev = pl.BlockSpec((BLOCK, kvw), lambda i: (jnp.maximum(i - 1, 0), N_Q_HEADS * HEAD_DIM // kvw))
    return cur, prev


def _attn2_fwd(qkr, sink2):
    m, width = qkr.shape
    nb, rows = m // BLOCK, N_Q_HEADS // N_KV_HEADS // 2 * BLOCK
    dq = N_Q_HEADS * HEAD_DIM

    def body(x_ref, prev_ref, s_ref, o_ref, p_ref):
        allowed = _attn_mask(pl.program_id(0), rows)
        col0 = lax.broadcasted_iota(jnp.int32, (rows, BAND), 1) == 0
        lane = lax.broadcasted_iota(jnp.int32, (rows, PAIR), 1)
        rsel = lax.broadcasted_iota(jnp.int32, (2 * BAND, PAIR), 0) < BAND
        lsel = lax.broadcasted_iota(jnp.int32, (2 * BAND, PAIR), 1) < HEAD_DIM
        ones2 = jnp.where(rsel == lsel, 1.0, 0.0).astype(BF16)
        for h in range(N_KV_HEADS):
            k, v = _kv_band(x_ref, prev_ref, h)
            s2 = lax.dot_general(_pair_rows(x_ref, h), _twice(k), _NT, preferred_element_type=F32)
            sink = s_ref[h]
            e, mx = [], []
            for half in range(2):
                s = jnp.where(allowed, s2[:, half * BAND:(half + 1) * BAND], NEG_INF)
                mx.append(jnp.maximum(jnp.max(s, axis=-1, keepdims=True), sink[:, half * HEAD_DIM:half * HEAD_DIM + 1]))
                e.append(jnp.exp(s - mx[half]).astype(BF16))
            eb2 = jnp.concatenate(e, axis=1)
            es2 = jnp.exp(sink - jnp.where(lane < HEAD_DIM, mx[0], mx[1]))
            ov2 = jnp.dot(eb2, _twice(v), preferred_element_type=F32)
            inv2 = 1.0 / (jnp.dot(eb2, ones2, preferred_element_type=F32) + es2)
            o2 = (ov2 * inv2).astype(BF16)
            ps2 = es2 * inv2
            for g in range(rows // BLOCK):
                col = (h * (rows // BLOCK) + g) * PAIR
                o_ref[:, col:col + PAIR] = o2[g * BLOCK:(g + 1) * BLOCK, :]
            for half in range(2):
                at = half * HEAD_DIM
                p = jnp.where(col0, ps2[:, at:at + 1], e[half].astype(F32) * inv2[:, at:at + 1])
                p_ref[h, :, half * BAND:(half + 1) * BAND] = p.astype(BF16)

    cur, prev = _attn2_specs(width)
    return pl.pallas_call(
        body, name="attn_fwd", grid=(nb,),
        out_shape=(jax.ShapeDtypeStruct((m, dq), BF16), jax.ShapeDtypeStruct((N_KV_HEADS, nb * rows, 2 * BAND), BF16)),
        in_specs=[cur, prev, pl.BlockSpec((N_KV_HEADS, rows, PAIR), lambda i: (0, 0, 0))],
        out_specs=(pl.BlockSpec((BLOCK, dq), lambda i: (i, 0)),
                   pl.BlockSpec((N_KV_HEADS, rows, 2 * BAND), lambda i: (0, i, 0))),
        compiler_params=_params(("parallel",)),
    )(qkr, qkr, sink2)


def _attn2_bwd(qkr, p, o, do):
    m, width = qkr.shape
    nb, rows = m // BLOCK, N_Q_HEADS // N_KV_HEADS // 2 * BLOCK
    dq = N_Q_HEADS * HEAD_DIM

    def body(x_ref, prev_ref, p_ref, o_ref, do_ref, dq_ref, dk_ref, dv_ref, ds_ref):
        colz = lax.broadcasted_iota(jnp.int32, (rows, 2 * BAND), 1) % BAND == 0
        even = lax.broadcasted_iota(jnp.int32, (rows, PAIR), 1) < HEAD_DIM

        @pl.when(pl.program_id(0) == 0)
        def _():
            ds_ref[...] = jnp.zeros_like(ds_ref)

        for h in range(N_KV_HEADS):
            k, v = _kv_band(x_ref, prev_ref, h)
            k2, v2 = _twice(k), _twice(v)
            q2, do2, pv = _pair_rows(x_ref, h), _pair_rows(do_ref, h), p_ref[h]
            prod = do2.astype(F32) * _pair_rows(o_ref, h).astype(F32)
            delta = [jnp.sum(jnp.where(even, prod, 0.0), axis=-1, keepdims=True),
                     jnp.sum(jnp.where(even, 0.0, prod), axis=-1, keepdims=True)]
            dp2 = lax.dot_general(do2, v2, _NT, preferred_element_type=F32)
            pb = jnp.where(colz, jnp.zeros_like(pv), pv)
            ds = [(pb[:, half * BAND:(half + 1) * BAND].astype(F32)
                   * (dp2[:, half * BAND:(half + 1) * BAND] - delta[half])).astype(BF16) for half in range(2)]
            dsb2 = jnp.concatenate(ds, axis=1)
            dq2 = jnp.dot(dsb2, k2, preferred_element_type=F32).astype(BF16)
            for g in range(rows // BLOCK):
                col = (h * (rows // BLOCK) + g) * PAIR
                dq_ref[:, col:col + PAIR] = dq2[g * BLOCK:(g + 1) * BLOCK, :]
            dkk = lax.dot_general(q2, dsb2, _TN, preferred_element_type=F32)
            dk_ref[h] = (dkk[:HEAD_DIM, :BAND] + dkk[HEAD_DIM:, BAND:]).T
            dvv = lax.dot_general(do2, pb, _TN, preferred_element_type=F32)
            dv_ref[h] = (dvv[:HEAD_DIM, :BAND] + dvv[HEAD_DIM:, BAND:]).T
            ds_ref[h] -= jnp.where(even, pv[:, 0:1].astype(F32) * delta[0], pv[:, BAND:BAND + 1].astype(F32) * delta[1])

    cur, prev = _attn2_specs(width)
    heads = pl.BlockSpec((BLOCK, dq), lambda i: (i, 0))
    band = pl.BlockSpec((N_KV_HEADS, None, BAND, HEAD_DIM), lambda i: (0, i, 0, 0))
    band_shape = jax.ShapeDtypeStruct((N_KV_HEADS, nb, BAND, HEAD_DIM), F32)
    sink = pl.BlockSpec((N_KV_HEADS, rows, PAIR), lambda i: (0, 0, 0))
    return pl.pallas_call(
        body, name="attn_bwd", grid=(nb,),
        out_shape=(jax.ShapeDtypeStruct((m, dq), BF16), band_shape, band_shape,
                   jax.ShapeDtypeStruct((N_KV_HEADS, rows, PAIR), F32)),
        in_specs=[cur, prev, pl.BlockSpec((N_KV_HEADS, rows, 2 * BAND), lambda i: (0, i, 0)), heads, heads],
        out_specs=(heads, band, band, sink), compiler_params=_params(("arbitrary",)),
    )(qkr, qkr, p, o, do)


def _rope_qk_bwd(dq, dkb, dvb, cos, sin):
    nb = dkb.shape[1]
    width = (N_Q_HEADS + 2 * N_KV_HEADS) * HEAD_DIM
    scale = HEAD_DIM ** -0.5
    nq, nk = N_Q_HEADS // 2, N_KV_HEADS // 2

    def body(dq_ref, kc_ref, kn_ref, vc_ref, vn_ref, c_ref, s_ref, o_ref):
        last = pl.program_id(0) == nb - 1
        c, s = c_ref[...], -s_ref[...]

        def band_sum(cur_ref, nxt_ref, t):
            return jnp.concatenate([cur_ref[2 * t + e, BLOCK:, :] + jnp.where(last, 0.0, nxt_ref[2 * t + e, :BLOCK, :])
                                    for e in range(2)], axis=1)

        cq, sq = c * scale, s * scale
        for t in range(nq):
            col = slice(t * PAIR, (t + 1) * PAIR)
            o_ref[:, col] = _rope_pair(dq_ref[:, col].astype(F32), cq, sq).astype(BF16)
        for t in range(nk):
            o_ref[:, (nq + t) * PAIR:(nq + t + 1) * PAIR] = _rope_pair(band_sum(kc_ref, kn_ref, t), c, s).astype(BF16)
            o_ref[:, (nq + nk + t) * PAIR:(nq + nk + t + 1) * PAIR] = band_sum(vc_ref, vn_ref, t).astype(BF16)

    tab = pl.BlockSpec((BLOCK, PAIR), lambda i: (i, 0))
    cur = pl.BlockSpec((N_KV_HEADS, None, BAND, HEAD_DIM), lambda i: (0, i, 0, 0))
    nxt = pl.BlockSpec((N_KV_HEADS, None, BAND, HEAD_DIM), lambda i: (0, jnp.minimum(i + 1, nb - 1), 0, 0))
    return pl.pallas_call(
        body, name="rope_qk_bwd", grid=(nb,), out_shape=jax.ShapeDtypeStruct((nb * BLOCK, width), BF16),
        in_specs=[pl.BlockSpec((BLOCK, N_Q_HEADS * HEAD_DIM), lambda i: (i, 0)), cur, nxt, cur, nxt, tab, tab],
        out_specs=pl.BlockSpec((BLOCK, width), lambda i: (i, 0)),
        compiler_params=_params(("parallel",)),
    )(dq, dkb, dkb, dvb, dvb, cos, sin)


def _tiles2d(r, c):
    tc = _pick(c, 2048, 128) if c % 128 == 0 else c
    tr = _pick(r, max(8, (1 << 20) // tc // 8 * 8), 8) if r % 8 == 0 else r
    return tr, tc


def _cast_bf16(name, w, place, wide):
    r, c = w.shape
    tr, tc = _tiles2d(r, c)
    if tr % 16:
        tr = r
    nc = c // tc

    def body(place_ref, w_ref, o_ref):
        o_ref[...] = w_ref[...].astype(BF16)

    if wide:
        out_shape = jax.ShapeDtypeStruct((r, N_CHIPS * c), BF16)
        out_spec = pl.BlockSpec((tr, tc), lambda i, j, p: (i, p[1] * nc + j))
    else:
        out_shape = jax.ShapeDtypeStruct((N_CHIPS, r, c), BF16)
        out_spec = pl.BlockSpec((None, tr, tc), lambda i, j, p: (p[1], i, j))
    return pl.pallas_call(
        body, name=name, out_shape=out_shape,
        grid_spec=pltpu.PrefetchScalarGridSpec(
            num_scalar_prefetch=1, grid=(r // tr, nc),
            in_specs=[pl.BlockSpec((tr, tc), lambda i, j, p: (i, j))], out_specs=out_spec),
        compiler_params=_params(("parallel", "parallel")),
    )(place, w)


def _pair_sum(name, g, got, place):
    n, r, c = g.shape
    half = r // 2
    tr, tc = _tiles2d(half, c)
    nh = half // tr

    def body(place_ref, g_ref, got_ref, o_ref, own_ref):
        s = (g_ref[...].astype(F32) + got_ref[...].astype(F32)).astype(BF16)
        o_ref[...] = s

        @pl.when(pl.program_id(2) == place_ref[1])
        def _():
            own_ref[...] = s

    tile = pl.BlockSpec((None, tr, tc), lambda i, j, k, p: (k, i, j))
    shape = jax.ShapeDtypeStruct((n, half, c), BF16)
    return pl.pallas_call(
        body, name=name, out_shape=(shape, shape),
        grid_spec=pltpu.PrefetchScalarGridSpec(
            num_scalar_prefetch=1, grid=(nh, c // tc, n),
            in_specs=[pl.BlockSpec((None, tr, tc), lambda i, j, k, p: (k, p[0] * nh + i, j)), tile],
            out_specs=(tile, pl.BlockSpec((None, tr, tc), lambda i, j, k, p: (p[1], i, j)))),
        compiler_params=_params(("parallel", "parallel", "arbitrary")),
    )(place, g, got)


def _chip_sum(name, parts, place):
    n, half, c = parts.shape
    tr, tc = _tiles2d(half, c)
    nh = half // tr

    def body(place_ref, p0, p1, p2, p3, o_ref):
        o_ref[...] = ((p0[...].astype(F32) + p1[...].astype(F32)) + p2[...].astype(F32)) + p3[...].astype(F32)

    def chip(k):
        return pl.BlockSpec((None, tr, tc), lambda i, j, p: (k, i, j))

    return pl.pallas_call(
        body, name=name, out_shape=jax.ShapeDtypeStruct((2 * half, c), F32),
        grid_spec=pltpu.PrefetchScalarGridSpec(
            num_scalar_prefetch=1, grid=(nh, c // tc),
            in_specs=[chip(k) for k in range(n)],
            out_specs=pl.BlockSpec((tr, tc), lambda i, j, p: (p[0] * nh + i, j))),
        compiler_params=_params(("parallel", "parallel")),
    )(place, parts, parts, parts, parts)


def _dev_sum(gathered):
    def body(g_ref, o_ref):
        acc = g_ref[0]
        for k in range(1, N_DEV):
            acc = acc + g_ref[k]
        o_ref[...] = acc

    return pl.pallas_call(body, name="dev_sum", out_shape=jax.ShapeDtypeStruct(gathered.shape[1:], F32))(gathered)


def _adamw(name, w, g, m, v):
    r, c = w.shape
    tr, tc = _tiles2d(r, c)
    if r % 8 == 0:
        tr = _pick(r, max(8, (1 << 18) // tc // 8 * 8), 8)

    def body(w_ref, g_ref, m_ref, v_ref, go_ref, d_ref, mo_ref, vo_ref):
        gv = g_ref[...]
        go_ref[...] = gv
        mn = ADAM_B1 * m_ref[...] + (1.0 - ADAM_B1) * gv
        vn = ADAM_B2 * v_ref[...] + (1.0 - ADAM_B2) * jnp.square(gv)
        m_hat = mn / (1.0 - ADAM_B1 ** ADAM_STEP)
        v_hat = vn / (1.0 - ADAM_B2 ** ADAM_STEP)
        d_ref[...] = -ADAM_LR * (m_hat / (jnp.sqrt(v_hat) + ADAM_EPS) + ADAM_WD * w_ref[...])
        mo_ref[...] = mn
        vo_ref[...] = vn

    tile = pl.BlockSpec((tr, tc), lambda i, j: (i, j))
    shape = jax.ShapeDtypeStruct((r, c), F32)
    return pl.pallas_call(
        body, name=name, grid=(r // tr, c // tc), out_shape=(shape,) * 4,
        in_specs=[tile] * 4, out_specs=(tile,) * 4, compiler_params=_params(("parallel", "parallel")),
    )(w, g, m, v)


MATRICES = ("w_in_conv", "w_out_conv", "w_up_0", "w_down_0", "w_qkv", "w_o", "w_up_1", "w_down_1")
COLUMN_SHARDED = ("w_in_conv", "w_up_0", "w_qkv", "w_up_1")
NORMS = ("norm_mix_0", "norm_mlp_0", "norm_mix_1", "norm_mlp_1", "norm_final")


def _rows(stack):
    return stack.reshape(N_CHIPS * stack.shape[1], stack.shape[2])


def _stack(full):
    return full.reshape(N_CHIPS, full.shape[0] // N_CHIPS, full.shape[1])


def _add_residual(acc, res):
    return acc + res


def _relu_sq_grad(acc, z):
    return acc * (2.0 * jnp.maximum(z.astype(F32), 0.0))


def _step(x, target, stacks, small, norms, sinks, place, update):
    d = D_MODEL
    dc = d // N_CHIPS
    pad = ROW0 - N_META
    m = x.shape[0] + ROW0
    grp = N_Q_HEADS // N_KV_HEADS
    cos, sin = _rope_tables(m)
    pairs = grp // 2
    sink2 = jnp.broadcast_to(sinks.astype(F32).reshape(N_KV_HEADS, pairs, 1, 2, 1),
                             (N_KV_HEADS, pairs, BLOCK, 2, HEAD_DIM)).reshape(N_KV_HEADS, pairs * BLOCK, PAIR)

    def gather(*names):
        return _gather_task([stacks[n] for n in names])

    def pair_sum(tag, grad, got):
        return _pair_sum("pair_sum_" + tag, grad, got, place)

    def chip_sum(tag, landed):
        return _chip_sum("chip_sum_" + tag, landed, place)

    (w_in, small_all), = _run("gather_first", [_relayed_gather_task([stacks["w_in_conv"], small])])
    small_full = jnp.transpose(small_all, (1, 0, 2)).reshape(SMALL_ROWS, d)
    conv_w8 = small_full[N_META:N_META + 8]
    h0 = jnp.concatenate([jnp.zeros((pad, d), F32), small_full[:N_META], x], axis=0)

    n0 = _rms_fwd("norm_mix_0", h0, norms["norm_mix_0"])
    bcu, ((w_out, w_up0),) = _mm_nn("conv_in", n0, w_in, BF16, tasks=[gather("w_out_conv", "w_up_0")])
    gate = _conv_fwd(bcu, conv_w8)
    (h1, n1), ((w_qkv,),) = _mm_nn("conv_out", gate, _rows(w_out), F32, epi=_add_residual, extras=(h0,),
                                   norm_gain=norms["norm_mlp_0"], tasks=[gather("w_qkv")])
    z0, ((w_down0, w_o),) = _mm_nn("mlp_up_0", n1, w_up0, BF16, tasks=[gather("w_down_0", "w_o")])
    h2, ((w_up1,),) = _mm_nn("mlp_down_0", z0, _rows(w_down0), F32, a_pro=_relu_sq, epi=_add_residual,
                             extras=(h1,), tasks=[gather("w_up_1")])
    n2 = _rms_fwd("norm_mix_1", h2, norms["norm_mix_1"])
    qkv = _mm_nn("attn_qkv", n2, w_qkv, BF16)
    qkr = _rope_qk(qkv, cos, sin)
    o, probs = _attn2_fwd(qkr, sink2)
    h3, n3 = _mm_nn("attn_out", o, _rows(w_o), F32, epi=_add_residual, extras=(h2,), norm_gain=norms["norm_mlp_1"])
    z1, ((w_down1,),) = _mm_nn("mlp_up_1", n3, w_up1, BF16, tasks=[gather("w_down_1")])
    h4 = _mm_nn("mlp_down_1", z1, _rows(w_down1), F32, a_pro=_relu_sq, epi=_add_residual, extras=(h3,))

    gn = {}
    loss, dh, dh_bf, gn["norm_final"] = _loss_head(h4, norms["norm_final"], target)
    dz = _mm_nt("mlp_down_dx_1", dh_bf, _rows(w_down1), BF16, epi=_relu_sq_grad, extras=(z1,))
    g_d1 = _stack(_mm_tn("mlp_down_dw_1", z1, dh_bf, stacked=False, a_pro=_relu_sq))
    g_u1, ((got,),) = _mm_tn("mlp_up_dw_1", n3, dz, stacked=True, tasks=[_pair_exchange_task([g_d1])])
    s_d1 = pair_sum("d1", g_d1, got)
    dn, ((got,), (landed,)) = _mm_nt("mlp_up_dx_1", dz, w_up1, BF16,
                                          tasks=[_pair_exchange_task([g_u1]), _chip_exchange_task([s_d1])])
    s_u1, b_d1 = pair_sum("u1", g_u1, got), chip_sum("d1", landed)
    dh, dh_bf, gn["norm_mlp_1"] = _rms_bwd("norm_mlp_bwd_1", dn, h3, norms["norm_mlp_1"], dh)
    do = _mm_nt("attn_out_dx", dh_bf, _rows(w_o), BF16)
    g_o = _stack(_mm_tn("attn_out_dw", o, dh_bf, stacked=False))
    dq, dkb, dvb, dsink = _attn2_bwd(qkr, probs, o, do)
    dqkv = _rope_qk_bwd(dq, dkb, dvb, cos, sin)
    g_qkv, ((got,),) = _mm_tn("attn_qkv_dw", n2, dqkv, stacked=True, tasks=[_pair_exchange_task([g_o])])
    s_o = pair_sum("o", g_o, got)
    dn, ((got,), (landed,)) = _mm_nt("attn_qkv_dx", dqkv, w_qkv, BF16,
                                          tasks=[_pair_exchange_task([g_qkv]), _chip_exchange_task([s_u1])])
    s_qkv, b_u1 = pair_sum("qkv", g_qkv, got), chip_sum("u1", landed)
    dh, dh_bf, gn["norm_mix_1"] = _rms_bwd("norm_mix_bwd_1", dn, h2, norms["norm_mix_1"], dh)
    dz, ((landed_o, landed_qkv), (r_d1,)) = _mm_nt(
        "mlp_down_dx_0", dh_bf, _rows(w_down0), BF16, epi=_relu_sq_grad, extras=(z0,),
        tasks=[_chip_exchange_task([s_o, s_qkv]), _pair_share_task([b_d1])])
    b_o, b_qkv = chip_sum("o", landed_o), chip_sum("qkv", landed_qkv)
    g_d0, ((r_u1,),) = _mm_tn("mlp_down_dw_0", z0, dh_bf, stacked=False, a_pro=_relu_sq, tasks=[_pair_share_task([b_u1])])
    g_d0 = _stack(g_d0)
    g_u0, ((got,), (r_o, r_qkv)) = _mm_tn("mlp_up_dw_0", n1, dz, stacked=True,
                                          tasks=[_pair_exchange_task([g_d0]), _pair_share_task([b_o, b_qkv])])
    s_d0 = pair_sum("d0", g_d0, got)
    dn, ((got,), (landed,)) = _mm_nt("mlp_up_dx_0", dz, w_up0, BF16,
                                          tasks=[_pair_exchange_task([g_u0]), _chip_exchange_task([s_d0])])
    s_u0, b_d0 = pair_sum("u0", g_u0, got), chip_sum("d0", landed)
    dh, dh_bf, gn["norm_mlp_0"] = _rms_bwd("norm_mlp_bwd_0", dn, h1, norms["norm_mlp_0"], dh)
    dgate = _mm_nt("conv_out_dx", dh_bf, _rows(w_out), BF16)
    dbcu, g_conv_w = _conv_bwd(bcu, conv_w8, dgate)
    g_in, ((landed,), (r_d0,)) = _mm_tn("conv_in_dw", n0, dbcu, stacked=True,
                                        tasks=[_chip_exchange_task([s_u0]), _pair_share_task([b_d0])])
    b_u0 = chip_sum("u0", landed)
    g_out, ((got,), (r_u0,)) = _mm_tn("conv_out_dw", gate, dh_bf, stacked=False,
                                      tasks=[_pair_exchange_task([g_in]), _pair_share_task([b_u0])])
    g_out = _stack(g_out)
    s_in = pair_sum("in", g_in, got)
    dn, ((landed,), (got,)) = _mm_nt("conv_in_dx", dbcu, w_in, BF16,
                                     tasks=[_chip_exchange_task([s_in]), _pair_exchange_task([g_out])])
    b_in, s_out = chip_sum("in", landed), pair_sum("out", g_out, got)
    grad_x, dh_first, gn["norm_mix_0"] = _rms_bwd_tokens("norm_mix_bwd_0", dn, h0, norms["norm_mix_0"], dh)

    g_small = jnp.zeros((SMALL_ROWS, d), F32).at[:N_META].set(dh_first[pad:ROW0]).at[N_META:N_META + 8].set(g_conv_w)
    g_small = jnp.transpose(g_small.reshape(SMALL_ROWS, N_CHIPS, dc), (1, 0, 2))
    rep = jnp.zeros((8, d), F32)
    for r, n in enumerate(NORMS):
        rep = rep.at[r].set(jnp.sum(gn[n], axis=0))
    dsink = jnp.sum(dsink.reshape(N_KV_HEADS, pairs, BLOCK, 2, HEAD_DIM)[..., 0], axis=2)
    rep = rep.at[len(NORMS), :N_Q_HEADS].set(dsink.reshape(N_Q_HEADS))
    (got, rep_all), = _run("tail_pair_exchange", [_pair_exchange_task([g_small], small=rep)])
    s_small = pair_sum("small", g_small, got)
    (landed_out, landed_small), = _run("tail_chip_exchange", [_chip_exchange_task([s_out, s_small])])
    b_out, b_small = chip_sum("out", landed_out), chip_sum("small", landed_small)
    (r_out, r_small, r_in), = _run("tail_pair_share", [_pair_share_task([b_out, b_small, b_in])])
    for n, r in (("w_down_1", r_d1), ("w_up_1", r_u1), ("w_down_0", r_d0), ("w_up_0", r_u0), ("w_o", r_o),
                 ("w_qkv", r_qkv), ("w_out_conv", r_out), ("w_in_conv", r_in)):
        update(n, r)
    return loss, grad_x, r_small, rep_all


def kernel(x, meta_tokens, norm_mix_0, w_in_conv, conv_w, w_out_conv, norm_mlp_0, w_up_0, w_down_0, norm_mix_1, w_qkv, attn_sinks, w_o, norm_mlp_1, w_up_1, w_down_1, norm_final, loss_target, m_meta_tokens, m_norm_mix_0, m_w_in_conv, m_conv_w, m_w_out_conv, m_norm_mlp_0, m_w_up_0, m_w_down_0, m_norm_mix_1, m_w_qkv, m_attn_sinks, m_w_o, m_norm_mlp_1, m_w_up_1, m_w_down_1, m_norm_final, v_meta_tokens, v_norm_mix_0, v_w_in_conv, v_conv_w, v_w_out_conv, v_norm_mlp_0, v_w_up_0, v_w_down_0, v_norm_mix_1, v_w_qkv, v_attn_sinks, v_w_o, v_norm_mlp_1, v_w_up_1, v_w_down_1, v_norm_final):
    given = dict(locals())
    names = ("meta_tokens", "norm_mix_0", "w_in_conv", "conv_w", "w_out_conv", "norm_mlp_0", "w_up_0", "w_down_0",
             "norm_mix_1", "w_qkv", "attn_sinks", "w_o", "norm_mlp_1", "w_up_1", "w_down_1", "norm_final")
    d = D_MODEL
    dc = d // N_CHIPS
    chip = 2 * lax.axis_index("x") + lax.axis_index("y")
    place = jnp.stack([lax.axis_index("c"), chip]).astype(jnp.int32)

    small = jnp.zeros((SMALL_ROWS, dc), F32).at[:N_META].set(meta_tokens).at[N_META:N_META + CONV_WIDTH].set(conv_w)
    small = lax.dynamic_update_slice(jnp.zeros((N_CHIPS, SMALL_ROWS, dc), F32), small[None], (chip, 0, 0))
    stacks = {n: _cast_bf16("cast_" + n, given[n], place, n in COLUMN_SHARDED) for n in MATRICES}

    g_out, delta, new_m, new_v = {}, {}, {}, {}

    def update(n, grad):
        wt = given[n]
        shape2 = wt.shape if wt.ndim == 2 else (1, wt.shape[0])
        outs = _adamw("adamw_" + n, wt.reshape(shape2), grad.reshape(shape2),
                      given["m_" + n].reshape(shape2), given["v_" + n].reshape(shape2))
        g_out[n], delta[n], new_m[n], new_v[n] = [o.reshape(wt.shape) for o in outs]

    norms = {n: given[n] for n in NORMS}
    loss_part, grad_x, r_small, rep_all = _step(x[0], loss_target[0], stacks, small, norms, attn_sinks, place, update)
    loss = lax.psum(loss_part[0, 0], ("x", "y", "c"))
    rep_sum = _dev_sum(rep_all)
    update("meta_tokens", r_small[:N_META])
    update("conv_w", r_small[N_META:N_META + CONV_WIDTH])
    for r, n in enumerate(NORMS):
        update(n, rep_sum[r])
    update("attn_sinks", rep_sum[len(NORMS), :N_Q_HEADS])
    return (loss, grad_x[None], *[g_out[n] for n in names], *[delta[n] for n in names],
            *[new_m[n] for n in names], *[new_v[n] for n in names])
```

```python
import jax
import jax.numpy as jnp
from jax import lax
from jax.experimental import pallas as pl
from jax.experimental.pallas import tpu as pltpu

F32 = jnp.float32
BF16 = jnp.bfloat16

D_MODEL = 2048
SEQ = 8192
N_META = 16
CONV_WIDTH = 3
HEAD_DIM = 64
N_Q_HEADS = 32
N_KV_HEADS = 4
BLOCK = 128
ROPE_THETA = 10000.0
D_FF = 4 * D_MODEL
RMS_EPS = 1e-5
NEG_INF = -1e30

ADAM_LR = 0.001
ADAM_B1 = 0.9
ADAM_B2 = 0.999
ADAM_EPS = 1e-08
ADAM_WD = 0.01
ADAM_STEP = 10

N_CHIPS = 4
N_DEV = 8
MESH = pl.DeviceIdType.MESH
VMEM_LIMIT = 56 * 1024 * 1024
SMALL_ROWS = 32
ROW0 = BLOCK


def _pick(n, target, mult):
    best = None
    for t in range(mult, min(n, target) + 1, mult):
        if n % t == 0:
            best = t
    assert best is not None, (n, target, mult)
    return best


def _params(sem=None):
    return pltpu.CompilerParams(dimension_semantics=sem, vmem_limit_bytes=VMEM_LIMIT)


HBM_SPEC = pl.BlockSpec(memory_space=pltpu.HBM)


class _Task:
    def __init__(self, inputs, outputs, aliases, sem_shapes, bind):
        self.inputs, self.outputs, self.aliases = list(inputs), list(outputs), dict(aliases)
        self.sem_shapes, self.bind = list(sem_shapes), bind


def _like(arrays):
    return [jax.ShapeDtypeStruct(a.shape, a.dtype) for a in arrays]


def _bind_tasks(tasks, in_refs, out_refs, sem_refs):
    bound, i, o, s = [], 0, 0, 0
    for t in tasks:
        ni, no, ns = len(t.inputs), len(t.outputs), len(t.sem_shapes)
        bound.append(t.bind(in_refs[i:i + ni], out_refs[o:o + no], sem_refs[s:s + ns]))
        i, o, s = i + ni, o + no, s + ns
    return bound


def _run_phase(bound, phase):
    for b in bound:
        if b[phase] is not None:
            b[phase]()


def _task_plumbing(tasks, in_offset, out_offset):
    ins = [a for t in tasks for a in t.inputs]
    outs = [o for t in tasks for o in t.outputs]
    sems = [s for t in tasks for s in t.sem_shapes]
    aliases, i, o = {}, in_offset, out_offset
    for t in tasks:
        for src, dst in t.aliases.items():
            aliases[i + src] = o + dst
        i, o = i + len(t.inputs), o + len(t.outputs)
    return ins, outs, sems, aliases


def _split_outputs(tasks, flat):
    res, o = [], 0
    for t in tasks:
        res.append(list(flat[o:o + len(t.outputs)]))
        o += len(t.outputs)
    return res


def _run(name, tasks):
    ins, outs, sems, aliases = _task_plumbing(tasks, 0, 0)

    def body(*refs):
        bound = _bind_tasks(tasks, refs[:len(ins)], refs[len(ins):len(ins) + len(outs)], refs[len(ins) + len(outs):])
        for phase in range(3):
            _run_phase(bound, phase)

    flat = pl.pallas_call(
        body, name=name, out_shape=outs, in_specs=[HBM_SPEC] * len(ins), out_specs=[HBM_SPEC] * len(outs),
        input_output_aliases=aliases, scratch_shapes=sems,
    )(*ins)
    return _split_outputs(tasks, flat)


def _place():
    x, y, c = lax.axis_index("x"), lax.axis_index("y"), lax.axis_index("c")
    chips = [(1 - x, y), (x, 1 - y), (1 - x, 1 - y)]
    return x, y, c, 2 * x + y, chips


def _gather_task(stacks):
    n = len(stacks)
    halves = [s.shape[-2] // 2 for s in stacks]

    def bind(_, dst, sems):
        send_a, recv_a, send_b, recv_b = sems
        x, y, c, me, chips = _place()
        sibling = (x, y, 1 - c)

        def half(w, chip, hc):
            rows = pl.ds(hc * halves[w], halves[w])
            if len(stacks[w].shape) == 3:
                return dst[w].at[chip, rows, :]
            cols = stacks[w].shape[1] // N_CHIPS
            return dst[w].at[rows, pl.ds(chip * cols, cols)]

        def over_ici(j, w, block):
            return pltpu.make_async_remote_copy(
                src_ref=half(w, block, c), dst_ref=half(w, block, c), send_sem=send_a.at[j * n + w],
                recv_sem=recv_a.at[j * n + w], device_id=(*chips[j], c), device_id_type=MESH)

        def over_d2d(j, w, hc):
            got = half(w, 2 * chips[j][0] + chips[j][1], hc)
            return pltpu.make_async_remote_copy(
                src_ref=got, dst_ref=got, send_sem=send_b.at[j * n + w], recv_sem=recv_b.at[j * n + w],
                device_id=sibling, device_id_type=MESH)

        pairs = [(j, w) for j in range(3) for w in range(n)]

        def start():
            for j, w in pairs:
                over_ici(j, w, me).start()

        def mid():
            for j, w in pairs:
                over_ici(j, w, 2 * chips[j][0] + chips[j][1]).wait_recv()
                over_d2d(j, w, c).start()

        def finish():
            for j, w in pairs:
                over_d2d(j, w, 1 - c).wait_recv()
            for j, w in pairs:
                over_ici(j, w, me).wait_send()
                over_d2d(j, w, c).wait_send()

        return start, mid, finish

    return _Task(stacks, _like(stacks), {w: w for w in range(n)}, [pltpu.SemaphoreType.DMA((3 * n,))] * 4, bind)


def _relayed_gather_task(stacks):
    n = len(stacks)
    halves = [s.shape[-2] // 2 for s in stacks]

    def bind(_, dst, sems):
        send_a, recv_a, send_b, recv_b = sems
        x, y, c, me, _ = _place()
        sibling = (x, y, 1 - c)
        across_x, across_y = (1 - x, y, c), (x, 1 - y, c)
        kx, ky, kd = 2 * (1 - x) + y, 2 * x + (1 - y), 2 * (1 - x) + (1 - y)

        def rows_of(w, chip, start, size):
            rows = pl.ds(start, size)
            if len(stacks[w].shape) == 3:
                return dst[w].at[chip, rows, :]
            cols = stacks[w].shape[1] // N_CHIPS
            return dst[w].at[rows, pl.ds(chip * cols, cols)]

        def half(w, chip, hc):
            return rows_of(w, chip, hc * halves[w], halves[w])

        def quarter(w, chip, q):
            return rows_of(w, chip, c * halves[w] + q * (halves[w] // 2), halves[w] // 2)

        def over_ici(k, w, block, to):
            return pltpu.make_async_remote_copy(src_ref=block, dst_ref=block, send_sem=send_a.at[4 * w + k],
                                                recv_sem=recv_a.at[4 * w + k], device_id=to, device_id_type=MESH)

        def over_d2d(k, w, chip, hc):
            got = half(w, chip, hc)
            return pltpu.make_async_remote_copy(src_ref=got, dst_ref=got, send_sem=send_b.at[3 * w + k],
                                                recv_sem=recv_b.at[3 * w + k], device_id=sibling, device_id_type=MESH)

        def start():
            for w in range(n):
                over_ici(0, w, half(w, me, c), across_x).start()
                over_ici(1, w, half(w, me, c), across_y).start()

        def mid():
            for w in range(n):
                over_ici(0, w, half(w, kx, c), across_x).wait_recv()
                over_ici(2, w, quarter(w, kx, 0), across_y).start()
                over_ici(1, w, half(w, ky, c), across_y).wait_recv()
                over_ici(3, w, quarter(w, ky, 1), across_x).start()
                over_d2d(0, w, kx, c).start()
                over_d2d(1, w, ky, c).start()
            for w in range(n):
                over_ici(2, w, quarter(w, kd, 0), across_y).wait_recv()
                over_ici(3, w, quarter(w, kd, 1), across_x).wait_recv()
                over_d2d(2, w, kd, c).start()

        def finish():
            for w in range(n):
                for k, chip in enumerate((kx, ky, kd)):
                    over_d2d(k, w, chip, 1 - c).wait_recv()
            for w in range(n):
                over_ici(0, w, half(w, me, c), across_x).wait_send()
                over_ici(1, w, half(w, me, c), across_y).wait_send()
                over_ici(2, w, quarter(w, kx, 0), across_y).wait_send()
                over_ici(3, w, quarter(w, ky, 1), across_x).wait_send()
                for k, chip in enumerate((kx, ky, kd)):
                    over_d2d(k, w, chip, c).wait_send()

        return start, mid, finish

    sem_shapes = [pltpu.SemaphoreType.DMA((4 * n,))] * 2 + [pltpu.SemaphoreType.DMA((3 * n,))] * 2
    return _Task(stacks, _like(stacks), {w: w for w in range(n)}, sem_shapes, bind)


def _pair_exchange_task(grads, small=None):
    n = len(grads)
    halves = [g.shape[1] // 2 for g in grads]

    def bind(src, dst, sems):
        send, recv = sems[0], sems[1]
        x, y, c, me, _ = _place()
        sibling = (x, y, 1 - c)
        dev = 2 * me + c

        def to_sibling(w):
            return pltpu.make_async_remote_copy(
                src_ref=src[w].at[:, pl.ds((1 - c) * halves[w], halves[w]), :], dst_ref=dst[w],
                send_sem=send.at[w], recv_sem=recv.at[w], device_id=sibling, device_id_type=MESH)

        def to_peer(t, block):
            tx, ty, tc = (t >> 2) & 1, (t >> 1) & 1, t & 1
            return pltpu.make_async_remote_copy(
                src_ref=src[n], dst_ref=dst[n].at[block], send_sem=sems[2].at[t], recv_sem=sems[3].at[t],
                device_id=(x ^ tx, y ^ ty, c ^ tc), device_id_type=MESH)

        def mine():
            return pltpu.make_async_copy(src[n], dst[n].at[dev], sems[4])

        def start():
            for w in range(n):
                to_sibling(w).start()
            if small is not None:
                mine().start()
                for t in range(1, N_DEV):
                    to_peer(t, dev).start()

        def finish():
            for w in range(n):
                to_sibling(w).wait_recv()
            if small is not None:
                for t in range(1, N_DEV):
                    to_peer(t, dev ^ t).wait_recv()
            for w in range(n):
                to_sibling(w).wait_send()
            if small is not None:
                for t in range(1, N_DEV):
                    to_peer(t, dev).wait_send()
                mine().wait()

        return start, None, finish

    outputs = [jax.ShapeDtypeStruct((N_CHIPS, h, g.shape[2]), g.dtype) for g, h in zip(grads, halves)]
    sem_shapes = [pltpu.SemaphoreType.DMA((n,)), pltpu.SemaphoreType.DMA((n,))]
    inputs = list(grads)
    if small is not None:
        inputs.append(small)
        outputs.append(jax.ShapeDtypeStruct((N_DEV,) + small.shape, small.dtype))
        sem_shapes += [pltpu.SemaphoreType.DMA((N_DEV,)), pltpu.SemaphoreType.DMA((N_DEV,)), pltpu.SemaphoreType.DMA(())]
    return _Task(inputs, outputs, {}, sem_shapes, bind)


def _chip_exchange_task(summed):
    n = len(summed)

    def bind(refs, dst, sems):
        src = refs[:n]
        send, recv = sems
        x, y, c, me, chips = _place()

        def copy(j, w, block_from, block_to):
            return pltpu.make_async_remote_copy(
                src_ref=src[w].at[block_from], dst_ref=dst[w].at[block_to], send_sem=send.at[j * n + w],
                recv_sem=recv.at[j * n + w], device_id=(*chips[j], c), device_id_type=MESH)

        pairs = [(j, w) for j in range(3) for w in range(n)]

        def start():
            for j, w in pairs:
                copy(j, w, 2 * chips[j][0] + chips[j][1], me).start()

        def finish():
            for j, w in pairs:
                copy(j, w, me, 2 * chips[j][0] + chips[j][1]).wait_recv()
            for j, w in pairs:
                copy(j, w, 2 * chips[j][0] + chips[j][1], me).wait_send()

        return start, None, finish

    partials, landing = [s[0] for s in summed], [s[1] for s in summed]
    return _Task(partials + landing, _like(landing), {n + w: w for w in range(n)},
                 [pltpu.SemaphoreType.DMA((3 * n,))] * 2, bind)


def _pair_share_task(blocks):
    n = len(blocks)

    def bind(_, dst, sems):
        send, recv = sems
        x, y, c, _, _ = _place()

        def copy(w, hc):
            h = blocks[w].shape[0] // 2
            rows = dst[w].at[pl.ds(hc * h, h), :]
            return pltpu.make_async_remote_copy(src_ref=rows, dst_ref=rows, send_sem=send.at[w], recv_sem=recv.at[w],
                                                device_id=(x, y, 1 - c), device_id_type=MESH)

        def start():
            for w in range(n):
                copy(w, c).start()

        def finish():
            for w in range(n):
                copy(w, 1 - c).wait_recv()
            for w in range(n):
                copy(w, c).wait_send()

        return start, None, finish

    return _Task(blocks, _like(blocks), {w: w for w in range(n)}, [pltpu.SemaphoreType.DMA((n,))] * 2, bind)


def _carrier_call(name, body, operands, *, grid, in_specs, out_specs, out_shape, semantics, tasks):
    n_in, n_out = len(operands), len(out_shape)
    t_ins, t_outs, t_sems, aliases = _task_plumbing(tasks, n_in, n_out)
    n_ti, n_to = len(t_ins), len(t_outs)
    total = 1
    for g in grid:
        total *= g
    mid_step = max(0, total - 1 - max(1, total // 8))

    def carrier(*refs):
        outs_at = n_in + n_ti
        if tasks:
            bound = _bind_tasks(tasks, refs[n_in:outs_at], refs[outs_at + n_out:outs_at + n_out + n_to],
                                refs[outs_at + n_out + n_to:])
            step = 0
            for axis, g in enumerate(grid):
                step = step * g + pl.program_id(axis)

            @pl.when(step == 0)
            def _():
                _run_phase(bound, 0)

        body(*refs[:n_in], *refs[outs_at:outs_at + n_out])

        if tasks:
            @pl.when(step == mid_step)
            def _():
                _run_phase(bound, 1)

            @pl.when(step == total - 1)
            def _():
                _run_phase(bound, 2)

    res = pl.pallas_call(
        carrier, name=name, grid=grid, out_shape=[*out_shape, *t_outs],
        in_specs=[*in_specs, *[HBM_SPEC] * n_ti], out_specs=[*out_specs, *[HBM_SPEC] * n_to],
        input_output_aliases=aliases, scratch_shapes=t_sems,
        compiler_params=_params(("arbitrary",) * len(grid) if tasks else semantics),
    )(*operands, *t_ins)
    return list(res[:n_out]), _split_outputs(tasks, res[n_out:])


def _mm(name, a, b, *, dims, grid, a_spec, b_spec, out_shape, out_spec,
        extras=(), extra_specs=(), a_pro=None, epi=None, norm_gain=None, tasks=()):
    n_ex = len(extras)
    normed = norm_gain is not None

    def body(a_ref, b_ref, *rest):
        outs = rest[n_ex + normed:]
        av = a_ref[...]
        if a_pro is not None:
            av = a_pro(av)
        acc = lax.dot_general(av, b_ref[...], dims, preferred_element_type=F32)
        if epi is not None:
            acc = epi(acc, *[e[...] for e in rest[:n_ex]])
        outs[0][...] = acc.astype(outs[0].dtype)
        if normed:
            rstd = lax.rsqrt(jnp.mean(acc * acc, axis=-1, keepdims=True) + RMS_EPS)
            outs[1][...] = ((acc * rstd) * rest[n_ex][...]).astype(BF16)

    operands, in_specs = [a, b, *extras], [a_spec, b_spec, *extra_specs]
    out_specs, out_shapes = [out_spec], [out_shape]
    if normed:
        width = out_shape.shape[1]
        operands.append(norm_gain.reshape(1, width))
        in_specs.append(pl.BlockSpec((1, width), lambda j, i: (0, 0)))
        out_specs.append(out_spec)
        out_shapes.append(jax.ShapeDtypeStruct(out_shape.shape, BF16))
    res, carried = _carrier_call(name, body, operands, grid=grid, in_specs=in_specs, out_specs=out_specs,
                                 out_shape=out_shapes, semantics=("parallel", "parallel"), tasks=tasks)
    res = tuple(res) if normed else res[0]
    return (res, carried) if tasks else res


_NN = (((1,), (0,)), ((), ()))
_NT = (((1,), (1,)), ((), ()))
_TN = (((0,), (0,)), ((), ()))


MM_TILE_BUDGET = 46 * 1024 * 1024


def _mm_tiles(m, n, contraction, out_bytes):
    for rows, cols in ((1664, 1280), (1664, 1024), (832, 1024), (416, 1024), (416, 512)):
        tm, tn = _pick(m, rows, 16), _pick(n, cols, 128)
        if 2 * 2 * contraction * (tm + tn) + tm * tn * (4 + 2 * out_bytes) <= MM_TILE_BUDGET:
            break
    return tm, tn


def _out_bytes(out_dtype, extras):
    return jnp.dtype(out_dtype).itemsize + sum(e.dtype.itemsize for e in extras)


def _mm_nn(name, a, w, out_dtype, a_pro=None, epi=None, extras=(), norm_gain=None, tasks=()):
    m, k = a.shape
    _, n = w.shape
    tm, tn = _mm_tiles(m, n, k, _out_bytes(out_dtype, extras))
    if norm_gain is not None:
        tm, tn = _pick(m, 416, 16), n
    tile = pl.BlockSpec((tm, tn), lambda j, i: (i, j))
    return _mm(name, a, w, dims=_NN, grid=(n // tn, m // tm),
               a_spec=pl.BlockSpec((tm, k), lambda j, i: (i, 0)), b_spec=pl.BlockSpec((k, tn), lambda j, i: (0, j)),
               out_shape=jax.ShapeDtypeStruct((m, n), out_dtype), out_spec=tile,
               extras=extras, extra_specs=[tile] * len(extras), a_pro=a_pro, epi=epi, norm_gain=norm_gain, tasks=tasks)


def _mm_nt(name, a, w, out_dtype, epi=None, extras=(), tasks=()):
    m, c = a.shape
    r, _ = w.shape
    tm, tn = _mm_tiles(m, r, c, _out_bytes(out_dtype, extras))
    tile = pl.BlockSpec((tm, tn), lambda j, i: (i, j))
    return _mm(name, a, w, dims=_NT, grid=(r // tn, m // tm),
               a_spec=pl.BlockSpec((tm, c), lambda j, i: (i, 0)), b_spec=pl.BlockSpec((tn, c), lambda j, i: (j, 0)),
               out_shape=jax.ShapeDtypeStruct((m, r), out_dtype), out_spec=tile,
               extras=extras, extra_specs=[tile] * len(extras), epi=epi, tasks=tasks)


def _mm_tn(name, a, b, stacked, a_pro=None, tasks=()):
    t, ka = a.shape
    _, nb = b.shape
    ns = nb // N_CHIPS if stacked else nb
    ta, tb = _pick(ka, 512, 128), _pick(ns, 640, 128)
    if stacked:
        per = ns // tb
        out_shape = jax.ShapeDtypeStruct((N_CHIPS, ka, ns), BF16)
        out_spec = pl.BlockSpec((None, ta, tb), lambda i, j: (j // per, i, j % per))
    else:
        out_shape = jax.ShapeDtypeStruct((ka, nb), BF16)
        out_spec = pl.BlockSpec((ta, tb), lambda i, j: (i, j))
    return _mm(name, a, b, dims=_TN, grid=(ka // ta, nb // tb),
               a_spec=pl.BlockSpec((t, ta), lambda i, j: (0, i)), b_spec=pl.BlockSpec((t, tb), lambda i, j: (0, j)),
               out_shape=out_shape, out_spec=out_spec, a_pro=a_pro, tasks=tasks)


def _relu_sq(z):
    a = jnp.maximum(z, 0)
    return a * a


def _rms_fwd(name, h, g):
    m, d = h.shape
    tr = _pick(m, 512, 16)

    def body(h_ref, g_ref, o_ref):
        x = h_ref[...]
        rstd = lax.rsqrt(jnp.mean(x * x, axis=-1, keepdims=True) + RMS_EPS)
        o_ref[...] = ((x * rstd) * g_ref[...]).astype(BF16)

    row = pl.BlockSpec((tr, d), lambda i: (i, 0))
    return pl.pallas_call(
        body, name=name, grid=(m // tr,), out_shape=jax.ShapeDtypeStruct((m, d), BF16),
        in_specs=[row, pl.BlockSpec((1, d), lambda i: (0, 0))], out_specs=row,
        compiler_params=_params(("parallel",)),
    )(h, g.reshape(1, d))


def _rms_bwd_math(x, g, dn):
    rstd = lax.rsqrt(jnp.mean(x * x, axis=-1, keepdims=True) + RMS_EPS)
    xhat = x * rstd
    dxhat = dn * g
    dx = rstd * (dxhat - xhat * jnp.mean(dxhat * xhat, axis=-1, keepdims=True))
    return dx, dn * xhat


def _fold8(v):
    r, c = v.shape
    return jnp.sum(v.reshape(r // 8, 8, c), axis=0)


def _rms_bwd(name, dn, h, g, dh_in):
    m, d = h.shape
    tr = _pick(m, 512, 16)
    nt = m // tr

    def body(dn_ref, h_ref, g_ref, dh_ref, o_ref, ob_ref, dg_ref):
        dx, dgp = _rms_bwd_math(h_ref[...], g_ref[...], dn_ref[...].astype(F32))
        dh = dh_ref[...] + dx
        o_ref[...] = dh
        ob_ref[...] = dh.astype(BF16)

        @pl.when(pl.program_id(0) == 0)
        def _():
            dg_ref[...] = jnp.zeros_like(dg_ref)

        dg_ref[...] += _fold8(dgp)

    row = pl.BlockSpec((tr, d), lambda i: (i, 0))
    return pl.pallas_call(
        body, name=name, grid=(nt,),
        out_shape=(jax.ShapeDtypeStruct((m, d), F32), jax.ShapeDtypeStruct((m, d), BF16),
                   jax.ShapeDtypeStruct((8, d), F32)),
        in_specs=[row, row, pl.BlockSpec((1, d), lambda i: (0, 0)), row],
        out_specs=(row, row, pl.BlockSpec((8, d), lambda i: (0, 0))),
        compiler_params=_params(("arbitrary",)),
    )(dn, h, g.reshape(1, d), dh_in)


def _rms_bwd_tokens(name, dn, h, g, dh_in):
    m, d = h.shape
    nb = m // BLOCK

    def body(dn_ref, h_ref, g_ref, dh_ref, gx_ref, first_ref, dg_ref):
        i = pl.program_id(0)
        dx, dgp = _rms_bwd_math(h_ref[...], g_ref[...], dn_ref[...].astype(F32))
        dh = dh_ref[...] + dx
        gx_ref[...] = dh

        @pl.when(i == 0)
        def _():
            first_ref[...] = dh
            dg_ref[...] = jnp.zeros_like(dg_ref)

        dg_ref[...] += _fold8(dgp)

    row = pl.BlockSpec((BLOCK, d), lambda i: (i, 0))
    return pl.pallas_call(
        body, name=name, grid=(nb,),
        out_shape=(jax.ShapeDtypeStruct((m - ROW0, d), F32), jax.ShapeDtypeStruct((ROW0, d), F32),
                   jax.ShapeDtypeStruct((8, d), F32)),
        in_specs=[row, row, pl.BlockSpec((1, d), lambda i: (0, 0)), row],
        out_specs=(pl.BlockSpec((BLOCK, d), lambda i: (jnp.maximum(i - 1, 0), 0)),
                   pl.BlockSpec((ROW0, d), lambda i: (0, 0)), pl.BlockSpec((8, d), lambda i: (0, 0))),
        compiler_params=_params(("arbitrary",)),
    )(dn, h, g.reshape(1, d), dh_in)


def _loss_head(h, g, target):
    m, d = h.shape
    tr = BLOCK

    def body(h_ref, g_ref, t_ref, loss_ref, o_ref, ob_ref, dg_ref):
        i = pl.program_id(0)
        x = h_ref[...]
        gv = g_ref[...]
        rstd = lax.rsqrt(jnp.mean(x * x, axis=-1, keepdims=True) + RMS_EPS)
        err = jnp.where(i > 0, (x * rstd) * gv - t_ref[...], 0.0)
        dx, dgp = _rms_bwd_math(x, gv, err * (1.0 / d))
        o_ref[...] = dx
        ob_ref[...] = dx.astype(BF16)

        @pl.when(i == 0)
        def _():
            dg_ref[...] = jnp.zeros_like(dg_ref)
            loss_ref[...] = jnp.zeros_like(loss_ref)

        dg_ref[...] += _fold8(dgp)
        sq = jnp.mean(err * err, axis=-1, keepdims=True)
        loss_ref[...] += 0.5 * jnp.sum(sq, axis=0, keepdims=True)

    row = pl.BlockSpec((tr, d), lambda i: (i, 0))
    return pl.pallas_call(
        body, name="loss_head", grid=(m // tr,),
        out_shape=(jax.ShapeDtypeStruct((8, 128), F32), jax.ShapeDtypeStruct((m, d), F32),
                   jax.ShapeDtypeStruct((m, d), BF16), jax.ShapeDtypeStruct((8, d), F32)),
        in_specs=[row, pl.BlockSpec((1, d), lambda i: (0, 0)),
                  pl.BlockSpec((tr, d), lambda i: (jnp.maximum(i - 1, 0), 0))],
        out_specs=(pl.BlockSpec((8, 128), lambda i: (0, 0)), row, row,
                   pl.BlockSpec((8, d), lambda i: (0, 0))),
        compiler_params=_params(("arbitrary",)),
    )(h, g.reshape(1, d), target)


HALO = 16


def _shift_down(cat, k):
    return pltpu.roll(cat, k, axis=0)[HALO:]


def _shift_up(cat, k):
    n = cat.shape[0]
    return pltpu.roll(cat, n - k, axis=0)[:n - HALO]


def _conv_fwd(bcu, cw):
    m, d3 = bcu.shape
    d = d3 // 3
    tr, tc = _pick(m, 416, 16), _pick(d, 512, 128)
    hb = tr // HALO

    def body(x_ref, xb_ref, w_ref, o_ref):
        i = pl.program_id(0)
        for j in range(d // tc):
            col = slice(j * tc, (j + 1) * tc)
            cb, cc, cu = (slice(q * d + j * tc, q * d + (j + 1) * tc) for q in range(3))
            v = x_ref[:, cc].astype(F32) * x_ref[:, cu].astype(F32)
            vh = jnp.where(i > 0, xb_ref[:, cc].astype(F32) * xb_ref[:, cu].astype(F32), 0.0)
            cat = jnp.concatenate([vh, v], axis=0)
            w = w_ref[:, col]
            conv = w[2:3] * v + w[1:2] * _shift_down(cat, 1) + w[0:1] * _shift_down(cat, 2)
            o_ref[:, col] = (x_ref[:, cb].astype(F32) * conv).astype(BF16)

    return pl.pallas_call(
        body, name="conv_fwd", grid=(m // tr,), out_shape=jax.ShapeDtypeStruct((m, d), BF16),
        in_specs=[pl.BlockSpec((tr, d3), lambda i: (i, 0)),
                  pl.BlockSpec((HALO, d3), lambda i: (jnp.maximum(i * hb - 1, 0), 0)),
                  pl.BlockSpec((8, d), lambda i: (0, 0))],
        out_specs=pl.BlockSpec((tr, d), lambda i: (i, 0)),
        compiler_params=_params(("parallel",)),
    )(bcu, bcu, cw)


def _conv_bwd(bcu, cw, dg):
    m, d3 = bcu.shape
    d = d3 // 3
    tr, tc = _pick(m, 208, 16), _pick(d, 512, 128)
    hb, nt = tr // HALO, m // tr

    def body(x_ref, xb_ref, xa_ref, dg_ref, dga_ref, w_ref, o_ref, dw_ref):
        i = pl.program_id(0)

        @pl.when(i == 0)
        def _():
            dw_ref[...] = jnp.zeros_like(dw_ref)

        for j in range(d // tc):
            col = slice(j * tc, (j + 1) * tc)
            cb, cc, cu = (slice(q * d + j * tc, q * d + (j + 1) * tc) for q in range(3))
            w = w_ref[:, col]
            b, c, u = x_ref[:, cb].astype(F32), x_ref[:, cc].astype(F32), x_ref[:, cu].astype(F32)
            dgv = dg_ref[:, col].astype(F32)
            v = c * u
            vh = jnp.where(i > 0, xb_ref[:, cc].astype(F32) * xb_ref[:, cu].astype(F32), 0.0)
            cat = jnp.concatenate([vh, v], axis=0)
            v1, v2 = _shift_down(cat, 1), _shift_down(cat, 2)
            dconv = dgv * b
            o_ref[:, cb] = (dgv * (w[2:3] * v + w[1:2] * v1 + w[0:1] * v2)).astype(BF16)
            taps = [jnp.sum(dconv * t, axis=0, keepdims=True) for t in (v2, v1, v)]
            dw_ref[:, col] += jnp.concatenate(taps + [jnp.zeros((5, tc), F32)], axis=0)
            nxt = jnp.where(i < nt - 1, dga_ref[:, col].astype(F32) * xa_ref[:, cb].astype(F32), 0.0)
            cat2 = jnp.concatenate([dconv, nxt], axis=0)
            dv = w[2:3] * dconv + w[1:2] * _shift_up(cat2, 1) + w[0:1] * _shift_up(cat2, 2)
            o_ref[:, cc] = (dv * u).astype(BF16)
            o_ref[:, cu] = (dv * c).astype(BF16)

    def rows(width):
        return pl.BlockSpec((tr, width), lambda i: (i, 0))

    def before(width):
        return pl.BlockSpec((HALO, width), lambda i: (jnp.maximum(i * hb - 1, 0), 0))

    def after(width):
        return pl.BlockSpec((HALO, width), lambda i: (jnp.minimum((i + 1) * hb, m // HALO - 1), 0))

    return pl.pallas_call(
        body, name="conv_bwd", grid=(nt,),
        out_shape=(jax.ShapeDtypeStruct((m, d3), BF16), jax.ShapeDtypeStruct((8, d), F32)),
        in_specs=[rows(d3), before(d3), after(d3), rows(d), after(d), pl.BlockSpec((8, d), lambda i: (0, 0))],
        out_specs=(rows(d3), pl.BlockSpec((8, d), lambda i: (0, 0))),
        compiler_params=_params(("arbitrary",)),
    )(bcu, bcu, bcu, dg, dg, cw)


PAIR = 2 * HEAD_DIM


def _rope_tables(m):
    pad = ROW0 - N_META
    pos = jnp.arange(m, dtype=F32) - pad
    inv = ROPE_THETA ** (-jnp.arange(0, HEAD_DIM, 2, dtype=F32) / HEAD_DIM)
    ang = pos[:, None] * inv[None, :]
    cos, sin = jnp.cos(ang), jnp.sin(ang)
    return jnp.tile(jnp.concatenate([cos, cos], axis=1), (1, 2)), jnp.tile(jnp.concatenate([-sin, sin], axis=1), (1, 2))


def _rope_pair(x, c, s):
    half = HEAD_DIM // 2
    lane = lax.broadcasted_iota(jnp.int32, x.shape, 1)
    swapped = jnp.where(lane % HEAD_DIM < half, pltpu.roll(x, PAIR - half, axis=1), pltpu.roll(x, half, axis=1))
    return x * c + swapped * s


def _attn_mask(i, rows):
    r = lax.broadcasted_iota(jnp.int32, (rows, 2 * BLOCK), 0) % BLOCK
    cidx = lax.broadcasted_iota(jnp.int32, (rows, 2 * BLOCK), 1)
    key = (i - 1) * BLOCK + cidx
    return (cidx > r) & (cidx <= r + BLOCK) & (key >= ROW0 - N_META)


BAND = 2 * BLOCK


def _rope_qk(qkv, cos, sin):
    m, width = qkv.shape
    scale = HEAD_DIM ** -0.5
    nq, nk = N_Q_HEADS // 2, N_KV_HEADS // 2

    def body(x_ref, c_ref, s_ref, o_ref):
        c, s = c_ref[...], s_ref[...]
        cq, sq = c * scale, s * scale
        for t in range(nq + 2 * nk):
            col = slice(t * PAIR, (t + 1) * PAIR)
            if t < nq:
                o_ref[:, col] = _rope_pair(x_ref[:, col].astype(F32), cq, sq).astype(BF16)
            elif t < nq + nk:
                o_ref[:, col] = _rope_pair(x_ref[:, col].astype(F32), c, s).astype(BF16)
            else:
                o_ref[:, col] = x_ref[:, col]

    row = pl.BlockSpec((BLOCK, width), lambda i: (i, 0))
    tab = pl.BlockSpec((BLOCK, PAIR), lambda i: (i, 0))
    return pl.pallas_call(
        body, name="rope_qk", grid=(m // BLOCK,), out_shape=jax.ShapeDtypeStruct((m, width), BF16),
        in_specs=[row, tab, tab], out_specs=row, compiler_params=_params(("parallel",)),
    )(qkv, cos, sin)


def _pair_rows(ref, h):
    pairs = N_Q_HEADS // N_KV_HEADS // 2
    return jnp.concatenate([ref[:, (h * pairs + g) * PAIR:(h * pairs + g + 1) * PAIR] for g in range(pairs)], axis=0)


def _twice(x):
    z = jnp.zeros_like(x)
    return jnp.concatenate([jnp.concatenate([x, z], axis=1), jnp.concatenate([z, x], axis=1)], axis=0)


def _kv_band(cur_ref, prev_ref, h):
    k0, v0 = N_Q_HEADS * HEAD_DIM + h * HEAD_DIM, (N_Q_HEADS + N_KV_HEADS) * HEAD_DIM + h * HEAD_DIM
    p0, p1 = h * HEAD_DIM, (N_KV_HEADS + h) * HEAD_DIM
    k = jnp.concatenate([prev_ref[:, p0:p0 + HEAD_DIM], cur_ref[:, k0:k0 + HEAD_DIM]], axis=0)
    v = jnp.concatenate([prev_ref[:, p1:p1 + HEAD_DIM], cur_ref[:, v0:v0 + HEAD_DIM]], axis=0)
    return k, v


def _attn2_specs(width):
    kvw = 2 * N_KV_HEADS * HEAD_DIM
    cur = pl.BlockSpec((BLOCK, width), lambda i: (i, 0))
    prev = pl.BlockSpec((BLOCK, kvw), lambda i: (jnp.maximum(i - 1, 0), N_Q_HEADS * HEAD_DIM // kvw))
    return cur, prev


def _attn2_fwd(qkr, sink2):
    m, width = qkr.shape
    nb, rows = m // BLOCK, N_Q_HEADS // N_KV_HEADS // 2 * BLOCK
    dq = N_Q_HEADS * HEAD_DIM

    def body(x_ref, prev_ref, s_ref, o_ref, p_ref):
        allowed = _attn_mask(pl.program_id(0), rows)
        col0 = lax.broadcasted_iota(jnp.int32, (rows, BAND), 1) == 0
        lane = lax.broadcasted_iota(jnp.int32, (rows, PAIR), 1)
        rsel = lax.broadcasted_iota(jnp.int32, (2 * BAND, PAIR), 0) < BAND
        lsel = lax.broadcasted_iota(jnp.int32, (2 * BAND, PAIR), 1) < HEAD_DIM
        ones2 = jnp.where(rsel == lsel, 1.0, 0.0).astype(BF16)
        for h in range(N_KV_HEADS):
            k, v = _kv_band(x_ref, prev_ref, h)
            s2 = lax.dot_general(_pair_rows(x_ref, h), _twice(k), _NT, preferred_element_type=F32)
            sink = s_ref[h]
            e, mx = [], []
            for half in range(2):
                s = jnp.where(allowed, s2[:, half * BAND:(half + 1) * BAND], NEG_INF)
                mx.append(jnp.maximum(jnp.max(s, axis=-1, keepdims=True), sink[:, half * HEAD_DIM:half * HEAD_DIM + 1]))
                e.append(jnp.exp(s - mx[half]).astype(BF16))
            eb2 = jnp.concatenate(e, axis=1)
            es2 = jnp.exp(sink - jnp.where(lane < HEAD_DIM, mx[0], mx[1]))
            ov2 = jnp.dot(eb2, _twice(v), preferred_element_type=F32)
            inv2 = 1.0 / (jnp.dot(eb2, ones2, preferred_element_type=F32) + es2)
            o2 = (ov2 * inv2).astype(BF16)
            ps2 = es2 * inv2
            for g in range(rows // BLOCK):
                col = (h * (rows // BLOCK) + g) * PAIR
                o_ref[:, col:col + PAIR] = o2[g * BLOCK:(g + 1) * BLOCK, :]
            for half in range(2):
                at = half * HEAD_DIM
                p = jnp.where(col0, ps2[:, at:at + 1], e[half].astype(F32) * inv2[:, at:at + 1])
                p_ref[h, :, half * BAND:(half + 1) * BAND] = p.astype(BF16)

    cur, prev = _attn2_specs(width)
    return pl.pallas_call(
        body, name="attn_fwd", grid=(nb,),
        out_shape=(jax.ShapeDtypeStruct((m, dq), BF16), jax.ShapeDtypeStruct((N_KV_HEADS, nb * rows, 2 * BAND), BF16)),
        in_specs=[cur, prev, pl.BlockSpec((N_KV_HEADS, rows, PAIR), lambda i: (0, 0, 0))],
        out_specs=(pl.BlockSpec((BLOCK, dq), lambda i: (i, 0)),
                   pl.BlockSpec((N_KV_HEADS, rows, 2 * BAND), lambda i: (0, i, 0))),
        compiler_params=_params(("parallel",)),
    )(qkr, qkr, sink2)


def _attn2_bwd(qkr, p, o, do):
    m, width = qkr.shape
    nb, rows = m // BLOCK, N_Q_HEADS // N_KV_HEADS // 2 * BLOCK
    dq = N_Q_HEADS * HEAD_DIM

    def body(x_ref, prev_ref, p_ref, o_ref, do_ref, dq_ref, dk_ref, dv_ref, ds_ref):
        colz = lax.broadcasted_iota(jnp.int32, (rows, 2 * BAND), 1) % BAND == 0
        even = lax.broadcasted_iota(jnp.int32, (rows, PAIR), 1) < HEAD_DIM

        @pl.when(pl.program_id(0) == 0)
        def _():
            ds_ref[...] = jnp.zeros_like(ds_ref)

        for h in range(N_KV_HEADS):
            k, v = _kv_band(x_ref, prev_ref, h)
            k2, v2 = _twice(k), _twice(v)
            q2, do2, pv = _pair_rows(x_ref, h), _pair_rows(do_ref, h), p_ref[h]
            prod = do2.astype(F32) * _pair_rows(o_ref, h).astype(F32)
            delta = [jnp.sum(jnp.where(even, prod, 0.0), axis=-1, keepdims=True),
                     jnp.sum(jnp.where(even, 0.0, prod), axis=-1, keepdims=True)]
            dp2 = lax.dot_general(do2, v2, _NT, preferred_element_type=F32)
            pb = jnp.where(colz, jnp.zeros_like(pv), pv)
            ds = [(pb[:, half * BAND:(half + 1) * BAND].astype(F32)
                   * (dp2[:, half * BAND:(half + 1) * BAND] - delta[half])).astype(BF16) for half in range(2)]
            dsb2 = jnp.concatenate(ds, axis=1)
            dq2 = jnp.dot(dsb2, k2, preferred_element_type=F32).astype(BF16)
            for g in range(rows // BLOCK):
                col = (h * (rows // BLOCK) + g) * PAIR
                dq_ref[:, col:col + PAIR] = dq2[g * BLOCK:(g + 1) * BLOCK, :]
            dkk = lax.dot_general(q2, dsb2, _TN, preferred_element_type=F32)
            dk_ref[h] = (dkk[:HEAD_DIM, :BAND] + dkk[HEAD_DIM:, BAND:]).T
            dvv = lax.dot_general(do2, pb, _TN, preferred_element_type=F32)
            dv_ref[h] = (dvv[:HEAD_DIM, :BAND] + dvv[HEAD_DIM:, BAND:]).T
            ds_ref[h] -= jnp.where(even, pv[:, 0:1].astype(F32) * delta[0], pv[:, BAND:BAND + 1].astype(F32) * delta[1])

    cur, prev = _attn2_specs(width)
    heads = pl.BlockSpec((BLOCK, dq), lambda i: (i, 0))
    band = pl.BlockSpec((N_KV_HEADS, None, BAND, HEAD_DIM), lambda i: (0, i, 0, 0))
    band_shape = jax.ShapeDtypeStruct((N_KV_HEADS, nb, BAND, HEAD_DIM), F32)
    sink = pl.BlockSpec((N_KV_HEADS, rows, PAIR), lambda i: (0, 0, 0))
    return pl.pallas_call(
        body, name="attn_bwd", grid=(nb,),
        out_shape=(jax.ShapeDtypeStruct((m, dq), BF16), band_shape, band_shape,
                   jax.ShapeDtypeStruct((N_KV_HEADS, rows, PAIR), F32)),
        in_specs=[cur, prev, pl.BlockSpec((N_KV_HEADS, rows, 2 * BAND), lambda i: (0, i, 0)), heads, heads],
        out_specs=(heads, band, band, sink), compiler_params=_params(("arbitrary",)),
    )(qkr, qkr, p, o, do)


def _rope_qk_bwd(dq, dkb, dvb, cos, sin):
    nb = dkb.shape[1]
    width = (N_Q_HEADS + 2 * N_KV_HEADS) * HEAD_DIM
    scale = HEAD_DIM ** -0.5
    nq, nk = N_Q_HEADS // 2, N_KV_HEADS // 2

    def body(dq_ref, kc_ref, kn_ref, vc_ref, vn_ref, c_ref, s_ref, o_ref):
        last = pl.program_id(0) == nb - 1
        c, s = c_ref[...], -s_ref[...]

        def band_sum(cur_ref, nxt_ref, t):
            return jnp.concatenate([cur_ref[2 * t + e, BLOCK:, :] + jnp.where(last, 0.0, nxt_ref[2 * t + e, :BLOCK, :])
                                    for e in range(2)], axis=1)

        cq, sq = c * scale, s * scale
        for t in range(nq):
            col = slice(t * PAIR, (t + 1) * PAIR)
            o_ref[:, col] = _rope_pair(dq_ref[:, col].astype(F32), cq, sq).astype(BF16)
        for t in range(nk):
            o_ref[:, (nq + t) * PAIR:(nq + t + 1) * PAIR] = _rope_pair(band_sum(kc_ref, kn_ref, t), c, s).astype(BF16)
            o_ref[:, (nq + nk + t) * PAIR:(nq + nk + t + 1) * PAIR] = band_sum(vc_ref, vn_ref, t).astype(BF16)

    tab = pl.BlockSpec((BLOCK, PAIR), lambda i: (i, 0))
    cur = pl.BlockSpec((N_KV_HEADS, None, BAND, HEAD_DIM), lambda i: (0, i, 0, 0))
    nxt = pl.BlockSpec((N_KV_HEADS, None, BAND, HEAD_DIM), lambda i: (0, jnp.minimum(i + 1, nb - 1), 0, 0))
    return pl.pallas_call(
        body, name="rope_qk_bwd", grid=(nb,), out_shape=jax.ShapeDtypeStruct((nb * BLOCK, width), BF16),
        in_specs=[pl.BlockSpec((BLOCK, N_Q_HEADS * HEAD_DIM), lambda i: (i, 0)), cur, nxt, cur, nxt, tab, tab],
        out_specs=pl.BlockSpec((BLOCK, width), lambda i: (i, 0)),
        compiler_params=_params(("parallel",)),
    )(dq, dkb, dkb, dvb, dvb, cos, sin)


def _tiles2d(r, c):
    tc = _pick(c, 2048, 128) if c % 128 == 0 else c
    tr = _pick(r, max(8, (1 << 20) // tc // 8 * 8), 8) if r % 8 == 0 else r
    return tr, tc


def _cast_bf16(name, w, place, wide):
    r, c = w.shape
    tr, tc = _tiles2d(r, c)
    if tr % 16:
        tr = r
    nc = c // tc

    def body(place_ref, w_ref, o_ref):
        o_ref[...] = w_ref[...].astype(BF16)

    if wide:
        out_shape = jax.ShapeDtypeStruct((r, N_CHIPS * c), BF16)
        out_spec = pl.BlockSpec((tr, tc), lambda i, j, p: (i, p[1] * nc + j))
    else:
        out_shape = jax.ShapeDtypeStruct((N_CHIPS, r, c), BF16)
        out_spec = pl.BlockSpec((None, tr, tc), lambda i, j, p: (p[1], i, j))
    return pl.pallas_call(
        body, name=name, out_shape=out_shape,
        grid_spec=pltpu.PrefetchScalarGridSpec(
            num_scalar_prefetch=1, grid=(r // tr, nc),
            in_specs=[pl.BlockSpec((tr, tc), lambda i, j, p: (i, j))], out_specs=out_spec),
        compiler_params=_params(("parallel", "parallel")),
    )(place, w)


def _pair_sum(name, g, got, place):
    n, r, c = g.shape
    half = r // 2
    tr, tc = _tiles2d(half, c)
    nh = half // tr

    def body(place_ref, g_ref, got_ref, o_ref, own_ref):
        s = (g_ref[...].astype(F32) + got_ref[...].astype(F32)).astype(BF16)
        o_ref[...] = s

        @pl.when(pl.program_id(2) == place_ref[1])
        def _():
            own_ref[...] = s

    tile = pl.BlockSpec((None, tr, tc), lambda i, j, k, p: (k, i, j))
    shape = jax.ShapeDtypeStruct((n, half, c), BF16)
    return pl.pallas_call(
        body, name=name, out_shape=(shape, shape),
        grid_spec=pltpu.PrefetchScalarGridSpec(
            num_scalar_prefetch=1, grid=(nh, c // tc, n),
            in_specs=[pl.BlockSpec((None, tr, tc), lambda i, j, k, p: (k, p[0] * nh + i, j)), tile],
            out_specs=(tile, pl.BlockSpec((None, tr, tc), lambda i, j, k, p: (p[1], i, j)))),
        compiler_params=_params(("parallel", "parallel", "arbitrary")),
    )(place, g, got)


def _chip_sum(name, parts, place):
    n, half, c = parts.shape
    tr, tc = _tiles2d(half, c)
    nh = half // tr

    def body(place_ref, p0, p1, p2, p3, o_ref):
        o_ref[...] = ((p0[...].astype(F32) + p1[...].astype(F32)) + p2[...].astype(F32)) + p3[...].astype(F32)

    def chip(k):
        return pl.BlockSpec((None, tr, tc), lambda i, j, p: (k, i, j))

    return pl.pallas_call(
        body, name=name, out_shape=jax.ShapeDtypeStruct((2 * half, c), F32),
        grid_spec=pltpu.PrefetchScalarGridSpec(
            num_scalar_prefetch=1, grid=(nh, c // tc),
            in_specs=[chip(k) for k in range(n)],
            out_specs=pl.BlockSpec((tr, tc), lambda i, j, p: (p[0] * nh + i, j))),
        compiler_params=_params(("parallel", "parallel")),
    )(place, parts, parts, parts, parts)


def _dev_sum(gathered):
    def body(g_ref, o_ref):
        acc = g_ref[0]
        for k in range(1, N_DEV):
            acc = acc + g_ref[k]
        o_ref[...] = acc

    return pl.pallas_call(body, name="dev_sum", out_shape=jax.ShapeDtypeStruct(gathered.shape[1:], F32))(gathered)


def _adamw(name, w, g, m, v):
    r, c = w.shape
    tr, tc = _tiles2d(r, c)
    if r % 8 == 0:
        tr = _pick(r, max(8, (1 << 18) // tc // 8 * 8), 8)

    def body(w_ref, g_ref, m_ref, v_ref, go_ref, d_ref, mo_ref, vo_ref):
        gv = g_ref[...]
        go_ref[...] = gv
        mn = ADAM_B1 * m_ref[...] + (1.0 - ADAM_B1) * gv
        vn = ADAM_B2 * v_ref[...] + (1.0 - ADAM_B2) * jnp.square(gv)
        m_hat = mn / (1.0 - ADAM_B1 ** ADAM_STEP)
        v_hat = vn / (1.0 - ADAM_B2 ** ADAM_STEP)
        d_ref[...] = -ADAM_LR * (m_hat / (jnp.sqrt(v_hat) + ADAM_EPS) + ADAM_WD * w_ref[...])
        mo_ref[...] = mn
        vo_ref[...] = vn

    tile = pl.BlockSpec((tr, tc), lambda i, j: (i, j))
    shape = jax.ShapeDtypeStruct((r, c), F32)
    return pl.pallas_call(
        body, name=name, grid=(r // tr, c // tc), out_shape=(shape,) * 4,
        in_specs=[tile] * 4, out_specs=(tile,) * 4, compiler_params=_params(("parallel", "parallel")),
    )(w, g, m, v)


MATRICES = ("w_in_conv", "w_out_conv", "w_up_0", "w_down_0", "w_qkv", "w_o", "w_up_1", "w_down_1")
COLUMN_SHARDED = ("w_in_conv", "w_up_0", "w_qkv", "w_up_1")
NORMS = ("norm_mix_0", "norm_mlp_0", "norm_mix_1", "norm_mlp_1", "norm_final")


def _rows(stack):
    return stack.reshape(N_CHIPS * stack.shape[1], stack.shape[2])


def _stack(full):
    return full.reshape(N_CHIPS, full.shape[0] // N_CHIPS, full.shape[1])


def _add_residual(acc, res):
    return acc + res


def _relu_sq_grad(acc, z):
    return acc * (2.0 * jnp.maximum(z.astype(F32), 0.0))


def _step(x, target, stacks, small, norms, sinks, place, update):
    d = D_MODEL
    dc = d // N_CHIPS
    pad = ROW0 - N_META
    m = x.shape[0] + ROW0
    grp = N_Q_HEADS // N_KV_HEADS
    cos, sin = _rope_tables(m)
    pairs = grp // 2
    sink2 = jnp.broadcast_to(sinks.astype(F32).reshape(N_KV_HEADS, pairs, 1, 2, 1),
                             (N_KV_HEADS, pairs, BLOCK, 2, HEAD_DIM)).reshape(N_KV_HEADS, pairs * BLOCK, PAIR)

    def gather(*names):
        return _gather_task([stacks[n] for n in names])

    def pair_sum(tag, grad, got):
        return _pair_sum("pair_sum_" + tag, grad, got, place)

    def chip_sum(tag, landed):
        return _chip_sum("chip_sum_" + tag, landed, place)

    (w_in, small_all), = _run("gather_first", [_relayed_gather_task([stacks["w_in_conv"], small])])
    small_full = jnp.transpose(small_all, (1, 0, 2)).reshape(SMALL_ROWS, d)
    conv_w8 = small_full[N_META:N_META + 8]
    h0 = jnp.concatenate([jnp.zeros((pad, d), F32), small_full[:N_META], x], axis=0)

    n0 = _rms_fwd("norm_mix_0", h0, norms["norm_mix_0"])
    bcu, ((w_out, w_up0),) = _mm_nn("conv_in", n0, w_in, BF16, tasks=[gather("w_out_conv", "w_up_0")])
    gate = _conv_fwd(bcu, conv_w8)
    (h1, n1), ((w_qkv,),) = _mm_nn("conv_out", gate, _rows(w_out), F32, epi=_add_residual, extras=(h0,),
                                   norm_gain=norms["norm_mlp_0"], tasks=[gather("w_qkv")])
    z0, ((w_down0, w_o),) = _mm_nn("mlp_up_0", n1, w_up0, BF16, tasks=[gather("w_down_0", "w_o")])
    h2, ((w_up1,),) = _mm_nn("mlp_down_0", z0, _rows(w_down0), F32, a_pro=_relu_sq, epi=_add_residual,
                             extras=(h1,), tasks=[gather("w_up_1")])
    n2 = _rms_fwd("norm_mix_1", h2, norms["norm_mix_1"])
    qkv = _mm_nn("attn_qkv", n2, w_qkv, BF16)
    qkr = _rope_qk(qkv, cos, sin)
    o, probs = _attn2_fwd(qkr, sink2)
    h3, n3 = _mm_nn("attn_out", o, _rows(w_o), F32, epi=_add_residual, extras=(h2,), norm_gain=norms["norm_mlp_1"])
    z1, ((w_down1,),) = _mm_nn("mlp_up_1", n3, w_up1, BF16, tasks=[gather("w_down_1")])
    h4 = _mm_nn("mlp_down_1", z1, _rows(w_down1), F32, a_pro=_relu_sq, epi=_add_residual, extras=(h3,))

    gn = {}
    loss, dh, dh_bf, gn["norm_final"] = _loss_head(h4, norms["norm_final"], target)
    dz = _mm_nt("mlp_down_dx_1", dh_bf, _rows(w_down1), BF16, epi=_relu_sq_grad, extras=(z1,))
    g_d1 = _stack(_mm_tn("mlp_down_dw_1", z1, dh_bf, stacked=False, a_pro=_relu_sq))
    g_u1, ((got,),) = _mm_tn("mlp_up_dw_1", n3, dz, stacked=True, tasks=[_pair_exchange_task([g_d1])])
    s_d1 = pair_sum("d1", g_d1, got)
    dn, ((got,), (landed,)) = _mm_nt("mlp_up_dx_1", dz, w_up1, BF16,
                                          tasks=[_pair_exchange_task([g_u1]), _chip_exchange_task([s_d1])])
    s_u1, b_d1 = pair_sum("u1", g_u1, got), chip_sum("d1", landed)
    dh, dh_bf, gn["norm_mlp_1"] = _rms_bwd("norm_mlp_bwd_1", dn, h3, norms["norm_mlp_1"], dh)
    do = _mm_nt("attn_out_dx", dh_bf, _rows(w_o), BF16)
    g_o = _stack(_mm_tn("attn_out_dw", o, dh_bf, stacked=False))
    dq, dkb, dvb, dsink = _attn2_bwd(qkr, probs, o, do)
    dqkv = _rope_qk_bwd(dq, dkb, dvb, cos, sin)
    g_qkv, ((got,),) = _mm_tn("attn_qkv_dw", n2, dqkv, stacked=True, tasks=[_pair_exchange_task([g_o])])
    s_o = pair_sum("o", g_o, got)
    dn, ((got,), (landed,)) = _mm_nt("attn_qkv_dx", dqkv, w_qkv, BF16,
                                          tasks=[_pair_exchange_task([g_qkv]), _chip_exchange_task([s_u1])])
    s_qkv, b_u1 = pair_sum("qkv", g_qkv, got), chip_sum("u1", landed)
    dh, dh_bf, gn["norm_mix_1"] = _rms_bwd("norm_mix_bwd_1", dn, h2, norms["norm_mix_1"], dh)
    dz, ((landed_o, landed_qkv), (r_d1,)) = _mm_nt(
        "mlp_down_dx_0", dh_bf, _rows(w_down0), BF16, epi=_relu_sq_grad, extras=(z0,),
        tasks=[_chip_exchange_task([s_o, s_qkv]), _pair_share_task([b_d1])])
    b_o, b_qkv = chip_sum("o", landed_o), chip_sum("qkv", landed_qkv)
    g_d0, ((r_u1,),) = _mm_tn("mlp_down_dw_0", z0, dh_bf, stacked=False, a_pro=_relu_sq, tasks=[_pair_share_task([b_u1])])
    g_d0 = _stack(g_d0)
    g_u0, ((got,), (r_o, r_qkv)) = _mm_tn("mlp_up_dw_0", n1, dz, stacked=True,
                                          tasks=[_pair_exchange_task([g_d0]), _pair_share_task([b_o, b_qkv])])
    s_d0 = pair_sum("d0", g_d0, got)
    dn, ((got,), (landed,)) = _mm_nt("mlp_up_dx_0", dz, w_up0, BF16,
                                          tasks=[_pair_exchange_task([g_u0]), _chip_exchange_task([s_d0])])
    s_u0, b_d0 = pair_sum("u0", g_u0, got), chip_sum("d0", landed)
    dh, dh_bf, gn["norm_mlp_0"] = _rms_bwd("norm_mlp_bwd_0", dn, h1, norms["norm_mlp_0"], dh)
    dgate = _mm_nt("conv_out_dx", dh_bf, _rows(w_out), BF16)
    dbcu, g_conv_w = _conv_bwd(bcu, conv_w8, dgate)
    g_in, ((landed,), (r_d0,)) = _mm_tn("conv_in_dw", n0, dbcu, stacked=True,
                                        tasks=[_chip_exchange_task([s_u0]), _pair_share_task([b_d0])])
    b_u0 = chip_sum("u0", landed)
    g_out, ((got,), (r_u0,)) = _mm_tn("conv_out_dw", gate, dh_bf, stacked=False,
                                      tasks=[_pair_exchange_task([g_in]), _pair_share_task([b_u0])])
    g_out = _stack(g_out)
    s_in = pair_sum("in", g_in, got)
    dn, ((landed,), (got,)) = _mm_nt("conv_in_dx", dbcu, w_in, BF16,
                                     tasks=[_chip_exchange_task([s_in]), _pair_exchange_task([g_out])])
    b_in, s_out = chip_sum("in", landed), pair_sum("out", g_out, got)
    grad_x, dh_first, gn["norm_mix_0"] = _rms_bwd_tokens("norm_mix_bwd_0", dn, h0, norms["norm_mix_0"], dh)

    g_small = jnp.zeros((SMALL_ROWS, d), F32).at[:N_META].set(dh_first[pad:ROW0]).at[N_META:N_META + 8].set(g_conv_w)
    g_small = jnp.transpose(g_small.reshape(SMALL_ROWS, N_CHIPS, dc), (1, 0, 2))
    rep = jnp.zeros((8, d), F32)
    for r, n in enumerate(NORMS):
        rep = rep.at[r].set(jnp.sum(gn[n], axis=0))
    dsink = jnp.sum(dsink.reshape(N_KV_HEADS, pairs, BLOCK, 2, HEAD_DIM)[..., 0], axis=2)
    rep = rep.at[len(NORMS), :N_Q_HEADS].set(dsink.reshape(N_Q_HEADS))
    (got, rep_all), = _run("tail_pair_exchange", [_pair_exchange_task([g_small], small=rep)])
    s_small = pair_sum("small", g_small, got)
    (landed_out, landed_small), = _run("tail_chip_exchange", [_chip_exchange_task([s_out, s_small])])
    b_out, b_small = chip_sum("out", landed_out), chip_sum("small", landed_small)
    (r_out, r_small, r_in), = _run("tail_pair_share", [_pair_share_task([b_out, b_small, b_in])])
    for n, r in (("w_down_1", r_d1), ("w_up_1", r_u1), ("w_down_0", r_d0), ("w_up_0", r_u0), ("w_o", r_o),
                 ("w_qkv", r_qkv), ("w_out_conv", r_out), ("w_in_conv", r_in)):
        update(n, r)
    return loss, grad_x, r_small, rep_all


def kernel(x, meta_tokens, norm_mix_0, w_in_conv, conv_w, w_out_conv, norm_mlp_0, w_up_0, w_down_0, norm_mix_1, w_qkv, attn_sinks, w_o, norm_mlp_1, w_up_1, w_down_1, norm_final, loss_target, m_meta_tokens, m_norm_mix_0, m_w_in_conv, m_conv_w, m_w_out_conv, m_norm_mlp_0, m_w_up_0, m_w_down_0, m_norm_mix_1, m_w_qkv, m_attn_sinks, m_w_o, m_norm_mlp_1, m_w_up_1, m_w_down_1, m_norm_final, v_meta_tokens, v_norm_mix_0, v_w_in_conv, v_conv_w, v_w_out_conv, v_norm_mlp_0, v_w_up_0, v_w_down_0, v_norm_mix_1, v_w_qkv, v_attn_sinks, v_w_o, v_norm_mlp_1, v_w_up_1, v_w_down_1, v_norm_final):
    given = dict(locals())
    names = ("meta_tokens", "norm_mix_0", "w_in_conv", "conv_w", "w_out_conv", "norm_mlp_0", "w_up_0", "w_down_0",
             "norm_mix_1", "w_qkv", "attn_sinks", "w_o", "norm_mlp_1", "w_up_1", "w_down_1", "norm_final")
    d = D_MODEL
    dc = d // N_CHIPS
    chip = 2 * lax.axis_index("x") + lax.axis_index("y")
    place = jnp.stack([lax.axis_index("c"), chip]).astype(jnp.int32)

    small = jnp.zeros((SMALL_ROWS, dc), F32).at[:N_META].set(meta_tokens).at[N_META:N_META + CONV_WIDTH].set(conv_w)
    small = lax.dynamic_update_slice(jnp.zeros((N_CHIPS, SMALL_ROWS, dc), F32), small[None], (chip, 0, 0))
    stacks = {n: _cast_bf16("cast_" + n, given[n], place, n in COLUMN_SHARDED) for n in MATRICES}

    g_out, delta, new_m, new_v = {}, {}, {}, {}

    def update(n, grad):
        wt = given[n]
        shape2 = wt.shape if wt.ndim == 2 else (1, wt.shape[0])
        outs = _adamw("adamw_" + n, wt.reshape(shape2), grad.reshape(shape2),
                      given["m_" + n].reshape(shape2), given["v_" + n].reshape(shape2))
        g_out[n], delta[n], new_m[n], new_v[n] = [o.reshape(wt.shape) for o in outs]

    norms = {n: given[n] for n in NORMS}
    loss_part, grad_x, r_small, rep_all = _step(x[0], loss_target[0], stacks, small, norms, attn_sinks, place, update)
    loss = lax.psum(loss_part[0, 0], ("x", "y", "c"))
    rep_sum = _dev_sum(rep_all)
    update("meta_tokens", r_small[:N_META])
    update("conv_w", r_small[N_META:N_META + CONV_WIDTH])
    for r, n in enumerate(NORMS):
        update(n, rep_sum[r])
    update("attn_sinks", rep_sum[len(NORMS), :N_Q_HEADS])
    return (loss, grad_x[None], *[g_out[n] for n in names], *[delta[n] for n in names],
            *[new_m[n] for n in names], *[new_v[n] for n in names])
```

```python
import jax
import jax.numpy as jnp
from jax import lax
from jax.experimental import pallas as pl
from jax.experimental.pallas import tpu as pltpu

F32 = jnp.float32
BF16 = jnp.bfloat16

D_MODEL = 2048
SEQ = 8192
N_META = 16
CONV_WIDTH = 3
HEAD_DIM = 64
N_Q_HEADS = 32
N_KV_HEADS = 4
BLOCK = 128
ROPE_THETA = 10000.0
D_FF = 4 * D_MODEL
RMS_EPS = 1e-5
NEG_INF = -1e30

ADAM_LR = 0.001
ADAM_B1 = 0.9
ADAM_B2 = 0.999
ADAM_EPS = 1e-08
ADAM_WD = 0.01
ADAM_STEP = 10

N_CHIPS = 4
N_DEV = 8
MESH = pl.DeviceIdType.MESH
VMEM_LIMIT = 56 * 1024 * 1024
SMALL_ROWS = 32
ROW0 = BLOCK


def _pick(n, target, mult):
    best = None
    for t in range(mult, min(n, target) + 1, mult):
        if n % t == 0:
            best = t
    assert best is not None, (n, target, mult)
    return best


def _params(sem=None):
    return pltpu.CompilerParams(dimension_semantics=sem, vmem_limit_bytes=VMEM_LIMIT)


HBM_SPEC = pl.BlockSpec(memory_space=pltpu.HBM)


class _Task:
    def __init__(self, inputs, outputs, aliases, sem_shapes, bind):
        self.inputs, self.outputs, self.aliases = list(inputs), list(outputs), dict(aliases)
        self.sem_shapes, self.bind = list(sem_shapes), bind


def _like(arrays):
    return [jax.ShapeDtypeStruct(a.shape, a.dtype) for a in arrays]


def _bind_tasks(tasks, in_refs, out_refs, sem_refs):
    bound, i, o, s = [], 0, 0, 0
    for t in tasks:
        ni, no, ns = len(t.inputs), len(t.outputs), len(t.sem_shapes)
        bound.append(t.bind(in_refs[i:i + ni], out_refs[o:o + no], sem_refs[s:s + ns]))
        i, o, s = i + ni, o + no, s + ns
    return bound


def _run_phase(bound, phase):
    for b in bound:
        if b[phase] is not None:
            b[phase]()


def _task_plumbing(tasks, in_offset, out_offset):
    ins = [a for t in tasks for a in t.inputs]
    outs = [o for t in tasks for o in t.outputs]
    sems = [s for t in tasks for s in t.sem_shapes]
    aliases, i, o = {}, in_offset, out_offset
    for t in tasks:
        for src, dst in t.aliases.items():
            aliases[i + src] = o + dst
        i, o = i + len(t.inputs), o + len(t.outputs)
    return ins, outs, sems, aliases


def _split_outputs(tasks, flat):
    res, o = [], 0
    for t in tasks:
        res.append(list(flat[o:o + len(t.outputs)]))
        o += len(t.outputs)
    return res


def _run(name, tasks):
    ins, outs, sems, aliases = _task_plumbing(tasks, 0, 0)

    def body(*refs):
        bound = _bind_tasks(tasks, refs[:len(ins)], refs[len(ins):len(ins) + len(outs)], refs[len(ins) + len(outs):])
        for phase in range(3):
            _run_phase(bound, phase)

    flat = pl.pallas_call(
        body, name=name, out_shape=outs, in_specs=[HBM_SPEC] * len(ins), out_specs=[HBM_SPEC] * len(outs),
        input_output_aliases=aliases, scratch_shapes=sems,
    )(*ins)
    return _split_outputs(tasks, flat)


def _place():
    x, y, c = lax.axis_index("x"), lax.axis_index("y"), lax.axis_index("c")
    chips = [(1 - x, y), (x, 1 - y), (1 - x, 1 - y)]
    return x, y, c, 2 * x + y, chips


def _gather_task(stacks):
    n = len(stacks)
    halves = [s.shape[-2] // 2 for s in stacks]

    def bind(_, dst, sems):
        send_a, recv_a, send_b, recv_b = sems
        x, y, c, me, chips = _place()
        sibling = (x, y, 1 - c)

        def half(w, chip, hc):
            rows = pl.ds(hc * halves[w], halves[w])
            if len(stacks[w].shape) == 3:
                return dst[w].at[chip, rows, :]
            cols = stacks[w].shape[1] // N_CHIPS
            return dst[w].at[rows, pl.ds(chip * cols, cols)]

        def over_ici(j, w, block):
            return pltpu.make_async_remote_copy(
                src_ref=half(w, block, c), dst_ref=half(w, block, c), send_sem=send_a.at[j * n + w],
                recv_sem=recv_a.at[j * n + w], device_id=(*chips[j], c), device_id_type=MESH)

        def over_d2d(j, w, hc):
            got = half(w, 2 * chips[j][0] + chips[j][1], hc)
            return pltpu.make_async_remote_copy(
                src_ref=got, dst_ref=got, send_sem=send_b.at[j * n + w], recv_sem=recv_b.at[j * n + w],
                device_id=sibling, device_id_type=MESH)

        pairs = [(j, w) for j in range(3) for w in range(n)]

        def start():
            for j, w in pairs:
                over_ici(j, w, me).start()

        def mid():
            for j, w in pairs:
                over_ici(j, w, 2 * chips[j][0] + chips[j][1]).wait_recv()
                over_d2d(j, w, c).start()

        def finish():
            for j, w in pairs:
                over_d2d(j, w, 1 - c).wait_recv()
            for j, w in pairs:
                over_ici(j, w, me).wait_send()
                over_d2d(j, w, c).wait_send()

        return start, mid, finish

    return _Task(stacks, _like(stacks), {w: w for w in range(n)}, [pltpu.SemaphoreType.DMA((3 * n,))] * 4, bind)


def _relayed_gather_task(stacks):
    n = len(stacks)
    halves = [s.shape[-2] // 2 for s in stacks]

    def bind(_, dst, sems):
        send_a, recv_a, send_b, recv_b = sems
        x, y, c, me, _ = _place()
        sibling = (x, y, 1 - c)
        across_x, across_y = (1 - x, y, c), (x, 1 - y, c)
        kx, ky, kd = 2 * (1 - x) + y, 2 * x + (1 - y), 2 * (1 - x) + (1 - y)

        def rows_of(w, chip, start, size):
            rows = pl.ds(start, size)
            if len(stacks[w].shape) == 3:
                return dst[w].at[chip, rows, :]
            cols = stacks[w].shape[1] // N_CHIPS
            return dst[w].at[rows, pl.ds(chip * cols, cols)]

        def half(w, chip, hc):
            return rows_of(w, chip, hc * halves[w], halves[w])

        def quarter(w, chip, q):
            return rows_of(w, chip, c * halves[w] + q * (halves[w] // 2), halves[w] // 2)

        def over_ici(k, w, block, to):
            return pltpu.make_async_remote_copy(src_ref=block, dst_ref=block, send_sem=send_a.at[4 * w + k],
                                                recv_sem=recv_a.at[4 * w + k], device_id=to, device_id_type=MESH)

        def over_d2d(k, w, chip, hc):
            got = half(w, chip, hc)
            return pltpu.make_async_remote_copy(src_ref=got, dst_ref=got, send_sem=send_b.at[3 * w + k],
                                                recv_sem=recv_b.at[3 * w + k], device_id=sibling, device_id_type=MESH)

        def start():
            for w in range(n):
                over_ici(0, w, half(w, me, c), across_x).start()
                over_ici(1, w, half(w, me, c), across_y).start()

        def mid():
            for w in range(n):
                over_ici(0, w, half(w, kx, c), across_x).wait_recv()
                over_ici(2, w, quarter(w, kx, 0), across_y).start()
                over_ici(1, w, half(w, ky, c), across_y).wait_recv()
                over_ici(3, w, quarter(w, ky, 1), across_x).start()
                over_d2d(0, w, kx, c).start()
                over_d2d(1, w, ky, c).start()
            for w in range(n):
                over_ici(2, w, quarter(w, kd, 0), across_y).wait_recv()
                over_ici(3, w, quarter(w, kd, 1), across_x).wait_recv()
                over_d2d(2, w, kd, c).start()

        def finish():
            for w in range(n):
                for k, chip in enumerate((kx, ky, kd)):
                    over_d2d(k, w, chip, 1 - c).wait_recv()
            for w in range(n):
                over_ici(0, w, half(w, me, c), across_x).wait_send()
                over_ici(1, w, half(w, me, c), across_y).wait_send()
                over_ici(2, w, quarter(w, kx, 0), across_y).wait_send()
                over_ici(3, w, quarter(w, ky, 1), across_x).wait_send()
                for k, chip in enumerate((kx, ky, kd)):
                    over_d2d(k, w, chip, c).wait_send()

        return start, mid, finish

    sem_shapes = [pltpu.SemaphoreType.DMA((4 * n,))] * 2 + [pltpu.SemaphoreType.DMA((3 * n,))] * 2
    return _Task(stacks, _like(stacks), {w: w for w in range(n)}, sem_shapes, bind)


def _pair_exchange_task(grads, small=None):
    n = len(grads)
    halves = [g.shape[1] // 2 for g in grads]

    def bind(src, dst, sems):
        send, recv = sems[0], sems[1]
        x, y, c, me, _ = _place()
        sibling = (x, y, 1 - c)
        dev = 2 * me + c

        def to_sibling(w):
            return pltpu.make_async_remote_copy(
                src_ref=src[w].at[:, pl.ds((1 - c) * halves[w], halves[w]), :], dst_ref=dst[w],
                send_sem=send.at[w], recv_sem=recv.at[w], device_id=sibling, device_id_type=MESH)

        def to_peer(t, block):
            tx, ty, tc = (t >> 2) & 1, (t >> 1) & 1, t & 1
            return pltpu.make_async_remote_copy(
                src_ref=src[n], dst_ref=dst[n].at[block], send_sem=sems[2].at[t], recv_sem=sems[3].at[t],
                device_id=(x ^ tx, y ^ ty, c ^ tc), device_id_type=MESH)

        def mine():
            return pltpu.make_async_copy(src[n], dst[n].at[dev], sems[4])

        def start():
            for w in range(n):
                to_sibling(w).start()
            if small is not None:
                mine().start()
                for t in range(1, N_DEV):
                    to_peer(t, dev).start()

        def finish():
            for w in range(n):
                to_sibling(w).wait_recv()
            if small is not None:
                for t in range(1, N_DEV):
                    to_peer(t, dev ^ t).wait_recv()
            for w in range(n):
                to_sibling(w).wait_send()
            if small is not None:
                for t in range(1, N_DEV):
                    to_peer(t, dev).wait_send()
                mine().wait()

        return start, None, finish

    outputs = [jax.ShapeDtypeStruct((N_CHIPS, h, g.shape[2]), g.dtype) for g, h in zip(grads, halves)]
    sem_shapes = [pltpu.SemaphoreType.DMA((n,)), pltpu.SemaphoreType.DMA((n,))]
    inputs = list(grads)
    if small is not None:
        inputs.append(small)
        outputs.append(jax.ShapeDtypeStruct((N_DEV,) + small.shape, small.dtype))
        sem_shapes += [pltpu.SemaphoreType.DMA((N_DEV,)), pltpu.SemaphoreType.DMA((N_DEV,)), pltpu.SemaphoreType.DMA(())]
    return _Task(inputs, outputs, {}, sem_shapes, bind)


def _chip_exchange_task(summed):
    n = len(summed)

    def bind(refs, dst, sems):
        src = refs[:n]
        send, recv = sems
        x, y, c, me, chips = _place()

        def copy(j, w, block_from, block_to):
            return pltpu.make_async_remote_copy(
                src_ref=src[w].at[block_from], dst_ref=dst[w].at[block_to], send_sem=send.at[j * n + w],
                recv_sem=recv.at[j * n + w], device_id=(*chips[j], c), device_id_type=MESH)

        pairs = [(j, w) for j in range(3) for w in range(n)]

        def start():
            for j, w in pairs:
                copy(j, w, 2 * chips[j][0] + chips[j][1], me).start()

        def finish():
            for j, w in pairs:
                copy(j, w, me, 2 * chips[j][0] + chips[j][1]).wait_recv()
            for j, w in pairs:
                copy(j, w, 2 * chips[j][0] + chips[j][1], me).wait_send()

        return start, None, finish

    partials, landing = [s[0] for s in summed], [s[1] for s in summed]
    return _Task(partials + landing, _like(landing), {n + w: w for w in range(n)},
                 [pltpu.SemaphoreType.DMA((3 * n,))] * 2, bind)


def _pair_share_task(blocks):
    n = len(blocks)

    def bind(_, dst, sems):
        send, recv = sems
        x, y, c, _, _ = _place()

        def copy(w, hc):
            h = blocks[w].shape[0] // 2
            rows = dst[w].at[pl.ds(hc * h, h), :]
            return pltpu.make_async_remote_copy(src_ref=rows, dst_ref=rows, send_sem=send.at[w], recv_sem=recv.at[w],
                                                device_id=(x, y, 1 - c), device_id_type=MESH)

        def start():
            for w in range(n):
                copy(w, c).start()

        def finish():
            for w in range(n):
                copy(w, 1 - c).wait_recv()
            for w in range(n):
                copy(w, c).wait_send()

        return start, None, finish

    return _Task(blocks, _like(blocks), {w: w for w in range(n)}, [pltpu.SemaphoreType.DMA((n,))] * 2, bind)


def _carrier_call(name, body, operands, *, grid, in_specs, out_specs, out_shape, semantics, tasks):
    n_in, n_out = len(operands), len(out_shape)
    t_ins, t_outs, t_sems, aliases = _task_plumbing(tasks, n_in, n_out)
    n_ti, n_to = len(t_ins), len(t_outs)
    total = 1
    for g in grid:
        total *= g
    mid_step = max(0, total - 1 - max(1, total // 8))

    def carrier(*refs):
        outs_at = n_in + n_ti
        if tasks:
            bound = _bind_tasks(tasks, refs[n_in:outs_at], refs[outs_at + n_out:outs_at + n_out + n_to],
                                refs[outs_at + n_out + n_to:])
            step = 0
            for axis, g in enumerate(grid):
                step = step * g + pl.program_id(axis)

            @pl.when(step == 0)
            def _():
                _run_phase(bound, 0)

        body(*refs[:n_in], *refs[outs_at:outs_at + n_out])

        if tasks:
            @pl.when(step == mid_step)
            def _():
                _run_phase(bound, 1)

            @pl.when(step == total - 1)
            def _():
                _run_phase(bound, 2)

    res = pl.pallas_call(
        carrier, name=name, grid=grid, out_shape=[*out_shape, *t_outs],
        in_specs=[*in_specs, *[HBM_SPEC] * n_ti], out_specs=[*out_specs, *[HBM_SPEC] * n_to],
        input_output_aliases=aliases, scratch_shapes=t_sems,
        compiler_params=_params(("arbitrary",) * len(grid) if tasks else semantics),
    )(*operands, *t_ins)
    return list(res[:n_out]), _split_outputs(tasks, res[n_out:])


def _mm(name, a, b, *, dims, grid, a_spec, b_spec, out_shape, out_spec,
        extras=(), extra_specs=(), a_pro=None, epi=None, norm_gain=None, tasks=()):
    n_ex = len(extras)
    normed = norm_gain is not None

    def body(a_ref, b_ref, *rest):
        outs = rest[n_ex + normed:]
        av = a_ref[...]
        if a_pro is not None:
            av = a_pro(av)
        acc = lax.dot_general(av, b_ref[...], dims, preferred_element_type=F32)
        if epi is not None:
            acc = epi(acc, *[e[...] for e in rest[:n_ex]])
        outs[0][...] = acc.astype(outs[0].dtype)
        if normed:
            rstd = lax.rsqrt(jnp.mean(acc * acc, axis=-1, keepdims=True) + RMS_EPS)
            outs[1][...] = ((acc * rstd) * rest[n_ex][...]).astype(BF16)

    operands, in_specs = [a, b, *extras], [a_spec, b_spec, *extra_specs]
    out_specs, out_shapes = [out_spec], [out_shape]
    if normed:
        width = out_shape.shape[1]
        operands.append(norm_gain.reshape(1, width))
        in_specs.append(pl.BlockSpec((1, width), lambda j, i: (0, 0)))
        out_specs.append(out_spec)
        out_shapes.append(jax.ShapeDtypeStruct(out_shape.shape, BF16))
    res, carried = _carrier_call(name, body, operands, grid=grid, in_specs=in_specs, out_specs=out_specs,
                                 out_shape=out_shapes, semantics=("parallel", "parallel"), tasks=tasks)
    res = tuple(res) if normed else res[0]
    return (res, carried) if tasks else res


_NN = (((1,), (0,)), ((), ()))
_NT = (((1,), (1,)), ((), ()))
_TN = (((0,), (0,)), ((), ()))


MM_TILE_BUDGET = 49 * 1024 * 1024


def _mm_tiles(m, n, contraction, out_bytes):
    for rows, cols in ((1664, 1280), (1664, 1024), (832, 1024), (416, 1024), (416, 512)):
        tm, tn = _pick(m, rows, 16), _pick(n, cols, 128)
        if 2 * 2 * contraction * (tm + tn) + tm * tn * (4 + 2 * out_bytes) <= MM_TILE_BUDGET:
            break
    return tm, tn


def _out_bytes(out_dtype, extras):
    return jnp.dtype(out_dtype).itemsize + sum(e.dtype.itemsize for e in extras)


def _mm_nn(name, a, w, out_dtype, a_pro=None, epi=None, extras=(), norm_gain=None, tasks=()):
    m, k = a.shape
    _, n = w.shape
    tm, tn = _mm_tiles(m, n, k, _out_bytes(out_dtype, extras))
    if norm_gain is not None:
        tm, tn = _pick(m, 416, 16), n
    tile = pl.BlockSpec((tm, tn), lambda j, i: (i, j))
    return _mm(name, a, w, dims=_NN, grid=(n // tn, m // tm),
               a_spec=pl.BlockSpec((tm, k), lambda j, i: (i, 0)), b_spec=pl.BlockSpec((k, tn), lambda j, i: (0, j)),
               out_shape=jax.ShapeDtypeStruct((m, n), out_dtype), out_spec=tile,
               extras=extras, extra_specs=[tile] * len(extras), a_pro=a_pro, epi=epi, norm_gain=norm_gain, tasks=tasks)


def _mm_nt(name, a, w, out_dtype, epi=None, extras=(), tasks=()):
    m, c = a.shape
    r, _ = w.shape
    tm, tn = _mm_tiles(m, r, c, _out_bytes(out_dtype, extras))
    tile = pl.BlockSpec((tm, tn), lambda j, i: (i, j))
    return _mm(name, a, w, dims=_NT, grid=(r // tn, m // tm),
               a_spec=pl.BlockSpec((tm, c), lambda j, i: (i, 0)), b_spec=pl.BlockSpec((tn, c), lambda j, i: (j, 0)),
               out_shape=jax.ShapeDtypeStruct((m, r), out_dtype), out_spec=tile,
               extras=extras, extra_specs=[tile] * len(extras), epi=epi, tasks=tasks)


def _mm_tn(name, a, b, stacked, a_pro=None, tasks=()):
    t, ka = a.shape
    _, nb = b.shape
    ns = nb // N_CHIPS if stacked else nb
    ta, tb = _pick(ka, 512, 128), _pick(ns, 640, 128)
    if stacked:
        per = ns // tb
        out_shape = jax.ShapeDtypeStruct((N_CHIPS, ka, ns), BF16)
        out_spec = pl.BlockSpec((None, ta, tb), lambda i, j: (j // per, i, j % per))
    else:
        out_shape = jax.ShapeDtypeStruct((ka, nb), BF16)
        out_spec = pl.BlockSpec((ta, tb), lambda i, j: (i, j))
    return _mm(name, a, b, dims=_TN, grid=(ka // ta, nb // tb),
               a_spec=pl.BlockSpec((t, ta), lambda i, j: (0, i)), b_spec=pl.BlockSpec((t, tb), lambda i, j: (0, j)),
               out_shape=out_shape, out_spec=out_spec, a_pro=a_pro, tasks=tasks)


def _relu_sq(z):
    a = jnp.maximum(z, 0)
    return a * a


def _rms_fwd(name, h, g):
    m, d = h.shape
    tr = _pick(m, 512, 16)

    def body(h_ref, g_ref, o_ref):
        x = h_ref[...]
        rstd = lax.rsqrt(jnp.mean(x * x, axis=-1, keepdims=True) + RMS_EPS)
        o_ref[...] = ((x * rstd) * g_ref[...]).astype(BF16)

    row = pl.BlockSpec((tr, d), lambda i: (i, 0))
    return pl.pallas_call(
        body, name=name, grid=(m // tr,), out_shape=jax.ShapeDtypeStruct((m, d), BF16),
        in_specs=[row, pl.BlockSpec((1, d), lambda i: (0, 0))], out_specs=row,
        compiler_params=_params(("parallel",)),
    )(h, g.reshape(1, d))


def _rms_bwd_math(x, g, dn):
    rstd = lax.rsqrt(jnp.mean(x * x, axis=-1, keepdims=True) + RMS_EPS)
    xhat = x * rstd
    dxhat = dn * g
    dx = rstd * (dxhat - xhat * jnp.mean(dxhat * xhat, axis=-1, keepdims=True))
    return dx, dn * xhat


def _fold8(v):
    r, c = v.shape
    return jnp.sum(v.reshape(r // 8, 8, c), axis=0)


def _rms_bwd(name, dn, h, g, dh_in):
    m, d = h.shape
    tr = _pick(m, 512, 16)
    nt = m // tr

    def body(dn_ref, h_ref, g_ref, dh_ref, o_ref, ob_ref, dg_ref):
        dx, dgp = _rms_bwd_math(h_ref[...], g_ref[...], dn_ref[...].astype(F32))
        dh = dh_ref[...] + dx
        o_ref[...] = dh
        ob_ref[...] = dh.astype(BF16)

        @pl.when(pl.program_id(0) == 0)
        def _():
            dg_ref[...] = jnp.zeros_like(dg_ref)

        dg_ref[...] += _fold8(dgp)

    row = pl.BlockSpec((tr, d), lambda i: (i, 0))
    return pl.pallas_call(
        body, name=name, grid=(nt,),
        out_shape=(jax.ShapeDtypeStruct((m, d), F32), jax.ShapeDtypeStruct((m, d), BF16),
                   jax.ShapeDtypeStruct((8, d), F32)),
        in_specs=[row, row, pl.BlockSpec((1, d), lambda i: (0, 0)), row],
        out_specs=(row, row, pl.BlockSpec((8, d), lambda i: (0, 0))),
        compiler_params=_params(("arbitrary",)),
    )(dn, h, g.reshape(1, d), dh_in)


def _rms_bwd_tokens(name, dn, h, g, dh_in):
    m, d = h.shape
    nb = m // BLOCK

    def body(dn_ref, h_ref, g_ref, dh_ref, gx_ref, first_ref, dg_ref):
        i = pl.program_id(0)
        dx, dgp = _rms_bwd_math(h_ref[...], g_ref[...], dn_ref[...].astype(F32))
        dh = dh_ref[...] + dx
        gx_ref[...] = dh

        @pl.when(i == 0)
        def _():
            first_ref[...] = dh
            dg_ref[...] = jnp.zeros_like(dg_ref)

        dg_ref[...] += _fold8(dgp)

    row = pl.BlockSpec((BLOCK, d), lambda i: (i, 0))
    return pl.pallas_call(
        body, name=name, grid=(nb,),
        out_shape=(jax.ShapeDtypeStruct((m - ROW0, d), F32), jax.ShapeDtypeStruct((ROW0, d), F32),
                   jax.ShapeDtypeStruct((8, d), F32)),
        in_specs=[row, row, pl.BlockSpec((1, d), lambda i: (0, 0)), row],
        out_specs=(pl.BlockSpec((BLOCK, d), lambda i: (jnp.maximum(i - 1, 0), 0)),
                   pl.BlockSpec((ROW0, d), lambda i: (0, 0)), pl.BlockSpec((8, d), lambda i: (0, 0))),
        compiler_params=_params(("arbitrary",)),
    )(dn, h, g.reshape(1, d), dh_in)


def _loss_head(h, g, target):
    m, d = h.shape
    tr = BLOCK

    def body(h_ref, g_ref, t_ref, loss_ref, o_ref, ob_ref, dg_ref):
        i = pl.program_id(0)
        x = h_ref[...]
        gv = g_ref[...]
        rstd = lax.rsqrt(jnp.mean(x * x, axis=-1, keepdims=True) + RMS_EPS)
        err = jnp.where(i > 0, (x * rstd) * gv - t_ref[...], 0.0)
        dx, dgp = _rms_bwd_math(x, gv, err * (1.0 / d))
        o_ref[...] = dx
        ob_ref[...] = dx.astype(BF16)

        @pl.when(i == 0)
        def _():
            dg_ref[...] = jnp.zeros_like(dg_ref)
            loss_ref[...] = jnp.zeros_like(loss_ref)

        dg_ref[...] += _fold8(dgp)
        sq = jnp.mean(err * err, axis=-1, keepdims=True)
        loss_ref[...] += 0.5 * jnp.sum(sq, axis=0, keepdims=True)

    row = pl.BlockSpec((tr, d), lambda i: (i, 0))
    return pl.pallas_call(
        body, name="loss_head", grid=(m // tr,),
        out_shape=(jax.ShapeDtypeStruct((8, 128), F32), jax.ShapeDtypeStruct((m, d), F32),
                   jax.ShapeDtypeStruct((m, d), BF16), jax.ShapeDtypeStruct((8, d), F32)),
        in_specs=[row, pl.BlockSpec((1, d), lambda i: (0, 0)),
                  pl.BlockSpec((tr, d), lambda i: (jnp.maximum(i - 1, 0), 0))],
        out_specs=(pl.BlockSpec((8, 128), lambda i: (0, 0)), row, row,
                   pl.BlockSpec((8, d), lambda i: (0, 0))),
        compiler_params=_params(("arbitrary",)),
    )(h, g.reshape(1, d), target)


HALO = 16


def _shift_down(cat, k):
    return pltpu.roll(cat, k, axis=0)[HALO:]


def _shift_up(cat, k):
    n = cat.shape[0]
    return pltpu.roll(cat, n - k, axis=0)[:n - HALO]


def _conv_fwd(bcu, cw):
    m, d3 = bcu.shape
    d = d3 // 3
    tr, tc = _pick(m, 416, 16), _pick(d, 512, 128)
    hb = tr // HALO

    def body(x_ref, xb_ref, w_ref, o_ref):
        i = pl.program_id(0)
        for j in range(d // tc):
            col = slice(j * tc, (j + 1) * tc)
            cb, cc, cu = (slice(q * d + j * tc, q * d + (j + 1) * tc) for q in range(3))
            v = x_ref[:, cc].astype(F32) * x_ref[:, cu].astype(F32)
            vh = jnp.where(i > 0, xb_ref[:, cc].astype(F32) * xb_ref[:, cu].astype(F32), 0.0)
            cat = jnp.concatenate([vh, v], axis=0)
            w = w_ref[:, col]
            conv = w[2:3] * v + w[1:2] * _shift_down(cat, 1) + w[0:1] * _shift_down(cat, 2)
            o_ref[:, col] = (x_ref[:, cb].astype(F32) * conv).astype(BF16)

    return pl.pallas_call(
        body, name="conv_fwd", grid=(m // tr,), out_shape=jax.ShapeDtypeStruct((m, d), BF16),
        in_specs=[pl.BlockSpec((tr, d3), lambda i: (i, 0)),
                  pl.BlockSpec((HALO, d3), lambda i: (jnp.maximum(i * hb - 1, 0), 0)),
                  pl.BlockSpec((8, d), lambda i: (0, 0))],
        out_specs=pl.BlockSpec((tr, d), lambda i: (i, 0)),
        compiler_params=_params(("parallel",)),
    )(bcu, bcu, cw)


def _conv_bwd(bcu, cw, dg):
    m, d3 = bcu.shape
    d = d3 // 3
    tr, tc = _pick(m, 208, 16), _pick(d, 512, 128)
    hb, nt = tr // HALO, m // tr

    def body(x_ref, xb_ref, xa_ref, dg_ref, dga_ref, w_ref, o_ref, dw_ref):
        i = pl.program_id(0)

        @pl.when(i == 0)
        def _():
            dw_ref[...] = jnp.zeros_like(dw_ref)

        for j in range(d // tc):
            col = slice(j * tc, (j + 1) * tc)
            cb, cc, cu = (slice(q * d + j * tc, q * d + (j + 1) * tc) for q in range(3))
            w = w_ref[:, col]
            b, c, u = x_ref[:, cb].astype(F32), x_ref[:, cc].astype(F32), x_ref[:, cu].astype(F32)
            dgv = dg_ref[:, col].astype(F32)
            v = c * u
            vh = jnp.where(i > 0, xb_ref[:, cc].astype(F32) * xb_ref[:, cu].astype(F32), 0.0)
            cat = jnp.concatenate([vh, v], axis=0)
            v1, v2 = _shift_down(cat, 1), _shift_down(cat, 2)
            dconv = dgv * b
            o_ref[:, cb] = (dgv * (w[2:3] * v + w[1:2] * v1 + w[0:1] * v2)).astype(BF16)
            taps = [jnp.sum(dconv * t, axis=0, keepdims=True) for t in (v2, v1, v)]
            dw_ref[:, col] += jnp.concatenate(taps + [jnp.zeros((5, tc), F32)], axis=0)
            nxt = jnp.where(i < nt - 1, dga_ref[:, col].astype(F32) * xa_ref[:, cb].astype(F32), 0.0)
            cat2 = jnp.concatenate([dconv, nxt], axis=0)
            dv = w[2:3] * dconv + w[1:2] * _shift_up(cat2, 1) + w[0:1] * _shift_up(cat2, 2)
            o_ref[:, cc] = (dv * u).astype(BF16)
            o_ref[:, cu] = (dv * c).astype(BF16)

    def rows(width):
        return pl.BlockSpec((tr, width), lambda i: (i, 0))

    def before(width):
        return pl.BlockSpec((HALO, width), lambda i: (jnp.maximum(i * hb - 1, 0), 0))

    def after(width):
        return pl.BlockSpec((HALO, width), lambda i: (jnp.minimum((i + 1) * hb, m // HALO - 1), 0))

    return pl.pallas_call(
        body, name="conv_bwd", grid=(nt,),
        out_shape=(jax.ShapeDtypeStruct((m, d3), BF16), jax.ShapeDtypeStruct((8, d), F32)),
        in_specs=[rows(d3), before(d3), after(d3), rows(d), after(d), pl.BlockSpec((8, d), lambda i: (0, 0))],
        out_specs=(rows(d3), pl.BlockSpec((8, d), lambda i: (0, 0))),
        compiler_params=_params(("arbitrary",)),
    )(bcu, bcu, bcu, dg, dg, cw)


PAIR = 2 * HEAD_DIM


def _rope_tables(m):
    pad = ROW0 - N_META
    pos = jnp.arange(m, dtype=F32) - pad
    inv = ROPE_THETA ** (-jnp.arange(0, HEAD_DIM, 2, dtype=F32) / HEAD_DIM)
    ang = pos[:, None] * inv[None, :]
    cos, sin = jnp.cos(ang), jnp.sin(ang)
    return jnp.tile(jnp.concatenate([cos, cos], axis=1), (1, 2)), jnp.tile(jnp.concatenate([-sin, sin], axis=1), (1, 2))


def _rope_pair(x, c, s):
    half = HEAD_DIM // 2
    lane = lax.broadcasted_iota(jnp.int32, x.shape, 1)
    swapped = jnp.where(lane % HEAD_DIM < half, pltpu.roll(x, PAIR - half, axis=1), pltpu.roll(x, half, axis=1))
    return x * c + swapped * s


def _attn_mask(i, rows):
    r = lax.broadcasted_iota(jnp.int32, (rows, 2 * BLOCK), 0) % BLOCK
    cidx = lax.broadcasted_iota(jnp.int32, (rows, 2 * BLOCK), 1)
    key = (i - 1) * BLOCK + cidx
    return (cidx > r) & (cidx <= r + BLOCK) & (key >= ROW0 - N_META)


BAND = 2 * BLOCK


def _rope_qk(qkv, cos, sin):
    m, width = qkv.shape
    scale = HEAD_DIM ** -0.5
    nq, nk = N_Q_HEADS // 2, N_KV_HEADS // 2

    def body(x_ref, c_ref, s_ref, o_ref):
        c, s = c_ref[...], s_ref[...]
        cq, sq = c * scale, s * scale
        for t in range(nq + 2 * nk):
            col = slice(t * PAIR, (t + 1) * PAIR)
            if t < nq:
                o_ref[:, col] = _rope_pair(x_ref[:, col].astype(F32), cq, sq).astype(BF16)
            elif t < nq + nk:
                o_ref[:, col] = _rope_pair(x_ref[:, col].astype(F32), c, s).astype(BF16)
            else:
                o_ref[:, col] = x_ref[:, col]

    row = pl.BlockSpec((BLOCK, width), lambda i: (i, 0))
    tab = pl.BlockSpec((BLOCK, PAIR), lambda i: (i, 0))
    return pl.pallas_call(
        body, name="rope_qk", grid=(m // BLOCK,), out_shape=jax.ShapeDtypeStruct((m, width), BF16),
        in_specs=[row, tab, tab], out_specs=row, compiler_params=_params(("parallel",)),
    )(qkv, cos, sin)


def _pair_rows(ref, h):
    pairs = N_Q_HEADS // N_KV_HEADS // 2
    return jnp.concatenate([ref[:, (h * pairs + g) * PAIR:(h * pairs + g + 1) * PAIR] for g in range(pairs)], axis=0)


def _twice(x):
    z = jnp.zeros_like(x)
    return jnp.concatenate([jnp.concatenate([x, z], axis=1), jnp.concatenate([z, x], axis=1)], axis=0)


def _kv_band(cur_ref, prev_ref, h):
    k0, v0 = N_Q_HEADS * HEAD_DIM + h * HEAD_DIM, (N_Q_HEADS + N_KV_HEADS) * HEAD_DIM + h * HEAD_DIM
    p0, p1 = h * HEAD_DIM, (N_KV_HEADS + h) * HEAD_DIM
    k = jnp.concatenate([prev_ref[:, p0:p0 + HEAD_DIM], cur_ref[:, k0:k0 + HEAD_DIM]], axis=0)
    v = jnp.concatenate([prev_ref[:, p1:p1 + HEAD_DIM], cur_ref[:, v0:v0 + HEAD_DIM]], axis=0)
    return k, v


def _attn2_specs(width):
    kvw = 2 * N_KV_HEADS * HEAD_DIM
    cur = pl.BlockSpec((BLOCK, width), lambda i: (i, 0))
    prev = pl.BlockSpec((BLOCK, kvw), lambda i: (jnp.maximum(i - 1, 0), N_Q_HEADS * HEAD_DIM // kvw))
    return cur, prev


def _attn2_fwd(qkr, sink2):
    m, width = qkr.shape
    nb, rows = m // BLOCK, N_Q_HEADS // N_KV_HEADS // 2 * BLOCK
    dq = N_Q_HEADS * HEAD_DIM

    def body(x_ref, prev_ref, s_ref, o_ref, p_ref):
        allowed = _attn_mask(pl.program_id(0), rows)
        col0 = lax.broadcasted_iota(jnp.int32, (rows, BAND), 1) == 0
        lane = lax.broadcasted_iota(jnp.int32, (rows, PAIR), 1)
        rsel = lax.broadcasted_iota(jnp.int32, (2 * BAND, PAIR), 0) < BAND
        lsel = lax.broadcasted_iota(jnp.int32, (2 * BAND, PAIR), 1) < HEAD_DIM
        ones2 = jnp.where(rsel == lsel, 1.0, 0.0).astype(BF16)
        for h in range(N_KV_HEADS):
            k, v = _kv_band(x_ref, prev_ref, h)
            s2 = lax.dot_general(_pair_rows(x_ref, h), _twice(k), _NT, preferred_element_type=F32)
            sink = s_ref[h]
            e, mx = [], []
            for half in range(2):
                s = jnp.where(allowed, s2[:, half * BAND:(half + 1) * BAND], NEG_INF)
                mx.append(jnp.maximum(jnp.max(s, axis=-1, keepdims=True), sink[:, half * HEAD_DIM:half * HEAD_DIM + 1]))
                e.append(jnp.exp(s - mx[half]).astype(BF16))
            eb2 = jnp.concatenate(e, axis=1)
            es2 = jnp.exp(sink - jnp.where(lane < HEAD_DIM, mx[0], mx[1]))
            ov2 = jnp.dot(eb2, _twice(v), preferred_element_type=F32)
            inv2 = 1.0 / (jnp.dot(eb2, ones2, preferred_element_type=F32) + es2)
            o2 = (ov2 * inv2).astype(BF16)
            ps2 = es2 * inv2
            for g in range(rows // BLOCK):
                col = (h * (rows // BLOCK) + g) * PAIR
                o_ref[:, col:col + PAIR] = o2[g * BLOCK:(g + 1) * BLOCK, :]
            for half in range(2):
                at = half * HEAD_DIM
                p = jnp.where(col0, ps2[:, at:at + 1], e[half].astype(F32) * inv2[:, at:at + 1])
                p_ref[h, :, half * BAND:(half + 1) * BAND] = p.astype(BF16)

    cur, prev = _attn2_specs(width)
    return pl.pallas_call(
        body, name="attn_fwd", grid=(nb,),
        out_shape=(jax.ShapeDtypeStruct((m, dq), BF16), jax.ShapeDtypeStruct((N_KV_HEADS, nb * rows, 2 * BAND), BF16)),
        in_specs=[cur, prev, pl.BlockSpec((N_KV_HEADS, rows, PAIR), lambda i: (0, 0, 0))],
        out_specs=(pl.BlockSpec((BLOCK, dq), lambda i: (i, 0)),
                   pl.BlockSpec((N_KV_HEADS, rows, 2 * BAND), lambda i: (0, i, 0))),
        compiler_params=_params(("parallel",)),
    )(qkr, qkr, sink2)


def _attn2_bwd(qkr, p, o, do):
    m, width = qkr.shape
    nb, rows = m // BLOCK, N_Q_HEADS // N_KV_HEADS // 2 * BLOCK
    dq = N_Q_HEADS * HEAD_DIM

    def body(x_ref, prev_ref, p_ref, o_ref, do_ref, dq_ref, dk_ref, dv_ref, ds_ref):
        colz = lax.broadcasted_iota(jnp.int32, (rows, 2 * BAND), 1) % BAND == 0
        even = lax.broadcasted_iota(jnp.int32, (rows, PAIR), 1) < HEAD_DIM

        @pl.when(pl.program_id(0) == 0)
        def _():
            ds_ref[...] = jnp.zeros_like(ds_ref)

        for h in range(N_KV_HEADS):
            k, v = _kv_band(x_ref, prev_ref, h)
            k2, v2 = _twice(k), _twice(v)
            q2, do2, pv = _pair_rows(x_ref, h), _pair_rows(do_ref, h), p_ref[h]
            prod = do2.astype(F32) * _pair_rows(o_ref, h).astype(F32)
            delta = [jnp.sum(jnp.where(even, prod, 0.0), axis=-1, keepdims=True),
                     jnp.sum(jnp.where(even, 0.0, prod), axis=-1, keepdims=True)]
            dp2 = lax.dot_general(do2, v2, _NT, preferred_element_type=F32)
            pb = jnp.where(colz, jnp.zeros_like(pv), pv)
            ds = [(pb[:, half * BAND:(half + 1) * BAND].astype(F32)
                   * (dp2[:, half * BAND:(half + 1) * BAND] - delta[half])).astype(BF16) for half in range(2)]
            dsb2 = jnp.concatenate(ds, axis=1)
            dq2 = jnp.dot(dsb2, k2, preferred_element_type=F32).astype(BF16)
            for g in range(rows // BLOCK):
                col = (h * (rows // BLOCK) + g) * PAIR
                dq_ref[:, col:col + PAIR] = dq2[g * BLOCK:(g + 1) * BLOCK, :]
            dkk = lax.dot_general(q2, dsb2, _TN, preferred_element_type=F32)
            dk_ref[h] = (dkk[:HEAD_DIM, :BAND] + dkk[HEAD_DIM:, BAND:]).T
            dvv = lax.dot_general(do2, pb, _TN, preferred_element_type=F32)
            dv_ref[h] = (dvv[:HEAD_DIM, :BAND] + dvv[HEAD_DIM:, BAND:]).T
            ds_ref[h] -= jnp.where(even, pv[:, 0:1].astype(F32) * delta[0], pv[:, BAND:BAND + 1].astype(F32) * delta[1])

    cur, prev = _attn2_specs(width)
    heads = pl.BlockSpec((BLOCK, dq), lambda i: (i, 0))
    band = pl.BlockSpec((N_KV_HEADS, None, BAND, HEAD_DIM), lambda i: (0, i, 0, 0))
    band_shape = jax.ShapeDtypeStruct((N_KV_HEADS, nb, BAND, HEAD_DIM), F32)
    sink = pl.BlockSpec((N_KV_HEADS, rows, PAIR), lambda i: (0, 0, 0))
    return pl.pallas_call(
        body, name="attn_bwd", grid=(nb,),
        out_shape=(jax.ShapeDtypeStruct((m, dq), BF16), band_shape, band_shape,
                   jax.ShapeDtypeStruct((N_KV_HEADS, rows, PAIR), F32)),
        in_specs=[cur, prev, pl.BlockSpec((N_KV_HEADS, rows, 2 * BAND), lambda i: (0, i, 0)), heads, heads],
        out_specs=(heads, band, band, sink), compiler_params=_params(("arbitrary",)),
    )(qkr, qkr, p, o, do)


def _rope_qk_bwd(dq, dkb, dvb, cos, sin):
    nb = dkb.shape[1]
    width = (N_Q_HEADS + 2 * N_KV_HEADS) * HEAD_DIM
    scale = HEAD_DIM ** -0.5
    nq, nk = N_Q_HEADS // 2, N_KV_HEADS // 2

    def body(dq_ref, kc_ref, kn_ref, vc_ref, vn_ref, c_ref, s_ref, o_ref):
        last = pl.program_id(0) == nb - 1
        c, s = c_ref[...], -s_ref[...]

        def band_sum(cur_ref, nxt_ref, t):
            return jnp.concatenate([cur_ref[2 * t + e, BLOCK:, :] + jnp.where(last, 0.0, nxt_ref[2 * t + e, :BLOCK, :])
                                    for e in range(2)], axis=1)

        cq, sq = c * scale, s * scale
        for t in range(nq):
            col = slice(t * PAIR, (t + 1) * PAIR)
            o_ref[:, col] = _rope_pair(dq_ref[:, col].astype(F32), cq, sq).astype(BF16)
        for t in range(nk):
            o_ref[:, (nq + t) * PAIR:(nq + t + 1) * PAIR] = _rope_pair(band_sum(kc_ref, kn_ref, t), c, s).astype(BF16)
            o_ref[:, (nq + nk + t) * PAIR:(nq + nk + t + 1) * PAIR] = band_sum(vc_ref, vn_ref, t).astype(BF16)

    tab = pl.BlockSpec((BLOCK, PAIR), lambda i: (i, 0))
    cur = pl.BlockSpec((N_KV_HEADS, None, BAND, HEAD_DIM), lambda i: (0, i, 0, 0))
    nxt = pl.BlockSpec((N_KV_HEADS, None, BAND, HEAD_DIM), lambda i: (0, jnp.minimum(i + 1, nb - 1), 0, 0))
    return pl.pallas_call(
        body, name="rope_qk_bwd", grid=(nb,), out_shape=jax.ShapeDtypeStruct((nb * BLOCK, width), BF16),
        in_specs=[pl.BlockSpec((BLOCK, N_Q_HEADS * HEAD_DIM), lambda i: (i, 0)), cur, nxt, cur, nxt, tab, tab],
        out_specs=pl.BlockSpec((BLOCK, width), lambda i: (i, 0)),
        compiler_params=_params(("parallel",)),
    )(dq, dkb, dkb, dvb, dvb, cos, sin)


def _tiles2d(r, c):
    tc = _pick(c, 2048, 128) if c % 128 == 0 else c
    tr = _pick(r, max(8, (1 << 20) // tc // 8 * 8), 8) if r % 8 == 0 else r
    return tr, tc


def _cast_bf16(name, w, place, wide):
    r, c = w.shape
    tr, tc = _tiles2d(r, c)
    if tr % 16:
        tr = r
    nc = c // tc

    def body(place_ref, w_ref, o_ref):
        o_ref[...] = w_ref[...].astype(BF16)

    if wide:
        out_shape = jax.ShapeDtypeStruct((r, N_CHIPS * c), BF16)
        out_spec = pl.BlockSpec((tr, tc), lambda i, j, p: (i, p[1] * nc + j))
    else:
        out_shape = jax.ShapeDtypeStruct((N_CHIPS, r, c), BF16)
        out_spec = pl.BlockSpec((None, tr, tc), lambda i, j, p: (p[1], i, j))
    return pl.pallas_call(
        body, name=name, out_shape=out_shape,
        grid_spec=pltpu.PrefetchScalarGridSpec(
            num_scalar_prefetch=1, grid=(r // tr, nc),
            in_specs=[pl.BlockSpec((tr, tc), lambda i, j, p: (i, j))], out_specs=out_spec),
        compiler_params=_params(("parallel", "parallel")),
    )(place, w)


def _pair_sum(name, g, got, place):
    n, r, c = g.shape
    half = r // 2
    tr, tc = _tiles2d(half, c)
    nh = half // tr

    def body(place_ref, g_ref, got_ref, o_ref, own_ref):
        s = (g_ref[...].astype(F32) + got_ref[...].astype(F32)).astype(BF16)
        o_ref[...] = s

        @pl.when(pl.program_id(2) == place_ref[1])
        def _():
            own_ref[...] = s

    tile = pl.BlockSpec((None, tr, tc), lambda i, j, k, p: (k, i, j))
    shape = jax.ShapeDtypeStruct((n, half, c), BF16)
    return pl.pallas_call(
        body, name=name, out_shape=(shape, shape),
        grid_spec=pltpu.PrefetchScalarGridSpec(
            num_scalar_prefetch=1, grid=(nh, c // tc, n),
            in_specs=[pl.BlockSpec((None, tr, tc), lambda i, j, k, p: (k, p[0] * nh + i, j)), tile],
            out_specs=(tile, pl.BlockSpec((None, tr, tc), lambda i, j, k, p: (p[1], i, j)))),
        compiler_params=_params(("parallel", "parallel", "arbitrary")),
    )(place, g, got)


def _chip_sum(name, parts, place):
    n, half, c = parts.shape
    tr, tc = _tiles2d(half, c)
    nh = half // tr

    def body(place_ref, p0, p1, p2, p3, o_ref):
        o_ref[...] = ((p0[...].astype(F32) + p1[...].astype(F32)) + p2[...].astype(F32)) + p3[...].astype(F32)

    def chip(k):
        return pl.BlockSpec((None, tr, tc), lambda i, j, p: (k, i, j))

    return pl.pallas_call(
        body, name=name, out_shape=jax.ShapeDtypeStruct((2 * half, c), F32),
        grid_spec=pltpu.PrefetchScalarGridSpec(
            num_scalar_prefetch=1, grid=(nh, c // tc),
            in_specs=[chip(k) for k in range(n)],
            out_specs=pl.BlockSpec((tr, tc), lambda i, j, p: (p[0] * nh + i, j))),
        compiler_params=_params(("parallel", "parallel")),
    )(place, parts, parts, parts, parts)


def _dev_sum(gathered):
    def body(g_ref, o_ref):
        acc = g_ref[0]
        for k in range(1, N_DEV):
            acc = acc + g_ref[k]
        o_ref[...] = acc

    return pl.pallas_call(body, name="dev_sum", out_shape=jax.ShapeDtypeStruct(gathered.shape[1:], F32))(gathered)


def _adamw(name, w, g, m, v):
    r, c = w.shape
    tr, tc = _tiles2d(r, c)
    if r % 8 == 0:
        tr = _pick(r, max(8, (1 << 18) // tc // 8 * 8), 8)

    def body(w_ref, g_ref, m_ref, v_ref, go_ref, d_ref, mo_ref, vo_ref):
        gv = g_ref[...]
        go_ref[...] = gv
        mn = ADAM_B1 * m_ref[...] + (1.0 - ADAM_B1) * gv
        vn = ADAM_B2 * v_ref[...] + (1.0 - ADAM_B2) * jnp.square(gv)
        m_hat = mn / (1.0 - ADAM_B1 ** ADAM_STEP)
        v_hat = vn / (1.0 - ADAM_B2 ** ADAM_STEP)
        d_ref[...] = -ADAM_LR * (m_hat / (jnp.sqrt(v_hat) + ADAM_EPS) + ADAM_WD * w_ref[...])
        mo_ref[...] = mn
        vo_ref[...] = vn

    tile = pl.BlockSpec((tr, tc), lambda i, j: (i, j))
    shape = jax.ShapeDtypeStruct((r, c), F32)
    return pl.pallas_call(
        body, name=name, grid=(r // tr, c // tc), out_shape=(shape,) * 4,
        in_specs=[tile] * 4, out_specs=(tile,) * 4, compiler_params=_params(("parallel", "parallel")),
    )(w, g, m, v)


MATRICES = ("w_in_conv", "w_out_conv", "w_up_0", "w_down_0", "w_qkv", "w_o", "w_up_1", "w_down_1")
COLUMN_SHARDED = ("w_in_conv", "w_up_0", "w_qkv", "w_up_1")
NORMS = ("norm_mix_0", "norm_mlp_0", "norm_mix_1", "norm_mlp_1", "norm_final")


def _rows(stack):
    return stack.reshape(N_CHIPS * stack.shape[1], stack.shape[2])


def _stack(full):
    return full.reshape(N_CHIPS, full.shape[0] // N_CHIPS, full.shape[1])


def _add_residual(acc, res):
    return acc + res


def _relu_sq_grad(acc, z):
    return acc * (2.0 * jnp.maximum(z.astype(F32), 0.0))


def _step(x, target, stacks, small, norms, sinks, place, update):
    d = D_MODEL
    dc = d // N_CHIPS
    pad = ROW0 - N_META
    m = x.shape[0] + ROW0
    grp = N_Q_HEADS // N_KV_HEADS
    cos, sin = _rope_tables(m)
    pairs = grp // 2
    sink2 = jnp.broadcast_to(sinks.astype(F32).reshape(N_KV_HEADS, pairs, 1, 2, 1),
                             (N_KV_HEADS, pairs, BLOCK, 2, HEAD_DIM)).reshape(N_KV_HEADS, pairs * BLOCK, PAIR)

    def gather(*names):
        return _gather_task([stacks[n] for n in names])

    def pair_sum(tag, grad, got):
        return _pair_sum("pair_sum_" + tag, grad, got, place)

    def chip_sum(tag, landed):
        return _chip_sum("chip_sum_" + tag, landed, place)

    (w_in, small_all), = _run("gather_first", [_relayed_gather_task([stacks["w_in_conv"], small])])
    small_full = jnp.transpose(small_all, (1, 0, 2)).reshape(SMALL_ROWS, d)
    conv_w8 = small_full[N_META:N_META + 8]
    h0 = jnp.concatenate([jnp.zeros((pad, d), F32), small_full[:N_META], x], axis=0)

    n0 = _rms_fwd("norm_mix_0", h0, norms["norm_mix_0"])
    bcu, ((w_out, w_up0),) = _mm_nn("conv_in", n0, w_in, BF16, tasks=[gather("w_out_conv", "w_up_0")])
    gate = _conv_fwd(bcu, conv_w8)
    (h1, n1), ((w_qkv,),) = _mm_nn("conv_out", gate, _rows(w_out), F32, epi=_add_residual, extras=(h0,),
                                   norm_gain=norms["norm_mlp_0"], tasks=[gather("w_qkv")])
    z0, ((w_down0, w_o),) = _mm_nn("mlp_up_0", n1, w_up0, BF16, tasks=[gather("w_down_0", "w_o")])
    h2, ((w_up1,),) = _mm_nn("mlp_down_0", z0, _rows(w_down0), F32, a_pro=_relu_sq, epi=_add_residual,
                             extras=(h1,), tasks=[gather("w_up_1")])
    n2 = _rms_fwd("norm_mix_1", h2, norms["norm_mix_1"])
    qkv = _mm_nn("attn_qkv", n2, w_qkv, BF16)
    qkr = _rope_qk(qkv, cos, sin)
    o, probs = _attn2_fwd(qkr, sink2)
    h3, n3 = _mm_nn("attn_out", o, _rows(w_o), F32, epi=_add_residual, extras=(h2,), norm_gain=norms["norm_mlp_1"])
    z1, ((w_down1,),) = _mm_nn("mlp_up_1", n3, w_up1, BF16, tasks=[gather("w_down_1")])
    h4 = _mm_nn("mlp_down_1", z1, _rows(w_down1), F32, a_pro=_relu_sq, epi=_add_residual, extras=(h3,))

    gn = {}
    loss, dh, dh_bf, gn["norm_final"] = _loss_head(h4, norms["norm_final"], target)
    dz = _mm_nt("mlp_down_dx_1", dh_bf, _rows(w_down1), BF16, epi=_relu_sq_grad, extras=(z1,))
    g_d1 = _stack(_mm_tn("mlp_down_dw_1", z1, dh_bf, stacked=False, a_pro=_relu_sq))
    g_u1, ((got,),) = _mm_tn("mlp_up_dw_1", n3, dz, stacked=True, tasks=[_pair_exchange_task([g_d1])])
    s_d1 = pair_sum("d1", g_d1, got)
    dn, ((got,), (landed,)) = _mm_nt("mlp_up_dx_1", dz, w_up1, BF16,
                                          tasks=[_pair_exchange_task([g_u1]), _chip_exchange_task([s_d1])])
    s_u1, b_d1 = pair_sum("u1", g_u1, got), chip_sum("d1", landed)
    dh, dh_bf, gn["norm_mlp_1"] = _rms_bwd("norm_mlp_bwd_1", dn, h3, norms["norm_mlp_1"], dh)
    do = _mm_nt("attn_out_dx", dh_bf, _rows(w_o), BF16)
    g_o = _stack(_mm_tn("attn_out_dw", o, dh_bf, stacked=False))
    dq, dkb, dvb, dsink = _attn2_bwd(qkr, probs, o, do)
    dqkv = _rope_qk_bwd(dq, dkb, dvb, cos, sin)
    g_qkv, ((got,),) = _mm_tn("attn_qkv_dw", n2, dqkv, stacked=True, tasks=[_pair_exchange_task([g_o])])
    s_o = pair_sum("o", g_o, got)
    dn, ((got,), (landed,)) = _mm_nt("attn_qkv_dx", dqkv, w_qkv, BF16,
                                          tasks=[_pair_exchange_task([g_qkv]), _chip_exchange_task([s_u1])])
    s_qkv, b_u1 = pair_sum("qkv", g_qkv, got), chip_sum("u1", landed)
    dh, dh_bf, gn["norm_mix_1"] = _rms_bwd("norm_mix_bwd_1", dn, h2, norms["norm_mix_1"], dh)
    dz, ((landed_o, landed_qkv), (r_d1,)) = _mm_nt(
        "mlp_down_dx_0", dh_bf, _rows(w_down0), BF16, epi=_relu_sq_grad, extras=(z0,),
        tasks=[_chip_exchange_task([s_o, s_qkv]), _pair_share_task([b_d1])])
    b_o, b_qkv = chip_sum("o", landed_o), chip_sum("qkv", landed_qkv)
    g_d0, ((r_u1,),) = _mm_tn("mlp_down_dw_0", z0, dh_bf, stacked=False, a_pro=_relu_sq, tasks=[_pair_share_task([b_u1])])
    g_d0 = _stack(g_d0)
    g_u0, ((got,), (r_o, r_qkv)) = _mm_tn("mlp_up_dw_0", n1, dz, stacked=True,
                                          tasks=[_pair_exchange_task([g_d0]), _pair_share_task([b_o, b_qkv])])
    s_d0 = pair_sum("d0", g_d0, got)
    dn, ((got,), (landed,)) = _mm_nt("mlp_up_dx_0", dz, w_up0, BF16,
                                          tasks=[_pair_exchange_task([g_u0]), _chip_exchange_task([s_d0])])
    s_u0, b_d0 = pair_sum("u0", g_u0, got), chip_sum("d0", landed)
    dh, dh_bf, gn["norm_mlp_0"] = _rms_bwd("norm_mlp_bwd_0", dn, h1, norms["norm_mlp_0"], dh)
    dgate = _mm_nt("conv_out_dx", dh_bf, _rows(w_out), BF16)
    dbcu, g_conv_w = _conv_bwd(bcu, conv_w8, dgate)
    g_in, ((landed,), (r_d0,)) = _mm_tn("conv_in_dw", n0, dbcu, stacked=True,
                                        tasks=[_chip_exchange_task([s_u0]), _pair_share_task([b_d0])])
    b_u0 = chip_sum("u0", landed)
    g_out, ((got,), (r_u0,)) = _mm_tn("conv_out_dw", gate, dh_bf, stacked=False,
                                      tasks=[_pair_exchange_task([g_in]), _pair_share_task([b_u0])])
    g_out = _stack(g_out)
    s_in = pair_sum("in", g_in, got)
    dn, ((landed,), (got,)) = _mm_nt("conv_in_dx", dbcu, w_in, BF16,
                                     tasks=[_chip_exchange_task([s_in]), _pair_exchange_task([g_out])])
    b_in, s_out = chip_sum("in", landed), pair_sum("out", g_out, got)
    grad_x, dh_first, gn["norm_mix_0"] = _rms_bwd_tokens("norm_mix_bwd_0", dn, h0, norms["norm_mix_0"], dh)

    g_small = jnp.zeros((SMALL_ROWS, d), F32).at[:N_META].set(dh_first[pad:ROW0]).at[N_META:N_META + 8].set(g_conv_w)
    g_small = jnp.transpose(g_small.reshape(SMALL_ROWS, N_CHIPS, dc), (1, 0, 2))
    rep = jnp.zeros((8, d), F32)
    for r, n in enumerate(NORMS):
        rep = rep.at[r].set(jnp.sum(gn[n], axis=0))
    dsink = jnp.sum(dsink.reshape(N_KV_HEADS, pairs, BLOCK, 2, HEAD_DIM)[..., 0], axis=2)
    rep = rep.at[len(NORMS), :N_Q_HEADS].set(dsink.reshape(N_Q_HEADS))
    (got, rep_all), = _run("tail_pair_exchange", [_pair_exchange_task([g_small], small=rep)])
    s_small = pair_sum("small", g_small, got)
    (landed_out, landed_small), = _run("tail_chip_exchange", [_chip_exchange_task([s_out, s_small])])
    b_out, b_small = chip_sum("out", landed_out), chip_sum("small", landed_small)
    (r_out, r_small, r_in), = _run("tail_pair_share", [_pair_share_task([b_out, b_small, b_in])])
    for n, r in (("w_down_1", r_d1), ("w_up_1", r_u1), ("w_down_0", r_d0), ("w_up_0", r_u0), ("w_o", r_o),
                 ("w_qkv", r_qkv), ("w_out_conv", r_out), ("w_in_conv", r_in)):
        update(n, r)
    return loss, grad_x, r_small, rep_all


def kernel(x, meta_tokens, norm_mix_0, w_in_conv, conv_w, w_out_conv, norm_mlp_0, w_up_0, w_down_0, norm_mix_1, w_qkv, attn_sinks, w_o, norm_mlp_1, w_up_1, w_down_1, norm_final, loss_target, m_meta_tokens, m_norm_mix_0, m_w_in_conv, m_conv_w, m_w_out_conv, m_norm_mlp_0, m_w_up_0, m_w_down_0, m_norm_mix_1, m_w_qkv, m_attn_sinks, m_w_o, m_norm_mlp_1, m_w_up_1, m_w_down_1, m_norm_final, v_meta_tokens, v_norm_mix_0, v_w_in_conv, v_conv_w, v_w_out_conv, v_norm_mlp_0, v_w_up_0, v_w_down_0, v_norm_mix_1, v_w_qkv, v_attn_sinks, v_w_o, v_norm_mlp_1, v_w_up_1, v_w_down_1, v_norm_final):
    given = dict(locals())
    names = ("meta_tokens", "norm_mix_0", "w_in_conv", "conv_w", "w_out_conv", "norm_mlp_0", "w_up_0", "w_down_0",
             "norm_mix_1", "w_qkv", "attn_sinks", "w_o", "norm_mlp_1", "w_up_1", "w_down_1", "norm_final")
    d = D_MODEL
    dc = d // N_CHIPS
    chip = 2 * lax.axis_index("x") + lax.axis_index("y")
    place = jnp.stack([lax.axis_index("c"), chip]).astype(jnp.int32)

    small = jnp.zeros((SMALL_ROWS, dc), F32).at[:N_META].set(meta_tokens).at[N_META:N_META + CONV_WIDTH].set(conv_w)
    small = lax.dynamic_update_slice(jnp.zeros((N_CHIPS, SMALL_ROWS, dc), F32), small[None], (chip, 0, 0))
    stacks = {n: _cast_bf16("cast_" + n, given[n], place, n in COLUMN_SHARDED) for n in MATRICES}

    g_out, delta, new_m, new_v = {}, {}, {}, {}

    def update(n, grad):
        wt = given[n]
        shape2 = wt.shape if wt.ndim == 2 else (1, wt.shape[0])
        outs = _adamw("adamw_" + n, wt.reshape(shape2), grad.reshape(shape2),
                      given["m_" + n].reshape(shape2), given["v_" + n].reshape(shape2))
        g_out[n], delta[n], new_m[n], new_v[n] = [o.reshape(wt.shape) for o in outs]

    norms = {n: given[n] for n in NORMS}
    loss_part, grad_x, r_small, rep_all = _step(x[0], loss_target[0], stacks, small, norms, attn_sinks, place, update)
    loss = lax.psum(loss_part[0, 0], ("x", "y", "c"))
    rep_sum = _dev_sum(rep_all)
    update("meta_tokens", r_small[:N_META])
    update("conv_w", r_small[N_META:N_META + CONV_WIDTH])
    for r, n in enumerate(NORMS):
        update(n, rep_sum[r])
    update("attn_sinks", rep_sum[len(NORMS), :N_Q_HEADS])
    return (loss, grad_x[None], *[g_out[n] for n in names], *[delta[n] for n in names],
            *[new_m[n] for n in names], *[new_v[n] for n in names])
```
